```python
import math, functools
import jax, jax.numpy as jnp
from jax import lax
import numpy as np

D_MODEL = 1024
BATCH = 16
SEQ = 256
DEPTH = 2
DEC_BATCH = 4
DEC_SEQ = 1024
PAST_LEN = 512

GRID_W = 64
MIX_WIDTH = D_MODEL
ATT_WIDTH = MIX_WIDTH // 2
CONV_WIDTH = MIX_WIDTH - ATT_WIDTH
N_ATT_HEADS = 4
V_HEAD_DIM = ATT_WIDTH // N_ATT_HEADS
QK_HEAD_DIM = V_HEAD_DIM // 2
N_CONV_GROUPS = 8
CONV_K = 3
ROPE_BASE = 10000.0
D_FF = 2816
N_EXPERTS = 8
TOP_K = 2
D_FF_EXPERT = 1408
N_DENSE = (DEPTH + 1) // 2
N_MOE = DEPTH // 2
N_MOD = 6
Q_BLOCK = 128
NORM_EPS = 1e-6
IN_COLS = 3 * ATT_WIDTH + 3 * CONV_WIDTH

kernel_name = 'hybrid_diffattn_shortconv_prefix_dit_step'


def rms_norm(x, g):
    xf = x.astype(jnp.float32)
    y = xf * lax.rsqrt(jnp.mean(xf * xf, axis=-1, keepdims=True) + NORM_EPS)
    return (y * g.astype(jnp.float32)).astype(x.dtype)


def modulation(cond, w_ada, b_ada):
    m = jax.nn.silu(cond) @ w_ada + b_ada
    return jnp.split(m[:, None, :], N_MOD, axis=-1)


def axial_rope(x):
    L = x.shape[1]
    rows = L // GRID_W
    row = jnp.repeat(jnp.arange(rows), GRID_W)
    col = jnp.tile(jnp.arange(GRID_W), rows)
    ax = QK_HEAD_DIM // 2
    half = ax // 2
    freqs = ROPE_BASE ** (-jnp.arange(half, dtype=jnp.float32) / half)

    def rotate(xa, pos):
        ang = pos.astype(jnp.float32)[:, None] * freqs[None, :]
        cos = jnp.cos(ang)[None, :, None, None, :]
        sin = jnp.sin(ang)[None, :, None, None, :]
        x1 = xa[..., :half].astype(jnp.float32)
        x2 = xa[..., half:].astype(jnp.float32)
        return jnp.concatenate([x1 * cos - x2 * sin, x1 * sin + x2 * cos], axis=-1)

    out = jnp.concatenate([rotate(x[..., :ax], row), rotate(x[..., ax:], col)], axis=-1)
    return out.astype(x.dtype)


def diff_attention(q, k, v, lambda_qk, subln_g, lam_init):
    B, Lq = q.shape[0], q.shape[1]
    qb = min(Q_BLOCK, Lq)
    nblk = Lq // qb
    lq = lambda_qk.astype(jnp.float32)
    lam = jnp.exp(jnp.sum(lq[0] * lq[1])) - jnp.exp(jnp.sum(lq[2] * lq[3])) + lam_init
    scale = QK_HEAD_DIM ** -0.5
    q_blocks = jnp.moveaxis(q.reshape(B, nblk, qb, N_ATT_HEADS, 2, QK_HEAD_DIM), 1, 0)

    def one_block(qblk):
        s = jnp.einsum('bqhsd,bkhsd->bshqk', qblk, k).astype(jnp.float32) * scale
        p = jax.nn.softmax(s, axis=-1)
        a = p[:, 0] - lam * p[:, 1]
        return jnp.einsum('bhqk,bkhe->bqhe', a.astype(v.dtype), v)

    o = lax.map(one_block, q_blocks)
    o = jnp.moveaxis(o, 0, 1).reshape(B, Lq, N_ATT_HEADS, V_HEAD_DIM).astype(jnp.float32)
    o = o * lax.rsqrt(jnp.mean(o * o, axis=-1, keepdims=True) + NORM_EPS)
    o = o * subln_g.astype(jnp.float32) * (1.0 - lam_init)
    return o.reshape(B, Lq, ATT_WIDTH).astype(v.dtype)


def short_conv(u, w):
    up = jnp.pad(u, ((0, 0), (1, 1), (0, 0)))
    return up[:, :-2] * w[0] + up[:, 1:-1] * w[1] + up[:, 2:] * w[2]


def mixer(h, w_in, lambda_qk, subln_g, conv_w, w_out, lam_init, k_ctx, v_ctx):
    B, L, _ = h.shape
    proj = h @ w_in
    q, k, v, gb, gc, xin = jnp.split(
        proj,
        [ATT_WIDTH, 2 * ATT_WIDTH, 3 * ATT_WIDTH, 3 * ATT_WIDTH + CONV_WIDTH, 3 * ATT_WIDTH + 2 * CONV_WIDTH],
        axis=-1)
    q = q.reshape(B, L, N_ATT_HEADS, 2, QK_HEAD_DIM)
    k = k.reshape(B, L, N_ATT_HEADS, 2, QK_HEAD_DIM)
    v = v.reshape(B, L, N_ATT_HEADS, V_HEAD_DIM)
    if k_ctx is None:
        keys, vals = k, v
    else:
        q = axial_rope(q)
        keys = jnp.concatenate([k_ctx, axial_rope(k)], axis=1)
        vals = jnp.concatenate([v_ctx, v], axis=1)
    att = diff_attention(q, keys, vals, lambda_qk, subln_g, lam_init)
    conv = gb * short_conv(gc * xin, conv_w)
    out = jnp.concatenate([att, conv], axis=-1) @ w_out
    return out, k, v


def swiglu(h, w_gu, w_down):
    g, u = jnp.split(h @ w_gu, 2, axis=-1)
    return (jax.nn.silu(g) * u) @ w_down


def moe_swiglu(h, w_router, w_gu, w_down):
    B, L, D = h.shape
    t = h.reshape(B * L, D)
    logits = (t @ w_router).astype(jnp.float32)
    top_v, top_i = lax.top_k(logits, TOP_K)
    top_w = jax.nn.softmax(top_v, axis=-1)
    gates = jnp.einsum('nk,nke->ne', top_w, jax.nn.one_hot(top_i, N_EXPERTS, dtype=jnp.float32)).astype(h.dtype)
    gu = jnp.einsum('nd,edf->enf', t, w_gu)
    g, u = jnp.split(gu, 2, axis=-1)
    eo = jnp.einsum('enf,efd->end', jax.nn.silu(g) * u, w_down)
    y = jnp.einsum('ne,end->nd', gates, eo)
    return y.reshape(B, L, D)


def trunk_layer(x, cond, k_ctx, v_ctx, lam_init, w_ada_l, b_ada_l, g_mix, g_ffn,
                w_in_l, lambda_l, subln_l, conv_l, w_out_l, ffn):
    shift1, scale1, gate1, shift2, scale2, gate2 = modulation(cond, w_ada_l, b_ada_l)
    h = rms_norm(x, g_mix) * (1 + scale1) + shift1
    mo, k, v = mixer(h, w_in_l, lambda_l, subln_l, conv_l, w_out_l, lam_init, k_ctx, v_ctx)
    x = x + gate1 * mo
    h = rms_norm(x, g_ffn) * (1 + scale2) + shift2
    x = x + gate2 * ffn(h)
    return x, k, v


def setup_inputs(seed: int = 0) -> dict:
    key = jax.random.key(seed)
    ks = jax.random.split(key, 21)
    f32 = jnp.float32

    def nrm(k, shape, s):
        return jax.random.normal(k, shape, f32) * s

    return {
        'x_prompt': nrm(ks[0], (BATCH, SEQ, D_MODEL), 1.0),
        'x_sample': nrm(ks[1], (DEC_BATCH, DEC_SEQ, D_MODEL), 1.0),
        'cache_k': nrm(ks[2], (DEC_BATCH, DEPTH, PAST_LEN, N_ATT_HEADS, 2, QK_HEAD_DIM), 1.0),
        'cache_v': nrm(ks[3], (DEC_BATCH, DEPTH, PAST_LEN, N_ATT_HEADS, V_HEAD_DIM), 1.0),
        'c': nrm(ks[4], (DEC_BATCH, D_MODEL), 1.0),
        'c_ctx': nrm(ks[5], (D_MODEL,), 1.0),
        'w_ada': nrm(ks[6], (DEPTH, D_MODEL, N_MOD * D_MODEL), 0.5 * D_MODEL ** -0.5),
        'b_ada': nrm(ks[7], (DEPTH, N_MOD * D_MODEL), 0.01),
        'norm_mix_g': 1.0 + nrm(ks[8], (DEPTH, D_MODEL), 0.02),
        'norm_ffn_g': 1.0 + nrm(ks[9], (DEPTH, D_MODEL), 0.02),
        'w_in': nrm(ks[10], (DEPTH, D_MODEL, IN_COLS), D_MODEL ** -0.5),
        'lambda_qk': nrm(ks[11], (DEPTH, 4, QK_HEAD_DIM), 0.1),
        'subln_g': 1.0 + nrm(ks[12], (DEPTH, V_HEAD_DIM), 0.02),
        'conv_w': nrm(ks[13], (DEPTH, CONV_K, CONV_WIDTH), CONV_K ** -0.5),
        'w_out': nrm(ks[14], (DEPTH, MIX_WIDTH, D_MODEL), MIX_WIDTH ** -0.5),
        'w_gu_dense': nrm(ks[15], (N_DENSE, D_MODEL, 2 * D_FF), D_MODEL ** -0.5),
        'w_down_dense': nrm(ks[16], (N_DENSE, D_FF, D_MODEL), D_FF ** -0.5),
        'w_router': nrm(ks[17], (N_MOE, D_MODEL, N_EXPERTS), D_MODEL ** -0.5),
        'w_gu_moe': nrm(ks[18], (N_MOE, N_EXPERTS, D_MODEL, 2 * D_FF_EXPERT), D_MODEL ** -0.5),
        'w_down_moe': nrm(ks[19], (N_MOE, N_EXPERTS, D_FF_EXPERT, D_MODEL), D_FF_EXPERT ** -0.5),
        'final_g': 1.0 + nrm(ks[20], (D_MODEL,), 0.02),
    }


def reference(x_prompt, x_sample, cache_k, cache_v, c, c_ctx, w_ada, b_ada, norm_mix_g, norm_ffn_g,
              w_in, lambda_qk, subln_g, conv_w, w_out, w_gu_dense, w_down_dense, w_router,
              w_gu_moe, w_down_moe, final_g):
    def layer_ffn(layer):
        if layer % 2 == 0:
            i = layer // 2
            return functools.partial(swiglu, w_gu=w_gu_dense[i], w_down=w_down_dense[i])
        i = layer // 2
        return functools.partial(moe_swiglu, w_router=w_router[i], w_gu=w_gu_moe[i], w_down=w_down_moe[i])

    ctx_cond = c_ctx[None, :]
    xp = x_prompt
    ks, vs = [], []
    for layer in range(DEPTH):
        lam_init = 0.8 - 0.6 * math.exp(-0.3 * layer)
        xp, k, v = trunk_layer(xp, ctx_cond, None, None, lam_init, w_ada[layer], b_ada[layer],
                               norm_mix_g[layer], norm_ffn_g[layer], w_in[layer], lambda_qk[layer],
                               subln_g[layer], conv_w[layer], w_out[layer], layer_ffn(layer))
        ks.append(k)
        vs.append(v)

    xs = x_sample
    for layer in range(DEPTH):
        lam_init = 0.8 - 0.6 * math.exp(-0.3 * layer)
        xs, _, _ = trunk_layer(xs, c, cache_k[:, layer], cache_v[:, layer], lam_init, w_ada[layer],
                               b_ada[layer], norm_mix_g[layer], norm_ffn_g[layer], w_in[layer],
                               lambda_qk[layer], subln_g[layer], conv_w[layer], w_out[layer],
                               layer_ffn(layer))

    y_prompt = rms_norm(xp, final_g)
    y_sample = rms_norm(xs, final_g)
    new_k = jnp.stack(ks, axis=1)
    new_v = jnp.stack(vs, axis=1)
    return (y_prompt, y_sample, new_k, new_v)
```

```python
import functools
import math

import jax
import jax.numpy as jnp
from jax import lax
from jax.experimental import pallas as pl
from jax.experimental.pallas import tpu as pltpu

D_MODEL = 1024
BATCH = 16
SEQ = 256
DEPTH = 2
DEC_BATCH = 4
DEC_SEQ = 1024
PAST_LEN = 512
GRID_W = 64
ATT_WIDTH = 512
CONV_WIDTH = 512
N_HEADS = 4
V_DIM = 128
QK_DIM = 64
ROPE_BASE = 10000.0
D_FF = 2816
N_EXPERTS = 8
D_FF_EXPERT = 1408
N_MOD = 6
NORM_EPS = 1e-6
IN_COLS = 3 * ATT_WIDTH + 3 * CONV_WIDTH

N_CTX = BATCH * SEQ
N_LAT = DEC_BATCH * DEC_SEQ
N_TOK = N_CTX + N_LAT
TM = 1024
N_TILES = N_TOK // TM
CTX_TILES = N_CTX // TM
COND_ROWS = 8
TN_IN = 512
TN_FF = 256
TN_DOWN = 512
TN_ADA = 1536
TG = 512
N_PAIRS = 2 * N_TOK
NT_G = N_PAIRS // TG + N_EXPERTS
R_PAD = NT_G * TG
LANES = 128
VMEM_LIMIT = 60 * 1024 * 1024

BF = jnp.bfloat16
F32 = jnp.float32


def _params(sem, vmem=VMEM_LIMIT):
    return pltpu.CompilerParams(dimension_semantics=sem, vmem_limit_bytes=vmem)


def _mod_row(i):
    return jnp.where(i < CTX_TILES, 0, i - (CTX_TILES - 1))


def _silu(x):
    return x / (1.0 + jnp.exp(-x))


def _rms(x):
    return x * lax.rsqrt(jnp.mean(x * x, axis=-1, keepdims=True) + NORM_EPS)


def _ada_kernel(c_ref, w_ref, b_ref, o_ref):
    s = _silu(c_ref[...]).astype(BF)
    o_ref[...] = jnp.dot(s, w_ref[...].astype(BF), preferred_element_type=F32) + b_ref[...]


def _ada(cond, w_ada, b_ada):
    n = N_MOD * D_MODEL
    return pl.pallas_call(
        _ada_kernel,
        grid=(DEPTH, n // TN_ADA),
        in_specs=[
            pl.BlockSpec((COND_ROWS, D_MODEL), lambda l, j: (0, 0)),
            pl.BlockSpec((None, D_MODEL, TN_ADA), lambda l, j: (l, 0, j)),
            pl.BlockSpec((None, 1, TN_ADA), lambda l, j: (l, 0, j)),
        ],
        out_specs=pl.BlockSpec((None, COND_ROWS, TN_ADA), lambda l, j: (l, 0, j)),
        out_shape=jax.ShapeDtypeStruct((DEPTH, COND_ROWS, n), F32),
        compiler_params=_params(("arbitrary", "arbitrary")),
        name="ada_mod",
    )(cond, w_ada, b_ada.reshape(DEPTH, 1, n))


def _in_kernel(x_ref, sh_ref, sc_ref, g_ref, w_ref, cos_ref, sin_ref, cw_ref,
               qkv_ref, kvf_ref, conv_ref, h_s, gb_s, gc_s):
    i = pl.program_id(0)
    j = pl.program_id(1)
    lat = i >= CTX_TILES

    @pl.when(j == 0)
    def _():
        y = _rms(x_ref[...]) * g_ref[...]
        h_s[...] = (y * (1.0 + sc_ref[...]) + sh_ref[...]).astype(BF)

    acc = jnp.dot(h_s[...], w_ref[...].astype(BF), preferred_element_type=F32)
    rope = jnp.logical_and(lat, j <= 1)

    @pl.when(rope)
    def _():
        lane = lax.broadcasted_iota(jnp.int32, acc.shape, 1)
        upper = (lane & 16) != 0
        partner = jnp.where(upper, pltpu.roll(acc, 16, 1), pltpu.roll(acc, TN_IN - 16, 1))
        qkv_ref[...] = (acc * cos_ref[...] + partner * sin_ref[...]).astype(BF)

    @pl.when(jnp.logical_and(j <= 2, jnp.logical_not(rope)))
    def _():
        qkv_ref[...] = acc.astype(BF)

    @pl.when(jnp.logical_and(jnp.logical_not(lat), jnp.logical_or(j == 1, j == 2)))
    def _():
        kvf_ref[...] = acc

    @pl.when(j == 3)
    def _():
        gb_s[...] = acc

    @pl.when(j == 4)
    def _():
        gc_s[...] = acc

    @pl.when(j == 5)
    def _():
        u = gc_s[...] * acc
        seq = jnp.where(lat, DEC_SEQ, SEQ)
        pos = lax.broadcasted_iota(jnp.int32, (TM, 1), 0) & (seq - 1)
        prev = jnp.where(pos == 0, 0.0, pltpu.roll(u, 1, 0))
        nxt = jnp.where(pos == seq - 1, 0.0, pltpu.roll(u, TM - 1, 0))
        cw = cw_ref[...]
        conv = prev * cw[0:1] + u * cw[1:2] + nxt * cw[2:3]
        conv_ref[...] = (gb_s[...] * conv).astype(BF)


def _in_proj(layer, x, mod5, g_mix, w_in, cos_t, sin_t, conv_w):
    def kv_map(i, j):
        return (jnp.minimum(i, CTX_TILES - 1),
                jnp.where(i < CTX_TILES, jnp.clip(j - 1, 0, 1), 1))

    return pl.pallas_call(
        _in_kernel,
        grid=(N_TILES, IN_COLS // TN_IN),
        in_specs=[
            pl.BlockSpec((TM, D_MODEL), lambda i, j: (i, 0)),
            pl.BlockSpec((None, None, None, 1, D_MODEL), lambda i, j: (layer, _mod_row(i), 0, 0, 0)),
            pl.BlockSpec((None, None, None, 1, D_MODEL), lambda i, j: (layer, _mod_row(i), 1, 0, 0)),
            pl.BlockSpec((None, 1, D_MODEL), lambda i, j: (layer, 0, 0)),
            pl.BlockSpec((None, D_MODEL, TN_IN), lambda i, j: (layer, 0, j)),
            pl.BlockSpec((DEC_SEQ, TN_IN), lambda i, j: (0, 0)),
            pl.BlockSpec((DEC_SEQ, TN_IN), lambda i, j: (0, 0)),
            pl.BlockSpec((None, 3, CONV_WIDTH), lambda i, j: (layer, 0, 0)),
        ],
        out_specs=[
            pl.BlockSpec((TM, TN_IN), lambda i, j: (i, jnp.minimum(j, 2))),
            pl.BlockSpec((TM, TN_IN), kv_map),
            pl.BlockSpec((TM, CONV_WIDTH), lambda i, j: (i, 0)),
        ],
        out_shape=[
            jax.ShapeDtypeStruct((N_TOK, 3 * ATT_WIDTH), BF),
            jax.ShapeDtypeStruct((N_CTX, 2 * ATT_WIDTH), F32),
            jax.ShapeDtypeStruct((N_TOK, CONV_WIDTH), BF),
        ],
        scratch_shapes=[
            pltpu.VMEM((TM, D_MODEL), BF),
            pltpu.VMEM((TM, CONV_WIDTH), F32),
            pltpu.VMEM((TM, CONV_WIDTH), F32),
        ],
        compiler_params=_params(("arbitrary", "arbitrary")),
        name=f"in_proj_l{layer}",
    )(x, mod5, mod5, g_mix, w_in, cos_t, sin_t, conv_w)


def _lambda(lq_ref, lam_init):
    lq = lq_ref[...]
    a = jnp.exp(jnp.sum(lq[0:1] * lq[1:2], axis=-1, keepdims=True))
    b = jnp.exp(jnp.sum(lq[2:3] * lq[3:4], axis=-1, keepdims=True))
    return a - b + lam_init


def _nt_dot(a, b):
    return lax.dot_general(a, b, (((1,), (1,)), ((), ())), preferred_element_type=F32)


def _head_norm(o, sg, lam_init):
    return _rms(o) * sg * (1.0 - lam_init)


def _attn_ctx_kernel(q_ref, k_ref, v_ref, lq_ref, sg_ref, o_ref, *, lam_init):
    lam = _lambda(lq_ref, lam_init)
    sg = sg_ref[...]
    for h in range(N_HEADS):
        cols = slice(h * V_DIM, (h + 1) * V_DIM)
        q = q_ref[:, cols] * (QK_DIM ** -0.5)
        k = k_ref[:, cols]
        v = v_ref[:, cols]
        probs = []
        for s in range(2):
            d = slice(s * QK_DIM, (s + 1) * QK_DIM)
            sc = _nt_dot(q[:, d], k[:, d])
            e = jnp.exp(sc - jnp.max(sc, axis=-1, keepdims=True))
            probs.append(e * (1.0 / jnp.sum(e, axis=-1, keepdims=True)))
        a = (probs[0] - lam * probs[1]).astype(BF)
        o = jnp.dot(a, v, preferred_element_type=F32)
        o_ref[:, cols] = _head_norm(o, sg, lam_init).astype(BF)


def _attn_ctx(layer, qkv, lambda_qk, subln_g, lam_init):
    return pl.pallas_call(
        functools.partial(_attn_ctx_kernel, lam_init=lam_init),
        grid=(BATCH,),
        in_specs=[
            pl.BlockSpec((SEQ, ATT_WIDTH), lambda b: (b, 0)),
            pl.BlockSpec((SEQ, ATT_WIDTH), lambda b: (b, 1)),
            pl.BlockSpec((SEQ, ATT_WIDTH), lambda b: (b, 2)),
            pl.BlockSpec((None, 4, QK_DIM), lambda b: (layer, 0, 0)),
            pl.BlockSpec((None, 1, V_DIM), lambda b: (layer, 0, 0)),
        ],
        out_specs=pl.BlockSpec((SEQ, ATT_WIDTH), lambda b: (b, 0)),
        out_shape=jax.ShapeDtypeStruct((N_TOK, ATT_WIDTH), BF),
        compiler_params=_params(("arbitrary",)),
        name=f"attn_ctx_l{layer}",
    )(qkv, qkv, qkv, lambda_qk, subln_g)


TQ = 256


def _attn_lat_kernel(q_ref, k_ref, v_ref, ck_ref, cv_ref, lq_ref, sg_ref, att_in_ref, o_ref, *, lam_init):
    del att_in_ref
    lam = _lambda(lq_ref, lam_init)
    sg = sg_ref[...]
    for h in range(N_HEADS):
        cols = slice(h * V_DIM, (h + 1) * V_DIM)
        q = q_ref[:, cols] * (QK_DIM ** -0.5)
        kl = k_ref[:, cols]
        kc = ck_ref[:, cols].astype(BF)
        probs = []
        for s in range(2):
            d = slice(s * QK_DIM, (s + 1) * QK_DIM)
            sc_c = _nt_dot(q[:, d], kc[:, d])
            sc_l = _nt_dot(q[:, d], kl[:, d])
            m = jnp.maximum(jnp.max(sc_c, axis=-1, keepdims=True), jnp.max(sc_l, axis=-1, keepdims=True))
            e_c = jnp.exp(sc_c - m)
            e_l = jnp.exp(sc_l - m)
            r = 1.0 / (jnp.sum(e_c, axis=-1, keepdims=True) + jnp.sum(e_l, axis=-1, keepdims=True))
            probs.append((e_c * r, e_l * r))
        a_c = (probs[0][0] - lam * probs[1][0]).astype(BF)
        a_l = (probs[0][1] - lam * probs[1][1]).astype(BF)
        o = jnp.dot(a_c, cv_ref[:, cols].astype(BF), preferred_element_type=F32)
        o = o + jnp.dot(a_l, v_ref[:, cols], preferred_element_type=F32)
        o_ref[:, cols] = _head_norm(o, sg, lam_init).astype(BF)


def _attn_lat(layer, qkv, cache_k4, cache_v4, lambda_qk, subln_g, att, lam_init):
    nqb = DEC_SEQ // TQ
    q0 = N_CTX // TQ
    return pl.pallas_call(
        functools.partial(_attn_lat_kernel, lam_init=lam_init),
        grid=(DEC_BATCH, nqb),
        in_specs=[
            pl.BlockSpec((TQ, ATT_WIDTH), lambda b, t: (q0 + b * nqb + t, 0)),
            pl.BlockSpec((DEC_SEQ, ATT_WIDTH), lambda b, t: (CTX_TILES + b, 1)),
            pl.BlockSpec((DEC_SEQ, ATT_WIDTH), lambda b, t: (CTX_TILES + b, 2)),
            pl.BlockSpec((None, None, PAST_LEN, ATT_WIDTH), lambda b, t: (b, layer, 0, 0)),
            pl.BlockSpec((None, None, PAST_LEN, ATT_WIDTH), lambda b, t: (b, layer, 0, 0)),
            pl.BlockSpec((None, 4, QK_DIM), lambda b, t: (layer, 0, 0)),
            pl.BlockSpec((None, 1, V_DIM), lambda b, t: (layer, 0, 0)),
            pl.BlockSpec(memory_space=pl.ANY),
        ],
        out_specs=pl.BlockSpec((TQ, ATT_WIDTH), lambda b, t: (q0 + b * nqb + t, 0)),
        out_shape=jax.ShapeDtypeStruct((N_TOK, ATT_WIDTH), BF),
        input_output_aliases={7: 0},
        compiler_params=_params(("arbitrary", "arbitrary")),
        name=f"attn_lat_l{layer}",
    )(qkv, qkv, qkv, cache_k4, cache_v4, lambda_qk, subln_g, att)


def _out_kernel(att_ref, conv_ref, w_ref, x_ref, g1_ref, sh_ref, sc_ref, gf_ref,
                xo_ref, h2_ref, *rest, with_f32):
    if with_f32:
        h2f_ref, wb_s = rest
    else:
        (wb_s,) = rest

    @pl.when(pl.program_id(0) == 0)
    def _():
        wb_s[...] = w_ref[...].astype(BF)

    mo = jnp.dot(att_ref[...], wb_s[:ATT_WIDTH, :], preferred_element_type=F32)
    mo = mo + jnp.dot(conv_ref[...], wb_s[ATT_WIDTH:, :], preferred_element_type=F32)
    xn = x_ref[...] + g1_ref[...] * mo
    xo_ref[...] = xn
    h2 = (_rms(xn) * gf_ref[...]) * (1.0 + sc_ref[...]) + sh_ref[...]
    h2_ref[...] = h2.astype(BF)
    if with_f32:
        h2f_ref[...] = h2


def _out_proj(layer, att, conv, w_out, x, mod5, g_ffn, with_f32):
    def mod_spec(c):
        return pl.BlockSpec((None, None, None, 1, D_MODEL), lambda i: (layer, _mod_row(i), c, 0, 0))

    row_spec = pl.BlockSpec((TM, D_MODEL), lambda i: (i, 0))
    out_specs = [row_spec, row_spec]
    out_shape = [jax.ShapeDtypeStruct((N_TOK, D_MODEL), F32), jax.ShapeDtypeStruct((N_TOK, D_MODEL), BF)]
    if with_f32:
        out_specs.append(row_spec)
        out_shape.append(jax.ShapeDtypeStruct((N_TOK, D_MODEL), F32))
    return pl.pallas_call(
        functools.partial(_out_kernel, with_f32=with_f32),
        grid=(N_TILES,),
        in_specs=[
            pl.BlockSpec((TM, ATT_WIDTH), lambda i: (i, 0)),
            pl.BlockSpec((TM, CONV_WIDTH), lambda i: (i, 0)),
            pl.BlockSpec((None, D_MODEL, D_MODEL), lambda i: (layer, 0, 0)),
            row_spec,
            mod_spec(2), mod_spec(3), mod_spec(4),
            pl.BlockSpec((None, 1, D_MODEL), lambda i: (layer, 0, 0)),
        ],
        out_specs=out_specs,
        out_shape=out_shape,
        scratch_shapes=[pltpu.VMEM((D_MODEL, D_MODEL), BF)],
        compiler_params=_params(("arbitrary",)),
        name=f"out_proj_l{layer}",
    )(att, conv, w_out, x, mod5, mod5, mod5, g_ffn)


def _gu_kernel(h_ref, wg_ref, wu_ref, o_ref):
    h = h_ref[...]
    g = jnp.dot(h, wg_ref[...].astype(BF), preferred_element_type=F32)
    u = jnp.dot(h, wu_ref[...].astype(BF), preferred_element_type=F32)
    o_ref[...] = (_silu(g) * u).astype(BF)


def _dense_gu(h2, w_gu):
    nj = D_FF // TN_FF
    return pl.pallas_call(
        _gu_kernel,
        grid=(N_TILES, nj),
        in_specs=[
            pl.BlockSpec((TM, D_MODEL), lambda i, j: (i, 0)),
            pl.BlockSpec((None, D_MODEL, TN_FF), lambda i, j: (0, 0, j)),
            pl.BlockSpec((None, D_MODEL, TN_FF), lambda i, j: (0, 0, j + nj)),
        ],
        out_specs=pl.BlockSpec((TM, TN_FF), lambda i, j: (i, j)),
        out_shape=jax.ShapeDtypeStruct((N_TOK, D_FF), BF),
        compiler_params=_params(("arbitrary", "arbitrary")),
        name="dense_gu",
    )(h2, w_gu, w_gu)


def _down_kernel(a_ref, w_ref, x_ref, g2_ref, o_ref):
    y = jnp.dot(a_ref[...], w_ref[...].astype(BF), preferred_element_type=F32)
    o_ref[...] = x_ref[...] + g2_ref[...] * y


def _dense_down(layer, act, w_down, x, mod5):
    return pl.pallas_call(
        _down_kernel,
        grid=(N_TILES, D_MODEL // TN_DOWN),
        in_specs=[
            pl.BlockSpec((TM, D_FF), lambda i, j: (i, 0)),
            pl.BlockSpec((None, D_FF, TN_DOWN), lambda i, j: (0, 0, j)),
            pl.BlockSpec((TM, TN_DOWN), lambda i, j: (i, j)),
            pl.BlockSpec((None, None, None, 1, TN_DOWN), lambda i, j: (layer, _mod_row(i), 5, 0, j)),
        ],
        out_specs=pl.BlockSpec((TM, TN_DOWN), lambda i, j: (i, j)),
        out_shape=jax.ShapeDtypeStruct((N_TOK, D_MODEL), F32),
        compiler_params=_params(("arbitrary", "arbitrary")),
        name="dense_down",
    )(act, w_down, x, mod5)


def _router_kernel(h_ref, wr_ref, ri_ref, rg_ref, cnt_ref, base_s):
    @pl.when(pl.program_id(0) == 0)
    def _():
        base_s[...] = jnp.zeros_like(base_s)

    logits = jnp.dot(h_ref[...], wr_ref[...].astype(BF), preferred_element_type=F32)
    lane = lax.broadcasted_iota(jnp.int32, logits.shape, 1)
    lg = jnp.where(lane < N_EXPERTS, logits, -jnp.inf)
    m1 = jnp.max(lg, axis=-1, keepdims=True)
    i1 = jnp.min(jnp.where(lg == m1, lane, LANES), axis=-1, keepdims=True)
    lg2 = jnp.where(lane == i1, -jnp.inf, lg)
    m2 = jnp.max(lg2, axis=-1, keepdims=True)
    i2 = jnp.min(jnp.where(lg2 == m2, lane, LANES), axis=-1, keepdims=True)
    e2 = jnp.exp(m2 - m1)
    w1 = 1.0 / (1.0 + e2)
    w2 = e2 / (1.0 + e2)

    sel1 = lane == i1
    sel2 = lane == i2
    onehot = jnp.logical_or(sel1, sel2)
    rows = lax.broadcasted_iota(jnp.int32, (TM, TM), 0)
    colsi = lax.broadcasted_iota(jnp.int32, (TM, TM), 1)
    tri = (colsi < rows).astype(BF)
    before = jnp.dot(tri, onehot.astype(BF), preferred_element_type=F32) + base_s[...]
    rank1 = jnp.sum(jnp.where(sel1, before, 0.0), axis=-1, keepdims=True)
    rank2 = jnp.sum(jnp.where(sel2, before, 0.0), axis=-1, keepdims=True)
    base_s[...] = base_s[...] + jnp.sum(onehot.astype(F32), axis=0, keepdims=True)
    cnt_ref[...] = base_s[...]

    ri = jnp.where(lane == 0, i1, jnp.where(lane == 1, i2, 0))
    ri = jnp.where(lane == 2, rank1.astype(jnp.int32), jnp.where(lane == 3, rank2.astype(jnp.int32), ri))
    ri_ref[...] = ri
    rg_ref[...] = jnp.where(lane == 0, w1, jnp.where(lane == 1, w2, 0.0))


def _router(h2, w_router_pad):
    return pl.pallas_call(
        _router_kernel,
        grid=(N_TILES,),
        in_specs=[
            pl.BlockSpec((TM, D_MODEL), lambda i: (i, 0)),
            pl.BlockSpec((D_MODEL, LANES), lambda i: (0, 0)),
        ],
        out_specs=[
            pl.BlockSpec((TM, LANES), lambda i: (i, 0)),
            pl.BlockSpec((TM, LANES), lambda i: (i, 0)),
            pl.BlockSpec((1, LANES), lambda i: (0, 0)),
        ],
        out_shape=[
            jax.ShapeDtypeStruct((N_TOK, LANES), jnp.int32),
            jax.ShapeDtypeStruct((N_TOK, LANES), F32),
            jax.ShapeDtypeStruct((1, LANES), F32),
        ],
        scratch_shapes=[pltpu.VMEM((1, LANES), F32)],
        compiler_params=_params(("arbitrary",)),
        name="router",
    )(h2, w_router_pad)


def _row_copy(src, src_row, dst, dst_row, sem):
    return pltpu.make_async_copy(src.at[pl.ds(src_row, 1)], dst.at[pl.ds(dst_row, 1)], sem)


def _dispatch_kernel(pos_ref, h_ref, xs_ref, sem):
    def issue(t, c):
        _row_copy(h_ref, t, xs_ref, pos_ref[2 * t], sem).start()
        _row_copy(h_ref, t, xs_ref, pos_ref[2 * t + 1], sem).start()
        return c

    lax.fori_loop(0, TM, issue, 0)

    def drain(t, c):
        _row_copy(h_ref, t, xs_ref, pos_ref[2 * t], sem).wait()
        _row_copy(h_ref, t, xs_ref, pos_ref[2 * t + 1], sem).wait()
        return c

    lax.fori_loop(0, TM, drain, 0)


def _dispatch(pos, h2f):
    return pl.pallas_call(
        _dispatch_kernel,
        grid=(N_TILES,),
        in_specs=[
            pl.BlockSpec((2 * TM,), lambda i: (i,), memory_space=pltpu.SMEM),
            pl.BlockSpec((TM, D_MODEL), lambda i: (i, 0)),
        ],
        out_specs=pl.BlockSpec(memory_space=pl.ANY),
        out_shape=jax.ShapeDtypeStruct((R_PAD, D_MODEL), F32),
        scratch_shapes=[pltpu.SemaphoreType.DMA(())],
        compiler_params=_params(("arbitrary",)),
        name="moe_dispatch",
    )(pos, h2f)


def _new_expert(te_ref, r):
    return jnp.logical_or(r == 0, te_ref[r] != te_ref[jnp.maximum(r - 1, 0)])


def _moe_gu_kernel(te_ref, nt_ref, x_ref, w_ref, o_ref, wb_s):
    r = pl.program_id(0)

    @pl.when(r < nt_ref[0])
    def _():
        @pl.when(_new_expert(te_ref, r))
        def _():
            wb_s[...] = w_ref[...].astype(BF)

        x = x_ref[...].astype(BF)
        g = jnp.dot(x, wb_s[:, :D_FF_EXPERT], preferred_element_type=F32)
        u = jnp.dot(x, wb_s[:, D_FF_EXPERT:], preferred_element_type=F32)
        o_ref[...] = (_silu(g) * u).astype(BF)


def _moe_gu(te, nt, xs, w_gu):
    grid_spec = pltpu.PrefetchScalarGridSpec(
        num_scalar_prefetch=2,
        grid=(NT_G,),
        in_specs=[
            pl.BlockSpec((TG, D_MODEL), lambda r, te, nt: (jnp.minimum(r, nt[0] - 1), 0)),
            pl.BlockSpec((None, None, D_MODEL, 2 * D_FF_EXPERT), lambda r, te, nt: (0, te[r], 0, 0)),
        ],
        out_specs=pl.BlockSpec((TG, D_FF_EXPERT), lambda r, te, nt: (jnp.minimum(r, nt[0] - 1), 0)),
        scratch_shapes=[pltpu.VMEM((D_MODEL, 2 * D_FF_EXPERT), BF)],
    )
    return pl.pallas_call(
        _moe_gu_kernel,
        grid_spec=grid_spec,
        out_shape=jax.ShapeDtypeStruct((R_PAD, D_FF_EXPERT), BF),
        compiler_params=_params(("arbitrary",)),
        name="moe_gu",
    )(te, nt, xs, w_gu)


def _moe_down_kernel(te_ref, nt_ref, a_ref, w_ref, o_ref, wb_s):
    r = pl.program_id(0)

    @pl.when(r < nt_ref[0])
    def _():
        @pl.when(_new_expert(te_ref, r))
        def _():
            wb_s[...] = w_ref[...].astype(BF)

        o_ref[...] = jnp.dot(a_ref[...], wb_s[...], preferred_element_type=F32)


def _moe_down(te, nt, act, w_down):
    grid_spec = pltpu.PrefetchScalarGridSpec(
        num_scalar_prefetch=2,
        grid=(NT_G,),
        in_specs=[
            pl.BlockSpec((TG, D_FF_EXPERT), lambda r, te, nt: (jnp.minimum(r, nt[0] - 1), 0)),
            pl.BlockSpec((None, None, D_FF_EXPERT, D_MODEL), lambda r, te, nt: (0, te[r], 0, 0)),
        ],
        out_specs=pl.BlockSpec((TG, D_MODEL), lambda r, te, nt: (jnp.minimum(r, nt[0] - 1), 0)),
        scratch_shapes=[pltpu.VMEM((D_FF_EXPERT, D_MODEL), BF)],
    )
    return pl.pallas_call(
        _moe_down_kernel,
        grid_spec=grid_spec,
        out_shape=jax.ShapeDtypeStruct((R_PAD, D_MODEL), F32),
        compiler_params=_params(("arbitrary",)),
        name="moe_down",
    )(te, nt, act, w_down)


def _combine_kernel(pos_ref, ys_ref, rg_ref, x_ref, g2_ref, fg_ref, o_ref, a_s, b_s, sem):
    def issue(t, c):
        _row_copy(ys_ref, pos_ref[2 * t], a_s, t, sem).start()
        _row_copy(ys_ref, pos_ref[2 * t + 1], b_s, t, sem).start()
        return c

    lax.fori_loop(0, TM, issue, 0)

    def drain(t, c):
        _row_copy(ys_ref, pos_ref[2 * t], a_s, t, sem).wait()
        _row_copy(ys_ref, pos_ref[2 * t + 1], b_s, t, sem).wait()
        return c

    lax.fori_loop(0, TM, drain, 0)

    rg = rg_ref[...]
    y = rg[:, 0:1] * a_s[...] + rg[:, 1:2] * b_s[...]
    xn = x_ref[...] + g2_ref[...] * y
    o_ref[...] = _rms(xn) * fg_ref[...]


def _combine(layer, pos, ys, rg, x, mod5, final_g):
    return pl.pallas_call(
        _combine_kernel,
        grid=(N_TILES,),
        in_specs=[
            pl.BlockSpec((2 * TM,), lambda i: (i,), memory_space=pltpu.SMEM),
            pl.BlockSpec(memory_space=pl.ANY),
            pl.BlockSpec((TM, LANES), lambda i: (i, 0)),
            pl.BlockSpec((TM, D_MODEL), lambda i: (i, 0)),
            pl.BlockSpec((None, None, None, 1, D_MODEL), lambda i: (layer, _mod_row(i), 5, 0, 0)),
            pl.BlockSpec((1, D_MODEL), lambda i: (0, 0)),
        ],
        out_specs=pl.BlockSpec((TM, D_MODEL), lambda i: (i, 0)),
        out_shape=jax.ShapeDtypeStruct((N_TOK, D_MODEL), F32),
        scratch_shapes=[
            pltpu.VMEM((TM, D_MODEL), F32),
            pltpu.VMEM((TM, D_MODEL), F32),
            pltpu.SemaphoreType.DMA(()),
        ],
        compiler_params=_params(("arbitrary",)),
        name="moe_combine",
    )(pos, ys, rg, x, mod5, final_g)


def _group_layout(ri, cnt):
    counts = cnt[0, :N_EXPERTS].astype(jnp.int32)
    tiles = (counts + TG - 1) // TG
    tile_end = jnp.cumsum(tiles)
    offs = (tile_end - tiles) * TG
    experts = ri[:, 0:2]
    ranks = ri[:, 2:4]
    pos = jnp.sum(jnp.where(experts[..., None] == jnp.arange(N_EXPERTS), offs, 0), axis=-1) + ranks
    nt = tile_end[-1]
    tile_id = jnp.minimum(jnp.arange(NT_G, dtype=jnp.int32), nt - 1)
    te = jnp.sum((tile_id[:, None] >= tile_end[None, :]).astype(jnp.int32), axis=-1)
    return pos.reshape(N_PAIRS).astype(jnp.int32), te.astype(jnp.int32), nt.reshape(1).astype(jnp.int32)


def _rope_tables():
    p = jnp.arange(DEC_SEQ)
    row = (p // GRID_W).astype(F32)
    col = (p % GRID_W).astype(F32)
    half = QK_DIM // 4
    freqs = ROPE_BASE ** (-jnp.arange(half, dtype=F32) / half)
    lane = jnp.arange(TN_IN)
    f = freqs[lane & (half - 1)]
    use_col = (lane & (2 * half)) != 0
    ang = jnp.where(use_col[None, :], col[:, None], row[:, None]) * f[None, :]
    upper = (lane & half) != 0
    return jnp.cos(ang), jnp.where(upper[None, :], jnp.sin(ang), -jnp.sin(ang))


def kernel(x_prompt, x_sample, cache_k, cache_v, c, c_ctx, w_ada, b_ada, norm_mix_g, norm_ffn_g,
           w_in, lambda_qk, subln_g, conv_w, w_out, w_gu_dense, w_down_dense, w_router,
           w_gu_moe, w_down_moe, final_g):
    x = jnp.concatenate([x_prompt.reshape(N_CTX, D_MODEL), x_sample.reshape(N_LAT, D_MODEL)], axis=0)
    cond = jnp.concatenate([c_ctx[None, :], c, jnp.zeros((COND_ROWS - 1 - DEC_BATCH, D_MODEL), F32)], axis=0)
    mod5 = _ada(cond, w_ada, b_ada).reshape(DEPTH, COND_ROWS, N_MOD, 1, D_MODEL)
    cos_t, sin_t = _rope_tables()
    cache_k4 = cache_k.reshape(DEC_BATCH, DEPTH, PAST_LEN, ATT_WIDTH)
    cache_v4 = cache_v.reshape(DEC_BATCH, DEPTH, PAST_LEN, ATT_WIDTH)
    g_mix = norm_mix_g.reshape(DEPTH, 1, D_MODEL)
    g_ffn = norm_ffn_g.reshape(DEPTH, 1, D_MODEL)
    sg = subln_g.reshape(DEPTH, 1, V_DIM)

    new_k, new_v = [], []
    y = None
    for layer in range(DEPTH):
        lam_init = 0.8 - 0.6 * math.exp(-0.3 * layer)
        moe = layer % 2 == 1
        qkv, kvf, conv = _in_proj(layer, x, mod5, g_mix, w_in, cos_t, sin_t, conv_w)
        new_k.append(kvf[:, :ATT_WIDTH].reshape(BATCH, SEQ, N_HEADS, 2, QK_DIM))
        new_v.append(kvf[:, ATT_WIDTH:].reshape(BATCH, SEQ, N_HEADS, V_DIM))
        att = _attn_ctx(layer, qkv, lambda_qk, sg, lam_init)
        att = _attn_lat(layer, qkv, cache_k4, cache_v4, lambda_qk, sg, att, lam_init)
        outs = _out_proj(layer, att, conv, w_out, x, mod5, g_ffn, with_f32=moe)
        if not moe:
            x1, h2 = outs
            act = _dense_gu(h2, w_gu_dense)
            x = _dense_down(layer, act, w_down_dense, x1, mod5)
        else:
            x1, h2, h2f = outs
            wr = jnp.pad(w_router[layer // 2], ((0, 0), (0, LANES - N_EXPERTS)))
            ri, rg, cnt = _router(h2, wr)
            pos, te, nt = _group_layout(ri, cnt)
            xs = _dispatch(pos, h2f)
            act = _moe_gu(te, nt, xs, w_gu_moe)
            ys = _moe_down(te, nt, act, w_down_moe)
            y = _combine(layer, pos, ys, rg, x1, mod5, final_g.reshape(1, D_MODEL))
    y_prompt = y[:N_CTX].reshape(BATCH, SEQ, D_MODEL)
    y_sample = y[N_CTX:].reshape(DEC_BATCH, DEC_SEQ, D_MODEL)
    return y_prompt, y_sample, jnp.stack(new_k, axis=1), jnp.stack(new_v, axis=1)
```

```python
import functools
import math

import numpy as np
import jax
import jax.numpy as jnp
from jax import lax
from jax.experimental import pallas as pl
from jax.experimental.pallas import tpu as pltpu

D_MODEL = 1024
BATCH = 16
SEQ = 256
DEPTH = 2
DEC_BATCH = 4
DEC_SEQ = 1024
PAST_LEN = 512
GRID_W = 64
ATT_WIDTH = 512
CONV_WIDTH = 512
N_HEADS = 4
V_DIM = 128
QK_DIM = 64
ROPE_BASE = 10000.0
D_FF = 2816
N_EXPERTS = 8
D_FF_EXPERT = 1408
N_MOD = 6
NORM_EPS = 1e-6
IN_COLS = 3 * ATT_WIDTH + 3 * CONV_WIDTH

N_CTX = BATCH * SEQ
N_LAT = DEC_BATCH * DEC_SEQ
N_TOK = N_CTX + N_LAT
TM = 1024
N_TILES = N_TOK // TM
CTX_TILES = N_CTX // TM
SEQ_PER_TILE = TM // SEQ
COND_ROWS = 8
TN_IN = 512
N_IN_TILES = IN_COLS // TN_IN
TN_FF = 256
N_FF_TILES = D_FF // TN_FF
TN_DOWN = 512
N_DOWN_TILES = D_MODEL // TN_DOWN
TN_ADA = 1536
TG = 512
N_PAIRS = 2 * N_TOK
NT_G = N_PAIRS // TG + N_EXPERTS
R_PAD = NT_G * TG
LANES = 128
VMEM_LIMIT = 60 * 1024 * 1024

BF = jnp.bfloat16
F32 = jnp.float32


def _params(sem, vmem=VMEM_LIMIT):
    return pltpu.CompilerParams(dimension_semantics=sem, vmem_limit_bytes=vmem)


def _mod_row(i):
    return jnp.where(i < CTX_TILES, 0, i - (CTX_TILES - 1))


def _mod_spec(layer, c, width=D_MODEL):
    return pl.BlockSpec((None, None, None, 1, width), lambda i, *_: (layer, _mod_row(i), c, 0, 0))


def _stream_specs(pair, width=D_MODEL):
    a = pl.BlockSpec((TM, width), lambda i, *_: (jnp.minimum(i, CTX_TILES - 1), 0))
    if pair:
        b = pl.BlockSpec((TM, width), lambda i, *_: (jnp.maximum(i - CTX_TILES, 0), 0))
    else:
        b = pl.BlockSpec((TM, width), lambda i, *_: (jnp.maximum(i, CTX_TILES), 0))
    return a, b


def _stream_rows(xa_ref, xb_ref, i):
    return jnp.where(i >= CTX_TILES, xb_ref[...], xa_ref[...])


def _silu(x):
    return x / (1.0 + jnp.exp(-x))


def _rms(x):
    return x * lax.rsqrt(jnp.mean(x * x, axis=-1, keepdims=True) + NORM_EPS)


def _ada_kernel(c_ref, w_ref, b_ref, o_ref):
    s = _silu(c_ref[...]).astype(BF)
    o_ref[...] = jnp.dot(s, w_ref[...].astype(BF), preferred_element_type=F32) + b_ref[...]


def _ada(cond, w_ada, b_ada):
    n = N_MOD * D_MODEL
    return pl.pallas_call(
        _ada_kernel,
        grid=(DEPTH, n // TN_ADA),
        in_specs=[
            pl.BlockSpec((COND_ROWS, D_MODEL), lambda l, j: (0, 0)),
            pl.BlockSpec((None, D_MODEL, TN_ADA), lambda l, j: (l, 0, j)),
            pl.BlockSpec((None, 1, TN_ADA), lambda l, j: (l, 0, j)),
        ],
        out_specs=pl.BlockSpec((None, COND_ROWS, TN_ADA), lambda l, j: (l, 0, j)),
        out_shape=jax.ShapeDtypeStruct((DEPTH, COND_ROWS, n), F32),
        compiler_params=_params(("arbitrary", "arbitrary")),
        name="ada_mod",
    )(cond, w_ada, b_ada.reshape(DEPTH, 1, n))


def _in_kernel(*refs, aliased):
    xa_ref, xb_ref, sh_ref, sc_ref, g_ref, w_ref, cos_ref, sin_ref, cw_ref = refs[:9]
    refs = refs[11:] if aliased else refs[9:]
    q_ref, v_ref, kt_ref, conv_ref, nk_ref, nv_ref, h_s, wb_s, gb_s, gc_s = refs
    i = pl.program_id(0)
    j = pl.program_id(1)
    lat = i >= CTX_TILES
    ctx = jnp.logical_not(lat)

    @pl.when(j == 0)
    def _():
        y = _rms(_stream_rows(xa_ref, xb_ref, i)) * g_ref[...]
        h_s[...] = (y * (1.0 + sc_ref[...]) + sh_ref[...]).astype(BF)

    @pl.when(i == 0)
    def _():
        wb_s[j] = w_ref[...].astype(BF)

    acc = jnp.dot(h_s[...], wb_s[j], preferred_element_type=F32)

    def roped(a):
        cos = jnp.concatenate([cos_ref[...]] * N_HEADS, axis=1)
        sin = jnp.concatenate([sin_ref[...]] * N_HEADS, axis=1)
        lane = lax.broadcasted_iota(jnp.int32, a.shape, 1)
        upper = (lane & (QK_DIM // 4)) != 0
        partner = jnp.where(upper, pltpu.roll(a, QK_DIM // 4, 1), pltpu.roll(a, TN_IN - QK_DIM // 4, 1))
        return a * cos + partner * sin

    @pl.when(jnp.logical_and(j == 0, lat))
    def _():
        q_ref[...] = roped(acc).astype(BF)

    @pl.when(jnp.logical_and(j == 0, ctx))
    def _():
        q_ref[...] = acc.astype(BF)

    @pl.when(jnp.logical_and(j == 1, lat))
    def _():
        kt_ref[...] = roped(acc).T.astype(BF)

    @pl.when(jnp.logical_and(j == 1, ctx))
    def _():
        kt = acc.T
        kt_ref[...] = kt.astype(BF)
        for s in range(SEQ_PER_TILE):
            nk_ref[s] = kt[:, s * SEQ:(s + 1) * SEQ]

    @pl.when(j == 2)
    def _():
        v_ref[...] = acc.astype(BF)

    @pl.when(jnp.logical_and(j == 2, ctx))
    def _():
        for s in range(SEQ_PER_TILE):
            for h in range(N_HEADS):
                nv_ref[s, pl.ds(h, SEQ, stride=N_HEADS), :] = acc[s * SEQ:(s + 1) * SEQ, h * V_DIM:(h + 1) * V_DIM]

    @pl.when(j == 3)
    def _():
        gb_s[...] = acc

    @pl.when(j == 4)
    def _():
        gc_s[...] = acc

    @pl.when(j == 5)
    def _():
        u = gc_s[...] * acc
        seq = jnp.where(lat, DEC_SEQ, SEQ)
        pos = lax.broadcasted_iota(jnp.int32, (TM, 1), 0) & (seq - 1)
        prev = jnp.where(pos == 0, 0.0, pltpu.roll(u, 1, 0))
        nxt = jnp.where(pos == seq - 1, 0.0, pltpu.roll(u, TM - 1, 0))
        cw = cw_ref[...]
        conv = prev * cw[0:1] + u * cw[1:2] + nxt * cw[2:3]
        conv_ref[...] = (gb_s[...] * conv).astype(BF)


def _in_proj(layer, xs, mod5, g_mix, w_in, cos_t, sin_t, conv_w, new_kv):
    pair = isinstance(xs, tuple)
    xa, xb = xs if pair else (xs, xs)
    spec_a, spec_b = _stream_specs(pair)
    ctx_i = lambda i: jnp.minimum(i, CTX_TILES - 1)
    in_specs = [
        spec_a, spec_b,
        _mod_spec(layer, 0), _mod_spec(layer, 1),
        pl.BlockSpec((None, 1, D_MODEL), lambda i, j: (layer, 0, 0)),
        pl.BlockSpec((None, D_MODEL, TN_IN), lambda i, j: (layer, 0, jnp.where(i == 0, j, N_IN_TILES - 1))),
        pl.BlockSpec((DEC_SEQ, V_DIM), lambda i, j: (0, 0)),
        pl.BlockSpec((DEC_SEQ, V_DIM), lambda i, j: (0, 0)),
        pl.BlockSpec((None, 3, CONV_WIDTH), lambda i, j: (layer, 0, 0)),
    ]
    args = [xa, xb, mod5, mod5, g_mix, w_in, cos_t, sin_t, conv_w]
    aliases = {}
    if new_kv is not None:
        in_specs += [pl.BlockSpec(memory_space=pl.ANY), pl.BlockSpec(memory_space=pl.ANY)]
        args += list(new_kv)
        aliases = {9: 4, 10: 5}
    row_tile = pl.BlockSpec((TM, TN_IN), lambda i, j: (i, 0))
    return pl.pallas_call(
        functools.partial(_in_kernel, aliased=new_kv is not None),
        grid=(N_TILES, N_IN_TILES),
        in_specs=in_specs,
        out_specs=[
            row_tile,
            row_tile,
            pl.BlockSpec((None, ATT_WIDTH, TM), lambda i, j: (i, 0, 0)),
            row_tile,
            pl.BlockSpec((SEQ_PER_TILE, None, ATT_WIDTH, SEQ), lambda i, j: (ctx_i(i), layer, 0, 0)),
            pl.BlockSpec((SEQ_PER_TILE, None, SEQ * N_HEADS, V_DIM), lambda i, j: (ctx_i(i), layer, 0, 0)),
        ],
        out_shape=[
            jax.ShapeDtypeStruct((N_TOK, ATT_WIDTH), BF),
            jax.ShapeDtypeStruct((N_TOK, ATT_WIDTH), BF),
            jax.ShapeDtypeStruct((N_TILES, ATT_WIDTH, TM), BF),
            jax.ShapeDtypeStruct((N_TOK, CONV_WIDTH), BF),
            jax.ShapeDtypeStruct((BATCH, DEPTH, ATT_WIDTH, SEQ), F32),
            jax.ShapeDtypeStruct((BATCH, DEPTH, SEQ * N_HEADS, V_DIM), F32),
        ],
        scratch_shapes=[
            pltpu.VMEM((TM, D_MODEL), BF),
            pltpu.VMEM((N_IN_TILES, D_MODEL, TN_IN), BF),
            pltpu.VMEM((TM, CONV_WIDTH), F32),
            pltpu.VMEM((TM, CONV_WIDTH), F32),
        ],
        input_output_aliases=aliases,
        compiler_params=_params(("arbitrary", "arbitrary")),
        name=f"in_proj_l{layer}",
    )(*args)


def _lambda(lq_ref, lam_init):
    lq = lq_ref[...]
    a = jnp.exp(jnp.sum(lq[0:1] * lq[1:2], axis=-1, keepdims=True))
    b = jnp.exp(jnp.sum(lq[2:3] * lq[3:4], axis=-1, keepdims=True))
    return a - b + lam_init


def _head_norm(o, sg, lam_init):
    return _rms(o) * sg * (1.0 - lam_init)


def _attn_ctx_kernel(q_ref, kt_ref, v_ref, lq_ref, sg_ref, o_ref, *, lam_init):
    lam = _lambda(lq_ref, lam_init)
    sg = sg_ref[...]
    for h in range(N_HEADS):
        cols = slice(h * V_DIM, (h + 1) * V_DIM)
        q = q_ref[:, cols] * (QK_DIM ** -0.5)
        probs = []
        for s in range(2):
            d = slice(s * QK_DIM, (s + 1) * QK_DIM)
            kt = kt_ref[h * V_DIM + s * QK_DIM:h * V_DIM + (s + 1) * QK_DIM, :]
            sc = jnp.dot(q[:, d], kt, preferred_element_type=F32)
            e = jnp.exp(sc - jnp.max(sc, axis=-1, keepdims=True))
            probs.append(e * (1.0 / jnp.sum(e, axis=-1, keepdims=True)))
        a = (probs[0] - lam * probs[1]).astype(BF)
        o = jnp.dot(a, v_ref[:, cols], preferred_element_type=F32)
        o_ref[:, cols] = _head_norm(o, sg, lam_init).astype(BF)


def _attn_ctx(layer, q, kt, v, lambda_qk, subln_g, lam_init):
    return pl.pallas_call(
        functools.partial(_attn_ctx_kernel, lam_init=lam_init),
        grid=(BATCH,),
        in_specs=[
            pl.BlockSpec((SEQ, ATT_WIDTH), lambda b: (b, 0)),
            pl.BlockSpec((None, ATT_WIDTH, SEQ), lambda b: (b // SEQ_PER_TILE, 0, b % SEQ_PER_TILE)),
            pl.BlockSpec((SEQ, ATT_WIDTH), lambda b: (b, 0)),
            pl.BlockSpec((None, 4, QK_DIM), lambda b: (layer, 0, 0)),
            pl.BlockSpec((None, 1, V_DIM), lambda b: (layer, 0, 0)),
        ],
        out_specs=pl.BlockSpec((SEQ, ATT_WIDTH), lambda b: (b, 0)),
        out_shape=jax.ShapeDtypeStruct((N_TOK, ATT_WIDTH), BF),
        compiler_params=_params(("arbitrary",)),
        name=f"attn_ctx_l{layer}",
    )(q, kt, v, lambda_qk, subln_g)


TQ = 256


def _attn_lat_kernel(q_ref, kt_ref, v_ref, ckt_ref, cv_ref, lq_ref, sg_ref, att_in_ref, o_ref, *, lam_init):
    del att_in_ref
    lam = _lambda(lq_ref, lam_init)
    sg = sg_ref[...]
    for h in range(N_HEADS):
        cols = slice(h * V_DIM, (h + 1) * V_DIM)
        q = q_ref[:, cols] * (QK_DIM ** -0.5)
        probs = []
        for s in range(2):
            d = slice(s * QK_DIM, (s + 1) * QK_DIM)
            rows = slice(h * V_DIM + s * QK_DIM, h * V_DIM + (s + 1) * QK_DIM)
            sc_c = jnp.dot(q[:, d], ckt_ref[rows, :].astype(BF), preferred_element_type=F32)
            sc_l = jnp.dot(q[:, d], kt_ref[rows, :], preferred_element_type=F32)
            m = jnp.maximum(jnp.max(sc_c, axis=-1, keepdims=True), jnp.max(sc_l, axis=-1, keepdims=True))
            e_c = jnp.exp(sc_c - m)
            e_l = jnp.exp(sc_l - m)
            r = 1.0 / (jnp.sum(e_c, axis=-1, keepdims=True) + jnp.sum(e_l, axis=-1, keepdims=True))
            probs.append((e_c * r, e_l * r))
        a_c = (probs[0][0] - lam * probs[1][0]).astype(BF)
        a_l = (probs[0][1] - lam * probs[1][1]).astype(BF)
        vc = cv_ref[pl.ds(h, PAST_LEN, stride=N_HEADS), :].astype(BF)
        o = jnp.dot(a_c, vc, preferred_element_type=F32)
        o = o + jnp.dot(a_l, v_ref[:, cols], preferred_element_type=F32)
        o_ref[:, cols] = _head_norm(o, sg, lam_init).astype(BF)


def _attn_lat(layer, q, kt, v, cache_kt, cache_v, lambda_qk, subln_g, att, lam_init):
    nqb = DEC_SEQ // TQ
    q0 = N_CTX // TQ
    return pl.pallas_call(
        functools.partial(_attn_lat_kernel, lam_init=lam_init),
        grid=(DEC_BATCH, nqb),
        in_specs=[
            pl.BlockSpec((TQ, ATT_WIDTH), lambda b, t: (q0 + b * nqb + t, 0)),
            pl.BlockSpec((None, ATT_WIDTH, DEC_SEQ), lambda b, t: (CTX_TILES + b, 0, 0)),
            pl.BlockSpec((DEC_SEQ, ATT_WIDTH), lambda b, t: (CTX_TILES + b, 0)),
            pl.BlockSpec((None, None, ATT_WIDTH, PAST_LEN), lambda b, t: (b, layer, 0, 0)),
            pl.BlockSpec((None, None, PAST_LEN * N_HEADS, V_DIM), lambda b, t: (b, layer, 0, 0)),
            pl.BlockSpec((None, 4, QK_DIM), lambda b, t: (layer, 0, 0)),
            pl.BlockSpec((None, 1, V_DIM), lambda b, t: (layer, 0, 0)),
            pl.BlockSpec(memory_space=pl.ANY),
        ],
        out_specs=pl.BlockSpec((TQ, ATT_WIDTH), lambda b, t: (q0 + b * nqb + t, 0)),
        out_shape=jax.ShapeDtypeStruct((N_TOK, ATT_WIDTH), BF),
        input_output_aliases={7: 0},
        compiler_params=_params(("arbitrary", "arbitrary")),
        name=f"attn_lat_l{layer}",
    )(q, kt, v, cache_kt, cache_v, lambda_qk, subln_g, att)


def _out_kernel(att_ref, conv_ref, w_ref, xa_ref, xb_ref, g1_ref, sh_ref, sc_ref, gf_ref,
                xo_ref, h2_ref, *rest, with_f32):
    if with_f32:
        h2f_ref, wb_s = rest
    else:
        (wb_s,) = rest
    i = pl.program_id(0)

    @pl.when(i == 0)
    def _():
        wb_s[...] = w_ref[...].astype(BF)

    mo = jnp.dot(att_ref[...], wb_s[:ATT_WIDTH, :], preferred_element_type=F32)
    mo = mo + jnp.dot(conv_ref[...], wb_s[ATT_WIDTH:, :], preferred_element_type=F32)
    xn = _stream_rows(xa_ref, xb_ref, i) + g1_ref[...] * mo
    xo_ref[...] = xn
    h2 = (_rms(xn) * gf_ref[...]) * (1.0 + sc_ref[...]) + sh_ref[...]
    h2_ref[...] = h2.astype(BF)
    if with_f32:
        h2f_ref[...] = h2


def _out_proj(layer, att, conv, w_out, xs, mod5, g_ffn, with_f32):
    pair = isinstance(xs, tuple)
    xa, xb = xs if pair else (xs, xs)
    spec_a, spec_b = _stream_specs(pair)
    row_spec = pl.BlockSpec((TM, D_MODEL), lambda i: (i, 0))
    out_specs = [row_spec, row_spec]
    out_shape = [jax.ShapeDtypeStruct((N_TOK, D_MODEL), F32), jax.ShapeDtypeStruct((N_TOK, D_MODEL), BF)]
    if with_f32:
        out_specs.append(row_spec)
        out_shape.append(jax.ShapeDtypeStruct((N_TOK, D_MODEL), F32))
    return pl.pallas_call(
        functools.partial(_out_kernel, with_f32=with_f32),
        grid=(N_TILES,),
        in_specs=[
            pl.BlockSpec((TM, ATT_WIDTH), lambda i: (i, 0)),
            pl.BlockSpec((TM, CONV_WIDTH), lambda i: (i, 0)),
            pl.BlockSpec((None, D_MODEL, D_MODEL), lambda i: (layer, 0, 0)),
            spec_a, spec_b,
            _mod_spec(layer, 2), _mod_spec(layer, 3), _mod_spec(layer, 4),
            pl.BlockSpec((None, 1, D_MODEL), lambda i: (layer, 0, 0)),
        ],
        out_specs=out_specs,
        out_shape=out_shape,
        scratch_shapes=[pltpu.VMEM((D_MODEL, D_MODEL), BF)],
        compiler_params=_params(("arbitrary",)),
        name=f"out_proj_l{layer}",
    )(att, conv, w_out, xa, xb, mod5, mod5, mod5, g_ffn)


def _gu_kernel(h_ref, wg_ref, wu_ref, o_ref, wb_s):
    i = pl.program_id(0)
    j = pl.program_id(1)

    @pl.when(i == 0)
    def _():
        wb_s[j] = wg_ref[...].astype(BF)
        wb_s[j + N_FF_TILES] = wu_ref[...].astype(BF)

    h = h_ref[...]
    g = jnp.dot(h, wb_s[j], preferred_element_type=F32)
    u = jnp.dot(h, wb_s[j + N_FF_TILES], preferred_element_type=F32)
    o_ref[...] = (_silu(g) * u).astype(BF)


def _dense_gu(h2, w_gu):
    nj = N_FF_TILES
    return pl.pallas_call(
        _gu_kernel,
        grid=(N_TILES, nj),
        in_specs=[
            pl.BlockSpec((TM, D_MODEL), lambda i, j: (i, 0)),
            pl.BlockSpec((None, D_MODEL, TN_FF), lambda i, j: (0, 0, jnp.where(i == 0, j, nj - 1))),
            pl.BlockSpec((None, D_MODEL, TN_FF), lambda i, j: (0, 0, jnp.where(i == 0, j, nj - 1) + nj)),
        ],
        out_specs=pl.BlockSpec((TM, TN_FF), lambda i, j: (i, j)),
        out_shape=jax.ShapeDtypeStruct((N_TOK, D_FF), BF),
        scratch_shapes=[pltpu.VMEM((2 * nj, D_MODEL, TN_FF), BF)],
        compiler_params=_params(("arbitrary", "arbitrary")),
        name="dense_gu",
    )(h2, w_gu, w_gu)


def _down_kernel(a_ref, w_ref, x_ref, g2_ref, o_ref, wb_s):
    j = pl.program_id(1)

    @pl.when(pl.program_id(0) == 0)
    def _():
        wb_s[j] = w_ref[...].astype(BF)

    y = jnp.dot(a_ref[...], wb_s[j], preferred_element_type=F32)
    o_ref[...] = x_ref[...] + g2_ref[...] * y


def _dense_down(layer, act, w_down, x, mod5):
    nj = N_DOWN_TILES
    return pl.pallas_call(
        _down_kernel,
        grid=(N_TILES, nj),
        in_specs=[
            pl.BlockSpec((TM, D_FF), lambda i, j: (i, 0)),
            pl.BlockSpec((None, D_FF, TN_DOWN), lambda i, j: (0, 0, jnp.where(i == 0, j, nj - 1))),
            pl.BlockSpec((TM, TN_DOWN), lambda i, j: (i, j)),
            pl.BlockSpec((None, None, None, 1, TN_DOWN), lambda i, j: (layer, _mod_row(i), 5, 0, j)),
        ],
        out_specs=pl.BlockSpec((TM, TN_DOWN), lambda i, j: (i, j)),
        out_shape=jax.ShapeDtypeStruct((N_TOK, D_MODEL), F32),
        scratch_shapes=[pltpu.VMEM((nj, D_FF, TN_DOWN), BF)],
        compiler_params=_params(("arbitrary", "arbitrary")),
        name="dense_down",
    )(act, w_down, x, mod5)


def _router_kernel(h_ref, wr_ref, ri_ref, rg_ref, cnt_ref, base_s):
    @pl.when(pl.program_id(0) == 0)
    def _():
        base_s[...] = jnp.zeros_like(base_s)

    logits = jnp.dot(h_ref[...], wr_ref[...].astype(BF), preferred_element_type=F32)
    lane = lax.broadcasted_iota(jnp.int32, logits.shape, 1)
    lg = jnp.where(lane < N_EXPERTS, logits, -jnp.inf)
    m1 = jnp.max(lg, axis=-1, keepdims=True)
    i1 = jnp.min(jnp.where(lg == m1, lane, LANES), axis=-1, keepdims=True)
    lg2 = jnp.where(lane == i1, -jnp.inf, lg)
    m2 = jnp.max(lg2, axis=-1, keepdims=True)
    i2 = jnp.min(jnp.where(lg2 == m2, lane, LANES), axis=-1, keepdims=True)
    e2 = jnp.exp(m2 - m1)
    w1 = 1.0 / (1.0 + e2)
    w2 = e2 / (1.0 + e2)

    sel1 = lane == i1
    sel2 = lane == i2
    onehot = jnp.logical_or(sel1, sel2)
    rows = lax.broadcasted_iota(jnp.int32, (TM, TM), 0)
    colsi = lax.broadcasted_iota(jnp.int32, (TM, TM), 1)
    tri = (colsi < rows).astype(BF)
    before = jnp.dot(tri, onehot.astype(BF), preferred_element_type=F32) + base_s[...]
    rank1 = jnp.sum(jnp.where(sel1, before, 0.0), axis=-1, keepdims=True)
    rank2 = jnp.sum(jnp.where(sel2, before, 0.0), axis=-1, keepdims=True)
    base_s[...] = base_s[...] + jnp.sum(onehot.astype(F32), axis=0, keepdims=True)
    cnt_ref[...] = base_s[...]

    ri = jnp.where(lane == 0, i1, jnp.where(lane == 1, i2, 0))
    ri = jnp.where(lane == 2, rank1.astype(jnp.int32), jnp.where(lane == 3, rank2.astype(jnp.int32), ri))
    ri_ref[...] = ri
    rg_ref[...] = jnp.where(lane == 0, w1, jnp.where(lane == 1, w2, 0.0))


def _router(h2, w_router_pad):
    return pl.pallas_call(
        _router_kernel,
        grid=(N_TILES,),
        in_specs=[
            pl.BlockSpec((TM, D_MODEL), lambda i: (i, 0)),
            pl.BlockSpec((D_MODEL, LANES), lambda i: (0, 0)),
        ],
        out_specs=[
            pl.BlockSpec((TM, LANES), lambda i: (i, 0)),
            pl.BlockSpec((TM, LANES), lambda i: (i, 0)),
            pl.BlockSpec((1, LANES), lambda i: (0, 0)),
        ],
        out_shape=[
            jax.ShapeDtypeStruct((N_TOK, LANES), jnp.int32),
            jax.ShapeDtypeStruct((N_TOK, LANES), F32),
            jax.ShapeDtypeStruct((1, LANES), F32),
        ],
        scratch_shapes=[pltpu.VMEM((1, LANES), F32)],
        compiler_params=_params(("arbitrary",)),
        name="router",
    )(h2, w_router_pad)


def _row_copy(src, src_row, dst, dst_row, sem):
    return pltpu.make_async_copy(src.at[pl.ds(src_row, 1)], dst.at[pl.ds(dst_row, 1)], sem)


def _dispatch_kernel(pos_ref, h_ref, xs_ref, sem):
    def issue(t, c):
        _row_copy(h_ref, t, xs_ref, pos_ref[2 * t], sem).start()
        _row_copy(h_ref, t, xs_ref, pos_ref[2 * t + 1], sem).start()
        return c

    lax.fori_loop(0, TM, issue, 0)

    def drain(t, c):
        _row_copy(h_ref, t, xs_ref, pos_ref[2 * t], sem).wait()
        _row_copy(h_ref, t, xs_ref, pos_ref[2 * t + 1], sem).wait()
        return c

    lax.fori_loop(0, TM, drain, 0)


def _dispatch(pos, h2f):
    return pl.pallas_call(
        _dispatch_kernel,
        grid=(N_TILES,),
        in_specs=[
            pl.BlockSpec((2 * TM,), lambda i: (i,), memory_space=pltpu.SMEM),
            pl.BlockSpec((TM, D_MODEL), lambda i: (i, 0)),
        ],
        out_specs=pl.BlockSpec(memory_space=pl.ANY),
        out_shape=jax.ShapeDtypeStruct((R_PAD, D_MODEL), F32),
        scratch_shapes=[pltpu.SemaphoreType.DMA(())],
        compiler_params=_params(("arbitrary",)),
        name="moe_dispatch",
    )(pos, h2f)


def _new_expert(te_ref, r):
    return jnp.logical_or(r == 0, te_ref[r] != te_ref[jnp.maximum(r - 1, 0)])


def _moe_gu_kernel(te_ref, nt_ref, x_ref, w_ref, o_ref, wb_s):
    r = pl.program_id(0)

    @pl.when(r < nt_ref[0])
    def _():
        @pl.when(_new_expert(te_ref, r))
        def _():
            wb_s[...] = w_ref[...].astype(BF)

        x = x_ref[...].astype(BF)
        g = jnp.dot(x, wb_s[:, :D_FF_EXPERT], preferred_element_type=F32)
        u = jnp.dot(x, wb_s[:, D_FF_EXPERT:], preferred_element_type=F32)
        o_ref[...] = (_silu(g) * u).astype(BF)


def _moe_gu(te, nt, xs, w_gu):
    grid_spec = pltpu.PrefetchScalarGridSpec(
        num_scalar_prefetch=2,
        grid=(NT_G,),
        in_specs=[
            pl.BlockSpec((TG, D_MODEL), lambda r, te, nt: (jnp.minimum(r, nt[0] - 1), 0)),
            pl.BlockSpec((None, None, D_MODEL, 2 * D_FF_EXPERT), lambda r, te, nt: (0, te[r], 0, 0)),
        ],
        out_specs=pl.BlockSpec((TG, D_FF_EXPERT), lambda r, te, nt: (jnp.minimum(r, nt[0] - 1), 0)),
        scratch_shapes=[pltpu.VMEM((D_MODEL, 2 * D_FF_EXPERT), BF)],
    )
    return pl.pallas_call(
        _moe_gu_kernel,
        grid_spec=grid_spec,
        out_shape=jax.ShapeDtypeStruct((R_PAD, D_FF_EXPERT), BF),
        compiler_params=_params(("arbitrary",)),
        name="moe_gu",
    )(te, nt, xs, w_gu)


def _moe_down_kernel(te_ref, nt_ref, a_ref, w_ref, o_ref, wb_s):
    r = pl.program_id(0)

    @pl.when(r < nt_ref[0])
    def _():
        @pl.when(_new_expert(te_ref, r))
        def _():
            wb_s[...] = w_ref[...].astype(BF)

        o_ref[...] = jnp.dot(a_ref[...], wb_s[...], preferred_element_type=F32)


def _moe_down(te, nt, act, w_down):
    grid_spec = pltpu.PrefetchScalarGridSpec(
        num_scalar_prefetch=2,
        grid=(NT_G,),
        in_specs=[
            pl.BlockSpec((TG, D_FF_EXPERT), lambda r, te, nt: (jnp.minimum(r, nt[0] - 1), 0)),
            pl.BlockSpec((None, None, D_FF_EXPERT, D_MODEL), lambda r, te, nt: (0, te[r], 0, 0)),
        ],
        out_specs=pl.BlockSpec((TG, D_MODEL), lambda r, te, nt: (jnp.minimum(r, nt[0] - 1), 0)),
        scratch_shapes=[pltpu.VMEM((D_FF_EXPERT, D_MODEL), BF)],
    )
    return pl.pallas_call(
        _moe_down_kernel,
        grid_spec=grid_spec,
        out_shape=jax.ShapeDtypeStruct((R_PAD, D_MODEL), F32),
        compiler_params=_params(("arbitrary",)),
        name="moe_down",
    )(te, nt, act, w_down)


def _combine_kernel(pos_ref, ys_ref, rg_ref, x_ref, g2_ref, fg_ref, oa_ref, ob_ref, a_s, b_s, sem):
    i = pl.program_id(0)

    def issue(t, c):
        _row_copy(ys_ref, pos_ref[2 * t], a_s, t, sem).start()
        _row_copy(ys_ref, pos_ref[2 * t + 1], b_s, t, sem).start()
        return c

    lax.fori_loop(0, TM, issue, 0)

    def drain(t, c):
        _row_copy(ys_ref, pos_ref[2 * t], a_s, t, sem).wait()
        _row_copy(ys_ref, pos_ref[2 * t + 1], b_s, t, sem).wait()
        return c

    lax.fori_loop(0, TM, drain, 0)

    rg = rg_ref[...]
    y = rg[:, 0:1] * a_s[...] + rg[:, 1:2] * b_s[...]
    xn = x_ref[...] + g2_ref[...] * y
    out = _rms(xn) * fg_ref[...]

    @pl.when(i < CTX_TILES)
    def _():
        oa_ref[...] = out

    @pl.when(i >= CTX_TILES)
    def _():
        ob_ref[...] = out


def _combine(layer, pos, ys, rg, x, mod5, final_g):
    return pl.pallas_call(
        _combine_kernel,
        grid=(N_TILES,),
        in_specs=[
            pl.BlockSpec((2 * TM,), lambda i: (i,), memory_space=pltpu.SMEM),
            pl.BlockSpec(memory_space=pl.ANY),
            pl.BlockSpec((TM, LANES), lambda i: (i, 0)),
            pl.BlockSpec((TM, D_MODEL), lambda i: (i, 0)),
            _mod_spec(layer, 5),
            pl.BlockSpec((1, D_MODEL), lambda i: (0, 0)),
        ],
        out_specs=[
            pl.BlockSpec((TM, D_MODEL), lambda i: (jnp.minimum(i, CTX_TILES - 1), 0)),
            pl.BlockSpec((TM, D_MODEL), lambda i: (jnp.maximum(i - CTX_TILES, 0), 0)),
        ],
        out_shape=[
            jax.ShapeDtypeStruct((N_CTX, D_MODEL), F32),
            jax.ShapeDtypeStruct((N_LAT, D_MODEL), F32),
        ],
        scratch_shapes=[
            pltpu.VMEM((TM, D_MODEL), F32),
            pltpu.VMEM((TM, D_MODEL), F32),
            pltpu.SemaphoreType.DMA(()),
        ],
        compiler_params=_params(("arbitrary",)),
        name="moe_combine",
    )(pos, ys, rg, x, mod5, final_g)


def _group_layout(ri, cnt):
    counts = cnt[0, :N_EXPERTS].astype(jnp.int32)
    tiles = (counts + TG - 1) // TG
    tile_end = jnp.cumsum(tiles)
    offs = (tile_end - tiles) * TG
    experts = ri[:, 0:2]
    ranks = ri[:, 2:4]
    pos = jnp.sum(jnp.where(experts[..., None] == jnp.arange(N_EXPERTS), offs, 0), axis=-1) + ranks
    nt = tile_end[-1]
    tile_id = jnp.minimum(jnp.arange(NT_G, dtype=jnp.int32), nt - 1)
    te = jnp.sum((tile_id[:, None] >= tile_end[None, :]).astype(jnp.int32), axis=-1)
    return pos.reshape(N_PAIRS).astype(jnp.int32), te.astype(jnp.int32), nt.reshape(1).astype(jnp.int32)


def _rope_tables():
    p = np.arange(DEC_SEQ)
    row = (p // GRID_W).astype(np.float32)
    col = (p % GRID_W).astype(np.float32)
    half = QK_DIM // 4
    freqs = (ROPE_BASE ** (-np.arange(half, dtype=np.float32) / half)).astype(np.float32)
    lane = np.arange(V_DIM)
    f = freqs[lane & (half - 1)]
    use_col = (lane & (2 * half)) != 0
    ang = (np.where(use_col[None, :], col[:, None], row[:, None]) * f[None, :]).astype(np.float32)
    upper = (lane & half) != 0
    sin = np.sin(ang)
    return jnp.asarray(np.cos(ang), F32), jnp.asarray(np.where(upper[None, :], sin, -sin), F32)


def kernel(x_prompt, x_sample, cache_k, cache_v, c, c_ctx, w_ada, b_ada, norm_mix_g, norm_ffn_g,
           w_in, lambda_qk, subln_g, conv_w, w_out, w_gu_dense, w_down_dense, w_router,
           w_gu_moe, w_down_moe, final_g):
    assert DEPTH == 2
    xs = (x_prompt.reshape(N_CTX, D_MODEL), x_sample.reshape(N_LAT, D_MODEL))
    cond = jnp.concatenate([c_ctx[None, :], c, jnp.zeros((COND_ROWS - 1 - DEC_BATCH, D_MODEL), F32)], axis=0)
    mod5 = _ada(cond, w_ada, b_ada).reshape(DEPTH, COND_ROWS, N_MOD, 1, D_MODEL)
    cos_t, sin_t = _rope_tables()
    cache_kt = jnp.transpose(cache_k, (0, 1, 3, 4, 5, 2)).reshape(DEC_BATCH, DEPTH, ATT_WIDTH, PAST_LEN)
    cache_v4 = cache_v.reshape(DEC_BATCH, DEPTH, PAST_LEN * N_HEADS, V_DIM)
    g_mix = norm_mix_g.reshape(DEPTH, 1, D_MODEL)
    g_ffn = norm_ffn_g.reshape(DEPTH, 1, D_MODEL)
    sg = subln_g.reshape(DEPTH, 1, V_DIM)

    new_kv = None
    for layer in range(DEPTH):
        lam_init = 0.8 - 0.6 * math.exp(-0.3 * layer)
        q, v, kt, conv, nk, nv = _in_proj(layer, xs, mod5, g_mix, w_in, cos_t, sin_t, conv_w, new_kv)
        new_kv = (nk, nv)
        att = _attn_ctx(layer, q, kt, v, lambda_qk, sg, lam_init)
        att = _attn_lat(layer, q, kt, v, cache_kt, cache_v4, lambda_qk, sg, att, lam_init)
        outs = _out_proj(layer, att, conv, w_out, xs, mod5, g_ffn, with_f32=layer == 1)
        if layer == 0:
            x1, h2 = outs
            act = _dense_gu(h2, w_gu_dense)
            xs = _dense_down(layer, act, w_down_dense, x1, mod5)
        else:
            x1, h2, h2f = outs
            wr = jnp.pad(w_router[0], ((0, 0), (0, LANES - N_EXPERTS)))
            ri, rg, cnt = _router(h2, wr)
            pos, te, nt = _group_layout(ri, cnt)
            xsort = _dispatch(pos, h2f)
            act = _moe_gu(te, nt, xsort, w_gu_moe)
            ys = _moe_down(te, nt, act, w_down_moe)
            y_ctx, y_lat = _combine(layer, pos, ys, rg, x1, mod5, final_g.reshape(1, D_MODEL))
    nk, nv = new_kv
    new_k = jnp.transpose(nk.reshape(BATCH, DEPTH, N_HEADS, 2, QK_DIM, SEQ), (0, 1, 5, 2, 3, 4))
    new_v = nv.reshape(BATCH, DEPTH, SEQ, N_HEADS, V_DIM)
    return (y_ctx.reshape(BATCH, SEQ, D_MODEL), y_lat.reshape(DEC_BATCH, DEC_SEQ, D_MODEL), new_k, new_v)
```

```python
import functools
import math

import numpy as np
import jax
import jax.numpy as jnp
from jax import lax
from jax.experimental import pallas as pl
from jax.experimental.pallas import tpu as pltpu

D_MODEL = 1024
BATCH = 16
SEQ = 256
DEPTH = 2
DEC_BATCH = 4
DEC_SEQ = 1024
PAST_LEN = 512
GRID_W = 64
ATT_WIDTH = 512
CONV_WIDTH = 512
N_HEADS = 4
V_DIM = 128
QK_DIM = 64
ROPE_BASE = 10000.0
D_FF = 2816
N_EXPERTS = 8
D_FF_EXPERT = 1408
N_MOD = 6
NORM_EPS = 1e-6
IN_COLS = 3 * ATT_WIDTH + 3 * CONV_WIDTH

N_CTX = BATCH * SEQ
N_LAT = DEC_BATCH * DEC_SEQ
N_TOK = N_CTX + N_LAT
TM = 1024
N_TILES = N_TOK // TM
CTX_TILES = N_CTX // TM
SEQ_PER_TILE = TM // SEQ
COND_ROWS = 8
TN_IN = 512
N_IN_TILES = IN_COLS // TN_IN
TN_FF = 256
N_FF_TILES = D_FF // TN_FF
TN_DOWN = 512
N_DOWN_TILES = D_MODEL // TN_DOWN
TN_ADA = 1536
TG = 512
ST = 256
SUB_PER_TILE = TM // ST
N_SUB = N_TOK // ST
CTX_SUB = N_CTX // ST
SEG_ALIGN = 16
SEG_PIECES = (256, 128, 64, 32, 16)
STAGE_ROWS = 640
NT_G = -(-(2 * N_TOK + N_SUB * N_EXPERTS * (SEG_ALIGN - 1) + N_EXPERTS * (TG - SEG_ALIGN)) // TG)
R_PAD = NT_G * TG
LANES = 128
VMEM_LIMIT = 60 * 1024 * 1024

BF = jnp.bfloat16
F32 = jnp.float32


def _params(sem, vmem=VMEM_LIMIT):
    return pltpu.CompilerParams(dimension_semantics=sem, vmem_limit_bytes=vmem)


def _mod_row(i):
    return jnp.where(i < CTX_TILES, 0, i - (CTX_TILES - 1))


def _mod_spec(layer, c, width=D_MODEL):
    return pl.BlockSpec((None, None, None, 1, width), lambda i, *_: (layer, _mod_row(i), c, 0, 0))


def _stream_specs(pair, width=D_MODEL):
    a = pl.BlockSpec((TM, width), lambda i, *_: (jnp.minimum(i, CTX_TILES - 1), 0))
    if pair:
        b = pl.BlockSpec((TM, width), lambda i, *_: (jnp.maximum(i - CTX_TILES, 0), 0))
    else:
        b = pl.BlockSpec((TM, width), lambda i, *_: (jnp.maximum(i, CTX_TILES), 0))
    return a, b


def _stream_rows(xa_ref, xb_ref, i):
    return jnp.where(i >= CTX_TILES, xb_ref[...], xa_ref[...])


def _silu(x):
    return x / (1.0 + jnp.exp(-x))


def _rms(x):
    return x * lax.rsqrt(jnp.mean(x * x, axis=-1, keepdims=True) + NORM_EPS)


def _ada_kernel(c_ref, w_ref, b_ref, o_ref):
    s = _silu(c_ref[...]).astype(BF)
    o_ref[...] = jnp.dot(s, w_ref[...].astype(BF), preferred_element_type=F32) + b_ref[...]


def _ada(cond, w_ada, b_ada):
    n = N_MOD * D_MODEL
    return pl.pallas_call(
        _ada_kernel,
        grid=(DEPTH, n // TN_ADA),
        in_specs=[
            pl.BlockSpec((COND_ROWS, D_MODEL), lambda l, j: (0, 0)),
            pl.BlockSpec((None, D_MODEL, TN_ADA), lambda l, j: (l, 0, j)),
            pl.BlockSpec((None, 1, TN_ADA), lambda l, j: (l, 0, j)),
        ],
        out_specs=pl.BlockSpec((None, COND_ROWS, TN_ADA), lambda l, j: (l, 0, j)),
        out_shape=jax.ShapeDtypeStruct((DEPTH, COND_ROWS, n), F32),
        compiler_params=_params(("arbitrary", "arbitrary")),
        name="ada_mod",
    )(cond, w_ada, b_ada.reshape(DEPTH, 1, n))


def _in_kernel(*refs, aliased):
    xa_ref, xb_ref, sh_ref, sc_ref, g_ref, w_ref, cos_ref, sin_ref, cw_ref = refs[:9]
    refs = refs[11:] if aliased else refs[9:]
    q_ref, v_ref, kt_ref, conv_ref, nk_ref, nv_ref, h_s, wb_s, gb_s, gc_s = refs
    i = pl.program_id(0)
    j = pl.program_id(1)
    lat = i >= CTX_TILES
    ctx = jnp.logical_not(lat)

    @pl.when(j == 0)
    def _():
        y = _rms(_stream_rows(xa_ref, xb_ref, i)) * g_ref[...]
        h_s[...] = (y * (1.0 + sc_ref[...]) + sh_ref[...]).astype(BF)

    @pl.when(i == 0)
    def _():
        wb_s[j] = w_ref[...].astype(BF)

    acc = jnp.dot(h_s[...], wb_s[j], preferred_element_type=F32)

    def roped(a):
        cos = jnp.concatenate([cos_ref[...]] * N_HEADS, axis=1)
        sin = jnp.concatenate([sin_ref[...]] * N_HEADS, axis=1)
        lane = lax.broadcasted_iota(jnp.int32, a.shape, 1)
        upper = (lane & (QK_DIM // 4)) != 0
        partner = jnp.where(upper, pltpu.roll(a, QK_DIM // 4, 1), pltpu.roll(a, TN_IN - QK_DIM // 4, 1))
        return a * cos + partner * sin

    @pl.when(jnp.logical_and(j == 0, lat))
    def _():
        q_ref[...] = roped(acc).astype(BF)

    @pl.when(jnp.logical_and(j == 0, ctx))
    def _():
        q_ref[...] = acc.astype(BF)

    @pl.when(jnp.logical_and(j == 1, lat))
    def _():
        kt_ref[...] = roped(acc).T.astype(BF)

    @pl.when(jnp.logical_and(j == 1, ctx))
    def _():
        kt = acc.T
        kt_ref[...] = kt.astype(BF)
        for s in range(SEQ_PER_TILE):
            nk_ref[s] = kt[:, s * SEQ:(s + 1) * SEQ]

    @pl.when(j == 2)
    def _():
        v_ref[...] = acc.astype(BF)

    @pl.when(jnp.logical_and(j == 2, ctx))
    def _():
        for s in range(SEQ_PER_TILE):
            for h in range(N_HEADS):
                nv_ref[s, pl.ds(h, SEQ, stride=N_HEADS), :] = acc[s * SEQ:(s + 1) * SEQ, h * V_DIM:(h + 1) * V_DIM]

    @pl.when(j == 3)
    def _():
        gb_s[...] = acc

    @pl.when(j == 4)
    def _():
        gc_s[...] = acc

    @pl.when(j == 5)
    def _():
        u = gc_s[...] * acc
        seq = jnp.where(lat, DEC_SEQ, SEQ)
        pos = lax.broadcasted_iota(jnp.int32, (TM, 1), 0) & (seq - 1)
        prev = jnp.where(pos == 0, 0.0, pltpu.roll(u, 1, 0))
        nxt = jnp.where(pos == seq - 1, 0.0, pltpu.roll(u, TM - 1, 0))
        cw = cw_ref[...]
        conv = prev * cw[0:1] + u * cw[1:2] + nxt * cw[2:3]
        conv_ref[...] = (gb_s[...] * conv).astype(BF)


def _in_proj(layer, xs, mod5, g_mix, w_in, cos_t, sin_t, conv_w, new_kv):
    pair = isinstance(xs, tuple)
    xa, xb = xs if pair else (xs, xs)
    spec_a, spec_b = _stream_specs(pair)
    ctx_i = lambda i: jnp.minimum(i, CTX_TILES - 1)
    in_specs = [
        spec_a, spec_b,
        _mod_spec(layer, 0), _mod_spec(layer, 1),
        pl.BlockSpec((None, 1, D_MODEL), lambda i, j: (layer, 0, 0)),
        pl.BlockSpec((None, D_MODEL, TN_IN), lambda i, j: (layer, 0, jnp.where(i == 0, j, N_IN_TILES - 1))),
        pl.BlockSpec((DEC_SEQ, V_DIM), lambda i, j: (0, 0)),
        pl.BlockSpec((DEC_SEQ, V_DIM), lambda i, j: (0, 0)),
        pl.BlockSpec((None, 3, CONV_WIDTH), lambda i, j: (layer, 0, 0)),
    ]
    args = [xa, xb, mod5, mod5, g_mix, w_in, cos_t, sin_t, conv_w]
    aliases = {}
    if new_kv is not None:
        in_specs += [pl.BlockSpec(memory_space=pl.ANY), pl.BlockSpec(memory_space=pl.ANY)]
        args += list(new_kv)
        aliases = {9: 4, 10: 5}
    row_tile = pl.BlockSpec((TM, TN_IN), lambda i, j: (i, 0))
    return pl.pallas_call(
        functools.partial(_in_kernel, aliased=new_kv is not None),
        grid=(N_TILES, N_IN_TILES),
        in_specs=in_specs,
        out_specs=[
            row_tile,
            row_tile,
            pl.BlockSpec((None, ATT_WIDTH, TM), lambda i, j: (i, 0, 0)),
            row_tile,
            pl.BlockSpec((SEQ_PER_TILE, None, ATT_WIDTH, SEQ), lambda i, j: (ctx_i(i), layer, 0, 0)),
            pl.BlockSpec((SEQ_PER_TILE, None, SEQ * N_HEADS, V_DIM), lambda i, j: (ctx_i(i), layer, 0, 0)),
        ],
        out_shape=[
            jax.ShapeDtypeStruct((N_TOK, ATT_WIDTH), BF),
            jax.ShapeDtypeStruct((N_TOK, ATT_WIDTH), BF),
            jax.ShapeDtypeStruct((N_TILES, ATT_WIDTH, TM), BF),
            jax.ShapeDtypeStruct((N_TOK, CONV_WIDTH), BF),
            jax.ShapeDtypeStruct((BATCH, DEPTH, ATT_WIDTH, SEQ), F32),
            jax.ShapeDtypeStruct((BATCH, DEPTH, SEQ * N_HEADS, V_DIM), F32),
        ],
        scratch_shapes=[
            pltpu.VMEM((TM, D_MODEL), BF),
            pltpu.VMEM((N_IN_TILES, D_MODEL, TN_IN), BF),
            pltpu.VMEM((TM, CONV_WIDTH), F32),
            pltpu.VMEM((TM, CONV_WIDTH), F32),
        ],
        input_output_aliases=aliases,
        compiler_params=_params(("arbitrary", "arbitrary")),
        name=f"in_proj_l{layer}",
    )(*args)


def _lambda(lq_ref, lam_init):
    lq = lq_ref[...]
    a = jnp.exp(jnp.sum(lq[0:1] * lq[1:2], axis=-1, keepdims=True))
    b = jnp.exp(jnp.sum(lq[2:3] * lq[3:4], axis=-1, keepdims=True))
    return a - b + lam_init


def _head_norm(o, sg, lam_init):
    return _rms(o) * sg * (1.0 - lam_init)


def _attn_ctx_kernel(q_ref, kt_ref, v_ref, lq_ref, sg_ref, o_ref, *, lam_init):
    lam = _lambda(lq_ref, lam_init)
    sg = sg_ref[...]
    for h in range(N_HEADS):
        cols = slice(h * V_DIM, (h + 1) * V_DIM)
        q = q_ref[:, cols] * (QK_DIM ** -0.5)
        probs = []
        for s in range(2):
            d = slice(s * QK_DIM, (s + 1) * QK_DIM)
            kt = kt_ref[h * V_DIM + s * QK_DIM:h * V_DIM + (s + 1) * QK_DIM, :]
            sc = jnp.dot(q[:, d], kt, preferred_element_type=F32)
            e = jnp.exp(sc - jnp.max(sc, axis=-1, keepdims=True))
            probs.append(e * (1.0 / jnp.sum(e, axis=-1, keepdims=True)))
        a = (probs[0] - lam * probs[1]).astype(BF)
        o = jnp.dot(a, v_ref[:, cols], preferred_element_type=F32)
        o_ref[:, cols] = _head_norm(o, sg, lam_init).astype(BF)


def _attn_ctx(layer, q, kt, v, lambda_qk, subln_g, lam_init):
    return pl.pallas_call(
        functools.partial(_attn_ctx_kernel, lam_init=lam_init),
        grid=(BATCH,),
        in_specs=[
            pl.BlockSpec((SEQ, ATT_WIDTH), lambda b: (b, 0)),
            pl.BlockSpec((None, ATT_WIDTH, SEQ), lambda b: (b // SEQ_PER_TILE, 0, b % SEQ_PER_TILE)),
            pl.BlockSpec((SEQ, ATT_WIDTH), lambda b: (b, 0)),
            pl.BlockSpec((None, 4, QK_DIM), lambda b: (layer, 0, 0)),
            pl.BlockSpec((None, 1, V_DIM), lambda b: (layer, 0, 0)),
        ],
        out_specs=pl.BlockSpec((SEQ, ATT_WIDTH), lambda b: (b, 0)),
        out_shape=jax.ShapeDtypeStruct((N_TOK, ATT_WIDTH), BF),
        compiler_params=_params(("arbitrary",)),
        name=f"attn_ctx_l{layer}",
    )(q, kt, v, lambda_qk, subln_g)


TQ = 256


def _attn_lat_kernel(q_ref, kt_ref, v_ref, ckt_ref, cv_ref, lq_ref, sg_ref, att_in_ref, o_ref, *, lam_init):
    del att_in_ref
    lam = _lambda(lq_ref, lam_init)
    sg = sg_ref[...]
    for h in range(N_HEADS):
        cols = slice(h * V_DIM, (h + 1) * V_DIM)
        q = q_ref[:, cols] * (QK_DIM ** -0.5)
        probs = []
        for s in range(2):
            d = slice(s * QK_DIM, (s + 1) * QK_DIM)
            rows = slice(h * V_DIM + s * QK_DIM, h * V_DIM + (s + 1) * QK_DIM)
            sc_c = jnp.dot(q[:, d], ckt_ref[rows, :].astype(BF), preferred_element_type=F32)
            sc_l = jnp.dot(q[:, d], kt_ref[rows, :], preferred_element_type=F32)
            m = jnp.maximum(jnp.max(sc_c, axis=-1, keepdims=True), jnp.max(sc_l, axis=-1, keepdims=True))
            e_c = jnp.exp(sc_c - m)
            e_l = jnp.exp(sc_l - m)
            r = 1.0 / (jnp.sum(e_c, axis=-1, keepdims=True) + jnp.sum(e_l, axis=-1, keepdims=True))
            probs.append((e_c * r, e_l * r))
        a_c = (probs[0][0] - lam * probs[1][0]).astype(BF)
        a_l = (probs[0][1] - lam * probs[1][1]).astype(BF)
        vc = cv_ref[pl.ds(h, PAST_LEN, stride=N_HEADS), :].astype(BF)
        o = jnp.dot(a_c, vc, preferred_element_type=F32)
        o = o + jnp.dot(a_l, v_ref[:, cols], preferred_element_type=F32)
        o_ref[:, cols] = _head_norm(o, sg, lam_init).astype(BF)


def _attn_lat(layer, q, kt, v, cache_kt, cache_v, lambda_qk, subln_g, att, lam_init):
    nqb = DEC_SEQ // TQ
    q0 = N_CTX // TQ
    return pl.pallas_call(
        functools.partial(_attn_lat_kernel, lam_init=lam_init),
        grid=(DEC_BATCH, nqb),
        in_specs=[
            pl.BlockSpec((TQ, ATT_WIDTH), lambda b, t: (q0 + b * nqb + t, 0)),
            pl.BlockSpec((None, ATT_WIDTH, DEC_SEQ), lambda b, t: (CTX_TILES + b, 0, 0)),
            pl.BlockSpec((DEC_SEQ, ATT_WIDTH), lambda b, t: (CTX_TILES + b, 0)),
            pl.BlockSpec((None, None, ATT_WIDTH, PAST_LEN), lambda b, t: (b, layer, 0, 0)),
            pl.BlockSpec((None, None, PAST_LEN * N_HEADS, V_DIM), lambda b, t: (b, layer, 0, 0)),
            pl.BlockSpec((None, 4, QK_DIM), lambda b, t: (layer, 0, 0)),
            pl.BlockSpec((None, 1, V_DIM), lambda b, t: (layer, 0, 0)),
            pl.BlockSpec(memory_space=pl.ANY),
        ],
        out_specs=pl.BlockSpec((TQ, ATT_WIDTH), lambda b, t: (q0 + b * nqb + t, 0)),
        out_shape=jax.ShapeDtypeStruct((N_TOK, ATT_WIDTH), BF),
        input_output_aliases={7: 0},
        compiler_params=_params(("arbitrary", "arbitrary")),
        name=f"attn_lat_l{layer}",
    )(q, kt, v, cache_kt, cache_v, lambda_qk, subln_g, att)


def _out_kernel(att_ref, conv_ref, w_ref, xa_ref, xb_ref, g1_ref, sh_ref, sc_ref, gf_ref,
                xo_ref, h2_ref, wb_s):
    i = pl.program_id(0)

    @pl.when(i == 0)
    def _():
        wb_s[...] = w_ref[...].astype(BF)

    mo = jnp.dot(att_ref[...], wb_s[:ATT_WIDTH, :], preferred_element_type=F32)
    mo = mo + jnp.dot(conv_ref[...], wb_s[ATT_WIDTH:, :], preferred_element_type=F32)
    xn = _stream_rows(xa_ref, xb_ref, i) + g1_ref[...] * mo
    xo_ref[...] = xn
    h2 = (_rms(xn) * gf_ref[...]) * (1.0 + sc_ref[...]) + sh_ref[...]
    h2_ref[...] = h2.astype(BF)


def _out_proj(layer, att, conv, w_out, xs, mod5, g_ffn):
    pair = isinstance(xs, tuple)
    xa, xb = xs if pair else (xs, xs)
    spec_a, spec_b = _stream_specs(pair)
    row_spec = pl.BlockSpec((TM, D_MODEL), lambda i: (i, 0))
    out_specs = [row_spec, row_spec]
    out_shape = [jax.ShapeDtypeStruct((N_TOK, D_MODEL), F32), jax.ShapeDtypeStruct((N_TOK, D_MODEL), BF)]
    return pl.pallas_call(
        _out_kernel,
        grid=(N_TILES,),
        in_specs=[
            pl.BlockSpec((TM, ATT_WIDTH), lambda i: (i, 0)),
            pl.BlockSpec((TM, CONV_WIDTH), lambda i: (i, 0)),
            pl.BlockSpec((None, D_MODEL, D_MODEL), lambda i: (layer, 0, 0)),
            spec_a, spec_b,
            _mod_spec(layer, 2), _mod_spec(layer, 3), _mod_spec(layer, 4),
            pl.BlockSpec((None, 1, D_MODEL), lambda i: (layer, 0, 0)),
        ],
        out_specs=out_specs,
        out_shape=out_shape,
        scratch_shapes=[pltpu.VMEM((D_MODEL, D_MODEL), BF)],
        compiler_params=_params(("arbitrary",)),
        name=f"out_proj_l{layer}",
    )(att, conv, w_out, xa, xb, mod5, mod5, mod5, g_ffn)


def _gu_kernel(h_ref, wg_ref, wu_ref, o_ref, wb_s):
    i = pl.program_id(0)
    j = pl.program_id(1)

    @pl.when(i == 0)
    def _():
        wb_s[j] = wg_ref[...].astype(BF)
        wb_s[j + N_FF_TILES] = wu_ref[...].astype(BF)

    h = h_ref[...]
    g = jnp.dot(h, wb_s[j], preferred_element_type=F32)
    u = jnp.dot(h, wb_s[j + N_FF_TILES], preferred_element_type=F32)
    o_ref[...] = (_silu(g) * u).astype(BF)


def _dense_gu(h2, w_gu):
    nj = N_FF_TILES
    return pl.pallas_call(
        _gu_kernel,
        grid=(N_TILES, nj),
        in_specs=[
            pl.BlockSpec((TM, D_MODEL), lambda i, j: (i, 0)),
            pl.BlockSpec((None, D_MODEL, TN_FF), lambda i, j: (0, 0, jnp.where(i == 0, j, nj - 1))),
            pl.BlockSpec((None, D_MODEL, TN_FF), lambda i, j: (0, 0, jnp.where(i == 0, j, nj - 1) + nj)),
        ],
        out_specs=pl.BlockSpec((TM, TN_FF), lambda i, j: (i, j)),
        out_shape=jax.ShapeDtypeStruct((N_TOK, D_FF), BF),
        scratch_shapes=[pltpu.VMEM((2 * nj, D_MODEL, TN_FF), BF)],
        compiler_params=_params(("arbitrary", "arbitrary")),
        name="dense_gu",
    )(h2, w_gu, w_gu)


def _down_kernel(a_ref, w_ref, x_ref, g2_ref, o_ref, wb_s):
    j = pl.program_id(1)

    @pl.when(pl.program_id(0) == 0)
    def _():
        wb_s[j] = w_ref[...].astype(BF)

    y = jnp.dot(a_ref[...], wb_s[j], preferred_element_type=F32)
    o_ref[...] = x_ref[...] + g2_ref[...] * y


def _dense_down(layer, act, w_down, x, mod5):
    nj = N_DOWN_TILES
    return pl.pallas_call(
        _down_kernel,
        grid=(N_TILES, nj),
        in_specs=[
            pl.BlockSpec((TM, D_FF), lambda i, j: (i, 0)),
            pl.BlockSpec((None, D_FF, TN_DOWN), lambda i, j: (0, 0, jnp.where(i == 0, j, nj - 1))),
            pl.BlockSpec((TM, TN_DOWN), lambda i, j: (i, j)),
            pl.BlockSpec((None, None, None, 1, TN_DOWN), lambda i, j: (layer, _mod_row(i), 5, 0, j)),
        ],
        out_specs=pl.BlockSpec((TM, TN_DOWN), lambda i, j: (i, j)),
        out_shape=jax.ShapeDtypeStruct((N_TOK, D_MODEL), F32),
        scratch_shapes=[pltpu.VMEM((nj, D_FF, TN_DOWN), BF)],
        compiler_params=_params(("arbitrary", "arbitrary")),
        name="dense_down",
    )(act, w_down, x, mod5)


def _router_kernel(h_ref, wr_ref, lp_ref, rg_ref, n_ref):
    logits = jnp.dot(h_ref[...], wr_ref[...].astype(BF), preferred_element_type=F32)
    lane = lax.broadcasted_iota(jnp.int32, logits.shape, 1)
    lg = jnp.where(lane < N_EXPERTS, logits, -jnp.inf)
    m1 = jnp.max(lg, axis=-1, keepdims=True)
    i1 = jnp.min(jnp.where(lg == m1, lane, LANES), axis=-1, keepdims=True)
    lg2 = jnp.where(lane == i1, -jnp.inf, lg)
    m2 = jnp.max(lg2, axis=-1, keepdims=True)
    i2 = jnp.min(jnp.where(lg2 == m2, lane, LANES), axis=-1, keepdims=True)
    e2 = jnp.exp(m2 - m1)
    w1 = 1.0 / (1.0 + e2)
    w2 = e2 / (1.0 + e2)

    sel1 = lane == i1
    sel2 = lane == i2
    onehot = jnp.logical_or(sel1, sel2)
    rows = lax.broadcasted_iota(jnp.int32, (TM, TM), 0)
    colsi = lax.broadcasted_iota(jnp.int32, (TM, TM), 1)
    earlier = jnp.logical_and(colsi < rows, (colsi // ST) == (rows // ST))
    before = jnp.dot(earlier.astype(BF), onehot.astype(BF), preferred_element_type=F32)
    onehot_f = onehot.astype(F32)
    counts = [jnp.sum(onehot_f[s * ST:(s + 1) * ST], axis=0, keepdims=True) for s in range(SUB_PER_TILE)]
    counts = jnp.concatenate(counts + [jnp.zeros((8 - SUB_PER_TILE, LANES), F32)], axis=0).astype(jnp.int32)
    seg_len = ((counts + (SEG_ALIGN - 1)) // SEG_ALIGN) * SEG_ALIGN
    n_ref[...] = seg_len
    la = lax.broadcasted_iota(jnp.int32, (LANES, LANES), 0)
    lb = lax.broadcasted_iota(jnp.int32, (LANES, LANES), 1)
    seg_start = jnp.dot(seg_len.astype(F32).astype(BF), (la < lb).astype(BF), preferred_element_type=F32)
    start = jnp.concatenate(
        [jnp.broadcast_to(seg_start[s:s + 1], (ST, LANES)) for s in range(SUB_PER_TILE)], axis=0)
    where = before + start
    lp1 = jnp.sum(jnp.where(sel1, where, 0.0), axis=-1, keepdims=True).astype(jnp.int32)
    lp2 = jnp.sum(jnp.where(sel2, where, 0.0), axis=-1, keepdims=True).astype(jnp.int32)
    lp_ref[...] = jnp.where(lane == 0, lp1, jnp.where(lane == 1, lp2, 0))
    rg_ref[...] = jnp.where(lane == 0, w1, jnp.where(lane == 1, w2, 0.0))


def _router(h2, w_router_pad):
    return pl.pallas_call(
        _router_kernel,
        grid=(N_TILES,),
        in_specs=[
            pl.BlockSpec((TM, D_MODEL), lambda i: (i, 0)),
            pl.BlockSpec((D_MODEL, LANES), lambda i: (0, 0)),
        ],
        out_specs=[
            pl.BlockSpec((TM, LANES), lambda i: (i, 0)),
            pl.BlockSpec((TM, LANES), lambda i: (i, 0)),
            pl.BlockSpec((None, 8, LANES), lambda i: (i, 0, 0)),
        ],
        out_shape=[
            jax.ShapeDtypeStruct((N_TOK, LANES), jnp.int32),
            jax.ShapeDtypeStruct((N_TOK, LANES), F32),
            jax.ShapeDtypeStruct((N_TILES, 8, LANES), jnp.int32),
        ],
        compiler_params=_params(("arbitrary",)),
        name="router",
    )(h2, w_router_pad)


def _segment_copies(s, seg_ref, n_ref, dst_ref, stage, rows_hbm, sem, *, to_hbm, wait):
    for e in range(N_EXPERTS):
        k = s * N_EXPERTS + e
        n = n_ref[k]
        v0 = seg_ref[k]
        h0 = dst_ref[k]
        for z in SEG_PIECES:
            off = n & (-2 * z)

            @pl.when((n & z) != 0)
            def _():
                v = stage.at[pl.ds(pl.multiple_of(v0 + off, SEG_ALIGN), z)]
                h = rows_hbm.at[pl.ds(pl.multiple_of(h0 + off, SEG_ALIGN), z)]
                cp = pltpu.make_async_copy(v, h, sem) if to_hbm else pltpu.make_async_copy(h, v, sem)
                if wait:
                    cp.wait()
                else:
                    cp.start()


def _dispatch_kernel(seg_ref, n_ref, dst_ref, h_ref, lp_ref, xs_ref, stage_s, sem):
    s = pl.program_id(0)
    slot = s % 2
    copies = functools.partial(_segment_copies, seg_ref=seg_ref, n_ref=n_ref, dst_ref=dst_ref,
                               rows_hbm=xs_ref, to_hbm=True)

    @pl.when(s >= 2)
    def _():
        copies(s - 2, stage=stage_s.at[slot], sem=sem.at[slot], wait=True)

    lpt = lp_ref[...].T
    r = lax.broadcasted_iota(jnp.int32, (STAGE_ROWS, ST), 0)
    pick = jnp.logical_or(r == lpt[0:1, :], r == lpt[1:2, :]).astype(BF)
    stage_s[slot] = jnp.dot(pick, h_ref[...], preferred_element_type=F32).astype(BF)
    copies(s, stage=stage_s.at[slot], sem=sem.at[slot], wait=False)

    @pl.when(s == N_SUB - 1)
    def _():
        copies(s - 1, stage=stage_s.at[1 - slot], sem=sem.at[1 - slot], wait=True)
        copies(s, stage=stage_s.at[slot], sem=sem.at[slot], wait=True)


def _dispatch(seg, n, dst, h2, lp):
    grid_spec = pltpu.PrefetchScalarGridSpec(
        num_scalar_prefetch=3,
        grid=(N_SUB,),
        in_specs=[
            pl.BlockSpec((ST, D_MODEL), lambda s, *_: (s, 0)),
            pl.BlockSpec((ST, LANES), lambda s, *_: (s, 0)),
        ],
        out_specs=pl.BlockSpec(memory_space=pl.ANY),
        scratch_shapes=[pltpu.VMEM((2, STAGE_ROWS, D_MODEL), BF), pltpu.SemaphoreType.DMA((2,))],
    )
    return pl.pallas_call(
        _dispatch_kernel,
        grid_spec=grid_spec,
        out_shape=jax.ShapeDtypeStruct((R_PAD, D_MODEL), BF),
        compiler_params=_params(("arbitrary",)),
        name="moe_dispatch",
    )(seg, n, dst, h2, lp)


def _new_expert(te_ref, r):
    return jnp.logical_or(r == 0, te_ref[r] != te_ref[jnp.maximum(r - 1, 0)])


def _moe_gu_kernel(te_ref, nt_ref, x_ref, w_ref, o_ref, wb_s):
    r = pl.program_id(0)

    @pl.when(r < nt_ref[0])
    def _():
        @pl.when(_new_expert(te_ref, r))
        def _():
            wb_s[...] = w_ref[...].astype(BF)

        x = x_ref[...]
        g = jnp.dot(x, wb_s[:, :D_FF_EXPERT], preferred_element_type=F32)
        u = jnp.dot(x, wb_s[:, D_FF_EXPERT:], preferred_element_type=F32)
        o_ref[...] = (_silu(g) * u).astype(BF)


def _moe_gu(te, nt, xs, w_gu):
    grid_spec = pltpu.PrefetchScalarGridSpec(
        num_scalar_prefetch=2,
        grid=(NT_G,),
        in_specs=[
            pl.BlockSpec((TG, D_MODEL), lambda r, te, nt: (jnp.minimum(r, nt[0] - 1), 0)),
            pl.BlockSpec((None, None, D_MODEL, 2 * D_FF_EXPERT), lambda r, te, nt: (0, te[r], 0, 0)),
        ],
        out_specs=pl.BlockSpec((TG, D_FF_EXPERT), lambda r, te, nt: (jnp.minimum(r, nt[0] - 1), 0)),
        scratch_shapes=[pltpu.VMEM((D_MODEL, 2 * D_FF_EXPERT), BF)],
    )
    return pl.pallas_call(
        _moe_gu_kernel,
        grid_spec=grid_spec,
        out_shape=jax.ShapeDtypeStruct((R_PAD, D_FF_EXPERT), BF),
        compiler_params=_params(("arbitrary",)),
        name="moe_gu",
    )(te, nt, xs, w_gu)


def _moe_down_kernel(te_ref, nt_ref, a_ref, w_ref, o_ref, wb_s):
    r = pl.program_id(0)

    @pl.when(r < nt_ref[0])
    def _():
        @pl.when(_new_expert(te_ref, r))
        def _():
            wb_s[...] = w_ref[...].astype(BF)

        o_ref[...] = jnp.dot(a_ref[...], wb_s[...], preferred_element_type=F32).astype(BF)


def _moe_down(te, nt, act, w_down):
    grid_spec = pltpu.PrefetchScalarGridSpec(
        num_scalar_prefetch=2,
        grid=(NT_G,),
        in_specs=[
            pl.BlockSpec((TG, D_FF_EXPERT), lambda r, te, nt: (jnp.minimum(r, nt[0] - 1), 0)),
            pl.BlockSpec((None, None, D_FF_EXPERT, D_MODEL), lambda r, te, nt: (0, te[r], 0, 0)),
        ],
        out_specs=pl.BlockSpec((TG, D_MODEL), lambda r, te, nt: (jnp.minimum(r, nt[0] - 1), 0)),
        scratch_shapes=[pltpu.VMEM((D_FF_EXPERT, D_MODEL), BF)],
    )
    return pl.pallas_call(
        _moe_down_kernel,
        grid_spec=grid_spec,
        out_shape=jax.ShapeDtypeStruct((R_PAD, D_MODEL), BF),
        compiler_params=_params(("arbitrary",)),
        name="moe_down",
    )(te, nt, act, w_down)


def _combine_kernel(seg_ref, n_ref, dst_ref, ys_ref, lp_ref, rg_ref, x_ref, g2_ref, fg_ref,
                    oa_ref, ob_ref, stage_s, sem):
    s = pl.program_id(0)
    slot = s % 2
    copies = functools.partial(_segment_copies, seg_ref=seg_ref, n_ref=n_ref, dst_ref=dst_ref,
                               rows_hbm=ys_ref, to_hbm=False)

    @pl.when(s == 0)
    def _():
        stage_s[...] = jnp.zeros_like(stage_s)
        copies(s, stage=stage_s.at[slot], sem=sem.at[slot], wait=False)

    @pl.when(s + 1 < N_SUB)
    def _():
        copies(s + 1, stage=stage_s.at[1 - slot], sem=sem.at[1 - slot], wait=False)

    copies(s, stage=stage_s.at[slot], sem=sem.at[slot], wait=True)

    lp = lp_ref[...]
    r = lax.broadcasted_iota(jnp.int32, (ST, STAGE_ROWS), 1)
    rows = stage_s[slot]
    a = jnp.dot((r == lp[:, 0:1]).astype(BF), rows, preferred_element_type=F32)
    b = jnp.dot((r == lp[:, 1:2]).astype(BF), rows, preferred_element_type=F32)
    rg = rg_ref[...]
    y = rg[:, 0:1] * a + rg[:, 1:2] * b
    xn = x_ref[...] + g2_ref[...] * y
    out = _rms(xn) * fg_ref[...]

    @pl.when(s < CTX_SUB)
    def _():
        oa_ref[...] = out

    @pl.when(s >= CTX_SUB)
    def _():
        ob_ref[...] = out


def _combine(layer, seg, n, dst, ys, lp, rg, x, mod5, final_g):
    def mod_map(s, *_):
        return (layer, _mod_row(s // SUB_PER_TILE), 5, 0, 0)

    grid_spec = pltpu.PrefetchScalarGridSpec(
        num_scalar_prefetch=3,
        grid=(N_SUB,),
        in_specs=[
            pl.BlockSpec(memory_space=pl.ANY),
            pl.BlockSpec((ST, LANES), lambda s, *_: (s, 0)),
            pl.BlockSpec((ST, LANES), lambda s, *_: (s, 0)),
            pl.BlockSpec((ST, D_MODEL), lambda s, *_: (s, 0)),
            pl.BlockSpec((None, None, None, 1, D_MODEL), mod_map),
            pl.BlockSpec((1, D_MODEL), lambda s, *_: (0, 0)),
        ],
        out_specs=[
            pl.BlockSpec((ST, D_MODEL), lambda s, *_: (jnp.minimum(s, CTX_SUB - 1), 0)),
            pl.BlockSpec((ST, D_MODEL), lambda s, *_: (jnp.maximum(s - CTX_SUB, 0), 0)),
        ],
        scratch_shapes=[pltpu.VMEM((2, STAGE_ROWS, D_MODEL), BF), pltpu.SemaphoreType.DMA((2,))],
    )
    return pl.pallas_call(
        _combine_kernel,
        grid_spec=grid_spec,
        out_shape=[
            jax.ShapeDtypeStruct((N_CTX, D_MODEL), F32),
            jax.ShapeDtypeStruct((N_LAT, D_MODEL), F32),
        ],
        compiler_params=_params(("arbitrary",)),
        name="moe_combine",
    )(seg, n, dst, ys, lp, rg, x, mod5, final_g)


def _group_layout(n_tiles):
    n = n_tiles[:, :SUB_PER_TILE, :N_EXPERTS].reshape(N_SUB, N_EXPERTS)
    tiles = (jnp.sum(n, axis=0) + TG - 1) // TG
    tile_end = jnp.cumsum(tiles)
    region = (tile_end - tiles) * TG
    dst = region[None, :] + jnp.cumsum(n, axis=0) - n
    seg = jnp.cumsum(n, axis=1) - n
    nt = tile_end[-1]
    tile_id = jnp.minimum(jnp.arange(NT_G, dtype=jnp.int32), nt - 1)
    te = jnp.sum((tile_id[:, None] >= tile_end[None, :]).astype(jnp.int32), axis=-1)
    flat = lambda a: a.reshape(N_SUB * N_EXPERTS).astype(jnp.int32)
    return flat(seg), flat(n), flat(dst), te.astype(jnp.int32), nt.reshape(1).astype(jnp.int32)


def _rope_tables():
    p = np.arange(DEC_SEQ)
    row = (p // GRID_W).astype(np.float32)
    col = (p % GRID_W).astype(np.float32)
    half = QK_DIM // 4
    freqs = (ROPE_BASE ** (-np.arange(half, dtype=np.float32) / half)).astype(np.float32)
    lane = np.arange(V_DIM)
    f = freqs[lane & (half - 1)]
    use_col = (lane & (2 * half)) != 0
    ang = (np.where(use_col[None, :], col[:, None], row[:, None]) * f[None, :]).astype(np.float32)
    upper = (lane & half) != 0
    sin = np.sin(ang)
    return jnp.asarray(np.cos(ang), F32), jnp.asarray(np.where(upper[None, :], sin, -sin), F32)


def kernel(x_prompt, x_sample, cache_k, cache_v, c, c_ctx, w_ada, b_ada, norm_mix_g, norm_ffn_g,
           w_in, lambda_qk, subln_g, conv_w, w_out, w_gu_dense, w_down_dense, w_router,
           w_gu_moe, w_down_moe, final_g):
    assert DEPTH == 2
    xs = (x_prompt.reshape(N_CTX, D_MODEL), x_sample.reshape(N_LAT, D_MODEL))
    cond = jnp.concatenate([c_ctx[None, :], c, jnp.zeros((COND_ROWS - 1 - DEC_BATCH, D_MODEL), F32)], axis=0)
    mod5 = _ada(cond, w_ada, b_ada).reshape(DEPTH, COND_ROWS, N_MOD, 1, D_MODEL)
    cos_t, sin_t = _rope_tables()
    cache_kt = jnp.transpose(cache_k, (0, 1, 3, 4, 5, 2)).reshape(DEC_BATCH, DEPTH, ATT_WIDTH, PAST_LEN)
    cache_v4 = cache_v.reshape(DEC_BATCH, DEPTH, PAST_LEN * N_HEADS, V_DIM)
    g_mix = norm_mix_g.reshape(DEPTH, 1, D_MODEL)
    g_ffn = norm_ffn_g.reshape(DEPTH, 1, D_MODEL)
    sg = subln_g.reshape(DEPTH, 1, V_DIM)

    new_kv = None
    for layer in range(DEPTH):
        lam_init = 0.8 - 0.6 * math.exp(-0.3 * layer)
        q, v, kt, conv, nk, nv = _in_proj(layer, xs, mod5, g_mix, w_in, cos_t, sin_t, conv_w, new_kv)
        new_kv = (nk, nv)
        att = _attn_ctx(layer, q, kt, v, lambda_qk, sg, lam_init)
        att = _attn_lat(layer, q, kt, v, cache_kt, cache_v4, lambda_qk, sg, att, lam_init)
        x1, h2 = _out_proj(layer, att, conv, w_out, xs, mod5, g_ffn)
        if layer == 0:
            act = _dense_gu(h2, w_gu_dense)
            xs = _dense_down(layer, act, w_down_dense, x1, mod5)
        else:
            wr = jnp.pad(w_router[0], ((0, 0), (0, LANES - N_EXPERTS)))
            lp, rg, n_tiles = _router(h2, wr)
            seg, n, dst, te, nt = _group_layout(n_tiles)
            xsort = _dispatch(seg, n, dst, h2, lp)
            act = _moe_gu(te, nt, xsort, w_gu_moe)
            ys = _moe_down(te, nt, act, w_down_moe)
            y_ctx, y_lat = _combine(layer, seg, n, dst, ys, lp, rg, x1, mod5, final_g.reshape(1, D_MODEL))
    nk, nv = new_kv
    new_k = jnp.transpose(nk.reshape(BATCH, DEPTH, N_HEADS, 2, QK_DIM, SEQ), (0, 1, 5, 2, 3, 4))
    new_v = nv.reshape(BATCH, DEPTH, SEQ, N_HEADS, V_DIM)
    return (y_ctx.reshape(BATCH, SEQ, D_MODEL), y_lat.reshape(DEC_BATCH, DEC_SEQ, D_MODEL), new_k, new_v)
```

```python
import functools
import math

import numpy as np
import jax
import jax.numpy as jnp
from jax import lax
from jax.experimental import pallas as pl
from jax.experimental.pallas import tpu as pltpu

D_MODEL = 1024
BATCH = 16
SEQ = 256
DEPTH = 2
DEC_BATCH = 4
DEC_SEQ = 1024
PAST_LEN = 512
GRID_W = 64
ATT_WIDTH = 512
CONV_WIDTH = 512
N_HEADS = 4
V_DIM = 128
QK_DIM = 64
ROPE_BASE = 10000.0
D_FF = 2816
N_EXPERTS = 8
D_FF_EXPERT = 1408
N_MOD = 6
NORM_EPS = 1e-6
IN_COLS = 3 * ATT_WIDTH + 3 * CONV_WIDTH

N_CTX = BATCH * SEQ
N_LAT = DEC_BATCH * DEC_SEQ
N_TOK = N_CTX + N_LAT
TM = 1024
N_TILES = N_TOK // TM
CTX_TILES = N_CTX // TM
SEQ_PER_TILE = TM // SEQ
COND_ROWS = 8
TN_IN = 512
N_IN_TILES = IN_COLS // TN_IN
ROW_CHUNK = 512
TN_FF = 1408
N_FF_TILES = D_FF // TN_FF
TN_DOWN = 512
N_DOWN_TILES = D_MODEL // TN_DOWN
TN_ADA = 1536
TG = 512
ST = 256
SUB_PER_TILE = TM // ST
N_SUB = N_TOK // ST
CTX_SUB = N_CTX // ST
SEG_ALIGN = 16
SEG_PIECES = (256, 128, 64, 32, 16)
STAGE_ROWS = 640
NT_G = -(-(2 * N_TOK + N_SUB * N_EXPERTS * (SEG_ALIGN - 1) + N_EXPERTS * (TG - SEG_ALIGN)) // TG)
R_PAD = NT_G * TG
LANES = 128
VMEM_LIMIT = 60 * 1024 * 1024

BF = jnp.bfloat16
F32 = jnp.float32


def _params(sem, vmem=VMEM_LIMIT):
    return pltpu.CompilerParams(dimension_semantics=sem, vmem_limit_bytes=vmem)


def _mod_row(i):
    return jnp.where(i < CTX_TILES, 0, i - (CTX_TILES - 1))


def _mod_spec(layer, c, width=D_MODEL):
    return pl.BlockSpec((None, None, None, 1, width), lambda i, *_: (layer, _mod_row(i), c, 0, 0))


def _stream_specs(pair, width=D_MODEL):
    a = pl.BlockSpec((TM, width), lambda i, *_: (jnp.minimum(i, CTX_TILES - 1), 0))
    if pair:
        b = pl.BlockSpec((TM, width), lambda i, *_: (jnp.maximum(i - CTX_TILES, 0), 0))
    else:
        b = pl.BlockSpec((TM, width), lambda i, *_: (jnp.maximum(i, CTX_TILES), 0))
    return a, b


def _stream_rows(xa_ref, xb_ref, i):
    return jnp.where(i >= CTX_TILES, xb_ref[...], xa_ref[...])


def _silu(x):
    return x / (1.0 + jnp.exp(-x))


def _rms(x):
    return x * lax.rsqrt(jnp.mean(x * x, axis=-1, keepdims=True) + NORM_EPS)


def _ada_kernel(c_ref, w_ref, b_ref, o_ref):
    s = _silu(c_ref[...]).astype(BF)
    o_ref[...] = jnp.dot(s, w_ref[...].astype(BF), preferred_element_type=F32) + b_ref[...]


def _ada(cond, w_ada, b_ada):
    n = N_MOD * D_MODEL
    return pl.pallas_call(
        _ada_kernel,
        grid=(DEPTH, n // TN_ADA),
        in_specs=[
            pl.BlockSpec((COND_ROWS, D_MODEL), lambda l, j: (0, 0)),
            pl.BlockSpec((None, D_MODEL, TN_ADA), lambda l, j: (l, 0, j)),
            pl.BlockSpec((None, 1, TN_ADA), lambda l, j: (l, 0, j)),
        ],
        out_specs=pl.BlockSpec((None, COND_ROWS, TN_ADA), lambda l, j: (l, 0, j)),
        out_shape=jax.ShapeDtypeStruct((DEPTH, COND_ROWS, n), F32),
        compiler_params=_params(("arbitrary", "arbitrary")),
        name="ada_mod",
    )(cond, w_ada, b_ada.reshape(DEPTH, 1, n))


def _in_kernel(*refs, aliased):
    xa_ref, xb_ref, sh_ref, sc_ref, g_ref, w_ref, cos_ref, sin_ref, cw_ref = refs[:9]
    refs = refs[11:] if aliased else refs[9:]
    q_ref, v_ref, kt_ref, conv_ref, nk_ref, nv_ref, h_s, wb_s, gb_s, gc_s = refs
    i = pl.program_id(0)
    j = pl.program_id(1)
    lat = i >= CTX_TILES
    ctx = jnp.logical_not(lat)

    @pl.when(i == 0)
    def _():
        wb_s[j] = w_ref[...].astype(BF)

    chunks = [slice(c * ROW_CHUNK, (c + 1) * ROW_CHUNK) for c in range(TM // ROW_CHUNK)]
    seqs_per_chunk = ROW_CHUNK // SEQ

    def norm(rows):
        gain = g_ref[...] * (1.0 + sc_ref[...])
        x = jnp.where(lat, xb_ref[rows, :], xa_ref[rows, :])
        h_s[rows, :] = (_rms(x) * gain + sh_ref[...]).astype(BF)

    def proj(rows):
        return jnp.dot(h_s[rows, :], wb_s[j], preferred_element_type=F32)

    def roped(a, rows):
        cos = jnp.concatenate([cos_ref[rows, :]] * N_HEADS, axis=1)
        sin = jnp.concatenate([sin_ref[rows, :]] * N_HEADS, axis=1)
        lane = lax.broadcasted_iota(jnp.int32, a.shape, 1)
        upper = (lane & (QK_DIM // 4)) != 0
        partner = jnp.where(upper, pltpu.roll(a, QK_DIM // 4, 1), pltpu.roll(a, TN_IN - QK_DIM // 4, 1))
        return a * cos + partner * sin

    @pl.when(jnp.logical_and(j == 0, lat))
    def _():
        for rows in chunks:
            norm(rows)
            q_ref[rows, :] = roped(proj(rows), rows).astype(BF)

    @pl.when(jnp.logical_and(j == 0, ctx))
    def _():
        for rows in chunks:
            norm(rows)
            q_ref[rows, :] = proj(rows).astype(BF)

    @pl.when(jnp.logical_and(j == 1, lat))
    def _():
        for rows in chunks:
            kt_ref[:, rows] = roped(proj(rows), rows).T.astype(BF)

    @pl.when(jnp.logical_and(j == 1, ctx))
    def _():
        for c, rows in enumerate(chunks):
            kt = proj(rows).T
            kt_ref[:, rows] = kt.astype(BF)
            for s in range(seqs_per_chunk):
                nk_ref[c * seqs_per_chunk + s] = kt[:, s * SEQ:(s + 1) * SEQ]

    @pl.when(jnp.logical_and(j == 2, lat))
    def _():
        for rows in chunks:
            v_ref[rows, :] = proj(rows).astype(BF)

    @pl.when(jnp.logical_and(j == 2, ctx))
    def _():
        for c, rows in enumerate(chunks):
            acc = proj(rows)
            v_ref[rows, :] = acc.astype(BF)
            for s in range(seqs_per_chunk):
                for h in range(N_HEADS):
                    nv_ref[c * seqs_per_chunk + s, pl.ds(h, SEQ, stride=N_HEADS), :] = (
                        acc[s * SEQ:(s + 1) * SEQ, h * V_DIM:(h + 1) * V_DIM])

    @pl.when(j == 3)
    def _():
        for rows in chunks:
            gb_s[rows, :] = proj(rows)

    @pl.when(j == 4)
    def _():
        for rows in chunks:
            gc_s[rows, :] = proj(rows)

    @pl.when(j == 5)
    def _():
        for rows in chunks:
            gc_s[rows, :] = gc_s[rows, :] * proj(rows)
        u = gc_s[...]
        seq = jnp.where(lat, DEC_SEQ, SEQ)
        pos = lax.broadcasted_iota(jnp.int32, (TM, 1), 0) & (seq - 1)
        prev = jnp.where(pos == 0, 0.0, pltpu.roll(u, 1, 0))
        nxt = jnp.where(pos == seq - 1, 0.0, pltpu.roll(u, TM - 1, 0))
        cw = cw_ref[...]
        conv = prev * cw[0:1] + u * cw[1:2] + nxt * cw[2:3]
        conv_ref[...] = (gb_s[...] * conv).astype(BF)


def _in_proj(layer, xs, mod5, g_mix, w_in, cos_t, sin_t, conv_w, new_kv):
    pair = isinstance(xs, tuple)
    xa, xb = xs if pair else (xs, xs)
    spec_a, spec_b = _stream_specs(pair)
    ctx_i = lambda i: jnp.minimum(i, CTX_TILES - 1)
    in_specs = [
        spec_a, spec_b,
        _mod_spec(layer, 0), _mod_spec(layer, 1),
        pl.BlockSpec((None, 1, D_MODEL), lambda i, j: (layer, 0, 0)),
        pl.BlockSpec((None, D_MODEL, TN_IN), lambda i, j: (layer, 0, jnp.where(i == 0, j, N_IN_TILES - 1))),
        pl.BlockSpec((DEC_SEQ, V_DIM), lambda i, j: (0, 0)),
        pl.BlockSpec((DEC_SEQ, V_DIM), lambda i, j: (0, 0)),
        pl.BlockSpec((None, 3, CONV_WIDTH), lambda i, j: (layer, 0, 0)),
    ]
    args = [xa, xb, mod5, mod5, g_mix, w_in, cos_t, sin_t, conv_w]
    aliases = {}
    if new_kv is not None:
        in_specs += [pl.BlockSpec(memory_space=pl.ANY), pl.BlockSpec(memory_space=pl.ANY)]
        args += list(new_kv)
        aliases = {9: 4, 10: 5}
    row_tile = pl.BlockSpec((TM, TN_IN), lambda i, j: (i, 0))
    return pl.pallas_call(
        functools.partial(_in_kernel, aliased=new_kv is not None),
        grid=(N_TILES, N_IN_TILES),
        in_specs=in_specs,
        out_specs=[
            row_tile,
            row_tile,
            pl.BlockSpec((None, ATT_WIDTH, TM), lambda i, j: (i, 0, 0)),
            row_tile,
            pl.BlockSpec((SEQ_PER_TILE, None, ATT_WIDTH, SEQ), lambda i, j: (ctx_i(i), layer, 0, 0)),
            pl.BlockSpec((SEQ_PER_TILE, None, SEQ * N_HEADS, V_DIM), lambda i, j: (ctx_i(i), layer, 0, 0)),
        ],
        out_shape=[
            jax.ShapeDtypeStruct((N_TOK, ATT_WIDTH), BF),
            jax.ShapeDtypeStruct((N_TOK, ATT_WIDTH), BF),
            jax.ShapeDtypeStruct((N_TILES, ATT_WIDTH, TM), BF),
            jax.ShapeDtypeStruct((N_TOK, CONV_WIDTH), BF),
            jax.ShapeDtypeStruct((BATCH, DEPTH, ATT_WIDTH, SEQ), F32),
            jax.ShapeDtypeStruct((BATCH, DEPTH, SEQ * N_HEADS, V_DIM), F32),
        ],
        scratch_shapes=[
            pltpu.VMEM((TM, D_MODEL), BF),
            pltpu.VMEM((N_IN_TILES, D_MODEL, TN_IN), BF),
            pltpu.VMEM((TM, CONV_WIDTH), F32),
            pltpu.VMEM((TM, CONV_WIDTH), F32),
        ],
        input_output_aliases=aliases,
        compiler_params=_params(("arbitrary", "arbitrary")),
        name=f"in_proj_l{layer}",
    )(*args)


def _lambda(lq_ref, lam_init):
    lq = lq_ref[...]
    a = jnp.exp(jnp.sum(lq[0:1] * lq[1:2], axis=-1, keepdims=True))
    b = jnp.exp(jnp.sum(lq[2:3] * lq[3:4], axis=-1, keepdims=True))
    return a - b + lam_init


def _head_norm(o, sg, lam_init):
    return _rms(o) * sg * (1.0 - lam_init)


def _attn_ctx_kernel(q_ref, kt_ref, v_ref, lq_ref, sg_ref, o_ref, *, lam_init):
    lam = _lambda(lq_ref, lam_init)
    sg = sg_ref[...]
    for h in range(N_HEADS):
        cols = slice(h * V_DIM, (h + 1) * V_DIM)
        q = q_ref[:, cols] * (QK_DIM ** -0.5)
        v = v_ref[:, cols]
        outs = []
        for s in range(2):
            d = slice(s * QK_DIM, (s + 1) * QK_DIM)
            kt = kt_ref[h * V_DIM + s * QK_DIM:h * V_DIM + (s + 1) * QK_DIM, :]
            sc = jnp.dot(q[:, d], kt, preferred_element_type=F32)
            e = jnp.exp(sc - jnp.max(sc, axis=-1, keepdims=True))
            r = 1.0 / jnp.sum(e, axis=-1, keepdims=True)
            outs.append(jnp.dot(e.astype(BF), v, preferred_element_type=F32) * r)
        o = outs[0] - lam * outs[1]
        o_ref[:, cols] = _head_norm(o, sg, lam_init).astype(BF)


def _attn_ctx(layer, q, kt, v, lambda_qk, subln_g, lam_init):
    return pl.pallas_call(
        functools.partial(_attn_ctx_kernel, lam_init=lam_init),
        grid=(BATCH,),
        in_specs=[
            pl.BlockSpec((SEQ, ATT_WIDTH), lambda b: (b, 0)),
            pl.BlockSpec((None, ATT_WIDTH, SEQ), lambda b: (b // SEQ_PER_TILE, 0, b % SEQ_PER_TILE)),
            pl.BlockSpec((SEQ, ATT_WIDTH), lambda b: (b, 0)),
            pl.BlockSpec((None, 4, QK_DIM), lambda b: (layer, 0, 0)),
            pl.BlockSpec((None, 1, V_DIM), lambda b: (layer, 0, 0)),
        ],
        out_specs=pl.BlockSpec((SEQ, ATT_WIDTH), lambda b: (b, 0)),
        out_shape=jax.ShapeDtypeStruct((N_TOK, ATT_WIDTH), BF),
        compiler_params=_params(("arbitrary",)),
        name=f"attn_ctx_l{layer}",
    )(q, kt, v, lambda_qk, subln_g)


TQ = 256


def _attn_lat_kernel(q_ref, kt_ref, v_ref, ckt_ref, cv_ref, lq_ref, sg_ref, att_in_ref, o_ref, *, lam_init):
    del att_in_ref
    lam = _lambda(lq_ref, lam_init)
    sg = sg_ref[...]
    for h in range(N_HEADS):
        cols = slice(h * V_DIM, (h + 1) * V_DIM)
        q = q_ref[:, cols] * (QK_DIM ** -0.5)
        vc = cv_ref[pl.ds(h, PAST_LEN, stride=N_HEADS), :].astype(BF)
        vl = v_ref[:, cols]
        probs = []
        for s in range(2):
            d = slice(s * QK_DIM, (s + 1) * QK_DIM)
            rows = slice(h * V_DIM + s * QK_DIM, h * V_DIM + (s + 1) * QK_DIM)
            sc_c = jnp.dot(q[:, d], ckt_ref[rows, :].astype(BF), preferred_element_type=F32)
            sc_l = jnp.dot(q[:, d], kt_ref[rows, :], preferred_element_type=F32)
            m = jnp.maximum(jnp.max(sc_c, axis=-1, keepdims=True), jnp.max(sc_l, axis=-1, keepdims=True))
            e_c = jnp.exp(sc_c - m)
            e_l = jnp.exp(sc_l - m)
            r = 1.0 / (jnp.sum(e_c, axis=-1, keepdims=True) + jnp.sum(e_l, axis=-1, keepdims=True))
            probs.append((e_c * r, e_l * r))
        a_c = (probs[0][0] - lam * probs[1][0]).astype(BF)
        a_l = (probs[0][1] - lam * probs[1][1]).astype(BF)
        o = jnp.dot(a_c, vc, preferred_element_type=F32) + jnp.dot(a_l, vl, preferred_element_type=F32)
        o_ref[:, cols] = _head_norm(o, sg, lam_init).astype(BF)


def _attn_lat(layer, q, kt, v, cache_kt, cache_v, lambda_qk, subln_g, att, lam_init):
    nqb = DEC_SEQ // TQ
    q0 = N_CTX // TQ
    return pl.pallas_call(
        functools.partial(_attn_lat_kernel, lam_init=lam_init),
        grid=(DEC_BATCH, nqb),
        in_specs=[
            pl.BlockSpec((TQ, ATT_WIDTH), lambda b, t: (q0 + b * nqb + t, 0)),
            pl.BlockSpec((None, ATT_WIDTH, DEC_SEQ), lambda b, t: (CTX_TILES + b, 0, 0)),
            pl.BlockSpec((DEC_SEQ, ATT_WIDTH), lambda b, t: (CTX_TILES + b, 0)),
            pl.BlockSpec((None, None, ATT_WIDTH, PAST_LEN), lambda b, t: (b, layer, 0, 0)),
            pl.BlockSpec((None, None, PAST_LEN * N_HEADS, V_DIM), lambda b, t: (b, layer, 0, 0)),
            pl.BlockSpec((None, 4, QK_DIM), lambda b, t: (layer, 0, 0)),
            pl.BlockSpec((None, 1, V_DIM), lambda b, t: (layer, 0, 0)),
            pl.BlockSpec(memory_space=pl.ANY),
        ],
        out_specs=pl.BlockSpec((TQ, ATT_WIDTH), lambda b, t: (q0 + b * nqb + t, 0)),
        out_shape=jax.ShapeDtypeStruct((N_TOK, ATT_WIDTH), BF),
        input_output_aliases={7: 0},
        compiler_params=_params(("arbitrary", "arbitrary")),
        name=f"attn_lat_l{layer}",
    )(q, kt, v, cache_kt, cache_v, lambda_qk, subln_g, att)


def _out_kernel(att_ref, conv_ref, w_ref, xa_ref, xb_ref, g1_ref, sh_ref, sc_ref, gf_ref,
                xo_ref, h2_ref, wb_s):
    i = pl.program_id(0)

    @pl.when(i == 0)
    def _():
        wb_s[...] = w_ref[...].astype(BF)

    mo = jnp.dot(att_ref[...], wb_s[:ATT_WIDTH, :], preferred_element_type=F32)
    mo = mo + jnp.dot(conv_ref[...], wb_s[ATT_WIDTH:, :], preferred_element_type=F32)
    xn = _stream_rows(xa_ref, xb_ref, i) + g1_ref[...] * mo
    xo_ref[...] = xn
    h2 = (_rms(xn) * gf_ref[...]) * (1.0 + sc_ref[...]) + sh_ref[...]
    h2_ref[...] = h2.astype(BF)


def _out_proj(layer, att, conv, w_out, xs, mod5, g_ffn):
    pair = isinstance(xs, tuple)
    xa, xb = xs if pair else (xs, xs)
    spec_a, spec_b = _stream_specs(pair)
    row_spec = pl.BlockSpec((TM, D_MODEL), lambda i: (i, 0))
    out_specs = [row_spec, row_spec]
    out_shape = [jax.ShapeDtypeStruct((N_TOK, D_MODEL), F32), jax.ShapeDtypeStruct((N_TOK, D_MODEL), BF)]
    return pl.pallas_call(
        _out_kernel,
        grid=(N_TILES,),
        in_specs=[
            pl.BlockSpec((TM, ATT_WIDTH), lambda i: (i, 0)),
            pl.BlockSpec((TM, CONV_WIDTH), lambda i: (i, 0)),
            pl.BlockSpec((None, D_MODEL, D_MODEL), lambda i: (layer, 0, 0)),
            spec_a, spec_b,
            _mod_spec(layer, 2), _mod_spec(layer, 3), _mod_spec(layer, 4),
            pl.BlockSpec((None, 1, D_MODEL), lambda i: (layer, 0, 0)),
        ],
        out_specs=out_specs,
        out_shape=out_shape,
        scratch_shapes=[pltpu.VMEM((D_MODEL, D_MODEL), BF)],
        compiler_params=_params(("arbitrary",)),
        name=f"out_proj_l{layer}",
    )(att, conv, w_out, xa, xb, mod5, mod5, mod5, g_ffn)


def _gu_kernel(h_ref, wg_ref, wu_ref, o_ref, wb_s):
    i = pl.program_id(0)
    j = pl.program_id(1)

    @pl.when(i == 0)
    def _():
        wb_s[j] = wg_ref[...].astype(BF)
        wb_s[j + N_FF_TILES] = wu_ref[...].astype(BF)

    h = h_ref[...]
    g = jnp.dot(h, wb_s[j], preferred_element_type=F32)
    u = jnp.dot(h, wb_s[j + N_FF_TILES], preferred_element_type=F32)
    o_ref[...] = (_silu(g) * u).astype(BF)


def _dense_gu(h2, w_gu):
    nj = N_FF_TILES
    return pl.pallas_call(
        _gu_kernel,
        grid=(N_TILES, nj),
        in_specs=[
            pl.BlockSpec((TM, D_MODEL), lambda i, j: (i, 0)),
            pl.BlockSpec((None, D_MODEL, TN_FF), lambda i, j: (0, 0, jnp.where(i == 0, j, nj - 1)),
                         pipeline_mode=pl.Buffered(1)),
            pl.BlockSpec((None, D_MODEL, TN_FF), lambda i, j: (0, 0, jnp.where(i == 0, j, nj - 1) + nj),
                         pipeline_mode=pl.Buffered(1)),
        ],
        out_specs=pl.BlockSpec((TM, TN_FF), lambda i, j: (i, j)),
        out_shape=jax.ShapeDtypeStruct((N_TOK, D_FF), BF),
        scratch_shapes=[pltpu.VMEM((2 * nj, D_MODEL, TN_FF), BF)],
        compiler_params=_params(("arbitrary", "arbitrary")),
        name="dense_gu",
    )(h2, w_gu, w_gu)


def _down_kernel(a_ref, w_ref, x_ref, g2_ref, o_ref, wb_s):
    j = pl.program_id(1)

    @pl.when(pl.program_id(0) == 0)
    def _():
        wb_s[j] = w_ref[...].astype(BF)

    y = jnp.dot(a_ref[...], wb_s[j], preferred_element_type=F32)
    o_ref[...] = x_ref[...] + g2_ref[...] * y


def _dense_down(layer, act, w_down, x, mod5):
    nj = N_DOWN_TILES
    return pl.pallas_call(
        _down_kernel,
        grid=(N_TILES, nj),
        in_specs=[
            pl.BlockSpec((TM, D_FF), lambda i, j: (i, 0)),
            pl.BlockSpec((None, D_FF, TN_DOWN), lambda i, j: (0, 0, jnp.where(i == 0, j, nj - 1))),
            pl.BlockSpec((TM, TN_DOWN), lambda i, j: (i, j)),
            pl.BlockSpec((None, None, None, 1, TN_DOWN), lambda i, j: (layer, _mod_row(i), 5, 0, j)),
        ],
        out_specs=pl.BlockSpec((TM, TN_DOWN), lambda i, j: (i, j)),
        out_shape=jax.ShapeDtypeStruct((N_TOK, D_MODEL), F32),
        scratch_shapes=[pltpu.VMEM((nj, D_FF, TN_DOWN), BF)],
        compiler_params=_params(("arbitrary", "arbitrary")),
        name="dense_down",
    )(act, w_down, x, mod5)


def _router_kernel(h_ref, wr_ref, lp_ref, rg_ref, n_ref):
    logits = jnp.dot(h_ref[...], wr_ref[...].astype(BF), preferred_element_type=F32)
    lane = lax.broadcasted_iota(jnp.int32, logits.shape, 1)
    lg = jnp.where(lane < N_EXPERTS, logits, -jnp.inf)
    m1 = jnp.max(lg, axis=-1, keepdims=True)
    i1 = jnp.min(jnp.where(lg == m1, lane, LANES), axis=-1, keepdims=True)
    lg2 = jnp.where(lane == i1, -jnp.inf, lg)
    m2 = jnp.max(lg2, axis=-1, keepdims=True)
    i2 = jnp.min(jnp.where(lg2 == m2, lane, LANES), axis=-1, keepdims=True)
    e2 = jnp.exp(m2 - m1)
    w1 = 1.0 / (1.0 + e2)
    w2 = e2 / (1.0 + e2)

    sel1 = lane == i1
    sel2 = lane == i2
    onehot = jnp.logical_or(sel1, sel2)
    rows = lax.broadcasted_iota(jnp.int32, (TM, TM), 0)
    colsi = lax.broadcasted_iota(jnp.int32, (TM, TM), 1)
    earlier = jnp.logical_and(colsi < rows, (colsi // ST) == (rows // ST))
    before = jnp.dot(earlier.astype(BF), onehot.astype(BF), preferred_element_type=F32)
    onehot_f = onehot.astype(F32)
    counts = [jnp.sum(onehot_f[s * ST:(s + 1) * ST], axis=0, keepdims=True) for s in range(SUB_PER_TILE)]
    counts = jnp.concatenate(counts + [jnp.zeros((8 - SUB_PER_TILE, LANES), F32)], axis=0).astype(jnp.int32)
    seg_len = ((counts + (SEG_ALIGN - 1)) // SEG_ALIGN) * SEG_ALIGN
    n_ref[...] = seg_len
    la = lax.broadcasted_iota(jnp.int32, (LANES, LANES), 0)
    lb = lax.broadcasted_iota(jnp.int32, (LANES, LANES), 1)
    seg_start = jnp.dot(seg_len.astype(F32).astype(BF), (la < lb).astype(BF), preferred_element_type=F32)
    start = jnp.concatenate(
        [jnp.broadcast_to(seg_start[s:s + 1], (ST, LANES)) for s in range(SUB_PER_TILE)], axis=0)
    where = before + start
    lp1 = jnp.sum(jnp.where(sel1, where, 0.0), axis=-1, keepdims=True).astype(jnp.int32)
    lp2 = jnp.sum(jnp.where(sel2, where, 0.0), axis=-1, keepdims=True).astype(jnp.int32)
    lp_ref[...] = jnp.where(lane == 0, lp1, jnp.where(lane == 1, lp2, 0))
    rg_ref[...] = jnp.where(lane == 0, w1, jnp.where(lane == 1, w2, 0.0))


def _router(h2, w_router_pad):
    return pl.pallas_call(
        _router_kernel,
        grid=(N_TILES,),
        in_specs=[
            pl.BlockSpec((TM, D_MODEL), lambda i: (i, 0)),
            pl.BlockSpec((D_MODEL, LANES), lambda i: (0, 0)),
        ],
        out_specs=[
            pl.BlockSpec((TM, LANES), lambda i: (i, 0)),
            pl.BlockSpec((TM, LANES), lambda i: (i, 0)),
            pl.BlockSpec((None, 8, LANES), lambda i: (i, 0, 0)),
        ],
        out_shape=[
            jax.ShapeDtypeStruct((N_TOK, LANES), jnp.int32),
            jax.ShapeDtypeStruct((N_TOK, LANES), F32),
            jax.ShapeDtypeStruct((N_TILES, 8, LANES), jnp.int32),
        ],
        compiler_params=_params(("arbitrary",)),
        name="router",
    )(h2, w_router_pad)


def _segment_copies(s, seg_ref, n_ref, dst_ref, stage, rows_hbm, sem, *, to_hbm, wait):
    for e in range(N_EXPERTS):
        k = s * N_EXPERTS + e
        n = n_ref[k]
        v0 = seg_ref[k]
        h0 = dst_ref[k]
        for z in SEG_PIECES:
            off = n & (-2 * z)

            @pl.when((n & z) != 0)
            def _():
                v = stage.at[pl.ds(pl.multiple_of(v0 + off, SEG_ALIGN), z)]
                h = rows_hbm.at[pl.ds(pl.multiple_of(h0 + off, SEG_ALIGN), z)]
                cp = pltpu.make_async_copy(v, h, sem) if to_hbm else pltpu.make_async_copy(h, v, sem)
                if wait:
                    cp.wait()
                else:
                    cp.start()


def _dispatch_kernel(seg_ref, n_ref, dst_ref, h_ref, lp_ref, xs_ref, stage_s, sem):
    s = pl.program_id(0)
    slot = s % 2
    copies = functools.partial(_segment_copies, seg_ref=seg_ref, n_ref=n_ref, dst_ref=dst_ref,
                               rows_hbm=xs_ref, to_hbm=True)

    @pl.when(s >= 2)
    def _():
        copies(s - 2, stage=stage_s.at[slot], sem=sem.at[slot], wait=True)

    lpt = lp_ref[...].T
    r = lax.broadcasted_iota(jnp.int32, (STAGE_ROWS, ST), 0)
    pick = jnp.logical_or(r == lpt[0:1, :], r == lpt[1:2, :]).astype(BF)
    stage_s[slot] = jnp.dot(pick, h_ref[...], preferred_element_type=F32).astype(BF)
    copies(s, stage=stage_s.at[slot], sem=sem.at[slot], wait=False)

    @pl.when(s == N_SUB - 1)
    def _():
        copies(s - 1, stage=stage_s.at[1 - slot], sem=sem.at[1 - slot], wait=True)
        copies(s, stage=stage_s.at[slot], sem=sem.at[slot], wait=True)


def _dispatch(seg, n, dst, h2, lp):
    grid_spec = pltpu.PrefetchScalarGridSpec(
        num_scalar_prefetch=3,
        grid=(N_SUB,),
        in_specs=[
            pl.BlockSpec((ST, D_MODEL), lambda s, *_: (s, 0)),
            pl.BlockSpec((ST, LANES), lambda s, *_: (s, 0)),
        ],
        out_specs=pl.BlockSpec(memory_space=pl.ANY),
        scratch_shapes=[pltpu.VMEM((2, STAGE_ROWS, D_MODEL), BF), pltpu.SemaphoreType.DMA((2,))],
    )
    return pl.pallas_call(
        _dispatch_kernel,
        grid_spec=grid_spec,
        out_shape=jax.ShapeDtypeStruct((R_PAD, D_MODEL), BF),
        compiler_params=_params(("arbitrary",)),
        name="moe_dispatch",
    )(seg, n, dst, h2, lp)


def _expert_weights(te_ref, nt_ref, nxt_ref, w_hbm, wf_s, wb_s, sem):
    r = pl.program_id(0)

    def fetch(e):
        return pltpu.make_async_copy(w_hbm.at[0, e], wf_s, sem)

    @pl.when(r == 0)
    def _():
        fetch(te_ref[0]).start()

    first = jnp.logical_or(r == 0, te_ref[r] != te_ref[jnp.maximum(r - 1, 0)])

    @pl.when(jnp.logical_and(r < nt_ref[0], first))
    def _():
        fetch(te_ref[r]).wait()
        wb_s[...] = wf_s[...].astype(BF)

        @pl.when(nxt_ref[r] >= 0)
        def _():
            fetch(nxt_ref[r]).start()


def _moe_gu_kernel(te_ref, nt_ref, nxt_ref, x_ref, w_hbm, o_ref, wf_s, wb_s, sem):
    _expert_weights(te_ref, nt_ref, nxt_ref, w_hbm, wf_s, wb_s, sem)

    @pl.when(pl.program_id(0) < nt_ref[0])
    def _():
        x = x_ref[...]
        g = jnp.dot(x, wb_s[:, :D_FF_EXPERT], preferred_element_type=F32)
        u = jnp.dot(x, wb_s[:, D_FF_EXPERT:], preferred_element_type=F32)
        o_ref[...] = (_silu(g) * u).astype(BF)


def _moe_down_kernel(te_ref, nt_ref, nxt_ref, a_ref, w_hbm, o_ref, wf_s, wb_s, sem):
    _expert_weights(te_ref, nt_ref, nxt_ref, w_hbm, wf_s, wb_s, sem)

    @pl.when(pl.program_id(0) < nt_ref[0])
    def _():
        o_ref[...] = jnp.dot(a_ref[...], wb_s[...], preferred_element_type=F32).astype(BF)


def _grouped_matmul(body, name, te, nt, nxt, rows, w, k, n_w, n_out):
    tile_map = lambda r, te, nt, nxt: (jnp.minimum(r, nt[0] - 1), 0)
    grid_spec = pltpu.PrefetchScalarGridSpec(
        num_scalar_prefetch=3,
        grid=(NT_G,),
        in_specs=[pl.BlockSpec((TG, k), tile_map), pl.BlockSpec(memory_space=pl.ANY)],
        out_specs=pl.BlockSpec((TG, n_out), tile_map),
        scratch_shapes=[pltpu.VMEM((k, n_w), F32), pltpu.VMEM((k, n_w), BF), pltpu.SemaphoreType.DMA(())],
    )
    return pl.pallas_call(
        body,
        grid_spec=grid_spec,
        out_shape=jax.ShapeDtypeStruct((R_PAD, n_out), BF),
        compiler_params=_params(("arbitrary",)),
        name=name,
    )(te, nt, nxt, rows, w)


def _moe_gu(te, nt, nxt, xs, w_gu):
    return _grouped_matmul(_moe_gu_kernel, "moe_gu", te, nt, nxt, xs, w_gu, D_MODEL, 2 * D_FF_EXPERT, D_FF_EXPERT)


def _moe_down(te, nt, nxt, act, w_down):
    return _grouped_matmul(_moe_down_kernel, "moe_down", te, nt, nxt, act, w_down, D_FF_EXPERT, D_MODEL, D_MODEL)


def _combine_kernel(seg_ref, n_ref, dst_ref, ys_ref, lp_ref, rg_ref, x_ref, g2_ref, fg_ref,
                    oa_ref, ob_ref, stage_s, sem):
    s = pl.program_id(0)
    slot = s % 2
    copies = functools.partial(_segment_copies, seg_ref=seg_ref, n_ref=n_ref, dst_ref=dst_ref,
                               rows_hbm=ys_ref, to_hbm=False)

    @pl.when(s == 0)
    def _():
        stage_s[...] = jnp.zeros_like(stage_s)
        copies(s, stage=stage_s.at[slot], sem=sem.at[slot], wait=False)

    @pl.when(s + 1 < N_SUB)
    def _():
        copies(s + 1, stage=stage_s.at[1 - slot], sem=sem.at[1 - slot], wait=False)

    copies(s, stage=stage_s.at[slot], sem=sem.at[slot], wait=True)

    lp = lp_ref[...]
    r = lax.broadcasted_iota(jnp.int32, (ST, STAGE_ROWS), 1)
    rows = stage_s[slot]
    a = jnp.dot((r == lp[:, 0:1]).astype(BF), rows, preferred_element_type=F32)
    b = jnp.dot((r == lp[:, 1:2]).astype(BF), rows, preferred_element_type=F32)
    rg = rg_ref[...]
    y = rg[:, 0:1] * a + rg[:, 1:2] * b
    xn = x_ref[...] + g2_ref[...] * y
    out = _rms(xn) * fg_ref[...]

    @pl.when(s < CTX_SUB)
    def _():
        oa_ref[...] = out

    @pl.when(s >= CTX_SUB)
    def _():
        ob_ref[...] = out


def _combine(layer, seg, n, dst, ys, lp, rg, x, mod5, final_g):
    def mod_map(s, *_):
        return (layer, _mod_row(s // SUB_PER_TILE), 5, 0, 0)

    grid_spec = pltpu.PrefetchScalarGridSpec(
        num_scalar_prefetch=3,
        grid=(N_SUB,),
        in_specs=[
            pl.BlockSpec(memory_space=pl.ANY),
            pl.BlockSpec((ST, LANES), lambda s, *_: (s, 0)),
            pl.BlockSpec((ST, LANES), lambda s, *_: (s, 0)),
            pl.BlockSpec((ST, D_MODEL), lambda s, *_: (s, 0)),
            pl.BlockSpec((None, None, None, 1, D_MODEL), mod_map),
            pl.BlockSpec((1, D_MODEL), lambda s, *_: (0, 0)),
        ],
        out_specs=[
            pl.BlockSpec((ST, D_MODEL), lambda s, *_: (jnp.minimum(s, CTX_SUB - 1), 0)),
            pl.BlockSpec((ST, D_MODEL), lambda s, *_: (jnp.maximum(s - CTX_SUB, 0), 0)),
        ],
        scratch_shapes=[pltpu.VMEM((2, STAGE_ROWS, D_MODEL), BF), pltpu.SemaphoreType.DMA((2,))],
    )
    return pl.pallas_call(
        _combine_kernel,
        grid_spec=grid_spec,
        out_shape=[
            jax.ShapeDtypeStruct((N_CTX, D_MODEL), F32),
            jax.ShapeDtypeStruct((N_LAT, D_MODEL), F32),
        ],
        compiler_params=_params(("arbitrary",)),
        name="moe_combine",
    )(seg, n, dst, ys, lp, rg, x, mod5, final_g)


def _group_layout(n_tiles):
    n = n_tiles[:, :SUB_PER_TILE, :N_EXPERTS].reshape(N_SUB, N_EXPERTS)
    tiles = (jnp.sum(n, axis=0) + TG - 1) // TG
    tile_end = jnp.cumsum(tiles)
    region = (tile_end - tiles) * TG
    dst = region[None, :] + jnp.cumsum(n, axis=0) - n
    seg = jnp.cumsum(n, axis=1) - n
    nt = tile_end[-1]
    tile_id = jnp.minimum(jnp.arange(NT_G, dtype=jnp.int32), nt - 1)
    te = jnp.sum((tile_id[:, None] >= tile_end[None, :]).astype(jnp.int32), axis=-1)
    after = jnp.sum(jnp.where(te[:, None] == jnp.arange(N_EXPERTS), tile_end[None, :], 0), axis=-1)
    nxt = jnp.where(after < nt, jnp.sum((after[:, None] >= tile_end[None, :]).astype(jnp.int32), axis=-1), -1)
    flat = lambda a: a.reshape(N_SUB * N_EXPERTS).astype(jnp.int32)
    i32 = lambda a: a.astype(jnp.int32)
    return flat(seg), flat(n), flat(dst), i32(te), i32(nt.reshape(1)), i32(nxt)


def _rope_tables():
    p = np.arange(DEC_SEQ)
    row = (p // GRID_W).astype(np.float32)
    col = (p % GRID_W).astype(np.float32)
    half = QK_DIM // 4
    freqs = (ROPE_BASE ** (-np.arange(half, dtype=np.float32) / half)).astype(np.float32)
    lane = np.arange(V_DIM)
    f = freqs[lane & (half - 1)]
    use_col = (lane & (2 * half)) != 0
    ang = (np.where(use_col[None, :], col[:, None], row[:, None]) * f[None, :]).astype(np.float32)
    upper = (lane & half) != 0
    sin = np.sin(ang)
    return jnp.asarray(np.cos(ang), F32), jnp.asarray(np.where(upper[None, :], sin, -sin), F32)


def kernel(x_prompt, x_sample, cache_k, cache_v, c, c_ctx, w_ada, b_ada, norm_mix_g, norm_ffn_g,
           w_in, lambda_qk, subln_g, conv_w, w_out, w_gu_dense, w_down_dense, w_router,
           w_gu_moe, w_down_moe, final_g):
    assert DEPTH == 2
    xs = (x_prompt.reshape(N_CTX, D_MODEL), x_sample.reshape(N_LAT, D_MODEL))
    cond = jnp.concatenate([c_ctx[None, :], c, jnp.zeros((COND_ROWS - 1 - DEC_BATCH, D_MODEL), F32)], axis=0)
    mod5 = _ada(cond, w_ada, b_ada).reshape(DEPTH, COND_ROWS, N_MOD, 1, D_MODEL)
    cos_t, sin_t = _rope_tables()
    cache_kt = jnp.transpose(cache_k, (0, 1, 3, 4, 5, 2)).reshape(DEC_BATCH, DEPTH, ATT_WIDTH, PAST_LEN)
    cache_v4 = cache_v.reshape(DEC_BATCH, DEPTH, PAST_LEN * N_HEADS, V_DIM)
    g_mix = norm_mix_g.reshape(DEPTH, 1, D_MODEL)
    g_ffn = norm_ffn_g.reshape(DEPTH, 1, D_MODEL)
    sg = subln_g.reshape(DEPTH, 1, V_DIM)

    new_kv = None
    for layer in range(DEPTH):
        lam_init = 0.8 - 0.6 * math.exp(-0.3 * layer)
        q, v, kt, conv, nk, nv = _in_proj(layer, xs, mod5, g_mix, w_in, cos_t, sin_t, conv_w, new_kv)
        new_kv = (nk, nv)
        att = _attn_ctx(layer, q, kt, v, lambda_qk, sg, lam_init)
        att = _attn_lat(layer, q, kt, v, cache_kt, cache_v4, lambda_qk, sg, att, lam_init)
        x1, h2 = _out_proj(layer, att, conv, w_out, xs, mod5, g_ffn)
        if layer == 0:
            act = _dense_gu(h2, w_gu_dense)
            xs = _dense_down(layer, act, w_down_dense, x1, mod5)
        else:
            wr = jnp.pad(w_router[0], ((0, 0), (0, LANES - N_EXPERTS)))
            lp, rg, n_tiles = _router(h2, wr)
            seg, n, dst, te, nt, nxt = _group_layout(n_tiles)
            xsort = _dispatch(seg, n, dst, h2, lp)
            act = _moe_gu(te, nt, nxt, xsort, w_gu_moe)
            ys = _moe_down(te, nt, nxt, act, w_down_moe)
            y_ctx, y_lat = _combine(layer, seg, n, dst, ys, lp, rg, x1, mod5, final_g.reshape(1, D_MODEL))
    nk, nv = new_kv
    new_k = jnp.transpose(nk.reshape(BATCH, DEPTH, N_HEADS, 2, QK_DIM, SEQ), (0, 1, 5, 2, 3, 4))
    new_v = nv.reshape(BATCH, DEPTH, SEQ, N_HEADS, V_DIM)
    return (y_ctx.reshape(BATCH, SEQ, D_MODEL), y_lat.reshape(DEC_BATCH, DEC_SEQ, D_MODEL), new_k, new_v)
```

```python
import functools
import math

import numpy as np
import jax
import jax.numpy as jnp
from jax import lax
from jax.experimental import pallas as pl
from jax.experimental.pallas import tpu as pltpu

D_MODEL = 1024
BATCH = 16
SEQ = 256
DEPTH = 2
DEC_BATCH = 4
DEC_SEQ = 1024
PAST_LEN = 512
GRID_W = 64
ATT_WIDTH = 512
CONV_WIDTH = 512
N_HEADS = 4
V_DIM = 128
QK_DIM = 64
ROPE_BASE = 10000.0
D_FF = 2816
N_EXPERTS = 8
D_FF_EXPERT = 1408
N_MOD = 6
NORM_EPS = 1e-6
Q_SCALE = QK_DIM ** -0.5 * math.log2(math.e)
IN_COLS = 3 * ATT_WIDTH + 3 * CONV_WIDTH

N_CTX = BATCH * SEQ
N_LAT = DEC_BATCH * DEC_SEQ
N_TOK = N_CTX + N_LAT
TM = 1024
N_TILES = N_TOK // TM
CTX_TILES = N_CTX // TM
SEQ_PER_TILE = TM // SEQ
COND_ROWS = 8
TN_IN = 512
N_IN_TILES = IN_COLS // TN_IN
ROW_CHUNK = 512
TN_FF = 1408
N_FF_TILES = D_FF // TN_FF
TN_DOWN = 512
N_DOWN_TILES = D_MODEL // TN_DOWN
TN_ADA = 1536
TG = 512
ST = 256
SUB_PER_TILE = TM // ST
N_SUB = N_TOK // ST
CTX_SUB = N_CTX // ST
SEG_ALIGN = 16
STAGE_ROWS = 640
STAGE_CHUNKS = STAGE_ROWS // SEG_ALIGN
NT_G = -(-(2 * N_TOK + N_SUB * N_EXPERTS * (SEG_ALIGN - 1) + N_EXPERTS * (TG - SEG_ALIGN)) // TG)
R_PAD = NT_G * TG
LANES = 128
VMEM_LIMIT = 60 * 1024 * 1024

BF = jnp.bfloat16
F32 = jnp.float32


def _params(sem, vmem=VMEM_LIMIT):
    return pltpu.CompilerParams(dimension_semantics=sem, vmem_limit_bytes=vmem)


def _mod_row(i):
    return jnp.where(i < CTX_TILES, 0, i - (CTX_TILES - 1))


def _mod_spec(layer, c, width=D_MODEL):
    return pl.BlockSpec((None, None, None, 1, width), lambda i, *_: (layer, _mod_row(i), c, 0, 0))


def _stream_specs(pair, width=D_MODEL):
    a = pl.BlockSpec((TM, width), lambda i, *_: (jnp.minimum(i, CTX_TILES - 1), 0))
    if pair:
        b = pl.BlockSpec((TM, width), lambda i, *_: (jnp.maximum(i - CTX_TILES, 0), 0))
    else:
        b = pl.BlockSpec((TM, width), lambda i, *_: (jnp.maximum(i, CTX_TILES), 0))
    return a, b


def _stream_rows(xa_ref, xb_ref, i):
    return jnp.where(i >= CTX_TILES, xb_ref[...], xa_ref[...])


def _silu(x):
    return x / (1.0 + jnp.exp(-x))


def _rms(x):
    return x * lax.rsqrt(jnp.mean(x * x, axis=-1, keepdims=True) + NORM_EPS)


def _ada_kernel(c_ref, w_ref, b_ref, o_ref):
    s = _silu(c_ref[...]).astype(BF)
    o_ref[...] = jnp.dot(s, w_ref[...].astype(BF), preferred_element_type=F32) + b_ref[...]


def _ada(cond, w_ada, b_ada):
    n = N_MOD * D_MODEL
    return pl.pallas_call(
        _ada_kernel,
        grid=(DEPTH, n // TN_ADA),
        in_specs=[
            pl.BlockSpec((COND_ROWS, D_MODEL), lambda l, j: (0, 0)),
            pl.BlockSpec((None, D_MODEL, TN_ADA), lambda l, j: (l, 0, j)),
            pl.BlockSpec((None, 1, TN_ADA), lambda l, j: (l, 0, j)),
        ],
        out_specs=pl.BlockSpec((None, COND_ROWS, TN_ADA), lambda l, j: (l, 0, j)),
        out_shape=jax.ShapeDtypeStruct((DEPTH, COND_ROWS, n), F32),
        compiler_params=_params(("arbitrary", "arbitrary")),
        name="ada_mod",
    )(cond, w_ada, b_ada.reshape(DEPTH, 1, n))


def _in_kernel(*refs, aliased):
    xa_ref, xb_ref, sh_ref, sc_ref, g_ref, w_ref, cos_ref, sin_ref, cw_ref = refs[:9]
    refs = refs[11:] if aliased else refs[9:]
    q_ref, v_ref, kt_ref, conv_ref, nk_ref, nv_ref, h_s, wb_s, gb_s, gc_s = refs
    i = pl.program_id(0)
    j = pl.program_id(1)
    lat = i >= CTX_TILES
    ctx = jnp.logical_not(lat)

    @pl.when(i == 0)
    def _():
        wb_s[j] = w_ref[...].astype(BF)

    chunks = [slice(c * ROW_CHUNK, (c + 1) * ROW_CHUNK) for c in range(TM // ROW_CHUNK)]
    seqs_per_chunk = ROW_CHUNK // SEQ

    def norm(rows):
        gain = g_ref[...] * (1.0 + sc_ref[...])
        x = jnp.where(lat, xb_ref[rows, :], xa_ref[rows, :])
        h_s[rows, :] = (_rms(x) * gain + sh_ref[...]).astype(BF)

    def proj(rows):
        return jnp.dot(h_s[rows, :], wb_s[j], preferred_element_type=F32)

    def roped(a, rows):
        cos = jnp.concatenate([cos_ref[rows, :]] * N_HEADS, axis=1)
        sin = jnp.concatenate([sin_ref[rows, :]] * N_HEADS, axis=1)
        lane = lax.broadcasted_iota(jnp.int32, a.shape, 1)
        upper = (lane & (QK_DIM // 4)) != 0
        partner = jnp.where(upper, pltpu.roll(a, QK_DIM // 4, 1), pltpu.roll(a, TN_IN - QK_DIM // 4, 1))
        return a * cos + partner * sin

    @pl.when(jnp.logical_and(j == 0, lat))
    def _():
        for rows in chunks:
            norm(rows)
            q_ref[rows, :] = (roped(proj(rows), rows) * Q_SCALE).astype(BF)

    @pl.when(jnp.logical_and(j == 0, ctx))
    def _():
        for rows in chunks:
            norm(rows)
            q_ref[rows, :] = (proj(rows) * Q_SCALE).astype(BF)

    @pl.when(jnp.logical_and(j == 1, lat))
    def _():
        for rows in chunks:
            kt_ref[:, rows] = roped(proj(rows), rows).T.astype(BF)

    @pl.when(jnp.logical_and(j == 1, ctx))
    def _():
        for c, rows in enumerate(chunks):
            kt = proj(rows).T
            kt_ref[:, rows] = kt.astype(BF)
            for s in range(seqs_per_chunk):
                nk_ref[c * seqs_per_chunk + s] = kt[:, s * SEQ:(s + 1) * SEQ]

    @pl.when(jnp.logical_and(j == 2, lat))
    def _():
        for rows in chunks:
            v_ref[rows, :] = proj(rows).astype(BF)

    @pl.when(jnp.logical_and(j == 2, ctx))
    def _():
        for c, rows in enumerate(chunks):
            acc = proj(rows)
            v_ref[rows, :] = acc.astype(BF)
            for s in range(seqs_per_chunk):
                for h in range(N_HEADS):
                    nv_ref[c * seqs_per_chunk + s, pl.ds(h, SEQ, stride=N_HEADS), :] = (
                        acc[s * SEQ:(s + 1) * SEQ, h * V_DIM:(h + 1) * V_DIM])

    @pl.when(j == 3)
    def _():
        for rows in chunks:
            gb_s[rows, :] = proj(rows)

    @pl.when(j == 4)
    def _():
        for rows in chunks:
            gc_s[rows, :] = proj(rows)

    @pl.when(j == 5)
    def _():
        for rows in chunks:
            gc_s[rows, :] = gc_s[rows, :] * proj(rows)
        u = gc_s[...]
        seq = jnp.where(lat, DEC_SEQ, SEQ)
        pos = lax.broadcasted_iota(jnp.int32, (TM, 1), 0) & (seq - 1)
        prev = jnp.where(pos == 0, 0.0, pltpu.roll(u, 1, 0))
        nxt = jnp.where(pos == seq - 1, 0.0, pltpu.roll(u, TM - 1, 0))
        cw = cw_ref[...]
        conv = prev * cw[0:1] + u * cw[1:2] + nxt * cw[2:3]
        conv_ref[...] = (gb_s[...] * conv).astype(BF)


def _in_proj(layer, xs, mod5, g_mix, w_in, cos_t, sin_t, conv_w, new_kv):
    pair = isinstance(xs, tuple)
    xa, xb = xs if pair else (xs, xs)
    spec_a, spec_b = _stream_specs(pair)
    ctx_i = lambda i: jnp.minimum(i, CTX_TILES - 1)
    in_specs = [
        spec_a, spec_b,
        _mod_spec(layer, 0), _mod_spec(layer, 1),
        pl.BlockSpec((None, 1, D_MODEL), lambda i, j: (layer, 0, 0)),
        pl.BlockSpec((None, D_MODEL, TN_IN), lambda i, j: (layer, 0, jnp.where(i == 0, j, N_IN_TILES - 1))),
        pl.BlockSpec((DEC_SEQ, V_DIM), lambda i, j: (0, 0)),
        pl.BlockSpec((DEC_SEQ, V_DIM), lambda i, j: (0, 0)),
        pl.BlockSpec((None, 3, CONV_WIDTH), lambda i, j: (layer, 0, 0)),
    ]
    args = [xa, xb, mod5, mod5, g_mix, w_in, cos_t, sin_t, conv_w]
    aliases = {}
    if new_kv is not None:
        in_specs += [pl.BlockSpec(memory_space=pl.ANY), pl.BlockSpec(memory_space=pl.ANY)]
        args += list(new_kv)
        aliases = {9: 4, 10: 5}
    row_tile = pl.BlockSpec((TM, TN_IN), lambda i, j: (i, 0))
    return pl.pallas_call(
        functools.partial(_in_kernel, aliased=new_kv is not None),
        grid=(N_TILES, N_IN_TILES),
        in_specs=in_specs,
        out_specs=[
            row_tile,
            row_tile,
            pl.BlockSpec((None, ATT_WIDTH, TM), lambda i, j: (i, 0, 0)),
            row_tile,
            pl.BlockSpec((SEQ_PER_TILE, None, ATT_WIDTH, SEQ), lambda i, j: (ctx_i(i), layer, 0, 0)),
            pl.BlockSpec((SEQ_PER_TILE, None, SEQ * N_HEADS, V_DIM), lambda i, j: (ctx_i(i), layer, 0, 0)),
        ],
        out_shape=[
            jax.ShapeDtypeStruct((N_TOK, ATT_WIDTH), BF),
            jax.ShapeDtypeStruct((N_TOK, ATT_WIDTH), BF),
            jax.ShapeDtypeStruct((N_TILES, ATT_WIDTH, TM), BF),
            jax.ShapeDtypeStruct((N_TOK, CONV_WIDTH), BF),
            jax.ShapeDtypeStruct((BATCH, DEPTH, ATT_WIDTH, SEQ), F32),
            jax.ShapeDtypeStruct((BATCH, DEPTH, SEQ * N_HEADS, V_DIM), F32),
        ],
        scratch_shapes=[
            pltpu.VMEM((TM, D_MODEL), BF),
            pltpu.VMEM((N_IN_TILES, D_MODEL, TN_IN), BF),
            pltpu.VMEM((TM, CONV_WIDTH), F32),
            pltpu.VMEM((TM, CONV_WIDTH), F32),
        ],
        input_output_aliases=aliases,
        compiler_params=_params(("arbitrary", "arbitrary")),
        name=f"in_proj_l{layer}",
    )(*args)


def _lambda(lq_ref, lam_init):
    lq = lq_ref[...]
    a = jnp.exp(jnp.sum(lq[0:1] * lq[1:2], axis=-1, keepdims=True))
    b = jnp.exp(jnp.sum(lq[2:3] * lq[3:4], axis=-1, keepdims=True))
    return a - b + lam_init


def _head_norm(o, sg, lam_init):
    return _rms(o) * sg * (1.0 - lam_init)


def _attn_ctx_kernel(q_ref, kt_ref, v_ref, lq_ref, sg_ref, o_ref, *, lam_init):
    lam = _lambda(lq_ref, lam_init)
    sg = sg_ref[...]
    for h in range(N_HEADS):
        cols = slice(h * V_DIM, (h + 1) * V_DIM)
        q = q_ref[:, cols]
        v = v_ref[:, cols]
        outs = []
        for s in range(2):
            d = slice(s * QK_DIM, (s + 1) * QK_DIM)
            kt = kt_ref[h * V_DIM + s * QK_DIM:h * V_DIM + (s + 1) * QK_DIM, :]
            sc = jnp.dot(q[:, d], kt, preferred_element_type=F32)
            e = jnp.exp2(sc - jnp.max(sc, axis=-1, keepdims=True))
            r = 1.0 / jnp.sum(e, axis=-1, keepdims=True)
            outs.append(jnp.dot(e.astype(BF), v, preferred_element_type=F32) * r)
        o = outs[0] - lam * outs[1]
        o_ref[:, cols] = _head_norm(o, sg, lam_init).astype(BF)


def _attn_ctx(layer, q, kt, v, lambda_qk, subln_g, lam_init):
    return pl.pallas_call(
        functools.partial(_attn_ctx_kernel, lam_init=lam_init),
        grid=(BATCH,),
        in_specs=[
            pl.BlockSpec((SEQ, ATT_WIDTH), lambda b: (b, 0)),
            pl.BlockSpec((None, ATT_WIDTH, SEQ), lambda b: (b // SEQ_PER_TILE, 0, b % SEQ_PER_TILE)),
            pl.BlockSpec((SEQ, ATT_WIDTH), lambda b: (b, 0)),
            pl.BlockSpec((None, 4, QK_DIM), lambda b: (layer, 0, 0)),
            pl.BlockSpec((None, 1, V_DIM), lambda b: (layer, 0, 0)),
        ],
        out_specs=pl.BlockSpec((SEQ, ATT_WIDTH), lambda b: (b, 0)),
        out_shape=jax.ShapeDtypeStruct((N_TOK, ATT_WIDTH), BF),
        compiler_params=_params(("arbitrary",)),
        name=f"attn_ctx_l{layer}",
    )(q, kt, v, lambda_qk, subln_g)


TQ = 256


def _attn_lat_kernel(q_ref, kt_ref, v_ref, ckt_ref, cv_ref, lq_ref, sg_ref, att_in_ref, o_ref, *, lam_init):
    del att_in_ref
    lam = _lambda(lq_ref, lam_init)
    sg = sg_ref[...]
    for h in range(N_HEADS):
        cols = slice(h * V_DIM, (h + 1) * V_DIM)
        q = q_ref[:, cols]
        vc = cv_ref[pl.ds(h, PAST_LEN, stride=N_HEADS), :].astype(BF)
        vl = v_ref[:, cols]
        probs = []
        for s in range(2):
            d = slice(s * QK_DIM, (s + 1) * QK_DIM)
            rows = slice(h * V_DIM + s * QK_DIM, h * V_DIM + (s + 1) * QK_DIM)
            sc_c = jnp.dot(q[:, d], ckt_ref[rows, :].astype(BF), preferred_element_type=F32)
            sc_l = jnp.dot(q[:, d], kt_ref[rows, :], preferred_element_type=F32)
            m = jnp.maximum(jnp.max(sc_c, axis=-1, keepdims=True), jnp.max(sc_l, axis=-1, keepdims=True))
            e_c = jnp.exp2(sc_c - m)
            e_l = jnp.exp2(sc_l - m)
            r = 1.0 / (jnp.sum(e_c, axis=-1, keepdims=True) + jnp.sum(e_l, axis=-1, keepdims=True))
            probs.append((e_c * r, e_l * r))
        a_c = (probs[0][0] - lam * probs[1][0]).astype(BF)
        a_l = (probs[0][1] - lam * probs[1][1]).astype(BF)
        o = jnp.dot(a_c, vc, preferred_element_type=F32) + jnp.dot(a_l, vl, preferred_element_type=F32)
        o_ref[:, cols] = _head_norm(o, sg, lam_init).astype(BF)


def _attn_lat(layer, q, kt, v, cache_kt, cache_v, lambda_qk, subln_g, att, lam_init):
    nqb = DEC_SEQ // TQ
    q0 = N_CTX // TQ
    return pl.pallas_call(
        functools.partial(_attn_lat_kernel, lam_init=lam_init),
        grid=(DEC_BATCH, nqb),
        in_specs=[
            pl.BlockSpec((TQ, ATT_WIDTH), lambda b, t: (q0 + b * nqb + t, 0)),
            pl.BlockSpec((None, ATT_WIDTH, DEC_SEQ), lambda b, t: (CTX_TILES + b, 0, 0)),
            pl.BlockSpec((DEC_SEQ, ATT_WIDTH), lambda b, t: (CTX_TILES + b, 0)),
            pl.BlockSpec((None, None, ATT_WIDTH, PAST_LEN), lambda b, t: (b, layer, 0, 0)),
            pl.BlockSpec((None, None, PAST_LEN * N_HEADS, V_DIM), lambda b, t: (b, layer, 0, 0)),
            pl.BlockSpec((None, 4, QK_DIM), lambda b, t: (layer, 0, 0)),
            pl.BlockSpec((None, 1, V_DIM), lambda b, t: (layer, 0, 0)),
            pl.BlockSpec(memory_space=pl.ANY),
        ],
        out_specs=pl.BlockSpec((TQ, ATT_WIDTH), lambda b, t: (q0 + b * nqb + t, 0)),
        out_shape=jax.ShapeDtypeStruct((N_TOK, ATT_WIDTH), BF),
        input_output_aliases={7: 0},
        compiler_params=_params(("arbitrary", "arbitrary")),
        name=f"attn_lat_l{layer}",
    )(q, kt, v, cache_kt, cache_v, lambda_qk, subln_g, att)


def _out_kernel(att_ref, conv_ref, w_ref, xa_ref, xb_ref, g1_ref, sh_ref, sc_ref, gf_ref,
                xo_ref, h2_ref, wb_s):
    i = pl.program_id(0)

    @pl.when(i == 0)
    def _():
        wb_s[...] = w_ref[...].astype(BF)

    mo = jnp.dot(att_ref[...], wb_s[:ATT_WIDTH, :], preferred_element_type=F32)
    mo = mo + jnp.dot(conv_ref[...], wb_s[ATT_WIDTH:, :], preferred_element_type=F32)
    xn = _stream_rows(xa_ref, xb_ref, i) + g1_ref[...] * mo
    xo_ref[...] = xn
    h2 = (_rms(xn) * gf_ref[...]) * (1.0 + sc_ref[...]) + sh_ref[...]
    h2_ref[...] = h2.astype(BF)


def _out_proj(layer, att, conv, w_out, xs, mod5, g_ffn):
    pair = isinstance(xs, tuple)
    xa, xb = xs if pair else (xs, xs)
    spec_a, spec_b = _stream_specs(pair)
    row_spec = pl.BlockSpec((TM, D_MODEL), lambda i: (i, 0))
    out_specs = [row_spec, row_spec]
    out_shape = [jax.ShapeDtypeStruct((N_TOK, D_MODEL), F32), jax.ShapeDtypeStruct((N_TOK, D_MODEL), BF)]
    return pl.pallas_call(
        _out_kernel,
        grid=(N_TILES,),
        in_specs=[
            pl.BlockSpec((TM, ATT_WIDTH), lambda i: (i, 0)),
            pl.BlockSpec((TM, CONV_WIDTH), lambda i: (i, 0)),
            pl.BlockSpec((None, D_MODEL, D_MODEL), lambda i: (layer, 0, 0)),
            spec_a, spec_b,
            _mod_spec(layer, 2), _mod_spec(layer, 3), _mod_spec(layer, 4),
            pl.BlockSpec((None, 1, D_MODEL), lambda i: (layer, 0, 0)),
        ],
        out_specs=out_specs,
        out_shape=out_shape,
        scratch_shapes=[pltpu.VMEM((D_MODEL, D_MODEL), BF)],
        compiler_params=_params(("arbitrary",)),
        name=f"out_proj_l{layer}",
    )(att, conv, w_out, xa, xb, mod5, mod5, mod5, g_ffn)


def _gu_kernel(h_ref, wg_ref, wu_ref, o_ref, wb_s):
    i = pl.program_id(0)
    j = pl.program_id(1)

    @pl.when(i == 0)
    def _():
        wb_s[j] = wg_ref[...].astype(BF)
        wb_s[j + N_FF_TILES] = wu_ref[...].astype(BF)

    h = h_ref[...]
    g = jnp.dot(h, wb_s[j], preferred_element_type=F32)
    u = jnp.dot(h, wb_s[j + N_FF_TILES], preferred_element_type=F32)
    o_ref[...] = (_silu(g) * u).astype(BF)


def _dense_gu(h2, w_gu):
    nj = N_FF_TILES
    return pl.pallas_call(
        _gu_kernel,
        grid=(N_TILES, nj),
        in_specs=[
            pl.BlockSpec((TM, D_MODEL), lambda i, j: (i, 0)),
            pl.BlockSpec((None, D_MODEL, TN_FF), lambda i, j: (0, 0, jnp.where(i == 0, j, nj - 1)),
                         pipeline_mode=pl.Buffered(1)),
            pl.BlockSpec((None, D_MODEL, TN_FF), lambda i, j: (0, 0, jnp.where(i == 0, j, nj - 1) + nj),
                         pipeline_mode=pl.Buffered(1)),
        ],
        out_specs=pl.BlockSpec((TM, TN_FF), lambda i, j: (i, j)),
        out_shape=jax.ShapeDtypeStruct((N_TOK, D_FF), BF),
        scratch_shapes=[pltpu.VMEM((2 * nj, D_MODEL, TN_FF), BF)],
        compiler_params=_params(("arbitrary", "arbitrary")),
        name="dense_gu",
    )(h2, w_gu, w_gu)


def _down_kernel(a_ref, w_ref, x_ref, g2_ref, o_ref, wb_s):
    j = pl.program_id(1)

    @pl.when(pl.program_id(0) == 0)
    def _():
        wb_s[j] = w_ref[...].astype(BF)

    y = jnp.dot(a_ref[...], wb_s[j], preferred_element_type=F32)
    o_ref[...] = x_ref[...] + g2_ref[...] * y


def _dense_down(layer, act, w_down, x, mod5):
    nj = N_DOWN_TILES
    return pl.pallas_call(
        _down_kernel,
        grid=(N_TILES, nj),
        in_specs=[
            pl.BlockSpec((TM, D_FF), lambda i, j: (i, 0)),
            pl.BlockSpec((None, D_FF, TN_DOWN), lambda i, j: (0, 0, jnp.where(i == 0, j, nj - 1))),
            pl.BlockSpec((TM, TN_DOWN), lambda i, j: (i, j)),
            pl.BlockSpec((None, None, None, 1, TN_DOWN), lambda i, j: (layer, _mod_row(i), 5, 0, j)),
        ],
        out_specs=pl.BlockSpec((TM, TN_DOWN), lambda i, j: (i, j)),
        out_shape=jax.ShapeDtypeStruct((N_TOK, D_MODEL), F32),
        scratch_shapes=[pltpu.VMEM((nj, D_FF, TN_DOWN), BF)],
        compiler_params=_params(("arbitrary", "arbitrary")),
        name="dense_down",
    )(act, w_down, x, mod5)


def _router_kernel(h_ref, wr_ref, lp_ref, rg_ref, n_ref):
    logits = jnp.dot(h_ref[...], wr_ref[...].astype(BF), preferred_element_type=F32)
    lane = lax.broadcasted_iota(jnp.int32, logits.shape, 1)
    lg = jnp.where(lane < N_EXPERTS, logits, -jnp.inf)
    m1 = jnp.max(lg, axis=-1, keepdims=True)
    i1 = jnp.min(jnp.where(lg == m1, lane, LANES), axis=-1, keepdims=True)
    lg2 = jnp.where(lane == i1, -jnp.inf, lg)
    m2 = jnp.max(lg2, axis=-1, keepdims=True)
    i2 = jnp.min(jnp.where(lg2 == m2, lane, LANES), axis=-1, keepdims=True)
    e2 = jnp.exp(m2 - m1)
    w1 = 1.0 / (1.0 + e2)
    w2 = e2 / (1.0 + e2)

    sel1 = lane == i1
    sel2 = lane == i2
    onehot = jnp.logical_or(sel1, sel2)
    rows = lax.broadcasted_iota(jnp.int32, (TM, TM), 0)
    colsi = lax.broadcasted_iota(jnp.int32, (TM, TM), 1)
    earlier = jnp.logical_and(colsi < rows, (colsi // ST) == (rows // ST))
    before = jnp.dot(earlier.astype(BF), onehot.astype(BF), preferred_element_type=F32)
    onehot_f = onehot.astype(F32)
    counts = [jnp.sum(onehot_f[s * ST:(s + 1) * ST], axis=0, keepdims=True) for s in range(SUB_PER_TILE)]
    counts = jnp.concatenate(counts + [jnp.zeros((8 - SUB_PER_TILE, LANES), F32)], axis=0).astype(jnp.int32)
    seg_len = ((counts + (SEG_ALIGN - 1)) // SEG_ALIGN) * SEG_ALIGN
    n_ref[...] = seg_len
    la = lax.broadcasted_iota(jnp.int32, (LANES, LANES), 0)
    lb = lax.broadcasted_iota(jnp.int32, (LANES, LANES), 1)
    seg_start = jnp.dot(seg_len.astype(F32).astype(BF), (la < lb).astype(BF), preferred_element_type=F32)
    start = jnp.concatenate(
        [jnp.broadcast_to(seg_start[s:s + 1], (ST, LANES)) for s in range(SUB_PER_TILE)], axis=0)
    where = before + start
    lp1 = jnp.sum(jnp.where(sel1, where, 0.0), axis=-1, keepdims=True).astype(jnp.int32)
    lp2 = jnp.sum(jnp.where(sel2, where, 0.0), axis=-1, keepdims=True).astype(jnp.int32)
    lp_ref[...] = jnp.where(lane == 0, lp1, jnp.where(lane == 1, lp2, 0))
    rg_ref[...] = jnp.where(lane == 0, w1, jnp.where(lane == 1, w2, 0.0))


def _router(h2, w_router_pad):
    return pl.pallas_call(
        _router_kernel,
        grid=(N_TILES,),
        in_specs=[
            pl.BlockSpec((TM, D_MODEL), lambda i: (i, 0)),
            pl.BlockSpec((D_MODEL, LANES), lambda i: (0, 0)),
        ],
        out_specs=[
            pl.BlockSpec((TM, LANES), lambda i: (i, 0)),
            pl.BlockSpec((TM, LANES), lambda i: (i, 0)),
            pl.BlockSpec((None, 8, LANES), lambda i: (i, 0, 0)),
        ],
        out_shape=[
            jax.ShapeDtypeStruct((N_TOK, LANES), jnp.int32),
            jax.ShapeDtypeStruct((N_TOK, LANES), F32),
            jax.ShapeDtypeStruct((N_TILES, 8, LANES), jnp.int32),
        ],
        compiler_params=_params(("arbitrary",)),
        name="router",
    )(h2, w_router_pad)


def _chunk_copies(s, cnt_ref, cdst_ref, stage, rows_hbm, sem, *, to_hbm, wait):
    def one(c, carry):
        v = stage.at[pl.ds(pl.multiple_of(c * SEG_ALIGN, SEG_ALIGN), SEG_ALIGN)]
        h = rows_hbm.at[pl.ds(pl.multiple_of(cdst_ref[s * STAGE_CHUNKS + c], SEG_ALIGN), SEG_ALIGN)]
        cp = pltpu.make_async_copy(v, h, sem) if to_hbm else pltpu.make_async_copy(h, v, sem)
        if wait:
            cp.wait()
        else:
            cp.start()
        return carry

    lax.fori_loop(0, cnt_ref[s], one, 0)


def _dispatch_kernel(cnt_ref, cdst_ref, h_ref, lp_ref, xs_ref, stage_s, sem):
    s = pl.program_id(0)
    slot = s % 2
    copies = functools.partial(_chunk_copies, cnt_ref=cnt_ref, cdst_ref=cdst_ref, rows_hbm=xs_ref, to_hbm=True)

    @pl.when(s >= 2)
    def _():
        copies(s - 2, stage=stage_s.at[slot], sem=sem.at[slot], wait=True)

    lpt = lp_ref[...].T
    r = lax.broadcasted_iota(jnp.int32, (STAGE_ROWS, ST), 0)
    pick = jnp.logical_or(r == lpt[0:1, :], r == lpt[1:2, :]).astype(BF)
    stage_s[slot] = jnp.dot(pick, h_ref[...], preferred_element_type=F32).astype(BF)
    copies(s, stage=stage_s.at[slot], sem=sem.at[slot], wait=False)

    @pl.when(s == N_SUB - 1)
    def _():
        copies(s - 1, stage=stage_s.at[1 - slot], sem=sem.at[1 - slot], wait=True)
        copies(s, stage=stage_s.at[slot], sem=sem.at[slot], wait=True)


def _dispatch(cnt, cdst, h2, lp):
    grid_spec = pltpu.PrefetchScalarGridSpec(
        num_scalar_prefetch=2,
        grid=(N_SUB,),
        in_specs=[
            pl.BlockSpec((ST, D_MODEL), lambda s, *_: (s, 0)),
            pl.BlockSpec((ST, LANES), lambda s, *_: (s, 0)),
        ],
        out_specs=pl.BlockSpec(memory_space=pl.ANY),
        scratch_shapes=[pltpu.VMEM((2, STAGE_ROWS, D_MODEL), BF), pltpu.SemaphoreType.DMA((2,))],
    )
    return pl.pallas_call(
        _dispatch_kernel,
        grid_spec=grid_spec,
        out_shape=jax.ShapeDtypeStruct((R_PAD, D_MODEL), BF),
        compiler_params=_params(("arbitrary",)),
        name="moe_dispatch",
    )(cnt, cdst, h2, lp)


def _expert_weights(te_ref, nt_ref, nxt_ref, w_hbm, wf_s, wb_s, sem):
    r = pl.program_id(0)

    def fetch(e):
        return pltpu.make_async_copy(w_hbm.at[0, e], wf_s, sem)

    @pl.when(r == 0)
    def _():
        fetch(te_ref[0]).start()

    first = jnp.logical_or(r == 0, te_ref[r] != te_ref[jnp.maximum(r - 1, 0)])

    @pl.when(jnp.logical_and(r < nt_ref[0], first))
    def _():
        fetch(te_ref[r]).wait()
        wb_s[...] = wf_s[...].astype(BF)

        @pl.when(nxt_ref[r] >= 0)
        def _():
            fetch(nxt_ref[r]).start()


def _moe_ffn_kernel(te_ref, nt_ref, nxt_ref, x_ref, wgu_hbm, wd_hbm, o_ref,
                    wgu_f, wgu_b, wd_f, wd_b, sem):
    _expert_weights(te_ref, nt_ref, nxt_ref, wgu_hbm, wgu_f, wgu_b, sem.at[0])
    _expert_weights(te_ref, nt_ref, nxt_ref, wd_hbm, wd_f, wd_b, sem.at[1])

    @pl.when(pl.program_id(0) < nt_ref[0])
    def _():
        x = x_ref[...]
        g = jnp.dot(x, wgu_b[:, :D_FF_EXPERT], preferred_element_type=F32)
        u = jnp.dot(x, wgu_b[:, D_FF_EXPERT:], preferred_element_type=F32)
        act = (_silu(g) * u).astype(BF)
        o_ref[...] = jnp.dot(act, wd_b[...], preferred_element_type=F32).astype(BF)


def _moe_ffn(te, nt, nxt, rows, w_gu, w_down):
    tile_map = lambda r, te, nt, nxt: (jnp.minimum(r, nt[0] - 1), 0)
    grid_spec = pltpu.PrefetchScalarGridSpec(
        num_scalar_prefetch=3,
        grid=(NT_G,),
        in_specs=[pl.BlockSpec((TG, D_MODEL), tile_map),
                  pl.BlockSpec(memory_space=pl.ANY), pl.BlockSpec(memory_space=pl.ANY)],
        out_specs=pl.BlockSpec((TG, D_MODEL), tile_map),
        scratch_shapes=[
            pltpu.VMEM((D_MODEL, 2 * D_FF_EXPERT), F32), pltpu.VMEM((D_MODEL, 2 * D_FF_EXPERT), BF),
            pltpu.VMEM((D_FF_EXPERT, D_MODEL), F32), pltpu.VMEM((D_FF_EXPERT, D_MODEL), BF),
            pltpu.SemaphoreType.DMA((2,)),
        ],
    )
    return pl.pallas_call(
        _moe_ffn_kernel,
        grid_spec=grid_spec,
        out_shape=jax.ShapeDtypeStruct((R_PAD, D_MODEL), BF),
        compiler_params=_params(("arbitrary",)),
        name="moe_ffn",
    )(te, nt, nxt, rows, w_gu, w_down)


def _combine_kernel(cnt_ref, cdst_ref, ys_ref, lp_ref, rg_ref, x_ref, g2_ref, fg_ref,
                    oa_ref, ob_ref, stage_s, sem):
    s = pl.program_id(0)
    slot = s % 2
    copies = functools.partial(_chunk_copies, cnt_ref=cnt_ref, cdst_ref=cdst_ref, rows_hbm=ys_ref, to_hbm=False)

    @pl.when(s == 0)
    def _():
        stage_s[...] = jnp.zeros_like(stage_s)
        copies(s, stage=stage_s.at[slot], sem=sem.at[slot], wait=False)

    @pl.when(s + 1 < N_SUB)
    def _():
        copies(s + 1, stage=stage_s.at[1 - slot], sem=sem.at[1 - slot], wait=False)

    copies(s, stage=stage_s.at[slot], sem=sem.at[slot], wait=True)

    lp = lp_ref[...]
    r = lax.broadcasted_iota(jnp.int32, (ST, STAGE_ROWS), 1)
    rows = stage_s[slot]
    a = jnp.dot((r == lp[:, 0:1]).astype(BF), rows, preferred_element_type=F32)
    b = jnp.dot((r == lp[:, 1:2]).astype(BF), rows, preferred_element_type=F32)
    rg = rg_ref[...]
    y = rg[:, 0:1] * a + rg[:, 1:2] * b
    xn = x_ref[...] + g2_ref[...] * y
    out = _rms(xn) * fg_ref[...]

    @pl.when(s < CTX_SUB)
    def _():
        oa_ref[...] = out

    @pl.when(s >= CTX_SUB)
    def _():
        ob_ref[...] = out


def _combine(layer, cnt, cdst, ys, lp, rg, x, mod5, final_g):
    def mod_map(s, *_):
        return (layer, _mod_row(s // SUB_PER_TILE), 5, 0, 0)

    grid_spec = pltpu.PrefetchScalarGridSpec(
        num_scalar_prefetch=2,
        grid=(N_SUB,),
        in_specs=[
            pl.BlockSpec(memory_space=pl.ANY),
            pl.BlockSpec((ST, LANES), lambda s, *_: (s, 0)),
            pl.BlockSpec((ST, LANES), lambda s, *_: (s, 0)),
            pl.BlockSpec((ST, D_MODEL), lambda s, *_: (s, 0)),
            pl.BlockSpec((None, None, None, 1, D_MODEL), mod_map),
            pl.BlockSpec((1, D_MODEL), lambda s, *_: (0, 0)),
        ],
        out_specs=[
            pl.BlockSpec((ST, D_MODEL), lambda s, *_: (jnp.minimum(s, CTX_SUB - 1), 0)),
            pl.BlockSpec((ST, D_MODEL), lambda s, *_: (jnp.maximum(s - CTX_SUB, 0), 0)),
        ],
        scratch_shapes=[pltpu.VMEM((2, STAGE_ROWS, D_MODEL), BF), pltpu.SemaphoreType.DMA((2,))],
    )
    return pl.pallas_call(
        _combine_kernel,
        grid_spec=grid_spec,
        out_shape=[
            jax.ShapeDtypeStruct((N_CTX, D_MODEL), F32),
            jax.ShapeDtypeStruct((N_LAT, D_MODEL), F32),
        ],
        compiler_params=_params(("arbitrary",)),
        name="moe_combine",
    )(cnt, cdst, ys, lp, rg, x, mod5, final_g)


def _group_layout(n_tiles):
    n = n_tiles[:, :SUB_PER_TILE, :N_EXPERTS].reshape(N_SUB, N_EXPERTS)
    tiles = (jnp.sum(n, axis=0) + TG - 1) // TG
    tile_end = jnp.cumsum(tiles)
    region = (tile_end - tiles) * TG
    dst = region[None, :] + jnp.cumsum(n, axis=0) - n
    seg_end = jnp.cumsum(n, axis=1)
    seg = seg_end - n
    row = jnp.arange(STAGE_CHUNKS, dtype=jnp.int32) * SEG_ALIGN
    owner = jnp.sum((row[None, :, None] >= seg_end[:, None, :]).astype(jnp.int32), axis=-1)
    own = jnp.minimum(owner, N_EXPERTS - 1)[..., None] == jnp.arange(N_EXPERTS)
    cdst = jnp.sum(jnp.where(own, (dst - seg)[:, None, :], 0), axis=-1) + row[None, :]
    cnt = seg_end[:, -1] // SEG_ALIGN
    nt = tile_end[-1]
    tile_id = jnp.minimum(jnp.arange(NT_G, dtype=jnp.int32), nt - 1)
    te = jnp.sum((tile_id[:, None] >= tile_end[None, :]).astype(jnp.int32), axis=-1)
    after = jnp.sum(jnp.where(te[:, None] == jnp.arange(N_EXPERTS), tile_end[None, :], 0), axis=-1)
    nxt = jnp.where(after < nt, jnp.sum((after[:, None] >= tile_end[None, :]).astype(jnp.int32), axis=-1), -1)
    i32 = lambda a: a.astype(jnp.int32)
    return i32(cnt), i32(cdst.reshape(N_SUB * STAGE_CHUNKS)), i32(te), i32(nt.reshape(1)), i32(nxt)


def _rope_tables():
    p = np.arange(DEC_SEQ)
    row = (p // GRID_W).astype(np.float32)
    col = (p % GRID_W).astype(np.float32)
    half = QK_DIM // 4
    freqs = (ROPE_BASE ** (-np.arange(half, dtype=np.float32) / half)).astype(np.float32)
    lane = np.arange(V_DIM)
    f = freqs[lane & (half - 1)]
    use_col = (lane & (2 * half)) != 0
    ang = (np.where(use_col[None, :], col[:, None], row[:, None]) * f[None, :]).astype(np.float32)
    upper = (lane & half) != 0
    sin = np.sin(ang)
    return jnp.asarray(np.cos(ang), F32), jnp.asarray(np.where(upper[None, :], sin, -sin), F32)


def kernel(x_prompt, x_sample, cache_k, cache_v, c, c_ctx, w_ada, b_ada, norm_mix_g, norm_ffn_g,
           w_in, lambda_qk, subln_g, conv_w, w_out, w_gu_dense, w_down_dense, w_router,
           w_gu_moe, w_down_moe, final_g):
    assert DEPTH == 2
    xs = (x_prompt.reshape(N_CTX, D_MODEL), x_sample.reshape(N_LAT, D_MODEL))
    cond = jnp.concatenate([c_ctx[None, :], c, jnp.zeros((COND_ROWS - 1 - DEC_BATCH, D_MODEL), F32)], axis=0)
    mod5 = _ada(cond, w_ada, b_ada).reshape(DEPTH, COND_ROWS, N_MOD, 1, D_MODEL)
    cos_t, sin_t = _rope_tables()
    cache_kt = jnp.transpose(cache_k, (0, 1, 3, 4, 5, 2)).reshape(DEC_BATCH, DEPTH, ATT_WIDTH, PAST_LEN)
    cache_v4 = cache_v.reshape(DEC_BATCH, DEPTH, PAST_LEN * N_HEADS, V_DIM)
    g_mix = norm_mix_g.reshape(DEPTH, 1, D_MODEL)
    g_ffn = norm_ffn_g.reshape(DEPTH, 1, D_MODEL)
    sg = subln_g.reshape(DEPTH, 1, V_DIM)

    new_kv = None
    for layer in range(DEPTH):
        lam_init = 0.8 - 0.6 * math.exp(-0.3 * layer)
        q, v, kt, conv, nk, nv = _in_proj(layer, xs, mod5, g_mix, w_in, cos_t, sin_t, conv_w, new_kv)
        new_kv = (nk, nv)
        att = _attn_ctx(layer, q, kt, v, lambda_qk, sg, lam_init)
        att = _attn_lat(layer, q, kt, v, cache_kt, cache_v4, lambda_qk, sg, att, lam_init)
        x1, h2 = _out_proj(layer, att, conv, w_out, xs, mod5, g_ffn)
        if layer == 0:
            act = _dense_gu(h2, w_gu_dense)
            xs = _dense_down(layer, act, w_down_dense, x1, mod5)
        else:
            wr = jnp.pad(w_router[0], ((0, 0), (0, LANES - N_EXPERTS)))
            lp, rg, n_tiles = _router(h2, wr)
            cnt, cdst, te, nt, nxt = _group_layout(n_tiles)
            xsort = _dispatch(cnt, cdst, h2, lp)
            ys = _moe_ffn(te, nt, nxt, xsort, w_gu_moe, w_down_moe)
            y_ctx, y_lat = _combine(layer, cnt, cdst, ys, lp, rg, x1, mod5, final_g.reshape(1, D_MODEL))
    nk, nv = new_kv
    new_k = jnp.transpose(nk.reshape(BATCH, DEPTH, N_HEADS, 2, QK_DIM, SEQ), (0, 1, 5, 2, 3, 4))
    new_v = nv.reshape(BATCH, DEPTH, SEQ, N_HEADS, V_DIM)
    return (y_ctx.reshape(BATCH, SEQ, D_MODEL), y_lat.reshape(DEC_BATCH, DEC_SEQ, D_MODEL), new_k, new_v)
```

```python
import functools
import math

import numpy as np
import jax
import jax.numpy as jnp
from jax import lax
from jax.experimental import pallas as pl
from jax.experimental.pallas import tpu as pltpu

D_MODEL = 1024
BATCH = 16
SEQ = 256
DEPTH = 2
DEC_BATCH = 4
DEC_SEQ = 1024
PAST_LEN = 512
GRID_W = 64
ATT_WIDTH = 512
CONV_WIDTH = 512
N_HEADS = 4
V_DIM = 128
QK_DIM = 64
ROPE_BASE = 10000.0
D_FF = 2816
N_EXPERTS = 8
D_FF_EXPERT = 1408
N_MOD = 6
NORM_EPS = 1e-6
Q_SCALE = QK_DIM ** -0.5 * math.log2(math.e)
IN_COLS = 3 * ATT_WIDTH + 3 * CONV_WIDTH

N_CTX = BATCH * SEQ
N_LAT = DEC_BATCH * DEC_SEQ
N_TOK = N_CTX + N_LAT
TM = 1024
N_TILES = N_TOK // TM
CTX_TILES = N_CTX // TM
SEQ_PER_TILE = TM // SEQ
COND_ROWS = 8
TN_IN = 512
N_IN_TILES = IN_COLS // TN_IN
ROW_CHUNK = 512
TN_FF = 1408
N_FF_TILES = D_FF // TN_FF
TN_DOWN = 512
N_DOWN_TILES = D_MODEL // TN_DOWN
TN_ADA = 1536
TG = 512
ST = 256
SUB_PER_TILE = TM // ST
N_SUB = N_TOK // ST
CTX_SUB = N_CTX // ST
SEG_ALIGN = 16
STAGE_ROWS = 640
STAGE_CHUNKS = STAGE_ROWS // SEG_ALIGN
WAIT_PIECES = (32, 16, 8, 4, 2, 1)
NT_G = -(-(2 * N_TOK + N_SUB * N_EXPERTS * (SEG_ALIGN - 1) + N_EXPERTS * (TG - SEG_ALIGN)) // TG)
R_PAD = NT_G * TG
LANES = 128
VMEM_LIMIT = 60 * 1024 * 1024

BF = jnp.bfloat16
F32 = jnp.float32


def _params(sem, vmem=VMEM_LIMIT):
    return pltpu.CompilerParams(dimension_semantics=sem, vmem_limit_bytes=vmem)


def _mod_row(i):
    return jnp.where(i < CTX_TILES, 0, i - (CTX_TILES - 1))


def _mod_spec(layer, c, width=D_MODEL):
    return pl.BlockSpec((None, None, None, 1, width), lambda i, *_: (layer, _mod_row(i), c, 0, 0))


def _stream_specs(pair, width=D_MODEL):
    a = pl.BlockSpec((TM, width), lambda i, *_: (jnp.minimum(i, CTX_TILES - 1), 0))
    if pair:
        b = pl.BlockSpec((TM, width), lambda i, *_: (jnp.maximum(i - CTX_TILES, 0), 0))
    else:
        b = pl.BlockSpec((TM, width), lambda i, *_: (jnp.maximum(i, CTX_TILES), 0))
    return a, b


def _stream_rows(xa_ref, xb_ref, i):
    return jnp.where(i >= CTX_TILES, xb_ref[...], xa_ref[...])


def _silu(x):
    return x / (1.0 + jnp.exp(-x))


def _rms(x):
    return x * lax.rsqrt(jnp.mean(x * x, axis=-1, keepdims=True) + NORM_EPS)


def _ada_kernel(c_ref, w_ref, b_ref, o_ref):
    s = _silu(c_ref[...]).astype(BF)
    o_ref[...] = jnp.dot(s, w_ref[...].astype(BF), preferred_element_type=F32) + b_ref[...]


def _ada(cond, w_ada, b_ada):
    n = N_MOD * D_MODEL
    return pl.pallas_call(
        _ada_kernel,
        grid=(DEPTH, n // TN_ADA),
        in_specs=[
            pl.BlockSpec((COND_ROWS, D_MODEL), lambda l, j: (0, 0)),
            pl.BlockSpec((None, D_MODEL, TN_ADA), lambda l, j: (l, 0, j)),
            pl.BlockSpec((None, 1, TN_ADA), lambda l, j: (l, 0, j)),
        ],
        out_specs=pl.BlockSpec((None, COND_ROWS, TN_ADA), lambda l, j: (l, 0, j)),
        out_shape=jax.ShapeDtypeStruct((DEPTH, COND_ROWS, n), F32),
        compiler_params=_params(("arbitrary", "arbitrary")),
        name="ada_mod",
    )(cond, w_ada, b_ada.reshape(DEPTH, 1, n))


def _in_kernel(*refs, aliased):
    xa_ref, xb_ref, sh_ref, sc_ref, g_ref, w_ref, cos_ref, sin_ref, cw_ref = refs[:9]
    refs = refs[11:] if aliased else refs[9:]
    q_ref, v_ref, kt_ref, conv_ref, nk_ref, nv_ref, h_s, wb_s, gb_s, gc_s = refs
    i = pl.program_id(0)
    j = pl.program_id(1)
    lat = i >= CTX_TILES
    ctx = jnp.logical_not(lat)

    @pl.when(i == 0)
    def _():
        wb_s[j] = w_ref[...].astype(BF)

    chunks = [slice(c * ROW_CHUNK, (c + 1) * ROW_CHUNK) for c in range(TM // ROW_CHUNK)]
    seqs_per_chunk = ROW_CHUNK // SEQ

    def norm(rows):
        gain = g_ref[...] * (1.0 + sc_ref[...])
        x = jnp.where(lat, xb_ref[rows, :], xa_ref[rows, :])
        h_s[rows, :] = (_rms(x) * gain + sh_ref[...]).astype(BF)

    def proj(rows):
        return jnp.dot(h_s[rows, :], wb_s[j], preferred_element_type=F32)

    def roped(a, rows):
        cos = jnp.concatenate([cos_ref[rows, :]] * N_HEADS, axis=1)
        sin = jnp.concatenate([sin_ref[rows, :]] * N_HEADS, axis=1)
        lane = lax.broadcasted_iota(jnp.int32, a.shape, 1)
        upper = (lane & (QK_DIM // 4)) != 0
        partner = jnp.where(upper, pltpu.roll(a, QK_DIM // 4, 1), pltpu.roll(a, TN_IN - QK_DIM // 4, 1))
        return a * cos + partner * sin

    @pl.when(jnp.logical_and(j == 0, lat))
    def _():
        for rows in chunks:
            norm(rows)
            q_ref[rows, :] = (roped(proj(rows), rows) * Q_SCALE).astype(BF)

    @pl.when(jnp.logical_and(j == 0, ctx))
    def _():
        for rows in chunks:
            norm(rows)
            q_ref[rows, :] = (proj(rows) * Q_SCALE).astype(BF)

    @pl.when(jnp.logical_and(j == 1, lat))
    def _():
        for rows in chunks:
            kt_ref[:, rows] = roped(proj(rows), rows).T.astype(BF)

    @pl.when(jnp.logical_and(j == 1, ctx))
    def _():
        for c, rows in enumerate(chunks):
            kt = proj(rows).T
            kt_ref[:, rows] = kt.astype(BF)
            for s in range(seqs_per_chunk):
                nk_ref[c * seqs_per_chunk + s] = kt[:, s * SEQ:(s + 1) * SEQ]

    @pl.when(jnp.logical_and(j == 2, lat))
    def _():
        for rows in chunks:
            v_ref[rows, :] = proj(rows).astype(BF)

    @pl.when(jnp.logical_and(j == 2, ctx))
    def _():
        for c, rows in enumerate(chunks):
            acc = proj(rows)
            v_ref[rows, :] = acc.astype(BF)
            for s in range(seqs_per_chunk):
                for h in range(N_HEADS):
                    nv_ref[c * seqs_per_chunk + s, pl.ds(h, SEQ, stride=N_HEADS), :] = (
                        acc[s * SEQ:(s + 1) * SEQ, h * V_DIM:(h + 1) * V_DIM])

    @pl.when(j == 3)
    def _():
        for rows in chunks:
            gb_s[rows, :] = proj(rows)

    @pl.when(j == 4)
    def _():
        for rows in chunks:
            gc_s[rows, :] = proj(rows)

    @pl.when(j == 5)
    def _():
        for rows in chunks:
            gc_s[rows, :] = gc_s[rows, :] * proj(rows)
        u = gc_s[...]
        seq = jnp.where(lat, DEC_SEQ, SEQ)
        pos = lax.broadcasted_iota(jnp.int32, (TM, 1), 0) & (seq - 1)
        prev = jnp.where(pos == 0, 0.0, pltpu.roll(u, 1, 0))
        nxt = jnp.where(pos == seq - 1, 0.0, pltpu.roll(u, TM - 1, 0))
        cw = cw_ref[...]
        conv = prev * cw[0:1] + u * cw[1:2] + nxt * cw[2:3]
        conv_ref[...] = (gb_s[...] * conv).astype(BF)


def _in_proj(layer, xs, mod5, g_mix, w_in, cos_t, sin_t, conv_w, new_kv):
    pair = isinstance(xs, tuple)
    xa, xb = xs if pair else (xs, xs)
    spec_a, spec_b = _stream_specs(pair)
    ctx_i = lambda i: jnp.minimum(i, CTX_TILES - 1)
    in_specs = [
        spec_a, spec_b,
        _mod_spec(layer, 0), _mod_spec(layer, 1),
        pl.BlockSpec((None, 1, D_MODEL), lambda i, j: (layer, 0, 0)),
        pl.BlockSpec((None, D_MODEL, TN_IN), lambda i, j: (layer, 0, jnp.where(i == 0, j, N_IN_TILES - 1))),
        pl.BlockSpec((DEC_SEQ, V_DIM), lambda i, j: (0, 0)),
        pl.BlockSpec((DEC_SEQ, V_DIM), lambda i, j: (0, 0)),
        pl.BlockSpec((None, 3, CONV_WIDTH), lambda i, j: (layer, 0, 0)),
    ]
    args = [xa, xb, mod5, mod5, g_mix, w_in, cos_t, sin_t, conv_w]
    aliases = {}
    if new_kv is not None:
        in_specs += [pl.BlockSpec(memory_space=pl.ANY), pl.BlockSpec(memory_space=pl.ANY)]
        args += list(new_kv)
        aliases = {9: 4, 10: 5}
    row_tile = pl.BlockSpec((TM, TN_IN), lambda i, j: (i, 0))
    return pl.pallas_call(
        functools.partial(_in_kernel, aliased=new_kv is not None),
        grid=(N_TILES, N_IN_TILES),
        in_specs=in_specs,
        out_specs=[
            row_tile,
            row_tile,
            pl.BlockSpec((None, ATT_WIDTH, TM), lambda i, j: (i, 0, 0)),
            row_tile,
            pl.BlockSpec((SEQ_PER_TILE, None, ATT_WIDTH, SEQ), lambda i, j: (ctx_i(i), layer, 0, 0)),
            pl.BlockSpec((SEQ_PER_TILE, None, SEQ * N_HEADS, V_DIM), lambda i, j: (ctx_i(i), layer, 0, 0)),
        ],
        out_shape=[
            jax.ShapeDtypeStruct((N_TOK, ATT_WIDTH), BF),
            jax.ShapeDtypeStruct((N_TOK, ATT_WIDTH), BF),
            jax.ShapeDtypeStruct((N_TILES, ATT_WIDTH, TM), BF),
            jax.ShapeDtypeStruct((N_TOK, CONV_WIDTH), BF),
            jax.ShapeDtypeStruct((BATCH, DEPTH, ATT_WIDTH, SEQ), F32),
            jax.ShapeDtypeStruct((BATCH, DEPTH, SEQ * N_HEADS, V_DIM), F32),
        ],
        scratch_shapes=[
            pltpu.VMEM((TM, D_MODEL), BF),
            pltpu.VMEM((N_IN_TILES, D_MODEL, TN_IN), BF),
            pltpu.VMEM((TM, CONV_WIDTH), F32),
            pltpu.VMEM((TM, CONV_WIDTH), F32),
        ],
        input_output_aliases=aliases,
        compiler_params=_params(("arbitrary", "arbitrary")),
        name=f"in_proj_l{layer}",
    )(*args)


def _lambda(lq_ref, lam_init):
    lq = lq_ref[...]
    a = jnp.exp(jnp.sum(lq[0:1] * lq[1:2], axis=-1, keepdims=True))
    b = jnp.exp(jnp.sum(lq[2:3] * lq[3:4], axis=-1, keepdims=True))
    return a - b + lam_init


def _head_norm(o, sg, lam_init):
    return _rms(o) * sg * (1.0 - lam_init)


def _attn_ctx_kernel(q_ref, kt_ref, v_ref, lq_ref, sg_ref, o_ref, sc_s, *, lam_init):
    lam = _lambda(lq_ref, lam_init)
    sg = sg_ref[...]

    def scores(h):
        for s in range(2):
            d = slice(h * V_DIM + s * QK_DIM, h * V_DIM + (s + 1) * QK_DIM)
            sc_s[h % 2, s] = jnp.dot(q_ref[:, d], kt_ref[d, :], preferred_element_type=F32)

    def finish(h):
        cols = slice(h * V_DIM, (h + 1) * V_DIM)
        v = v_ref[:, cols]
        outs = []
        for s in range(2):
            sc = sc_s[h % 2, s]
            e = jnp.exp2(sc - jnp.max(sc, axis=-1, keepdims=True))
            r = 1.0 / jnp.sum(e, axis=-1, keepdims=True)
            outs.append(jnp.dot(e.astype(BF), v, preferred_element_type=F32) * r)
        o = outs[0] - lam * outs[1]
        o_ref[:, cols] = _head_norm(o, sg, lam_init).astype(BF)

    scores(0)
    for h in range(N_HEADS):
        if h + 1 < N_HEADS:
            scores(h + 1)
        finish(h)


def _attn_ctx(layer, q, kt, v, lambda_qk, subln_g, lam_init):
    return pl.pallas_call(
        functools.partial(_attn_ctx_kernel, lam_init=lam_init),
        grid=(BATCH,),
        in_specs=[
            pl.BlockSpec((SEQ, ATT_WIDTH), lambda b: (b, 0)),
            pl.BlockSpec((None, ATT_WIDTH, SEQ), lambda b: (b // SEQ_PER_TILE, 0, b % SEQ_PER_TILE)),
            pl.BlockSpec((SEQ, ATT_WIDTH), lambda b: (b, 0)),
            pl.BlockSpec((None, 4, QK_DIM), lambda b: (layer, 0, 0)),
            pl.BlockSpec((None, 1, V_DIM), lambda b: (layer, 0, 0)),
        ],
        out_specs=pl.BlockSpec((SEQ, ATT_WIDTH), lambda b: (b, 0)),
        out_shape=jax.ShapeDtypeStruct((N_TOK, ATT_WIDTH), BF),
        scratch_shapes=[pltpu.VMEM((2, 2, SEQ, SEQ), F32)],
        compiler_params=_params(("arbitrary",)),
        name=f"attn_ctx_l{layer}",
    )(q, kt, v, lambda_qk, subln_g)


TQ = 256


def _attn_lat_kernel(q_ref, kt_ref, v_ref, ckt_ref, cv_ref, lq_ref, sg_ref, att_in_ref, o_ref, sc_s, *, lam_init):
    del att_in_ref
    lam = _lambda(lq_ref, lam_init)
    sg = sg_ref[...]

    def scores(h, s):
        d = slice(h * V_DIM + s * QK_DIM, h * V_DIM + (s + 1) * QK_DIM)
        q = q_ref[:, d]
        sc_s[h % 2, s, :, :PAST_LEN] = jnp.dot(q, ckt_ref[d, :].astype(BF), preferred_element_type=F32)
        sc_s[h % 2, s, :, PAST_LEN:] = jnp.dot(q, kt_ref[d, :], preferred_element_type=F32)

    def softmax(h, s):
        sc = sc_s[h % 2, s]
        e = jnp.exp2(sc - jnp.max(sc, axis=-1, keepdims=True))
        return e, 1.0 / jnp.sum(e, axis=-1, keepdims=True)

    def finish(h, p1, p2):
        cols = slice(h * V_DIM, (h + 1) * V_DIM)
        a = (p1[0] * p1[1] - p2[0] * (lam * p2[1])).astype(BF)
        vc = cv_ref[pl.ds(h, PAST_LEN, stride=N_HEADS), :].astype(BF)
        o = jnp.dot(a[:, :PAST_LEN], vc, preferred_element_type=F32)
        o = o + jnp.dot(a[:, PAST_LEN:], v_ref[:, cols], preferred_element_type=F32)
        o_ref[:, cols] = _head_norm(o, sg, lam_init).astype(BF)

    scores(0, 0)
    scores(0, 1)
    for h in range(N_HEADS):
        more = h + 1 < N_HEADS
        if more:
            scores(h + 1, 0)
        p1 = softmax(h, 0)
        if more:
            scores(h + 1, 1)
        finish(h, p1, softmax(h, 1))


def _attn_lat(layer, q, kt, v, cache_kt, cache_v, lambda_qk, subln_g, att, lam_init):
    nqb = DEC_SEQ // TQ
    q0 = N_CTX // TQ
    return pl.pallas_call(
        functools.partial(_attn_lat_kernel, lam_init=lam_init),
        grid=(DEC_BATCH, nqb),
        in_specs=[
            pl.BlockSpec((TQ, ATT_WIDTH), lambda b, t: (q0 + b * nqb + t, 0)),
            pl.BlockSpec((None, ATT_WIDTH, DEC_SEQ), lambda b, t: (CTX_TILES + b, 0, 0)),
            pl.BlockSpec((DEC_SEQ, ATT_WIDTH), lambda b, t: (CTX_TILES + b, 0)),
            pl.BlockSpec((None, None, ATT_WIDTH, PAST_LEN), lambda b, t: (b, layer, 0, 0)),
            pl.BlockSpec((None, None, PAST_LEN * N_HEADS, V_DIM), lambda b, t: (b, layer, 0, 0)),
            pl.BlockSpec((None, 4, QK_DIM), lambda b, t: (layer, 0, 0)),
            pl.BlockSpec((None, 1, V_DIM), lambda b, t: (layer, 0, 0)),
            pl.BlockSpec(memory_space=pl.ANY),
        ],
        out_specs=pl.BlockSpec((TQ, ATT_WIDTH), lambda b, t: (q0 + b * nqb + t, 0)),
        out_shape=jax.ShapeDtypeStruct((N_TOK, ATT_WIDTH), BF),
        scratch_shapes=[pltpu.VMEM((2, 2, TQ, PAST_LEN + DEC_SEQ), F32)],
        input_output_aliases={7: 0},
        compiler_params=_params(("arbitrary", "arbitrary")),
        name=f"attn_lat_l{layer}",
    )(q, kt, v, cache_kt, cache_v, lambda_qk, subln_g, att)


def _out_kernel(att_ref, conv_ref, w_ref, xa_ref, xb_ref, g1_ref, sh_ref, sc_ref, gf_ref,
                xo_ref, h2_ref, wb_s):
    i = pl.program_id(0)

    @pl.when(i == 0)
    def _():
        wb_s[...] = w_ref[...].astype(BF)

    mo = jnp.dot(att_ref[...], wb_s[:ATT_WIDTH, :], preferred_element_type=F32)
    mo = mo + jnp.dot(conv_ref[...], wb_s[ATT_WIDTH:, :], preferred_element_type=F32)
    xn = _stream_rows(xa_ref, xb_ref, i) + g1_ref[...] * mo
    xo_ref[...] = xn
    h2 = (_rms(xn) * gf_ref[...]) * (1.0 + sc_ref[...]) + sh_ref[...]
    h2_ref[...] = h2.astype(BF)


def _out_proj(layer, att, conv, w_out, xs, mod5, g_ffn):
    pair = isinstance(xs, tuple)
    xa, xb = xs if pair else (xs, xs)
    spec_a, spec_b = _stream_specs(pair)
    row_spec = pl.BlockSpec((TM, D_MODEL), lambda i: (i, 0))
    out_specs = [row_spec, row_spec]
    out_shape = [jax.ShapeDtypeStruct((N_TOK, D_MODEL), F32), jax.ShapeDtypeStruct((N_TOK, D_MODEL), BF)]
    return pl.pallas_call(
        _out_kernel,
        grid=(N_TILES,),
        in_specs=[
            pl.BlockSpec((TM, ATT_WIDTH), lambda i: (i, 0)),
            pl.BlockSpec((TM, CONV_WIDTH), lambda i: (i, 0)),
            pl.BlockSpec((None, D_MODEL, D_MODEL), lambda i: (layer, 0, 0)),
            spec_a, spec_b,
            _mod_spec(layer, 2), _mod_spec(layer, 3), _mod_spec(layer, 4),
            pl.BlockSpec((None, 1, D_MODEL), lambda i: (layer, 0, 0)),
        ],
        out_specs=out_specs,
        out_shape=out_shape,
        scratch_shapes=[pltpu.VMEM((D_MODEL, D_MODEL), BF)],
        compiler_params=_params(("arbitrary",)),
        name=f"out_proj_l{layer}",
    )(att, conv, w_out, xa, xb, mod5, mod5, mod5, g_ffn)


def _gu_kernel(h_ref, wg_ref, wu_ref, o_ref, wb_s):
    i = pl.program_id(0)
    j = pl.program_id(1)

    @pl.when(i == 0)
    def _():
        wb_s[j] = wg_ref[...].astype(BF)
        wb_s[j + N_FF_TILES] = wu_ref[...].astype(BF)

    h = h_ref[...]
    g = jnp.dot(h, wb_s[j], preferred_element_type=F32)
    u = jnp.dot(h, wb_s[j + N_FF_TILES], preferred_element_type=F32)
    o_ref[...] = (_silu(g) * u).astype(BF)


def _dense_gu(h2, w_gu):
    nj = N_FF_TILES
    return pl.pallas_call(
        _gu_kernel,
        grid=(N_TILES, nj),
        in_specs=[
            pl.BlockSpec((TM, D_MODEL), lambda i, j: (i, 0)),
            pl.BlockSpec((None, D_MODEL, TN_FF), lambda i, j: (0, 0, jnp.where(i == 0, j, nj - 1)),
                         pipeline_mode=pl.Buffered(1)),
            pl.BlockSpec((None, D_MODEL, TN_FF), lambda i, j: (0, 0, jnp.where(i == 0, j, nj - 1) + nj),
                         pipeline_mode=pl.Buffered(1)),
        ],
        out_specs=pl.BlockSpec((TM, TN_FF), lambda i, j: (i, j)),
        out_shape=jax.ShapeDtypeStruct((N_TOK, D_FF), BF),
        scratch_shapes=[pltpu.VMEM((2 * nj, D_MODEL, TN_FF), BF)],
        compiler_params=_params(("arbitrary", "arbitrary")),
        name="dense_gu",
    )(h2, w_gu, w_gu)


def _down_kernel(a_ref, w_ref, x_ref, g2_ref, o_ref, wb_s):
    j = pl.program_id(1)

    @pl.when(pl.program_id(0) == 0)
    def _():
        wb_s[j] = w_ref[...].astype(BF)

    y = jnp.dot(a_ref[...], wb_s[j], preferred_element_type=F32)
    o_ref[...] = x_ref[...] + g2_ref[...] * y


def _dense_down(layer, act, w_down, x, mod5):
    nj = N_DOWN_TILES
    return pl.pallas_call(
        _down_kernel,
        grid=(N_TILES, nj),
        in_specs=[
            pl.BlockSpec((TM, D_FF), lambda i, j: (i, 0)),
            pl.BlockSpec((None, D_FF, TN_DOWN), lambda i, j: (0, 0, jnp.where(i == 0, j, nj - 1))),
            pl.BlockSpec((TM, TN_DOWN), lambda i, j: (i, j)),
            pl.BlockSpec((None, None, None, 1, TN_DOWN), lambda i, j: (layer, _mod_row(i), 5, 0, j)),
        ],
        out_specs=pl.BlockSpec((TM, TN_DOWN), lambda i, j: (i, j)),
        out_shape=jax.ShapeDtypeStruct((N_TOK, D_MODEL), F32),
        scratch_shapes=[pltpu.VMEM((nj, D_FF, TN_DOWN), BF)],
        compiler_params=_params(("arbitrary", "arbitrary")),
        name="dense_down",
    )(act, w_down, x, mod5)


def _router_kernel(h_ref, wr_ref, lp_ref, rg_ref, n_ref):
    logits = jnp.dot(h_ref[...], wr_ref[...].astype(BF), preferred_element_type=F32)
    lane = lax.broadcasted_iota(jnp.int32, logits.shape, 1)
    lg = jnp.where(lane < N_EXPERTS, logits, -jnp.inf)
    m1 = jnp.max(lg, axis=-1, keepdims=True)
    i1 = jnp.min(jnp.where(lg == m1, lane, LANES), axis=-1, keepdims=True)
    lg2 = jnp.where(lane == i1, -jnp.inf, lg)
    m2 = jnp.max(lg2, axis=-1, keepdims=True)
    i2 = jnp.min(jnp.where(lg2 == m2, lane, LANES), axis=-1, keepdims=True)
    e2 = jnp.exp(m2 - m1)
    w1 = 1.0 / (1.0 + e2)
    w2 = e2 / (1.0 + e2)

    sel1 = lane == i1
    sel2 = lane == i2
    onehot = jnp.logical_or(sel1, sel2)
    rows = lax.broadcasted_iota(jnp.int32, (TM, TM), 0)
    colsi = lax.broadcasted_iota(jnp.int32, (TM, TM), 1)
    earlier = jnp.logical_and(colsi < rows, (colsi // ST) == (rows // ST))
    before = jnp.dot(earlier.astype(BF), onehot.astype(BF), preferred_element_type=F32)
    onehot_f = onehot.astype(F32)
    counts = [jnp.sum(onehot_f[s * ST:(s + 1) * ST], axis=0, keepdims=True) for s in range(SUB_PER_TILE)]
    counts = jnp.concatenate(counts + [jnp.zeros((8 - SUB_PER_TILE, LANES), F32)], axis=0).astype(jnp.int32)
    seg_len = ((counts + (SEG_ALIGN - 1)) // SEG_ALIGN) * SEG_ALIGN
    n_ref[...] = seg_len
    la = lax.broadcasted_iota(jnp.int32, (LANES, LANES), 0)
    lb = lax.broadcasted_iota(jnp.int32, (LANES, LANES), 1)
    seg_start = jnp.dot(seg_len.astype(F32).astype(BF), (la < lb).astype(BF), preferred_element_type=F32)
    start = jnp.concatenate(
        [jnp.broadcast_to(seg_start[s:s + 1], (ST, LANES)) for s in range(SUB_PER_TILE)], axis=0)
    where = before + start
    lp1 = jnp.sum(jnp.where(sel1, where, 0.0), axis=-1, keepdims=True).astype(jnp.int32)
    lp2 = jnp.sum(jnp.where(sel2, where, 0.0), axis=-1, keepdims=True).astype(jnp.int32)
    lp_ref[...] = jnp.where(lane == 0, lp1, jnp.where(lane == 1, lp2, 0))
    rg_ref[...] = jnp.where(lane == 0, w1, jnp.where(lane == 1, w2, 0.0))


def _router(h2, w_router_pad):
    return pl.pallas_call(
        _router_kernel,
        grid=(N_TILES,),
        in_specs=[
            pl.BlockSpec((TM, D_MODEL), lambda i: (i, 0)),
            pl.BlockSpec((D_MODEL, LANES), lambda i: (0, 0)),
        ],
        out_specs=[
            pl.BlockSpec((TM, LANES), lambda i: (i, 0)),
            pl.BlockSpec((TM, LANES), lambda i: (i, 0)),
            pl.BlockSpec((None, 8, LANES), lambda i: (i, 0, 0)),
        ],
        out_shape=[
            jax.ShapeDtypeStruct((N_TOK, LANES), jnp.int32),
            jax.ShapeDtypeStruct((N_TOK, LANES), F32),
            jax.ShapeDtypeStruct((N_TILES, 8, LANES), jnp.int32),
        ],
        compiler_params=_params(("arbitrary",)),
        name="router",
    )(h2, w_router_pad)


def _chunk_copies(s, cnt_ref, cdst_ref, stage, rows_hbm, sem, *, to_hbm, wait):
    def copy(v, h):
        return pltpu.make_async_copy(v, h, sem) if to_hbm else pltpu.make_async_copy(h, v, sem)

    if wait:
        for z in WAIT_PIECES:
            @pl.when((cnt_ref[s] & z) != 0)
            def _():
                copy(stage.at[pl.ds(0, z * SEG_ALIGN)], rows_hbm.at[pl.ds(0, z * SEG_ALIGN)]).wait()
        return

    def one(c, carry):
        v = stage.at[pl.ds(pl.multiple_of(c * SEG_ALIGN, SEG_ALIGN), SEG_ALIGN)]
        h = rows_hbm.at[pl.ds(pl.multiple_of(cdst_ref[s * STAGE_CHUNKS + c], SEG_ALIGN), SEG_ALIGN)]
        copy(v, h).start()
        return carry

    lax.fori_loop(0, cnt_ref[s], one, 0)


def _dispatch_kernel(cnt_ref, cdst_ref, h_ref, lp_ref, xs_ref, stage_s, sem):
    s = pl.program_id(0)
    slot = s % 2
    copies = functools.partial(_chunk_copies, cnt_ref=cnt_ref, cdst_ref=cdst_ref, rows_hbm=xs_ref, to_hbm=True)

    @pl.when(s >= 2)
    def _():
        copies(s - 2, stage=stage_s.at[slot], sem=sem.at[slot], wait=True)

    lpt = lp_ref[...].T
    r = lax.broadcasted_iota(jnp.int32, (STAGE_ROWS, ST), 0)
    pick = jnp.logical_or(r == lpt[0:1, :], r == lpt[1:2, :]).astype(BF)
    stage_s[slot] = jnp.dot(pick, h_ref[...], preferred_element_type=F32).astype(BF)
    copies(s, stage=stage_s.at[slot], sem=sem.at[slot], wait=False)

    @pl.when(s == N_SUB - 1)
    def _():
        copies(s - 1, stage=stage_s.at[1 - slot], sem=sem.at[1 - slot], wait=True)
        copies(s, stage=stage_s.at[slot], sem=sem.at[slot], wait=True)


def _dispatch(cnt, cdst, h2, lp):
    grid_spec = pltpu.PrefetchScalarGridSpec(
        num_scalar_prefetch=2,
        grid=(N_SUB,),
        in_specs=[
            pl.BlockSpec((ST, D_MODEL), lambda s, *_: (s, 0)),
            pl.BlockSpec((ST, LANES), lambda s, *_: (s, 0)),
        ],
        out_specs=pl.BlockSpec(memory_space=pl.ANY),
        scratch_shapes=[pltpu.VMEM((2, STAGE_ROWS, D_MODEL), BF), pltpu.SemaphoreType.DMA((2,))],
    )
    return pl.pallas_call(
        _dispatch_kernel,
        grid_spec=grid_spec,
        out_shape=jax.ShapeDtypeStruct((R_PAD, D_MODEL), BF),
        compiler_params=_params(("arbitrary",)),
        name="moe_dispatch",
    )(cnt, cdst, h2, lp)


def _expert_weights(te_ref, nt_ref, nxt_ref, w_hbm, wf_s, wb_s, sem):
    r = pl.program_id(0)

    def fetch(e):
        return pltpu.make_async_copy(w_hbm.at[0, e], wf_s, sem)

    @pl.when(r == 0)
    def _():
        fetch(te_ref[0]).start()

    first = jnp.logical_or(r == 0, te_ref[r] != te_ref[jnp.maximum(r - 1, 0)])

    @pl.when(jnp.logical_and(r < nt_ref[0], first))
    def _():
        fetch(te_ref[r]).wait()
        wb_s[...] = wf_s[...].astype(BF)

        @pl.when(nxt_ref[r] >= 0)
        def _():
            fetch(nxt_ref[r]).start()


def _moe_ffn_kernel(te_ref, nt_ref, nxt_ref, used_ref, x_ref, wgu_hbm, wd_hbm, o_ref,
                    wgu_f, wgu_b, wd_f, wd_b, sem):
    r = pl.program_id(0)
    _expert_weights(te_ref, nt_ref, nxt_ref, wgu_hbm, wgu_f, wgu_b, sem.at[0])
    _expert_weights(te_ref, nt_ref, nxt_ref, wd_hbm, wd_f, wd_b, sem.at[1])

    def ffn(rows):
        x = x_ref[rows, :]
        g = jnp.dot(x, wgu_b[:, :D_FF_EXPERT], preferred_element_type=F32)
        u = jnp.dot(x, wgu_b[:, D_FF_EXPERT:], preferred_element_type=F32)
        act = (_silu(g) * u).astype(BF)
        o_ref[rows, :] = jnp.dot(act, wd_b[...], preferred_element_type=F32).astype(BF)

    @pl.when(jnp.logical_and(r < nt_ref[0], used_ref[r] > TG // 2))
    def _():
        ffn(slice(0, TG))

    @pl.when(jnp.logical_and(r < nt_ref[0], used_ref[r] <= TG // 2))
    def _():
        ffn(slice(0, TG // 2))


def _moe_ffn(te, nt, nxt, used, rows, w_gu, w_down):
    tile_map = lambda r, te, nt, nxt, used: (jnp.minimum(r, nt[0] - 1), 0)
    grid_spec = pltpu.PrefetchScalarGridSpec(
        num_scalar_prefetch=4,
        grid=(NT_G,),
        in_specs=[pl.BlockSpec((TG, D_MODEL), tile_map),
                  pl.BlockSpec(memory_space=pl.ANY), pl.BlockSpec(memory_space=pl.ANY)],
        out_specs=pl.BlockSpec((TG, D_MODEL), tile_map),
        scratch_shapes=[
            pltpu.VMEM((D_MODEL, 2 * D_FF_EXPERT), F32), pltpu.VMEM((D_MODEL, 2 * D_FF_EXPERT), BF),
            pltpu.VMEM((D_FF_EXPERT, D_MODEL), F32), pltpu.VMEM((D_FF_EXPERT, D_MODEL), BF),
            pltpu.SemaphoreType.DMA((2,)),
        ],
    )
    return pl.pallas_call(
        _moe_ffn_kernel,
        grid_spec=grid_spec,
        out_shape=jax.ShapeDtypeStruct((R_PAD, D_MODEL), BF),
        compiler_params=_params(("arbitrary",)),
        name="moe_ffn",
    )(te, nt, nxt, used, rows, w_gu, w_down)


def _combine_kernel(cnt_ref, cdst_ref, ys_ref, lp_ref, rg_ref, x_ref, g2_ref, fg_ref,
                    oa_ref, ob_ref, stage_s, sem):
    s = pl.program_id(0)
    slot = s % 2
    copies = functools.partial(_chunk_copies, cnt_ref=cnt_ref, cdst_ref=cdst_ref, rows_hbm=ys_ref, to_hbm=False)

    @pl.when(s == 0)
    def _():
        stage_s[...] = jnp.zeros_like(stage_s)
        copies(s, stage=stage_s.at[slot], sem=sem.at[slot], wait=False)

    @pl.when(s + 1 < N_SUB)
    def _():
        copies(s + 1, stage=stage_s.at[1 - slot], sem=sem.at[1 - slot], wait=False)

    copies(s, stage=stage_s.at[slot], sem=sem.at[slot], wait=True)

    lp = lp_ref[...]
    r = lax.broadcasted_iota(jnp.int32, (ST, STAGE_ROWS), 1)
    rows = stage_s[slot]
    a = jnp.dot((r == lp[:, 0:1]).astype(BF), rows, preferred_element_type=F32)
    b = jnp.dot((r == lp[:, 1:2]).astype(BF), rows, preferred_element_type=F32)
    rg = rg_ref[...]
    y = rg[:, 0:1] * a + rg[:, 1:2] * b
    xn = x_ref[...] + g2_ref[...] * y
    out = _rms(xn) * fg_ref[...]

    @pl.when(s < CTX_SUB)
    def _():
        oa_ref[...] = out

    @pl.when(s >= CTX_SUB)
    def _():
        ob_ref[...] = out


def _combine(layer, cnt, cdst, ys, lp, rg, x, mod5, final_g):
    def mod_map(s, *_):
        return (layer, _mod_row(s // SUB_PER_TILE), 5, 0, 0)

    grid_spec = pltpu.PrefetchScalarGridSpec(
        num_scalar_prefetch=2,
        grid=(N_SUB,),
        in_specs=[
            pl.BlockSpec(memory_space=pl.ANY),
            pl.BlockSpec((ST, LANES), lambda s, *_: (s, 0)),
            pl.BlockSpec((ST, LANES), lambda s, *_: (s, 0)),
            pl.BlockSpec((ST, D_MODEL), lambda s, *_: (s, 0)),
            pl.BlockSpec((None, None, None, 1, D_MODEL), mod_map),
            pl.BlockSpec((1, D_MODEL), lambda s, *_: (0, 0)),
        ],
        out_specs=[
            pl.BlockSpec((ST, D_MODEL), lambda s, *_: (jnp.minimum(s, CTX_SUB - 1), 0)),
            pl.BlockSpec((ST, D_MODEL), lambda s, *_: (jnp.maximum(s - CTX_SUB, 0), 0)),
        ],
        scratch_shapes=[pltpu.VMEM((2, STAGE_ROWS, D_MODEL), BF), pltpu.SemaphoreType.DMA((2,))],
    )
    return pl.pallas_call(
        _combine_kernel,
        grid_spec=grid_spec,
        out_shape=[
            jax.ShapeDtypeStruct((N_CTX, D_MODEL), F32),
            jax.ShapeDtypeStruct((N_LAT, D_MODEL), F32),
        ],
        compiler_params=_params(("arbitrary",)),
        name="moe_combine",
    )(cnt, cdst, ys, lp, rg, x, mod5, final_g)


def _group_layout(n_tiles):
    n = n_tiles[:, :SUB_PER_TILE, :N_EXPERTS].reshape(N_SUB, N_EXPERTS)
    tiles = (jnp.sum(n, axis=0) + TG - 1) // TG
    tile_end = jnp.cumsum(tiles)
    region = (tile_end - tiles) * TG
    dst = region[None, :] + jnp.cumsum(n, axis=0) - n
    seg_end = jnp.cumsum(n, axis=1)
    seg = seg_end - n
    row = jnp.arange(STAGE_CHUNKS, dtype=jnp.int32) * SEG_ALIGN
    owner = jnp.sum((row[None, :, None] >= seg_end[:, None, :]).astype(jnp.int32), axis=-1)
    own = jnp.minimum(owner, N_EXPERTS - 1)[..., None] == jnp.arange(N_EXPERTS)
    cdst = jnp.sum(jnp.where(own, (dst - seg)[:, None, :], 0), axis=-1) + row[None, :]
    cnt = seg_end[:, -1] // SEG_ALIGN
    nt = tile_end[-1]
    tile_id = jnp.minimum(jnp.arange(NT_G, dtype=jnp.int32), nt - 1)
    te = jnp.sum((tile_id[:, None] >= tile_end[None, :]).astype(jnp.int32), axis=-1)
    after = jnp.sum(jnp.where(te[:, None] == jnp.arange(N_EXPERTS), tile_end[None, :], 0), axis=-1)
    nxt = jnp.where(after < nt, jnp.sum((after[:, None] >= tile_end[None, :]).astype(jnp.int32), axis=-1), -1)
    mine = te[:, None] == jnp.arange(N_EXPERTS)
    region_end = jnp.sum(jnp.where(mine, (region + jnp.sum(n, axis=0))[None, :], 0), axis=-1)
    used = jnp.clip(region_end - tile_id * TG, 0, TG)
    i32 = lambda a: a.astype(jnp.int32)
    return (i32(cnt), i32(cdst.reshape(N_SUB * STAGE_CHUNKS)), i32(te), i32(nt.reshape(1)), i32(nxt), i32(used))


def _rope_tables():
    p = np.arange(DEC_SEQ)
    row = (p // GRID_W).astype(np.float32)
    col = (p % GRID_W).astype(np.float32)
    half = QK_DIM // 4
    freqs = (ROPE_BASE ** (-np.arange(half, dtype=np.float32) / half)).astype(np.float32)
    lane = np.arange(V_DIM)
    f = freqs[lane & (half - 1)]
    use_col = (lane & (2 * half)) != 0
    ang = (np.where(use_col[None, :], col[:, None], row[:, None]) * f[None, :]).astype(np.float32)
    upper = (lane & half) != 0
    sin = np.sin(ang)
    return jnp.asarray(np.cos(ang), F32), jnp.asarray(np.where(upper[None, :], sin, -sin), F32)


def kernel(x_prompt, x_sample, cache_k, cache_v, c, c_ctx, w_ada, b_ada, norm_mix_g, norm_ffn_g,
           w_in, lambda_qk, subln_g, conv_w, w_out, w_gu_dense, w_down_dense, w_router,
           w_gu_moe, w_down_moe, final_g):
    assert DEPTH == 2
    xs = (x_prompt.reshape(N_CTX, D_MODEL), x_sample.reshape(N_LAT, D_MODEL))
    cond = jnp.concatenate([c_ctx[None, :], c, jnp.zeros((COND_ROWS - 1 - DEC_BATCH, D_MODEL), F32)], axis=0)
    mod5 = _ada(cond, w_ada, b_ada).reshape(DEPTH, COND_ROWS, N_MOD, 1, D_MODEL)
    cos_t, sin_t = _rope_tables()
    cache_kt = jnp.transpose(cache_k, (0, 1, 3, 4, 5, 2)).reshape(DEC_BATCH, DEPTH, ATT_WIDTH, PAST_LEN)
    cache_v4 = cache_v.reshape(DEC_BATCH, DEPTH, PAST_LEN * N_HEADS, V_DIM)
    g_mix = norm_mix_g.reshape(DEPTH, 1, D_MODEL)
    g_ffn = norm_ffn_g.reshape(DEPTH, 1, D_MODEL)
    sg = subln_g.reshape(DEPTH, 1, V_DIM)

    new_kv = None
    for layer in range(DEPTH):
        lam_init = 0.8 - 0.6 * math.exp(-0.3 * layer)
        q, v, kt, conv, nk, nv = _in_proj(layer, xs, mod5, g_mix, w_in, cos_t, sin_t, conv_w, new_kv)
        new_kv = (nk, nv)
        att = _attn_ctx(layer, q, kt, v, lambda_qk, sg, lam_init)
        att = _attn_lat(layer, q, kt, v, cache_kt, cache_v4, lambda_qk, sg, att, lam_init)
        x1, h2 = _out_proj(layer, att, conv, w_out, xs, mod5, g_ffn)
        if layer == 0:
            act = _dense_gu(h2, w_gu_dense)
            xs = _dense_down(layer, act, w_down_dense, x1, mod5)
        else:
            wr = jnp.pad(w_router[0], ((0, 0), (0, LANES - N_EXPERTS)))
            lp, rg, n_tiles = _router(h2, wr)
            cnt, cdst, te, nt, nxt, used = _group_layout(n_tiles)
            xsort = _dispatch(cnt, cdst, h2, lp)
            ys = _moe_ffn(te, nt, nxt, used, xsort, w_gu_moe, w_down_moe)
            y_ctx, y_lat = _combine(layer, cnt, cdst, ys, lp, rg, x1, mod5, final_g.reshape(1, D_MODEL))
    nk, nv = new_kv
    new_k = jnp.transpose(nk.reshape(BATCH, DEPTH, N_HEADS, 2, QK_DIM, SEQ), (0, 1, 5, 2, 3, 4))
    new_v = nv.reshape(BATCH, DEPTH, SEQ, N_HEADS, V_DIM)
    return (y_ctx.reshape(BATCH, SEQ, D_MODEL), y_lat.reshape(DEC_BATCH, DEC_SEQ, D_MODEL), new_k, new_v)
```

```python
import functools
import math

import numpy as np
import jax
import jax.numpy as jnp
from jax import lax
from jax.experimental import pallas as pl
from jax.experimental.pallas import tpu as pltpu

D_MODEL = 1024
BATCH = 16
SEQ = 256
DEPTH = 2
DEC_BATCH = 4
DEC_SEQ = 1024
PAST_LEN = 512
GRID_W = 64
ATT_WIDTH = 512
CONV_WIDTH = 512
N_HEADS = 4
V_DIM = 128
QK_DIM = 64
ROPE_BASE = 10000.0
D_FF = 2816
N_EXPERTS = 8
D_FF_EXPERT = 1408
N_MOD = 6
NORM_EPS = 1e-6
Q_SCALE = QK_DIM ** -0.5 * math.log2(math.e)
IN_COLS = 3 * ATT_WIDTH + 3 * CONV_WIDTH

N_CTX = BATCH * SEQ
N_LAT = DEC_BATCH * DEC_SEQ
N_TOK = N_CTX + N_LAT
TM = 1024
N_TILES = N_TOK // TM
CTX_TILES = N_CTX // TM
SEQ_PER_TILE = TM // SEQ
COND_ROWS = 8
TN_IN = 512
N_IN_TILES = IN_COLS // TN_IN
ROW_CHUNK = 512
TN_FF = 1408
N_FF_TILES = D_FF // TN_FF
TN_DOWN = 512
N_DOWN_TILES = D_MODEL // TN_DOWN
TN_ADA = 1536
TG = 512
ST = 256
SUB_PER_TILE = TM // ST
N_SUB = N_TOK // ST
CTX_SUB = N_CTX // ST
SEG_ALIGN = 16
STAGE_ROWS = 640
STAGE_CHUNKS = STAGE_ROWS // SEG_ALIGN
WAIT_PIECES = (32, 16, 8, 4, 2, 1)
NT_G = -(-(2 * N_TOK + N_SUB * N_EXPERTS * (SEG_ALIGN - 1) + N_EXPERTS * (TG - SEG_ALIGN)) // TG)
R_PAD = NT_G * TG
LANES = 128
VMEM_LIMIT = 60 * 1024 * 1024

BF = jnp.bfloat16
F32 = jnp.float32


def _params(sem, vmem=VMEM_LIMIT):
    return pltpu.CompilerParams(dimension_semantics=sem, vmem_limit_bytes=vmem)


def _mod_row(i):
    return jnp.where(i < CTX_TILES, 0, i - (CTX_TILES - 1))


def _mod_spec(layer, c, width=D_MODEL):
    return pl.BlockSpec((None, None, None, 1, width), lambda i, *_: (layer, _mod_row(i), c, 0, 0))


def _stream_specs(pair, width=D_MODEL):
    a = pl.BlockSpec((TM, width), lambda i, *_: (jnp.minimum(i, CTX_TILES - 1), 0))
    if pair:
        b = pl.BlockSpec((TM, width), lambda i, *_: (jnp.maximum(i - CTX_TILES, 0), 0))
    else:
        b = pl.BlockSpec((TM, width), lambda i, *_: (jnp.maximum(i, CTX_TILES), 0))
    return a, b


def _stream_rows(xa_ref, xb_ref, i):
    return jnp.where(i >= CTX_TILES, xb_ref[...], xa_ref[...])


def _silu(x):
    return x / (1.0 + jnp.exp(-x))


def _rms(x):
    return x * lax.rsqrt(jnp.mean(x * x, axis=-1, keepdims=True) + NORM_EPS)


def _ada_kernel(c_ref, w_ref, b_ref, o_ref):
    s = _silu(c_ref[...]).astype(BF)
    o_ref[...] = jnp.dot(s, w_ref[...].astype(BF), preferred_element_type=F32) + b_ref[...]


def _ada(cond, w_ada, b_ada):
    n = N_MOD * D_MODEL
    return pl.pallas_call(
        _ada_kernel,
        grid=(DEPTH, n // TN_ADA),
        in_specs=[
            pl.BlockSpec((COND_ROWS, D_MODEL), lambda l, j: (0, 0)),
            pl.BlockSpec((None, D_MODEL, TN_ADA), lambda l, j: (l, 0, j)),
            pl.BlockSpec((None, 1, TN_ADA), lambda l, j: (l, 0, j)),
        ],
        out_specs=pl.BlockSpec((None, COND_ROWS, TN_ADA), lambda l, j: (l, 0, j)),
        out_shape=jax.ShapeDtypeStruct((DEPTH, COND_ROWS, n), F32),
        compiler_params=_params(("arbitrary", "arbitrary")),
        name="ada_mod",
    )(cond, w_ada, b_ada.reshape(DEPTH, 1, n))


def _in_kernel(*refs, aliased):
    xa_ref, xb_ref, sh_ref, sc_ref, g_ref, w_ref, cos_ref, sin_ref, cw_ref = refs[:9]
    refs = refs[11:] if aliased else refs[9:]
    q_ref, v_ref, kt_ref, conv_ref, nk_ref, nv_ref, h_s, wb_s, gb_s, gc_s = refs
    i = pl.program_id(0)
    j = pl.program_id(1)
    lat = i >= CTX_TILES
    ctx = jnp.logical_not(lat)

    @pl.when(i == 0)
    def _():
        wb_s[j] = w_ref[...].astype(BF)

    chunks = [slice(c * ROW_CHUNK, (c + 1) * ROW_CHUNK) for c in range(TM // ROW_CHUNK)]
    seqs_per_chunk = ROW_CHUNK // SEQ

    def norm(rows):
        gain = g_ref[...] * (1.0 + sc_ref[...])
        x = jnp.where(lat, xb_ref[rows, :], xa_ref[rows, :])
        h_s[rows, :] = (_rms(x) * gain + sh_ref[...]).astype(BF)

    def proj(rows):
        return jnp.dot(h_s[rows, :], wb_s[j], preferred_element_type=F32)

    def roped(a, rows):
        cos = jnp.concatenate([cos_ref[rows, :]] * N_HEADS, axis=1)
        sin = jnp.concatenate([sin_ref[rows, :]] * N_HEADS, axis=1)
        lane = lax.broadcasted_iota(jnp.int32, a.shape, 1)
        upper = (lane & (QK_DIM // 4)) != 0
        partner = jnp.where(upper, pltpu.roll(a, QK_DIM // 4, 1), pltpu.roll(a, TN_IN - QK_DIM // 4, 1))
        return a * cos + partner * sin

    @pl.when(jnp.logical_and(j == 0, lat))
    def _():
        for rows in chunks:
            norm(rows)
            q_ref[rows, :] = (roped(proj(rows), rows) * Q_SCALE).astype(BF)

    @pl.when(jnp.logical_and(j == 0, ctx))
    def _():
        for rows in chunks:
            norm(rows)
            q_ref[rows, :] = (proj(rows) * Q_SCALE).astype(BF)

    @pl.when(jnp.logical_and(j == 1, lat))
    def _():
        for rows in chunks:
            kt_ref[:, rows] = roped(proj(rows), rows).T.astype(BF)

    @pl.when(jnp.logical_and(j == 1, ctx))
    def _():
        for c, rows in enumerate(chunks):
            kt = proj(rows).T
            kt_ref[:, rows] = kt.astype(BF)
            for s in range(seqs_per_chunk):
                nk_ref[c * seqs_per_chunk + s] = kt[:, s * SEQ:(s + 1) * SEQ]

    @pl.when(jnp.logical_and(j == 2, lat))
    def _():
        for rows in chunks:
            v_ref[rows, :] = proj(rows).astype(BF)

    @pl.when(jnp.logical_and(j == 2, ctx))
    def _():
        for c, rows in enumerate(chunks):
            acc = proj(rows)
            v_ref[rows, :] = acc.astype(BF)
            for s in range(seqs_per_chunk):
                for h in range(N_HEADS):
                    nv_ref[c * seqs_per_chunk + s, pl.ds(h, SEQ, stride=N_HEADS), :] = (
                        acc[s * SEQ:(s + 1) * SEQ, h * V_DIM:(h + 1) * V_DIM])

    @pl.when(j == 3)
    def _():
        for rows in chunks:
            gb_s[rows, :] = proj(rows)

    @pl.when(j == 4)
    def _():
        for rows in chunks:
            gc_s[rows, :] = proj(rows)

    @pl.when(j == 5)
    def _():
        for rows in chunks:
            gc_s[rows, :] = gc_s[rows, :] * proj(rows)
        u = gc_s[...]
        seq = jnp.where(lat, DEC_SEQ, SEQ)
        pos = lax.broadcasted_iota(jnp.int32, (TM, 1), 0) & (seq - 1)
        prev = jnp.where(pos == 0, 0.0, pltpu.roll(u, 1, 0))
        nxt = jnp.where(pos == seq - 1, 0.0, pltpu.roll(u, TM - 1, 0))
        cw = cw_ref[...]
        conv = prev * cw[0:1] + u * cw[1:2] + nxt * cw[2:3]
        conv_ref[...] = (gb_s[...] * conv).astype(BF)


def _in_proj(layer, xs, mod5, g_mix, w_in, cos_t, sin_t, conv_w, new_kv):
    pair = isinstance(xs, tuple)
    xa, xb = xs if pair else (xs, xs)
    spec_a, spec_b = _stream_specs(pair)
    ctx_i = lambda i: jnp.minimum(i, CTX_TILES - 1)
    in_specs = [
        spec_a, spec_b,
        _mod_spec(layer, 0), _mod_spec(layer, 1),
        pl.BlockSpec((None, 1, D_MODEL), lambda i, j: (layer, 0, 0)),
        pl.BlockSpec((None, D_MODEL, TN_IN), lambda i, j: (layer, 0, jnp.where(i == 0, j, N_IN_TILES - 1))),
        pl.BlockSpec((DEC_SEQ, V_DIM), lambda i, j: (0, 0)),
        pl.BlockSpec((DEC_SEQ, V_DIM), lambda i, j: (0, 0)),
        pl.BlockSpec((None, 3, CONV_WIDTH), lambda i, j: (layer, 0, 0)),
    ]
    args = [xa, xb, mod5, mod5, g_mix, w_in, cos_t, sin_t, conv_w]
    aliases = {}
    if new_kv is not None:
        in_specs += [pl.BlockSpec(memory_space=pl.ANY), pl.BlockSpec(memory_space=pl.ANY)]
        args += list(new_kv)
        aliases = {9: 4, 10: 5}
    row_tile = pl.BlockSpec((TM, TN_IN), lambda i, j: (i, 0))
    return pl.pallas_call(
        functools.partial(_in_kernel, aliased=new_kv is not None),
        grid=(N_TILES, N_IN_TILES),
        in_specs=in_specs,
        out_specs=[
            row_tile,
            row_tile,
            pl.BlockSpec((None, ATT_WIDTH, TM), lambda i, j: (i, 0, 0)),
            row_tile,
            pl.BlockSpec((SEQ_PER_TILE, None, ATT_WIDTH, SEQ), lambda i, j: (ctx_i(i), layer, 0, 0)),
            pl.BlockSpec((SEQ_PER_TILE, None, SEQ * N_HEADS, V_DIM), lambda i, j: (ctx_i(i), layer, 0, 0)),
        ],
        out_shape=[
            jax.ShapeDtypeStruct((N_TOK, ATT_WIDTH), BF),
            jax.ShapeDtypeStruct((N_TOK, ATT_WIDTH), BF),
            jax.ShapeDtypeStruct((N_TILES, ATT_WIDTH, TM), BF),
            jax.ShapeDtypeStruct((N_TOK, CONV_WIDTH), BF),
            jax.ShapeDtypeStruct((BATCH, DEPTH, ATT_WIDTH, SEQ), F32),
            jax.ShapeDtypeStruct((BATCH, DEPTH, SEQ * N_HEADS, V_DIM), F32),
        ],
        scratch_shapes=[
            pltpu.VMEM((TM, D_MODEL), BF),
            pltpu.VMEM((N_IN_TILES, D_MODEL, TN_IN), BF),
            pltpu.VMEM((TM, CONV_WIDTH), F32),
            pltpu.VMEM((TM, CONV_WIDTH), F32),
        ],
        input_output_aliases=aliases,
        compiler_params=_params(("arbitrary", "arbitrary")),
        name=f"in_proj_l{layer}",
    )(*args)


def _lambda(lq_ref, lam_init):
    lq = lq_ref[...]
    a = jnp.exp(jnp.sum(lq[0:1] * lq[1:2], axis=-1, keepdims=True))
    b = jnp.exp(jnp.sum(lq[2:3] * lq[3:4], axis=-1, keepdims=True))
    return a - b + lam_init


def _head_norm(o, sg, lam_init):
    return _rms(o) * sg * (1.0 - lam_init)


def _attn_ctx_kernel(q_ref, kt_ref, v_ref, lq_ref, sg_ref, o_ref, sc_s, *, lam_init):
    lam = _lambda(lq_ref, lam_init)
    sg = sg_ref[...]

    def scores(h):
        for s in range(2):
            d = slice(h * V_DIM + s * QK_DIM, h * V_DIM + (s + 1) * QK_DIM)
            sc_s[h % 2, s] = jnp.dot(q_ref[:, d], kt_ref[d, :], preferred_element_type=F32)

    def finish(h):
        cols = slice(h * V_DIM, (h + 1) * V_DIM)
        v = v_ref[:, cols]
        outs = []
        for s in range(2):
            sc = sc_s[h % 2, s]
            e = jnp.exp2(sc - jnp.max(sc, axis=-1, keepdims=True))
            r = 1.0 / jnp.sum(e, axis=-1, keepdims=True)
            outs.append(jnp.dot(e.astype(BF), v, preferred_element_type=F32) * r)
        o = outs[0] - lam * outs[1]
        o_ref[:, cols] = _head_norm(o, sg, lam_init).astype(BF)

    scores(0)
    for h in range(N_HEADS):
        if h + 1 < N_HEADS:
            scores(h + 1)
        finish(h)


def _attn_ctx(layer, q, kt, v, lambda_qk, subln_g, lam_init):
    return pl.pallas_call(
        functools.partial(_attn_ctx_kernel, lam_init=lam_init),
        grid=(BATCH,),
        in_specs=[
            pl.BlockSpec((SEQ, ATT_WIDTH), lambda b: (b, 0)),
            pl.BlockSpec((None, ATT_WIDTH, SEQ), lambda b: (b // SEQ_PER_TILE, 0, b % SEQ_PER_TILE)),
            pl.BlockSpec((SEQ, ATT_WIDTH), lambda b: (b, 0)),
            pl.BlockSpec((None, 4, QK_DIM), lambda b: (layer, 0, 0)),
            pl.BlockSpec((None, 1, V_DIM), lambda b: (layer, 0, 0)),
        ],
        out_specs=pl.BlockSpec((SEQ, ATT_WIDTH), lambda b: (b, 0)),
        out_shape=jax.ShapeDtypeStruct((N_TOK, ATT_WIDTH), BF),
        scratch_shapes=[pltpu.VMEM((2, 2, SEQ, SEQ), F32)],
        compiler_params=_params(("arbitrary",)),
        name=f"attn_ctx_l{layer}",
    )(q, kt, v, lambda_qk, subln_g)


TQ = 256


def _attn_lat_kernel(q_ref, kt_ref, v_ref, ckt_ref, cv_ref, lq_ref, sg_ref, att_in_ref, o_ref, sc_s, *, lam_init):
    del att_in_ref
    lam = _lambda(lq_ref, lam_init)
    sg = sg_ref[...]

    def scores(h, s):
        d = slice(h * V_DIM + s * QK_DIM, h * V_DIM + (s + 1) * QK_DIM)
        q = q_ref[:, d]
        sc_s[h % 2, s, :, :PAST_LEN] = jnp.dot(q, ckt_ref[d, :].astype(BF), preferred_element_type=F32)
        sc_s[h % 2, s, :, PAST_LEN:] = jnp.dot(q, kt_ref[d, :], preferred_element_type=F32)

    def softmax(h, s):
        sc = sc_s[h % 2, s]
        e = jnp.exp2(sc - jnp.max(sc, axis=-1, keepdims=True))
        return e, 1.0 / jnp.sum(e, axis=-1, keepdims=True)

    def finish(h, p1, p2):
        cols = slice(h * V_DIM, (h + 1) * V_DIM)
        a = (p1[0] * p1[1] - p2[0] * (lam * p2[1])).astype(BF)
        vc = cv_ref[pl.ds(h, PAST_LEN, stride=N_HEADS), :].astype(BF)
        o = jnp.dot(a[:, :PAST_LEN], vc, preferred_element_type=F32)
        o = o + jnp.dot(a[:, PAST_LEN:], v_ref[:, cols], preferred_element_type=F32)
        o_ref[:, cols] = _head_norm(o, sg, lam_init).astype(BF)

    scores(0, 0)
    scores(0, 1)
    for h in range(N_HEADS):
        more = h + 1 < N_HEADS
        if more:
            scores(h + 1, 0)
        p1 = softmax(h, 0)
        if more:
            scores(h + 1, 1)
        finish(h, p1, softmax(h, 1))


def _attn_lat(layer, q, kt, v, cache_kt, cache_v, lambda_qk, subln_g, att, lam_init):
    nqb = DEC_SEQ // TQ
    q0 = N_CTX // TQ
    return pl.pallas_call(
        functools.partial(_attn_lat_kernel, lam_init=lam_init),
        grid=(DEC_BATCH, nqb),
        in_specs=[
            pl.BlockSpec((TQ, ATT_WIDTH), lambda b, t: (q0 + b * nqb + t, 0)),
            pl.BlockSpec((None, ATT_WIDTH, DEC_SEQ), lambda b, t: (CTX_TILES + b, 0, 0)),
            pl.BlockSpec((DEC_SEQ, ATT_WIDTH), lambda b, t: (CTX_TILES + b, 0)),
            pl.BlockSpec((None, None, ATT_WIDTH, PAST_LEN), lambda b, t: (b, layer, 0, 0)),
            pl.BlockSpec((None, None, PAST_LEN * N_HEADS, V_DIM), lambda b, t: (b, layer, 0, 0)),
            pl.BlockSpec((None, 4, QK_DIM), lambda b, t: (layer, 0, 0)),
            pl.BlockSpec((None, 1, V_DIM), lambda b, t: (layer, 0, 0)),
            pl.BlockSpec(memory_space=pl.ANY),
        ],
        out_specs=pl.BlockSpec((TQ, ATT_WIDTH), lambda b, t: (q0 + b * nqb + t, 0)),
        out_shape=jax.ShapeDtypeStruct((N_TOK, ATT_WIDTH), BF),
        scratch_shapes=[pltpu.VMEM((2, 2, TQ, PAST_LEN + DEC_SEQ), F32)],
        input_output_aliases={7: 0},
        compiler_params=_params(("arbitrary", "arbitrary")),
        name=f"attn_lat_l{layer}",
    )(q, kt, v, cache_kt, cache_v, lambda_qk, subln_g, att)


def _out_kernel(att_ref, conv_ref, w_ref, xa_ref, xb_ref, g1_ref, sh_ref, sc_ref, gf_ref, *rest, route):
    if route:
        wr_ref, xo_ref, h2_ref, lp_ref, rg_ref, n_ref, wb_s = rest
    else:
        xo_ref, h2_ref, wb_s = rest
    i = pl.program_id(0)

    @pl.when(i == 0)
    def _():
        wb_s[...] = w_ref[...].astype(BF)

    mo = jnp.dot(att_ref[...], wb_s[:ATT_WIDTH, :], preferred_element_type=F32)
    mo = mo + jnp.dot(conv_ref[...], wb_s[ATT_WIDTH:, :], preferred_element_type=F32)
    xn = _stream_rows(xa_ref, xb_ref, i) + g1_ref[...] * mo
    xo_ref[...] = xn
    h2 = (_rms(xn) * gf_ref[...]) * (1.0 + sc_ref[...]) + sh_ref[...]
    h2b = h2.astype(BF)
    h2_ref[...] = h2b
    if route:
        _route(h2b, wr_ref, lp_ref, rg_ref, n_ref)


def _out_proj(layer, att, conv, w_out, xs, mod5, g_ffn, w_router_pad=None):
    pair = isinstance(xs, tuple)
    xa, xb = xs if pair else (xs, xs)
    spec_a, spec_b = _stream_specs(pair)
    row_spec = pl.BlockSpec((TM, D_MODEL), lambda i: (i, 0))
    lane_spec = pl.BlockSpec((TM, LANES), lambda i: (i, 0))
    in_specs = [
        pl.BlockSpec((TM, ATT_WIDTH), lambda i: (i, 0)),
        pl.BlockSpec((TM, CONV_WIDTH), lambda i: (i, 0)),
        pl.BlockSpec((None, D_MODEL, D_MODEL), lambda i: (layer, 0, 0)),
        spec_a, spec_b,
        _mod_spec(layer, 2), _mod_spec(layer, 3), _mod_spec(layer, 4),
        pl.BlockSpec((None, 1, D_MODEL), lambda i: (layer, 0, 0)),
    ]
    args = [att, conv, w_out, xa, xb, mod5, mod5, mod5, g_ffn]
    out_specs = [row_spec, row_spec]
    out_shape = [jax.ShapeDtypeStruct((N_TOK, D_MODEL), F32), jax.ShapeDtypeStruct((N_TOK, D_MODEL), BF)]
    route = w_router_pad is not None
    if route:
        in_specs.append(pl.BlockSpec((D_MODEL, LANES), lambda i: (0, 0)))
        args.append(w_router_pad)
        out_specs += [lane_spec, lane_spec, pl.BlockSpec((None, 8, LANES), lambda i: (i, 0, 0))]
        out_shape += [jax.ShapeDtypeStruct((N_TOK, LANES), jnp.int32), jax.ShapeDtypeStruct((N_TOK, LANES), F32),
                      jax.ShapeDtypeStruct((N_TILES, 8, LANES), jnp.int32)]
    return pl.pallas_call(
        functools.partial(_out_kernel, route=route),
        grid=(N_TILES,),
        in_specs=in_specs,
        out_specs=out_specs,
        out_shape=out_shape,
        scratch_shapes=[pltpu.VMEM((D_MODEL, D_MODEL), BF)],
        compiler_params=_params(("arbitrary",)),
        name=f"out_proj_l{layer}",
    )(*args)


def _gu_kernel(h_ref, wg_ref, wu_ref, o_ref, wb_s):
    i = pl.program_id(0)
    j = pl.program_id(1)

    @pl.when(i == 0)
    def _():
        wb_s[j] = wg_ref[...].astype(BF)
        wb_s[j + N_FF_TILES] = wu_ref[...].astype(BF)

    h = h_ref[...]
    g = jnp.dot(h, wb_s[j], preferred_element_type=F32)
    u = jnp.dot(h, wb_s[j + N_FF_TILES], preferred_element_type=F32)
    o_ref[...] = (_silu(g) * u).astype(BF)


def _dense_gu(h2, w_gu):
    nj = N_FF_TILES
    return pl.pallas_call(
        _gu_kernel,
        grid=(N_TILES, nj),
        in_specs=[
            pl.BlockSpec((TM, D_MODEL), lambda i, j: (i, 0)),
            pl.BlockSpec((None, D_MODEL, TN_FF), lambda i, j: (0, 0, jnp.where(i == 0, j, nj - 1)),
                         pipeline_mode=pl.Buffered(1)),
            pl.BlockSpec((None, D_MODEL, TN_FF), lambda i, j: (0, 0, jnp.where(i == 0, j, nj - 1) + nj),
                         pipeline_mode=pl.Buffered(1)),
        ],
        out_specs=pl.BlockSpec((TM, TN_FF), lambda i, j: (i, j)),
        out_shape=jax.ShapeDtypeStruct((N_TOK, D_FF), BF),
        scratch_shapes=[pltpu.VMEM((2 * nj, D_MODEL, TN_FF), BF)],
        compiler_params=_params(("arbitrary", "arbitrary")),
        name="dense_gu",
    )(h2, w_gu, w_gu)


def _down_kernel(a_ref, w_ref, x_ref, g2_ref, o_ref, wb_s):
    j = pl.program_id(1)

    @pl.when(pl.program_id(0) == 0)
    def _():
        wb_s[j] = w_ref[...].astype(BF)

    y = jnp.dot(a_ref[...], wb_s[j], preferred_element_type=F32)
    o_ref[...] = x_ref[...] + g2_ref[...] * y


def _dense_down(layer, act, w_down, x, mod5):
    nj = N_DOWN_TILES
    return pl.pallas_call(
        _down_kernel,
        grid=(N_TILES, nj),
        in_specs=[
            pl.BlockSpec((TM, D_FF), lambda i, j: (i, 0)),
            pl.BlockSpec((None, D_FF, TN_DOWN), lambda i, j: (0, 0, jnp.where(i == 0, j, nj - 1))),
            pl.BlockSpec((TM, TN_DOWN), lambda i, j: (i, j)),
            pl.BlockSpec((None, None, None, 1, TN_DOWN), lambda i, j: (layer, _mod_row(i), 5, 0, j)),
        ],
        out_specs=pl.BlockSpec((TM, TN_DOWN), lambda i, j: (i, j)),
        out_shape=jax.ShapeDtypeStruct((N_TOK, D_MODEL), F32),
        scratch_shapes=[pltpu.VMEM((nj, D_FF, TN_DOWN), BF)],
        compiler_params=_params(("arbitrary", "arbitrary")),
        name="dense_down",
    )(act, w_down, x, mod5)


def _route(h, wr_ref, lp_ref, rg_ref, n_ref):
    logits = jnp.dot(h, wr_ref[...].astype(BF), preferred_element_type=F32)
    lane = lax.broadcasted_iota(jnp.int32, logits.shape, 1)
    lg = jnp.where(lane < N_EXPERTS, logits, -jnp.inf)
    m1 = jnp.max(lg, axis=-1, keepdims=True)
    i1 = jnp.min(jnp.where(lg == m1, lane, LANES), axis=-1, keepdims=True)
    lg2 = jnp.where(lane == i1, -jnp.inf, lg)
    m2 = jnp.max(lg2, axis=-1, keepdims=True)
    i2 = jnp.min(jnp.where(lg2 == m2, lane, LANES), axis=-1, keepdims=True)
    e2 = jnp.exp(m2 - m1)
    w1 = 1.0 / (1.0 + e2)
    w2 = e2 / (1.0 + e2)

    sel1 = lane == i1
    sel2 = lane == i2
    onehot = jnp.logical_or(sel1, sel2)
    rows = lax.broadcasted_iota(jnp.int32, (TM, TM), 0)
    colsi = lax.broadcasted_iota(jnp.int32, (TM, TM), 1)
    earlier = jnp.logical_and(colsi < rows, (colsi // ST) == (rows // ST))
    before = jnp.dot(earlier.astype(BF), onehot.astype(BF), preferred_element_type=F32)
    onehot_f = onehot.astype(F32)
    counts = [jnp.sum(onehot_f[s * ST:(s + 1) * ST], axis=0, keepdims=True) for s in range(SUB_PER_TILE)]
    counts = jnp.concatenate(counts + [jnp.zeros((8 - SUB_PER_TILE, LANES), F32)], axis=0).astype(jnp.int32)
    seg_len = ((counts + (SEG_ALIGN - 1)) // SEG_ALIGN) * SEG_ALIGN
    n_ref[...] = seg_len
    la = lax.broadcasted_iota(jnp.int32, (LANES, LANES), 0)
    lb = lax.broadcasted_iota(jnp.int32, (LANES, LANES), 1)
    seg_start = jnp.dot(seg_len.astype(F32).astype(BF), (la < lb).astype(BF), preferred_element_type=F32)
    start = jnp.concatenate(
        [jnp.broadcast_to(seg_start[s:s + 1], (ST, LANES)) for s in range(SUB_PER_TILE)], axis=0)
    where = before + start
    lp1 = jnp.sum(jnp.where(sel1, where, 0.0), axis=-1, keepdims=True).astype(jnp.int32)
    lp2 = jnp.sum(jnp.where(sel2, where, 0.0), axis=-1, keepdims=True).astype(jnp.int32)
    lp_ref[...] = jnp.where(lane == 0, lp1, jnp.where(lane == 1, lp2, 0))
    rg_ref[...] = jnp.where(lane == 0, w1, jnp.where(lane == 1, w2, 0.0))


def _chunk_copies(s, cnt_ref, cdst_ref, stage, rows_hbm, sem, *, to_hbm, wait):
    def copy(v, h):
        return pltpu.make_async_copy(v, h, sem) if to_hbm else pltpu.make_async_copy(h, v, sem)

    if wait:
        for z in WAIT_PIECES:
            @pl.when((cnt_ref[s] & z) != 0)
            def _():
                copy(stage.at[pl.ds(0, z * SEG_ALIGN)], rows_hbm.at[pl.ds(0, z * SEG_ALIGN)]).wait()
        return

    def one(c, carry):
        v = stage.at[pl.ds(pl.multiple_of(c * SEG_ALIGN, SEG_ALIGN), SEG_ALIGN)]
        h = rows_hbm.at[pl.ds(pl.multiple_of(cdst_ref[s * STAGE_CHUNKS + c], SEG_ALIGN), SEG_ALIGN)]
        copy(v, h).start()
        return carry

    lax.fori_loop(0, cnt_ref[s], one, 0)


def _dispatch_kernel(cnt_ref, cdst_ref, h_ref, lp_ref, xs_ref, stage_s, sem):
    copies = functools.partial(_chunk_copies, cnt_ref=cnt_ref, cdst_ref=cdst_ref, rows_hbm=xs_ref, to_hbm=True)
    for k in range(SUB_PER_TILE):
        s = pl.program_id(0) * SUB_PER_TILE + k
        slot = k % 2
        rows = slice(k * ST, (k + 1) * ST)

        @pl.when(s >= 2)
        def _():
            copies(s - 2, stage=stage_s.at[slot], sem=sem.at[slot], wait=True)

        lpt = lp_ref[rows, :].T
        r = lax.broadcasted_iota(jnp.int32, (STAGE_ROWS, ST), 0)
        pick = jnp.logical_or(r == lpt[0:1, :], r == lpt[1:2, :]).astype(BF)
        stage_s[slot] = jnp.dot(pick, h_ref[rows, :], preferred_element_type=F32).astype(BF)
        copies(s, stage=stage_s.at[slot], sem=sem.at[slot], wait=False)

    @pl.when(pl.program_id(0) == N_TILES - 1)
    def _():
        copies(N_SUB - 2, stage=stage_s.at[0], sem=sem.at[0], wait=True)
        copies(N_SUB - 1, stage=stage_s.at[1], sem=sem.at[1], wait=True)


def _dispatch(cnt, cdst, h2, lp):
    assert SUB_PER_TILE % 2 == 0
    grid_spec = pltpu.PrefetchScalarGridSpec(
        num_scalar_prefetch=2,
        grid=(N_TILES,),
        in_specs=[
            pl.BlockSpec((TM, D_MODEL), lambda i, *_: (i, 0)),
            pl.BlockSpec((TM, LANES), lambda i, *_: (i, 0)),
        ],
        out_specs=pl.BlockSpec(memory_space=pl.ANY),
        scratch_shapes=[pltpu.VMEM((2, STAGE_ROWS, D_MODEL), BF), pltpu.SemaphoreType.DMA((2,))],
    )
    return pl.pallas_call(
        _dispatch_kernel,
        grid_spec=grid_spec,
        out_shape=jax.ShapeDtypeStruct((R_PAD, D_MODEL), BF),
        compiler_params=_params(("arbitrary",)),
        name="moe_dispatch",
    )(cnt, cdst, h2, lp)


def _expert_weights(te_ref, nt_ref, nxt_ref, w_hbm, wf_s, wb_s, sem):
    r = pl.program_id(0)

    def fetch(e):
        return pltpu.make_async_copy(w_hbm.at[0, e], wf_s, sem)

    @pl.when(r == 0)
    def _():
        fetch(te_ref[0]).start()

    first = jnp.logical_or(r == 0, te_ref[r] != te_ref[jnp.maximum(r - 1, 0)])

    @pl.when(jnp.logical_and(r < nt_ref[0], first))
    def _():
        fetch(te_ref[r]).wait()
        wb_s[...] = wf_s[...].astype(BF)

        @pl.when(nxt_ref[r] >= 0)
        def _():
            fetch(nxt_ref[r]).start()


def _moe_ffn_kernel(te_ref, nt_ref, nxt_ref, used_ref, x_ref, wgu_hbm, wd_hbm, o_ref,
                    wgu_f, wgu_b, wd_f, wd_b, sem):
    r = pl.program_id(0)
    _expert_weights(te_ref, nt_ref, nxt_ref, wgu_hbm, wgu_f, wgu_b, sem.at[0])
    _expert_weights(te_ref, nt_ref, nxt_ref, wd_hbm, wd_f, wd_b, sem.at[1])

    def ffn(rows):
        x = x_ref[rows, :]
        g = jnp.dot(x, wgu_b[:, :D_FF_EXPERT], preferred_element_type=F32)
        u = jnp.dot(x, wgu_b[:, D_FF_EXPERT:], preferred_element_type=F32)
        act = (_silu(g) * u).astype(BF)
        o_ref[rows, :] = jnp.dot(act, wd_b[...], preferred_element_type=F32).astype(BF)

    @pl.when(jnp.logical_and(r < nt_ref[0], used_ref[r] > TG // 2))
    def _():
        ffn(slice(0, TG))

    @pl.when(jnp.logical_and(r < nt_ref[0], used_ref[r] <= TG // 2))
    def _():
        ffn(slice(0, TG // 2))


def _moe_ffn(te, nt, nxt, used, rows, w_gu, w_down):
    tile_map = lambda r, te, nt, nxt, used: (jnp.minimum(r, nt[0] - 1), 0)
    grid_spec = pltpu.PrefetchScalarGridSpec(
        num_scalar_prefetch=4,
        grid=(NT_G,),
        in_specs=[pl.BlockSpec((TG, D_MODEL), tile_map),
                  pl.BlockSpec(memory_space=pl.ANY), pl.BlockSpec(memory_space=pl.ANY)],
        out_specs=pl.BlockSpec((TG, D_MODEL), tile_map),
        scratch_shapes=[
            pltpu.VMEM((D_MODEL, 2 * D_FF_EXPERT), F32), pltpu.VMEM((D_MODEL, 2 * D_FF_EXPERT), BF),
            pltpu.VMEM((D_FF_EXPERT, D_MODEL), F32), pltpu.VMEM((D_FF_EXPERT, D_MODEL), BF),
            pltpu.SemaphoreType.DMA((2,)),
        ],
    )
    return pl.pallas_call(
        _moe_ffn_kernel,
        grid_spec=grid_spec,
        out_shape=jax.ShapeDtypeStruct((R_PAD, D_MODEL), BF),
        compiler_params=_params(("arbitrary",)),
        name="moe_ffn",
    )(te, nt, nxt, used, rows, w_gu, w_down)


def _combine_kernel(cnt_ref, cdst_ref, ys_ref, lp_ref, rg_ref, x_ref, g2_ref, fg_ref,
                    oa_ref, ob_ref, stage_s, sem):
    i = pl.program_id(0)
    copies = functools.partial(_chunk_copies, cnt_ref=cnt_ref, cdst_ref=cdst_ref, rows_hbm=ys_ref, to_hbm=False)

    @pl.when(i == 0)
    def _():
        stage_s[...] = jnp.zeros_like(stage_s)
        copies(0, stage=stage_s.at[0], sem=sem.at[0], wait=False)

    for k in range(SUB_PER_TILE):
        s = i * SUB_PER_TILE + k
        slot = k % 2
        rows = slice(k * ST, (k + 1) * ST)

        @pl.when(s + 1 < N_SUB)
        def _():
            copies(s + 1, stage=stage_s.at[1 - slot], sem=sem.at[1 - slot], wait=False)

        copies(s, stage=stage_s.at[slot], sem=sem.at[slot], wait=True)

        lp = lp_ref[rows, :]
        r = lax.broadcasted_iota(jnp.int32, (ST, STAGE_ROWS), 1)
        staged = stage_s[slot]
        a = jnp.dot((r == lp[:, 0:1]).astype(BF), staged, preferred_element_type=F32)
        b = jnp.dot((r == lp[:, 1:2]).astype(BF), staged, preferred_element_type=F32)
        rg = rg_ref[rows, :]
        y = rg[:, 0:1] * a + rg[:, 1:2] * b
        xn = x_ref[rows, :] + g2_ref[...] * y
        out = _rms(xn) * fg_ref[...]

        @pl.when(i < CTX_TILES)
        def _():
            oa_ref[rows, :] = out

        @pl.when(i >= CTX_TILES)
        def _():
            ob_ref[rows, :] = out


def _combine(layer, cnt, cdst, ys, lp, rg, x, mod5, final_g):
    assert SUB_PER_TILE % 2 == 0
    grid_spec = pltpu.PrefetchScalarGridSpec(
        num_scalar_prefetch=2,
        grid=(N_TILES,),
        in_specs=[
            pl.BlockSpec(memory_space=pl.ANY),
            pl.BlockSpec((TM, LANES), lambda i, *_: (i, 0)),
            pl.BlockSpec((TM, LANES), lambda i, *_: (i, 0)),
            pl.BlockSpec((TM, D_MODEL), lambda i, *_: (i, 0)),
            _mod_spec(layer, 5),
            pl.BlockSpec((1, D_MODEL), lambda i, *_: (0, 0)),
        ],
        out_specs=[
            pl.BlockSpec((TM, D_MODEL), lambda i, *_: (jnp.minimum(i, CTX_TILES - 1), 0)),
            pl.BlockSpec((TM, D_MODEL), lambda i, *_: (jnp.maximum(i - CTX_TILES, 0), 0)),
        ],
        scratch_shapes=[pltpu.VMEM((2, STAGE_ROWS, D_MODEL), BF), pltpu.SemaphoreType.DMA((2,))],
    )
    return pl.pallas_call(
        _combine_kernel,
        grid_spec=grid_spec,
        out_shape=[
            jax.ShapeDtypeStruct((N_CTX, D_MODEL), F32),
            jax.ShapeDtypeStruct((N_LAT, D_MODEL), F32),
        ],
        compiler_params=_params(("arbitrary",)),
        name="moe_combine",
    )(cnt, cdst, ys, lp, rg, x, mod5, final_g)


def _group_layout(n_tiles):
    n = n_tiles[:, :SUB_PER_TILE, :N_EXPERTS].reshape(N_SUB, N_EXPERTS)
    tiles = (jnp.sum(n, axis=0) + TG - 1) // TG
    tile_end = jnp.cumsum(tiles)
    region = (tile_end - tiles) * TG
    dst = region[None, :] + jnp.cumsum(n, axis=0) - n
    seg_end = jnp.cumsum(n, axis=1)
    seg = seg_end - n
    row = jnp.arange(STAGE_CHUNKS, dtype=jnp.int32) * SEG_ALIGN
    owner = jnp.sum((row[None, :, None] >= seg_end[:, None, :]).astype(jnp.int32), axis=-1)
    own = jnp.minimum(owner, N_EXPERTS - 1)[..., None] == jnp.arange(N_EXPERTS)
    cdst = jnp.sum(jnp.where(own, (dst - seg)[:, None, :], 0), axis=-1) + row[None, :]
    cnt = seg_end[:, -1] // SEG_ALIGN
    nt = tile_end[-1]
    tile_id = jnp.minimum(jnp.arange(NT_G, dtype=jnp.int32), nt - 1)
    te = jnp.sum((tile_id[:, None] >= tile_end[None, :]).astype(jnp.int32), axis=-1)
    after = jnp.sum(jnp.where(te[:, None] == jnp.arange(N_EXPERTS), tile_end[None, :], 0), axis=-1)
    nxt = jnp.where(after < nt, jnp.sum((after[:, None] >= tile_end[None, :]).astype(jnp.int32), axis=-1), -1)
    mine = te[:, None] == jnp.arange(N_EXPERTS)
    region_end = jnp.sum(jnp.where(mine, (region + jnp.sum(n, axis=0))[None, :], 0), axis=-1)
    used = jnp.clip(region_end - tile_id * TG, 0, TG)
    i32 = lambda a: a.astype(jnp.int32)
    return (i32(cnt), i32(cdst.reshape(N_SUB * STAGE_CHUNKS)), i32(te), i32(nt.reshape(1)), i32(nxt), i32(used))


def _rope_tables():
    p = np.arange(DEC_SEQ)
    row = (p // GRID_W).astype(np.float32)
    col = (p % GRID_W).astype(np.float32)
    half = QK_DIM // 4
    freqs = (ROPE_BASE ** (-np.arange(half, dtype=np.float32) / half)).astype(np.float32)
    lane = np.arange(V_DIM)
    f = freqs[lane & (half - 1)]
    use_col = (lane & (2 * half)) != 0
    ang = (np.where(use_col[None, :], col[:, None], row[:, None]) * f[None, :]).astype(np.float32)
    upper = (lane & half) != 0
    sin = np.sin(ang)
    return jnp.asarray(np.cos(ang), F32), jnp.asarray(np.where(upper[None, :], sin, -sin), F32)


def kernel(x_prompt, x_sample, cache_k, cache_v, c, c_ctx, w_ada, b_ada, norm_mix_g, norm_ffn_g,
           w_in, lambda_qk, subln_g, conv_w, w_out, w_gu_dense, w_down_dense, w_router,
           w_gu_moe, w_down_moe, final_g):
    assert DEPTH == 2
    xs = (x_prompt.reshape(N_CTX, D_MODEL), x_sample.reshape(N_LAT, D_MODEL))
    cond = jnp.concatenate([c_ctx[None, :], c, jnp.zeros((COND_ROWS - 1 - DEC_BATCH, D_MODEL), F32)], axis=0)
    mod5 = _ada(cond, w_ada, b_ada).reshape(DEPTH, COND_ROWS, N_MOD, 1, D_MODEL)
    cos_t, sin_t = _rope_tables()
    cache_kt = jnp.transpose(cache_k, (0, 1, 3, 4, 5, 2)).reshape(DEC_BATCH, DEPTH, ATT_WIDTH, PAST_LEN)
    cache_v4 = cache_v.reshape(DEC_BATCH, DEPTH, PAST_LEN * N_HEADS, V_DIM)
    g_mix = norm_mix_g.reshape(DEPTH, 1, D_MODEL)
    g_ffn = norm_ffn_g.reshape(DEPTH, 1, D_MODEL)
    sg = subln_g.reshape(DEPTH, 1, V_DIM)

    new_kv = None
    for layer in range(DEPTH):
        lam_init = 0.8 - 0.6 * math.exp(-0.3 * layer)
        q, v, kt, conv, nk, nv = _in_proj(layer, xs, mod5, g_mix, w_in, cos_t, sin_t, conv_w, new_kv)
        new_kv = (nk, nv)
        att = _attn_ctx(layer, q, kt, v, lambda_qk, sg, lam_init)
        att = _attn_lat(layer, q, kt, v, cache_kt, cache_v4, lambda_qk, sg, att, lam_init)
        if layer == 0:
            x1, h2 = _out_proj(layer, att, conv, w_out, xs, mod5, g_ffn)
            act = _dense_gu(h2, w_gu_dense)
            xs = _dense_down(layer, act, w_down_dense, x1, mod5)
        else:
            wr = jnp.pad(w_router[0], ((0, 0), (0, LANES - N_EXPERTS)))
            x1, h2, lp, rg, n_tiles = _out_proj(layer, att, conv, w_out, xs, mod5, g_ffn, wr)
            cnt, cdst, te, nt, nxt, used = _group_layout(n_tiles)
            xsort = _dispatch(cnt, cdst, h2, lp)
            ys = _moe_ffn(te, nt, nxt, used, xsort, w_gu_moe, w_down_moe)
            y_ctx, y_lat = _combine(layer, cnt, cdst, ys, lp, rg, x1, mod5, final_g.reshape(1, D_MODEL))
    nk, nv = new_kv
    new_k = jnp.transpose(nk.reshape(BATCH, DEPTH, N_HEADS, 2, QK_DIM, SEQ), (0, 1, 5, 2, 3, 4))
    new_v = nv.reshape(BATCH, DEPTH, SEQ, N_HEADS, V_DIM)
    return (y_ctx.reshape(BATCH, SEQ, D_MODEL), y_lat.reshape(DEC_BATCH, DEC_SEQ, D_MODEL), new_k, new_v)
```

```python
import functools
import math

import numpy as np
import jax
import jax.numpy as jnp
from jax import lax
from jax.experimental import pallas as pl
from jax.experimental.pallas import tpu as pltpu

D_MODEL = 1024
BATCH = 16
SEQ = 256
DEPTH = 2
DEC_BATCH = 4
DEC_SEQ = 1024
PAST_LEN = 512
GRID_W = 64
ATT_WIDTH = 512
CONV_WIDTH = 512
N_HEADS = 4
V_DIM = 128
QK_DIM = 64
ROPE_BASE = 10000.0
D_FF = 2816
N_EXPERTS = 8
D_FF_EXPERT = 1408
N_MOD = 6
NORM_EPS = 1e-6
Q_SCALE = QK_DIM ** -0.5 * math.log2(math.e)
IN_COLS = 3 * ATT_WIDTH + 3 * CONV_WIDTH

N_CTX = BATCH * SEQ
N_LAT = DEC_BATCH * DEC_SEQ
N_TOK = N_CTX + N_LAT
TM = 1024
N_TILES = N_TOK // TM
CTX_TILES = N_CTX // TM
SEQ_PER_TILE = TM // SEQ
COND_ROWS = 8
TN_IN = 1024
N_IN_TILES = IN_COLS // TN_IN
ROW_CHUNK = 512
TN_FF = 1408
N_FF_TILES = D_FF // TN_FF
TN_DOWN = 512
N_DOWN_TILES = D_MODEL // TN_DOWN
TN_ADA = 1536
TG = 512
ST = 256
SUB_PER_TILE = TM // ST
N_SUB = N_TOK // ST
CTX_SUB = N_CTX // ST
SEG_ALIGN = 16
STAGE_ROWS = 640
STAGE_CHUNKS = STAGE_ROWS // SEG_ALIGN
WAIT_PIECES = (32, 16, 8, 4, 2, 1)
COMBINE_SLOTS = 4
COMBINE_AHEAD = 2
NT_G = -(-(2 * N_TOK + N_SUB * N_EXPERTS * (SEG_ALIGN - 1) + N_EXPERTS * (TG - SEG_ALIGN)) // TG)
R_PAD = NT_G * TG
LANES = 128
VMEM_LIMIT = 60 * 1024 * 1024

BF = jnp.bfloat16
F32 = jnp.float32


def _params(sem, vmem=VMEM_LIMIT):
    return pltpu.CompilerParams(dimension_semantics=sem, vmem_limit_bytes=vmem)


def _mod_row(i):
    return jnp.where(i < CTX_TILES, 0, i - (CTX_TILES - 1))


def _mod_spec(layer, c, width=D_MODEL):
    return pl.BlockSpec((None, None, None, 1, width), lambda i, *_: (layer, _mod_row(i), c, 0, 0))


def _stream_specs(pair, width=D_MODEL):
    a = pl.BlockSpec((TM, width), lambda i, *_: (jnp.minimum(i, CTX_TILES - 1), 0))
    if pair:
        b = pl.BlockSpec((TM, width), lambda i, *_: (jnp.maximum(i - CTX_TILES, 0), 0))
    else:
        b = pl.BlockSpec((TM, width), lambda i, *_: (jnp.maximum(i, CTX_TILES), 0))
    return a, b


def _stream_rows(xa_ref, xb_ref, i):
    return jnp.where(i >= CTX_TILES, xb_ref[...], xa_ref[...])


def _silu(x):
    return x / (1.0 + jnp.exp(-x))


def _rms(x):
    return x * lax.rsqrt(jnp.mean(x * x, axis=-1, keepdims=True) + NORM_EPS)


def _ada_kernel(c_ref, w_ref, b_ref, o_ref):
    s = _silu(c_ref[...]).astype(BF)
    o_ref[...] = jnp.dot(s, w_ref[...].astype(BF), preferred_element_type=F32) + b_ref[...]


def _ada(cond, w_ada, b_ada):
    n = N_MOD * D_MODEL
    return pl.pallas_call(
        _ada_kernel,
        grid=(DEPTH, n // TN_ADA),
        in_specs=[
            pl.BlockSpec((COND_ROWS, D_MODEL), lambda l, j: (0, 0)),
            pl.BlockSpec((None, D_MODEL, TN_ADA), lambda l, j: (l, 0, j)),
            pl.BlockSpec((None, 1, TN_ADA), lambda l, j: (l, 0, j)),
        ],
        out_specs=pl.BlockSpec((None, COND_ROWS, TN_ADA), lambda l, j: (l, 0, j)),
        out_shape=jax.ShapeDtypeStruct((DEPTH, COND_ROWS, n), F32),
        compiler_params=_params(("arbitrary", "arbitrary")),
        name="ada_mod",
    )(cond, w_ada, b_ada.reshape(DEPTH, 1, n))


def _in_kernel(*refs, aliased):
    xa_ref, xb_ref, sh_ref, sc_ref, g_ref, w_ref, cos_ref, sin_ref, cw_ref = refs[:9]
    refs = refs[11:] if aliased else refs[9:]
    q_ref, v_ref, kt_ref, conv_ref, nk_ref, nv_ref, h_s, wb_s, gb_s, gc_s = refs
    i = pl.program_id(0)
    j = pl.program_id(1)
    lat = i >= CTX_TILES
    ctx = jnp.logical_not(lat)

    @pl.when(i == 0)
    def _():
        wb_s[j] = w_ref[...].astype(BF)

    chunks = [slice(c * ROW_CHUNK, (c + 1) * ROW_CHUNK) for c in range(TM // ROW_CHUNK)]
    seqs_per_chunk = ROW_CHUNK // SEQ

    def norm(rows):
        gain = g_ref[...] * (1.0 + sc_ref[...])
        x = jnp.where(lat, xb_ref[rows, :], xa_ref[rows, :])
        h_s[rows, :] = (_rms(x) * gain + sh_ref[...]).astype(BF)

    def proj(rows):
        acc = jnp.dot(h_s[rows, :], wb_s[j], preferred_element_type=F32)
        return acc[:, :ATT_WIDTH], acc[:, ATT_WIDTH:]

    def roped(a, rows):
        cos = jnp.concatenate([cos_ref[rows, :]] * N_HEADS, axis=1)
        sin = jnp.concatenate([sin_ref[rows, :]] * N_HEADS, axis=1)
        lane = lax.broadcasted_iota(jnp.int32, a.shape, 1)
        upper = (lane & (QK_DIM // 4)) != 0
        partner = jnp.where(upper, pltpu.roll(a, QK_DIM // 4, 1), pltpu.roll(a, ATT_WIDTH - QK_DIM // 4, 1))
        return a * cos + partner * sin

    @pl.when(jnp.logical_and(j == 0, lat))
    def _():
        for rows in chunks:
            norm(rows)
            q, k = proj(rows)
            q_ref[rows, :] = (roped(q, rows) * Q_SCALE).astype(BF)
            kt_ref[:, rows] = roped(k, rows).T.astype(BF)

    @pl.when(jnp.logical_and(j == 0, ctx))
    def _():
        for c, rows in enumerate(chunks):
            norm(rows)
            q, k = proj(rows)
            q_ref[rows, :] = (q * Q_SCALE).astype(BF)
            kt = k.T
            kt_ref[:, rows] = kt.astype(BF)
            for s in range(seqs_per_chunk):
                nk_ref[c * seqs_per_chunk + s] = kt[:, s * SEQ:(s + 1) * SEQ]

    @pl.when(jnp.logical_and(j == 1, lat))
    def _():
        for rows in chunks:
            v, gb = proj(rows)
            v_ref[rows, :] = v.astype(BF)
            gb_s[rows, :] = gb

    @pl.when(jnp.logical_and(j == 1, ctx))
    def _():
        for c, rows in enumerate(chunks):
            v, gb = proj(rows)
            v_ref[rows, :] = v.astype(BF)
            gb_s[rows, :] = gb
            for s in range(seqs_per_chunk):
                for h in range(N_HEADS):
                    nv_ref[c * seqs_per_chunk + s, pl.ds(h, SEQ, stride=N_HEADS), :] = (
                        v[s * SEQ:(s + 1) * SEQ, h * V_DIM:(h + 1) * V_DIM])

    @pl.when(j == 2)
    def _():
        for rows in chunks:
            gc, xin = proj(rows)
            gc_s[rows, :] = gc * xin
        u = gc_s[...]
        seq = jnp.where(lat, DEC_SEQ, SEQ)
        pos = lax.broadcasted_iota(jnp.int32, (TM, 1), 0) & (seq - 1)
        prev = jnp.where(pos == 0, 0.0, pltpu.roll(u, 1, 0))
        nxt = jnp.where(pos == seq - 1, 0.0, pltpu.roll(u, TM - 1, 0))
        cw = cw_ref[...]
        conv = prev * cw[0:1] + u * cw[1:2] + nxt * cw[2:3]
        conv_ref[...] = (gb_s[...] * conv).astype(BF)


def _in_proj(layer, xs, mod5, g_mix, w_in, cos_t, sin_t, conv_w, new_kv):
    pair = isinstance(xs, tuple)
    xa, xb = xs if pair else (xs, xs)
    spec_a, spec_b = _stream_specs(pair)
    ctx_i = lambda i: jnp.minimum(i, CTX_TILES - 1)
    in_specs = [
        spec_a, spec_b,
        _mod_spec(layer, 0), _mod_spec(layer, 1),
        pl.BlockSpec((None, 1, D_MODEL), lambda i, j: (layer, 0, 0)),
        pl.BlockSpec((None, D_MODEL, TN_IN), lambda i, j: (layer, 0, jnp.where(i == 0, j, N_IN_TILES - 1))),
        pl.BlockSpec((DEC_SEQ, V_DIM), lambda i, j: (0, 0)),
        pl.BlockSpec((DEC_SEQ, V_DIM), lambda i, j: (0, 0)),
        pl.BlockSpec((None, 3, CONV_WIDTH), lambda i, j: (layer, 0, 0)),
    ]
    args = [xa, xb, mod5, mod5, g_mix, w_in, cos_t, sin_t, conv_w]
    aliases = {}
    if new_kv is not None:
        in_specs += [pl.BlockSpec(memory_space=pl.ANY), pl.BlockSpec(memory_space=pl.ANY)]
        args += list(new_kv)
        aliases = {9: 4, 10: 5}
    row_tile = pl.BlockSpec((TM, ATT_WIDTH), lambda i, j: (i, 0))
    return pl.pallas_call(
        functools.partial(_in_kernel, aliased=new_kv is not None),
        grid=(N_TILES, N_IN_TILES),
        in_specs=in_specs,
        out_specs=[
            row_tile,
            row_tile,
            pl.BlockSpec((None, ATT_WIDTH, TM), lambda i, j: (i, 0, 0)),
            row_tile,
            pl.BlockSpec((SEQ_PER_TILE, None, ATT_WIDTH, SEQ), lambda i, j: (ctx_i(i), layer, 0, 0)),
            pl.BlockSpec((SEQ_PER_TILE, None, SEQ * N_HEADS, V_DIM), lambda i, j: (ctx_i(i), layer, 0, 0)),
        ],
        out_shape=[
            jax.ShapeDtypeStruct((N_TOK, ATT_WIDTH), BF),
            jax.ShapeDtypeStruct((N_TOK, ATT_WIDTH), BF),
            jax.ShapeDtypeStruct((N_TILES, ATT_WIDTH, TM), BF),
            jax.ShapeDtypeStruct((N_TOK, CONV_WIDTH), BF),
            jax.ShapeDtypeStruct((BATCH, DEPTH, ATT_WIDTH, SEQ), F32),
            jax.ShapeDtypeStruct((BATCH, DEPTH, SEQ * N_HEADS, V_DIM), F32),
        ],
        scratch_shapes=[
            pltpu.VMEM((TM, D_MODEL), BF),
            pltpu.VMEM((N_IN_TILES, D_MODEL, TN_IN), BF),
            pltpu.VMEM((TM, CONV_WIDTH), F32),
            pltpu.VMEM((TM, CONV_WIDTH), F32),
        ],
        input_output_aliases=aliases,
        compiler_params=_params(("arbitrary", "arbitrary")),
        name=f"in_proj_l{layer}",
    )(*args)


def _lambda(lq_ref, lam_init):
    lq = lq_ref[...]
    a = jnp.exp(jnp.sum(lq[0:1] * lq[1:2], axis=-1, keepdims=True))
    b = jnp.exp(jnp.sum(lq[2:3] * lq[3:4], axis=-1, keepdims=True))
    return a - b + lam_init


def _head_norm(o, sg, lam_init):
    return _rms(o) * sg * (1.0 - lam_init)


def _attn_ctx_kernel(q_ref, kt_ref, v_ref, lq_ref, sg_ref, o_ref, sc_s, *, lam_init):
    lam = _lambda(lq_ref, lam_init)
    sg = sg_ref[...]

    def scores(h):
        for s in range(2):
            d = slice(h * V_DIM + s * QK_DIM, h * V_DIM + (s + 1) * QK_DIM)
            sc_s[h % 2, s] = jnp.dot(q_ref[:, d], kt_ref[d, :], preferred_element_type=F32)

    def finish(h):
        cols = slice(h * V_DIM, (h + 1) * V_DIM)
        v = v_ref[:, cols]
        outs = []
        for s in range(2):
            sc = sc_s[h % 2, s]
            e = jnp.exp2(sc - jnp.max(sc, axis=-1, keepdims=True))
            r = 1.0 / jnp.sum(e, axis=-1, keepdims=True)
            outs.append(jnp.dot(e.astype(BF), v, preferred_element_type=F32) * r)
        o = outs[0] - lam * outs[1]
        o_ref[:, cols] = _head_norm(o, sg, lam_init).astype(BF)

    scores(0)
    for h in range(N_HEADS):
        if h + 1 < N_HEADS:
            scores(h + 1)
        finish(h)


def _attn_ctx(layer, q, kt, v, lambda_qk, subln_g, lam_init):
    return pl.pallas_call(
        functools.partial(_attn_ctx_kernel, lam_init=lam_init),
        grid=(BATCH,),
        in_specs=[
            pl.BlockSpec((SEQ, ATT_WIDTH), lambda b: (b, 0)),
            pl.BlockSpec((None, ATT_WIDTH, SEQ), lambda b: (b // SEQ_PER_TILE, 0, b % SEQ_PER_TILE)),
            pl.BlockSpec((SEQ, ATT_WIDTH), lambda b: (b, 0)),
            pl.BlockSpec((None, 4, QK_DIM), lambda b: (layer, 0, 0)),
            pl.BlockSpec((None, 1, V_DIM), lambda b: (layer, 0, 0)),
        ],
        out_specs=pl.BlockSpec((SEQ, ATT_WIDTH), lambda b: (b, 0)),
        out_shape=jax.ShapeDtypeStruct((N_TOK, ATT_WIDTH), BF),
        scratch_shapes=[pltpu.VMEM((2, 2, SEQ, SEQ), F32)],
        compiler_params=_params(("arbitrary",)),
        name=f"attn_ctx_l{layer}",
    )(q, kt, v, lambda_qk, subln_g)


TQ = 256


def _attn_lat_kernel(q_ref, kt_ref, v_ref, ckt_ref, cv_ref, lq_ref, sg_ref, att_in_ref, o_ref, sc_s, *, lam_init):
    del att_in_ref
    lam = _lambda(lq_ref, lam_init)
    sg = sg_ref[...]

    def scores(h, s):
        d = slice(h * V_DIM + s * QK_DIM, h * V_DIM + (s + 1) * QK_DIM)
        q = q_ref[:, d]
        sc_s[h % 2, s, :, :PAST_LEN] = jnp.dot(q, ckt_ref[d, :].astype(BF), preferred_element_type=F32)
        sc_s[h % 2, s, :, PAST_LEN:] = jnp.dot(q, kt_ref[d, :], preferred_element_type=F32)

    def softmax(h, s):
        sc = sc_s[h % 2, s]
        e = jnp.exp2(sc - jnp.max(sc, axis=-1, keepdims=True))
        return e, 1.0 / jnp.sum(e, axis=-1, keepdims=True)

    def finish(h, p1, p2):
        cols = slice(h * V_DIM, (h + 1) * V_DIM)
        e = jnp.concatenate([p1[0].astype(BF), p2[0].astype(BF)], axis=0)
        vc = cv_ref[pl.ds(h, PAST_LEN, stride=N_HEADS), :].astype(BF)
        pv = jnp.dot(e[:, :PAST_LEN], vc, preferred_element_type=F32)
        pv = pv + jnp.dot(e[:, PAST_LEN:], v_ref[:, cols], preferred_element_type=F32)
        o = pv[:TQ] * p1[1] - pv[TQ:] * (lam * p2[1])
        o_ref[:, cols] = _head_norm(o, sg, lam_init).astype(BF)

    scores(0, 0)
    scores(0, 1)
    for h in range(N_HEADS):
        more = h + 1 < N_HEADS
        if more:
            scores(h + 1, 0)
        p1 = softmax(h, 0)
        if more:
            scores(h + 1, 1)
        finish(h, p1, softmax(h, 1))


def _attn_lat(layer, q, kt, v, cache_kt, cache_v, lambda_qk, subln_g, att, lam_init):
    nqb = DEC_SEQ // TQ
    q0 = N_CTX // TQ
    return pl.pallas_call(
        functools.partial(_attn_lat_kernel, lam_init=lam_init),
        grid=(DEC_BATCH, nqb),
        in_specs=[
            pl.BlockSpec((TQ, ATT_WIDTH), lambda b, t: (q0 + b * nqb + t, 0)),
            pl.BlockSpec((None, ATT_WIDTH, DEC_SEQ), lambda b, t: (CTX_TILES + b, 0, 0)),
            pl.BlockSpec((DEC_SEQ, ATT_WIDTH), lambda b, t: (CTX_TILES + b, 0)),
            pl.BlockSpec((None, None, ATT_WIDTH, PAST_LEN), lambda b, t: (b, layer, 0, 0)),
            pl.BlockSpec((None, None, PAST_LEN * N_HEADS, V_DIM), lambda b, t: (b, layer, 0, 0)),
            pl.BlockSpec((None, 4, QK_DIM), lambda b, t: (layer, 0, 0)),
            pl.BlockSpec((None, 1, V_DIM), lambda b, t: (layer, 0, 0)),
            pl.BlockSpec(memory_space=pl.ANY),
        ],
        out_specs=pl.BlockSpec((TQ, ATT_WIDTH), lambda b, t: (q0 + b * nqb + t, 0)),
        out_shape=jax.ShapeDtypeStruct((N_TOK, ATT_WIDTH), BF),
        scratch_shapes=[pltpu.VMEM((2, 2, TQ, PAST_LEN + DEC_SEQ), F32)],
        input_output_aliases={7: 0},
        compiler_params=_params(("arbitrary", "arbitrary")),
        name=f"attn_lat_l{layer}",
    )(q, kt, v, cache_kt, cache_v, lambda_qk, subln_g, att)


def _out_kernel(att_ref, conv_ref, w_ref, xa_ref, xb_ref, g1_ref, sh_ref, sc_ref, gf_ref, *rest, route):
    if route:
        wr_ref, xo_ref, h2_ref, lp_ref, rg_ref, n_ref, wb_s = rest
    else:
        xo_ref, h2_ref, wb_s = rest
    i = pl.program_id(0)

    @pl.when(i == 0)
    def _():
        wb_s[...] = w_ref[...].astype(BF)

    mo = jnp.dot(att_ref[...], wb_s[:ATT_WIDTH, :], preferred_element_type=F32)
    mo = mo + jnp.dot(conv_ref[...], wb_s[ATT_WIDTH:, :], preferred_element_type=F32)
    xn = _stream_rows(xa_ref, xb_ref, i) + g1_ref[...] * mo
    xo_ref[...] = xn
    h2 = (_rms(xn) * gf_ref[...]) * (1.0 + sc_ref[...]) + sh_ref[...]
    h2b = h2.astype(BF)
    h2_ref[...] = h2b
    if route:
        _route(h2b, wr_ref, lp_ref, rg_ref, n_ref)


def _out_proj(layer, att, conv, w_out, xs, mod5, g_ffn, w_router_pad=None):
    pair = isinstance(xs, tuple)
    xa, xb = xs if pair else (xs, xs)
    spec_a, spec_b = _stream_specs(pair)
    row_spec = pl.BlockSpec((TM, D_MODEL), lambda i: (i, 0))
    lane_spec = pl.BlockSpec((TM, LANES), lambda i: (i, 0))
    in_specs = [
        pl.BlockSpec((TM, ATT_WIDTH), lambda i: (i, 0)),
        pl.BlockSpec((TM, CONV_WIDTH), lambda i: (i, 0)),
        pl.BlockSpec((None, D_MODEL, D_MODEL), lambda i: (layer, 0, 0)),
        spec_a, spec_b,
        _mod_spec(layer, 2), _mod_spec(layer, 3), _mod_spec(layer, 4),
        pl.BlockSpec((None, 1, D_MODEL), lambda i: (layer, 0, 0)),
    ]
    args = [att, conv, w_out, xa, xb, mod5, mod5, mod5, g_ffn]
    out_specs = [row_spec, row_spec]
    out_shape = [jax.ShapeDtypeStruct((N_TOK, D_MODEL), F32), jax.ShapeDtypeStruct((N_TOK, D_MODEL), BF)]
    route = w_router_pad is not None
    if route:
        in_specs.append(pl.BlockSpec((D_MODEL, LANES), lambda i: (0, 0)))
        args.append(w_router_pad)
        out_specs += [lane_spec, lane_spec, pl.BlockSpec((None, 8, LANES), lambda i: (i, 0, 0))]
        out_shape += [jax.ShapeDtypeStruct((N_TOK, LANES), jnp.int32), jax.ShapeDtypeStruct((N_TOK, LANES), F32),
                      jax.ShapeDtypeStruct((N_TILES, 8, LANES), jnp.int32)]
    return pl.pallas_call(
        functools.partial(_out_kernel, route=route),
        grid=(N_TILES,),
        in_specs=in_specs,
        out_specs=out_specs,
        out_shape=out_shape,
        scratch_shapes=[pltpu.VMEM((D_MODEL, D_MODEL), BF)],
        compiler_params=_params(("arbitrary",)),
        name=f"out_proj_l{layer}",
    )(*args)


def _gu_kernel(h_ref, wg_ref, wu_ref, o_ref, wb_s):
    i = pl.program_id(0)
    j = pl.program_id(1)

    @pl.when(i == 0)
    def _():
        wb_s[j] = wg_ref[...].astype(BF)
        wb_s[j + N_FF_TILES] = wu_ref[...].astype(BF)

    h = h_ref[...]
    g = jnp.dot(h, wb_s[j], preferred_element_type=F32)
    u = jnp.dot(h, wb_s[j + N_FF_TILES], preferred_element_type=F32)
    o_ref[...] = (_silu(g) * u).astype(BF)


def _dense_gu(h2, w_gu):
    nj = N_FF_TILES
    return pl.pallas_call(
        _gu_kernel,
        grid=(N_TILES, nj),
        in_specs=[
            pl.BlockSpec((TM, D_MODEL), lambda i, j: (i, 0)),
            pl.BlockSpec((None, D_MODEL, TN_FF), lambda i, j: (0, 0, jnp.where(i == 0, j, nj - 1)),
                         pipeline_mode=pl.Buffered(1)),
            pl.BlockSpec((None, D_MODEL, TN_FF), lambda i, j: (0, 0, jnp.where(i == 0, j, nj - 1) + nj),
                         pipeline_mode=pl.Buffered(1)),
        ],
        out_specs=pl.BlockSpec((TM, TN_FF), lambda i, j: (i, j)),
        out_shape=jax.ShapeDtypeStruct((N_TOK, D_FF), BF),
        scratch_shapes=[pltpu.VMEM((2 * nj, D_MODEL, TN_FF), BF)],
        compiler_params=_params(("arbitrary", "arbitrary")),
        name="dense_gu",
    )(h2, w_gu, w_gu)


def _down_kernel(a_ref, w_ref, x_ref, g2_ref, o_ref, wb_s):
    j = pl.program_id(1)

    @pl.when(pl.program_id(0) == 0)
    def _():
        wb_s[j] = w_ref[...].astype(BF)

    y = jnp.dot(a_ref[...], wb_s[j], preferred_element_type=F32)
    o_ref[...] = x_ref[...] + g2_ref[...] * y


def _dense_down(layer, act, w_down, x, mod5):
    nj = N_DOWN_TILES
    return pl.pallas_call(
        _down_kernel,
        grid=(N_TILES, nj),
        in_specs=[
            pl.BlockSpec((TM, D_FF), lambda i, j: (i, 0)),
            pl.BlockSpec((None, D_FF, TN_DOWN), lambda i, j: (0, 0, jnp.where(i == 0, j, nj - 1))),
            pl.BlockSpec((TM, TN_DOWN), lambda i, j: (i, j)),
            pl.BlockSpec((None, None, None, 1, TN_DOWN), lambda i, j: (layer, _mod_row(i), 5, 0, j)),
        ],
        out_specs=pl.BlockSpec((TM, TN_DOWN), lambda i, j: (i, j)),
        out_shape=jax.ShapeDtypeStruct((N_TOK, D_MODEL), F32),
        scratch_shapes=[pltpu.VMEM((nj, D_FF, TN_DOWN), BF)],
        compiler_params=_params(("arbitrary", "arbitrary")),
        name="dense_down",
    )(act, w_down, x, mod5)


def _route(h, wr_ref, lp_ref, rg_ref, n_ref):
    logits = jnp.dot(h, wr_ref[...].astype(BF), preferred_element_type=F32)
    lane = lax.broadcasted_iota(jnp.int32, logits.shape, 1)
    lg = jnp.where(lane < N_EXPERTS, logits, -jnp.inf)
    m1 = jnp.max(lg, axis=-1, keepdims=True)
    i1 = jnp.min(jnp.where(lg == m1, lane, LANES), axis=-1, keepdims=True)
    lg2 = jnp.where(lane == i1, -jnp.inf, lg)
    m2 = jnp.max(lg2, axis=-1, keepdims=True)
    i2 = jnp.min(jnp.where(lg2 == m2, lane, LANES), axis=-1, keepdims=True)
    e2 = jnp.exp(m2 - m1)
    w1 = 1.0 / (1.0 + e2)
    w2 = e2 / (1.0 + e2)

    sel1 = lane == i1
    sel2 = lane == i2
    onehot = jnp.logical_or(sel1, sel2)
    rows = lax.broadcasted_iota(jnp.int32, (TM, TM), 0)
    colsi = lax.broadcasted_iota(jnp.int32, (TM, TM), 1)
    earlier = jnp.logical_and(colsi < rows, (colsi // ST) == (rows // ST))
    before = jnp.dot(earlier.astype(BF), onehot.astype(BF), preferred_element_type=F32)
    onehot_f = onehot.astype(F32)
    counts = [jnp.sum(onehot_f[s * ST:(s + 1) * ST], axis=0, keepdims=True) for s in range(SUB_PER_TILE)]
    counts = jnp.concatenate(counts + [jnp.zeros((8 - SUB_PER_TILE, LANES), F32)], axis=0).astype(jnp.int32)
    seg_len = ((counts + (SEG_ALIGN - 1)) // SEG_ALIGN) * SEG_ALIGN
    n_ref[...] = seg_len
    la = lax.broadcasted_iota(jnp.int32, (LANES, LANES), 0)
    lb = lax.broadcasted_iota(jnp.int32, (LANES, LANES), 1)
    seg_start = jnp.dot(seg_len.astype(F32).astype(BF), (la < lb).astype(BF), preferred_element_type=F32)
    start = jnp.concatenate(
        [jnp.broadcast_to(seg_start[s:s + 1], (ST, LANES)) for s in range(SUB_PER_TILE)], axis=0)
    where = before + start
    lp1 = jnp.sum(jnp.where(sel1, where, 0.0), axis=-1, keepdims=True).astype(jnp.int32)
    lp2 = jnp.sum(jnp.where(sel2, where, 0.0), axis=-1, keepdims=True).astype(jnp.int32)
    lp_ref[...] = jnp.where(lane == 0, lp1, jnp.where(lane == 1, lp2, 0))
    rg_ref[...] = jnp.where(lane == 0, w1, jnp.where(lane == 1, w2, 0.0))


def _chunk_copies(s, cnt_ref, cdst_ref, stage, rows_hbm, sem, *, to_hbm, wait):
    def copy(v, h):
        return pltpu.make_async_copy(v, h, sem) if to_hbm else pltpu.make_async_copy(h, v, sem)

    if wait:
        for z in WAIT_PIECES:
            @pl.when((cnt_ref[s] & z) != 0)
            def _():
                copy(stage.at[pl.ds(0, z * SEG_ALIGN)], rows_hbm.at[pl.ds(0, z * SEG_ALIGN)]).wait()
        return

    def one(c, carry):
        v = stage.at[pl.ds(pl.multiple_of(c * SEG_ALIGN, SEG_ALIGN), SEG_ALIGN)]
        h = rows_hbm.at[pl.ds(pl.multiple_of(cdst_ref[s * STAGE_CHUNKS + c], SEG_ALIGN), SEG_ALIGN)]
        copy(v, h).start()
        return carry

    lax.fori_loop(0, cnt_ref[s], one, 0)


def _dispatch_kernel(cnt_ref, cdst_ref, h_ref, lp_ref, xs_ref, stage_s, sem):
    copies = functools.partial(_chunk_copies, cnt_ref=cnt_ref, cdst_ref=cdst_ref, rows_hbm=xs_ref, to_hbm=True)
    for k in range(SUB_PER_TILE):
        s = pl.program_id(0) * SUB_PER_TILE + k
        slot = k % 2
        rows = slice(k * ST, (k + 1) * ST)

        @pl.when(s >= 2)
        def _():
            copies(s - 2, stage=stage_s.at[slot], sem=sem.at[slot], wait=True)

        lpt = lp_ref[rows, :].T
        r = lax.broadcasted_iota(jnp.int32, (STAGE_ROWS, ST), 0)
        pick = jnp.logical_or(r == lpt[0:1, :], r == lpt[1:2, :]).astype(BF)
        stage_s[slot] = jnp.dot(pick, h_ref[rows, :], preferred_element_type=F32).astype(BF)
        copies(s, stage=stage_s.at[slot], sem=sem.at[slot], wait=False)

    @pl.when(pl.program_id(0) == N_TILES - 1)
    def _():
        copies(N_SUB - 2, stage=stage_s.at[0], sem=sem.at[0], wait=True)
        copies(N_SUB - 1, stage=stage_s.at[1], sem=sem.at[1], wait=True)


def _dispatch(cnt, cdst, h2, lp):
    assert SUB_PER_TILE % 2 == 0
    grid_spec = pltpu.PrefetchScalarGridSpec(
        num_scalar_prefetch=2,
        grid=(N_TILES,),
        in_specs=[
            pl.BlockSpec((TM, D_MODEL), lambda i, *_: (i, 0)),
            pl.BlockSpec((TM, LANES), lambda i, *_: (i, 0)),
        ],
        out_specs=pl.BlockSpec(memory_space=pl.ANY),
        scratch_shapes=[pltpu.VMEM((2, STAGE_ROWS, D_MODEL), BF), pltpu.SemaphoreType.DMA((2,))],
    )
    return pl.pallas_call(
        _dispatch_kernel,
        grid_spec=grid_spec,
        out_shape=jax.ShapeDtypeStruct((R_PAD, D_MODEL), BF),
        compiler_params=_params(("arbitrary",)),
        name="moe_dispatch",
    )(cnt, cdst, h2, lp)


def _expert_weights(te_ref, nt_ref, nxt_ref, w_hbm, wf_s, wb_s, sem):
    r = pl.program_id(0)

    def fetch(e):
        return pltpu.make_async_copy(w_hbm.at[0, e], wf_s, sem)

    @pl.when(r == 0)
    def _():
        fetch(te_ref[0]).start()

    first = jnp.logical_or(r == 0, te_ref[r] != te_ref[jnp.maximum(r - 1, 0)])

    @pl.when(jnp.logical_and(r < nt_ref[0], first))
    def _():
        fetch(te_ref[r]).wait()
        wb_s[...] = wf_s[...].astype(BF)

        @pl.when(nxt_ref[r] >= 0)
        def _():
            fetch(nxt_ref[r]).start()


def _moe_ffn_kernel(te_ref, nt_ref, nxt_ref, used_ref, x_ref, wgu_hbm, wd_hbm, o_ref,
                    wgu_f, wgu_b, wd_f, wd_b, sem):
    r = pl.program_id(0)
    _expert_weights(te_ref, nt_ref, nxt_ref, wgu_hbm, wgu_f, wgu_b, sem.at[0])
    _expert_weights(te_ref, nt_ref, nxt_ref, wd_hbm, wd_f, wd_b, sem.at[1])

    def ffn(rows):
        x = x_ref[rows, :]
        g = jnp.dot(x, wgu_b[:, :D_FF_EXPERT], preferred_element_type=F32)
        u = jnp.dot(x, wgu_b[:, D_FF_EXPERT:], preferred_element_type=F32)
        act = (_silu(g) * u).astype(BF)
        o_ref[rows, :] = jnp.dot(act, wd_b[...], preferred_element_type=F32).astype(BF)

    @pl.when(jnp.logical_and(r < nt_ref[0], used_ref[r] > TG // 2))
    def _():
        ffn(slice(0, TG))

    @pl.when(jnp.logical_and(r < nt_ref[0], used_ref[r] <= TG // 2))
    def _():
        ffn(slice(0, TG // 2))


def _moe_ffn(te, nt, nxt, used, rows, w_gu, w_down):
    tile_map = lambda r, te, nt, nxt, used: (jnp.minimum(r, nt[0] - 1), 0)
    grid_spec = pltpu.PrefetchScalarGridSpec(
        num_scalar_prefetch=4,
        grid=(NT_G,),
        in_specs=[pl.BlockSpec((TG, D_MODEL), tile_map),
                  pl.BlockSpec(memory_space=pl.ANY), pl.BlockSpec(memory_space=pl.ANY)],
        out_specs=pl.BlockSpec((TG, D_MODEL), tile_map),
        scratch_shapes=[
            pltpu.VMEM((D_MODEL, 2 * D_FF_EXPERT), F32), pltpu.VMEM((D_MODEL, 2 * D_FF_EXPERT), BF),
            pltpu.VMEM((D_FF_EXPERT, D_MODEL), F32), pltpu.VMEM((D_FF_EXPERT, D_MODEL), BF),
            pltpu.SemaphoreType.DMA((2,)),
        ],
    )
    return pl.pallas_call(
        _moe_ffn_kernel,
        grid_spec=grid_spec,
        out_shape=jax.ShapeDtypeStruct((R_PAD, D_MODEL), BF),
        compiler_params=_params(("arbitrary",)),
        name="moe_ffn",
    )(te, nt, nxt, used, rows, w_gu, w_down)


def _combine_kernel(cnt_ref, cdst_ref, ys_ref, lp_ref, rg_ref, x_ref, g2_ref, fg_ref,
                    oa_ref, ob_ref, stage_s, sem):
    i = pl.program_id(0)
    copies = functools.partial(_chunk_copies, cnt_ref=cnt_ref, cdst_ref=cdst_ref, rows_hbm=ys_ref, to_hbm=False)

    @pl.when(i == 0)
    def _():
        stage_s[...] = jnp.zeros_like(stage_s)
        for s in range(COMBINE_AHEAD):
            copies(s, stage=stage_s.at[s], sem=sem.at[s], wait=False)

    for k in range(SUB_PER_TILE):
        s = i * SUB_PER_TILE + k
        slot = k % COMBINE_SLOTS
        ahead = (k + COMBINE_AHEAD) % COMBINE_SLOTS
        rows = slice(k * ST, (k + 1) * ST)

        @pl.when(s + COMBINE_AHEAD < N_SUB)
        def _():
            copies(s + COMBINE_AHEAD, stage=stage_s.at[ahead], sem=sem.at[ahead], wait=False)

        copies(s, stage=stage_s.at[slot], sem=sem.at[slot], wait=True)

        lp = lp_ref[rows, :]
        r = lax.broadcasted_iota(jnp.int32, (ST, STAGE_ROWS), 1)
        staged = stage_s[slot]
        a = jnp.dot((r == lp[:, 0:1]).astype(BF), staged, preferred_element_type=F32)
        b = jnp.dot((r == lp[:, 1:2]).astype(BF), staged, preferred_element_type=F32)
        rg = rg_ref[rows, :]
        y = rg[:, 0:1] * a + rg[:, 1:2] * b
        xn = x_ref[rows, :] + g2_ref[...] * y
        out = _rms(xn) * fg_ref[...]

        @pl.when(i < CTX_TILES)
        def _():
            oa_ref[rows, :] = out

        @pl.when(i >= CTX_TILES)
        def _():
            ob_ref[rows, :] = out


def _combine(layer, cnt, cdst, ys, lp, rg, x, mod5, final_g):
    assert SUB_PER_TILE % COMBINE_SLOTS == 0
    grid_spec = pltpu.PrefetchScalarGridSpec(
        num_scalar_prefetch=2,
        grid=(N_TILES,),
        in_specs=[
            pl.BlockSpec(memory_space=pl.ANY),
            pl.BlockSpec((TM, LANES), lambda i, *_: (i, 0)),
            pl.BlockSpec((TM, LANES), lambda i, *_: (i, 0)),
            pl.BlockSpec((TM, D_MODEL), lambda i, *_: (i, 0)),
            _mod_spec(layer, 5),
            pl.BlockSpec((1, D_MODEL), lambda i, *_: (0, 0)),
        ],
        out_specs=[
            pl.BlockSpec((TM, D_MODEL), lambda i, *_: (jnp.minimum(i, CTX_TILES - 1), 0)),
            pl.BlockSpec((TM, D_MODEL), lambda i, *_: (jnp.maximum(i - CTX_TILES, 0), 0)),
        ],
        scratch_shapes=[pltpu.VMEM((COMBINE_SLOTS, STAGE_ROWS, D_MODEL), BF),
                        pltpu.SemaphoreType.DMA((COMBINE_SLOTS,))],
    )
    return pl.pallas_call(
        _combine_kernel,
        grid_spec=grid_spec,
        out_shape=[
            jax.ShapeDtypeStruct((N_CTX, D_MODEL), F32),
            jax.ShapeDtypeStruct((N_LAT, D_MODEL), F32),
        ],
        compiler_params=_params(("arbitrary",)),
        name="moe_combine",
    )(cnt, cdst, ys, lp, rg, x, mod5, final_g)


def _group_layout(n_tiles):
    n = n_tiles[:, :SUB_PER_TILE, :N_EXPERTS].reshape(N_SUB, N_EXPERTS)
    tiles = (jnp.sum(n, axis=0) + TG - 1) // TG
    tile_end = jnp.cumsum(tiles)
    region = (tile_end - tiles) * TG
    dst = region[None, :] + jnp.cumsum(n, axis=0) - n
    seg_end = jnp.cumsum(n, axis=1)
    seg = seg_end - n
    row = jnp.arange(STAGE_CHUNKS, dtype=jnp.int32) * SEG_ALIGN
    owner = jnp.sum((row[None, :, None] >= seg_end[:, None, :]).astype(jnp.int32), axis=-1)
    own = jnp.minimum(owner, N_EXPERTS - 1)[..., None] == jnp.arange(N_EXPERTS)
    cdst = jnp.sum(jnp.where(own, (dst - seg)[:, None, :], 0), axis=-1) + row[None, :]
    cnt = seg_end[:, -1] // SEG_ALIGN
    nt = tile_end[-1]
    tile_id = jnp.minimum(jnp.arange(NT_G, dtype=jnp.int32), nt - 1)
    te = jnp.sum((tile_id[:, None] >= tile_end[None, :]).astype(jnp.int32), axis=-1)
    after = jnp.sum(jnp.where(te[:, None] == jnp.arange(N_EXPERTS), tile_end[None, :], 0), axis=-1)
    nxt = jnp.where(after < nt, jnp.sum((after[:, None] >= tile_end[None, :]).astype(jnp.int32), axis=-1), -1)
    mine = te[:, None] == jnp.arange(N_EXPERTS)
    region_end = jnp.sum(jnp.where(mine, (region + jnp.sum(n, axis=0))[None, :], 0), axis=-1)
    used = jnp.clip(region_end - tile_id * TG, 0, TG)
    i32 = lambda a: a.astype(jnp.int32)
    return (i32(cnt), i32(cdst.reshape(N_SUB * STAGE_CHUNKS)), i32(te), i32(nt.reshape(1)), i32(nxt), i32(used))


def _rope_tables():
    p = np.arange(DEC_SEQ)
    row = (p // GRID_W).astype(np.float32)
    col = (p % GRID_W).astype(np.float32)
    half = QK_DIM // 4
    freqs = (ROPE_BASE ** (-np.arange(half, dtype=np.float32) / half)).astype(np.float32)
    lane = np.arange(V_DIM)
    f = freqs[lane & (half - 1)]
    use_col = (lane & (2 * half)) != 0
    ang = (np.where(use_col[None, :], col[:, None], row[:, None]) * f[None, :]).astype(np.float32)
    upper = (lane & half) != 0
    sin = np.sin(ang)
    return jnp.asarray(np.cos(ang), F32), jnp.asarray(np.where(upper[None, :], sin, -sin), F32)


def kernel(x_prompt, x_sample, cache_k, cache_v, c, c_ctx, w_ada, b_ada, norm_mix_g, norm_ffn_g,
           w_in, lambda_qk, subln_g, conv_w, w_out, w_gu_dense, w_down_dense, w_router,
           w_gu_moe, w_down_moe, final_g):
    assert DEPTH == 2
    xs = (x_prompt.reshape(N_CTX, D_MODEL), x_sample.reshape(N_LAT, D_MODEL))
    cond = jnp.concatenate([c_ctx[None, :], c, jnp.zeros((COND_ROWS - 1 - DEC_BATCH, D_MODEL), F32)], axis=0)
    mod5 = _ada(cond, w_ada, b_ada).reshape(DEPTH, COND_ROWS, N_MOD, 1, D_MODEL)
    cos_t, sin_t = _rope_tables()
    cache_kt = jnp.transpose(cache_k, (0, 1, 3, 4, 5, 2)).reshape(DEC_BATCH, DEPTH, ATT_WIDTH, PAST_LEN)
    cache_v4 = cache_v.reshape(DEC_BATCH, DEPTH, PAST_LEN * N_HEADS, V_DIM)
    g_mix = norm_mix_g.reshape(DEPTH, 1, D_MODEL)
    g_ffn = norm_ffn_g.reshape(DEPTH, 1, D_MODEL)
    sg = subln_g.reshape(DEPTH, 1, V_DIM)

    new_kv = None
    for layer in range(DEPTH):
        lam_init = 0.8 - 0.6 * math.exp(-0.3 * layer)
        q, v, kt, conv, nk, nv = _in_proj(layer, xs, mod5, g_mix, w_in, cos_t, sin_t, conv_w, new_kv)
        new_kv = (nk, nv)
        att = _attn_ctx(layer, q, kt, v, lambda_qk, sg, lam_init)
        att = _attn_lat(layer, q, kt, v, cache_kt, cache_v4, lambda_qk, sg, att, lam_init)
        if layer == 0:
            x1, h2 = _out_proj(layer, att, conv, w_out, xs, mod5, g_ffn)
            act = _dense_gu(h2, w_gu_dense)
            xs = _dense_down(layer, act, w_down_dense, x1, mod5)
        else:
            wr = jnp.pad(w_router[0], ((0, 0), (0, LANES - N_EXPERTS)))
            x1, h2, lp, rg, n_tiles = _out_proj(layer, att, conv, w_out, xs, mod5, g_ffn, wr)
            cnt, cdst, te, nt, nxt, used = _group_layout(n_tiles)
            xsort = _dispatch(cnt, cdst, h2, lp)
            ys = _moe_ffn(te, nt, nxt, used, xsort, w_gu_moe, w_down_moe)
            y_ctx, y_lat = _combine(layer, cnt, cdst, ys, lp, rg, x1, mod5, final_g.reshape(1, D_MODEL))
    nk, nv = new_kv
    new_k = jnp.transpose(nk.reshape(BATCH, DEPTH, N_HEADS, 2, QK_DIM, SEQ), (0, 1, 5, 2, 3, 4))
    new_v = nv.reshape(BATCH, DEPTH, SEQ, N_HEADS, V_DIM)
    return (y_ctx.reshape(BATCH, SEQ, D_MODEL), y_lat.reshape(DEC_BATCH, DEC_SEQ, D_MODEL), new_k, new_v)
```

```python
import functools
import math

import numpy as np
import jax
import jax.numpy as jnp
from jax import lax
from jax.experimental import pallas as pl
from jax.experimental.pallas import tpu as pltpu

D_MODEL = 1024
BATCH = 16
SEQ = 256
DEPTH = 2
DEC_BATCH = 4
DEC_SEQ = 1024
PAST_LEN = 512
GRID_W = 64
ATT_WIDTH = 512
CONV_WIDTH = 512
N_HEADS = 4
V_DIM = 128
QK_DIM = 64
ROPE_BASE = 10000.0
D_FF = 2816
N_EXPERTS = 8
D_FF_EXPERT = 1408
N_MOD = 6
NORM_EPS = 1e-6
Q_SCALE = QK_DIM ** -0.5 * math.log2(math.e)
IN_COLS = 3 * ATT_WIDTH + 3 * CONV_WIDTH

N_CTX = BATCH * SEQ
N_LAT = DEC_BATCH * DEC_SEQ
N_TOK = N_CTX + N_LAT
TM = 1024
N_TILES = N_TOK // TM
CTX_TILES = N_CTX // TM
SEQ_PER_TILE = TM // SEQ
CTX_SEQ_PER_STEP = 2
COND_ROWS = 8
TN_IN = 1024
N_IN_TILES = IN_COLS // TN_IN
ROW_CHUNK = 512
TN_FF = 1408
N_FF_TILES = D_FF // TN_FF
TN_DOWN = 1024
N_DOWN_TILES = D_MODEL // TN_DOWN
TN_ADA = 1536
TG = 512
ST = 256
SUB_PER_TILE = TM // ST
N_SUB = N_TOK // ST
CTX_SUB = N_CTX // ST
SEG_ALIGN = 16
STAGE_ROWS = 640
STAGE_CHUNKS = STAGE_ROWS // SEG_ALIGN
WAIT_PIECES = (32, 16, 8, 4, 2, 1)
COMBINE_SLOTS = 4
COMBINE_AHEAD = 2
NT_G = -(-(2 * N_TOK + N_SUB * N_EXPERTS * (SEG_ALIGN - 1) + N_EXPERTS * (TG - SEG_ALIGN)) // TG)
R_PAD = NT_G * TG
LANES = 128
VMEM_LIMIT = 60 * 1024 * 1024

BF = jnp.bfloat16
F32 = jnp.float32


def _params(sem, vmem=VMEM_LIMIT):
    return pltpu.CompilerParams(dimension_semantics=sem, vmem_limit_bytes=vmem)


def _mod_row(i):
    return jnp.where(i < CTX_TILES, 0, i - (CTX_TILES - 1))


def _mod_spec(layer, c, width=D_MODEL):
    return pl.BlockSpec((None, None, None, 1, width), lambda i, *_: (layer, _mod_row(i), c, 0, 0))


def _stream_specs(pair, width=D_MODEL):
    a = pl.BlockSpec((TM, width), lambda i, *_: (jnp.minimum(i, CTX_TILES - 1), 0))
    if pair:
        b = pl.BlockSpec((TM, width), lambda i, *_: (jnp.maximum(i - CTX_TILES, 0), 0))
    else:
        b = pl.BlockSpec((TM, width), lambda i, *_: (jnp.maximum(i, CTX_TILES), 0))
    return a, b


def _stream_rows(xa_ref, xb_ref, i):
    return jnp.where(i >= CTX_TILES, xb_ref[...], xa_ref[...])


def _silu(x):
    return x / (1.0 + jnp.exp(-x))


def _rms(x):
    return x * lax.rsqrt(jnp.mean(x * x, axis=-1, keepdims=True) + NORM_EPS)


def _ada_kernel(c_ref, w_ref, b_ref, o_ref):
    s = _silu(c_ref[...]).astype(BF)
    o_ref[...] = jnp.dot(s, w_ref[...].astype(BF), preferred_element_type=F32) + b_ref[...]


def _ada(cond, w_ada, b_ada):
    n = N_MOD * D_MODEL
    return pl.pallas_call(
        _ada_kernel,
        grid=(DEPTH, n // TN_ADA),
        in_specs=[
            pl.BlockSpec((COND_ROWS, D_MODEL), lambda l, j: (0, 0)),
            pl.BlockSpec((None, D_MODEL, TN_ADA), lambda l, j: (l, 0, j)),
            pl.BlockSpec((None, 1, TN_ADA), lambda l, j: (l, 0, j)),
        ],
        out_specs=pl.BlockSpec((None, COND_ROWS, TN_ADA), lambda l, j: (l, 0, j)),
        out_shape=jax.ShapeDtypeStruct((DEPTH, COND_ROWS, n), F32),
        compiler_params=_params(("arbitrary", "arbitrary")),
        name="ada_mod",
    )(cond, w_ada, b_ada.reshape(DEPTH, 1, n))


def _in_kernel(*refs, aliased):
    xa_ref, xb_ref, sh_ref, sc_ref, g_ref, w_ref, cos_ref, sin_ref, cw_ref = refs[:9]
    refs = refs[11:] if aliased else refs[9:]
    q_ref, v_ref, kt_ref, conv_ref, nk_ref, nv_ref, h_s, wb_s, gb_s, gc_s = refs
    i = pl.program_id(0)
    j = pl.program_id(1)
    lat = i >= CTX_TILES
    ctx = jnp.logical_not(lat)

    @pl.when(i == 0)
    def _():
        wb_s[j] = w_ref[...].astype(BF)

    chunks = [slice(c * ROW_CHUNK, (c + 1) * ROW_CHUNK) for c in range(TM // ROW_CHUNK)]
    seqs_per_chunk = ROW_CHUNK // SEQ

    def norm(rows):
        gain = g_ref[...] * (1.0 + sc_ref[...])
        x = jnp.where(lat, xb_ref[rows, :], xa_ref[rows, :])
        h_s[rows, :] = (_rms(x) * gain + sh_ref[...]).astype(BF)

    def proj(rows):
        acc = jnp.dot(h_s[rows, :], wb_s[j], preferred_element_type=F32)
        return acc[:, :ATT_WIDTH], acc[:, ATT_WIDTH:]

    def roped(a, rows):
        cos = jnp.concatenate([cos_ref[rows, :]] * N_HEADS, axis=1)
        sin = jnp.concatenate([sin_ref[rows, :]] * N_HEADS, axis=1)
        lane = lax.broadcasted_iota(jnp.int32, a.shape, 1)
        upper = (lane & (QK_DIM // 4)) != 0
        partner = jnp.where(upper, pltpu.roll(a, QK_DIM // 4, 1), pltpu.roll(a, ATT_WIDTH - QK_DIM // 4, 1))
        return a * cos + partner * sin

    @pl.when(jnp.logical_and(j == 0, lat))
    def _():
        for rows in chunks:
            norm(rows)
            q, k = proj(rows)
            q_ref[rows, :] = (roped(q, rows) * Q_SCALE).astype(BF)
            kt_ref[:, rows] = roped(k, rows).T.astype(BF)

    @pl.when(jnp.logical_and(j == 0, ctx))
    def _():
        for c, rows in enumerate(chunks):
            norm(rows)
            q, k = proj(rows)
            q_ref[rows, :] = (q * Q_SCALE).astype(BF)
            kt = k.T
            kt_ref[:, rows] = kt.astype(BF)
            for s in range(seqs_per_chunk):
                nk_ref[c * seqs_per_chunk + s] = kt[:, s * SEQ:(s + 1) * SEQ]

    @pl.when(jnp.logical_and(j == 1, lat))
    def _():
        for rows in chunks:
            v, gb = proj(rows)
            v_ref[rows, :] = v.astype(BF)
            gb_s[rows, :] = gb

    @pl.when(jnp.logical_and(j == 1, ctx))
    def _():
        for c, rows in enumerate(chunks):
            v, gb = proj(rows)
            v_ref[rows, :] = v.astype(BF)
            gb_s[rows, :] = gb
            for s in range(seqs_per_chunk):
                for h in range(N_HEADS):
                    nv_ref[c * seqs_per_chunk + s, pl.ds(h, SEQ, stride=N_HEADS), :] = (
                        v[s * SEQ:(s + 1) * SEQ, h * V_DIM:(h + 1) * V_DIM])

    @pl.when(j == 2)
    def _():
        for rows in chunks:
            gc, xin = proj(rows)
            gc_s[rows, :] = gc * xin
        u = gc_s[...]
        seq = jnp.where(lat, DEC_SEQ, SEQ)
        pos = lax.broadcasted_iota(jnp.int32, (TM, 1), 0) & (seq - 1)
        prev = jnp.where(pos == 0, 0.0, pltpu.roll(u, 1, 0))
        nxt = jnp.where(pos == seq - 1, 0.0, pltpu.roll(u, TM - 1, 0))
        cw = cw_ref[...]
        conv = prev * cw[0:1] + u * cw[1:2] + nxt * cw[2:3]
        conv_ref[...] = (gb_s[...] * conv).astype(BF)


def _in_proj(layer, xs, mod5, g_mix, w_in, cos_t, sin_t, conv_w, new_kv):
    pair = isinstance(xs, tuple)
    xa, xb = xs if pair else (xs, xs)
    spec_a, spec_b = _stream_specs(pair)
    ctx_i = lambda i: jnp.minimum(i, CTX_TILES - 1)
    in_specs = [
        spec_a, spec_b,
        _mod_spec(layer, 0), _mod_spec(layer, 1),
        pl.BlockSpec((None, 1, D_MODEL), lambda i, j: (layer, 0, 0)),
        pl.BlockSpec((None, D_MODEL, TN_IN), lambda i, j: (layer, 0, jnp.where(i == 0, j, N_IN_TILES - 1))),
        pl.BlockSpec((DEC_SEQ, V_DIM), lambda i, j: (0, 0)),
        pl.BlockSpec((DEC_SEQ, V_DIM), lambda i, j: (0, 0)),
        pl.BlockSpec((None, 3, CONV_WIDTH), lambda i, j: (layer, 0, 0)),
    ]
    args = [xa, xb, mod5, mod5, g_mix, w_in, cos_t, sin_t, conv_w]
    aliases = {}
    if new_kv is not None:
        in_specs += [pl.BlockSpec(memory_space=pl.ANY), pl.BlockSpec(memory_space=pl.ANY)]
        args += list(new_kv)
        aliases = {9: 4, 10: 5}
    row_tile = pl.BlockSpec((TM, ATT_WIDTH), lambda i, j: (i, 0))
    return pl.pallas_call(
        functools.partial(_in_kernel, aliased=new_kv is not None),
        grid=(N_TILES, N_IN_TILES),
        in_specs=in_specs,
        out_specs=[
            row_tile,
            row_tile,
            pl.BlockSpec((None, ATT_WIDTH, TM), lambda i, j: (i, 0, 0)),
            row_tile,
            pl.BlockSpec((SEQ_PER_TILE, None, ATT_WIDTH, SEQ), lambda i, j: (ctx_i(i), layer, 0, 0)),
            pl.BlockSpec((SEQ_PER_TILE, None, SEQ * N_HEADS, V_DIM), lambda i, j: (ctx_i(i), layer, 0, 0)),
        ],
        out_shape=[
            jax.ShapeDtypeStruct((N_TOK, ATT_WIDTH), BF),
            jax.ShapeDtypeStruct((N_TOK, ATT_WIDTH), BF),
            jax.ShapeDtypeStruct((N_TILES, ATT_WIDTH, TM), BF),
            jax.ShapeDtypeStruct((N_TOK, CONV_WIDTH), BF),
            jax.ShapeDtypeStruct((BATCH, DEPTH, ATT_WIDTH, SEQ), F32),
            jax.ShapeDtypeStruct((BATCH, DEPTH, SEQ * N_HEADS, V_DIM), F32),
        ],
        scratch_shapes=[
            pltpu.VMEM((TM, D_MODEL), BF),
            pltpu.VMEM((N_IN_TILES, D_MODEL, TN_IN), BF),
            pltpu.VMEM((TM, CONV_WIDTH), F32),
            pltpu.VMEM((TM, CONV_WIDTH), F32),
        ],
        input_output_aliases=aliases,
        compiler_params=_params(("arbitrary", "arbitrary")),
        name=f"in_proj_l{layer}",
    )(*args)


def _lambda(lq_ref, lam_init):
    lq = lq_ref[...]
    a = jnp.exp(jnp.sum(lq[0:1] * lq[1:2], axis=-1, keepdims=True))
    b = jnp.exp(jnp.sum(lq[2:3] * lq[3:4], axis=-1, keepdims=True))
    return a - b + lam_init


def _head_norm(o, sg, lam_init):
    return _rms(o) * sg * (1.0 - lam_init)


def _attn_ctx_kernel(q_ref, kt_ref, v_ref, lq_ref, sg_ref, o_ref, sc_s, *, lam_init):
    lam = _lambda(lq_ref, lam_init)
    sg = sg_ref[...]

    units = [(b, h) for b in range(CTX_SEQ_PER_STEP) for h in range(N_HEADS)]

    def scores(u):
        b, h = units[u]
        pos = slice(b * SEQ, (b + 1) * SEQ)
        for s in range(2):
            d = slice(h * V_DIM + s * QK_DIM, h * V_DIM + (s + 1) * QK_DIM)
            sc_s[u % 2, s] = jnp.dot(q_ref[pos, d], kt_ref[d, pos], preferred_element_type=F32)

    def finish(u):
        b, h = units[u]
        pos = slice(b * SEQ, (b + 1) * SEQ)
        cols = slice(h * V_DIM, (h + 1) * V_DIM)
        v = v_ref[pos, cols]
        outs = []
        for s in range(2):
            sc = sc_s[u % 2, s]
            e = jnp.exp2(sc - jnp.max(sc, axis=-1, keepdims=True))
            r = 1.0 / jnp.sum(e, axis=-1, keepdims=True)
            outs.append(jnp.dot(e.astype(BF), v, preferred_element_type=F32) * r)
        o = outs[0] - lam * outs[1]
        o_ref[pos, cols] = _head_norm(o, sg, lam_init).astype(BF)

    scores(0)
    for u in range(len(units)):
        if u + 1 < len(units):
            scores(u + 1)
        finish(u)


def _attn_ctx(layer, q, kt, v, lambda_qk, subln_g, lam_init):
    rows = CTX_SEQ_PER_STEP * SEQ
    per_tile = TM // rows
    return pl.pallas_call(
        functools.partial(_attn_ctx_kernel, lam_init=lam_init),
        grid=(N_CTX // rows,),
        in_specs=[
            pl.BlockSpec((rows, ATT_WIDTH), lambda b: (b, 0)),
            pl.BlockSpec((None, ATT_WIDTH, rows), lambda b: (b // per_tile, 0, b % per_tile)),
            pl.BlockSpec((rows, ATT_WIDTH), lambda b: (b, 0)),
            pl.BlockSpec((None, 4, QK_DIM), lambda b: (layer, 0, 0)),
            pl.BlockSpec((None, 1, V_DIM), lambda b: (layer, 0, 0)),
        ],
        out_specs=pl.BlockSpec((rows, ATT_WIDTH), lambda b: (b, 0)),
        out_shape=jax.ShapeDtypeStruct((N_TOK, ATT_WIDTH), BF),
        scratch_shapes=[pltpu.VMEM((2, 2, SEQ, SEQ), F32)],
        compiler_params=_params(("arbitrary",)),
        name=f"attn_ctx_l{layer}",
    )(q, kt, v, lambda_qk, subln_g)


TQ = 256


def _attn_lat_kernel(q_ref, kt_ref, v_ref, ckt_ref, cv_ref, lq_ref, sg_ref, att_in_ref, o_ref, sc_s, *, lam_init):
    del att_in_ref
    lam = _lambda(lq_ref, lam_init)
    sg = sg_ref[...]

    def scores(h, s):
        d = slice(h * V_DIM + s * QK_DIM, h * V_DIM + (s + 1) * QK_DIM)
        q = q_ref[:, d]
        sc_s[h % 2, s, :, :PAST_LEN] = jnp.dot(q, ckt_ref[d, :].astype(BF), preferred_element_type=F32)
        sc_s[h % 2, s, :, PAST_LEN:] = jnp.dot(q, kt_ref[d, :], preferred_element_type=F32)

    def softmax(h, s):
        sc = sc_s[h % 2, s]
        e = jnp.exp2(sc - jnp.max(sc, axis=-1, keepdims=True))
        return e, 1.0 / jnp.sum(e, axis=-1, keepdims=True)

    def finish(h, p1, p2):
        cols = slice(h * V_DIM, (h + 1) * V_DIM)
        e = jnp.concatenate([p1[0].astype(BF), p2[0].astype(BF)], axis=0)
        vc = cv_ref[pl.ds(h, PAST_LEN, stride=N_HEADS), :].astype(BF)
        pv = jnp.dot(e[:, :PAST_LEN], vc, preferred_element_type=F32)
        pv = pv + jnp.dot(e[:, PAST_LEN:], v_ref[:, cols], preferred_element_type=F32)
        o = pv[:TQ] * p1[1] - pv[TQ:] * (lam * p2[1])
        o_ref[:, cols] = _head_norm(o, sg, lam_init).astype(BF)

    scores(0, 0)
    scores(0, 1)
    for h in range(N_HEADS):
        more = h + 1 < N_HEADS
        if more:
            scores(h + 1, 0)
        p1 = softmax(h, 0)
        if more:
            scores(h + 1, 1)
        finish(h, p1, softmax(h, 1))


def _attn_lat(layer, q, kt, v, cache_kt, cache_v, lambda_qk, subln_g, att, lam_init):
    nqb = DEC_SEQ // TQ
    q0 = N_CTX // TQ
    return pl.pallas_call(
        functools.partial(_attn_lat_kernel, lam_init=lam_init),
        grid=(DEC_BATCH, nqb),
        in_specs=[
            pl.BlockSpec((TQ, ATT_WIDTH), lambda b, t: (q0 + b * nqb + t, 0)),
            pl.BlockSpec((None, ATT_WIDTH, DEC_SEQ), lambda b, t: (CTX_TILES + b, 0, 0)),
            pl.BlockSpec((DEC_SEQ, ATT_WIDTH), lambda b, t: (CTX_TILES + b, 0)),
            pl.BlockSpec((None, None, ATT_WIDTH, PAST_LEN), lambda b, t: (b, layer, 0, 0)),
            pl.BlockSpec((None, None, PAST_LEN * N_HEADS, V_DIM), lambda b, t: (b, layer, 0, 0)),
            pl.BlockSpec((None, 4, QK_DIM), lambda b, t: (layer, 0, 0)),
            pl.BlockSpec((None, 1, V_DIM), lambda b, t: (layer, 0, 0)),
            pl.BlockSpec(memory_space=pl.ANY),
        ],
        out_specs=pl.BlockSpec((TQ, ATT_WIDTH), lambda b, t: (q0 + b * nqb + t, 0)),
        out_shape=jax.ShapeDtypeStruct((N_TOK, ATT_WIDTH), BF),
        scratch_shapes=[pltpu.VMEM((2, 2, TQ, PAST_LEN + DEC_SEQ), F32)],
        input_output_aliases={7: 0},
        compiler_params=_params(("arbitrary", "arbitrary")),
        name=f"attn_lat_l{layer}",
    )(q, kt, v, cache_kt, cache_v, lambda_qk, subln_g, att)


def _out_kernel(att_ref, conv_ref, w_ref, xa_ref, xb_ref, g1_ref, sh_ref, sc_ref, gf_ref, *rest, route):
    if route:
        wr_ref, xo_ref, h2_ref, lp_ref, rg_ref, n_ref, wb_s = rest
    else:
        xo_ref, h2_ref, wb_s = rest
    i = pl.program_id(0)

    @pl.when(i == 0)
    def _():
        wb_s[...] = w_ref[...].astype(BF)

    mo = jnp.dot(att_ref[...], wb_s[:ATT_WIDTH, :], preferred_element_type=F32)
    mo = mo + jnp.dot(conv_ref[...], wb_s[ATT_WIDTH:, :], preferred_element_type=F32)
    xn = _stream_rows(xa_ref, xb_ref, i) + g1_ref[...] * mo
    xo_ref[...] = xn
    h2 = (_rms(xn) * gf_ref[...]) * (1.0 + sc_ref[...]) + sh_ref[...]
    h2b = h2.astype(BF)
    h2_ref[...] = h2b
    if route:
        _route(h2b, wr_ref, lp_ref, rg_ref, n_ref)


def _out_proj(layer, att, conv, w_out, xs, mod5, g_ffn, w_router_pad=None):
    pair = isinstance(xs, tuple)
    xa, xb = xs if pair else (xs, xs)
    spec_a, spec_b = _stream_specs(pair)
    row_spec = pl.BlockSpec((TM, D_MODEL), lambda i: (i, 0))
    lane_spec = pl.BlockSpec((TM, LANES), lambda i: (i, 0))
    in_specs = [
        pl.BlockSpec((TM, ATT_WIDTH), lambda i: (i, 0)),
        pl.BlockSpec((TM, CONV_WIDTH), lambda i: (i, 0)),
        pl.BlockSpec((None, D_MODEL, D_MODEL), lambda i: (layer, 0, 0)),
        spec_a, spec_b,
        _mod_spec(layer, 2), _mod_spec(layer, 3), _mod_spec(layer, 4),
        pl.BlockSpec((None, 1, D_MODEL), lambda i: (layer, 0, 0)),
    ]
    args = [att, conv, w_out, xa, xb, mod5, mod5, mod5, g_ffn]
    out_specs = [row_spec, row_spec]
    out_shape = [jax.ShapeDtypeStruct((N_TOK, D_MODEL), F32), jax.ShapeDtypeStruct((N_TOK, D_MODEL), BF)]
    route = w_router_pad is not None
    if route:
        in_specs.append(pl.BlockSpec((D_MODEL, LANES), lambda i: (0, 0)))
        args.append(w_router_pad)
        out_specs += [lane_spec, lane_spec, pl.BlockSpec((None, 8, LANES), lambda i: (i, 0, 0))]
        out_shape += [jax.ShapeDtypeStruct((N_TOK, LANES), jnp.int32), jax.ShapeDtypeStruct((N_TOK, LANES), F32),
                      jax.ShapeDtypeStruct((N_TILES, 8, LANES), jnp.int32)]
    return pl.pallas_call(
        functools.partial(_out_kernel, route=route),
        grid=(N_TILES,),
        in_specs=in_specs,
        out_specs=out_specs,
        out_shape=out_shape,
        scratch_shapes=[pltpu.VMEM((D_MODEL, D_MODEL), BF)],
        compiler_params=_params(("arbitrary",)),
        name=f"out_proj_l{layer}",
    )(*args)


def _gu_kernel(h_ref, wg_ref, wu_ref, o_ref, wb_s):
    i = pl.program_id(0)
    j = pl.program_id(1)

    @pl.when(i == 0)
    def _():
        wb_s[j, :, :TN_FF] = wg_ref[...].astype(BF)
        wb_s[j, :, TN_FF:] = wu_ref[...].astype(BF)

    gu = jnp.dot(h_ref[...], wb_s[j], preferred_element_type=F32)
    o_ref[...] = (_silu(gu[:, :TN_FF]) * gu[:, TN_FF:]).astype(BF)


def _dense_gu(h2, w_gu):
    nj = N_FF_TILES
    return pl.pallas_call(
        _gu_kernel,
        grid=(N_TILES, nj),
        in_specs=[
            pl.BlockSpec((TM, D_MODEL), lambda i, j: (i, 0)),
            pl.BlockSpec((None, D_MODEL, TN_FF), lambda i, j: (0, 0, jnp.where(i == 0, j, nj - 1)),
                         pipeline_mode=pl.Buffered(1)),
            pl.BlockSpec((None, D_MODEL, TN_FF), lambda i, j: (0, 0, jnp.where(i == 0, j, nj - 1) + nj),
                         pipeline_mode=pl.Buffered(1)),
        ],
        out_specs=pl.BlockSpec((TM, TN_FF), lambda i, j: (i, j)),
        out_shape=jax.ShapeDtypeStruct((N_TOK, D_FF), BF),
        scratch_shapes=[pltpu.VMEM((nj, D_MODEL, 2 * TN_FF), BF)],
        compiler_params=_params(("arbitrary", "arbitrary")),
        name="dense_gu",
    )(h2, w_gu, w_gu)


def _down_kernel(a_ref, w_ref, x_ref, g2_ref, o_ref, wb_s):
    j = pl.program_id(1)

    @pl.when(pl.program_id(0) == 0)
    def _():
        wb_s[j] = w_ref[...].astype(BF)

    y = jnp.dot(a_ref[...], wb_s[j], preferred_element_type=F32)
    o_ref[...] = x_ref[...] + g2_ref[...] * y


def _dense_down(layer, act, w_down, x, mod5):
    nj = N_DOWN_TILES
    return pl.pallas_call(
        _down_kernel,
        grid=(N_TILES, nj),
        in_specs=[
            pl.BlockSpec((TM, D_FF), lambda i, j: (i, 0)),
            pl.BlockSpec((None, D_FF, TN_DOWN), lambda i, j: (0, 0, jnp.where(i == 0, j, nj - 1)),
                         pipeline_mode=pl.Buffered(1)),
            pl.BlockSpec((TM, TN_DOWN), lambda i, j: (i, j)),
            pl.BlockSpec((None, None, None, 1, TN_DOWN), lambda i, j: (layer, _mod_row(i), 5, 0, j)),
        ],
        out_specs=pl.BlockSpec((TM, TN_DOWN), lambda i, j: (i, j)),
        out_shape=jax.ShapeDtypeStruct((N_TOK, D_MODEL), F32),
        scratch_shapes=[pltpu.VMEM((nj, D_FF, TN_DOWN), BF)],
        compiler_params=_params(("arbitrary", "arbitrary")),
        name="dense_down",
    )(act, w_down, x, mod5)


def _route(h, wr_ref, lp_ref, rg_ref, n_ref):
    logits = jnp.dot(h, wr_ref[...].astype(BF), preferred_element_type=F32)
    lane = lax.broadcasted_iota(jnp.int32, logits.shape, 1)
    lg = jnp.where(lane < N_EXPERTS, logits, -jnp.inf)
    m1 = jnp.max(lg, axis=-1, keepdims=True)
    i1 = jnp.min(jnp.where(lg == m1, lane, LANES), axis=-1, keepdims=True)
    lg2 = jnp.where(lane == i1, -jnp.inf, lg)
    m2 = jnp.max(lg2, axis=-1, keepdims=True)
    i2 = jnp.min(jnp.where(lg2 == m2, lane, LANES), axis=-1, keepdims=True)
    e2 = jnp.exp(m2 - m1)
    w1 = 1.0 / (1.0 + e2)
    w2 = e2 / (1.0 + e2)

    sel1 = lane == i1
    sel2 = lane == i2
    onehot = jnp.logical_or(sel1, sel2)
    rows = lax.broadcasted_iota(jnp.int32, (TM, TM), 0)
    colsi = lax.broadcasted_iota(jnp.int32, (TM, TM), 1)
    earlier = jnp.logical_and(colsi < rows, (colsi // ST) == (rows // ST))
    before = jnp.dot(earlier.astype(BF), onehot.astype(BF), preferred_element_type=F32)
    onehot_f = onehot.astype(F32)
    counts = [jnp.sum(onehot_f[s * ST:(s + 1) * ST], axis=0, keepdims=True) for s in range(SUB_PER_TILE)]
    counts = jnp.concatenate(counts + [jnp.zeros((8 - SUB_PER_TILE, LANES), F32)], axis=0).astype(jnp.int32)
    seg_len = ((counts + (SEG_ALIGN - 1)) // SEG_ALIGN) * SEG_ALIGN
    n_ref[...] = seg_len
    la = lax.broadcasted_iota(jnp.int32, (LANES, LANES), 0)
    lb = lax.broadcasted_iota(jnp.int32, (LANES, LANES), 1)
    seg_start = jnp.dot(seg_len.astype(F32).astype(BF), (la < lb).astype(BF), preferred_element_type=F32)
    start = jnp.concatenate(
        [jnp.broadcast_to(seg_start[s:s + 1], (ST, LANES)) for s in range(SUB_PER_TILE)], axis=0)
    where = before + start
    lp1 = jnp.sum(jnp.where(sel1, where, 0.0), axis=-1, keepdims=True).astype(jnp.int32)
    lp2 = jnp.sum(jnp.where(sel2, where, 0.0), axis=-1, keepdims=True).astype(jnp.int32)
    lp_ref[...] = jnp.where(lane == 0, lp1, jnp.where(lane == 1, lp2, 0))
    rg_ref[...] = jnp.where(lane == 0, w1, jnp.where(lane == 1, w2, 0.0))


def _chunk_copies(s, cnt_ref, cdst_ref, stage, rows_hbm, sem, *, to_hbm, wait):
    def copy(v, h):
        return pltpu.make_async_copy(v, h, sem) if to_hbm else pltpu.make_async_copy(h, v, sem)

    if wait:
        for z in WAIT_PIECES:
            @pl.when((cnt_ref[s] & z) != 0)
            def _():
                copy(stage.at[pl.ds(0, z * SEG_ALIGN)], rows_hbm.at[pl.ds(0, z * SEG_ALIGN)]).wait()
        return

    def one(c, carry):
        v = stage.at[pl.ds(pl.multiple_of(c * SEG_ALIGN, SEG_ALIGN), SEG_ALIGN)]
        h = rows_hbm.at[pl.ds(pl.multiple_of(cdst_ref[s * STAGE_CHUNKS + c], SEG_ALIGN), SEG_ALIGN)]
        copy(v, h).start()
        return carry

    lax.fori_loop(0, cnt_ref[s], one, 0)


def _dispatch_kernel(cnt_ref, cdst_ref, h_ref, lp_ref, xs_ref, stage_s, sem):
    copies = functools.partial(_chunk_copies, cnt_ref=cnt_ref, cdst_ref=cdst_ref, rows_hbm=xs_ref, to_hbm=True)
    for k in range(SUB_PER_TILE):
        s = pl.program_id(0) * SUB_PER_TILE + k
        slot = k % 2
        rows = slice(k * ST, (k + 1) * ST)

        @pl.when(s >= 2)
        def _():
            copies(s - 2, stage=stage_s.at[slot], sem=sem.at[slot], wait=True)

        lpt = lp_ref[rows, :].T
        r = lax.broadcasted_iota(jnp.int32, (STAGE_ROWS, ST), 0)
        pick = jnp.logical_or(r == lpt[0:1, :], r == lpt[1:2, :]).astype(BF)
        stage_s[slot] = jnp.dot(pick, h_ref[rows, :], preferred_element_type=F32).astype(BF)
        copies(s, stage=stage_s.at[slot], sem=sem.at[slot], wait=False)

    @pl.when(pl.program_id(0) == N_TILES - 1)
    def _():
        copies(N_SUB - 2, stage=stage_s.at[0], sem=sem.at[0], wait=True)
        copies(N_SUB - 1, stage=stage_s.at[1], sem=sem.at[1], wait=True)


def _dispatch(cnt, cdst, h2, lp):
    assert SUB_PER_TILE % 2 == 0
    grid_spec = pltpu.PrefetchScalarGridSpec(
        num_scalar_prefetch=2,
        grid=(N_TILES,),
        in_specs=[
            pl.BlockSpec((TM, D_MODEL), lambda i, *_: (i, 0)),
            pl.BlockSpec((TM, LANES), lambda i, *_: (i, 0)),
        ],
        out_specs=pl.BlockSpec(memory_space=pl.ANY),
        scratch_shapes=[pltpu.VMEM((2, STAGE_ROWS, D_MODEL), BF), pltpu.SemaphoreType.DMA((2,))],
    )
    return pl.pallas_call(
        _dispatch_kernel,
        grid_spec=grid_spec,
        out_shape=jax.ShapeDtypeStruct((R_PAD, D_MODEL), BF),
        compiler_params=_params(("arbitrary",)),
        name="moe_dispatch",
    )(cnt, cdst, h2, lp)


def _expert_weights(te_ref, nt_ref, nxt_ref, w_hbm, wf_s, wb_s, sem):
    r = pl.program_id(0)

    def fetch(e):
        return pltpu.make_async_copy(w_hbm.at[0, e], wf_s, sem)

    @pl.when(r == 0)
    def _():
        fetch(te_ref[0]).start()

    first = jnp.logical_or(r == 0, te_ref[r] != te_ref[jnp.maximum(r - 1, 0)])

    @pl.when(jnp.logical_and(r < nt_ref[0], first))
    def _():
        fetch(te_ref[r]).wait()
        wb_s[...] = wf_s[...].astype(BF)

        @pl.when(nxt_ref[r] >= 0)
        def _():
            fetch(nxt_ref[r]).start()


def _moe_ffn_kernel(te_ref, nt_ref, nxt_ref, used_ref, x_ref, wgu_hbm, wd_hbm, o_ref,
                    wgu_f, wgu_b, wd_f, wd_b, sem):
    r = pl.program_id(0)
    _expert_weights(te_ref, nt_ref, nxt_ref, wgu_hbm, wgu_f, wgu_b, sem.at[0])
    _expert_weights(te_ref, nt_ref, nxt_ref, wd_hbm, wd_f, wd_b, sem.at[1])

    def ffn(rows):
        gu = jnp.dot(x_ref[rows, :], wgu_b[...], preferred_element_type=F32)
        act = (_silu(gu[:, :D_FF_EXPERT]) * gu[:, D_FF_EXPERT:]).astype(BF)
        o_ref[rows, :] = jnp.dot(act, wd_b[...], preferred_element_type=F32).astype(BF)

    @pl.when(jnp.logical_and(r < nt_ref[0], used_ref[r] > TG // 2))
    def _():
        ffn(slice(0, TG))

    @pl.when(jnp.logical_and(r < nt_ref[0], used_ref[r] <= TG // 2))
    def _():
        ffn(slice(0, TG // 2))


def _moe_ffn(te, nt, nxt, used, rows, w_gu, w_down):
    tile_map = lambda r, te, nt, nxt, used: (jnp.minimum(r, nt[0] - 1), 0)
    grid_spec = pltpu.PrefetchScalarGridSpec(
        num_scalar_prefetch=4,
        grid=(NT_G,),
        in_specs=[pl.BlockSpec((TG, D_MODEL), tile_map),
                  pl.BlockSpec(memory_space=pl.ANY), pl.BlockSpec(memory_space=pl.ANY)],
        out_specs=pl.BlockSpec((TG, D_MODEL), tile_map),
        scratch_shapes=[
            pltpu.VMEM((D_MODEL, 2 * D_FF_EXPERT), F32), pltpu.VMEM((D_MODEL, 2 * D_FF_EXPERT), BF),
            pltpu.VMEM((D_FF_EXPERT, D_MODEL), F32), pltpu.VMEM((D_FF_EXPERT, D_MODEL), BF),
            pltpu.SemaphoreType.DMA((2,)),
        ],
    )
    return pl.pallas_call(
        _moe_ffn_kernel,
        grid_spec=grid_spec,
        out_shape=jax.ShapeDtypeStruct((R_PAD, D_MODEL), BF),
        compiler_params=_params(("arbitrary",)),
        name="moe_ffn",
    )(te, nt, nxt, used, rows, w_gu, w_down)


def _combine_kernel(cnt_ref, cdst_ref, ys_ref, lp_ref, rg_ref, x_ref, g2_ref, fg_ref,
                    oa_ref, ob_ref, stage_s, sem):
    i = pl.program_id(0)
    copies = functools.partial(_chunk_copies, cnt_ref=cnt_ref, cdst_ref=cdst_ref, rows_hbm=ys_ref, to_hbm=False)

    @pl.when(i == 0)
    def _():
        stage_s[...] = jnp.zeros_like(stage_s)
        for s in range(COMBINE_AHEAD):
            copies(s, stage=stage_s.at[s], sem=sem.at[s], wait=False)

    for k in range(SUB_PER_TILE):
        s = i * SUB_PER_TILE + k
        slot = k % COMBINE_SLOTS
        ahead = (k + COMBINE_AHEAD) % COMBINE_SLOTS
        rows = slice(k * ST, (k + 1) * ST)

        @pl.when(s + COMBINE_AHEAD < N_SUB)
        def _():
            copies(s + COMBINE_AHEAD, stage=stage_s.at[ahead], sem=sem.at[ahead], wait=False)

        copies(s, stage=stage_s.at[slot], sem=sem.at[slot], wait=True)

        lp = lp_ref[rows, :]
        r = lax.broadcasted_iota(jnp.int32, (ST, STAGE_ROWS), 1)
        staged = stage_s[slot]
        a = jnp.dot((r == lp[:, 0:1]).astype(BF), staged, preferred_element_type=F32)
        b = jnp.dot((r == lp[:, 1:2]).astype(BF), staged, preferred_element_type=F32)
        rg = rg_ref[rows, :]
        y = rg[:, 0:1] * a + rg[:, 1:2] * b
        xn = x_ref[rows, :] + g2_ref[...] * y
        out = _rms(xn) * fg_ref[...]

        @pl.when(i < CTX_TILES)
        def _():
            oa_ref[rows, :] = out

        @pl.when(i >= CTX_TILES)
        def _():
            ob_ref[rows, :] = out


def _combine(layer, cnt, cdst, ys, lp, rg, x, mod5, final_g):
    assert SUB_PER_TILE % COMBINE_SLOTS == 0
    grid_spec = pltpu.PrefetchScalarGridSpec(
        num_scalar_prefetch=2,
        grid=(N_TILES,),
        in_specs=[
            pl.BlockSpec(memory_space=pl.ANY),
            pl.BlockSpec((TM, LANES), lambda i, *_: (i, 0)),
            pl.BlockSpec((TM, LANES), lambda i, *_: (i, 0)),
            pl.BlockSpec((TM, D_MODEL), lambda i, *_: (i, 0)),
            _mod_spec(layer, 5),
            pl.BlockSpec((1, D_MODEL), lambda i, *_: (0, 0)),
        ],
        out_specs=[
            pl.BlockSpec((TM, D_MODEL), lambda i, *_: (jnp.minimum(i, CTX_TILES - 1), 0)),
            pl.BlockSpec((TM, D_MODEL), lambda i, *_: (jnp.maximum(i - CTX_TILES, 0), 0)),
        ],
        scratch_shapes=[pltpu.VMEM((COMBINE_SLOTS, STAGE_ROWS, D_MODEL), BF),
                        pltpu.SemaphoreType.DMA((COMBINE_SLOTS,))],
    )
    return pl.pallas_call(
        _combine_kernel,
        grid_spec=grid_spec,
        out_shape=[
            jax.ShapeDtypeStruct((N_CTX, D_MODEL), F32),
            jax.ShapeDtypeStruct((N_LAT, D_MODEL), F32),
        ],
        compiler_params=_params(("arbitrary",)),
        name="moe_combine",
    )(cnt, cdst, ys, lp, rg, x, mod5, final_g)


def _group_layout(n_tiles):
    n = n_tiles[:, :SUB_PER_TILE, :N_EXPERTS].reshape(N_SUB, N_EXPERTS)
    tiles = (jnp.sum(n, axis=0) + TG - 1) // TG
    tile_end = jnp.cumsum(tiles)
    region = (tile_end - tiles) * TG
    dst = region[None, :] + jnp.cumsum(n, axis=0) - n
    seg_end = jnp.cumsum(n, axis=1)
    seg = seg_end - n
    row = jnp.arange(STAGE_CHUNKS, dtype=jnp.int32) * SEG_ALIGN
    owner = jnp.sum((row[None, :, None] >= seg_end[:, None, :]).astype(jnp.int32), axis=-1)
    own = jnp.minimum(owner, N_EXPERTS - 1)[..., None] == jnp.arange(N_EXPERTS)
    cdst = jnp.sum(jnp.where(own, (dst - seg)[:, None, :], 0), axis=-1) + row[None, :]
    cnt = seg_end[:, -1] // SEG_ALIGN
    nt = tile_end[-1]
    tile_id = jnp.minimum(jnp.arange(NT_G, dtype=jnp.int32), nt - 1)
    te = jnp.sum((tile_id[:, None] >= tile_end[None, :]).astype(jnp.int32), axis=-1)
    after = jnp.sum(jnp.where(te[:, None] == jnp.arange(N_EXPERTS), tile_end[None, :], 0), axis=-1)
    nxt = jnp.where(after < nt, jnp.sum((after[:, None] >= tile_end[None, :]).astype(jnp.int32), axis=-1), -1)
    mine = te[:, None] == jnp.arange(N_EXPERTS)
    region_end = jnp.sum(jnp.where(mine, (region + jnp.sum(n, axis=0))[None, :], 0), axis=-1)
    used = jnp.clip(region_end - tile_id * TG, 0, TG)
    i32 = lambda a: a.astype(jnp.int32)
    return (i32(cnt), i32(cdst.reshape(N_SUB * STAGE_CHUNKS)), i32(te), i32(nt.reshape(1)), i32(nxt), i32(used))


def _rope_tables():
    p = np.arange(DEC_SEQ)
    row = (p // GRID_W).astype(np.float32)
    col = (p % GRID_W).astype(np.float32)
    half = QK_DIM // 4
    freqs = (ROPE_BASE ** (-np.arange(half, dtype=np.float32) / half)).astype(np.float32)
    lane = np.arange(V_DIM)
    f = freqs[lane & (half - 1)]
    use_col = (lane & (2 * half)) != 0
    ang = (np.where(use_col[None, :], col[:, None], row[:, None]) * f[None, :]).astype(np.float32)
    upper = (lane & half) != 0
    sin = np.sin(ang)
    return jnp.asarray(np.cos(ang), F32), jnp.asarray(np.where(upper[None, :], sin, -sin), F32)


def kernel(x_prompt, x_sample, cache_k, cache_v, c, c_ctx, w_ada, b_ada, norm_mix_g, norm_ffn_g,
           w_in, lambda_qk, subln_g, conv_w, w_out, w_gu_dense, w_down_dense, w_router,
           w_gu_moe, w_down_moe, final_g):
    assert DEPTH == 2
    xs = (x_prompt.reshape(N_CTX, D_MODEL), x_sample.reshape(N_LAT, D_MODEL))
    cond = jnp.concatenate([c_ctx[None, :], c, jnp.zeros((COND_ROWS - 1 - DEC_BATCH, D_MODEL), F32)], axis=0)
    mod5 = _ada(cond, w_ada, b_ada).reshape(DEPTH, COND_ROWS, N_MOD, 1, D_MODEL)
    cos_t, sin_t = _rope_tables()
    cache_kt = jnp.transpose(cache_k, (0, 1, 3, 4, 5, 2)).reshape(DEC_BATCH, DEPTH, ATT_WIDTH, PAST_LEN)
    cache_v4 = cache_v.reshape(DEC_BATCH, DEPTH, PAST_LEN * N_HEADS, V_DIM)
    g_mix = norm_mix_g.reshape(DEPTH, 1, D_MODEL)
    g_ffn = norm_ffn_g.reshape(DEPTH, 1, D_MODEL)
    sg = subln_g.reshape(DEPTH, 1, V_DIM)

    new_kv = None
    for layer in range(DEPTH):
        lam_init = 0.8 - 0.6 * math.exp(-0.3 * layer)
        q, v, kt, conv, nk, nv = _in_proj(layer, xs, mod5, g_mix, w_in, cos_t, sin_t, conv_w, new_kv)
        new_kv = (nk, nv)
        att = _attn_ctx(layer, q, kt, v, lambda_qk, sg, lam_init)
        att = _attn_lat(layer, q, kt, v, cache_kt, cache_v4, lambda_qk, sg, att, lam_init)
        if layer == 0:
            x1, h2 = _out_proj(layer, att, conv, w_out, xs, mod5, g_ffn)
            act = _dense_gu(h2, w_gu_dense)
            xs = _dense_down(layer, act, w_down_dense, x1, mod5)
        else:
            wr = jnp.pad(w_router[0], ((0, 0), (0, LANES - N_EXPERTS)))
            x1, h2, lp, rg, n_tiles = _out_proj(layer, att, conv, w_out, xs, mod5, g_ffn, wr)
            cnt, cdst, te, nt, nxt, used = _group_layout(n_tiles)
            xsort = _dispatch(cnt, cdst, h2, lp)
            ys = _moe_ffn(te, nt, nxt, used, xsort, w_gu_moe, w_down_moe)
            y_ctx, y_lat = _combine(layer, cnt, cdst, ys, lp, rg, x1, mod5, final_g.reshape(1, D_MODEL))
    nk, nv = new_kv
    new_k = jnp.transpose(nk.reshape(BATCH, DEPTH, N_HEADS, 2, QK_DIM, SEQ), (0, 1, 5, 2, 3, 4))
    new_v = nv.reshape(BATCH, DEPTH, SEQ, N_HEADS, V_DIM)
    return (y_ctx.reshape(BATCH, SEQ, D_MODEL), y_lat.reshape(DEC_BATCH, DEC_SEQ, D_MODEL), new_k, new_v)
```

```python
import functools
import math

import numpy as np
import jax
import jax.numpy as jnp
from jax import lax
from jax.experimental import pallas as pl
from jax.experimental.pallas import tpu as pltpu

D_MODEL = 1024
BATCH = 16
SEQ = 256
DEPTH = 2
DEC_BATCH = 4
DEC_SEQ = 1024
PAST_LEN = 512
GRID_W = 64
ATT_WIDTH = 512
CONV_WIDTH = 512
N_HEADS = 4
V_DIM = 128
QK_DIM = 64
ROPE_BASE = 10000.0
D_FF = 2816
N_EXPERTS = 8
D_FF_EXPERT = 1408
N_MOD = 6
NORM_EPS = 1e-6
Q_SCALE = QK_DIM ** -0.5 * math.log2(math.e)
IN_COLS = 3 * ATT_WIDTH + 3 * CONV_WIDTH

N_CTX = BATCH * SEQ
N_LAT = DEC_BATCH * DEC_SEQ
N_TOK = N_CTX + N_LAT
TM = 1024
N_TILES = N_TOK // TM
CTX_TILES = N_CTX // TM
SEQ_PER_TILE = TM // SEQ
CTX_SEQ_PER_STEP = 1
COND_ROWS = 8
TN_IN = 1024
N_IN_TILES = IN_COLS // TN_IN
ROW_CHUNK = 512
TN_FF = 1408
N_FF_TILES = D_FF // TN_FF
TN_ADA = 1536
TG = 512
ST = 256
SUB_PER_TILE = TM // ST
N_SUB = N_TOK // ST
CTX_SUB = N_CTX // ST
SEG_ALIGN = 16
STAGE_ROWS = 640
STAGE_CHUNKS = STAGE_ROWS // SEG_ALIGN
WAIT_PIECES = (32, 16, 8, 4, 2, 1)
COMBINE_SLOTS = 4
COMBINE_AHEAD = 2
NT_G = -(-(2 * N_TOK + N_SUB * N_EXPERTS * (SEG_ALIGN - 1) + N_EXPERTS * (TG - SEG_ALIGN)) // TG)
R_PAD = NT_G * TG
LANES = 128
VMEM_LIMIT = 60 * 1024 * 1024

BF = jnp.bfloat16
F32 = jnp.float32


def _params(sem, vmem=VMEM_LIMIT):
    return pltpu.CompilerParams(dimension_semantics=sem, vmem_limit_bytes=vmem)


def _mod_row(i):
    return jnp.where(i < CTX_TILES, 0, i - (CTX_TILES - 1))


WHOLE = pl.BlockSpec(memory_space=pltpu.VMEM)


def _mod(mod_ref, layer, i, c):
    return mod_ref[layer, pl.ds(_mod_row(i), 1), c * D_MODEL:(c + 1) * D_MODEL]


def _stream_specs(pair, width=D_MODEL):
    a = pl.BlockSpec((TM, width), lambda i, *_: (jnp.minimum(i, CTX_TILES - 1), 0))
    if pair:
        b = pl.BlockSpec((TM, width), lambda i, *_: (jnp.maximum(i - CTX_TILES, 0), 0))
    else:
        b = pl.BlockSpec((TM, width), lambda i, *_: (jnp.maximum(i, CTX_TILES), 0))
    return a, b


def _stream_rows(xa_ref, xb_ref, i):
    return jnp.where(i >= CTX_TILES, xb_ref[...], xa_ref[...])


def _silu(x):
    return x / (1.0 + jnp.exp(-x))


def _rms(x):
    return x * lax.rsqrt(jnp.mean(x * x, axis=-1, keepdims=True) + NORM_EPS)


def _ada_kernel(c_ref, w_ref, b_ref, o_ref):
    s = _silu(c_ref[...]).astype(BF)
    bias = b_ref[pl.ds(pl.program_id(0), 1), :]
    o_ref[...] = jnp.dot(s, w_ref[...].astype(BF), preferred_element_type=F32) + bias


def _ada(cond, w_ada, b_ada):
    n = N_MOD * D_MODEL
    return pl.pallas_call(
        _ada_kernel,
        grid=(DEPTH, n // TN_ADA),
        in_specs=[
            pl.BlockSpec((COND_ROWS, D_MODEL), lambda l, j: (0, 0)),
            pl.BlockSpec((None, D_MODEL, TN_ADA), lambda l, j: (l, 0, j)),
            pl.BlockSpec((DEPTH, TN_ADA), lambda l, j: (0, j)),
        ],
        out_specs=pl.BlockSpec((None, COND_ROWS, TN_ADA), lambda l, j: (l, 0, j)),
        out_shape=jax.ShapeDtypeStruct((DEPTH, COND_ROWS, n), F32),
        compiler_params=_params(("arbitrary", "arbitrary")),
        name="ada_mod",
    )(cond, w_ada, b_ada)


def _in_kernel(*refs, layer, aliased):
    xa_ref, xb_ref, mod_ref, g_ref, w_ref, cos_ref, sin_ref, cw_ref = refs[:8]
    refs = refs[10:] if aliased else refs[8:]
    q_ref, v_ref, kt_ref, conv_ref, nk_ref, nv_ref, h_s, wb_s, gb_s, gc_s = refs
    i = pl.program_id(0)
    j = pl.program_id(1)
    lat = i >= CTX_TILES
    ctx = jnp.logical_not(lat)

    @pl.when(i == 0)
    def _():
        wb_s[j] = w_ref[...].astype(BF)

    chunks = [slice(c * ROW_CHUNK, (c + 1) * ROW_CHUNK) for c in range(TM // ROW_CHUNK)]
    seqs_per_chunk = ROW_CHUNK // SEQ

    def norm(rows):
        gain = g_ref[layer:layer + 1, :] * (1.0 + _mod(mod_ref, layer, i, 1))
        x = jnp.where(lat, xb_ref[rows, :], xa_ref[rows, :])
        h_s[rows, :] = (_rms(x) * gain + _mod(mod_ref, layer, i, 0)).astype(BF)

    def proj(rows):
        acc = jnp.dot(h_s[rows, :], wb_s[j], preferred_element_type=F32)
        return acc[:, :ATT_WIDTH], acc[:, ATT_WIDTH:]

    def roped(a, rows):
        cos = jnp.concatenate([cos_ref[rows, :]] * N_HEADS, axis=1)
        sin = jnp.concatenate([sin_ref[rows, :]] * N_HEADS, axis=1)
        lane = lax.broadcasted_iota(jnp.int32, a.shape, 1)
        upper = (lane & (QK_DIM // 4)) != 0
        partner = jnp.where(upper, pltpu.roll(a, QK_DIM // 4, 1), pltpu.roll(a, ATT_WIDTH - QK_DIM // 4, 1))
        return a * cos + partner * sin

    @pl.when(jnp.logical_and(j == 0, lat))
    def _():
        for rows in chunks:
            norm(rows)
            q, k = proj(rows)
            q_ref[rows, :] = (roped(q, rows) * Q_SCALE).astype(BF)
            kt_ref[:, rows] = roped(k, rows).T.astype(BF)

    @pl.when(jnp.logical_and(j == 0, ctx))
    def _():
        for c, rows in enumerate(chunks):
            norm(rows)
            q, k = proj(rows)
            q_ref[rows, :] = (q * Q_SCALE).astype(BF)
            kt = k.T
            kt_ref[:, rows] = kt.astype(BF)
            for s in range(seqs_per_chunk):
                nk_ref[c * seqs_per_chunk + s] = kt[:, s * SEQ:(s + 1) * SEQ]

    @pl.when(jnp.logical_and(j == 1, lat))
    def _():
        for rows in chunks:
            v, gb = proj(rows)
            v_ref[rows, :] = v.astype(BF)
            gb_s[rows, :] = gb

    @pl.when(jnp.logical_and(j == 1, ctx))
    def _():
        for c, rows in enumerate(chunks):
            v, gb = proj(rows)
            v_ref[rows, :] = v.astype(BF)
            gb_s[rows, :] = gb
            for s in range(seqs_per_chunk):
                for h in range(N_HEADS):
                    nv_ref[c * seqs_per_chunk + s, pl.ds(h, SEQ, stride=N_HEADS), :] = (
                        v[s * SEQ:(s + 1) * SEQ, h * V_DIM:(h + 1) * V_DIM])

    @pl.when(j == 2)
    def _():
        for rows in chunks:
            gc, xin = proj(rows)
            gc_s[rows, :] = gc * xin
        u = gc_s[...]
        seq = jnp.where(lat, DEC_SEQ, SEQ)
        pos = lax.broadcasted_iota(jnp.int32, (TM, 1), 0) & (seq - 1)
        prev = jnp.where(pos == 0, 0.0, pltpu.roll(u, 1, 0))
        nxt = jnp.where(pos == seq - 1, 0.0, pltpu.roll(u, TM - 1, 0))
        cw = cw_ref[layer]
        conv = prev * cw[0:1] + u * cw[1:2] + nxt * cw[2:3]
        conv_ref[...] = (gb_s[...] * conv).astype(BF)


def _in_proj(layer, xs, mod, g_mix, w_in, cos_t, sin_t, conv_w, new_kv):
    pair = isinstance(xs, tuple)
    xa, xb = xs if pair else (xs, xs)
    spec_a, spec_b = _stream_specs(pair)
    ctx_i = lambda i: jnp.minimum(i, CTX_TILES - 1)
    in_specs = [
        spec_a, spec_b,
        WHOLE, WHOLE,
        pl.BlockSpec((None, D_MODEL, TN_IN), lambda i, j: (layer, 0, jnp.where(i == 0, j, N_IN_TILES - 1))),
        WHOLE, WHOLE, WHOLE,
    ]
    args = [xa, xb, mod, g_mix, w_in, cos_t, sin_t, conv_w]
    aliases = {}
    if new_kv is not None:
        in_specs += [pl.BlockSpec(memory_space=pl.ANY), pl.BlockSpec(memory_space=pl.ANY)]
        args += list(new_kv)
        aliases = {8: 4, 9: 5}
    row_tile = pl.BlockSpec((TM, ATT_WIDTH), lambda i, j: (i, 0))
    return pl.pallas_call(
        functools.partial(_in_kernel, layer=layer, aliased=new_kv is not None),
        grid=(N_TILES, N_IN_TILES),
        in_specs=in_specs,
        out_specs=[
            row_tile,
            row_tile,
            pl.BlockSpec((None, ATT_WIDTH, TM), lambda i, j: (i, 0, 0)),
            row_tile,
            pl.BlockSpec((SEQ_PER_TILE, None, ATT_WIDTH, SEQ), lambda i, j: (ctx_i(i), layer, 0, 0)),
            pl.BlockSpec((SEQ_PER_TILE, None, SEQ * N_HEADS, V_DIM), lambda i, j: (ctx_i(i), layer, 0, 0)),
        ],
        out_shape=[
            jax.ShapeDtypeStruct((N_TOK, ATT_WIDTH), BF),
            jax.ShapeDtypeStruct((N_TOK, ATT_WIDTH), BF),
            jax.ShapeDtypeStruct((N_TILES, ATT_WIDTH, TM), BF),
            jax.ShapeDtypeStruct((N_TOK, CONV_WIDTH), BF),
            jax.ShapeDtypeStruct((BATCH, DEPTH, ATT_WIDTH, SEQ), F32),
            jax.ShapeDtypeStruct((BATCH, DEPTH, SEQ * N_HEADS, V_DIM), F32),
        ],
        scratch_shapes=[
            pltpu.VMEM((TM, D_MODEL), BF),
            pltpu.VMEM((N_IN_TILES, D_MODEL, TN_IN), BF),
            pltpu.VMEM((TM, CONV_WIDTH), F32),
            pltpu.VMEM((TM, CONV_WIDTH), F32),
        ],
        input_output_aliases=aliases,
        compiler_params=_params(("arbitrary", "arbitrary")),
        name=f"in_proj_l{layer}",
    )(*args)


def _lambda(lq_ref, layer, lam_init):
    lq = lq_ref[layer]
    a = jnp.exp(jnp.sum(lq[0:1] * lq[1:2], axis=-1, keepdims=True))
    b = jnp.exp(jnp.sum(lq[2:3] * lq[3:4], axis=-1, keepdims=True))
    return a - b + lam_init


def _head_norm(o, sg, lam_init):
    return _rms(o) * sg * (1.0 - lam_init)


def _attn_ctx_kernel(q_ref, kt_ref, v_ref, lq_ref, sg_ref, o_ref, sc_s, *, layer, lam_init):
    lam = _lambda(lq_ref, layer, lam_init)
    sg = sg_ref[layer:layer + 1, :]

    units = [(b, h) for b in range(CTX_SEQ_PER_STEP) for h in range(N_HEADS)]

    def scores(u):
        b, h = units[u]
        pos = slice(b * SEQ, (b + 1) * SEQ)
        for s in range(2):
            d = slice(h * V_DIM + s * QK_DIM, h * V_DIM + (s + 1) * QK_DIM)
            sc_s[u % 2, s] = jnp.dot(q_ref[pos, d], kt_ref[d, pos], preferred_element_type=F32)

    def finish(u):
        b, h = units[u]
        pos = slice(b * SEQ, (b + 1) * SEQ)
        cols = slice(h * V_DIM, (h + 1) * V_DIM)
        v = v_ref[pos, cols]
        outs = []
        for s in range(2):
            sc = sc_s[u % 2, s]
            e = jnp.exp2(sc - jnp.max(sc, axis=-1, keepdims=True))
            r = 1.0 / jnp.sum(e, axis=-1, keepdims=True)
            outs.append(jnp.dot(e.astype(BF), v, preferred_element_type=F32) * r)
        o = outs[0] - lam * outs[1]
        o_ref[pos, cols] = _head_norm(o, sg, lam_init).astype(BF)

    scores(0)
    for u in range(len(units)):
        if u + 1 < len(units):
            scores(u + 1)
        finish(u)


def _attn_ctx(layer, q, kt, v, lambda_qk, subln_g, lam_init):
    rows = CTX_SEQ_PER_STEP * SEQ
    per_tile = TM // rows
    return pl.pallas_call(
        functools.partial(_attn_ctx_kernel, layer=layer, lam_init=lam_init),
        grid=(N_CTX // rows,),
        in_specs=[
            pl.BlockSpec((rows, ATT_WIDTH), lambda b: (b, 0)),
            pl.BlockSpec((None, ATT_WIDTH, rows), lambda b: (b // per_tile, 0, b % per_tile)),
            pl.BlockSpec((rows, ATT_WIDTH), lambda b: (b, 0)),
            WHOLE, WHOLE,
        ],
        out_specs=pl.BlockSpec((rows, ATT_WIDTH), lambda b: (b, 0)),
        out_shape=jax.ShapeDtypeStruct((N_TOK, ATT_WIDTH), BF),
        scratch_shapes=[pltpu.VMEM((2, 2, SEQ, SEQ), F32)],
        compiler_params=_params(("arbitrary",)),
        name=f"attn_ctx_l{layer}",
    )(q, kt, v, lambda_qk, subln_g)


TQ = 256


def _attn_lat_kernel(q_ref, kt_ref, v_ref, ckt_ref, cv_ref, lq_ref, sg_ref, att_in_ref, o_ref, sc_s, *,
                     layer, lam_init):
    del att_in_ref
    lam = _lambda(lq_ref, layer, lam_init)
    sg = sg_ref[layer:layer + 1, :]

    def scores(h, s):
        d = slice(h * V_DIM + s * QK_DIM, h * V_DIM + (s + 1) * QK_DIM)
        q = q_ref[:, d]
        sc_s[h % 2, s, :, :PAST_LEN] = jnp.dot(q, ckt_ref[d, :].astype(BF), preferred_element_type=F32)
        sc_s[h % 2, s, :, PAST_LEN:] = jnp.dot(q, kt_ref[d, :], preferred_element_type=F32)

    def softmax(h, s):
        sc = sc_s[h % 2, s]
        e = jnp.exp2(sc - jnp.max(sc, axis=-1, keepdims=True))
        return e, 1.0 / jnp.sum(e, axis=-1, keepdims=True)

    def finish(h, p1, p2):
        cols = slice(h * V_DIM, (h + 1) * V_DIM)
        e = jnp.concatenate([p1[0].astype(BF), p2[0].astype(BF)], axis=0)
        vc = cv_ref[pl.ds(h, PAST_LEN, stride=N_HEADS), :].astype(BF)
        pv = jnp.dot(e[:, :PAST_LEN], vc, preferred_element_type=F32)
        pv = pv + jnp.dot(e[:, PAST_LEN:], v_ref[:, cols], preferred_element_type=F32)
        o = pv[:TQ] * p1[1] - pv[TQ:] * (lam * p2[1])
        o_ref[:, cols] = _head_norm(o, sg, lam_init).astype(BF)

    scores(0, 0)
    scores(0, 1)
    for h in range(N_HEADS):
        more = h + 1 < N_HEADS
        if more:
            scores(h + 1, 0)
        p1 = softmax(h, 0)
        if more:
            scores(h + 1, 1)
        finish(h, p1, softmax(h, 1))


def _attn_lat(layer, q, kt, v, cache_kt, cache_v, lambda_qk, subln_g, att, lam_init):
    nqb = DEC_SEQ // TQ
    q0 = N_CTX // TQ
    return pl.pallas_call(
        functools.partial(_attn_lat_kernel, layer=layer, lam_init=lam_init),
        grid=(DEC_BATCH, nqb),
        in_specs=[
            pl.BlockSpec((TQ, ATT_WIDTH), lambda b, t: (q0 + b * nqb + t, 0)),
            pl.BlockSpec((None, ATT_WIDTH, DEC_SEQ), lambda b, t: (CTX_TILES + b, 0, 0)),
            pl.BlockSpec((DEC_SEQ, ATT_WIDTH), lambda b, t: (CTX_TILES + b, 0)),
            pl.BlockSpec((None, None, ATT_WIDTH, PAST_LEN), lambda b, t: (b, layer, 0, 0)),
            pl.BlockSpec((None, None, PAST_LEN * N_HEADS, V_DIM), lambda b, t: (b, layer, 0, 0)),
            WHOLE, WHOLE,
            pl.BlockSpec(memory_space=pl.ANY),
        ],
        out_specs=pl.BlockSpec((TQ, ATT_WIDTH), lambda b, t: (q0 + b * nqb + t, 0)),
        out_shape=jax.ShapeDtypeStruct((N_TOK, ATT_WIDTH), BF),
        scratch_shapes=[pltpu.VMEM((2, 2, TQ, PAST_LEN + DEC_SEQ), F32)],
        input_output_aliases={7: 0},
        compiler_params=_params(("arbitrary", "arbitrary")),
        name=f"attn_lat_l{layer}",
    )(q, kt, v, cache_kt, cache_v, lambda_qk, subln_g, att)


def _out_kernel(att_ref, conv_ref, w_ref, xa_ref, xb_ref, mod_ref, gf_ref, *rest, layer, route):
    if route:
        wr_ref, xo_ref, h2_ref, lp_ref, rg_ref, n_ref, wb_s = rest
    else:
        xo_ref, h2_ref, wb_s = rest
    i = pl.program_id(0)

    @pl.when(i == 0)
    def _():
        wb_s[...] = w_ref[...].astype(BF)

    mo = jnp.dot(att_ref[...], wb_s[:ATT_WIDTH, :], preferred_element_type=F32)
    mo = mo + jnp.dot(conv_ref[...], wb_s[ATT_WIDTH:, :], preferred_element_type=F32)
    xn = _stream_rows(xa_ref, xb_ref, i) + _mod(mod_ref, layer, i, 2) * mo
    xo_ref[...] = xn
    h2 = (_rms(xn) * gf_ref[layer:layer + 1, :]) * (1.0 + _mod(mod_ref, layer, i, 4)) + _mod(mod_ref, layer, i, 3)
    h2b = h2.astype(BF)
    h2_ref[...] = h2b
    if route:
        _route(h2b, wr_ref, lp_ref, rg_ref, n_ref)


def _out_proj(layer, att, conv, w_out, xs, mod, g_ffn, w_router_pad=None):
    pair = isinstance(xs, tuple)
    xa, xb = xs if pair else (xs, xs)
    spec_a, spec_b = _stream_specs(pair)
    row_spec = pl.BlockSpec((TM, D_MODEL), lambda i: (i, 0))
    lane_spec = pl.BlockSpec((TM, LANES), lambda i: (i, 0))
    in_specs = [
        pl.BlockSpec((TM, ATT_WIDTH), lambda i: (i, 0)),
        pl.BlockSpec((TM, CONV_WIDTH), lambda i: (i, 0)),
        pl.BlockSpec((None, D_MODEL, D_MODEL), lambda i: (layer, 0, 0)),
        spec_a, spec_b,
        WHOLE, WHOLE,
    ]
    args = [att, conv, w_out, xa, xb, mod, g_ffn]
    out_specs = [row_spec, row_spec]
    out_shape = [jax.ShapeDtypeStruct((N_TOK, D_MODEL), F32), jax.ShapeDtypeStruct((N_TOK, D_MODEL), BF)]
    route = w_router_pad is not None
    if route:
        in_specs.append(WHOLE)
        args.append(w_router_pad)
        out_specs += [lane_spec, lane_spec, pl.BlockSpec((None, 8, LANES), lambda i: (i, 0, 0))]
        out_shape += [jax.ShapeDtypeStruct((N_TOK, LANES), jnp.int32), jax.ShapeDtypeStruct((N_TOK, LANES), F32),
                      jax.ShapeDtypeStruct((N_TILES, 8, LANES), jnp.int32)]
    return pl.pallas_call(
        functools.partial(_out_kernel, layer=layer, route=route),
        grid=(N_TILES,),
        in_specs=in_specs,
        out_specs=out_specs,
        out_shape=out_shape,
        scratch_shapes=[pltpu.VMEM((D_MODEL, D_MODEL), BF)],
        compiler_params=_params(("arbitrary",)),
        name=f"out_proj_l{layer}",
    )(*args)


def _gu_kernel(h_ref, wg_ref, wu_ref, o_ref, wb_s):
    i = pl.program_id(0)
    j = pl.program_id(1)

    @pl.when(i == 0)
    def _():
        wb_s[j, :, :TN_FF] = wg_ref[...].astype(BF)
        wb_s[j, :, TN_FF:] = wu_ref[...].astype(BF)

    gu = jnp.dot(h_ref[...], wb_s[j], preferred_element_type=F32)
    o_ref[...] = (_silu(gu[:, :TN_FF]) * gu[:, TN_FF:]).astype(BF)


def _dense_gu(h2, w_gu):
    nj = N_FF_TILES
    return pl.pallas_call(
        _gu_kernel,
        grid=(N_TILES, nj),
        in_specs=[
            pl.BlockSpec((TM, D_MODEL), lambda i, j: (i, 0)),
            pl.BlockSpec((None, D_MODEL, TN_FF), lambda i, j: (0, 0, jnp.where(i == 0, j, nj - 1)),
                         pipeline_mode=pl.Buffered(1)),
            pl.BlockSpec((None, D_MODEL, TN_FF), lambda i, j: (0, 0, jnp.where(i == 0, j, nj - 1) + nj),
                         pipeline_mode=pl.Buffered(1)),
        ],
        out_specs=pl.BlockSpec((TM, TN_FF), lambda i, j: (i, j)),
        out_shape=jax.ShapeDtypeStruct((N_TOK, D_FF), BF),
        scratch_shapes=[pltpu.VMEM((nj, D_MODEL, 2 * TN_FF), BF)],
        compiler_params=_params(("arbitrary", "arbitrary")),
        name="dense_gu",
    )(h2, w_gu, w_gu)


def _down_kernel(a_ref, w_ref, x_ref, mod_ref, o_ref, wb_s, *, layer):
    i = pl.program_id(0)

    @pl.when(i == 0)
    def _():
        wb_s[...] = w_ref[...].astype(BF)

    y = jnp.dot(a_ref[...], wb_s[...], preferred_element_type=F32)
    o_ref[...] = x_ref[...] + _mod(mod_ref, layer, i, 5) * y


def _dense_down(layer, act, w_down, x, mod):
    row_spec = pl.BlockSpec((TM, D_MODEL), lambda i: (i, 0))
    return pl.pallas_call(
        functools.partial(_down_kernel, layer=layer),
        grid=(N_TILES,),
        in_specs=[
            pl.BlockSpec((TM, D_FF), lambda i: (i, 0)),
            pl.BlockSpec((None, D_FF, D_MODEL), lambda i: (0, 0, 0), pipeline_mode=pl.Buffered(1)),
            row_spec,
            WHOLE,
        ],
        out_specs=row_spec,
        out_shape=jax.ShapeDtypeStruct((N_TOK, D_MODEL), F32),
        scratch_shapes=[pltpu.VMEM((D_FF, D_MODEL), BF)],
        compiler_params=_params(("arbitrary",)),
        name="dense_down",
    )(act, w_down, x, mod)


def _route(h, wr_ref, lp_ref, rg_ref, n_ref):
    logits = jnp.dot(h, wr_ref[...].astype(BF), preferred_element_type=F32)
    lane = lax.broadcasted_iota(jnp.int32, logits.shape, 1)
    lg = jnp.where(lane < N_EXPERTS, logits, -jnp.inf)
    m1 = jnp.max(lg, axis=-1, keepdims=True)
    i1 = jnp.min(jnp.where(lg == m1, lane, LANES), axis=-1, keepdims=True)
    lg2 = jnp.where(lane == i1, -jnp.inf, lg)
    m2 = jnp.max(lg2, axis=-1, keepdims=True)
    i2 = jnp.min(jnp.where(lg2 == m2, lane, LANES), axis=-1, keepdims=True)
    e2 = jnp.exp(m2 - m1)
    w1 = 1.0 / (1.0 + e2)
    w2 = e2 / (1.0 + e2)

    sel1 = lane == i1
    sel2 = lane == i2
    onehot = jnp.logical_or(sel1, sel2)
    rows = lax.broadcasted_iota(jnp.int32, (TM, TM), 0)
    colsi = lax.broadcasted_iota(jnp.int32, (TM, TM), 1)
    earlier = jnp.logical_and(colsi < rows, (colsi // ST) == (rows // ST))
    before = jnp.dot(earlier.astype(BF), onehot.astype(BF), preferred_element_type=F32)
    onehot_f = onehot.astype(F32)
    counts = [jnp.sum(onehot_f[s * ST:(s + 1) * ST], axis=0, keepdims=True) for s in range(SUB_PER_TILE)]
    counts = jnp.concatenate(counts + [jnp.zeros((8 - SUB_PER_TILE, LANES), F32)], axis=0).astype(jnp.int32)
    seg_len = ((counts + (SEG_ALIGN - 1)) // SEG_ALIGN) * SEG_ALIGN
    n_ref[...] = seg_len
    la = lax.broadcasted_iota(jnp.int32, (LANES, LANES), 0)
    lb = lax.broadcasted_iota(jnp.int32, (LANES, LANES), 1)
    seg_start = jnp.dot(seg_len.astype(F32).astype(BF), (la < lb).astype(BF), preferred_element_type=F32)
    start = jnp.concatenate(
        [jnp.broadcast_to(seg_start[s:s + 1], (ST, LANES)) for s in range(SUB_PER_TILE)], axis=0)
    where = before + start
    lp1 = jnp.sum(jnp.where(sel1, where, 0.0), axis=-1, keepdims=True).astype(jnp.int32)
    lp2 = jnp.sum(jnp.where(sel2, where, 0.0), axis=-1, keepdims=True).astype(jnp.int32)
    lp_ref[...] = jnp.where(lane == 0, lp1, jnp.where(lane == 1, lp2, 0))
    rg_ref[...] = jnp.where(lane == 0, w1, jnp.where(lane == 1, w2, 0.0))


def _chunk_copies(s, cnt_ref, cdst_ref, stage, rows_hbm, sem, *, to_hbm, wait):
    def copy(v, h):
        return pltpu.make_async_copy(v, h, sem) if to_hbm else pltpu.make_async_copy(h, v, sem)

    if wait:
        for z in WAIT_PIECES:
            @pl.when((cnt_ref[s] & z) != 0)
            def _():
                copy(stage.at[pl.ds(0, z * SEG_ALIGN)], rows_hbm.at[pl.ds(0, z * SEG_ALIGN)]).wait()
        return

    def one(c, carry):
        v = stage.at[pl.ds(pl.multiple_of(c * SEG_ALIGN, SEG_ALIGN), SEG_ALIGN)]
        h = rows_hbm.at[pl.ds(pl.multiple_of(cdst_ref[s * STAGE_CHUNKS + c], SEG_ALIGN), SEG_ALIGN)]
        copy(v, h).start()
        return carry

    lax.fori_loop(0, cnt_ref[s], one, 0)


def _dispatch_kernel(cnt_ref, cdst_ref, h_ref, lp_ref, xs_ref, stage_s, sem):
    copies = functools.partial(_chunk_copies, cnt_ref=cnt_ref, cdst_ref=cdst_ref, rows_hbm=xs_ref, to_hbm=True)
    for k in range(SUB_PER_TILE):
        s = pl.program_id(0) * SUB_PER_TILE + k
        slot = k % 2
        rows = slice(k * ST, (k + 1) * ST)

        @pl.when(s >= 2)
        def _():
            copies(s - 2, stage=stage_s.at[slot], sem=sem.at[slot], wait=True)

        lpt = lp_ref[rows, :].T
        r = lax.broadcasted_iota(jnp.int32, (STAGE_ROWS, ST), 0)
        pick = jnp.logical_or(r == lpt[0:1, :], r == lpt[1:2, :]).astype(BF)
        stage_s[slot] = jnp.dot(pick, h_ref[rows, :], preferred_element_type=F32).astype(BF)
        copies(s, stage=stage_s.at[slot], sem=sem.at[slot], wait=False)

    @pl.when(pl.program_id(0) == N_TILES - 1)
    def _():
        copies(N_SUB - 2, stage=stage_s.at[0], sem=sem.at[0], wait=True)
        copies(N_SUB - 1, stage=stage_s.at[1], sem=sem.at[1], wait=True)


def _dispatch(cnt, cdst, h2, lp):
    assert SUB_PER_TILE % 2 == 0
    grid_spec = pltpu.PrefetchScalarGridSpec(
        num_scalar_prefetch=2,
        grid=(N_TILES,),
        in_specs=[
            pl.BlockSpec((TM, D_MODEL), lambda i, *_: (i, 0)),
            pl.BlockSpec((TM, LANES), lambda i, *_: (i, 0)),
        ],
        out_specs=pl.BlockSpec(memory_space=pl.ANY),
        scratch_shapes=[pltpu.VMEM((2, STAGE_ROWS, D_MODEL), BF), pltpu.SemaphoreType.DMA((2,))],
    )
    return pl.pallas_call(
        _dispatch_kernel,
        grid_spec=grid_spec,
        out_shape=jax.ShapeDtypeStruct((R_PAD, D_MODEL), BF),
        compiler_params=_params(("arbitrary",)),
        name="moe_dispatch",
    )(cnt, cdst, h2, lp)


def _expert_weights(te_ref, nt_ref, nxt_ref, w_hbm, wf_s, wb_s, sem):
    r = pl.program_id(0)

    def fetch(e):
        return pltpu.make_async_copy(w_hbm.at[0, e], wf_s, sem)

    @pl.when(r == 0)
    def _():
        fetch(te_ref[0]).start()

    first = jnp.logical_or(r == 0, te_ref[r] != te_ref[jnp.maximum(r - 1, 0)])

    @pl.when(jnp.logical_and(r < nt_ref[0], first))
    def _():
        fetch(te_ref[r]).wait()
        wb_s[...] = wf_s[...].astype(BF)

        @pl.when(nxt_ref[r] >= 0)
        def _():
            fetch(nxt_ref[r]).start()


def _moe_ffn_kernel(te_ref, nt_ref, nxt_ref, used_ref, x_ref, wgu_hbm, wd_hbm, o_ref,
                    wgu_f, wgu_b, wd_f, wd_b, sem):
    r = pl.program_id(0)
    _expert_weights(te_ref, nt_ref, nxt_ref, wgu_hbm, wgu_f, wgu_b, sem.at[0])
    _expert_weights(te_ref, nt_ref, nxt_ref, wd_hbm, wd_f, wd_b, sem.at[1])

    def ffn(rows):
        gu = jnp.dot(x_ref[rows, :], wgu_b[...], preferred_element_type=F32)
        act = (_silu(gu[:, :D_FF_EXPERT]) * gu[:, D_FF_EXPERT:]).astype(BF)
        o_ref[rows, :] = jnp.dot(act, wd_b[...], preferred_element_type=F32).astype(BF)

    @pl.when(jnp.logical_and(r < nt_ref[0], used_ref[r] > TG // 2))
    def _():
        ffn(slice(0, TG))

    @pl.when(jnp.logical_and(r < nt_ref[0], used_ref[r] <= TG // 2))
    def _():
        ffn(slice(0, TG // 2))


def _moe_ffn(te, nt, nxt, used, rows, w_gu, w_down):
    tile_map = lambda r, te, nt, nxt, used: (jnp.minimum(r, nt[0] - 1), 0)
    grid_spec = pltpu.PrefetchScalarGridSpec(
        num_scalar_prefetch=4,
        grid=(NT_G,),
        in_specs=[pl.BlockSpec((TG, D_MODEL), tile_map),
                  pl.BlockSpec(memory_space=pl.ANY), pl.BlockSpec(memory_space=pl.ANY)],
        out_specs=pl.BlockSpec((TG, D_MODEL), tile_map),
        scratch_shapes=[
            pltpu.VMEM((D_MODEL, 2 * D_FF_EXPERT), F32), pltpu.VMEM((D_MODEL, 2 * D_FF_EXPERT), BF),
            pltpu.VMEM((D_FF_EXPERT, D_MODEL), F32), pltpu.VMEM((D_FF_EXPERT, D_MODEL), BF),
            pltpu.SemaphoreType.DMA((2,)),
        ],
    )
    return pl.pallas_call(
        _moe_ffn_kernel,
        grid_spec=grid_spec,
        out_shape=jax.ShapeDtypeStruct((R_PAD, D_MODEL), BF),
        compiler_params=_params(("arbitrary",)),
        name="moe_ffn",
    )(te, nt, nxt, used, rows, w_gu, w_down)


def _combine_kernel(cnt_ref, cdst_ref, ys_ref, lp_ref, rg_ref, x_ref, mod_ref, fg_ref,
                    oa_ref, ob_ref, stage_s, sem, *, layer):
    i = pl.program_id(0)
    copies = functools.partial(_chunk_copies, cnt_ref=cnt_ref, cdst_ref=cdst_ref, rows_hbm=ys_ref, to_hbm=False)

    @pl.when(i == 0)
    def _():
        stage_s[...] = jnp.zeros_like(stage_s)
        for s in range(COMBINE_AHEAD):
            copies(s, stage=stage_s.at[s], sem=sem.at[s], wait=False)

    for k in range(SUB_PER_TILE):
        s = i * SUB_PER_TILE + k
        slot = k % COMBINE_SLOTS
        ahead = (k + COMBINE_AHEAD) % COMBINE_SLOTS
        rows = slice(k * ST, (k + 1) * ST)

        @pl.when(s + COMBINE_AHEAD < N_SUB)
        def _():
            copies(s + COMBINE_AHEAD, stage=stage_s.at[ahead], sem=sem.at[ahead], wait=False)

        copies(s, stage=stage_s.at[slot], sem=sem.at[slot], wait=True)

        lp = lp_ref[rows, :]
        r = lax.broadcasted_iota(jnp.int32, (ST, STAGE_ROWS), 1)
        staged = stage_s[slot]
        a = jnp.dot((r == lp[:, 0:1]).astype(BF), staged, preferred_element_type=F32)
        b = jnp.dot((r == lp[:, 1:2]).astype(BF), staged, preferred_element_type=F32)
        rg = rg_ref[rows, :]
        y = rg[:, 0:1] * a + rg[:, 1:2] * b
        xn = x_ref[rows, :] + _mod(mod_ref, layer, i, 5) * y
        out = _rms(xn) * fg_ref[...]

        @pl.when(i < CTX_TILES)
        def _():
            oa_ref[rows, :] = out

        @pl.when(i >= CTX_TILES)
        def _():
            ob_ref[rows, :] = out


def _combine(layer, cnt, cdst, ys, lp, rg, x, mod, final_g):
    assert SUB_PER_TILE % COMBINE_SLOTS == 0
    grid_spec = pltpu.PrefetchScalarGridSpec(
        num_scalar_prefetch=2,
        grid=(N_TILES,),
        in_specs=[
            pl.BlockSpec(memory_space=pl.ANY),
            pl.BlockSpec((TM, LANES), lambda i, *_: (i, 0)),
            pl.BlockSpec((TM, LANES), lambda i, *_: (i, 0)),
            pl.BlockSpec((TM, D_MODEL), lambda i, *_: (i, 0)),
            WHOLE, WHOLE,
        ],
        out_specs=[
            pl.BlockSpec((TM, D_MODEL), lambda i, *_: (jnp.minimum(i, CTX_TILES - 1), 0)),
            pl.BlockSpec((TM, D_MODEL), lambda i, *_: (jnp.maximum(i - CTX_TILES, 0), 0)),
        ],
        scratch_shapes=[pltpu.VMEM((COMBINE_SLOTS, STAGE_ROWS, D_MODEL), BF),
                        pltpu.SemaphoreType.DMA((COMBINE_SLOTS,))],
    )
    return pl.pallas_call(
        functools.partial(_combine_kernel, layer=layer),
        grid_spec=grid_spec,
        out_shape=[
            jax.ShapeDtypeStruct((N_CTX, D_MODEL), F32),
            jax.ShapeDtypeStruct((N_LAT, D_MODEL), F32),
        ],
        compiler_params=_params(("arbitrary",)),
        name="moe_combine",
    )(cnt, cdst, ys, lp, rg, x, mod, final_g)


def _group_layout(n_tiles):
    n = n_tiles[:, :SUB_PER_TILE, :N_EXPERTS].reshape(N_SUB, N_EXPERTS)
    tiles = (jnp.sum(n, axis=0) + TG - 1) // TG
    tile_end = jnp.cumsum(tiles)
    region = (tile_end - tiles) * TG
    dst = region[None, :] + jnp.cumsum(n, axis=0) - n
    seg_end = jnp.cumsum(n, axis=1)
    seg = seg_end - n
    row = jnp.arange(STAGE_CHUNKS, dtype=jnp.int32) * SEG_ALIGN
    owner = jnp.sum((row[None, :, None] >= seg_end[:, None, :]).astype(jnp.int32), axis=-1)
    own = jnp.minimum(owner, N_EXPERTS - 1)[..., None] == jnp.arange(N_EXPERTS)
    cdst = jnp.sum(jnp.where(own, (dst - seg)[:, None, :], 0), axis=-1) + row[None, :]
    cnt = seg_end[:, -1] // SEG_ALIGN
    nt = tile_end[-1]
    tile_id = jnp.minimum(jnp.arange(NT_G, dtype=jnp.int32), nt - 1)
    te = jnp.sum((tile_id[:, None] >= tile_end[None, :]).astype(jnp.int32), axis=-1)
    after = jnp.sum(jnp.where(te[:, None] == jnp.arange(N_EXPERTS), tile_end[None, :], 0), axis=-1)
    nxt = jnp.where(after < nt, jnp.sum((after[:, None] >= tile_end[None, :]).astype(jnp.int32), axis=-1), -1)
    mine = te[:, None] == jnp.arange(N_EXPERTS)
    region_end = jnp.sum(jnp.where(mine, (region + jnp.sum(n, axis=0))[None, :], 0), axis=-1)
    used = jnp.clip(region_end - tile_id * TG, 0, TG)
    i32 = lambda a: a.astype(jnp.int32)
    return (i32(cnt), i32(cdst.reshape(N_SUB * STAGE_CHUNKS)), i32(te), i32(nt.reshape(1)), i32(nxt), i32(used))


def _rope_tables():
    p = np.arange(DEC_SEQ)
    row = (p // GRID_W).astype(np.float32)
    col = (p % GRID_W).astype(np.float32)
    half = QK_DIM // 4
    freqs = (ROPE_BASE ** (-np.arange(half, dtype=np.float32) / half)).astype(np.float32)
    lane = np.arange(V_DIM)
    f = freqs[lane & (half - 1)]
    use_col = (lane & (2 * half)) != 0
    ang = (np.where(use_col[None, :], col[:, None], row[:, None]) * f[None, :]).astype(np.float32)
    upper = (lane & half) != 0
    sin = np.sin(ang)
    return jnp.asarray(np.cos(ang), F32), jnp.asarray(np.where(upper[None, :], sin, -sin), F32)


def kernel(x_prompt, x_sample, cache_k, cache_v, c, c_ctx, w_ada, b_ada, norm_mix_g, norm_ffn_g,
           w_in, lambda_qk, subln_g, conv_w, w_out, w_gu_dense, w_down_dense, w_router,
           w_gu_moe, w_down_moe, final_g):
    assert DEPTH == 2
    xs = (x_prompt.reshape(N_CTX, D_MODEL), x_sample.reshape(N_LAT, D_MODEL))
    cond = jnp.concatenate([c_ctx[None, :], c, jnp.zeros((COND_ROWS - 1 - DEC_BATCH, D_MODEL), F32)], axis=0)
    mod = _ada(cond, w_ada, b_ada)
    cos_t, sin_t = _rope_tables()
    cache_kt = jnp.transpose(cache_k, (0, 1, 3, 4, 5, 2)).reshape(DEC_BATCH, DEPTH, ATT_WIDTH, PAST_LEN)
    cache_v4 = cache_v.reshape(DEC_BATCH, DEPTH, PAST_LEN * N_HEADS, V_DIM)

    new_kv = None
    for layer in range(DEPTH):
        lam_init = 0.8 - 0.6 * math.exp(-0.3 * layer)
        q, v, kt, conv, nk, nv = _in_proj(layer, xs, mod, norm_mix_g, w_in, cos_t, sin_t, conv_w, new_kv)
        new_kv = (nk, nv)
        att = _attn_ctx(layer, q, kt, v, lambda_qk, subln_g, lam_init)
        att = _attn_lat(layer, q, kt, v, cache_kt, cache_v4, lambda_qk, subln_g, att, lam_init)
        if layer == 0:
            x1, h2 = _out_proj(layer, att, conv, w_out, xs, mod, norm_ffn_g)
            act = _dense_gu(h2, w_gu_dense)
            xs = _dense_down(layer, act, w_down_dense, x1, mod)
        else:
            wr = jnp.pad(w_router[0], ((0, 0), (0, LANES - N_EXPERTS)))
            x1, h2, lp, rg, n_tiles = _out_proj(layer, att, conv, w_out, xs, mod, norm_ffn_g, wr)
            cnt, cdst, te, nt, nxt, used = _group_layout(n_tiles)
            xsort = _dispatch(cnt, cdst, h2, lp)
            ys = _moe_ffn(te, nt, nxt, used, xsort, w_gu_moe, w_down_moe)
            y_ctx, y_lat = _combine(layer, cnt, cdst, ys, lp, rg, x1, mod, final_g.reshape(1, D_MODEL))
    nk, nv = new_kv
    new_k = jnp.transpose(nk.reshape(BATCH, DEPTH, N_HEADS, 2, QK_DIM, SEQ), (0, 1, 5, 2, 3, 4))
    new_v = nv.reshape(BATCH, DEPTH, SEQ, N_HEADS, V_DIM)
    return (y_ctx.reshape(BATCH, SEQ, D_MODEL), y_lat.reshape(DEC_BATCH, DEC_SEQ, D_MODEL), new_k, new_v)
```

```python
import functools
import math

import numpy as np
import jax
import jax.numpy as jnp
from jax import lax
from jax.experimental import pallas as pl
from jax.experimental.pallas import tpu as pltpu

D_MODEL = 1024
BATCH = 16
SEQ = 256
DEPTH = 2
DEC_BATCH = 4
DEC_SEQ = 1024
PAST_LEN = 512
GRID_W = 64
ATT_WIDTH = 512
CONV_WIDTH = 512
N_HEADS = 4
V_DIM = 128
QK_DIM = 64
ROPE_BASE = 10000.0
D_FF = 2816
N_EXPERTS = 8
D_FF_EXPERT = 1408
N_MOD = 6
NORM_EPS = 1e-6
Q_SCALE = QK_DIM ** -0.5 * math.log2(math.e)
IN_COLS = 3 * ATT_WIDTH + 3 * CONV_WIDTH

N_CTX = BATCH * SEQ
N_LAT = DEC_BATCH * DEC_SEQ
N_TOK = N_CTX + N_LAT
TM = 1024
N_TILES = N_TOK // TM
CTX_TILES = N_CTX // TM
SEQ_PER_TILE = TM // SEQ
CTX_SEQ_PER_STEP = 1
COND_ROWS = 8
TN_IN = 1024
N_IN_TILES = IN_COLS // TN_IN
ROW_CHUNK = 512
FF_TM = 512
FF_HALVES = 2
FF_HALF = D_FF // FF_HALVES
STAGE_W_ROWS = 1024
STAGE_W_COLS = 512
TN_ADA = 1536
TG = 512
ST = 256
SUB_PER_TILE = TM // ST
N_SUB = N_TOK // ST
CTX_SUB = N_CTX // ST
SEG_ALIGN = 16
STAGE_ROWS = 640
STAGE_CHUNKS = STAGE_ROWS // SEG_ALIGN
WAIT_PIECES = (32, 16, 8, 4, 2, 1)
COMBINE_SLOTS = 4
COMBINE_AHEAD = 2
NT_G = -(-(2 * N_TOK + N_SUB * N_EXPERTS * (SEG_ALIGN - 1) + N_EXPERTS * (TG - SEG_ALIGN)) // TG)
R_PAD = NT_G * TG
LANES = 128
VMEM_LIMIT = 60 * 1024 * 1024

BF = jnp.bfloat16
F32 = jnp.float32


def _params(sem, vmem=VMEM_LIMIT):
    return pltpu.CompilerParams(dimension_semantics=sem, vmem_limit_bytes=vmem)


def _mod_row(i):
    return jnp.where(i < CTX_TILES, 0, i - (CTX_TILES - 1))


WHOLE = pl.BlockSpec(memory_space=pltpu.VMEM)


def _mod(mod_ref, layer, i, c):
    return mod_ref[layer, pl.ds(_mod_row(i), 1), c * D_MODEL:(c + 1) * D_MODEL]


def _stream_specs(pair, width=D_MODEL):
    a = pl.BlockSpec((TM, width), lambda i, *_: (jnp.minimum(i, CTX_TILES - 1), 0))
    if pair:
        b = pl.BlockSpec((TM, width), lambda i, *_: (jnp.maximum(i - CTX_TILES, 0), 0))
    else:
        b = pl.BlockSpec((TM, width), lambda i, *_: (jnp.maximum(i, CTX_TILES), 0))
    return a, b


def _stream_rows(xa_ref, xb_ref, i):
    return jnp.where(i >= CTX_TILES, xb_ref[...], xa_ref[...])


def _silu(x):
    return x / (1.0 + jnp.exp(-x))


def _rms(x):
    return x * lax.rsqrt(jnp.mean(x * x, axis=-1, keepdims=True) + NORM_EPS)


def _ada_kernel(c_ref, w_ref, b_ref, o_ref):
    s = _silu(c_ref[...]).astype(BF)
    bias = b_ref[pl.ds(pl.program_id(0), 1), :]
    o_ref[...] = jnp.dot(s, w_ref[...].astype(BF), preferred_element_type=F32) + bias


def _ada(cond, w_ada, b_ada):
    n = N_MOD * D_MODEL
    return pl.pallas_call(
        _ada_kernel,
        grid=(DEPTH, n // TN_ADA),
        in_specs=[
            pl.BlockSpec((COND_ROWS, D_MODEL), lambda l, j: (0, 0)),
            pl.BlockSpec((None, D_MODEL, TN_ADA), lambda l, j: (l, 0, j)),
            pl.BlockSpec((DEPTH, TN_ADA), lambda l, j: (0, j)),
        ],
        out_specs=pl.BlockSpec((None, COND_ROWS, TN_ADA), lambda l, j: (l, 0, j)),
        out_shape=jax.ShapeDtypeStruct((DEPTH, COND_ROWS, n), F32),
        compiler_params=_params(("arbitrary", "arbitrary")),
        name="ada_mod",
    )(cond, w_ada, b_ada)


def _in_kernel(*refs, layer, aliased):
    xa_ref, xb_ref, mod_ref, g_ref, w_ref, cos_ref, sin_ref, cw_ref = refs[:8]
    refs = refs[10:] if aliased else refs[8:]
    q_ref, v_ref, kt_ref, conv_ref, nk_ref, nv_ref, h_s, wb_s, gb_s, gc_s = refs
    i = pl.program_id(0)
    j = pl.program_id(1)
    lat = i >= CTX_TILES
    ctx = jnp.logical_not(lat)

    @pl.when(i == 0)
    def _():
        wb_s[j] = w_ref[...].astype(BF)

    chunks = [slice(c * ROW_CHUNK, (c + 1) * ROW_CHUNK) for c in range(TM // ROW_CHUNK)]
    seqs_per_chunk = ROW_CHUNK // SEQ

    def norm(rows):
        gain = g_ref[layer:layer + 1, :] * (1.0 + _mod(mod_ref, layer, i, 1))
        x = jnp.where(lat, xb_ref[rows, :], xa_ref[rows, :])
        h_s[rows, :] = (_rms(x) * gain + _mod(mod_ref, layer, i, 0)).astype(BF)

    def proj(rows):
        acc = jnp.dot(h_s[rows, :], wb_s[j], preferred_element_type=F32)
        return acc[:, :ATT_WIDTH], acc[:, ATT_WIDTH:]

    def roped(a, rows):
        cos = jnp.concatenate([cos_ref[rows, :]] * N_HEADS, axis=1)
        sin = jnp.concatenate([sin_ref[rows, :]] * N_HEADS, axis=1)
        lane = lax.broadcasted_iota(jnp.int32, a.shape, 1)
        upper = (lane & (QK_DIM // 4)) != 0
        partner = jnp.where(upper, pltpu.roll(a, QK_DIM // 4, 1), pltpu.roll(a, ATT_WIDTH - QK_DIM // 4, 1))
        return a * cos + partner * sin

    @pl.when(jnp.logical_and(j == 0, lat))
    def _():
        for rows in chunks:
            norm(rows)
            q, k = proj(rows)
            q_ref[rows, :] = (roped(q, rows) * Q_SCALE).astype(BF)
            kt_ref[:, rows] = roped(k, rows).T.astype(BF)

    @pl.when(jnp.logical_and(j == 0, ctx))
    def _():
        for c, rows in enumerate(chunks):
            norm(rows)
            q, k = proj(rows)
            q_ref[rows, :] = (q * Q_SCALE).astype(BF)
            kt = k.T
            kt_ref[:, rows] = kt.astype(BF)
            for s in range(seqs_per_chunk):
                nk_ref[c * seqs_per_chunk + s] = kt[:, s * SEQ:(s + 1) * SEQ]

    @pl.when(jnp.logical_and(j == 1, lat))
    def _():
        for rows in chunks:
            v, gb = proj(rows)
            v_ref[rows, :] = v.astype(BF)
            gb_s[rows, :] = gb

    @pl.when(jnp.logical_and(j == 1, ctx))
    def _():
        for c, rows in enumerate(chunks):
            v, gb = proj(rows)
            v_ref[rows, :] = v.astype(BF)
            gb_s[rows, :] = gb
            for s in range(seqs_per_chunk):
                for h in range(N_HEADS):
                    nv_ref[c * seqs_per_chunk + s, pl.ds(h, SEQ, stride=N_HEADS), :] = (
                        v[s * SEQ:(s + 1) * SEQ, h * V_DIM:(h + 1) * V_DIM])

    @pl.when(j == 2)
    def _():
        for rows in chunks:
            gc, xin = proj(rows)
            gc_s[rows, :] = gc * xin
        u = gc_s[...]
        seq = jnp.where(lat, DEC_SEQ, SEQ)
        pos = lax.broadcasted_iota(jnp.int32, (TM, 1), 0) & (seq - 1)
        prev = jnp.where(pos == 0, 0.0, pltpu.roll(u, 1, 0))
        nxt = jnp.where(pos == seq - 1, 0.0, pltpu.roll(u, TM - 1, 0))
        cw = cw_ref[layer]
        conv = prev * cw[0:1] + u * cw[1:2] + nxt * cw[2:3]
        conv_ref[...] = (gb_s[...] * conv).astype(BF)


def _in_proj(layer, xs, mod, g_mix, w_in, cos_t, sin_t, conv_w, new_kv):
    pair = isinstance(xs, tuple)
    xa, xb = xs if pair else (xs, xs)
    spec_a, spec_b = _stream_specs(pair)
    ctx_i = lambda i: jnp.minimum(i, CTX_TILES - 1)
    in_specs = [
        spec_a, spec_b,
        WHOLE, WHOLE,
        pl.BlockSpec((None, D_MODEL, TN_IN), lambda i, j: (layer, 0, jnp.where(i == 0, j, N_IN_TILES - 1))),
        WHOLE, WHOLE, WHOLE,
    ]
    args = [xa, xb, mod, g_mix, w_in, cos_t, sin_t, conv_w]
    aliases = {}
    if new_kv is not None:
        in_specs += [pl.BlockSpec(memory_space=pl.ANY), pl.BlockSpec(memory_space=pl.ANY)]
        args += list(new_kv)
        aliases = {8: 4, 9: 5}
    row_tile = pl.BlockSpec((TM, ATT_WIDTH), lambda i, j: (i, 0))
    return pl.pallas_call(
        functools.partial(_in_kernel, layer=layer, aliased=new_kv is not None),
        grid=(N_TILES, N_IN_TILES),
        in_specs=in_specs,
        out_specs=[
            row_tile,
            row_tile,
            pl.BlockSpec((None, ATT_WIDTH, TM), lambda i, j: (i, 0, 0)),
            row_tile,
            pl.BlockSpec((SEQ_PER_TILE, None, ATT_WIDTH, SEQ), lambda i, j: (ctx_i(i), layer, 0, 0)),
            pl.BlockSpec((SEQ_PER_TILE, None, SEQ * N_HEADS, V_DIM), lambda i, j: (ctx_i(i), layer, 0, 0)),
        ],
        out_shape=[
            jax.ShapeDtypeStruct((N_TOK, ATT_WIDTH), BF),
            jax.ShapeDtypeStruct((N_TOK, ATT_WIDTH), BF),
            jax.ShapeDtypeStruct((N_TILES, ATT_WIDTH, TM), BF),
            jax.ShapeDtypeStruct((N_TOK, CONV_WIDTH), BF),
            jax.ShapeDtypeStruct((BATCH, DEPTH, ATT_WIDTH, SEQ), F32),
            jax.ShapeDtypeStruct((BATCH, DEPTH, SEQ * N_HEADS, V_DIM), F32),
        ],
        scratch_shapes=[
            pltpu.VMEM((TM, D_MODEL), BF),
            pltpu.VMEM((N_IN_TILES, D_MODEL, TN_IN), BF),
            pltpu.VMEM((TM, CONV_WIDTH), F32),
            pltpu.VMEM((TM, CONV_WIDTH), F32),
        ],
        input_output_aliases=aliases,
        compiler_params=_params(("arbitrary", "arbitrary")),
        name=f"in_proj_l{layer}",
    )(*args)


def _lambda(lq_ref, layer, lam_init):
    lq = lq_ref[layer]
    a = jnp.exp(jnp.sum(lq[0:1] * lq[1:2], axis=-1, keepdims=True))
    b = jnp.exp(jnp.sum(lq[2:3] * lq[3:4], axis=-1, keepdims=True))
    return a - b + lam_init


def _head_norm(o, sg, lam_init):
    return _rms(o) * sg * (1.0 - lam_init)


def _attn_ctx_kernel(q_ref, kt_ref, v_ref, lq_ref, sg_ref, o_ref, sc_s, *, layer, lam_init):
    lam = _lambda(lq_ref, layer, lam_init)
    sg = sg_ref[layer:layer + 1, :]

    units = [(b, h) for b in range(CTX_SEQ_PER_STEP) for h in range(N_HEADS)]

    def scores(u):
        b, h = units[u]
        pos = slice(b * SEQ, (b + 1) * SEQ)
        for s in range(2):
            d = slice(h * V_DIM + s * QK_DIM, h * V_DIM + (s + 1) * QK_DIM)
            sc_s[u % 2, s] = jnp.dot(q_ref[pos, d], kt_ref[d, pos], preferred_element_type=F32)

    def finish(u):
        b, h = units[u]
        pos = slice(b * SEQ, (b + 1) * SEQ)
        cols = slice(h * V_DIM, (h + 1) * V_DIM)
        v = v_ref[pos, cols]
        outs = []
        for s in range(2):
            sc = sc_s[u % 2, s]
            e = jnp.exp2(sc - jnp.max(sc, axis=-1, keepdims=True))
            r = 1.0 / jnp.sum(e, axis=-1, keepdims=True)
            outs.append(jnp.dot(e.astype(BF), v, preferred_element_type=F32) * r)
        o = outs[0] - lam * outs[1]
        o_ref[pos, cols] = _head_norm(o, sg, lam_init).astype(BF)

    scores(0)
    for u in range(len(units)):
        if u + 1 < len(units):
            scores(u + 1)
        finish(u)


def _attn_ctx(layer, q, kt, v, lambda_qk, subln_g, lam_init):
    rows = CTX_SEQ_PER_STEP * SEQ
    per_tile = TM // rows
    return pl.pallas_call(
        functools.partial(_attn_ctx_kernel, layer=layer, lam_init=lam_init),
        grid=(N_CTX // rows,),
        in_specs=[
            pl.BlockSpec((rows, ATT_WIDTH), lambda b: (b, 0)),
            pl.BlockSpec((None, ATT_WIDTH, rows), lambda b: (b // per_tile, 0, b % per_tile)),
            pl.BlockSpec((rows, ATT_WIDTH), lambda b: (b, 0)),
            WHOLE, WHOLE,
        ],
        out_specs=pl.BlockSpec((rows, ATT_WIDTH), lambda b: (b, 0)),
        out_shape=jax.ShapeDtypeStruct((N_TOK, ATT_WIDTH), BF),
        scratch_shapes=[pltpu.VMEM((2, 2, SEQ, SEQ), F32)],
        compiler_params=_params(("arbitrary",)),
        name=f"attn_ctx_l{layer}",
    )(q, kt, v, lambda_qk, subln_g)


TQ = 256


def _attn_lat_kernel(q_ref, kt_ref, v_ref, ckt_ref, cv_ref, lq_ref, sg_ref, att_in_ref, o_ref, sc_s, *,
                     layer, lam_init):
    del att_in_ref
    lam = _lambda(lq_ref, layer, lam_init)
    sg = sg_ref[layer:layer + 1, :]

    def scores(h, s):
        d = slice(h * V_DIM + s * QK_DIM, h * V_DIM + (s + 1) * QK_DIM)
        q = q_ref[:, d]
        sc_s[h % 2, s, :, :PAST_LEN] = jnp.dot(q, ckt_ref[d, :].astype(BF), preferred_element_type=F32)
        sc_s[h % 2, s, :, PAST_LEN:] = jnp.dot(q, kt_ref[d, :], preferred_element_type=F32)

    def softmax(h, s):
        sc = sc_s[h % 2, s]
        e = jnp.exp2(sc - jnp.max(sc, axis=-1, keepdims=True))
        return e, 1.0 / jnp.sum(e, axis=-1, keepdims=True)

    def finish(h, p1, p2):
        cols = slice(h * V_DIM, (h + 1) * V_DIM)
        e = jnp.concatenate([p1[0].astype(BF), p2[0].astype(BF)], axis=0)
        vc = cv_ref[pl.ds(h, PAST_LEN, stride=N_HEADS), :].astype(BF)
        pv = jnp.dot(e[:, :PAST_LEN], vc, preferred_element_type=F32)
        pv = pv + jnp.dot(e[:, PAST_LEN:], v_ref[:, cols], preferred_element_type=F32)
        o = pv[:TQ] * p1[1] - pv[TQ:] * (lam * p2[1])
        o_ref[:, cols] = _head_norm(o, sg, lam_init).astype(BF)

    scores(0, 0)
    scores(0, 1)
    for h in range(N_HEADS):
        more = h + 1 < N_HEADS
        if more:
            scores(h + 1, 0)
        p1 = softmax(h, 0)
        if more:
            scores(h + 1, 1)
        finish(h, p1, softmax(h, 1))


def _attn_lat(layer, q, kt, v, cache_kt, cache_v, lambda_qk, subln_g, att, lam_init):
    nqb = DEC_SEQ // TQ
    q0 = N_CTX // TQ
    return pl.pallas_call(
        functools.partial(_attn_lat_kernel, layer=layer, lam_init=lam_init),
        grid=(DEC_BATCH, nqb),
        in_specs=[
            pl.BlockSpec((TQ, ATT_WIDTH), lambda b, t: (q0 + b * nqb + t, 0)),
            pl.BlockSpec((None, ATT_WIDTH, DEC_SEQ), lambda b, t: (CTX_TILES + b, 0, 0)),
            pl.BlockSpec((DEC_SEQ, ATT_WIDTH), lambda b, t: (CTX_TILES + b, 0)),
            pl.BlockSpec((None, None, ATT_WIDTH, PAST_LEN), lambda b, t: (b, layer, 0, 0)),
            pl.BlockSpec((None, None, PAST_LEN * N_HEADS, V_DIM), lambda b, t: (b, layer, 0, 0)),
            WHOLE, WHOLE,
            pl.BlockSpec(memory_space=pl.ANY),
        ],
        out_specs=pl.BlockSpec((TQ, ATT_WIDTH), lambda b, t: (q0 + b * nqb + t, 0)),
        out_shape=jax.ShapeDtypeStruct((N_TOK, ATT_WIDTH), BF),
        scratch_shapes=[pltpu.VMEM((2, 2, TQ, PAST_LEN + DEC_SEQ), F32)],
        input_output_aliases={7: 0},
        compiler_params=_params(("arbitrary", "arbitrary")),
        name=f"attn_lat_l{layer}",
    )(q, kt, v, cache_kt, cache_v, lambda_qk, subln_g, att)


def _out_kernel(att_ref, conv_ref, w_ref, xa_ref, xb_ref, mod_ref, gf_ref, *rest, layer, route):
    if route:
        wr_ref, xo_ref, h2_ref, lp_ref, rg_ref, n_ref, wb_s = rest
    else:
        xo_ref, h2_ref, wb_s = rest
    i = pl.program_id(0)

    @pl.when(i == 0)
    def _():
        wb_s[...] = w_ref[...].astype(BF)

    mo = jnp.dot(att_ref[...], wb_s[:ATT_WIDTH, :], preferred_element_type=F32)
    mo = mo + jnp.dot(conv_ref[...], wb_s[ATT_WIDTH:, :], preferred_element_type=F32)
    xn = _stream_rows(xa_ref, xb_ref, i) + _mod(mod_ref, layer, i, 2) * mo
    xo_ref[...] = xn
    h2 = (_rms(xn) * gf_ref[layer:layer + 1, :]) * (1.0 + _mod(mod_ref, layer, i, 4)) + _mod(mod_ref, layer, i, 3)
    h2b = h2.astype(BF)
    h2_ref[...] = h2b
    if route:
        _route(h2b, wr_ref, lp_ref, rg_ref, n_ref)


def _out_proj(layer, att, conv, w_out, xs, mod, g_ffn, w_router_pad=None):
    pair = isinstance(xs, tuple)
    xa, xb = xs if pair else (xs, xs)
    spec_a, spec_b = _stream_specs(pair)
    row_spec = pl.BlockSpec((TM, D_MODEL), lambda i: (i, 0))
    lane_spec = pl.BlockSpec((TM, LANES), lambda i: (i, 0))
    in_specs = [
        pl.BlockSpec((TM, ATT_WIDTH), lambda i: (i, 0)),
        pl.BlockSpec((TM, CONV_WIDTH), lambda i: (i, 0)),
        pl.BlockSpec((None, D_MODEL, D_MODEL), lambda i: (layer, 0, 0)),
        spec_a, spec_b,
        WHOLE, WHOLE,
    ]
    args = [att, conv, w_out, xa, xb, mod, g_ffn]
    out_specs = [row_spec, row_spec]
    out_shape = [jax.ShapeDtypeStruct((N_TOK, D_MODEL), F32), jax.ShapeDtypeStruct((N_TOK, D_MODEL), BF)]
    route = w_router_pad is not None
    if route:
        in_specs.append(WHOLE)
        args.append(w_router_pad)
        out_specs += [lane_spec, lane_spec, pl.BlockSpec((None, 8, LANES), lambda i: (i, 0, 0))]
        out_shape += [jax.ShapeDtypeStruct((N_TOK, LANES), jnp.int32), jax.ShapeDtypeStruct((N_TOK, LANES), F32),
                      jax.ShapeDtypeStruct((N_TILES, 8, LANES), jnp.int32)]
    return pl.pallas_call(
        functools.partial(_out_kernel, layer=layer, route=route),
        grid=(N_TILES,),
        in_specs=in_specs,
        out_specs=out_specs,
        out_shape=out_shape,
        scratch_shapes=[pltpu.VMEM((D_MODEL, D_MODEL), BF)],
        compiler_params=_params(("arbitrary",)),
        name=f"out_proj_l{layer}",
    )(*args)


def _weight_pieces(layer):
    pieces = []
    for c0 in range(0, D_MODEL, STAGE_W_COLS):
        cols = slice(c0, c0 + STAGE_W_COLS)
        pieces.append((0, layer, slice(0, D_MODEL), cols, None, slice(0, D_MODEL), cols))
    for half in range(FF_HALVES):
        for part in range(2):
            src0 = part * D_FF + half * FF_HALF
            for off in range(0, FF_HALF, STAGE_W_COLS):
                n = min(STAGE_W_COLS, FF_HALF - off)
                pieces.append((1, 0, slice(0, D_MODEL), slice(src0 + off, src0 + off + n),
                               half, slice(0, D_MODEL), slice(part * FF_HALF + off, part * FF_HALF + off + n)))
    for r0 in range(0, D_FF, STAGE_W_ROWS):
        rows = slice(r0, min(r0 + STAGE_W_ROWS, D_FF))
        for c0 in range(0, D_MODEL, STAGE_W_COLS):
            cols = slice(c0, c0 + STAGE_W_COLS)
            pieces.append((2, 0, rows, cols, None, rows, cols))
    return pieces


def _load_dense_weights(layer, hbm, resident, stage_s, sem):
    pieces = _weight_pieces(layer)

    def copy(k):
        src, idx, rows, cols, _, _, _ = pieces[k]
        nr, nc = rows.stop - rows.start, cols.stop - cols.start
        return pltpu.make_async_copy(hbm[src].at[idx, rows, cols], stage_s.at[k % 2, :nr, :nc], sem.at[k % 2])

    copy(0).start()
    for k, (src, _, rows, cols, didx, drows, dcols) in enumerate(pieces):
        if k + 1 < len(pieces):
            copy(k + 1).start()
        copy(k).wait()
        nr, nc = rows.stop - rows.start, cols.stop - cols.start
        piece = stage_s[k % 2, :nr, :nc].astype(BF)
        if didx is None:
            resident[src][drows, dcols] = piece
        else:
            resident[src][didx, drows, dcols] = piece


def _dense_layer_kernel(att_ref, conv_ref, xa_ref, xb_ref, mod_ref, gf_ref, wo_hbm, wgu_hbm, wd_hbm,
                        o_ref, wo_b, wgu_b, wd_b, stage_s, sem, *, layer):
    i = pl.program_id(0)
    tile = i // (TM // FF_TM)

    @pl.when(i == 0)
    def _():
        _load_dense_weights(layer, (wo_hbm, wgu_hbm, wd_hbm), (wo_b, wgu_b, wd_b), stage_s, sem)

    mo = jnp.dot(att_ref[...], wo_b[:ATT_WIDTH, :], preferred_element_type=F32)
    mo = mo + jnp.dot(conv_ref[...], wo_b[ATT_WIDTH:, :], preferred_element_type=F32)
    x = jnp.where(tile >= CTX_TILES, xb_ref[...], xa_ref[...])
    xn = x + _mod(mod_ref, layer, tile, 2) * mo
    h = (_rms(xn) * gf_ref[layer:layer + 1, :]) * (1.0 + _mod(mod_ref, layer, tile, 4)) + _mod(mod_ref, layer, tile, 3)
    h = h.astype(BF)
    y = None
    for half in range(FF_HALVES):
        gu = jnp.dot(h, wgu_b[half], preferred_element_type=F32)
        act = (_silu(gu[:, :FF_HALF]) * gu[:, FF_HALF:]).astype(BF)
        part = jnp.dot(act, wd_b[half * FF_HALF:(half + 1) * FF_HALF, :], preferred_element_type=F32)
        y = part if y is None else y + part
    o_ref[...] = xn + _mod(mod_ref, layer, tile, 5) * y


def _dense_layer(layer, att, conv, xs, mod, g_ffn, w_out, w_gu, w_down):
    xa, xb = xs
    n_ctx = N_CTX // FF_TM
    row = lambda width: pl.BlockSpec((FF_TM, width), lambda i: (i, 0))
    hbm = pl.BlockSpec(memory_space=pl.ANY)
    return pl.pallas_call(
        functools.partial(_dense_layer_kernel, layer=layer),
        grid=(N_TOK // FF_TM,),
        in_specs=[
            row(ATT_WIDTH), row(CONV_WIDTH),
            pl.BlockSpec((FF_TM, D_MODEL), lambda i: (jnp.minimum(i, n_ctx - 1), 0)),
            pl.BlockSpec((FF_TM, D_MODEL), lambda i: (jnp.maximum(i - n_ctx, 0), 0)),
            WHOLE, WHOLE, hbm, hbm, hbm,
        ],
        out_specs=row(D_MODEL),
        out_shape=jax.ShapeDtypeStruct((N_TOK, D_MODEL), F32),
        scratch_shapes=[
            pltpu.VMEM((D_MODEL, D_MODEL), BF),
            pltpu.VMEM((FF_HALVES, D_MODEL, 2 * FF_HALF), BF),
            pltpu.VMEM((D_FF, D_MODEL), BF),
            pltpu.VMEM((2, STAGE_W_ROWS, STAGE_W_COLS), F32),
            pltpu.SemaphoreType.DMA((2,)),
        ],
        compiler_params=_params(("arbitrary",)),
        name="dense_layer",
    )(att, conv, xa, xb, mod, g_ffn, w_out, w_gu, w_down)


def _route(h, wr_ref, lp_ref, rg_ref, n_ref):
    logits = jnp.dot(h, wr_ref[...].astype(BF), preferred_element_type=F32)
    lane = lax.broadcasted_iota(jnp.int32, logits.shape, 1)
    lg = jnp.where(lane < N_EXPERTS, logits, -jnp.inf)
    m1 = jnp.max(lg, axis=-1, keepdims=True)
    i1 = jnp.min(jnp.where(lg == m1, lane, LANES), axis=-1, keepdims=True)
    lg2 = jnp.where(lane == i1, -jnp.inf, lg)
    m2 = jnp.max(lg2, axis=-1, keepdims=True)
    i2 = jnp.min(jnp.where(lg2 == m2, lane, LANES), axis=-1, keepdims=True)
    e2 = jnp.exp(m2 - m1)
    w1 = 1.0 / (1.0 + e2)
    w2 = e2 / (1.0 + e2)

    sel1 = lane == i1
    sel2 = lane == i2
    onehot = jnp.logical_or(sel1, sel2)
    rows = lax.broadcasted_iota(jnp.int32, (TM, TM), 0)
    colsi = lax.broadcasted_iota(jnp.int32, (TM, TM), 1)
    earlier = jnp.logical_and(colsi < rows, (colsi // ST) == (rows // ST))
    before = jnp.dot(earlier.astype(BF), onehot.astype(BF), preferred_element_type=F32)
    onehot_f = onehot.astype(F32)
    counts = [jnp.sum(onehot_f[s * ST:(s + 1) * ST], axis=0, keepdims=True) for s in range(SUB_PER_TILE)]
    counts = jnp.concatenate(counts + [jnp.zeros((8 - SUB_PER_TILE, LANES), F32)], axis=0).astype(jnp.int32)
    seg_len = ((counts + (SEG_ALIGN - 1)) // SEG_ALIGN) * SEG_ALIGN
    n_ref[...] = seg_len
    la = lax.broadcasted_iota(jnp.int32, (LANES, LANES), 0)
    lb = lax.broadcasted_iota(jnp.int32, (LANES, LANES), 1)
    seg_start = jnp.dot(seg_len.astype(F32).astype(BF), (la < lb).astype(BF), preferred_element_type=F32)
    start = jnp.concatenate(
        [jnp.broadcast_to(seg_start[s:s + 1], (ST, LANES)) for s in range(SUB_PER_TILE)], axis=0)
    where = before + start
    lp1 = jnp.sum(jnp.where(sel1, where, 0.0), axis=-1, keepdims=True).astype(jnp.int32)
    lp2 = jnp.sum(jnp.where(sel2, where, 0.0), axis=-1, keepdims=True).astype(jnp.int32)
    lp_ref[...] = jnp.where(lane == 0, lp1, jnp.where(lane == 1, lp2, 0))
    rg_ref[...] = jnp.where(lane == 0, w1, jnp.where(lane == 1, w2, 0.0))


def _chunk_copies(s, cnt_ref, cdst_ref, stage, rows_hbm, sem, *, to_hbm, wait):
    def copy(v, h):
        return pltpu.make_async_copy(v, h, sem) if to_hbm else pltpu.make_async_copy(h, v, sem)

    if wait:
        for z in WAIT_PIECES:
            @pl.when((cnt_ref[s] & z) != 0)
            def _():
                copy(stage.at[pl.ds(0, z * SEG_ALIGN)], rows_hbm.at[pl.ds(0, z * SEG_ALIGN)]).wait()
        return

    def one(c, carry):
        v = stage.at[pl.ds(pl.multiple_of(c * SEG_ALIGN, SEG_ALIGN), SEG_ALIGN)]
        h = rows_hbm.at[pl.ds(pl.multiple_of(cdst_ref[s * STAGE_CHUNKS + c], SEG_ALIGN), SEG_ALIGN)]
        copy(v, h).start()
        return carry

    lax.fori_loop(0, cnt_ref[s], one, 0)


def _dispatch_kernel(cnt_ref, cdst_ref, h_ref, lp_ref, xs_ref, stage_s, sem):
    copies = functools.partial(_chunk_copies, cnt_ref=cnt_ref, cdst_ref=cdst_ref, rows_hbm=xs_ref, to_hbm=True)
    for k in range(SUB_PER_TILE):
        s = pl.program_id(0) * SUB_PER_TILE + k
        slot = k % 2
        rows = slice(k * ST, (k + 1) * ST)

        @pl.when(s >= 2)
        def _():
            copies(s - 2, stage=stage_s.at[slot], sem=sem.at[slot], wait=True)

        lpt = lp_ref[rows, :].T
        r = lax.broadcasted_iota(jnp.int32, (STAGE_ROWS, ST), 0)
        pick = jnp.logical_or(r == lpt[0:1, :], r == lpt[1:2, :]).astype(BF)
        stage_s[slot] = jnp.dot(pick, h_ref[rows, :], preferred_element_type=F32).astype(BF)
        copies(s, stage=stage_s.at[slot], sem=sem.at[slot], wait=False)

    @pl.when(pl.program_id(0) == N_TILES - 1)
    def _():
        copies(N_SUB - 2, stage=stage_s.at[0], sem=sem.at[0], wait=True)
        copies(N_SUB - 1, stage=stage_s.at[1], sem=sem.at[1], wait=True)


def _dispatch(cnt, cdst, h2, lp):
    assert SUB_PER_TILE % 2 == 0
    grid_spec = pltpu.PrefetchScalarGridSpec(
        num_scalar_prefetch=2,
        grid=(N_TILES,),
        in_specs=[
            pl.BlockSpec((TM, D_MODEL), lambda i, *_: (i, 0)),
            pl.BlockSpec((TM, LANES), lambda i, *_: (i, 0)),
        ],
        out_specs=pl.BlockSpec(memory_space=pl.ANY),
        scratch_shapes=[pltpu.VMEM((2, STAGE_ROWS, D_MODEL), BF), pltpu.SemaphoreType.DMA((2,))],
    )
    return pl.pallas_call(
        _dispatch_kernel,
        grid_spec=grid_spec,
        out_shape=jax.ShapeDtypeStruct((R_PAD, D_MODEL), BF),
        compiler_params=_params(("arbitrary",)),
        name="moe_dispatch",
    )(cnt, cdst, h2, lp)


def _expert_weights(te_ref, nt_ref, nxt_ref, w_hbm, wf_s, wb_s, sem):
    r = pl.program_id(0)

    def fetch(e):
        return pltpu.make_async_copy(w_hbm.at[0, e], wf_s, sem)

    @pl.when(r == 0)
    def _():
        fetch(te_ref[0]).start()

    first = jnp.logical_or(r == 0, te_ref[r] != te_ref[jnp.maximum(r - 1, 0)])

    @pl.when(jnp.logical_and(r < nt_ref[0], first))
    def _():
        fetch(te_ref[r]).wait()
        wb_s[...] = wf_s[...].astype(BF)

        @pl.when(nxt_ref[r] >= 0)
        def _():
            fetch(nxt_ref[r]).start()


def _moe_ffn_kernel(te_ref, nt_ref, nxt_ref, used_ref, x_ref, wgu_hbm, wd_hbm, o_ref,
                    wgu_f, wgu_b, wd_f, wd_b, sem):
    r = pl.program_id(0)
    _expert_weights(te_ref, nt_ref, nxt_ref, wgu_hbm, wgu_f, wgu_b, sem.at[0])
    _expert_weights(te_ref, nt_ref, nxt_ref, wd_hbm, wd_f, wd_b, sem.at[1])

    def ffn(rows):
        gu = jnp.dot(x_ref[rows, :], wgu_b[...], preferred_element_type=F32)
        act = (_silu(gu[:, :D_FF_EXPERT]) * gu[:, D_FF_EXPERT:]).astype(BF)
        o_ref[rows, :] = jnp.dot(act, wd_b[...], preferred_element_type=F32).astype(BF)

    @pl.when(jnp.logical_and(r < nt_ref[0], used_ref[r] > TG // 2))
    def _():
        ffn(slice(0, TG))

    @pl.when(jnp.logical_and(r < nt_ref[0], used_ref[r] <= TG // 2))
    def _():
        ffn(slice(0, TG // 2))


def _moe_ffn(te, nt, nxt, used, rows, w_gu, w_down):
    tile_map = lambda r, te, nt, nxt, used: (jnp.minimum(r, nt[0] - 1), 0)
    grid_spec = pltpu.PrefetchScalarGridSpec(
        num_scalar_prefetch=4,
        grid=(NT_G,),
        in_specs=[pl.BlockSpec((TG, D_MODEL), tile_map),
                  pl.BlockSpec(memory_space=pl.ANY), pl.BlockSpec(memory_space=pl.ANY)],
        out_specs=pl.BlockSpec((TG, D_MODEL), tile_map),
        scratch_shapes=[
            pltpu.VMEM((D_MODEL, 2 * D_FF_EXPERT), F32), pltpu.VMEM((D_MODEL, 2 * D_FF_EXPERT), BF),
            pltpu.VMEM((D_FF_EXPERT, D_MODEL), F32), pltpu.VMEM((D_FF_EXPERT, D_MODEL), BF),
            pltpu.SemaphoreType.DMA((2,)),
        ],
    )
    return pl.pallas_call(
        _moe_ffn_kernel,
        grid_spec=grid_spec,
        out_shape=jax.ShapeDtypeStruct((R_PAD, D_MODEL), BF),
        compiler_params=_params(("arbitrary",)),
        name="moe_ffn",
    )(te, nt, nxt, used, rows, w_gu, w_down)


def _combine_kernel(cnt_ref, cdst_ref, ys_ref, lp_ref, rg_ref, x_ref, mod_ref, fg_ref,
                    oa_ref, ob_ref, stage_s, sem, *, layer):
    i = pl.program_id(0)
    copies = functools.partial(_chunk_copies, cnt_ref=cnt_ref, cdst_ref=cdst_ref, rows_hbm=ys_ref, to_hbm=False)

    @pl.when(i == 0)
    def _():
        stage_s[...] = jnp.zeros_like(stage_s)
        for s in range(COMBINE_AHEAD):
            copies(s, stage=stage_s.at[s], sem=sem.at[s], wait=False)

    for k in range(SUB_PER_TILE):
        s = i * SUB_PER_TILE + k
        slot = k % COMBINE_SLOTS
        ahead = (k + COMBINE_AHEAD) % COMBINE_SLOTS
        rows = slice(k * ST, (k + 1) * ST)

        @pl.when(s + COMBINE_AHEAD < N_SUB)
        def _():
            copies(s + COMBINE_AHEAD, stage=stage_s.at[ahead], sem=sem.at[ahead], wait=False)

        copies(s, stage=stage_s.at[slot], sem=sem.at[slot], wait=True)

        lp = lp_ref[rows, :]
        r = lax.broadcasted_iota(jnp.int32, (ST, STAGE_ROWS), 1)
        staged = stage_s[slot]
        a = jnp.dot((r == lp[:, 0:1]).astype(BF), staged, preferred_element_type=F32)
        b = jnp.dot((r == lp[:, 1:2]).astype(BF), staged, preferred_element_type=F32)
        rg = rg_ref[rows, :]
        y = rg[:, 0:1] * a + rg[:, 1:2] * b
        xn = x_ref[rows, :] + _mod(mod_ref, layer, i, 5) * y
        out = _rms(xn) * fg_ref[...]

        @pl.when(i < CTX_TILES)
        def _():
            oa_ref[rows, :] = out

        @pl.when(i >= CTX_TILES)
        def _():
            ob_ref[rows, :] = out


def _combine(layer, cnt, cdst, ys, lp, rg, x, mod, final_g):
    assert SUB_PER_TILE % COMBINE_SLOTS == 0
    grid_spec = pltpu.PrefetchScalarGridSpec(
        num_scalar_prefetch=2,
        grid=(N_TILES,),
        in_specs=[
            pl.BlockSpec(memory_space=pl.ANY),
            pl.BlockSpec((TM, LANES), lambda i, *_: (i, 0)),
            pl.BlockSpec((TM, LANES), lambda i, *_: (i, 0)),
            pl.BlockSpec((TM, D_MODEL), lambda i, *_: (i, 0)),
            WHOLE, WHOLE,
        ],
        out_specs=[
            pl.BlockSpec((TM, D_MODEL), lambda i, *_: (jnp.minimum(i, CTX_TILES - 1), 0)),
            pl.BlockSpec((TM, D_MODEL), lambda i, *_: (jnp.maximum(i - CTX_TILES, 0), 0)),
        ],
        scratch_shapes=[pltpu.VMEM((COMBINE_SLOTS, STAGE_ROWS, D_MODEL), BF),
                        pltpu.SemaphoreType.DMA((COMBINE_SLOTS,))],
    )
    return pl.pallas_call(
        functools.partial(_combine_kernel, layer=layer),
        grid_spec=grid_spec,
        out_shape=[
            jax.ShapeDtypeStruct((N_CTX, D_MODEL), F32),
            jax.ShapeDtypeStruct((N_LAT, D_MODEL), F32),
        ],
        compiler_params=_params(("arbitrary",)),
        name="moe_combine",
    )(cnt, cdst, ys, lp, rg, x, mod, final_g)


def _group_layout(n_tiles):
    n = n_tiles[:, :SUB_PER_TILE, :N_EXPERTS].reshape(N_SUB, N_EXPERTS)
    tiles = (jnp.sum(n, axis=0) + TG - 1) // TG
    tile_end = jnp.cumsum(tiles)
    region = (tile_end - tiles) * TG
    dst = region[None, :] + jnp.cumsum(n, axis=0) - n
    seg_end = jnp.cumsum(n, axis=1)
    seg = seg_end - n
    row = jnp.arange(STAGE_CHUNKS, dtype=jnp.int32) * SEG_ALIGN
    owner = jnp.sum((row[None, :, None] >= seg_end[:, None, :]).astype(jnp.int32), axis=-1)
    own = jnp.minimum(owner, N_EXPERTS - 1)[..., None] == jnp.arange(N_EXPERTS)
    cdst = jnp.sum(jnp.where(own, (dst - seg)[:, None, :], 0), axis=-1) + row[None, :]
    cnt = seg_end[:, -1] // SEG_ALIGN
    nt = tile_end[-1]
    tile_id = jnp.minimum(jnp.arange(NT_G, dtype=jnp.int32), nt - 1)
    te = jnp.sum((tile_id[:, None] >= tile_end[None, :]).astype(jnp.int32), axis=-1)
    after = jnp.sum(jnp.where(te[:, None] == jnp.arange(N_EXPERTS), tile_end[None, :], 0), axis=-1)
    nxt = jnp.where(after < nt, jnp.sum((after[:, None] >= tile_end[None, :]).astype(jnp.int32), axis=-1), -1)
    mine = te[:, None] == jnp.arange(N_EXPERTS)
    region_end = jnp.sum(jnp.where(mine, (region + jnp.sum(n, axis=0))[None, :], 0), axis=-1)
    used = jnp.clip(region_end - tile_id * TG, 0, TG)
    i32 = lambda a: a.astype(jnp.int32)
    return (i32(cnt), i32(cdst.reshape(N_SUB * STAGE_CHUNKS)), i32(te), i32(nt.reshape(1)), i32(nxt), i32(used))


def _rope_tables():
    p = np.arange(DEC_SEQ)
    row = (p // GRID_W).astype(np.float32)
    col = (p % GRID_W).astype(np.float32)
    half = QK_DIM // 4
    freqs = (ROPE_BASE ** (-np.arange(half, dtype=np.float32) / half)).astype(np.float32)
    lane = np.arange(V_DIM)
    f = freqs[lane & (half - 1)]
    use_col = (lane & (2 * half)) != 0
    ang = (np.where(use_col[None, :], col[:, None], row[:, None]) * f[None, :]).astype(np.float32)
    upper = (lane & half) != 0
    sin = np.sin(ang)
    return jnp.asarray(np.cos(ang), F32), jnp.asarray(np.where(upper[None, :], sin, -sin), F32)


def kernel(x_prompt, x_sample, cache_k, cache_v, c, c_ctx, w_ada, b_ada, norm_mix_g, norm_ffn_g,
           w_in, lambda_qk, subln_g, conv_w, w_out, w_gu_dense, w_down_dense, w_router,
           w_gu_moe, w_down_moe, final_g):
    assert DEPTH == 2
    xs = (x_prompt.reshape(N_CTX, D_MODEL), x_sample.reshape(N_LAT, D_MODEL))
    cond = jnp.concatenate([c_ctx[None, :], c, jnp.zeros((COND_ROWS - 1 - DEC_BATCH, D_MODEL), F32)], axis=0)
    mod = _ada(cond, w_ada, b_ada)
    cos_t, sin_t = _rope_tables()
    cache_kt = jnp.transpose(cache_k, (0, 1, 3, 4, 5, 2)).reshape(DEC_BATCH, DEPTH, ATT_WIDTH, PAST_LEN)
    cache_v4 = cache_v.reshape(DEC_BATCH, DEPTH, PAST_LEN * N_HEADS, V_DIM)

    new_kv = None
    for layer in range(DEPTH):
        lam_init = 0.8 - 0.6 * math.exp(-0.3 * layer)
        q, v, kt, conv, nk, nv = _in_proj(layer, xs, mod, norm_mix_g, w_in, cos_t, sin_t, conv_w, new_kv)
        new_kv = (nk, nv)
        att = _attn_ctx(layer, q, kt, v, lambda_qk, subln_g, lam_init)
        att = _attn_lat(layer, q, kt, v, cache_kt, cache_v4, lambda_qk, subln_g, att, lam_init)
        if layer == 0:
            xs = _dense_layer(layer, att, conv, xs, mod, norm_ffn_g, w_out, w_gu_dense, w_down_dense)
        else:
            wr = jnp.pad(w_router[0], ((0, 0), (0, LANES - N_EXPERTS)))
            x1, h2, lp, rg, n_tiles = _out_proj(layer, att, conv, w_out, xs, mod, norm_ffn_g, wr)
            cnt, cdst, te, nt, nxt, used = _group_layout(n_tiles)
            xsort = _dispatch(cnt, cdst, h2, lp)
            ys = _moe_ffn(te, nt, nxt, used, xsort, w_gu_moe, w_down_moe)
            y_ctx, y_lat = _combine(layer, cnt, cdst, ys, lp, rg, x1, mod, final_g.reshape(1, D_MODEL))
    nk, nv = new_kv
    new_k = jnp.transpose(nk.reshape(BATCH, DEPTH, N_HEADS, 2, QK_DIM, SEQ), (0, 1, 5, 2, 3, 4))
    new_v = nv.reshape(BATCH, DEPTH, SEQ, N_HEADS, V_DIM)
    return (y_ctx.reshape(BATCH, SEQ, D_MODEL), y_lat.reshape(DEC_BATCH, DEC_SEQ, D_MODEL), new_k, new_v)
```

```python
import functools
import math

import numpy as np
import jax
import jax.numpy as jnp
from jax import lax
from jax.experimental import pallas as pl
from jax.experimental.pallas import tpu as pltpu

D_MODEL = 1024
BATCH = 16
SEQ = 256
DEPTH = 2
DEC_BATCH = 4
DEC_SEQ = 1024
PAST_LEN = 512
GRID_W = 64
ATT_WIDTH = 512
CONV_WIDTH = 512
N_HEADS = 4
V_DIM = 128
QK_DIM = 64
ROPE_BASE = 10000.0
D_FF = 2816
N_EXPERTS = 8
D_FF_EXPERT = 1408
N_MOD = 6
NORM_EPS = 1e-6
Q_SCALE = QK_DIM ** -0.5 * math.log2(math.e)
IN_COLS = 3 * ATT_WIDTH + 3 * CONV_WIDTH

N_CTX = BATCH * SEQ
N_LAT = DEC_BATCH * DEC_SEQ
N_TOK = N_CTX + N_LAT
TM = 1024
N_TILES = N_TOK // TM
CTX_TILES = N_CTX // TM
SEQ_PER_TILE = TM // SEQ
CTX_SEQ_PER_STEP = 1
COND_ROWS = 8
TN_IN = 1024
N_IN_TILES = IN_COLS // TN_IN
ROW_CHUNK = 512
FF_TM = 512
FF_HALVES = 2
FF_HALF = D_FF // FF_HALVES
STAGE_W_ROWS = 1024
STAGE_W_COLS = 512
TN_ADA = 1536
TG = 512
ST = 256
SUB_PER_TILE = TM // ST
N_SUB = N_TOK // ST
CTX_SUB = N_CTX // ST
SEG_ALIGN = 8
STAGE_ROWS = 640
STAGE_CHUNKS = STAGE_ROWS // SEG_ALIGN
WAIT_PIECES = (64, 32, 16, 8, 4, 2, 1)
COMBINE_SLOTS = 4
COMBINE_AHEAD = 2
NT_G = -(-(2 * N_TOK + N_SUB * N_EXPERTS * (SEG_ALIGN - 1) + N_EXPERTS * (TG - SEG_ALIGN)) // TG)
R_PAD = NT_G * TG
LANES = 128
VMEM_LIMIT = 60 * 1024 * 1024

BF = jnp.bfloat16
F32 = jnp.float32


def _params(sem, vmem=VMEM_LIMIT):
    return pltpu.CompilerParams(dimension_semantics=sem, vmem_limit_bytes=vmem)


def _mod_row(i):
    return jnp.where(i < CTX_TILES, 0, i - (CTX_TILES - 1))


WHOLE = pl.BlockSpec(memory_space=pltpu.VMEM)


def _mod(mod_ref, layer, i, c):
    return mod_ref[layer, pl.ds(_mod_row(i), 1), c * D_MODEL:(c + 1) * D_MODEL]


def _stream_specs(pair, width=D_MODEL):
    a = pl.BlockSpec((TM, width), lambda i, *_: (jnp.minimum(i, CTX_TILES - 1), 0))
    if pair:
        b = pl.BlockSpec((TM, width), lambda i, *_: (jnp.maximum(i - CTX_TILES, 0), 0))
    else:
        b = pl.BlockSpec((TM, width), lambda i, *_: (jnp.maximum(i, CTX_TILES), 0))
    return a, b


def _stream_rows(xa_ref, xb_ref, i):
    return jnp.where(i >= CTX_TILES, xb_ref[...], xa_ref[...])


def _silu(x):
    return x / (1.0 + jnp.exp(-x))


def _rms(x):
    return x * lax.rsqrt(jnp.mean(x * x, axis=-1, keepdims=True) + NORM_EPS)


def _ada_kernel(c_ref, w_ref, b_ref, o_ref):
    s = _silu(c_ref[...]).astype(BF)
    bias = b_ref[pl.ds(pl.program_id(0), 1), :]
    o_ref[...] = jnp.dot(s, w_ref[...].astype(BF), preferred_element_type=F32) + bias


def _ada(cond, w_ada, b_ada):
    n = N_MOD * D_MODEL
    return pl.pallas_call(
        _ada_kernel,
        grid=(DEPTH, n // TN_ADA),
        in_specs=[
            pl.BlockSpec((COND_ROWS, D_MODEL), lambda l, j: (0, 0)),
            pl.BlockSpec((None, D_MODEL, TN_ADA), lambda l, j: (l, 0, j)),
            pl.BlockSpec((DEPTH, TN_ADA), lambda l, j: (0, j)),
        ],
        out_specs=pl.BlockSpec((None, COND_ROWS, TN_ADA), lambda l, j: (l, 0, j)),
        out_shape=jax.ShapeDtypeStruct((DEPTH, COND_ROWS, n), F32),
        compiler_params=_params(("arbitrary", "arbitrary")),
        name="ada_mod",
    )(cond, w_ada, b_ada)


def _in_kernel(*refs, layer, aliased):
    xa_ref, xb_ref, mod_ref, g_ref, w_ref, cos_ref, sin_ref, cw_ref = refs[:8]
    refs = refs[10:] if aliased else refs[8:]
    q_ref, v_ref, kt_ref, conv_ref, nk_ref, nv_ref, h_s, wb_s, gb_s, gc_s = refs
    i = pl.program_id(0)
    j = pl.program_id(1)
    lat = i >= CTX_TILES
    ctx = jnp.logical_not(lat)

    @pl.when(i == 0)
    def _():
        wb_s[j] = w_ref[...].astype(BF)

    chunks = [slice(c * ROW_CHUNK, (c + 1) * ROW_CHUNK) for c in range(TM // ROW_CHUNK)]
    seqs_per_chunk = ROW_CHUNK // SEQ

    def norm(rows):
        gain = g_ref[layer:layer + 1, :] * (1.0 + _mod(mod_ref, layer, i, 1))
        x = jnp.where(lat, xb_ref[rows, :], xa_ref[rows, :])
        h_s[rows, :] = (_rms(x) * gain + _mod(mod_ref, layer, i, 0)).astype(BF)

    def proj(rows):
        acc = jnp.dot(h_s[rows, :], wb_s[j], preferred_element_type=F32)
        return acc[:, :ATT_WIDTH], acc[:, ATT_WIDTH:]

    def roped(a, rows):
        cos = jnp.concatenate([cos_ref[rows, :]] * N_HEADS, axis=1)
        sin = jnp.concatenate([sin_ref[rows, :]] * N_HEADS, axis=1)
        lane = lax.broadcasted_iota(jnp.int32, a.shape, 1)
        upper = (lane & (QK_DIM // 4)) != 0
        partner = jnp.where(upper, pltpu.roll(a, QK_DIM // 4, 1), pltpu.roll(a, ATT_WIDTH - QK_DIM // 4, 1))
        return a * cos + partner * sin

    @pl.when(jnp.logical_and(j == 0, lat))
    def _():
        for rows in chunks:
            norm(rows)
            q, k = proj(rows)
            q_ref[rows, :] = (roped(q, rows) * Q_SCALE).astype(BF)
            kt_ref[:, rows] = roped(k, rows).T.astype(BF)

    @pl.when(jnp.logical_and(j == 0, ctx))
    def _():
        for c, rows in enumerate(chunks):
            norm(rows)
            q, k = proj(rows)
            q_ref[rows, :] = (q * Q_SCALE).astype(BF)
            kt = k.T
            kt_ref[:, rows] = kt.astype(BF)
            for s in range(seqs_per_chunk):
                nk_ref[c * seqs_per_chunk + s] = kt[:, s * SEQ:(s + 1) * SEQ]

    @pl.when(jnp.logical_and(j == 1, lat))
    def _():
        for rows in chunks:
            v, gb = proj(rows)
            v_ref[rows, :] = v.astype(BF)
            gb_s[rows, :] = gb

    @pl.when(jnp.logical_and(j == 1, ctx))
    def _():
        for c, rows in enumerate(chunks):
            v, gb = proj(rows)
            v_ref[rows, :] = v.astype(BF)
            gb_s[rows, :] = gb
            for s in range(seqs_per_chunk):
                for h in range(N_HEADS):
                    nv_ref[c * seqs_per_chunk + s, pl.ds(h, SEQ, stride=N_HEADS), :] = (
                        v[s * SEQ:(s + 1) * SEQ, h * V_DIM:(h + 1) * V_DIM])

    @pl.when(j == 2)
    def _():
        for rows in chunks:
            gc, xin = proj(rows)
            gc_s[rows, :] = gc * xin
        u = gc_s[...]
        seq = jnp.where(lat, DEC_SEQ, SEQ)
        pos = lax.broadcasted_iota(jnp.int32, (TM, 1), 0) & (seq - 1)
        prev = jnp.where(pos == 0, 0.0, pltpu.roll(u, 1, 0))
        nxt = jnp.where(pos == seq - 1, 0.0, pltpu.roll(u, TM - 1, 0))
        cw = cw_ref[layer]
        conv = prev * cw[0:1] + u * cw[1:2] + nxt * cw[2:3]
        conv_ref[...] = (gb_s[...] * conv).astype(BF)


def _in_proj(layer, xs, mod, g_mix, w_in, cos_t, sin_t, conv_w, new_kv):
    pair = isinstance(xs, tuple)
    xa, xb = xs if pair else (xs, xs)
    spec_a, spec_b = _stream_specs(pair)
    ctx_i = lambda i: jnp.minimum(i, CTX_TILES - 1)
    in_specs = [
        spec_a, spec_b,
        WHOLE, WHOLE,
        pl.BlockSpec((None, D_MODEL, TN_IN), lambda i, j: (layer, 0, jnp.where(i == 0, j, N_IN_TILES - 1))),
        WHOLE, WHOLE, WHOLE,
    ]
    args = [xa, xb, mod, g_mix, w_in, cos_t, sin_t, conv_w]
    aliases = {}
    if new_kv is not None:
        in_specs += [pl.BlockSpec(memory_space=pl.ANY), pl.BlockSpec(memory_space=pl.ANY)]
        args += list(new_kv)
        aliases = {8: 4, 9: 5}
    row_tile = pl.BlockSpec((TM, ATT_WIDTH), lambda i, j: (i, 0))
    return pl.pallas_call(
        functools.partial(_in_kernel, layer=layer, aliased=new_kv is not None),
        grid=(N_TILES, N_IN_TILES),
        in_specs=in_specs,
        out_specs=[
            row_tile,
            row_tile,
            pl.BlockSpec((None, ATT_WIDTH, TM), lambda i, j: (i, 0, 0)),
            row_tile,
            pl.BlockSpec((SEQ_PER_TILE, None, ATT_WIDTH, SEQ), lambda i, j: (ctx_i(i), layer, 0, 0)),
            pl.BlockSpec((SEQ_PER_TILE, None, SEQ * N_HEADS, V_DIM), lambda i, j: (ctx_i(i), layer, 0, 0)),
        ],
        out_shape=[
            jax.ShapeDtypeStruct((N_TOK, ATT_WIDTH), BF),
            jax.ShapeDtypeStruct((N_TOK, ATT_WIDTH), BF),
            jax.ShapeDtypeStruct((N_TILES, ATT_WIDTH, TM), BF),
            jax.ShapeDtypeStruct((N_TOK, CONV_WIDTH), BF),
            jax.ShapeDtypeStruct((BATCH, DEPTH, ATT_WIDTH, SEQ), F32),
            jax.ShapeDtypeStruct((BATCH, DEPTH, SEQ * N_HEADS, V_DIM), F32),
        ],
        scratch_shapes=[
            pltpu.VMEM((TM, D_MODEL), BF),
            pltpu.VMEM((N_IN_TILES, D_MODEL, TN_IN), BF),
            pltpu.VMEM((TM, CONV_WIDTH), F32),
            pltpu.VMEM((TM, CONV_WIDTH), F32),
        ],
        input_output_aliases=aliases,
        compiler_params=_params(("arbitrary", "arbitrary")),
        name=f"in_proj_l{layer}",
    )(*args)


def _lambda(lq_ref, layer, lam_init):
    lq = lq_ref[layer]
    a = jnp.exp(jnp.sum(lq[0:1] * lq[1:2], axis=-1, keepdims=True))
    b = jnp.exp(jnp.sum(lq[2:3] * lq[3:4], axis=-1, keepdims=True))
    return a - b + lam_init


def _head_norm(o, sg, lam_init):
    return _rms(o) * sg * (1.0 - lam_init)


def _attn_ctx_kernel(q_ref, kt_ref, v_ref, lq_ref, sg_ref, o_ref, sc_s, *, layer, lam_init):
    lam = _lambda(lq_ref, layer, lam_init)
    sg = sg_ref[layer:layer + 1, :]

    units = [(b, h) for b in range(CTX_SEQ_PER_STEP) for h in range(N_HEADS)]

    def scores(u):
        b, h = units[u]
        pos = slice(b * SEQ, (b + 1) * SEQ)
        for s in range(2):
            d = slice(h * V_DIM + s * QK_DIM, h * V_DIM + (s + 1) * QK_DIM)
            sc_s[u % 2, s] = jnp.dot(q_ref[pos, d], kt_ref[d, pos], preferred_element_type=F32)

    def finish(u):
        b, h = units[u]
        pos = slice(b * SEQ, (b + 1) * SEQ)
        cols = slice(h * V_DIM, (h + 1) * V_DIM)
        v = v_ref[pos, cols]
        outs = []
        for s in range(2):
            sc = sc_s[u % 2, s]
            e = jnp.exp2(sc - jnp.max(sc, axis=-1, keepdims=True))
            r = 1.0 / jnp.sum(e, axis=-1, keepdims=True)
            outs.append(jnp.dot(e.astype(BF), v, preferred_element_type=F32) * r)
        o = outs[0] - lam * outs[1]
        o_ref[pos, cols] = _head_norm(o, sg, lam_init).astype(BF)

    scores(0)
    for u in range(len(units)):
        if u + 1 < len(units):
            scores(u + 1)
        finish(u)


def _attn_ctx(layer, q, kt, v, lambda_qk, subln_g, lam_init):
    rows = CTX_SEQ_PER_STEP * SEQ
    per_tile = TM // rows
    return pl.pallas_call(
        functools.partial(_attn_ctx_kernel, layer=layer, lam_init=lam_init),
        grid=(N_CTX // rows,),
        in_specs=[
            pl.BlockSpec((rows, ATT_WIDTH), lambda b: (b, 0)),
            pl.BlockSpec((None, ATT_WIDTH, rows), lambda b: (b // per_tile, 0, b % per_tile)),
            pl.BlockSpec((rows, ATT_WIDTH), lambda b: (b, 0)),
            WHOLE, WHOLE,
        ],
        out_specs=pl.BlockSpec((rows, ATT_WIDTH), lambda b: (b, 0)),
        out_shape=jax.ShapeDtypeStruct((N_TOK, ATT_WIDTH), BF),
        scratch_shapes=[pltpu.VMEM((2, 2, SEQ, SEQ), F32)],
        compiler_params=_params(("arbitrary",)),
        name=f"attn_ctx_l{layer}",
    )(q, kt, v, lambda_qk, subln_g)


TQ = 256


def _attn_lat_kernel(q_ref, kt_ref, v_ref, ckt_ref, cv_ref, lq_ref, sg_ref, att_in_ref, o_ref, sc_s, *,
                     layer, lam_init):
    del att_in_ref
    lam = _lambda(lq_ref, layer, lam_init)
    sg = sg_ref[layer:layer + 1, :]

    def scores(h, s):
        d = slice(h * V_DIM + s * QK_DIM, h * V_DIM + (s + 1) * QK_DIM)
        q = q_ref[:, d]
        sc_s[h % 2, s, :, :PAST_LEN] = jnp.dot(q, ckt_ref[d, :].astype(BF), preferred_element_type=F32)
        sc_s[h % 2, s, :, PAST_LEN:] = jnp.dot(q, kt_ref[d, :], preferred_element_type=F32)

    def softmax(h, s):
        sc = sc_s[h % 2, s]
        e = jnp.exp2(sc - jnp.max(sc, axis=-1, keepdims=True))
        return e, 1.0 / jnp.sum(e, axis=-1, keepdims=True)

    def finish(h, p1, p2):
        cols = slice(h * V_DIM, (h + 1) * V_DIM)
        e = jnp.concatenate([p1[0].astype(BF), p2[0].astype(BF)], axis=0)
        vc = cv_ref[pl.ds(h, PAST_LEN, stride=N_HEADS), :].astype(BF)
        pv = jnp.dot(e[:, :PAST_LEN], vc, preferred_element_type=F32)
        pv = pv + jnp.dot(e[:, PAST_LEN:], v_ref[:, cols], preferred_element_type=F32)
        o = pv[:TQ] * p1[1] - pv[TQ:] * (lam * p2[1])
        o_ref[:, cols] = _head_norm(o, sg, lam_init).astype(BF)

    scores(0, 0)
    scores(0, 1)
    for h in range(N_HEADS):
        more = h + 1 < N_HEADS
        if more:
            scores(h + 1, 0)
        p1 = softmax(h, 0)
        if more:
            scores(h + 1, 1)
        finish(h, p1, softmax(h, 1))


def _attn_lat(layer, q, kt, v, cache_kt, cache_v, lambda_qk, subln_g, att, lam_init):
    nqb = DEC_SEQ // TQ
    q0 = N_CTX // TQ
    return pl.pallas_call(
        functools.partial(_attn_lat_kernel, layer=layer, lam_init=lam_init),
        grid=(DEC_BATCH, nqb),
        in_specs=[
            pl.BlockSpec((TQ, ATT_WIDTH), lambda b, t: (q0 + b * nqb + t, 0)),
            pl.BlockSpec((None, ATT_WIDTH, DEC_SEQ), lambda b, t: (CTX_TILES + b, 0, 0)),
            pl.BlockSpec((DEC_SEQ, ATT_WIDTH), lambda b, t: (CTX_TILES + b, 0)),
            pl.BlockSpec((None, None, ATT_WIDTH, PAST_LEN), lambda b, t: (b, layer, 0, 0)),
            pl.BlockSpec((None, None, PAST_LEN * N_HEADS, V_DIM), lambda b, t: (b, layer, 0, 0)),
            WHOLE, WHOLE,
            pl.BlockSpec(memory_space=pl.ANY),
        ],
        out_specs=pl.BlockSpec((TQ, ATT_WIDTH), lambda b, t: (q0 + b * nqb + t, 0)),
        out_shape=jax.ShapeDtypeStruct((N_TOK, ATT_WIDTH), BF),
        scratch_shapes=[pltpu.VMEM((2, 2, TQ, PAST_LEN + DEC_SEQ), F32)],
        input_output_aliases={7: 0},
        compiler_params=_params(("arbitrary", "arbitrary")),
        name=f"attn_lat_l{layer}",
    )(q, kt, v, cache_kt, cache_v, lambda_qk, subln_g, att)


def _out_kernel(att_ref, conv_ref, w_ref, xa_ref, xb_ref, mod_ref, gf_ref, *rest, layer, route):
    if route:
        wr_ref, xo_ref, h2_ref, lp_ref, rg_ref, n_ref, wb_s = rest
    else:
        xo_ref, h2_ref, wb_s = rest
    i = pl.program_id(0)

    @pl.when(i == 0)
    def _():
        wb_s[...] = w_ref[...].astype(BF)

    mo = jnp.dot(att_ref[...], wb_s[:ATT_WIDTH, :], preferred_element_type=F32)
    mo = mo + jnp.dot(conv_ref[...], wb_s[ATT_WIDTH:, :], preferred_element_type=F32)
    xn = _stream_rows(xa_ref, xb_ref, i) + _mod(mod_ref, layer, i, 2) * mo
    xo_ref[...] = xn
    h2 = (_rms(xn) * gf_ref[layer:layer + 1, :]) * (1.0 + _mod(mod_ref, layer, i, 4)) + _mod(mod_ref, layer, i, 3)
    h2b = h2.astype(BF)
    h2_ref[...] = h2b
    if route:
        _route(h2b, wr_ref, lp_ref, rg_ref, n_ref)


def _out_proj(layer, att, conv, w_out, xs, mod, g_ffn, w_router_pad=None):
    pair = isinstance(xs, tuple)
    xa, xb = xs if pair else (xs, xs)
    spec_a, spec_b = _stream_specs(pair)
    row_spec = pl.BlockSpec((TM, D_MODEL), lambda i: (i, 0))
    lane_spec = pl.BlockSpec((TM, LANES), lambda i: (i, 0))
    in_specs = [
        pl.BlockSpec((TM, ATT_WIDTH), lambda i: (i, 0)),
        pl.BlockSpec((TM, CONV_WIDTH), lambda i: (i, 0)),
        pl.BlockSpec((None, D_MODEL, D_MODEL), lambda i: (layer, 0, 0)),
        spec_a, spec_b,
        WHOLE, WHOLE,
    ]
    args = [att, conv, w_out, xa, xb, mod, g_ffn]
    out_specs = [row_spec, row_spec]
    out_shape = [jax.ShapeDtypeStruct((N_TOK, D_MODEL), F32), jax.ShapeDtypeStruct((N_TOK, D_MODEL), BF)]
    route = w_router_pad is not None
    if route:
        in_specs.append(WHOLE)
        args.append(w_router_pad)
        out_specs += [lane_spec, lane_spec, pl.BlockSpec((None, 8, LANES), lambda i: (i, 0, 0))]
        out_shape += [jax.ShapeDtypeStruct((N_TOK, LANES), jnp.int32), jax.ShapeDtypeStruct((N_TOK, LANES), F32),
                      jax.ShapeDtypeStruct((N_TILES, 8, LANES), jnp.int32)]
    return pl.pallas_call(
        functools.partial(_out_kernel, layer=layer, route=route),
        grid=(N_TILES,),
        in_specs=in_specs,
        out_specs=out_specs,
        out_shape=out_shape,
        scratch_shapes=[pltpu.VMEM((D_MODEL, D_MODEL), BF)],
        compiler_params=_params(("arbitrary",)),
        name=f"out_proj_l{layer}",
    )(*args)


def _weight_pieces(layer):
    pieces = []
    for c0 in range(0, D_MODEL, STAGE_W_COLS):
        cols = slice(c0, c0 + STAGE_W_COLS)
        pieces.append((0, layer, slice(0, D_MODEL), cols, None, slice(0, D_MODEL), cols))
    for half in range(FF_HALVES):
        for part in range(2):
            src0 = part * D_FF + half * FF_HALF
            for off in range(0, FF_HALF, STAGE_W_COLS):
                n = min(STAGE_W_COLS, FF_HALF - off)
                pieces.append((1, 0, slice(0, D_MODEL), slice(src0 + off, src0 + off + n),
                               half, slice(0, D_MODEL), slice(part * FF_HALF + off, part * FF_HALF + off + n)))
    for r0 in range(0, D_FF, STAGE_W_ROWS):
        rows = slice(r0, min(r0 + STAGE_W_ROWS, D_FF))
        for c0 in range(0, D_MODEL, STAGE_W_COLS):
            cols = slice(c0, c0 + STAGE_W_COLS)
            pieces.append((2, 0, rows, cols, None, rows, cols))
    return pieces


def _load_dense_weights(layer, hbm, resident, stage_s, sem):
    pieces = _weight_pieces(layer)

    def copy(k):
        src, idx, rows, cols, _, _, _ = pieces[k]
        nr, nc = rows.stop - rows.start, cols.stop - cols.start
        return pltpu.make_async_copy(hbm[src].at[idx, rows, cols], stage_s.at[k % 2, :nr, :nc], sem.at[k % 2])

    copy(0).start()
    for k, (src, _, rows, cols, didx, drows, dcols) in enumerate(pieces):
        if k + 1 < len(pieces):
            copy(k + 1).start()
        copy(k).wait()
        nr, nc = rows.stop - rows.start, cols.stop - cols.start
        piece = stage_s[k % 2, :nr, :nc].astype(BF)
        if didx is None:
            resident[src][drows, dcols] = piece
        else:
            resident[src][didx, drows, dcols] = piece


def _dense_layer_kernel(att_ref, conv_ref, xa_ref, xb_ref, mod_ref, gf_ref, wo_hbm, wgu_hbm, wd_hbm,
                        o_ref, wo_b, wgu_b, wd_b, stage_s, sem, *, layer):
    i = pl.program_id(0)
    tile = i // (TM // FF_TM)

    @pl.when(i == 0)
    def _():
        _load_dense_weights(layer, (wo_hbm, wgu_hbm, wd_hbm), (wo_b, wgu_b, wd_b), stage_s, sem)

    mo = jnp.dot(att_ref[...], wo_b[:ATT_WIDTH, :], preferred_element_type=F32)
    mo = mo + jnp.dot(conv_ref[...], wo_b[ATT_WIDTH:, :], preferred_element_type=F32)
    x = jnp.where(tile >= CTX_TILES, xb_ref[...], xa_ref[...])
    xn = x + _mod(mod_ref, layer, tile, 2) * mo
    h = (_rms(xn) * gf_ref[layer:layer + 1, :]) * (1.0 + _mod(mod_ref, layer, tile, 4)) + _mod(mod_ref, layer, tile, 3)
    h = h.astype(BF)
    y = None
    for half in range(FF_HALVES):
        gu = jnp.dot(h, wgu_b[half], preferred_element_type=F32)
        act = (_silu(gu[:, :FF_HALF]) * gu[:, FF_HALF:]).astype(BF)
        part = jnp.dot(act, wd_b[half * FF_HALF:(half + 1) * FF_HALF, :], preferred_element_type=F32)
        y = part if y is None else y + part
    o_ref[...] = xn + _mod(mod_ref, layer, tile, 5) * y


def _dense_layer(layer, att, conv, xs, mod, g_ffn, w_out, w_gu, w_down):
    xa, xb = xs
    n_ctx = N_CTX // FF_TM
    row = lambda width: pl.BlockSpec((FF_TM, width), lambda i: (i, 0))
    hbm = pl.BlockSpec(memory_space=pl.ANY)
    return pl.pallas_call(
        functools.partial(_dense_layer_kernel, layer=layer),
        grid=(N_TOK // FF_TM,),
        in_specs=[
            row(ATT_WIDTH), row(CONV_WIDTH),
            pl.BlockSpec((FF_TM, D_MODEL), lambda i: (jnp.minimum(i, n_ctx - 1), 0)),
            pl.BlockSpec((FF_TM, D_MODEL), lambda i: (jnp.maximum(i - n_ctx, 0), 0)),
            WHOLE, WHOLE, hbm, hbm, hbm,
        ],
        out_specs=row(D_MODEL),
        out_shape=jax.ShapeDtypeStruct((N_TOK, D_MODEL), F32),
        scratch_shapes=[
            pltpu.VMEM((D_MODEL, D_MODEL), BF),
            pltpu.VMEM((FF_HALVES, D_MODEL, 2 * FF_HALF), BF),
            pltpu.VMEM((D_FF, D_MODEL), BF),
            pltpu.VMEM((2, STAGE_W_ROWS, STAGE_W_COLS), F32),
            pltpu.SemaphoreType.DMA((2,)),
        ],
        compiler_params=_params(("arbitrary",)),
        name="dense_layer",
    )(att, conv, xa, xb, mod, g_ffn, w_out, w_gu, w_down)


def _route(h, wr_ref, lp_ref, rg_ref, n_ref):
    logits = jnp.dot(h, wr_ref[...].astype(BF), preferred_element_type=F32)
    lane = lax.broadcasted_iota(jnp.int32, logits.shape, 1)
    lg = jnp.where(lane < N_EXPERTS, logits, -jnp.inf)
    m1 = jnp.max(lg, axis=-1, keepdims=True)
    i1 = jnp.min(jnp.where(lg == m1, lane, LANES), axis=-1, keepdims=True)
    lg2 = jnp.where(lane == i1, -jnp.inf, lg)
    m2 = jnp.max(lg2, axis=-1, keepdims=True)
    i2 = jnp.min(jnp.where(lg2 == m2, lane, LANES), axis=-1, keepdims=True)
    e2 = jnp.exp(m2 - m1)
    w1 = 1.0 / (1.0 + e2)
    w2 = e2 / (1.0 + e2)

    sel1 = lane == i1
    sel2 = lane == i2
    onehot = jnp.logical_or(sel1, sel2)
    rows = lax.broadcasted_iota(jnp.int32, (TM, TM), 0)
    colsi = lax.broadcasted_iota(jnp.int32, (TM, TM), 1)
    earlier = jnp.logical_and(colsi < rows, (colsi // ST) == (rows // ST))
    before = jnp.dot(earlier.astype(BF), onehot.astype(BF), preferred_element_type=F32)
    onehot_f = onehot.astype(F32)
    counts = [jnp.sum(onehot_f[s * ST:(s + 1) * ST], axis=0, keepdims=True) for s in range(SUB_PER_TILE)]
    counts = jnp.concatenate(counts + [jnp.zeros((8 - SUB_PER_TILE, LANES), F32)], axis=0).astype(jnp.int32)
    seg_len = ((counts + (SEG_ALIGN - 1)) // SEG_ALIGN) * SEG_ALIGN
    n_ref[...] = seg_len
    la = lax.broadcasted_iota(jnp.int32, (LANES, LANES), 0)
    lb = lax.broadcasted_iota(jnp.int32, (LANES, LANES), 1)
    seg_start = jnp.dot(seg_len.astype(F32).astype(BF), (la < lb).astype(BF), preferred_element_type=F32)
    start = jnp.concatenate(
        [jnp.broadcast_to(seg_start[s:s + 1], (ST, LANES)) for s in range(SUB_PER_TILE)], axis=0)
    where = before + start
    lp1 = jnp.sum(jnp.where(sel1, where, 0.0), axis=-1, keepdims=True).astype(jnp.int32)
    lp2 = jnp.sum(jnp.where(sel2, where, 0.0), axis=-1, keepdims=True).astype(jnp.int32)
    lp_ref[...] = jnp.where(lane == 0, lp1, jnp.where(lane == 1, lp2, 0))
    rg_ref[...] = jnp.where(lane == 0, w1, jnp.where(lane == 1, w2, 0.0))


def _chunk_copies(s, cnt_ref, cdst_ref, stage, rows_hbm, sem, *, to_hbm, wait):
    def copy(v, h):
        return pltpu.make_async_copy(v, h, sem) if to_hbm else pltpu.make_async_copy(h, v, sem)

    if wait:
        for z in WAIT_PIECES:
            @pl.when((cnt_ref[s] & z) != 0)
            def _():
                copy(stage.at[pl.ds(0, z * SEG_ALIGN)], rows_hbm.at[pl.ds(0, z * SEG_ALIGN)]).wait()
        return

    def one(c, carry):
        v = stage.at[pl.ds(pl.multiple_of(c * SEG_ALIGN, SEG_ALIGN), SEG_ALIGN)]
        h = rows_hbm.at[pl.ds(pl.multiple_of(cdst_ref[s * STAGE_CHUNKS + c], SEG_ALIGN), SEG_ALIGN)]
        copy(v, h).start()
        return carry

    lax.fori_loop(0, cnt_ref[s], one, 0)


def _dispatch_kernel(cnt_ref, cdst_ref, h_ref, lp_ref, xs_ref, stage_s, sem):
    copies = functools.partial(_chunk_copies, cnt_ref=cnt_ref, cdst_ref=cdst_ref, rows_hbm=xs_ref, to_hbm=True)
    for k in range(SUB_PER_TILE):
        s = pl.program_id(0) * SUB_PER_TILE + k
        slot = k % 2
        rows = slice(k * ST, (k + 1) * ST)

        @pl.when(s >= 2)
        def _():
            copies(s - 2, stage=stage_s.at[slot], sem=sem.at[slot], wait=True)

        lpt = lp_ref[rows, :].T
        r = lax.broadcasted_iota(jnp.int32, (STAGE_ROWS, ST), 0)
        pick = jnp.logical_or(r == lpt[0:1, :], r == lpt[1:2, :]).astype(BF)
        stage_s[slot] = jnp.dot(pick, h_ref[rows, :], preferred_element_type=F32).astype(BF)
        copies(s, stage=stage_s.at[slot], sem=sem.at[slot], wait=False)

    @pl.when(pl.program_id(0) == N_TILES - 1)
    def _():
        copies(N_SUB - 2, stage=stage_s.at[0], sem=sem.at[0], wait=True)
        copies(N_SUB - 1, stage=stage_s.at[1], sem=sem.at[1], wait=True)


def _dispatch(cnt, cdst, h2, lp):
    assert SUB_PER_TILE % 2 == 0
    grid_spec = pltpu.PrefetchScalarGridSpec(
        num_scalar_prefetch=2,
        grid=(N_TILES,),
        in_specs=[
            pl.BlockSpec((TM, D_MODEL), lambda i, *_: (i, 0)),
            pl.BlockSpec((TM, LANES), lambda i, *_: (i, 0)),
        ],
        out_specs=pl.BlockSpec(memory_space=pl.ANY),
        scratch_shapes=[pltpu.VMEM((2, STAGE_ROWS, D_MODEL), BF), pltpu.SemaphoreType.DMA((2,))],
    )
    return pl.pallas_call(
        _dispatch_kernel,
        grid_spec=grid_spec,
        out_shape=jax.ShapeDtypeStruct((R_PAD, D_MODEL), BF),
        compiler_params=_params(("arbitrary",)),
        name="moe_dispatch",
    )(cnt, cdst, h2, lp)


def _expert_weights(te_ref, nt_ref, nxt_ref, w_hbm, wf_s, wb_s, sem):
    r = pl.program_id(0)

    def fetch(e):
        return pltpu.make_async_copy(w_hbm.at[0, e], wf_s, sem)

    @pl.when(r == 0)
    def _():
        fetch(te_ref[0]).start()

    first = jnp.logical_or(r == 0, te_ref[r] != te_ref[jnp.maximum(r - 1, 0)])

    @pl.when(jnp.logical_and(r < nt_ref[0], first))
    def _():
        fetch(te_ref[r]).wait()
        wb_s[...] = wf_s[...].astype(BF)

        @pl.when(nxt_ref[r] >= 0)
        def _():
            fetch(nxt_ref[r]).start()


def _moe_ffn_kernel(te_ref, nt_ref, nxt_ref, used_ref, x_ref, wgu_hbm, wd_hbm, o_ref,
                    wgu_f, wgu_b, wd_f, wd_b, sem):
    r = pl.program_id(0)
    _expert_weights(te_ref, nt_ref, nxt_ref, wgu_hbm, wgu_f, wgu_b, sem.at[0])
    _expert_weights(te_ref, nt_ref, nxt_ref, wd_hbm, wd_f, wd_b, sem.at[1])

    def ffn(rows):
        gu = jnp.dot(x_ref[rows, :], wgu_b[...], preferred_element_type=F32)
        act = (_silu(gu[:, :D_FF_EXPERT]) * gu[:, D_FF_EXPERT:]).astype(BF)
        o_ref[rows, :] = jnp.dot(act, wd_b[...], preferred_element_type=F32).astype(BF)

    @pl.when(jnp.logical_and(r < nt_ref[0], used_ref[r] > TG // 2))
    def _():
        ffn(slice(0, TG))

    @pl.when(jnp.logical_and(r < nt_ref[0], used_ref[r] <= TG // 2))
    def _():
        ffn(slice(0, TG // 2))


def _moe_ffn(te, nt, nxt, used, rows, w_gu, w_down):
    tile_map = lambda r, te, nt, nxt, used: (jnp.minimum(r, nt[0] - 1), 0)
    grid_spec = pltpu.PrefetchScalarGridSpec(
        num_scalar_prefetch=4,
        grid=(NT_G,),
        in_specs=[pl.BlockSpec((TG, D_MODEL), tile_map),
                  pl.BlockSpec(memory_space=pl.ANY), pl.BlockSpec(memory_space=pl.ANY)],
        out_specs=pl.BlockSpec((TG, D_MODEL), tile_map),
        scratch_shapes=[
            pltpu.VMEM((D_MODEL, 2 * D_FF_EXPERT), F32), pltpu.VMEM((D_MODEL, 2 * D_FF_EXPERT), BF),
            pltpu.VMEM((D_FF_EXPERT, D_MODEL), F32), pltpu.VMEM((D_FF_EXPERT, D_MODEL), BF),
            pltpu.SemaphoreType.DMA((2,)),
        ],
    )
    return pl.pallas_call(
        _moe_ffn_kernel,
        grid_spec=grid_spec,
        out_shape=jax.ShapeDtypeStruct((R_PAD, D_MODEL), BF),
        compiler_params=_params(("arbitrary",)),
        name="moe_ffn",
    )(te, nt, nxt, used, rows, w_gu, w_down)


def _combine_kernel(cnt_ref, cdst_ref, ys_ref, lp_ref, rg_ref, x_ref, mod_ref, fg_ref,
                    oa_ref, ob_ref, stage_s, sem, *, layer):
    i = pl.program_id(0)
    copies = functools.partial(_chunk_copies, cnt_ref=cnt_ref, cdst_ref=cdst_ref, rows_hbm=ys_ref, to_hbm=False)

    @pl.when(i == 0)
    def _():
        stage_s[...] = jnp.zeros_like(stage_s)
        for s in range(COMBINE_AHEAD):
            copies(s, stage=stage_s.at[s], sem=sem.at[s], wait=False)

    for k in range(SUB_PER_TILE):
        s = i * SUB_PER_TILE + k
        slot = k % COMBINE_SLOTS
        ahead = (k + COMBINE_AHEAD) % COMBINE_SLOTS
        rows = slice(k * ST, (k + 1) * ST)

        @pl.when(s + COMBINE_AHEAD < N_SUB)
        def _():
            copies(s + COMBINE_AHEAD, stage=stage_s.at[ahead], sem=sem.at[ahead], wait=False)

        copies(s, stage=stage_s.at[slot], sem=sem.at[slot], wait=True)

        lp = lp_ref[rows, :]
        r = lax.broadcasted_iota(jnp.int32, (ST, STAGE_ROWS), 1)
        staged = stage_s[slot]
        a = jnp.dot((r == lp[:, 0:1]).astype(BF), staged, preferred_element_type=F32)
        b = jnp.dot((r == lp[:, 1:2]).astype(BF), staged, preferred_element_type=F32)
        rg = rg_ref[rows, :]
        y = rg[:, 0:1] * a + rg[:, 1:2] * b
        xn = x_ref[rows, :] + _mod(mod_ref, layer, i, 5) * y
        out = _rms(xn) * fg_ref[...]

        @pl.when(i < CTX_TILES)
        def _():
            oa_ref[rows, :] = out

        @pl.when(i >= CTX_TILES)
        def _():
            ob_ref[rows, :] = out


def _combine(layer, cnt, cdst, ys, lp, rg, x, mod, final_g):
    assert SUB_PER_TILE % COMBINE_SLOTS == 0
    grid_spec = pltpu.PrefetchScalarGridSpec(
        num_scalar_prefetch=2,
        grid=(N_TILES,),
        in_specs=[
            pl.BlockSpec(memory_space=pl.ANY),
            pl.BlockSpec((TM, LANES), lambda i, *_: (i, 0)),
            pl.BlockSpec((TM, LANES), lambda i, *_: (i, 0)),
            pl.BlockSpec((TM, D_MODEL), lambda i, *_: (i, 0)),
            WHOLE, WHOLE,
        ],
        out_specs=[
            pl.BlockSpec((TM, D_MODEL), lambda i, *_: (jnp.minimum(i, CTX_TILES - 1), 0)),
            pl.BlockSpec((TM, D_MODEL), lambda i, *_: (jnp.maximum(i - CTX_TILES, 0), 0)),
        ],
        scratch_shapes=[pltpu.VMEM((COMBINE_SLOTS, STAGE_ROWS, D_MODEL), BF),
                        pltpu.SemaphoreType.DMA((COMBINE_SLOTS,))],
    )
    return pl.pallas_call(
        functools.partial(_combine_kernel, layer=layer),
        grid_spec=grid_spec,
        out_shape=[
            jax.ShapeDtypeStruct((N_CTX, D_MODEL), F32),
            jax.ShapeDtypeStruct((N_LAT, D_MODEL), F32),
        ],
        compiler_params=_params(("arbitrary",)),
        name="moe_combine",
    )(cnt, cdst, ys, lp, rg, x, mod, final_g)


def _group_layout(n_tiles):
    n = n_tiles[:, :SUB_PER_TILE, :N_EXPERTS].reshape(N_SUB, N_EXPERTS)
    tiles = (jnp.sum(n, axis=0) + TG - 1) // TG
    tile_end = jnp.cumsum(tiles)
    region = (tile_end - tiles) * TG
    dst = region[None, :] + jnp.cumsum(n, axis=0) - n
    seg_end = jnp.cumsum(n, axis=1)
    seg = seg_end - n
    row = jnp.arange(STAGE_CHUNKS, dtype=jnp.int32) * SEG_ALIGN
    owner = jnp.sum((row[None, :, None] >= seg_end[:, None, :]).astype(jnp.int32), axis=-1)
    own = jnp.minimum(owner, N_EXPERTS - 1)[..., None] == jnp.arange(N_EXPERTS)
    cdst = jnp.sum(jnp.where(own, (dst - seg)[:, None, :], 0), axis=-1) + row[None, :]
    cnt = seg_end[:, -1] // SEG_ALIGN
    nt = tile_end[-1]
    tile_id = jnp.minimum(jnp.arange(NT_G, dtype=jnp.int32), nt - 1)
    te = jnp.sum((tile_id[:, None] >= tile_end[None, :]).astype(jnp.int32), axis=-1)
    after = jnp.sum(jnp.where(te[:, None] == jnp.arange(N_EXPERTS), tile_end[None, :], 0), axis=-1)
    nxt = jnp.where(after < nt, jnp.sum((after[:, None] >= tile_end[None, :]).astype(jnp.int32), axis=-1), -1)
    mine = te[:, None] == jnp.arange(N_EXPERTS)
    region_end = jnp.sum(jnp.where(mine, (region + jnp.sum(n, axis=0))[None, :], 0), axis=-1)
    used = jnp.clip(region_end - tile_id * TG, 0, TG)
    i32 = lambda a: a.astype(jnp.int32)
    return (i32(cnt), i32(cdst.reshape(N_SUB * STAGE_CHUNKS)), i32(te), i32(nt.reshape(1)), i32(nxt), i32(used))


def _rope_tables():
    p = np.arange(DEC_SEQ)
    row = (p // GRID_W).astype(np.float32)
    col = (p % GRID_W).astype(np.float32)
    half = QK_DIM // 4
    freqs = (ROPE_BASE ** (-np.arange(half, dtype=np.float32) / half)).astype(np.float32)
    lane = np.arange(V_DIM)
    f = freqs[lane & (half - 1)]
    use_col = (lane & (2 * half)) != 0
    ang = (np.where(use_col[None, :], col[:, None], row[:, None]) * f[None, :]).astype(np.float32)
    upper = (lane & half) != 0
    sin = np.sin(ang)
    return jnp.asarray(np.cos(ang), F32), jnp.asarray(np.where(upper[None, :], sin, -sin), F32)


def kernel(x_prompt, x_sample, cache_k, cache_v, c, c_ctx, w_ada, b_ada, norm_mix_g, norm_ffn_g,
           w_in, lambda_qk, subln_g, conv_w, w_out, w_gu_dense, w_down_dense, w_router,
           w_gu_moe, w_down_moe, final_g):
    assert DEPTH == 2
    xs = (x_prompt.reshape(N_CTX, D_MODEL), x_sample.reshape(N_LAT, D_MODEL))
    cond = jnp.concatenate([c_ctx[None, :], c, jnp.zeros((COND_ROWS - 1 - DEC_BATCH, D_MODEL), F32)], axis=0)
    mod = _ada(cond, w_ada, b_ada)
    cos_t, sin_t = _rope_tables()
    cache_kt = jnp.transpose(cache_k, (0, 1, 3, 4, 5, 2)).reshape(DEC_BATCH, DEPTH, ATT_WIDTH, PAST_LEN)
    cache_v4 = cache_v.reshape(DEC_BATCH, DEPTH, PAST_LEN * N_HEADS, V_DIM)

    new_kv = None
    for layer in range(DEPTH):
        lam_init = 0.8 - 0.6 * math.exp(-0.3 * layer)
        q, v, kt, conv, nk, nv = _in_proj(layer, xs, mod, norm_mix_g, w_in, cos_t, sin_t, conv_w, new_kv)
        new_kv = (nk, nv)
        att = _attn_ctx(layer, q, kt, v, lambda_qk, subln_g, lam_init)
        att = _attn_lat(layer, q, kt, v, cache_kt, cache_v4, lambda_qk, subln_g, att, lam_init)
        if layer == 0:
            xs = _dense_layer(layer, att, conv, xs, mod, norm_ffn_g, w_out, w_gu_dense, w_down_dense)
        else:
            wr = jnp.pad(w_router[0], ((0, 0), (0, LANES - N_EXPERTS)))
            x1, h2, lp, rg, n_tiles = _out_proj(layer, att, conv, w_out, xs, mod, norm_ffn_g, wr)
            cnt, cdst, te, nt, nxt, used = _group_layout(n_tiles)
            xsort = _dispatch(cnt, cdst, h2, lp)
            ys = _moe_ffn(te, nt, nxt, used, xsort, w_gu_moe, w_down_moe)
            y_ctx, y_lat = _combine(layer, cnt, cdst, ys, lp, rg, x1, mod, final_g.reshape(1, D_MODEL))
    nk, nv = new_kv
    new_k = jnp.transpose(nk.reshape(BATCH, DEPTH, N_HEADS, 2, QK_DIM, SEQ), (0, 1, 5, 2, 3, 4))
    new_v = nv.reshape(BATCH, DEPTH, SEQ, N_HEADS, V_DIM)
    return (y_ctx.reshape(BATCH, SEQ, D_MODEL), y_lat.reshape(DEC_BATCH, DEC_SEQ, D_MODEL), new_k, new_v)
```

```python
import functools
import math

import numpy as np
import jax
import jax.numpy as jnp
from jax import lax
from jax.experimental import pallas as pl
from jax.experimental.pallas import tpu as pltpu

D_MODEL = 1024
BATCH = 16
SEQ = 256
DEPTH = 2
DEC_BATCH = 4
DEC_SEQ = 1024
PAST_LEN = 512
GRID_W = 64
ATT_WIDTH = 512
CONV_WIDTH = 512
N_HEADS = 4
V_DIM = 128
QK_DIM = 64
ROPE_BASE = 10000.0
D_FF = 2816
N_EXPERTS = 8
D_FF_EXPERT = 1408
N_MOD = 6
NORM_EPS = 1e-6
Q_SCALE = QK_DIM ** -0.5 * math.log2(math.e)
IN_COLS = 3 * ATT_WIDTH + 3 * CONV_WIDTH

N_CTX = BATCH * SEQ
N_LAT = DEC_BATCH * DEC_SEQ
N_TOK = N_CTX + N_LAT
TM = 1024
N_TILES = N_TOK // TM
CTX_TILES = N_CTX // TM
SEQ_PER_TILE = TM // SEQ
CTX_SEQ_PER_STEP = 1
COND_ROWS = 8
TN_IN = 1024
N_IN_TILES = IN_COLS // TN_IN
ROW_CHUNK = 512
FF_TM = 512
FF_HALVES = 2
FF_HALF = D_FF // FF_HALVES
STAGE_W_ROWS = 1024
STAGE_W_COLS = 512
TN_ADA = 1536
TG = 512
ST = 256
SUB_PER_TILE = TM // ST
N_SUB = N_TOK // ST
CTX_SUB = N_CTX // ST
SEG_ALIGN = 8
STAGE_ROWS = 640
STAGE_CHUNKS = STAGE_ROWS // SEG_ALIGN
WAIT_PIECES = (64, 32, 16, 8, 4, 2, 1)
COMBINE_SLOTS = 4
COMBINE_AHEAD = 2
NT_G = -(-(2 * N_TOK + N_SUB * N_EXPERTS * (SEG_ALIGN - 1) + N_EXPERTS * (TG - SEG_ALIGN)) // TG)
R_PAD = NT_G * TG
LANES = 128
VMEM_LIMIT = 60 * 1024 * 1024

BF = jnp.bfloat16
F32 = jnp.float32


def _params(sem, vmem=VMEM_LIMIT):
    return pltpu.CompilerParams(dimension_semantics=sem, vmem_limit_bytes=vmem)


def _mod_row(i):
    return jnp.where(i < CTX_TILES, 0, i - (CTX_TILES - 1))


WHOLE = pl.BlockSpec(memory_space=pltpu.VMEM)


def _mod(mod_ref, layer, i, c):
    return mod_ref[layer, pl.ds(_mod_row(i), 1), c * D_MODEL:(c + 1) * D_MODEL]


def _stream_specs(pair, width=D_MODEL):
    a = pl.BlockSpec((TM, width), lambda i, *_: (jnp.minimum(i, CTX_TILES - 1), 0))
    if pair:
        b = pl.BlockSpec((TM, width), lambda i, *_: (jnp.maximum(i - CTX_TILES, 0), 0))
    else:
        b = pl.BlockSpec((TM, width), lambda i, *_: (jnp.maximum(i, CTX_TILES), 0))
    return a, b


def _stream_rows(xa_ref, xb_ref, i):
    return jnp.where(i >= CTX_TILES, xb_ref[...], xa_ref[...])


def _silu(x):
    return x / (1.0 + jnp.exp(-x))


def _rms(x):
    return x * lax.rsqrt(jnp.mean(x * x, axis=-1, keepdims=True) + NORM_EPS)


def _ada_kernel(c_ref, w_ref, b_ref, o_ref):
    s = _silu(c_ref[...]).astype(BF)
    bias = b_ref[pl.ds(pl.program_id(0), 1), :]
    o_ref[...] = jnp.dot(s, w_ref[...].astype(BF), preferred_element_type=F32) + bias


def _ada(cond, w_ada, b_ada):
    n = N_MOD * D_MODEL
    return pl.pallas_call(
        _ada_kernel,
        grid=(DEPTH, n // TN_ADA),
        in_specs=[
            pl.BlockSpec((COND_ROWS, D_MODEL), lambda l, j: (0, 0)),
            pl.BlockSpec((None, D_MODEL, TN_ADA), lambda l, j: (l, 0, j)),
            pl.BlockSpec((DEPTH, TN_ADA), lambda l, j: (0, j)),
        ],
        out_specs=pl.BlockSpec((None, COND_ROWS, TN_ADA), lambda l, j: (l, 0, j)),
        out_shape=jax.ShapeDtypeStruct((DEPTH, COND_ROWS, n), F32),
        compiler_params=_params(("arbitrary", "arbitrary")),
        name="ada_mod",
    )(cond, w_ada, b_ada)


def _in_kernel(*refs, layer, aliased):
    xa_ref, xb_ref, mod_ref, g_ref, w_ref, cos_ref, sin_ref, cw_ref = refs[:8]
    refs = refs[10:] if aliased else refs[8:]
    q_ref, v_ref, kt_ref, conv_ref, nk_ref, nv_ref, h_s, wb_s, gb_s, gc_s = refs
    i = pl.program_id(0)
    j = pl.program_id(1)
    lat = i >= CTX_TILES
    ctx = jnp.logical_not(lat)

    @pl.when(i == 0)
    def _():
        wb_s[j] = w_ref[...].astype(BF)

    chunks = [slice(c * ROW_CHUNK, (c + 1) * ROW_CHUNK) for c in range(TM // ROW_CHUNK)]
    seqs_per_chunk = ROW_CHUNK // SEQ

    def norm(rows):
        gain = g_ref[layer:layer + 1, :] * (1.0 + _mod(mod_ref, layer, i, 1))
        x = jnp.where(lat, xb_ref[rows, :], xa_ref[rows, :])
        h_s[rows, :] = (_rms(x) * gain + _mod(mod_ref, layer, i, 0)).astype(BF)

    def proj(rows):
        acc = jnp.dot(h_s[rows, :], wb_s[j], preferred_element_type=F32)
        return acc[:, :ATT_WIDTH], acc[:, ATT_WIDTH:]

    def roped(a, rows):
        cos = jnp.concatenate([cos_ref[rows, :]] * N_HEADS, axis=1)
        sin = jnp.concatenate([sin_ref[rows, :]] * N_HEADS, axis=1)
        lane = lax.broadcasted_iota(jnp.int32, a.shape, 1)
        upper = (lane & (QK_DIM // 4)) != 0
        partner = jnp.where(upper, pltpu.roll(a, QK_DIM // 4, 1), pltpu.roll(a, ATT_WIDTH - QK_DIM // 4, 1))
        return a * cos + partner * sin

    @pl.when(jnp.logical_and(j == 0, lat))
    def _():
        for rows in chunks:
            norm(rows)
            q, k = proj(rows)
            q_ref[rows, :] = (roped(q, rows) * Q_SCALE).astype(BF)
            kt_ref[:, rows] = roped(k, rows).T.astype(BF)

    @pl.when(jnp.logical_and(j == 0, ctx))
    def _():
        for c, rows in enumerate(chunks):
            norm(rows)
            q, k = proj(rows)
            q_ref[rows, :] = (q * Q_SCALE).astype(BF)
            kt = k.T
            kt_ref[:, rows] = kt.astype(BF)
            for s in range(seqs_per_chunk):
                nk_ref[c * seqs_per_chunk + s] = kt[:, s * SEQ:(s + 1) * SEQ]

    @pl.when(jnp.logical_and(j == 1, lat))
    def _():
        for rows in chunks:
            v, gb = proj(rows)
            v_ref[rows, :] = v.astype(BF)
            gb_s[rows, :] = gb

    @pl.when(jnp.logical_and(j == 1, ctx))
    def _():
        for c, rows in enumerate(chunks):
            v, gb = proj(rows)
            v_ref[rows, :] = v.astype(BF)
            gb_s[rows, :] = gb
            for s in range(seqs_per_chunk):
                for h in range(N_HEADS):
                    nv_ref[c * seqs_per_chunk + s, pl.ds(h, SEQ, stride=N_HEADS), :] = (
                        v[s * SEQ:(s + 1) * SEQ, h * V_DIM:(h + 1) * V_DIM])

    @pl.when(j == 2)
    def _():
        for rows in chunks:
            gc, xin = proj(rows)
            gc_s[rows, :] = gc * xin
        u = gc_s[...]
        seq = jnp.where(lat, DEC_SEQ, SEQ)
        pos = lax.broadcasted_iota(jnp.int32, (TM, 1), 0) & (seq - 1)
        prev = jnp.where(pos == 0, 0.0, pltpu.roll(u, 1, 0))
        nxt = jnp.where(pos == seq - 1, 0.0, pltpu.roll(u, TM - 1, 0))
        cw = cw_ref[layer]
        conv = prev * cw[0:1] + u * cw[1:2] + nxt * cw[2:3]
        conv_ref[...] = (gb_s[...] * conv).astype(BF)


def _in_proj(layer, xs, mod, g_mix, w_in, cos_t, sin_t, conv_w, new_kv):
    pair = isinstance(xs, tuple)
    xa, xb = xs if pair else (xs, xs)
    spec_a, spec_b = _stream_specs(pair)
    ctx_i = lambda i: jnp.minimum(i, CTX_TILES - 1)
    in_specs = [
        spec_a, spec_b,
        WHOLE, WHOLE,
        pl.BlockSpec((None, D_MODEL, TN_IN), lambda i, j: (layer, 0, jnp.where(i == 0, j, N_IN_TILES - 1))),
        WHOLE, WHOLE, WHOLE,
    ]
    args = [xa, xb, mod, g_mix, w_in, cos_t, sin_t, conv_w]
    aliases = {}
    if new_kv is not None:
        in_specs += [pl.BlockSpec(memory_space=pl.ANY), pl.BlockSpec(memory_space=pl.ANY)]
        args += list(new_kv)
        aliases = {8: 4, 9: 5}
    row_tile = pl.BlockSpec((TM, ATT_WIDTH), lambda i, j: (i, 0))
    return pl.pallas_call(
        functools.partial(_in_kernel, layer=layer, aliased=new_kv is not None),
        grid=(N_TILES, N_IN_TILES),
        in_specs=in_specs,
        out_specs=[
            row_tile,
            row_tile,
            pl.BlockSpec((None, ATT_WIDTH, TM), lambda i, j: (i, 0, 0)),
            row_tile,
            pl.BlockSpec((SEQ_PER_TILE, None, ATT_WIDTH, SEQ), lambda i, j: (ctx_i(i), layer, 0, 0)),
            pl.BlockSpec((SEQ_PER_TILE, None, SEQ * N_HEADS, V_DIM), lambda i, j: (ctx_i(i), layer, 0, 0)),
        ],
        out_shape=[
            jax.ShapeDtypeStruct((N_TOK, ATT_WIDTH), BF),
            jax.ShapeDtypeStruct((N_TOK, ATT_WIDTH), BF),
            jax.ShapeDtypeStruct((N_TILES, ATT_WIDTH, TM), BF),
            jax.ShapeDtypeStruct((N_TOK, CONV_WIDTH), BF),
            jax.ShapeDtypeStruct((BATCH, DEPTH, ATT_WIDTH, SEQ), F32),
            jax.ShapeDtypeStruct((BATCH, DEPTH, SEQ * N_HEADS, V_DIM), F32),
        ],
        scratch_shapes=[
            pltpu.VMEM((TM, D_MODEL), BF),
            pltpu.VMEM((N_IN_TILES, D_MODEL, TN_IN), BF),
            pltpu.VMEM((TM, CONV_WIDTH), F32),
            pltpu.VMEM((TM, CONV_WIDTH), F32),
        ],
        input_output_aliases=aliases,
        compiler_params=_params(("arbitrary", "arbitrary")),
        name=f"in_proj_l{layer}",
    )(*args)


def _lambda(lq_ref, layer, lam_init):
    lq = lq_ref[layer]
    a = jnp.exp(jnp.sum(lq[0:1] * lq[1:2], axis=-1, keepdims=True))
    b = jnp.exp(jnp.sum(lq[2:3] * lq[3:4], axis=-1, keepdims=True))
    return a - b + lam_init


def _head_norm(o, sg, lam_init):
    return _rms(o) * sg * (1.0 - lam_init)


def _attn_ctx_kernel(q_ref, kt_ref, v_ref, lq_ref, sg_ref, o_ref, sc_s, *, layer, lam_init):
    lam = _lambda(lq_ref, layer, lam_init)
    sg = sg_ref[layer:layer + 1, :]

    units = [(b, h) for b in range(CTX_SEQ_PER_STEP) for h in range(N_HEADS)]

    def scores(u):
        b, h = units[u]
        pos = slice(b * SEQ, (b + 1) * SEQ)
        for s in range(2):
            d = slice(h * V_DIM + s * QK_DIM, h * V_DIM + (s + 1) * QK_DIM)
            sc_s[u % 2, s] = jnp.dot(q_ref[pos, d], kt_ref[d, pos], preferred_element_type=F32)

    def finish(u):
        b, h = units[u]
        pos = slice(b * SEQ, (b + 1) * SEQ)
        cols = slice(h * V_DIM, (h + 1) * V_DIM)
        v = v_ref[pos, cols]
        outs = []
        for s in range(2):
            sc = sc_s[u % 2, s]
            e = jnp.exp2(sc - jnp.max(sc, axis=-1, keepdims=True))
            r = 1.0 / jnp.sum(e, axis=-1, keepdims=True)
            outs.append(jnp.dot(e.astype(BF), v, preferred_element_type=F32) * r)
        o = outs[0] - lam * outs[1]
        o_ref[pos, cols] = _head_norm(o, sg, lam_init).astype(BF)

    scores(0)
    for u in range(len(units)):
        if u + 1 < len(units):
            scores(u + 1)
        finish(u)


def _attn_ctx(layer, q, kt, v, lambda_qk, subln_g, lam_init):
    rows = CTX_SEQ_PER_STEP * SEQ
    per_tile = TM // rows
    return pl.pallas_call(
        functools.partial(_attn_ctx_kernel, layer=layer, lam_init=lam_init),
        grid=(N_CTX // rows,),
        in_specs=[
            pl.BlockSpec((rows, ATT_WIDTH), lambda b: (b, 0)),
            pl.BlockSpec((None, ATT_WIDTH, rows), lambda b: (b // per_tile, 0, b % per_tile)),
            pl.BlockSpec((rows, ATT_WIDTH), lambda b: (b, 0)),
            WHOLE, WHOLE,
        ],
        out_specs=pl.BlockSpec((rows, ATT_WIDTH), lambda b: (b, 0)),
        out_shape=jax.ShapeDtypeStruct((N_TOK, ATT_WIDTH), BF),
        scratch_shapes=[pltpu.VMEM((2, 2, SEQ, SEQ), F32)],
        compiler_params=_params(("arbitrary",)),
        name=f"attn_ctx_l{layer}",
    )(q, kt, v, lambda_qk, subln_g)


TQ = 256


def _attn_lat_kernel(q_ref, kt_ref, v_ref, ckt_ref, cv_ref, lq_ref, sg_ref, att_in_ref, o_ref, sc_s, *,
                     layer, lam_init):
    del att_in_ref
    lam = _lambda(lq_ref, layer, lam_init)
    sg = sg_ref[layer:layer + 1, :]

    def scores(h, s):
        d = slice(h * V_DIM + s * QK_DIM, h * V_DIM + (s + 1) * QK_DIM)
        q = q_ref[:, d]
        sc_s[h % 2, s, :, :PAST_LEN] = jnp.dot(q, ckt_ref[d, :].astype(BF), preferred_element_type=F32)
        sc_s[h % 2, s, :, PAST_LEN:] = jnp.dot(q, kt_ref[d, :], preferred_element_type=F32)

    def softmax(h, s):
        sc = sc_s[h % 2, s]
        e = jnp.exp2(sc - jnp.max(sc, axis=-1, keepdims=True))
        return e, 1.0 / jnp.sum(e, axis=-1, keepdims=True)

    def finish(h, p1, p2):
        cols = slice(h * V_DIM, (h + 1) * V_DIM)
        e = jnp.concatenate([p1[0].astype(BF), p2[0].astype(BF)], axis=0)
        vc = cv_ref[pl.ds(h, PAST_LEN, stride=N_HEADS), :].astype(BF)
        pv = jnp.dot(e[:, :PAST_LEN], vc, preferred_element_type=F32)
        pv = pv + jnp.dot(e[:, PAST_LEN:], v_ref[:, cols], preferred_element_type=F32)
        o = pv[:TQ] * p1[1] - pv[TQ:] * (lam * p2[1])
        o_ref[:, cols] = _head_norm(o, sg, lam_init).astype(BF)

    scores(0, 0)
    scores(0, 1)
    for h in range(N_HEADS):
        more = h + 1 < N_HEADS
        if more:
            scores(h + 1, 0)
        p1 = softmax(h, 0)
        if more:
            scores(h + 1, 1)
        finish(h, p1, softmax(h, 1))


def _attn_lat(layer, q, kt, v, cache_kt, cache_v, lambda_qk, subln_g, att, lam_init):
    nqb = DEC_SEQ // TQ
    q0 = N_CTX // TQ
    return pl.pallas_call(
        functools.partial(_attn_lat_kernel, layer=layer, lam_init=lam_init),
        grid=(DEC_BATCH, nqb),
        in_specs=[
            pl.BlockSpec((TQ, ATT_WIDTH), lambda b, t: (q0 + b * nqb + t, 0)),
            pl.BlockSpec((None, ATT_WIDTH, DEC_SEQ), lambda b, t: (CTX_TILES + b, 0, 0)),
            pl.BlockSpec((DEC_SEQ, ATT_WIDTH), lambda b, t: (CTX_TILES + b, 0)),
            pl.BlockSpec((None, None, ATT_WIDTH, PAST_LEN), lambda b, t: (b, layer, 0, 0)),
            pl.BlockSpec((None, None, PAST_LEN * N_HEADS, V_DIM), lambda b, t: (b, layer, 0, 0)),
            WHOLE, WHOLE,
            pl.BlockSpec(memory_space=pl.ANY),
        ],
        out_specs=pl.BlockSpec((TQ, ATT_WIDTH), lambda b, t: (q0 + b * nqb + t, 0)),
        out_shape=jax.ShapeDtypeStruct((N_TOK, ATT_WIDTH), BF),
        scratch_shapes=[pltpu.VMEM((2, 2, TQ, PAST_LEN + DEC_SEQ), F32)],
        input_output_aliases={7: 0},
        compiler_params=_params(("arbitrary", "arbitrary")),
        name=f"attn_lat_l{layer}",
    )(q, kt, v, cache_kt, cache_v, lambda_qk, subln_g, att)


def _out_kernel(att_ref, conv_ref, w_ref, xa_ref, xb_ref, mod_ref, gf_ref, *rest, layer, route):
    if route:
        wr_ref, xo_ref, h2_ref, lp_ref, rg_ref, n_ref, wb_s = rest
    else:
        xo_ref, h2_ref, wb_s = rest
    i = pl.program_id(0)

    @pl.when(i == 0)
    def _():
        wb_s[...] = w_ref[...].astype(BF)

    mo = jnp.dot(att_ref[...], wb_s[:ATT_WIDTH, :], preferred_element_type=F32)
    mo = mo + jnp.dot(conv_ref[...], wb_s[ATT_WIDTH:, :], preferred_element_type=F32)
    xn = _stream_rows(xa_ref, xb_ref, i) + _mod(mod_ref, layer, i, 2) * mo
    xo_ref[...] = xn
    h2 = (_rms(xn) * gf_ref[layer:layer + 1, :]) * (1.0 + _mod(mod_ref, layer, i, 4)) + _mod(mod_ref, layer, i, 3)
    h2b = h2.astype(BF)
    h2_ref[...] = h2b
    if route:
        _route(h2b, wr_ref, lp_ref, rg_ref, n_ref)


def _out_proj(layer, att, conv, w_out, xs, mod, g_ffn, w_router_pad=None):
    pair = isinstance(xs, tuple)
    xa, xb = xs if pair else (xs, xs)
    spec_a, spec_b = _stream_specs(pair)
    row_spec = pl.BlockSpec((TM, D_MODEL), lambda i: (i, 0))
    lane_spec = pl.BlockSpec((TM, LANES), lambda i: (i, 0))
    in_specs = [
        pl.BlockSpec((TM, ATT_WIDTH), lambda i: (i, 0)),
        pl.BlockSpec((TM, CONV_WIDTH), lambda i: (i, 0)),
        pl.BlockSpec((None, D_MODEL, D_MODEL), lambda i: (layer, 0, 0)),
        spec_a, spec_b,
        WHOLE, WHOLE,
    ]
    args = [att, conv, w_out, xa, xb, mod, g_ffn]
    out_specs = [row_spec, row_spec]
    out_shape = [jax.ShapeDtypeStruct((N_TOK, D_MODEL), F32), jax.ShapeDtypeStruct((N_TOK, D_MODEL), BF)]
    route = w_router_pad is not None
    if route:
        in_specs.append(WHOLE)
        args.append(w_router_pad)
        out_specs += [lane_spec, lane_spec, pl.BlockSpec((None, 8, LANES), lambda i: (i, 0, 0))]
        out_shape += [jax.ShapeDtypeStruct((N_TOK, LANES), jnp.int32), jax.ShapeDtypeStruct((N_TOK, LANES), F32),
                      jax.ShapeDtypeStruct((N_TILES, 8, LANES), jnp.int32)]
    return pl.pallas_call(
        functools.partial(_out_kernel, layer=layer, route=route),
        grid=(N_TILES,),
        in_specs=in_specs,
        out_specs=out_specs,
        out_shape=out_shape,
        scratch_shapes=[pltpu.VMEM((D_MODEL, D_MODEL), BF)],
        compiler_params=_params(("arbitrary",)),
        name=f"out_proj_l{layer}",
    )(*args)


def _weight_pieces(layer):
    pieces = []
    for c0 in range(0, D_MODEL, STAGE_W_COLS):
        cols = slice(c0, c0 + STAGE_W_COLS)
        pieces.append((0, layer, slice(0, D_MODEL), cols, None, slice(0, D_MODEL), cols))
    for half in range(FF_HALVES):
        for part in range(2):
            src0 = part * D_FF + half * FF_HALF
            for off in range(0, FF_HALF, STAGE_W_COLS):
                n = min(STAGE_W_COLS, FF_HALF - off)
                pieces.append((1, 0, slice(0, D_MODEL), slice(src0 + off, src0 + off + n),
                               half, slice(0, D_MODEL), slice(part * FF_HALF + off, part * FF_HALF + off + n)))
    for r0 in range(0, D_FF, STAGE_W_ROWS):
        rows = slice(r0, min(r0 + STAGE_W_ROWS, D_FF))
        for c0 in range(0, D_MODEL, STAGE_W_COLS):
            cols = slice(c0, c0 + STAGE_W_COLS)
            pieces.append((2, 0, rows, cols, None, rows, cols))
    return pieces


def _load_dense_weights(layer, hbm, resident, stage_s, sem):
    pieces = _weight_pieces(layer)

    def copy(k):
        src, idx, rows, cols, _, _, _ = pieces[k]
        nr, nc = rows.stop - rows.start, cols.stop - cols.start
        return pltpu.make_async_copy(hbm[src].at[idx, rows, cols], stage_s.at[k % 2, :nr, :nc], sem.at[k % 2])

    copy(0).start()
    for k, (src, _, rows, cols, didx, drows, dcols) in enumerate(pieces):
        if k + 1 < len(pieces):
            copy(k + 1).start()
        copy(k).wait()
        nr, nc = rows.stop - rows.start, cols.stop - cols.start
        piece = stage_s[k % 2, :nr, :nc].astype(BF)
        if didx is None:
            resident[src][drows, dcols] = piece
        else:
            resident[src][didx, drows, dcols] = piece


def _dense_layer_kernel(att_ref, conv_ref, xa_ref, xb_ref, mod_ref, gf_ref, wo_hbm, wgu_hbm, wd_hbm,
                        o_ref, wo_b, wgu_b, wd_b, stage_s, sem, *, layer):
    i = pl.program_id(0)
    tile = i // (TM // FF_TM)

    @pl.when(i == 0)
    def _():
        _load_dense_weights(layer, (wo_hbm, wgu_hbm, wd_hbm), (wo_b, wgu_b, wd_b), stage_s, sem)

    mo = jnp.dot(att_ref[...], wo_b[:ATT_WIDTH, :], preferred_element_type=F32)
    mo = mo + jnp.dot(conv_ref[...], wo_b[ATT_WIDTH:, :], preferred_element_type=F32)
    x = jnp.where(tile >= CTX_TILES, xb_ref[...], xa_ref[...])
    xn = x + _mod(mod_ref, layer, tile, 2) * mo
    h = (_rms(xn) * gf_ref[layer:layer + 1, :]) * (1.0 + _mod(mod_ref, layer, tile, 4)) + _mod(mod_ref, layer, tile, 3)
    h = h.astype(BF)
    y = None
    for half in range(FF_HALVES):
        gu = jnp.dot(h, wgu_b[half], preferred_element_type=F32)
        act = (_silu(gu[:, :FF_HALF]) * gu[:, FF_HALF:]).astype(BF)
        part = jnp.dot(act, wd_b[half * FF_HALF:(half + 1) * FF_HALF, :], preferred_element_type=F32)
        y = part if y is None else y + part
    o_ref[...] = xn + _mod(mod_ref, layer, tile, 5) * y


def _dense_layer(layer, att, conv, xs, mod, g_ffn, w_out, w_gu, w_down):
    xa, xb = xs
    n_ctx = N_CTX // FF_TM
    row = lambda width: pl.BlockSpec((FF_TM, width), lambda i: (i, 0))
    hbm = pl.BlockSpec(memory_space=pl.ANY)
    return pl.pallas_call(
        functools.partial(_dense_layer_kernel, layer=layer),
        grid=(N_TOK // FF_TM,),
        in_specs=[
            row(ATT_WIDTH), row(CONV_WIDTH),
            pl.BlockSpec((FF_TM, D_MODEL), lambda i: (jnp.minimum(i, n_ctx - 1), 0)),
            pl.BlockSpec((FF_TM, D_MODEL), lambda i: (jnp.maximum(i - n_ctx, 0), 0)),
            WHOLE, WHOLE, hbm, hbm, hbm,
        ],
        out_specs=row(D_MODEL),
        out_shape=jax.ShapeDtypeStruct((N_TOK, D_MODEL), F32),
        scratch_shapes=[
            pltpu.VMEM((D_MODEL, D_MODEL), BF),
            pltpu.VMEM((FF_HALVES, D_MODEL, 2 * FF_HALF), BF),
            pltpu.VMEM((D_FF, D_MODEL), BF),
            pltpu.VMEM((2, STAGE_W_ROWS, STAGE_W_COLS), F32),
            pltpu.SemaphoreType.DMA((2,)),
        ],
        compiler_params=_params(("arbitrary",)),
        name="dense_layer",
    )(att, conv, xa, xb, mod, g_ffn, w_out, w_gu, w_down)


def _route(h, wr_ref, lp_ref, rg_ref, n_ref):
    logits = jnp.dot(h, wr_ref[...].astype(BF), preferred_element_type=F32)
    lane = lax.broadcasted_iota(jnp.int32, logits.shape, 1)
    lg = jnp.where(lane < N_EXPERTS, logits, -jnp.inf)
    m1 = jnp.max(lg, axis=-1, keepdims=True)
    i1 = jnp.min(jnp.where(lg == m1, lane, LANES), axis=-1, keepdims=True)
    lg2 = jnp.where(lane == i1, -jnp.inf, lg)
    m2 = jnp.max(lg2, axis=-1, keepdims=True)
    i2 = jnp.min(jnp.where(lg2 == m2, lane, LANES), axis=-1, keepdims=True)
    e2 = jnp.exp(m2 - m1)
    w1 = 1.0 / (1.0 + e2)
    w2 = e2 / (1.0 + e2)

    sel1 = lane == i1
    sel2 = lane == i2
    onehot = jnp.logical_or(sel1, sel2)
    rows = lax.broadcasted_iota(jnp.int32, (TM, TM), 0)
    colsi = lax.broadcasted_iota(jnp.int32, (TM, TM), 1)
    earlier = jnp.logical_and(colsi < rows, (colsi // ST) == (rows // ST))
    before = jnp.dot(earlier.astype(BF), onehot.astype(BF), preferred_element_type=F32)
    onehot_f = onehot.astype(F32)
    counts = [jnp.sum(onehot_f[s * ST:(s + 1) * ST], axis=0, keepdims=True) for s in range(SUB_PER_TILE)]
    counts = jnp.concatenate(counts + [jnp.zeros((8 - SUB_PER_TILE, LANES), F32)], axis=0).astype(jnp.int32)
    seg_len = ((counts + (SEG_ALIGN - 1)) // SEG_ALIGN) * SEG_ALIGN
    n_ref[...] = seg_len
    la = lax.broadcasted_iota(jnp.int32, (LANES, LANES), 0)
    lb = lax.broadcasted_iota(jnp.int32, (LANES, LANES), 1)
    seg_start = jnp.dot(seg_len.astype(F32).astype(BF), (la < lb).astype(BF), preferred_element_type=F32)
    start = jnp.concatenate(
        [jnp.broadcast_to(seg_start[s:s + 1], (ST, LANES)) for s in range(SUB_PER_TILE)], axis=0)
    where = before + start
    lp1 = jnp.sum(jnp.where(sel1, where, 0.0), axis=-1, keepdims=True).astype(jnp.int32)
    lp2 = jnp.sum(jnp.where(sel2, where, 0.0), axis=-1, keepdims=True).astype(jnp.int32)
    lp_ref[...] = jnp.where(lane == 0, lp1, jnp.where(lane == 1, lp2, 0))
    rg_ref[...] = jnp.where(lane == 0, w1, jnp.where(lane == 1, w2, 0.0))


def _chunk_copies(s, cnt_ref, cdst_ref, stage, rows_hbm, sem, *, to_hbm, wait):
    def copy(v, h):
        return pltpu.make_async_copy(v, h, sem) if to_hbm else pltpu.make_async_copy(h, v, sem)

    if wait:
        for z in WAIT_PIECES:
            @pl.when((cnt_ref[s] & z) != 0)
            def _():
                copy(stage.at[pl.ds(0, z * SEG_ALIGN)], rows_hbm.at[pl.ds(0, z * SEG_ALIGN)]).wait()
        return

    def start(c, priority):
        v = stage.at[pl.ds(pl.multiple_of(c * SEG_ALIGN, SEG_ALIGN), SEG_ALIGN)]
        h = rows_hbm.at[pl.ds(pl.multiple_of(cdst_ref[s * STAGE_CHUNKS + c], SEG_ALIGN), SEG_ALIGN)]
        copy(v, h).start(priority=priority)

    def pair(p, carry):
        start(2 * p, 0)
        start(2 * p + 1, 1)
        return carry

    n = cnt_ref[s]
    lax.fori_loop(0, n // 2, pair, 0)

    @pl.when((n & 1) != 0)
    def _():
        start(n - 1, 0)


def _dispatch_kernel(cnt_ref, cdst_ref, h_ref, lp_ref, xs_ref, stage_s, sem):
    copies = functools.partial(_chunk_copies, cnt_ref=cnt_ref, cdst_ref=cdst_ref, rows_hbm=xs_ref, to_hbm=True)
    for k in range(SUB_PER_TILE):
        s = pl.program_id(0) * SUB_PER_TILE + k
        slot = k % 2
        rows = slice(k * ST, (k + 1) * ST)

        @pl.when(s >= 2)
        def _():
            copies(s - 2, stage=stage_s.at[slot], sem=sem.at[slot], wait=True)

        lpt = lp_ref[rows, :].T
        r = lax.broadcasted_iota(jnp.int32, (STAGE_ROWS, ST), 0)
        pick = jnp.logical_or(r == lpt[0:1, :], r == lpt[1:2, :]).astype(BF)
        stage_s[slot] = jnp.dot(pick, h_ref[rows, :], preferred_element_type=F32).astype(BF)
        copies(s, stage=stage_s.at[slot], sem=sem.at[slot], wait=False)

    @pl.when(pl.program_id(0) == N_TILES - 1)
    def _():
        copies(N_SUB - 2, stage=stage_s.at[0], sem=sem.at[0], wait=True)
        copies(N_SUB - 1, stage=stage_s.at[1], sem=sem.at[1], wait=True)


def _dispatch(cnt, cdst, h2, lp):
    assert SUB_PER_TILE % 2 == 0
    grid_spec = pltpu.PrefetchScalarGridSpec(
        num_scalar_prefetch=2,
        grid=(N_TILES,),
        in_specs=[
            pl.BlockSpec((TM, D_MODEL), lambda i, *_: (i, 0)),
            pl.BlockSpec((TM, LANES), lambda i, *_: (i, 0)),
        ],
        out_specs=pl.BlockSpec(memory_space=pl.ANY),
        scratch_shapes=[pltpu.VMEM((2, STAGE_ROWS, D_MODEL), BF), pltpu.SemaphoreType.DMA((2,))],
    )
    return pl.pallas_call(
        _dispatch_kernel,
        grid_spec=grid_spec,
        out_shape=jax.ShapeDtypeStruct((R_PAD, D_MODEL), BF),
        compiler_params=_params(("arbitrary",)),
        name="moe_dispatch",
    )(cnt, cdst, h2, lp)


def _expert_weights(te_ref, nt_ref, nxt_ref, w_hbm, wf_s, wb_s, sem):
    r = pl.program_id(0)

    def fetch(e):
        return pltpu.make_async_copy(w_hbm.at[0, e], wf_s, sem)

    @pl.when(r == 0)
    def _():
        fetch(te_ref[0]).start()

    first = jnp.logical_or(r == 0, te_ref[r] != te_ref[jnp.maximum(r - 1, 0)])

    @pl.when(jnp.logical_and(r < nt_ref[0], first))
    def _():
        fetch(te_ref[r]).wait()
        wb_s[...] = wf_s[...].astype(BF)

        @pl.when(nxt_ref[r] >= 0)
        def _():
            fetch(nxt_ref[r]).start()


def _moe_ffn_kernel(te_ref, nt_ref, nxt_ref, used_ref, x_ref, wgu_hbm, wd_hbm, o_ref,
                    wgu_f, wgu_b, wd_f, wd_b, sem):
    r = pl.program_id(0)
    _expert_weights(te_ref, nt_ref, nxt_ref, wgu_hbm, wgu_f, wgu_b, sem.at[0])
    _expert_weights(te_ref, nt_ref, nxt_ref, wd_hbm, wd_f, wd_b, sem.at[1])

    def ffn(rows):
        gu = jnp.dot(x_ref[rows, :], wgu_b[...], preferred_element_type=F32)
        act = (_silu(gu[:, :D_FF_EXPERT]) * gu[:, D_FF_EXPERT:]).astype(BF)
        o_ref[rows, :] = jnp.dot(act, wd_b[...], preferred_element_type=F32).astype(BF)

    @pl.when(jnp.logical_and(r < nt_ref[0], used_ref[r] > TG // 2))
    def _():
        ffn(slice(0, TG))

    @pl.when(jnp.logical_and(r < nt_ref[0], used_ref[r] <= TG // 2))
    def _():
        ffn(slice(0, TG // 2))


def _moe_ffn(te, nt, nxt, used, rows, w_gu, w_down):
    tile_map = lambda r, te, nt, nxt, used: (jnp.minimum(r, nt[0] - 1), 0)
    grid_spec = pltpu.PrefetchScalarGridSpec(
        num_scalar_prefetch=4,
        grid=(NT_G,),
        in_specs=[pl.BlockSpec((TG, D_MODEL), tile_map),
                  pl.BlockSpec(memory_space=pl.ANY), pl.BlockSpec(memory_space=pl.ANY)],
        out_specs=pl.BlockSpec((TG, D_MODEL), tile_map),
        scratch_shapes=[
            pltpu.VMEM((D_MODEL, 2 * D_FF_EXPERT), F32), pltpu.VMEM((D_MODEL, 2 * D_FF_EXPERT), BF),
            pltpu.VMEM((D_FF_EXPERT, D_MODEL), F32), pltpu.VMEM((D_FF_EXPERT, D_MODEL), BF),
            pltpu.SemaphoreType.DMA((2,)),
        ],
    )
    return pl.pallas_call(
        _moe_ffn_kernel,
        grid_spec=grid_spec,
        out_shape=jax.ShapeDtypeStruct((R_PAD, D_MODEL), BF),
        compiler_params=_params(("arbitrary",)),
        name="moe_ffn",
    )(te, nt, nxt, used, rows, w_gu, w_down)


def _combine_kernel(cnt_ref, cdst_ref, ys_ref, lp_ref, rg_ref, x_ref, mod_ref, fg_ref,
                    oa_ref, ob_ref, stage_s, sem, *, layer):
    i = pl.program_id(0)
    copies = functools.partial(_chunk_copies, cnt_ref=cnt_ref, cdst_ref=cdst_ref, rows_hbm=ys_ref, to_hbm=False)

    @pl.when(i == 0)
    def _():
        stage_s[...] = jnp.zeros_like(stage_s)
        for s in range(COMBINE_AHEAD):
            copies(s, stage=stage_s.at[s], sem=sem.at[s], wait=False)

    for k in range(SUB_PER_TILE):
        s = i * SUB_PER_TILE + k
        slot = k % COMBINE_SLOTS
        ahead = (k + COMBINE_AHEAD) % COMBINE_SLOTS
        rows = slice(k * ST, (k + 1) * ST)

        @pl.when(s + COMBINE_AHEAD < N_SUB)
        def _():
            copies(s + COMBINE_AHEAD, stage=stage_s.at[ahead], sem=sem.at[ahead], wait=False)

        copies(s, stage=stage_s.at[slot], sem=sem.at[slot], wait=True)

        lp = lp_ref[rows, :]
        r = lax.broadcasted_iota(jnp.int32, (ST, STAGE_ROWS), 1)
        staged = stage_s[slot]
        a = jnp.dot((r == lp[:, 0:1]).astype(BF), staged, preferred_element_type=F32)
        b = jnp.dot((r == lp[:, 1:2]).astype(BF), staged, preferred_element_type=F32)
        rg = rg_ref[rows, :]
        y = rg[:, 0:1] * a + rg[:, 1:2] * b
        xn = x_ref[rows, :] + _mod(mod_ref, layer, i, 5) * y
        out = _rms(xn) * fg_ref[...]

        @pl.when(i < CTX_TILES)
        def _():
            oa_ref[rows, :] = out

        @pl.when(i >= CTX_TILES)
        def _():
            ob_ref[rows, :] = out


def _combine(layer, cnt, cdst, ys, lp, rg, x, mod, final_g):
    assert SUB_PER_TILE % COMBINE_SLOTS == 0
    grid_spec = pltpu.PrefetchScalarGridSpec(
        num_scalar_prefetch=2,
        grid=(N_TILES,),
        in_specs=[
            pl.BlockSpec(memory_space=pl.ANY),
            pl.BlockSpec((TM, LANES), lambda i, *_: (i, 0)),
            pl.BlockSpec((TM, LANES), lambda i, *_: (i, 0)),
            pl.BlockSpec((TM, D_MODEL), lambda i, *_: (i, 0)),
            WHOLE, WHOLE,
        ],
        out_specs=[
            pl.BlockSpec((TM, D_MODEL), lambda i, *_: (jnp.minimum(i, CTX_TILES - 1), 0)),
            pl.BlockSpec((TM, D_MODEL), lambda i, *_: (jnp.maximum(i - CTX_TILES, 0), 0)),
        ],
        scratch_shapes=[pltpu.VMEM((COMBINE_SLOTS, STAGE_ROWS, D_MODEL), BF),
                        pltpu.SemaphoreType.DMA((COMBINE_SLOTS,))],
    )
    return pl.pallas_call(
        functools.partial(_combine_kernel, layer=layer),
        grid_spec=grid_spec,
        out_shape=[
            jax.ShapeDtypeStruct((N_CTX, D_MODEL), F32),
            jax.ShapeDtypeStruct((N_LAT, D_MODEL), F32),
        ],
        compiler_params=_params(("arbitrary",)),
        name="moe_combine",
    )(cnt, cdst, ys, lp, rg, x, mod, final_g)


def _group_layout(n_tiles):
    n = n_tiles[:, :SUB_PER_TILE, :N_EXPERTS].reshape(N_SUB, N_EXPERTS)
    tiles = (jnp.sum(n, axis=0) + TG - 1) // TG
    tile_end = jnp.cumsum(tiles)
    region = (tile_end - tiles) * TG
    dst = region[None, :] + jnp.cumsum(n, axis=0) - n
    seg_end = jnp.cumsum(n, axis=1)
    seg = seg_end - n
    row = jnp.arange(STAGE_CHUNKS, dtype=jnp.int32) * SEG_ALIGN
    owner = jnp.sum((row[None, :, None] >= seg_end[:, None, :]).astype(jnp.int32), axis=-1)
    own = jnp.minimum(owner, N_EXPERTS - 1)[..., None] == jnp.arange(N_EXPERTS)
    cdst = jnp.sum(jnp.where(own, (dst - seg)[:, None, :], 0), axis=-1) + row[None, :]
    cnt = seg_end[:, -1] // SEG_ALIGN
    nt = tile_end[-1]
    tile_id = jnp.minimum(jnp.arange(NT_G, dtype=jnp.int32), nt - 1)
    te = jnp.sum((tile_id[:, None] >= tile_end[None, :]).astype(jnp.int32), axis=-1)
    after = jnp.sum(jnp.where(te[:, None] == jnp.arange(N_EXPERTS), tile_end[None, :], 0), axis=-1)
    nxt = jnp.where(after < nt, jnp.sum((after[:, None] >= tile_end[None, :]).astype(jnp.int32), axis=-1), -1)
    mine = te[:, None] == jnp.arange(N_EXPERTS)
    region_end = jnp.sum(jnp.where(mine, (region + jnp.sum(n, axis=0))[None, :], 0), axis=-1)
    used = jnp.clip(region_end - tile_id * TG, 0, TG)
    i32 = lambda a: a.astype(jnp.int32)
    return (i32(cnt), i32(cdst.reshape(N_SUB * STAGE_CHUNKS)), i32(te), i32(nt.reshape(1)), i32(nxt), i32(used))


def _rope_tables():
    p = np.arange(DEC_SEQ)
    row = (p // GRID_W).astype(np.float32)
    col = (p % GRID_W).astype(np.float32)
    half = QK_DIM // 4
    freqs = (ROPE_BASE ** (-np.arange(half, dtype=np.float32) / half)).astype(np.float32)
    lane = np.arange(V_DIM)
    f = freqs[lane & (half - 1)]
    use_col = (lane & (2 * half)) != 0
    ang = (np.where(use_col[None, :], col[:, None], row[:, None]) * f[None, :]).astype(np.float32)
    upper = (lane & half) != 0
    sin = np.sin(ang)
    return jnp.asarray(np.cos(ang), F32), jnp.asarray(np.where(upper[None, :], sin, -sin), F32)


def kernel(x_prompt, x_sample, cache_k, cache_v, c, c_ctx, w_ada, b_ada, norm_mix_g, norm_ffn_g,
           w_in, lambda_qk, subln_g, conv_w, w_out, w_gu_dense, w_down_dense, w_router,
           w_gu_moe, w_down_moe, final_g):
    assert DEPTH == 2
    xs = (x_prompt.reshape(N_CTX, D_MODEL), x_sample.reshape(N_LAT, D_MODEL))
    cond = jnp.concatenate([c_ctx[None, :], c, jnp.zeros((COND_ROWS - 1 - DEC_BATCH, D_MODEL), F32)], axis=0)
    mod = _ada(cond, w_ada, b_ada)
    cos_t, sin_t = _rope_tables()
    cache_kt = jnp.transpose(cache_k, (0, 1, 3, 4, 5, 2)).reshape(DEC_BATCH, DEPTH, ATT_WIDTH, PAST_LEN)
    cache_v4 = cache_v.reshape(DEC_BATCH, DEPTH, PAST_LEN * N_HEADS, V_DIM)

    new_kv = None
    for layer in range(DEPTH):
        lam_init = 0.8 - 0.6 * math.exp(-0.3 * layer)
        q, v, kt, conv, nk, nv = _in_proj(layer, xs, mod, norm_mix_g, w_in, cos_t, sin_t, conv_w, new_kv)
        new_kv = (nk, nv)
        att = _attn_ctx(layer, q, kt, v, lambda_qk, subln_g, lam_init)
        att = _attn_lat(layer, q, kt, v, cache_kt, cache_v4, lambda_qk, subln_g, att, lam_init)
        if layer == 0:
            xs = _dense_layer(layer, att, conv, xs, mod, norm_ffn_g, w_out, w_gu_dense, w_down_dense)
        else:
            wr = jnp.pad(w_router[0], ((0, 0), (0, LANES - N_EXPERTS)))
            x1, h2, lp, rg, n_tiles = _out_proj(layer, att, conv, w_out, xs, mod, norm_ffn_g, wr)
            cnt, cdst, te, nt, nxt, used = _group_layout(n_tiles)
            xsort = _dispatch(cnt, cdst, h2, lp)
            ys = _moe_ffn(te, nt, nxt, used, xsort, w_gu_moe, w_down_moe)
            y_ctx, y_lat = _combine(layer, cnt, cdst, ys, lp, rg, x1, mod, final_g.reshape(1, D_MODEL))
    nk, nv = new_kv
    new_k = jnp.transpose(nk.reshape(BATCH, DEPTH, N_HEADS, 2, QK_DIM, SEQ), (0, 1, 5, 2, 3, 4))
    new_v = nv.reshape(BATCH, DEPTH, SEQ, N_HEADS, V_DIM)
    return (y_ctx.reshape(BATCH, SEQ, D_MODEL), y_lat.reshape(DEC_BATCH, DEC_SEQ, D_MODEL), new_k, new_v)
```

```python
import functools
import math

import numpy as np
import jax
import jax.numpy as jnp
from jax import lax
from jax.experimental import pallas as pl
from jax.experimental.pallas import tpu as pltpu

D_MODEL = 1024
BATCH = 16
SEQ = 256
DEPTH = 2
DEC_BATCH = 4
DEC_SEQ = 1024
PAST_LEN = 512
GRID_W = 64
ATT_WIDTH = 512
CONV_WIDTH = 512
N_HEADS = 4
V_DIM = 128
QK_DIM = 64
ROPE_BASE = 10000.0
D_FF = 2816
N_EXPERTS = 8
D_FF_EXPERT = 1408
N_MOD = 6
NORM_EPS = 1e-6
Q_SCALE = QK_DIM ** -0.5 * math.log2(math.e)
IN_COLS = 3 * ATT_WIDTH + 3 * CONV_WIDTH

N_CTX = BATCH * SEQ
N_LAT = DEC_BATCH * DEC_SEQ
N_TOK = N_CTX + N_LAT
TM = 1024
N_TILES = N_TOK // TM
CTX_TILES = N_CTX // TM
SEQ_PER_TILE = TM // SEQ
CTX_SEQ_PER_STEP = 1
COND_ROWS = 8
TN_IN = 1024
N_IN_TILES = IN_COLS // TN_IN
ROW_CHUNK = 512
FF_TM = 512
FF_HALVES = 2
FF_HALF = D_FF // FF_HALVES
STAGE_W_ROWS = 1024
STAGE_W_COLS = 512
TN_ADA = 1536
TG = 512
ST = 256
SUB_PER_TILE = TM // ST
N_SUB = N_TOK // ST
CTX_SUB = N_CTX // ST
SEG_ALIGN = 8
STAGE_ROWS = 640
STAGE_CHUNKS = STAGE_ROWS // SEG_ALIGN
WAIT_PIECES = (64, 32, 16, 8, 4, 2, 1)
COMBINE_SLOTS = 4
COMBINE_AHEAD = 2
NT_G = -(-(2 * N_TOK + N_SUB * N_EXPERTS * (SEG_ALIGN - 1) + N_EXPERTS * (TG - SEG_ALIGN)) // TG)
R_PAD = NT_G * TG
LANES = 128
VMEM_LIMIT = 60 * 1024 * 1024

BF = jnp.bfloat16
F32 = jnp.float32


def _params(sem, vmem=VMEM_LIMIT):
    return pltpu.CompilerParams(dimension_semantics=sem, vmem_limit_bytes=vmem)


def _mod_row(i):
    return jnp.where(i < CTX_TILES, 0, i - (CTX_TILES - 1))


WHOLE = pl.BlockSpec(memory_space=pltpu.VMEM)


def _mod(mod_ref, layer, i, c):
    return mod_ref[layer, pl.ds(_mod_row(i), 1), c * D_MODEL:(c + 1) * D_MODEL]


def _stream_specs(pair, width=D_MODEL):
    a = pl.BlockSpec((TM, width), lambda i, *_: (jnp.minimum(i, CTX_TILES - 1), 0))
    if pair:
        b = pl.BlockSpec((TM, width), lambda i, *_: (jnp.maximum(i - CTX_TILES, 0), 0))
    else:
        b = pl.BlockSpec((TM, width), lambda i, *_: (jnp.maximum(i, CTX_TILES), 0))
    return a, b


def _silu(x):
    return x / (1.0 + jnp.exp(-x))


def _rms(x):
    return x * lax.rsqrt(jnp.mean(x * x, axis=-1, keepdims=True) + NORM_EPS)


def _ada_kernel(c_ref, w_ref, b_ref, o_ref):
    s = _silu(c_ref[...]).astype(BF)
    bias = b_ref[pl.ds(pl.program_id(0), 1), :]
    o_ref[...] = jnp.dot(s, w_ref[...].astype(BF), preferred_element_type=F32) + bias


def _ada(cond, w_ada, b_ada):
    n = N_MOD * D_MODEL
    return pl.pallas_call(
        _ada_kernel,
        grid=(DEPTH, n // TN_ADA),
        in_specs=[
            pl.BlockSpec((COND_ROWS, D_MODEL), lambda l, j: (0, 0)),
            pl.BlockSpec((None, D_MODEL, TN_ADA), lambda l, j: (l, 0, j)),
            pl.BlockSpec((DEPTH, TN_ADA), lambda l, j: (0, j)),
        ],
        out_specs=pl.BlockSpec((None, COND_ROWS, TN_ADA), lambda l, j: (l, 0, j)),
        out_shape=jax.ShapeDtypeStruct((DEPTH, COND_ROWS, n), F32),
        compiler_params=_params(("arbitrary", "arbitrary")),
        name="ada_mod",
    )(cond, w_ada, b_ada)


def _in_kernel(*refs, layer, aliased):
    xa_ref, xb_ref, mod_ref, g_ref, w_ref, cos_ref, sin_ref, cw_ref = refs[:8]
    refs = refs[10:] if aliased else refs[8:]
    q_ref, v_ref, kt_ref, conv_ref, nk_ref, nv_ref, h_s, wb_s, gb_s, gc_s = refs
    i = pl.program_id(0)
    j = pl.program_id(1)
    lat = i >= CTX_TILES
    ctx = jnp.logical_not(lat)

    @pl.when(i == 0)
    def _():
        wb_s[j] = w_ref[...].astype(BF)

    chunks = [slice(c * ROW_CHUNK, (c + 1) * ROW_CHUNK) for c in range(TM // ROW_CHUNK)]
    seqs_per_chunk = ROW_CHUNK // SEQ

    def norm(rows):
        gain = g_ref[layer:layer + 1, :] * (1.0 + _mod(mod_ref, layer, i, 1))
        x = jnp.where(lat, xb_ref[rows, :], xa_ref[rows, :])
        h_s[rows, :] = (_rms(x) * gain + _mod(mod_ref, layer, i, 0)).astype(BF)

    def proj(rows):
        acc = jnp.dot(h_s[rows, :], wb_s[j], preferred_element_type=F32)
        return acc[:, :ATT_WIDTH], acc[:, ATT_WIDTH:]

    def roped(a, rows):
        cos = jnp.concatenate([cos_ref[rows, :]] * N_HEADS, axis=1)
        sin = jnp.concatenate([sin_ref[rows, :]] * N_HEADS, axis=1)
        lane = lax.broadcasted_iota(jnp.int32, a.shape, 1)
        upper = (lane & (QK_DIM // 4)) != 0
        partner = jnp.where(upper, pltpu.roll(a, QK_DIM // 4, 1), pltpu.roll(a, ATT_WIDTH - QK_DIM // 4, 1))
        return a * cos + partner * sin

    @pl.when(jnp.logical_and(j == 0, lat))
    def _():
        for rows in chunks:
            norm(rows)
            q, k = proj(rows)
            q_ref[rows, :] = (roped(q, rows) * Q_SCALE).astype(BF)
            kt_ref[:, rows] = roped(k, rows).T.astype(BF)

    @pl.when(jnp.logical_and(j == 0, ctx))
    def _():
        for c, rows in enumerate(chunks):
            norm(rows)
            q, k = proj(rows)
            q_ref[rows, :] = (q * Q_SCALE).astype(BF)
            kt = k.T
            kt_ref[:, rows] = kt.astype(BF)
            for s in range(seqs_per_chunk):
                nk_ref[c * seqs_per_chunk + s] = kt[:, s * SEQ:(s + 1) * SEQ]

    @pl.when(jnp.logical_and(j == 1, lat))
    def _():
        for rows in chunks:
            v, gb = proj(rows)
            v_ref[rows, :] = v.astype(BF)
            gb_s[rows, :] = gb

    @pl.when(jnp.logical_and(j == 1, ctx))
    def _():
        for c, rows in enumerate(chunks):
            v, gb = proj(rows)
            v_ref[rows, :] = v.astype(BF)
            gb_s[rows, :] = gb
            for s in range(seqs_per_chunk):
                for h in range(N_HEADS):
                    nv_ref[c * seqs_per_chunk + s, pl.ds(h, SEQ, stride=N_HEADS), :] = (
                        v[s * SEQ:(s + 1) * SEQ, h * V_DIM:(h + 1) * V_DIM])

    @pl.when(j == 2)
    def _():
        for rows in chunks:
            gc, xin = proj(rows)
            gc_s[rows, :] = gc * xin
        u = gc_s[...]
        seq = jnp.where(lat, DEC_SEQ, SEQ)
        pos = lax.broadcasted_iota(jnp.int32, (TM, 1), 0) & (seq - 1)
        prev = jnp.where(pos == 0, 0.0, pltpu.roll(u, 1, 0))
        nxt = jnp.where(pos == seq - 1, 0.0, pltpu.roll(u, TM - 1, 0))
        cw = cw_ref[layer]
        conv = prev * cw[0:1] + u * cw[1:2] + nxt * cw[2:3]
        conv_ref[...] = (gb_s[...] * conv).astype(BF)


def _in_proj(layer, xs, mod, g_mix, w_in, cos_t, sin_t, conv_w, new_kv):
    pair = isinstance(xs, tuple)
    xa, xb = xs if pair else (xs, xs)
    spec_a, spec_b = _stream_specs(pair)
    ctx_i = lambda i: jnp.minimum(i, CTX_TILES - 1)
    in_specs = [
        spec_a, spec_b,
        WHOLE, WHOLE,
        pl.BlockSpec((None, D_MODEL, TN_IN), lambda i, j: (layer, 0, jnp.where(i == 0, j, N_IN_TILES - 1))),
        WHOLE, WHOLE, WHOLE,
    ]
    args = [xa, xb, mod, g_mix, w_in, cos_t, sin_t, conv_w]
    aliases = {}
    if new_kv is not None:
        in_specs += [pl.BlockSpec(memory_space=pl.ANY), pl.BlockSpec(memory_space=pl.ANY)]
        args += list(new_kv)
        aliases = {8: 4, 9: 5}
    row_tile = pl.BlockSpec((TM, ATT_WIDTH), lambda i, j: (i, 0))
    return pl.pallas_call(
        functools.partial(_in_kernel, layer=layer, aliased=new_kv is not None),
        grid=(N_TILES, N_IN_TILES),
        in_specs=in_specs,
        out_specs=[
            row_tile,
            row_tile,
            pl.BlockSpec((None, ATT_WIDTH, TM), lambda i, j: (i, 0, 0)),
            row_tile,
            pl.BlockSpec((SEQ_PER_TILE, None, ATT_WIDTH, SEQ), lambda i, j: (ctx_i(i), layer, 0, 0)),
            pl.BlockSpec((SEQ_PER_TILE, None, SEQ * N_HEADS, V_DIM), lambda i, j: (ctx_i(i), layer, 0, 0)),
        ],
        out_shape=[
            jax.ShapeDtypeStruct((N_TOK, ATT_WIDTH), BF),
            jax.ShapeDtypeStruct((N_TOK, ATT_WIDTH), BF),
            jax.ShapeDtypeStruct((N_TILES, ATT_WIDTH, TM), BF),
            jax.ShapeDtypeStruct((N_TOK, CONV_WIDTH), BF),
            jax.ShapeDtypeStruct((BATCH, DEPTH, ATT_WIDTH, SEQ), F32),
            jax.ShapeDtypeStruct((BATCH, DEPTH, SEQ * N_HEADS, V_DIM), F32),
        ],
        scratch_shapes=[
            pltpu.VMEM((TM, D_MODEL), BF),
            pltpu.VMEM((N_IN_TILES, D_MODEL, TN_IN), BF),
            pltpu.VMEM((TM, CONV_WIDTH), F32),
            pltpu.VMEM((TM, CONV_WIDTH), F32),
        ],
        input_output_aliases=aliases,
        compiler_params=_params(("arbitrary", "arbitrary")),
        name=f"in_proj_l{layer}",
    )(*args)


def _lambda(lq_ref, layer, lam_init):
    lq = lq_ref[layer]
    a = jnp.exp(jnp.sum(lq[0:1] * lq[1:2], axis=-1, keepdims=True))
    b = jnp.exp(jnp.sum(lq[2:3] * lq[3:4], axis=-1, keepdims=True))
    return a - b + lam_init


def _head_norm(o, sg, lam_init):
    return _rms(o) * sg * (1.0 - lam_init)


def _attn_ctx_kernel(q_ref, kt_ref, v_ref, lq_ref, sg_ref, o_ref, sc_s, *, layer, lam_init):
    lam = _lambda(lq_ref, layer, lam_init)
    sg = sg_ref[layer:layer + 1, :]

    units = [(b, h) for b in range(CTX_SEQ_PER_STEP) for h in range(N_HEADS)]

    def scores(u):
        b, h = units[u]
        pos = slice(b * SEQ, (b + 1) * SEQ)
        for s in range(2):
            d = slice(h * V_DIM + s * QK_DIM, h * V_DIM + (s + 1) * QK_DIM)
            sc_s[u % 2, s] = jnp.dot(q_ref[pos, d], kt_ref[d, pos], preferred_element_type=F32)

    def finish(u):
        b, h = units[u]
        pos = slice(b * SEQ, (b + 1) * SEQ)
        cols = slice(h * V_DIM, (h + 1) * V_DIM)
        v = v_ref[pos, cols]
        outs = []
        for s in range(2):
            sc = sc_s[u % 2, s]
            e = jnp.exp2(sc - jnp.max(sc, axis=-1, keepdims=True))
            r = 1.0 / jnp.sum(e, axis=-1, keepdims=True)
            outs.append(jnp.dot(e.astype(BF), v, preferred_element_type=F32) * r)
        o = outs[0] - lam * outs[1]
        o_ref[pos, cols] = _head_norm(o, sg, lam_init).astype(BF)

    scores(0)
    for u in range(len(units)):
        if u + 1 < len(units):
            scores(u + 1)
        finish(u)


def _attn_ctx(layer, q, kt, v, lambda_qk, subln_g, lam_init):
    rows = CTX_SEQ_PER_STEP * SEQ
    per_tile = TM // rows
    return pl.pallas_call(
        functools.partial(_attn_ctx_kernel, layer=layer, lam_init=lam_init),
        grid=(N_CTX // rows,),
        in_specs=[
            pl.BlockSpec((rows, ATT_WIDTH), lambda b: (b, 0)),
            pl.BlockSpec((None, ATT_WIDTH, rows), lambda b: (b // per_tile, 0, b % per_tile)),
            pl.BlockSpec((rows, ATT_WIDTH), lambda b: (b, 0)),
            WHOLE, WHOLE,
        ],
        out_specs=pl.BlockSpec((rows, ATT_WIDTH), lambda b: (b, 0)),
        out_shape=jax.ShapeDtypeStruct((N_TOK, ATT_WIDTH), BF),
        scratch_shapes=[pltpu.VMEM((2, 2, SEQ, SEQ), F32)],
        compiler_params=_params(("arbitrary",)),
        name=f"attn_ctx_l{layer}",
    )(q, kt, v, lambda_qk, subln_g)


TQ = 256
LAT_Q_PER_STEP = 2


def _attn_lat_kernel(q_ref, kt_ref, v_ref, ckt_ref, cv_ref, lq_ref, sg_ref, att_in_ref, o_ref, sc_s, *,
                     layer, lam_init):
    del att_in_ref
    lam = _lambda(lq_ref, layer, lam_init)
    sg = sg_ref[layer:layer + 1, :]

    units = [(b, h) for b in range(LAT_Q_PER_STEP) for h in range(N_HEADS)]

    def scores(u, s):
        b, h = units[u]
        d = slice(h * V_DIM + s * QK_DIM, h * V_DIM + (s + 1) * QK_DIM)
        q = q_ref[b * TQ:(b + 1) * TQ, d]
        sc_s[u % 2, s, :, :PAST_LEN] = jnp.dot(q, ckt_ref[d, :].astype(BF), preferred_element_type=F32)
        sc_s[u % 2, s, :, PAST_LEN:] = jnp.dot(q, kt_ref[d, :], preferred_element_type=F32)

    def softmax(u, s):
        sc = sc_s[u % 2, s]
        e = jnp.exp2(sc - jnp.max(sc, axis=-1, keepdims=True))
        return e, 1.0 / jnp.sum(e, axis=-1, keepdims=True)

    def finish(u, p1, p2):
        b, h = units[u]
        cols = slice(h * V_DIM, (h + 1) * V_DIM)
        e = jnp.concatenate([p1[0].astype(BF), p2[0].astype(BF)], axis=0)
        vc = cv_ref[pl.ds(h, PAST_LEN, stride=N_HEADS), :].astype(BF)
        pv = jnp.dot(e[:, :PAST_LEN], vc, preferred_element_type=F32)
        pv = pv + jnp.dot(e[:, PAST_LEN:], v_ref[:, cols], preferred_element_type=F32)
        o = pv[:TQ] * p1[1] - pv[TQ:] * (lam * p2[1])
        o_ref[b * TQ:(b + 1) * TQ, cols] = _head_norm(o, sg, lam_init).astype(BF)

    scores(0, 0)
    scores(0, 1)
    for u in range(len(units)):
        more = u + 1 < len(units)
        if more:
            scores(u + 1, 0)
        p1 = softmax(u, 0)
        if more:
            scores(u + 1, 1)
        finish(u, p1, softmax(u, 1))


def _attn_lat(layer, q, kt, v, cache_kt, cache_v, lambda_qk, subln_g, att, lam_init):
    rows = LAT_Q_PER_STEP * TQ
    nqb = DEC_SEQ // rows
    q0 = N_CTX // rows
    return pl.pallas_call(
        functools.partial(_attn_lat_kernel, layer=layer, lam_init=lam_init),
        grid=(DEC_BATCH, nqb),
        in_specs=[
            pl.BlockSpec((rows, ATT_WIDTH), lambda b, t: (q0 + b * nqb + t, 0)),
            pl.BlockSpec((None, ATT_WIDTH, DEC_SEQ), lambda b, t: (CTX_TILES + b, 0, 0)),
            pl.BlockSpec((DEC_SEQ, ATT_WIDTH), lambda b, t: (CTX_TILES + b, 0)),
            pl.BlockSpec((None, None, ATT_WIDTH, PAST_LEN), lambda b, t: (b, layer, 0, 0)),
            pl.BlockSpec((None, None, PAST_LEN * N_HEADS, V_DIM), lambda b, t: (b, layer, 0, 0)),
            WHOLE, WHOLE,
            pl.BlockSpec(memory_space=pl.ANY),
        ],
        out_specs=pl.BlockSpec((rows, ATT_WIDTH), lambda b, t: (q0 + b * nqb + t, 0)),
        out_shape=jax.ShapeDtypeStruct((N_TOK, ATT_WIDTH), BF),
        scratch_shapes=[pltpu.VMEM((2, 2, TQ, PAST_LEN + DEC_SEQ), F32)],
        input_output_aliases={7: 0},
        compiler_params=_params(("arbitrary", "arbitrary")),
        name=f"attn_lat_l{layer}",
    )(q, kt, v, cache_kt, cache_v, lambda_qk, subln_g, att)


def _out_kernel(att_ref, conv_ref, w_ref, xa_ref, xb_ref, mod_ref, gf_ref, *rest, layer, route):
    if route:
        wr_ref, xo_ref, h2_ref, lp_ref, rg_ref, n_ref, wb_s = rest
    else:
        xo_ref, h2_ref, wb_s = rest
    i = pl.program_id(0)

    @pl.when(i == 0)
    def _():
        wb_s[...] = w_ref[...].astype(BF)

    lat = i >= CTX_TILES
    gain = gf_ref[layer:layer + 1, :] * (1.0 + _mod(mod_ref, layer, i, 4))
    for c in range(TM // ROW_CHUNK):
        rows = slice(c * ROW_CHUNK, (c + 1) * ROW_CHUNK)
        mo = jnp.dot(att_ref[rows, :], wb_s[:ATT_WIDTH, :], preferred_element_type=F32)
        mo = mo + jnp.dot(conv_ref[rows, :], wb_s[ATT_WIDTH:, :], preferred_element_type=F32)
        xn = jnp.where(lat, xb_ref[rows, :], xa_ref[rows, :]) + _mod(mod_ref, layer, i, 2) * mo
        xo_ref[rows, :] = xn
        h2_ref[rows, :] = (_rms(xn) * gain + _mod(mod_ref, layer, i, 3)).astype(BF)
    if route:
        _route(h2_ref[...], wr_ref, lp_ref, rg_ref, n_ref)


def _out_proj(layer, att, conv, w_out, xs, mod, g_ffn, w_router_pad=None):
    pair = isinstance(xs, tuple)
    xa, xb = xs if pair else (xs, xs)
    spec_a, spec_b = _stream_specs(pair)
    row_spec = pl.BlockSpec((TM, D_MODEL), lambda i: (i, 0))
    lane_spec = pl.BlockSpec((TM, LANES), lambda i: (i, 0))
    in_specs = [
        pl.BlockSpec((TM, ATT_WIDTH), lambda i: (i, 0)),
        pl.BlockSpec((TM, CONV_WIDTH), lambda i: (i, 0)),
        pl.BlockSpec((None, D_MODEL, D_MODEL), lambda i: (layer, 0, 0)),
        spec_a, spec_b,
        WHOLE, WHOLE,
    ]
    args = [att, conv, w_out, xa, xb, mod, g_ffn]
    out_specs = [row_spec, row_spec]
    out_shape = [jax.ShapeDtypeStruct((N_TOK, D_MODEL), F32), jax.ShapeDtypeStruct((N_TOK, D_MODEL), BF)]
    route = w_router_pad is not None
    if route:
        in_specs.append(WHOLE)
        args.append(w_router_pad)
        out_specs += [lane_spec, lane_spec, pl.BlockSpec((None, 8, LANES), lambda i: (i, 0, 0))]
        out_shape += [jax.ShapeDtypeStruct((N_TOK, LANES), jnp.int32), jax.ShapeDtypeStruct((N_TOK, LANES), F32),
                      jax.ShapeDtypeStruct((N_TILES, 8, LANES), jnp.int32)]
    return pl.pallas_call(
        functools.partial(_out_kernel, layer=layer, route=route),
        grid=(N_TILES,),
        in_specs=in_specs,
        out_specs=out_specs,
        out_shape=out_shape,
        scratch_shapes=[pltpu.VMEM((D_MODEL, D_MODEL), BF)],
        compiler_params=_params(("arbitrary",)),
        name=f"out_proj_l{layer}",
    )(*args)


def _weight_pieces(layer):
    pieces = []
    for c0 in range(0, D_MODEL, STAGE_W_COLS):
        cols = slice(c0, c0 + STAGE_W_COLS)
        pieces.append((0, layer, slice(0, D_MODEL), cols, None, slice(0, D_MODEL), cols))
    for half in range(FF_HALVES):
        for part in range(2):
            src0 = part * D_FF + half * FF_HALF
            for off in range(0, FF_HALF, STAGE_W_COLS):
                n = min(STAGE_W_COLS, FF_HALF - off)
                pieces.append((1, 0, slice(0, D_MODEL), slice(src0 + off, src0 + off + n),
                               half, slice(0, D_MODEL), slice(part * FF_HALF + off, part * FF_HALF + off + n)))
    for r0 in range(0, D_FF, STAGE_W_ROWS):
        rows = slice(r0, min(r0 + STAGE_W_ROWS, D_FF))
        for c0 in range(0, D_MODEL, STAGE_W_COLS):
            cols = slice(c0, c0 + STAGE_W_COLS)
            pieces.append((2, 0, rows, cols, None, rows, cols))
    return pieces


def _load_dense_weights(layer, hbm, resident, stage_s, sem):
    pieces = _weight_pieces(layer)

    def copy(k):
        src, idx, rows, cols, _, _, _ = pieces[k]
        nr, nc = rows.stop - rows.start, cols.stop - cols.start
        return pltpu.make_async_copy(hbm[src].at[idx, rows, cols], stage_s.at[k % 2, :nr, :nc], sem.at[k % 2])

    copy(0).start()
    for k, (src, _, rows, cols, didx, drows, dcols) in enumerate(pieces):
        if k + 1 < len(pieces):
            copy(k + 1).start()
        copy(k).wait()
        nr, nc = rows.stop - rows.start, cols.stop - cols.start
        piece = stage_s[k % 2, :nr, :nc].astype(BF)
        if didx is None:
            resident[src][drows, dcols] = piece
        else:
            resident[src][didx, drows, dcols] = piece


def _dense_layer_kernel(att_ref, conv_ref, xa_ref, xb_ref, mod_ref, gf_ref, wo_hbm, wgu_hbm, wd_hbm,
                        o_ref, wo_b, wgu_b, wd_b, stage_s, sem, *, layer):
    i = pl.program_id(0)
    tile = i // (TM // FF_TM)

    @pl.when(i == 0)
    def _():
        _load_dense_weights(layer, (wo_hbm, wgu_hbm, wd_hbm), (wo_b, wgu_b, wd_b), stage_s, sem)

    mo = jnp.dot(att_ref[...], wo_b[:ATT_WIDTH, :], preferred_element_type=F32)
    mo = mo + jnp.dot(conv_ref[...], wo_b[ATT_WIDTH:, :], preferred_element_type=F32)
    x = jnp.where(tile >= CTX_TILES, xb_ref[...], xa_ref[...])
    xn = x + _mod(mod_ref, layer, tile, 2) * mo
    h = (_rms(xn) * gf_ref[layer:layer + 1, :]) * (1.0 + _mod(mod_ref, layer, tile, 4)) + _mod(mod_ref, layer, tile, 3)
    h = h.astype(BF)
    y = None
    for half in range(FF_HALVES):
        gu = jnp.dot(h, wgu_b[half], preferred_element_type=F32)
        act = (_silu(gu[:, :FF_HALF]) * gu[:, FF_HALF:]).astype(BF)
        part = jnp.dot(act, wd_b[half * FF_HALF:(half + 1) * FF_HALF, :], preferred_element_type=F32)
        y = part if y is None else y + part
    o_ref[...] = xn + _mod(mod_ref, layer, tile, 5) * y


def _dense_layer(layer, att, conv, xs, mod, g_ffn, w_out, w_gu, w_down):
    xa, xb = xs
    n_ctx = N_CTX // FF_TM
    row = lambda width: pl.BlockSpec((FF_TM, width), lambda i: (i, 0))
    hbm = pl.BlockSpec(memory_space=pl.ANY)
    return pl.pallas_call(
        functools.partial(_dense_layer_kernel, layer=layer),
        grid=(N_TOK // FF_TM,),
        in_specs=[
            row(ATT_WIDTH), row(CONV_WIDTH),
            pl.BlockSpec((FF_TM, D_MODEL), lambda i: (jnp.minimum(i, n_ctx - 1), 0)),
            pl.BlockSpec((FF_TM, D_MODEL), lambda i: (jnp.maximum(i - n_ctx, 0), 0)),
            WHOLE, WHOLE, hbm, hbm, hbm,
        ],
        out_specs=row(D_MODEL),
        out_shape=jax.ShapeDtypeStruct((N_TOK, D_MODEL), F32),
        scratch_shapes=[
            pltpu.VMEM((D_MODEL, D_MODEL), BF),
            pltpu.VMEM((FF_HALVES, D_MODEL, 2 * FF_HALF), BF),
            pltpu.VMEM((D_FF, D_MODEL), BF),
            pltpu.VMEM((2, STAGE_W_ROWS, STAGE_W_COLS), F32),
            pltpu.SemaphoreType.DMA((2,)),
        ],
        compiler_params=_params(("arbitrary",)),
        name="dense_layer",
    )(att, conv, xa, xb, mod, g_ffn, w_out, w_gu, w_down)


def _route(h, wr_ref, lp_ref, rg_ref, n_ref):
    logits = jnp.dot(h, wr_ref[...].astype(BF), preferred_element_type=F32)
    lane = lax.broadcasted_iota(jnp.int32, logits.shape, 1)
    lg = jnp.where(lane < N_EXPERTS, logits, -jnp.inf)
    m1 = jnp.max(lg, axis=-1, keepdims=True)
    i1 = jnp.min(jnp.where(lg == m1, lane, LANES), axis=-1, keepdims=True)
    lg2 = jnp.where(lane == i1, -jnp.inf, lg)
    m2 = jnp.max(lg2, axis=-1, keepdims=True)
    i2 = jnp.min(jnp.where(lg2 == m2, lane, LANES), axis=-1, keepdims=True)
    e2 = jnp.exp(m2 - m1)
    w1 = 1.0 / (1.0 + e2)
    w2 = e2 / (1.0 + e2)

    sel1 = lane == i1
    sel2 = lane == i2
    onehot = jnp.logical_or(sel1, sel2)
    rows = lax.broadcasted_iota(jnp.int32, (ST, ST), 0)
    colsi = lax.broadcasted_iota(jnp.int32, (ST, ST), 1)
    earlier = (colsi < rows).astype(BF)
    onehot_b = onehot.astype(BF)
    before = jnp.concatenate(
        [jnp.dot(earlier, onehot_b[s * ST:(s + 1) * ST], preferred_element_type=F32) for s in range(SUB_PER_TILE)],
        axis=0)
    onehot_f = onehot.astype(F32)
    counts = [jnp.sum(onehot_f[s * ST:(s + 1) * ST], axis=0, keepdims=True) for s in range(SUB_PER_TILE)]
    counts = jnp.concatenate(counts + [jnp.zeros((8 - SUB_PER_TILE, LANES), F32)], axis=0).astype(jnp.int32)
    seg_len = ((counts + (SEG_ALIGN - 1)) // SEG_ALIGN) * SEG_ALIGN
    n_ref[...] = seg_len
    la = lax.broadcasted_iota(jnp.int32, (LANES, LANES), 0)
    lb = lax.broadcasted_iota(jnp.int32, (LANES, LANES), 1)
    seg_start = jnp.dot(seg_len.astype(F32).astype(BF), (la < lb).astype(BF), preferred_element_type=F32)
    start = jnp.concatenate(
        [jnp.broadcast_to(seg_start[s:s + 1], (ST, LANES)) for s in range(SUB_PER_TILE)], axis=0)
    where = before + start
    lp1 = jnp.sum(jnp.where(sel1, where, 0.0), axis=-1, keepdims=True).astype(jnp.int32)
    lp2 = jnp.sum(jnp.where(sel2, where, 0.0), axis=-1, keepdims=True).astype(jnp.int32)
    lp_ref[...] = jnp.where(lane == 0, lp1, jnp.where(lane == 1, lp2, 0))
    rg_ref[...] = jnp.where(lane == 0, w1, jnp.where(lane == 1, w2, 0.0))


def _chunk_copies(s, cnt_ref, cdst_ref, stage, rows_hbm, sem, *, to_hbm, wait):
    def copy(v, h):
        return pltpu.make_async_copy(v, h, sem) if to_hbm else pltpu.make_async_copy(h, v, sem)

    if wait:
        for z in WAIT_PIECES:
            @pl.when((cnt_ref[s] & z) != 0)
            def _():
                copy(stage.at[pl.ds(0, z * SEG_ALIGN)], rows_hbm.at[pl.ds(0, z * SEG_ALIGN)]).wait()
        return

    def start(c, priority):
        v = stage.at[pl.ds(pl.multiple_of(c * SEG_ALIGN, SEG_ALIGN), SEG_ALIGN)]
        h = rows_hbm.at[pl.ds(pl.multiple_of(cdst_ref[s * STAGE_CHUNKS + c], SEG_ALIGN), SEG_ALIGN)]
        copy(v, h).start(priority=priority)

    def pair(p, carry):
        start(2 * p, 0)
        start(2 * p + 1, 1)
        return carry

    n = cnt_ref[s]
    lax.fori_loop(0, n // 2, pair, 0)

    @pl.when((n & 1) != 0)
    def _():
        start(n - 1, 0)


def _dispatch_kernel(cnt_ref, cdst_ref, h_ref, lp_ref, xs_ref, stage_s, sem):
    copies = functools.partial(_chunk_copies, cnt_ref=cnt_ref, cdst_ref=cdst_ref, rows_hbm=xs_ref, to_hbm=True)
    for k in range(SUB_PER_TILE):
        s = pl.program_id(0) * SUB_PER_TILE + k
        slot = k % 2
        rows = slice(k * ST, (k + 1) * ST)

        @pl.when(s >= 2)
        def _():
            copies(s - 2, stage=stage_s.at[slot], sem=sem.at[slot], wait=True)

        lpt = lp_ref[rows, :].T
        r = lax.broadcasted_iota(jnp.int32, (STAGE_ROWS, ST), 0)
        pick = jnp.logical_or(r == lpt[0:1, :], r == lpt[1:2, :]).astype(BF)
        stage_s[slot] = jnp.dot(pick, h_ref[rows, :], preferred_element_type=F32).astype(BF)
        copies(s, stage=stage_s.at[slot], sem=sem.at[slot], wait=False)

    @pl.when(pl.program_id(0) == N_TILES - 1)
    def _():
        copies(N_SUB - 2, stage=stage_s.at[0], sem=sem.at[0], wait=True)
        copies(N_SUB - 1, stage=stage_s.at[1], sem=sem.at[1], wait=True)


def _dispatch(cnt, cdst, h2, lp):
    assert SUB_PER_TILE % 2 == 0
    grid_spec = pltpu.PrefetchScalarGridSpec(
        num_scalar_prefetch=2,
        grid=(N_TILES,),
        in_specs=[
            pl.BlockSpec((TM, D_MODEL), lambda i, *_: (i, 0)),
            pl.BlockSpec((TM, LANES), lambda i, *_: (i, 0)),
        ],
        out_specs=pl.BlockSpec(memory_space=pl.ANY),
        scratch_shapes=[pltpu.VMEM((2, STAGE_ROWS, D_MODEL), BF), pltpu.SemaphoreType.DMA((2,))],
    )
    return pl.pallas_call(
        _dispatch_kernel,
        grid_spec=grid_spec,
        out_shape=jax.ShapeDtypeStruct((R_PAD, D_MODEL), BF),
        compiler_params=_params(("arbitrary",)),
        name="moe_dispatch",
    )(cnt, cdst, h2, lp)


def _expert_weights(te_ref, nt_ref, nxt_ref, w_hbm, wf_s, wb_s, sem):
    r = pl.program_id(0)

    def fetch(e):
        return pltpu.make_async_copy(w_hbm.at[0, e], wf_s, sem)

    @pl.when(r == 0)
    def _():
        fetch(te_ref[0]).start()

    first = jnp.logical_or(r == 0, te_ref[r] != te_ref[jnp.maximum(r - 1, 0)])

    @pl.when(jnp.logical_and(r < nt_ref[0], first))
    def _():
        fetch(te_ref[r]).wait()
        wb_s[...] = wf_s[...].astype(BF)

        @pl.when(nxt_ref[r] >= 0)
        def _():
            fetch(nxt_ref[r]).start()


def _moe_ffn_kernel(te_ref, nt_ref, nxt_ref, used_ref, x_ref, wgu_hbm, wd_hbm, o_ref,
                    wgu_f, wgu_b, wd_f, wd_b, sem):
    r = pl.program_id(0)
    _expert_weights(te_ref, nt_ref, nxt_ref, wgu_hbm, wgu_f, wgu_b, sem.at[0])
    _expert_weights(te_ref, nt_ref, nxt_ref, wd_hbm, wd_f, wd_b, sem.at[1])

    def ffn(rows):
        gu = jnp.dot(x_ref[rows, :], wgu_b[...], preferred_element_type=F32)
        act = (_silu(gu[:, :D_FF_EXPERT]) * gu[:, D_FF_EXPERT:]).astype(BF)
        o_ref[rows, :] = jnp.dot(act, wd_b[...], preferred_element_type=F32).astype(BF)

    @pl.when(jnp.logical_and(r < nt_ref[0], used_ref[r] > TG // 2))
    def _():
        ffn(slice(0, TG))

    @pl.when(jnp.logical_and(r < nt_ref[0], used_ref[r] <= TG // 2))
    def _():
        ffn(slice(0, TG // 2))


def _moe_ffn(te, nt, nxt, used, rows, w_gu, w_down):
    tile_map = lambda r, te, nt, nxt, used: (jnp.minimum(r, nt[0] - 1), 0)
    grid_spec = pltpu.PrefetchScalarGridSpec(
        num_scalar_prefetch=4,
        grid=(NT_G,),
        in_specs=[pl.BlockSpec((TG, D_MODEL), tile_map),
                  pl.BlockSpec(memory_space=pl.ANY), pl.BlockSpec(memory_space=pl.ANY)],
        out_specs=pl.BlockSpec((TG, D_MODEL), tile_map),
        scratch_shapes=[
            pltpu.VMEM((D_MODEL, 2 * D_FF_EXPERT), F32), pltpu.VMEM((D_MODEL, 2 * D_FF_EXPERT), BF),
            pltpu.VMEM((D_FF_EXPERT, D_MODEL), F32), pltpu.VMEM((D_FF_EXPERT, D_MODEL), BF),
            pltpu.SemaphoreType.DMA((2,)),
        ],
    )
    return pl.pallas_call(
        _moe_ffn_kernel,
        grid_spec=grid_spec,
        out_shape=jax.ShapeDtypeStruct((R_PAD, D_MODEL), BF),
        compiler_params=_params(("arbitrary",)),
        name="moe_ffn",
    )(te, nt, nxt, used, rows, w_gu, w_down)


def _combine_kernel(cnt_ref, cdst_ref, ys_ref, lp_ref, rg_ref, x_ref, mod_ref, fg_ref,
                    oa_ref, ob_ref, stage_s, sem, *, layer):
    i = pl.program_id(0)
    copies = functools.partial(_chunk_copies, cnt_ref=cnt_ref, cdst_ref=cdst_ref, rows_hbm=ys_ref, to_hbm=False)

    @pl.when(i == 0)
    def _():
        stage_s[...] = jnp.zeros_like(stage_s)
        for s in range(COMBINE_AHEAD):
            copies(s, stage=stage_s.at[s], sem=sem.at[s], wait=False)

    for k in range(SUB_PER_TILE):
        s = i * SUB_PER_TILE + k
        slot = k % COMBINE_SLOTS
        ahead = (k + COMBINE_AHEAD) % COMBINE_SLOTS
        rows = slice(k * ST, (k + 1) * ST)

        @pl.when(s + COMBINE_AHEAD < N_SUB)
        def _():
            copies(s + COMBINE_AHEAD, stage=stage_s.at[ahead], sem=sem.at[ahead], wait=False)

        copies(s, stage=stage_s.at[slot], sem=sem.at[slot], wait=True)

        lp = lp_ref[rows, :]
        r = lax.broadcasted_iota(jnp.int32, (ST, STAGE_ROWS), 1)
        staged = stage_s[slot]
        a = jnp.dot((r == lp[:, 0:1]).astype(BF), staged, preferred_element_type=F32)
        b = jnp.dot((r == lp[:, 1:2]).astype(BF), staged, preferred_element_type=F32)
        rg = rg_ref[rows, :]
        y = rg[:, 0:1] * a + rg[:, 1:2] * b
        xn = x_ref[rows, :] + _mod(mod_ref, layer, i, 5) * y
        out = _rms(xn) * fg_ref[...]

        @pl.when(i < CTX_TILES)
        def _():
            oa_ref[rows, :] = out

        @pl.when(i >= CTX_TILES)
        def _():
            ob_ref[rows, :] = out


def _combine(layer, cnt, cdst, ys, lp, rg, x, mod, final_g):
    assert SUB_PER_TILE % COMBINE_SLOTS == 0
    grid_spec = pltpu.PrefetchScalarGridSpec(
        num_scalar_prefetch=2,
        grid=(N_TILES,),
        in_specs=[
            pl.BlockSpec(memory_space=pl.ANY),
            pl.BlockSpec((TM, LANES), lambda i, *_: (i, 0)),
            pl.BlockSpec((TM, LANES), lambda i, *_: (i, 0)),
            pl.BlockSpec((TM, D_MODEL), lambda i, *_: (i, 0)),
            WHOLE, WHOLE,
        ],
        out_specs=[
            pl.BlockSpec((TM, D_MODEL), lambda i, *_: (jnp.minimum(i, CTX_TILES - 1), 0)),
            pl.BlockSpec((TM, D_MODEL), lambda i, *_: (jnp.maximum(i - CTX_TILES, 0), 0)),
        ],
        scratch_shapes=[pltpu.VMEM((COMBINE_SLOTS, STAGE_ROWS, D_MODEL), BF),
                        pltpu.SemaphoreType.DMA((COMBINE_SLOTS,))],
    )
    return pl.pallas_call(
        functools.partial(_combine_kernel, layer=layer),
        grid_spec=grid_spec,
        out_shape=[
            jax.ShapeDtypeStruct((N_CTX, D_MODEL), F32),
            jax.ShapeDtypeStruct((N_LAT, D_MODEL), F32),
        ],
        compiler_params=_params(("arbitrary",)),
        name="moe_combine",
    )(cnt, cdst, ys, lp, rg, x, mod, final_g)


def _group_layout(n_tiles):
    n = n_tiles[:, :SUB_PER_TILE, :N_EXPERTS].reshape(N_SUB, N_EXPERTS)
    tiles = (jnp.sum(n, axis=0) + TG - 1) // TG
    tile_end = jnp.cumsum(tiles)
    region = (tile_end - tiles) * TG
    dst = region[None, :] + jnp.cumsum(n, axis=0) - n
    seg_end = jnp.cumsum(n, axis=1)
    seg = seg_end - n
    row = jnp.arange(STAGE_CHUNKS, dtype=jnp.int32) * SEG_ALIGN
    owner = jnp.sum((row[None, :, None] >= seg_end[:, None, :]).astype(jnp.int32), axis=-1)
    own = jnp.minimum(owner, N_EXPERTS - 1)[..., None] == jnp.arange(N_EXPERTS)
    cdst = jnp.sum(jnp.where(own, (dst - seg)[:, None, :], 0), axis=-1) + row[None, :]
    cnt = seg_end[:, -1] // SEG_ALIGN
    nt = tile_end[-1]
    tile_id = jnp.minimum(jnp.arange(NT_G, dtype=jnp.int32), nt - 1)
    te = jnp.sum((tile_id[:, None] >= tile_end[None, :]).astype(jnp.int32), axis=-1)
    after = jnp.sum(jnp.where(te[:, None] == jnp.arange(N_EXPERTS), tile_end[None, :], 0), axis=-1)
    nxt = jnp.where(after < nt, jnp.sum((after[:, None] >= tile_end[None, :]).astype(jnp.int32), axis=-1), -1)
    mine = te[:, None] == jnp.arange(N_EXPERTS)
    region_end = jnp.sum(jnp.where(mine, (region + jnp.sum(n, axis=0))[None, :], 0), axis=-1)
    used = jnp.clip(region_end - tile_id * TG, 0, TG)
    i32 = lambda a: a.astype(jnp.int32)
    return (i32(cnt), i32(cdst.reshape(N_SUB * STAGE_CHUNKS)), i32(te), i32(nt.reshape(1)), i32(nxt), i32(used))


def _rope_tables():
    p = np.arange(DEC_SEQ)
    row = (p // GRID_W).astype(np.float32)
    col = (p % GRID_W).astype(np.float32)
    half = QK_DIM // 4
    freqs = (ROPE_BASE ** (-np.arange(half, dtype=np.float32) / half)).astype(np.float32)
    lane = np.arange(V_DIM)
    f = freqs[lane & (half - 1)]
    use_col = (lane & (2 * half)) != 0
    ang = (np.where(use_col[None, :], col[:, None], row[:, None]) * f[None, :]).astype(np.float32)
    upper = (lane & half) != 0
    sin = np.sin(ang)
    return jnp.asarray(np.cos(ang), F32), jnp.asarray(np.where(upper[None, :], sin, -sin), F32)


def kernel(x_prompt, x_sample, cache_k, cache_v, c, c_ctx, w_ada, b_ada, norm_mix_g, norm_ffn_g,
           w_in, lambda_qk, subln_g, conv_w, w_out, w_gu_dense, w_down_dense, w_router,
           w_gu_moe, w_down_moe, final_g):
    assert DEPTH == 2
    xs = (x_prompt.reshape(N_CTX, D_MODEL), x_sample.reshape(N_LAT, D_MODEL))
    cond = jnp.concatenate([c_ctx[None, :], c, jnp.zeros((COND_ROWS - 1 - DEC_BATCH, D_MODEL), F32)], axis=0)
    mod = _ada(cond, w_ada, b_ada)
    cos_t, sin_t = _rope_tables()
    cache_kt = jnp.transpose(cache_k, (0, 1, 3, 4, 5, 2)).reshape(DEC_BATCH, DEPTH, ATT_WIDTH, PAST_LEN)
    cache_v4 = cache_v.reshape(DEC_BATCH, DEPTH, PAST_LEN * N_HEADS, V_DIM)

    new_kv = None
    for layer in range(DEPTH):
        lam_init = 0.8 - 0.6 * math.exp(-0.3 * layer)
        q, v, kt, conv, nk, nv = _in_proj(layer, xs, mod, norm_mix_g, w_in, cos_t, sin_t, conv_w, new_kv)
        new_kv = (nk, nv)
        att = _attn_ctx(layer, q, kt, v, lambda_qk, subln_g, lam_init)
        att = _attn_lat(layer, q, kt, v, cache_kt, cache_v4, lambda_qk, subln_g, att, lam_init)
        if layer == 0:
            xs = _dense_layer(layer, att, conv, xs, mod, norm_ffn_g, w_out, w_gu_dense, w_down_dense)
        else:
            wr = jnp.pad(w_router[0], ((0, 0), (0, LANES - N_EXPERTS)))
            x1, h2, lp, rg, n_tiles = _out_proj(layer, att, conv, w_out, xs, mod, norm_ffn_g, wr)
            cnt, cdst, te, nt, nxt, used = _group_layout(n_tiles)
            xsort = _dispatch(cnt, cdst, h2, lp)
            ys = _moe_ffn(te, nt, nxt, used, xsort, w_gu_moe, w_down_moe)
            y_ctx, y_lat = _combine(layer, cnt, cdst, ys, lp, rg, x1, mod, final_g.reshape(1, D_MODEL))
    nk, nv = new_kv
    new_k = jnp.transpose(nk.reshape(BATCH, DEPTH, N_HEADS, 2, QK_DIM, SEQ), (0, 1, 5, 2, 3, 4))
    new_v = nv.reshape(BATCH, DEPTH, SEQ, N_HEADS, V_DIM)
    return (y_ctx.reshape(BATCH, SEQ, D_MODEL), y_lat.reshape(DEC_BATCH, DEC_SEQ, D_MODEL), new_k, new_v)
```

```python
import functools
import math

import numpy as np
import jax
import jax.numpy as jnp
from jax import lax
from jax.experimental import pallas as pl
from jax.experimental.pallas import tpu as pltpu

D_MODEL = 1024
BATCH = 16
SEQ = 256
DEPTH = 2
DEC_BATCH = 4
DEC_SEQ = 1024
PAST_LEN = 512
GRID_W = 64
ATT_WIDTH = 512
CONV_WIDTH = 512
N_HEADS = 4
V_DIM = 128
QK_DIM = 64
ROPE_BASE = 10000.0
D_FF = 2816
N_EXPERTS = 8
D_FF_EXPERT = 1408
N_MOD = 6
NORM_EPS = 1e-6
Q_SCALE = QK_DIM ** -0.5 * math.log2(math.e)
IN_COLS = 3 * ATT_WIDTH + 3 * CONV_WIDTH

N_CTX = BATCH * SEQ
N_LAT = DEC_BATCH * DEC_SEQ
N_TOK = N_CTX + N_LAT
TM = 1024
N_TILES = N_TOK // TM
CTX_TILES = N_CTX // TM
SEQ_PER_TILE = TM // SEQ
CTX_SEQ_PER_STEP = 1
COND_ROWS = 8
TN_IN = 1024
N_IN_TILES = IN_COLS // TN_IN
ROW_CHUNK = 512
FF_TM = 512
FF_HALVES = 2
FF_HALF = D_FF // FF_HALVES
STAGE_W_ROWS = 1024
STAGE_W_COLS = 512
TN_ADA = 1536
TG = 512
ST = 256
SUB_PER_TILE = TM // ST
N_SUB = N_TOK // ST
CTX_SUB = N_CTX // ST
SEG_ALIGN = 8
STAGE_ROWS = 640
STAGE_CHUNKS = STAGE_ROWS // SEG_ALIGN
WAIT_PIECES = (64, 32, 16, 8, 4, 2, 1)
COMBINE_SLOTS = 4
COMBINE_AHEAD = 2
TILES_PER_STEP = 2
_MAX_SORTED_ROWS = 2 * N_TOK + N_SUB * N_EXPERTS * (SEG_ALIGN - 1) + N_EXPERTS * (TG - SEG_ALIGN)
NT_G = -(-_MAX_SORTED_ROWS // (TG * TILES_PER_STEP)) * TILES_PER_STEP
R_PAD = NT_G * TG
LANES = 128
VMEM_LIMIT = 60 * 1024 * 1024

BF = jnp.bfloat16
F32 = jnp.float32


def _params(sem, vmem=VMEM_LIMIT):
    return pltpu.CompilerParams(dimension_semantics=sem, vmem_limit_bytes=vmem)


def _mod_row(i):
    return jnp.where(i < CTX_TILES, 0, i - (CTX_TILES - 1))


WHOLE = pl.BlockSpec(memory_space=pltpu.VMEM)


def _mod(mod_ref, layer, i, c):
    return mod_ref[layer, pl.ds(_mod_row(i), 1), c * D_MODEL:(c + 1) * D_MODEL]


def _stream_specs(pair, width=D_MODEL):
    a = pl.BlockSpec((TM, width), lambda i, *_: (jnp.minimum(i, CTX_TILES - 1), 0))
    if pair:
        b = pl.BlockSpec((TM, width), lambda i, *_: (jnp.maximum(i - CTX_TILES, 0), 0))
    else:
        b = pl.BlockSpec((TM, width), lambda i, *_: (jnp.maximum(i, CTX_TILES), 0))
    return a, b


def _silu(x):
    return x / (1.0 + jnp.exp(-x))


def _rms(x):
    return x * lax.rsqrt(jnp.mean(x * x, axis=-1, keepdims=True) + NORM_EPS)


def _ada_kernel(c_ref, w_ref, b_ref, o_ref):
    s = _silu(c_ref[...]).astype(BF)
    bias = b_ref[pl.ds(pl.program_id(0), 1), :]
    o_ref[...] = jnp.dot(s, w_ref[...].astype(BF), preferred_element_type=F32) + bias


def _ada(cond, w_ada, b_ada):
    n = N_MOD * D_MODEL
    return pl.pallas_call(
        _ada_kernel,
        grid=(DEPTH, n // TN_ADA),
        in_specs=[
            pl.BlockSpec((COND_ROWS, D_MODEL), lambda l, j: (0, 0)),
            pl.BlockSpec((None, D_MODEL, TN_ADA), lambda l, j: (l, 0, j)),
            pl.BlockSpec((DEPTH, TN_ADA), lambda l, j: (0, j)),
        ],
        out_specs=pl.BlockSpec((None, COND_ROWS, TN_ADA), lambda l, j: (l, 0, j)),
        out_shape=jax.ShapeDtypeStruct((DEPTH, COND_ROWS, n), F32),
        compiler_params=_params(("arbitrary", "arbitrary")),
        name="ada_mod",
    )(cond, w_ada, b_ada)


def _in_kernel(*refs, layer, aliased):
    xa_ref, xb_ref, mod_ref, g_ref, w_ref, cos_ref, sin_ref, cw_ref = refs[:8]
    refs = refs[10:] if aliased else refs[8:]
    q_ref, v_ref, kt_ref, conv_ref, nk_ref, nv_ref, h_s, wb_s, gb_s, gc_s = refs
    i = pl.program_id(0)
    j = pl.program_id(1)
    lat = i >= CTX_TILES
    ctx = jnp.logical_not(lat)

    @pl.when(i == 0)
    def _():
        wb_s[j] = w_ref[...].astype(BF)

    chunks = [slice(c * ROW_CHUNK, (c + 1) * ROW_CHUNK) for c in range(TM // ROW_CHUNK)]
    seqs_per_chunk = ROW_CHUNK // SEQ

    def norm(rows):
        gain = g_ref[layer:layer + 1, :] * (1.0 + _mod(mod_ref, layer, i, 1))
        x = jnp.where(lat, xb_ref[rows, :], xa_ref[rows, :])
        h_s[rows, :] = (_rms(x) * gain + _mod(mod_ref, layer, i, 0)).astype(BF)

    def proj(rows):
        acc = jnp.dot(h_s[rows, :], wb_s[j], preferred_element_type=F32)
        return acc[:, :ATT_WIDTH], acc[:, ATT_WIDTH:]

    def roped(a, rows):
        cos = jnp.concatenate([cos_ref[rows, :]] * N_HEADS, axis=1)
        sin = jnp.concatenate([sin_ref[rows, :]] * N_HEADS, axis=1)
        lane = lax.broadcasted_iota(jnp.int32, a.shape, 1)
        upper = (lane & (QK_DIM // 4)) != 0
        partner = jnp.where(upper, pltpu.roll(a, QK_DIM // 4, 1), pltpu.roll(a, ATT_WIDTH - QK_DIM // 4, 1))
        return a * cos + partner * sin

    @pl.when(jnp.logical_and(j == 0, lat))
    def _():
        for rows in chunks:
            norm(rows)
            q, k = proj(rows)
            q_ref[rows, :] = (roped(q, rows) * Q_SCALE).astype(BF)
            kt_ref[:, rows] = roped(k, rows).T.astype(BF)

    @pl.when(jnp.logical_and(j == 0, ctx))
    def _():
        for c, rows in enumerate(chunks):
            norm(rows)
            q, k = proj(rows)
            q_ref[rows, :] = (q * Q_SCALE).astype(BF)
            kt = k.T
            kt_ref[:, rows] = kt.astype(BF)
            for s in range(seqs_per_chunk):
                nk_ref[c * seqs_per_chunk + s] = kt[:, s * SEQ:(s + 1) * SEQ]

    @pl.when(jnp.logical_and(j == 1, lat))
    def _():
        for rows in chunks:
            v, gb = proj(rows)
            v_ref[rows, :] = v.astype(BF)
            gb_s[rows, :] = gb

    @pl.when(jnp.logical_and(j == 1, ctx))
    def _():
        for c, rows in enumerate(chunks):
            v, gb = proj(rows)
            v_ref[rows, :] = v.astype(BF)
            gb_s[rows, :] = gb
            for s in range(seqs_per_chunk):
                for h in range(N_HEADS):
                    nv_ref[c * seqs_per_chunk + s, pl.ds(h, SEQ, stride=N_HEADS), :] = (
                        v[s * SEQ:(s + 1) * SEQ, h * V_DIM:(h + 1) * V_DIM])

    @pl.when(j == 2)
    def _():
        for rows in chunks:
            gc, xin = proj(rows)
            gc_s[rows, :] = gc * xin
        u = gc_s[...]
        seq = jnp.where(lat, DEC_SEQ, SEQ)
        pos = lax.broadcasted_iota(jnp.int32, (TM, 1), 0) & (seq - 1)
        prev = jnp.where(pos == 0, 0.0, pltpu.roll(u, 1, 0))
        nxt = jnp.where(pos == seq - 1, 0.0, pltpu.roll(u, TM - 1, 0))
        cw = cw_ref[layer]
        conv = prev * cw[0:1] + u * cw[1:2] + nxt * cw[2:3]
        conv_ref[...] = (gb_s[...] * conv).astype(BF)


def _in_proj(layer, xs, mod, g_mix, w_in, cos_t, sin_t, conv_w, new_kv):
    pair = isinstance(xs, tuple)
    xa, xb = xs if pair else (xs, xs)
    spec_a, spec_b = _stream_specs(pair)
    ctx_i = lambda i: jnp.minimum(i, CTX_TILES - 1)
    in_specs = [
        spec_a, spec_b,
        WHOLE, WHOLE,
        pl.BlockSpec((None, D_MODEL, TN_IN), lambda i, j: (layer, 0, jnp.where(i == 0, j, N_IN_TILES - 1))),
        WHOLE, WHOLE, WHOLE,
    ]
    args = [xa, xb, mod, g_mix, w_in, cos_t, sin_t, conv_w]
    aliases = {}
    if new_kv is not None:
        in_specs += [pl.BlockSpec(memory_space=pl.ANY), pl.BlockSpec(memory_space=pl.ANY)]
        args += list(new_kv)
        aliases = {8: 4, 9: 5}
    row_tile = pl.BlockSpec((TM, ATT_WIDTH), lambda i, j: (i, 0))
    return pl.pallas_call(
        functools.partial(_in_kernel, layer=layer, aliased=new_kv is not None),
        grid=(N_TILES, N_IN_TILES),
        in_specs=in_specs,
        out_specs=[
            row_tile,
            row_tile,
            pl.BlockSpec((None, ATT_WIDTH, TM), lambda i, j: (i, 0, 0)),
            row_tile,
            pl.BlockSpec((SEQ_PER_TILE, None, ATT_WIDTH, SEQ), lambda i, j: (ctx_i(i), layer, 0, 0)),
            pl.BlockSpec((SEQ_PER_TILE, None, SEQ * N_HEADS, V_DIM), lambda i, j: (ctx_i(i), layer, 0, 0)),
        ],
        out_shape=[
            jax.ShapeDtypeStruct((N_TOK, ATT_WIDTH), BF),
            jax.ShapeDtypeStruct((N_TOK, ATT_WIDTH), BF),
            jax.ShapeDtypeStruct((N_TILES, ATT_WIDTH, TM), BF),
            jax.ShapeDtypeStruct((N_TOK, CONV_WIDTH), BF),
            jax.ShapeDtypeStruct((BATCH, DEPTH, ATT_WIDTH, SEQ), F32),
            jax.ShapeDtypeStruct((BATCH, DEPTH, SEQ * N_HEADS, V_DIM), F32),
        ],
        scratch_shapes=[
            pltpu.VMEM((TM, D_MODEL), BF),
            pltpu.VMEM((N_IN_TILES, D_MODEL, TN_IN), BF),
            pltpu.VMEM((TM, CONV_WIDTH), F32),
            pltpu.VMEM((TM, CONV_WIDTH), F32),
        ],
        input_output_aliases=aliases,
        compiler_params=_params(("arbitrary", "arbitrary")),
        name=f"in_proj_l{layer}",
    )(*args)


def _lambda(lq_ref, layer, lam_init):
    lq = lq_ref[layer]
    a = jnp.exp(jnp.sum(lq[0:1] * lq[1:2], axis=-1, keepdims=True))
    b = jnp.exp(jnp.sum(lq[2:3] * lq[3:4], axis=-1, keepdims=True))
    return a - b + lam_init


def _head_norm(o, sg, lam_init):
    return _rms(o) * sg * (1.0 - lam_init)


def _attn_ctx_kernel(q_ref, kt_ref, v_ref, lq_ref, sg_ref, o_ref, sc_s, *, layer, lam_init):
    lam = _lambda(lq_ref, layer, lam_init)
    sg = sg_ref[layer:layer + 1, :]

    units = [(b, h) for b in range(CTX_SEQ_PER_STEP) for h in range(N_HEADS)]

    def scores(u):
        b, h = units[u]
        pos = slice(b * SEQ, (b + 1) * SEQ)
        for s in range(2):
            d = slice(h * V_DIM + s * QK_DIM, h * V_DIM + (s + 1) * QK_DIM)
            sc_s[u % 2, s] = jnp.dot(q_ref[pos, d], kt_ref[d, pos], preferred_element_type=F32)

    def finish(u):
        b, h = units[u]
        pos = slice(b * SEQ, (b + 1) * SEQ)
        cols = slice(h * V_DIM, (h + 1) * V_DIM)
        v = v_ref[pos, cols]
        outs = []
        for s in range(2):
            sc = sc_s[u % 2, s]
            e = jnp.exp2(sc - jnp.max(sc, axis=-1, keepdims=True))
            r = 1.0 / jnp.sum(e, axis=-1, keepdims=True)
            outs.append(jnp.dot(e.astype(BF), v, preferred_element_type=F32) * r)
        o = outs[0] - lam * outs[1]
        o_ref[pos, cols] = _head_norm(o, sg, lam_init).astype(BF)

    scores(0)
    for u in range(len(units)):
        if u + 1 < len(units):
            scores(u + 1)
        finish(u)


def _attn_ctx(layer, q, kt, v, lambda_qk, subln_g, lam_init):
    rows = CTX_SEQ_PER_STEP * SEQ
    per_tile = TM // rows
    return pl.pallas_call(
        functools.partial(_attn_ctx_kernel, layer=layer, lam_init=lam_init),
        grid=(N_CTX // rows,),
        in_specs=[
            pl.BlockSpec((rows, ATT_WIDTH), lambda b: (b, 0)),
            pl.BlockSpec((None, ATT_WIDTH, rows), lambda b: (b // per_tile, 0, b % per_tile)),
            pl.BlockSpec((rows, ATT_WIDTH), lambda b: (b, 0)),
            WHOLE, WHOLE,
        ],
        out_specs=pl.BlockSpec((rows, ATT_WIDTH), lambda b: (b, 0)),
        out_shape=jax.ShapeDtypeStruct((N_TOK, ATT_WIDTH), BF),
        scratch_shapes=[pltpu.VMEM((2, 2, SEQ, SEQ), F32)],
        compiler_params=_params(("arbitrary",)),
        name=f"attn_ctx_l{layer}",
    )(q, kt, v, lambda_qk, subln_g)


TQ = 256
LAT_Q_PER_STEP = 4


def _attn_lat_kernel(q_ref, kt_ref, v_ref, ckt_ref, cv_ref, lq_ref, sg_ref, att_in_ref, o_ref, sc_s, *,
                     layer, lam_init):
    del att_in_ref
    lam = _lambda(lq_ref, layer, lam_init)
    sg = sg_ref[layer:layer + 1, :]

    units = [(b, h) for b in range(LAT_Q_PER_STEP) for h in range(N_HEADS)]

    def scores(u, s):
        b, h = units[u]
        d = slice(h * V_DIM + s * QK_DIM, h * V_DIM + (s + 1) * QK_DIM)
        q = q_ref[b * TQ:(b + 1) * TQ, d]
        sc_s[u % 2, s, :, :PAST_LEN] = jnp.dot(q, ckt_ref[d, :].astype(BF), preferred_element_type=F32)
        sc_s[u % 2, s, :, PAST_LEN:] = jnp.dot(q, kt_ref[d, :], preferred_element_type=F32)

    def softmax(u, s):
        sc = sc_s[u % 2, s]
        e = jnp.exp2(sc - jnp.max(sc, axis=-1, keepdims=True))
        return e, 1.0 / jnp.sum(e, axis=-1, keepdims=True)

    def finish(u, p1, p2):
        b, h = units[u]
        cols = slice(h * V_DIM, (h + 1) * V_DIM)
        e = jnp.concatenate([p1[0].astype(BF), p2[0].astype(BF)], axis=0)
        vc = cv_ref[pl.ds(h, PAST_LEN, stride=N_HEADS), :].astype(BF)
        pv = jnp.dot(e[:, :PAST_LEN], vc, preferred_element_type=F32)
        pv = pv + jnp.dot(e[:, PAST_LEN:], v_ref[:, cols], preferred_element_type=F32)
        o = pv[:TQ] * p1[1] - pv[TQ:] * (lam * p2[1])
        o_ref[b * TQ:(b + 1) * TQ, cols] = _head_norm(o, sg, lam_init).astype(BF)

    scores(0, 0)
    scores(0, 1)
    for u in range(len(units)):
        more = u + 1 < len(units)
        if more:
            scores(u + 1, 0)
        p1 = softmax(u, 0)
        if more:
            scores(u + 1, 1)
        finish(u, p1, softmax(u, 1))


def _attn_lat(layer, q, kt, v, cache_kt, cache_v, lambda_qk, subln_g, att, lam_init):
    rows = LAT_Q_PER_STEP * TQ
    nqb = DEC_SEQ // rows
    q0 = N_CTX // rows
    return pl.pallas_call(
        functools.partial(_attn_lat_kernel, layer=layer, lam_init=lam_init),
        grid=(DEC_BATCH, nqb),
        in_specs=[
            pl.BlockSpec((rows, ATT_WIDTH), lambda b, t: (q0 + b * nqb + t, 0)),
            pl.BlockSpec((None, ATT_WIDTH, DEC_SEQ), lambda b, t: (CTX_TILES + b, 0, 0)),
            pl.BlockSpec((DEC_SEQ, ATT_WIDTH), lambda b, t: (CTX_TILES + b, 0)),
            pl.BlockSpec((None, None, ATT_WIDTH, PAST_LEN), lambda b, t: (b, layer, 0, 0)),
            pl.BlockSpec((None, None, PAST_LEN * N_HEADS, V_DIM), lambda b, t: (b, layer, 0, 0)),
            WHOLE, WHOLE,
            pl.BlockSpec(memory_space=pl.ANY),
        ],
        out_specs=pl.BlockSpec((rows, ATT_WIDTH), lambda b, t: (q0 + b * nqb + t, 0)),
        out_shape=jax.ShapeDtypeStruct((N_TOK, ATT_WIDTH), BF),
        scratch_shapes=[pltpu.VMEM((2, 2, TQ, PAST_LEN + DEC_SEQ), F32)],
        input_output_aliases={7: 0},
        compiler_params=_params(("arbitrary", "arbitrary")),
        name=f"attn_lat_l{layer}",
    )(q, kt, v, cache_kt, cache_v, lambda_qk, subln_g, att)


def _out_kernel(att_ref, conv_ref, w_ref, xa_ref, xb_ref, mod_ref, gf_ref, *rest, layer, route):
    if route:
        wr_ref, xo_ref, h2_ref, lp_ref, rg_ref, n_ref, wb_s = rest
    else:
        xo_ref, h2_ref, wb_s = rest
    i = pl.program_id(0)

    @pl.when(i == 0)
    def _():
        wb_s[...] = w_ref[...].astype(BF)

    lat = i >= CTX_TILES
    gain = gf_ref[layer:layer + 1, :] * (1.0 + _mod(mod_ref, layer, i, 4))
    for c in range(TM // ROW_CHUNK):
        rows = slice(c * ROW_CHUNK, (c + 1) * ROW_CHUNK)
        mo = jnp.dot(att_ref[rows, :], wb_s[:ATT_WIDTH, :], preferred_element_type=F32)
        mo = mo + jnp.dot(conv_ref[rows, :], wb_s[ATT_WIDTH:, :], preferred_element_type=F32)
        xn = jnp.where(lat, xb_ref[rows, :], xa_ref[rows, :]) + _mod(mod_ref, layer, i, 2) * mo
        xo_ref[rows, :] = xn
        h2_ref[rows, :] = (_rms(xn) * gain + _mod(mod_ref, layer, i, 3)).astype(BF)
    if route:
        _route(h2_ref[...], wr_ref, lp_ref, rg_ref, n_ref)


def _out_proj(layer, att, conv, w_out, xs, mod, g_ffn, w_router_pad=None):
    pair = isinstance(xs, tuple)
    xa, xb = xs if pair else (xs, xs)
    spec_a, spec_b = _stream_specs(pair)
    row_spec = pl.BlockSpec((TM, D_MODEL), lambda i: (i, 0))
    lane_spec = pl.BlockSpec((TM, LANES), lambda i: (i, 0))
    in_specs = [
        pl.BlockSpec((TM, ATT_WIDTH), lambda i: (i, 0)),
        pl.BlockSpec((TM, CONV_WIDTH), lambda i: (i, 0)),
        pl.BlockSpec((None, D_MODEL, D_MODEL), lambda i: (layer, 0, 0)),
        spec_a, spec_b,
        WHOLE, WHOLE,
    ]
    args = [att, conv, w_out, xa, xb, mod, g_ffn]
    out_specs = [row_spec, row_spec]
    out_shape = [jax.ShapeDtypeStruct((N_TOK, D_MODEL), F32), jax.ShapeDtypeStruct((N_TOK, D_MODEL), BF)]
    route = w_router_pad is not None
    if route:
        in_specs.append(WHOLE)
        args.append(w_router_pad)
        out_specs += [lane_spec, lane_spec, pl.BlockSpec((None, 8, LANES), lambda i: (i, 0, 0))]
        out_shape += [jax.ShapeDtypeStruct((N_TOK, LANES), jnp.int32), jax.ShapeDtypeStruct((N_TOK, LANES), F32),
                      jax.ShapeDtypeStruct((N_TILES, 8, LANES), jnp.int32)]
    return pl.pallas_call(
        functools.partial(_out_kernel, layer=layer, route=route),
        grid=(N_TILES,),
        in_specs=in_specs,
        out_specs=out_specs,
        out_shape=out_shape,
        scratch_shapes=[pltpu.VMEM((D_MODEL, D_MODEL), BF)],
        compiler_params=_params(("arbitrary",)),
        name=f"out_proj_l{layer}",
    )(*args)


def _weight_pieces(layer):
    pieces = []
    for c0 in range(0, D_MODEL, STAGE_W_COLS):
        cols = slice(c0, c0 + STAGE_W_COLS)
        pieces.append((0, layer, slice(0, D_MODEL), cols, None, slice(0, D_MODEL), cols))
    for half in range(FF_HALVES):
        for part in range(2):
            src0 = part * D_FF + half * FF_HALF
            for off in range(0, FF_HALF, STAGE_W_COLS):
                n = min(STAGE_W_COLS, FF_HALF - off)
                pieces.append((1, 0, slice(0, D_MODEL), slice(src0 + off, src0 + off + n),
                               half, slice(0, D_MODEL), slice(part * FF_HALF + off, part * FF_HALF + off + n)))
    for r0 in range(0, D_FF, STAGE_W_ROWS):
        rows = slice(r0, min(r0 + STAGE_W_ROWS, D_FF))
        for c0 in range(0, D_MODEL, STAGE_W_COLS):
            cols = slice(c0, c0 + STAGE_W_COLS)
            pieces.append((2, 0, rows, cols, None, rows, cols))
    return pieces


def _load_dense_weights(layer, hbm, resident, stage_s, sem):
    pieces = _weight_pieces(layer)

    def copy(k):
        src, idx, rows, cols, _, _, _ = pieces[k]
        nr, nc = rows.stop - rows.start, cols.stop - cols.start
        return pltpu.make_async_copy(hbm[src].at[idx, rows, cols], stage_s.at[k % 2, :nr, :nc], sem.at[k % 2])

    copy(0).start()
    for k, (src, _, rows, cols, didx, drows, dcols) in enumerate(pieces):
        if k + 1 < len(pieces):
            copy(k + 1).start()
        copy(k).wait()
        nr, nc = rows.stop - rows.start, cols.stop - cols.start
        piece = stage_s[k % 2, :nr, :nc].astype(BF)
        if didx is None:
            resident[src][drows, dcols] = piece
        else:
            resident[src][didx, drows, dcols] = piece


def _dense_layer_kernel(att_ref, conv_ref, xa_ref, xb_ref, mod_ref, gf_ref, wo_hbm, wgu_hbm, wd_hbm,
                        o_ref, wo_b, wgu_b, wd_b, stage_s, sem, *, layer):
    i = pl.program_id(0)
    tile = i // (TM // FF_TM)

    @pl.when(i == 0)
    def _():
        _load_dense_weights(layer, (wo_hbm, wgu_hbm, wd_hbm), (wo_b, wgu_b, wd_b), stage_s, sem)

    mo = jnp.dot(att_ref[...], wo_b[:ATT_WIDTH, :], preferred_element_type=F32)
    mo = mo + jnp.dot(conv_ref[...], wo_b[ATT_WIDTH:, :], preferred_element_type=F32)
    x = jnp.where(tile >= CTX_TILES, xb_ref[...], xa_ref[...])
    xn = x + _mod(mod_ref, layer, tile, 2) * mo
    h = (_rms(xn) * gf_ref[layer:layer + 1, :]) * (1.0 + _mod(mod_ref, layer, tile, 4)) + _mod(mod_ref, layer, tile, 3)
    h = h.astype(BF)
    y = None
    for half in range(FF_HALVES):
        gu = jnp.dot(h, wgu_b[half], preferred_element_type=F32)
        act = (_silu(gu[:, :FF_HALF]) * gu[:, FF_HALF:]).astype(BF)
        part = jnp.dot(act, wd_b[half * FF_HALF:(half + 1) * FF_HALF, :], preferred_element_type=F32)
        y = part if y is None else y + part
    o_ref[...] = xn + _mod(mod_ref, layer, tile, 5) * y


def _dense_layer(layer, att, conv, xs, mod, g_ffn, w_out, w_gu, w_down):
    xa, xb = xs
    n_ctx = N_CTX // FF_TM
    row = lambda width: pl.BlockSpec((FF_TM, width), lambda i: (i, 0))
    hbm = pl.BlockSpec(memory_space=pl.ANY)
    return pl.pallas_call(
        functools.partial(_dense_layer_kernel, layer=layer),
        grid=(N_TOK // FF_TM,),
        in_specs=[
            row(ATT_WIDTH), row(CONV_WIDTH),
            pl.BlockSpec((FF_TM, D_MODEL), lambda i: (jnp.minimum(i, n_ctx - 1), 0)),
            pl.BlockSpec((FF_TM, D_MODEL), lambda i: (jnp.maximum(i - n_ctx, 0), 0)),
            WHOLE, WHOLE, hbm, hbm, hbm,
        ],
        out_specs=row(D_MODEL),
        out_shape=jax.ShapeDtypeStruct((N_TOK, D_MODEL), F32),
        scratch_shapes=[
            pltpu.VMEM((D_MODEL, D_MODEL), BF),
            pltpu.VMEM((FF_HALVES, D_MODEL, 2 * FF_HALF), BF),
            pltpu.VMEM((D_FF, D_MODEL), BF),
            pltpu.VMEM((2, STAGE_W_ROWS, STAGE_W_COLS), F32),
            pltpu.SemaphoreType.DMA((2,)),
        ],
        compiler_params=_params(("arbitrary",)),
        name="dense_layer",
    )(att, conv, xa, xb, mod, g_ffn, w_out, w_gu, w_down)


def _route(h, wr_ref, lp_ref, rg_ref, n_ref):
    logits = jnp.dot(h, wr_ref[...].astype(BF), preferred_element_type=F32)
    lane = lax.broadcasted_iota(jnp.int32, logits.shape, 1)
    lg = jnp.where(lane < N_EXPERTS, logits, -jnp.inf)
    m1 = jnp.max(lg, axis=-1, keepdims=True)
    i1 = jnp.min(jnp.where(lg == m1, lane, LANES), axis=-1, keepdims=True)
    lg2 = jnp.where(lane == i1, -jnp.inf, lg)
    m2 = jnp.max(lg2, axis=-1, keepdims=True)
    i2 = jnp.min(jnp.where(lg2 == m2, lane, LANES), axis=-1, keepdims=True)
    e2 = jnp.exp(m2 - m1)
    w1 = 1.0 / (1.0 + e2)
    w2 = e2 / (1.0 + e2)

    sel1 = lane == i1
    sel2 = lane == i2
    onehot = jnp.logical_or(sel1, sel2)
    rows = lax.broadcasted_iota(jnp.int32, (ST, ST), 0)
    colsi = lax.broadcasted_iota(jnp.int32, (ST, ST), 1)
    earlier = (colsi < rows).astype(BF)
    onehot_b = onehot.astype(BF)
    before = jnp.concatenate(
        [jnp.dot(earlier, onehot_b[s * ST:(s + 1) * ST], preferred_element_type=F32) for s in range(SUB_PER_TILE)],
        axis=0)
    onehot_f = onehot.astype(F32)
    counts = [jnp.sum(onehot_f[s * ST:(s + 1) * ST], axis=0, keepdims=True) for s in range(SUB_PER_TILE)]
    counts = jnp.concatenate(counts + [jnp.zeros((8 - SUB_PER_TILE, LANES), F32)], axis=0).astype(jnp.int32)
    seg_len = ((counts + (SEG_ALIGN - 1)) // SEG_ALIGN) * SEG_ALIGN
    n_ref[...] = seg_len
    la = lax.broadcasted_iota(jnp.int32, (LANES, LANES), 0)
    lb = lax.broadcasted_iota(jnp.int32, (LANES, LANES), 1)
    seg_start = jnp.dot(seg_len.astype(F32).astype(BF), (la < lb).astype(BF), preferred_element_type=F32)
    start = jnp.concatenate(
        [jnp.broadcast_to(seg_start[s:s + 1], (ST, LANES)) for s in range(SUB_PER_TILE)], axis=0)
    where = before + start
    lp1 = jnp.sum(jnp.where(sel1, where, 0.0), axis=-1, keepdims=True).astype(jnp.int32)
    lp2 = jnp.sum(jnp.where(sel2, where, 0.0), axis=-1, keepdims=True).astype(jnp.int32)
    lp_ref[...] = jnp.where(lane == 0, lp1, jnp.where(lane == 1, lp2, 0))
    rg_ref[...] = jnp.where(lane == 0, w1, jnp.where(lane == 1, w2, 0.0))


def _chunk_copies(s, cnt_ref, cdst_ref, stage, rows_hbm, sem, *, to_hbm, wait):
    def copy(v, h):
        return pltpu.make_async_copy(v, h, sem) if to_hbm else pltpu.make_async_copy(h, v, sem)

    if wait:
        for z in WAIT_PIECES:
            @pl.when((cnt_ref[s] & z) != 0)
            def _():
                copy(stage.at[pl.ds(0, z * SEG_ALIGN)], rows_hbm.at[pl.ds(0, z * SEG_ALIGN)]).wait()
        return

    def start(c, priority):
        v = stage.at[pl.ds(pl.multiple_of(c * SEG_ALIGN, SEG_ALIGN), SEG_ALIGN)]
        h = rows_hbm.at[pl.ds(pl.multiple_of(cdst_ref[s * STAGE_CHUNKS + c], SEG_ALIGN), SEG_ALIGN)]
        copy(v, h).start(priority=priority)

    def pair(p, carry):
        start(2 * p, 0)
        start(2 * p + 1, 1)
        return carry

    n = cnt_ref[s]
    lax.fori_loop(0, n // 2, pair, 0)

    @pl.when((n & 1) != 0)
    def _():
        start(n - 1, 0)


def _dispatch_kernel(cnt_ref, cdst_ref, h_ref, lp_ref, xs_ref, stage_s, sem):
    copies = functools.partial(_chunk_copies, cnt_ref=cnt_ref, cdst_ref=cdst_ref, rows_hbm=xs_ref, to_hbm=True)
    for k in range(SUB_PER_TILE):
        s = pl.program_id(0) * SUB_PER_TILE + k
        slot = k % 2
        rows = slice(k * ST, (k + 1) * ST)

        @pl.when(s >= 2)
        def _():
            copies(s - 2, stage=stage_s.at[slot], sem=sem.at[slot], wait=True)

        lpt = lp_ref[rows, :].T
        r = lax.broadcasted_iota(jnp.int32, (STAGE_ROWS, ST), 0)
        pick = jnp.logical_or(r == lpt[0:1, :], r == lpt[1:2, :]).astype(BF)
        stage_s[slot] = jnp.dot(pick, h_ref[rows, :], preferred_element_type=F32).astype(BF)
        copies(s, stage=stage_s.at[slot], sem=sem.at[slot], wait=False)

    @pl.when(pl.program_id(0) == N_TILES - 1)
    def _():
        copies(N_SUB - 2, stage=stage_s.at[0], sem=sem.at[0], wait=True)
        copies(N_SUB - 1, stage=stage_s.at[1], sem=sem.at[1], wait=True)


def _dispatch(cnt, cdst, h2, lp):
    assert SUB_PER_TILE % 2 == 0
    grid_spec = pltpu.PrefetchScalarGridSpec(
        num_scalar_prefetch=2,
        grid=(N_TILES,),
        in_specs=[
            pl.BlockSpec((TM, D_MODEL), lambda i, *_: (i, 0)),
            pl.BlockSpec((TM, LANES), lambda i, *_: (i, 0)),
        ],
        out_specs=pl.BlockSpec(memory_space=pl.ANY),
        scratch_shapes=[pltpu.VMEM((2, STAGE_ROWS, D_MODEL), BF), pltpu.SemaphoreType.DMA((2,))],
    )
    return pl.pallas_call(
        _dispatch_kernel,
        grid_spec=grid_spec,
        out_shape=jax.ShapeDtypeStruct((R_PAD, D_MODEL), BF),
        compiler_params=_params(("arbitrary",)),
        name="moe_dispatch",
    )(cnt, cdst, h2, lp)


def _expert_weights(r, te_ref, nt_ref, nxt_ref, w_hbm, wf_s, wb_s, sem):
    def fetch(e):
        return pltpu.make_async_copy(w_hbm.at[0, e], wf_s, sem)

    @pl.when(r == 0)
    def _():
        fetch(te_ref[0]).start()

    first = jnp.logical_or(r == 0, te_ref[r] != te_ref[jnp.maximum(r - 1, 0)])

    @pl.when(jnp.logical_and(r < nt_ref[0], first))
    def _():
        fetch(te_ref[r]).wait()
        wb_s[...] = wf_s[...].astype(BF)

        @pl.when(nxt_ref[r] >= 0)
        def _():
            fetch(nxt_ref[r]).start()


def _moe_ffn_kernel(te_ref, nt_ref, nxt_ref, used_ref, x_ref, wgu_hbm, wd_hbm, o_ref,
                    wgu_f, wgu_b, wd_f, wd_b, sem):
    def ffn(rows):
        gu = jnp.dot(x_ref[rows, :], wgu_b[...], preferred_element_type=F32)
        act = (_silu(gu[:, :D_FF_EXPERT]) * gu[:, D_FF_EXPERT:]).astype(BF)
        o_ref[rows, :] = jnp.dot(act, wd_b[...], preferred_element_type=F32).astype(BF)

    for k in range(TILES_PER_STEP):
        r = pl.program_id(0) * TILES_PER_STEP + k
        _expert_weights(r, te_ref, nt_ref, nxt_ref, wgu_hbm, wgu_f, wgu_b, sem.at[0])
        _expert_weights(r, te_ref, nt_ref, nxt_ref, wd_hbm, wd_f, wd_b, sem.at[1])

        @pl.when(jnp.logical_and(r < nt_ref[0], used_ref[r] > TG // 2))
        def _():
            ffn(slice(k * TG, (k + 1) * TG))

        @pl.when(jnp.logical_and(r < nt_ref[0], used_ref[r] <= TG // 2))
        def _():
            ffn(slice(k * TG, k * TG + TG // 2))


def _moe_ffn(te, nt, nxt, used, rows, w_gu, w_down):
    tile_map = lambda i, te, nt, nxt, used: (jnp.minimum(i, (nt[0] - 1) // TILES_PER_STEP), 0)
    grid_spec = pltpu.PrefetchScalarGridSpec(
        num_scalar_prefetch=4,
        grid=(NT_G // TILES_PER_STEP,),
        in_specs=[pl.BlockSpec((TILES_PER_STEP * TG, D_MODEL), tile_map),
                  pl.BlockSpec(memory_space=pl.ANY), pl.BlockSpec(memory_space=pl.ANY)],
        out_specs=pl.BlockSpec((TILES_PER_STEP * TG, D_MODEL), tile_map),
        scratch_shapes=[
            pltpu.VMEM((D_MODEL, 2 * D_FF_EXPERT), F32), pltpu.VMEM((D_MODEL, 2 * D_FF_EXPERT), BF),
            pltpu.VMEM((D_FF_EXPERT, D_MODEL), F32), pltpu.VMEM((D_FF_EXPERT, D_MODEL), BF),
            pltpu.SemaphoreType.DMA((2,)),
        ],
    )
    return pl.pallas_call(
        _moe_ffn_kernel,
        grid_spec=grid_spec,
        out_shape=jax.ShapeDtypeStruct((R_PAD, D_MODEL), BF),
        compiler_params=_params(("arbitrary",)),
        name="moe_ffn",
    )(te, nt, nxt, used, rows, w_gu, w_down)


def _combine_kernel(cnt_ref, cdst_ref, ys_ref, lp_ref, rg_ref, x_ref, mod_ref, fg_ref,
                    oa_ref, ob_ref, stage_s, sem, *, layer):
    i = pl.program_id(0)
    copies = functools.partial(_chunk_copies, cnt_ref=cnt_ref, cdst_ref=cdst_ref, rows_hbm=ys_ref, to_hbm=False)

    @pl.when(i == 0)
    def _():
        stage_s[...] = jnp.zeros_like(stage_s)
        for s in range(COMBINE_AHEAD):
            copies(s, stage=stage_s.at[s], sem=sem.at[s], wait=False)

    for k in range(SUB_PER_TILE):
        s = i * SUB_PER_TILE + k
        slot = k % COMBINE_SLOTS
        ahead = (k + COMBINE_AHEAD) % COMBINE_SLOTS
        rows = slice(k * ST, (k + 1) * ST)

        @pl.when(s + COMBINE_AHEAD < N_SUB)
        def _():
            copies(s + COMBINE_AHEAD, stage=stage_s.at[ahead], sem=sem.at[ahead], wait=False)

        copies(s, stage=stage_s.at[slot], sem=sem.at[slot], wait=True)

        lp = lp_ref[rows, :]
        r = lax.broadcasted_iota(jnp.int32, (ST, STAGE_ROWS), 1)
        staged = stage_s[slot]
        a = jnp.dot((r == lp[:, 0:1]).astype(BF), staged, preferred_element_type=F32)
        b = jnp.dot((r == lp[:, 1:2]).astype(BF), staged, preferred_element_type=F32)
        rg = rg_ref[rows, :]
        y = rg[:, 0:1] * a + rg[:, 1:2] * b
        xn = x_ref[rows, :] + _mod(mod_ref, layer, i, 5) * y
        out = _rms(xn) * fg_ref[...]

        @pl.when(i < CTX_TILES)
        def _():
            oa_ref[rows, :] = out

        @pl.when(i >= CTX_TILES)
        def _():
            ob_ref[rows, :] = out


def _combine(layer, cnt, cdst, ys, lp, rg, x, mod, final_g):
    assert SUB_PER_TILE % COMBINE_SLOTS == 0
    grid_spec = pltpu.PrefetchScalarGridSpec(
        num_scalar_prefetch=2,
        grid=(N_TILES,),
        in_specs=[
            pl.BlockSpec(memory_space=pl.ANY),
            pl.BlockSpec((TM, LANES), lambda i, *_: (i, 0)),
            pl.BlockSpec((TM, LANES), lambda i, *_: (i, 0)),
            pl.BlockSpec((TM, D_MODEL), lambda i, *_: (i, 0)),
            WHOLE, WHOLE,
        ],
        out_specs=[
            pl.BlockSpec((TM, D_MODEL), lambda i, *_: (jnp.minimum(i, CTX_TILES - 1), 0)),
            pl.BlockSpec((TM, D_MODEL), lambda i, *_: (jnp.maximum(i - CTX_TILES, 0), 0)),
        ],
        scratch_shapes=[pltpu.VMEM((COMBINE_SLOTS, STAGE_ROWS, D_MODEL), BF),
                        pltpu.SemaphoreType.DMA((COMBINE_SLOTS,))],
    )
    return pl.pallas_call(
        functools.partial(_combine_kernel, layer=layer),
        grid_spec=grid_spec,
        out_shape=[
            jax.ShapeDtypeStruct((N_CTX, D_MODEL), F32),
            jax.ShapeDtypeStruct((N_LAT, D_MODEL), F32),
        ],
        compiler_params=_params(("arbitrary",)),
        name="moe_combine",
    )(cnt, cdst, ys, lp, rg, x, mod, final_g)


def _group_layout(n_tiles):
    n = n_tiles[:, :SUB_PER_TILE, :N_EXPERTS].reshape(N_SUB, N_EXPERTS)
    tiles = (jnp.sum(n, axis=0) + TG - 1) // TG
    tile_end = jnp.cumsum(tiles)
    region = (tile_end - tiles) * TG
    dst = region[None, :] + jnp.cumsum(n, axis=0) - n
    seg_end = jnp.cumsum(n, axis=1)
    seg = seg_end - n
    row = jnp.arange(STAGE_CHUNKS, dtype=jnp.int32) * SEG_ALIGN
    owner = jnp.sum((row[None, :, None] >= seg_end[:, None, :]).astype(jnp.int32), axis=-1)
    own = jnp.minimum(owner, N_EXPERTS - 1)[..., None] == jnp.arange(N_EXPERTS)
    cdst = jnp.sum(jnp.where(own, (dst - seg)[:, None, :], 0), axis=-1) + row[None, :]
    cnt = seg_end[:, -1] // SEG_ALIGN
    nt = tile_end[-1]
    tile_id = jnp.minimum(jnp.arange(NT_G, dtype=jnp.int32), nt - 1)
    te = jnp.sum((tile_id[:, None] >= tile_end[None, :]).astype(jnp.int32), axis=-1)
    after = jnp.sum(jnp.where(te[:, None] == jnp.arange(N_EXPERTS), tile_end[None, :], 0), axis=-1)
    nxt = jnp.where(after < nt, jnp.sum((after[:, None] >= tile_end[None, :]).astype(jnp.int32), axis=-1), -1)
    mine = te[:, None] == jnp.arange(N_EXPERTS)
    region_end = jnp.sum(jnp.where(mine, (region + jnp.sum(n, axis=0))[None, :], 0), axis=-1)
    used = jnp.clip(region_end - tile_id * TG, 0, TG)
    i32 = lambda a: a.astype(jnp.int32)
    return (i32(cnt), i32(cdst.reshape(N_SUB * STAGE_CHUNKS)), i32(te), i32(nt.reshape(1)), i32(nxt), i32(used))


def _rope_tables():
    p = np.arange(DEC_SEQ)
    row = (p // GRID_W).astype(np.float32)
    col = (p % GRID_W).astype(np.float32)
    half = QK_DIM // 4
    freqs = (ROPE_BASE ** (-np.arange(half, dtype=np.float32) / half)).astype(np.float32)
    lane = np.arange(V_DIM)
    f = freqs[lane & (half - 1)]
    use_col = (lane & (2 * half)) != 0
    ang = (np.where(use_col[None, :], col[:, None], row[:, None]) * f[None, :]).astype(np.float32)
    upper = (lane & half) != 0
    sin = np.sin(ang)
    return jnp.asarray(np.cos(ang), F32), jnp.asarray(np.where(upper[None, :], sin, -sin), F32)


def kernel(x_prompt, x_sample, cache_k, cache_v, c, c_ctx, w_ada, b_ada, norm_mix_g, norm_ffn_g,
           w_in, lambda_qk, subln_g, conv_w, w_out, w_gu_dense, w_down_dense, w_router,
           w_gu_moe, w_down_moe, final_g):
    assert DEPTH == 2
    xs = (x_prompt.reshape(N_CTX, D_MODEL), x_sample.reshape(N_LAT, D_MODEL))
    cond = jnp.concatenate([c_ctx[None, :], c, jnp.zeros((COND_ROWS - 1 - DEC_BATCH, D_MODEL), F32)], axis=0)
    mod = _ada(cond, w_ada, b_ada)
    cos_t, sin_t = _rope_tables()
    cache_kt = jnp.transpose(cache_k, (0, 1, 3, 4, 5, 2)).reshape(DEC_BATCH, DEPTH, ATT_WIDTH, PAST_LEN)
    cache_v4 = cache_v.reshape(DEC_BATCH, DEPTH, PAST_LEN * N_HEADS, V_DIM)

    new_kv = None
    for layer in range(DEPTH):
        lam_init = 0.8 - 0.6 * math.exp(-0.3 * layer)
        q, v, kt, conv, nk, nv = _in_proj(layer, xs, mod, norm_mix_g, w_in, cos_t, sin_t, conv_w, new_kv)
        new_kv = (nk, nv)
        att = _attn_ctx(layer, q, kt, v, lambda_qk, subln_g, lam_init)
        att = _attn_lat(layer, q, kt, v, cache_kt, cache_v4, lambda_qk, subln_g, att, lam_init)
        if layer == 0:
            xs = _dense_layer(layer, att, conv, xs, mod, norm_ffn_g, w_out, w_gu_dense, w_down_dense)
        else:
            wr = jnp.pad(w_router[0], ((0, 0), (0, LANES - N_EXPERTS)))
            x1, h2, lp, rg, n_tiles = _out_proj(layer, att, conv, w_out, xs, mod, norm_ffn_g, wr)
            cnt, cdst, te, nt, nxt, used = _group_layout(n_tiles)
            xsort = _dispatch(cnt, cdst, h2, lp)
            ys = _moe_ffn(te, nt, nxt, used, xsort, w_gu_moe, w_down_moe)
            y_ctx, y_lat = _combine(layer, cnt, cdst, ys, lp, rg, x1, mod, final_g.reshape(1, D_MODEL))
    nk, nv = new_kv
    new_k = jnp.transpose(nk.reshape(BATCH, DEPTH, N_HEADS, 2, QK_DIM, SEQ), (0, 1, 5, 2, 3, 4))
    new_v = nv.reshape(BATCH, DEPTH, SEQ, N_HEADS, V_DIM)
    return (y_ctx.reshape(BATCH, SEQ, D_MODEL), y_lat.reshape(DEC_BATCH, DEC_SEQ, D_MODEL), new_k, new_v)
```

```python
import functools
import math

import numpy as np
import jax
import jax.numpy as jnp
from jax import lax
from jax.experimental import pallas as pl
from jax.experimental.pallas import tpu as pltpu

D_MODEL = 1024
BATCH = 16
SEQ = 256
DEPTH = 2
DEC_BATCH = 4
DEC_SEQ = 1024
PAST_LEN = 512
GRID_W = 64
ATT_WIDTH = 512
CONV_WIDTH = 512
N_HEADS = 4
V_DIM = 128
QK_DIM = 64
ROPE_BASE = 10000.0
D_FF = 2816
N_EXPERTS = 8
D_FF_EXPERT = 1408
N_MOD = 6
NORM_EPS = 1e-6
Q_SCALE = QK_DIM ** -0.5 * math.log2(math.e)
IN_COLS = 3 * ATT_WIDTH + 3 * CONV_WIDTH

N_CTX = BATCH * SEQ
N_LAT = DEC_BATCH * DEC_SEQ
N_TOK = N_CTX + N_LAT
TM = 1024
N_TILES = N_TOK // TM
CTX_TILES = N_CTX // TM
SEQ_PER_TILE = TM // SEQ
CTX_SEQ_PER_STEP = 1
COND_ROWS = 8
TN_IN = 1024
N_IN_TILES = IN_COLS // TN_IN
ROW_CHUNK = 512
FF_TM = 512
FF_HALVES = 2
FF_HALF = D_FF // FF_HALVES
STAGE_W_ROWS = 1024
STAGE_W_COLS = 512
TN_ADA = 1536
TG = 512
ST = 256
SUB_PER_TILE = TM // ST
N_SUB = N_TOK // ST
CTX_SUB = N_CTX // ST
SEG_ALIGN = 8
STAGE_ROWS = 640
STAGE_CHUNKS = STAGE_ROWS // SEG_ALIGN
WAIT_PIECES = (64, 32, 16, 8, 4, 2, 1)
COMBINE_SLOTS = 4
COMBINE_AHEAD = 2
TILES_PER_STEP = 2
_MAX_SORTED_ROWS = 2 * N_TOK + N_SUB * N_EXPERTS * (SEG_ALIGN - 1) + N_EXPERTS * (TG - SEG_ALIGN)
NT_G = -(-_MAX_SORTED_ROWS // (TG * TILES_PER_STEP)) * TILES_PER_STEP
R_PAD = NT_G * TG
LANES = 128
SUBLANES = 8
GATE_LANE = 0
ROW_LANE = 2
VMEM_LIMIT = 60 * 1024 * 1024

BF = jnp.bfloat16
F32 = jnp.float32


def _params(sem, vmem=VMEM_LIMIT):
    return pltpu.CompilerParams(dimension_semantics=sem, vmem_limit_bytes=vmem)


def _mod_row(i):
    return jnp.where(i < CTX_TILES, 0, i - (CTX_TILES - 1))


WHOLE = pl.BlockSpec(memory_space=pltpu.VMEM)


def _mod(mod_ref, layer, i, c):
    return mod_ref[layer, pl.ds(_mod_row(i), 1), c * D_MODEL:(c + 1) * D_MODEL]


def _stream_specs(pair, width=D_MODEL):
    a = pl.BlockSpec((TM, width), lambda i, *_: (jnp.minimum(i, CTX_TILES - 1), 0))
    if pair:
        b = pl.BlockSpec((TM, width), lambda i, *_: (jnp.maximum(i - CTX_TILES, 0), 0))
    else:
        b = pl.BlockSpec((TM, width), lambda i, *_: (jnp.maximum(i, CTX_TILES), 0))
    return a, b


def _silu(x):
    return x / (1.0 + jnp.exp(-x))


def _rms(x):
    return x * lax.rsqrt(jnp.mean(x * x, axis=-1, keepdims=True) + NORM_EPS)


def _ada_kernel(cc_ref, c_ref, w_ref, b_ref, o_ref):
    pad = jnp.zeros((COND_ROWS - 1 - DEC_BATCH, D_MODEL), F32)
    cond = jnp.concatenate([cc_ref[...], c_ref[...], pad], axis=0)
    s = _silu(cond).astype(BF)
    bias = b_ref[pl.ds(pl.program_id(0), 1), :]
    o_ref[...] = jnp.dot(s, w_ref[...].astype(BF), preferred_element_type=F32) + bias


def _ada(c_ctx, c, w_ada, b_ada):
    n = N_MOD * D_MODEL
    return pl.pallas_call(
        _ada_kernel,
        grid=(DEPTH, n // TN_ADA),
        in_specs=[
            WHOLE, WHOLE,
            pl.BlockSpec((None, D_MODEL, TN_ADA), lambda l, j: (l, 0, j)),
            pl.BlockSpec((DEPTH, TN_ADA), lambda l, j: (0, j)),
        ],
        out_specs=pl.BlockSpec((None, COND_ROWS, TN_ADA), lambda l, j: (l, 0, j)),
        out_shape=jax.ShapeDtypeStruct((DEPTH, COND_ROWS, n), F32),
        compiler_params=_params(("arbitrary", "arbitrary")),
        name="ada_mod",
    )(c_ctx.reshape(1, D_MODEL), c, w_ada, b_ada)


def _in_kernel(*refs, layer, aliased):
    xa_ref, xb_ref, mod_ref, g_ref, w_ref, cos_ref, sin_ref, cw_ref = refs[:8]
    refs = refs[10:] if aliased else refs[8:]
    q_ref, v_ref, kt_ref, conv_ref, nk_ref, nv_ref, h_s, wb_s, gb_s, gc_s = refs
    i = pl.program_id(0)
    j = pl.program_id(1)
    lat = i >= CTX_TILES
    ctx = jnp.logical_not(lat)

    @pl.when(i == 0)
    def _():
        wb_s[j] = w_ref[...].astype(BF)

    chunks = [slice(c * ROW_CHUNK, (c + 1) * ROW_CHUNK) for c in range(TM // ROW_CHUNK)]
    seqs_per_chunk = ROW_CHUNK // SEQ

    def norm(rows):
        gain = g_ref[layer:layer + 1, :] * (1.0 + _mod(mod_ref, layer, i, 1))
        x = jnp.where(lat, xb_ref[rows, :], xa_ref[rows, :])
        h_s[rows, :] = (_rms(x) * gain + _mod(mod_ref, layer, i, 0)).astype(BF)

    def proj(rows):
        acc = jnp.dot(h_s[rows, :], wb_s[j], preferred_element_type=F32)
        return acc[:, :ATT_WIDTH], acc[:, ATT_WIDTH:]

    def roped(a, rows):
        cos = jnp.concatenate([cos_ref[rows, :]] * N_HEADS, axis=1)
        sin = jnp.concatenate([sin_ref[rows, :]] * N_HEADS, axis=1)
        lane = lax.broadcasted_iota(jnp.int32, a.shape, 1)
        upper = (lane & (QK_DIM // 4)) != 0
        partner = jnp.where(upper, pltpu.roll(a, QK_DIM // 4, 1), pltpu.roll(a, ATT_WIDTH - QK_DIM // 4, 1))
        return a * cos + partner * sin

    @pl.when(jnp.logical_and(j == 0, lat))
    def _():
        for rows in chunks:
            norm(rows)
            q, k = proj(rows)
            q_ref[rows, :] = (roped(q, rows) * Q_SCALE).astype(BF)
            kt_ref[:, rows] = roped(k, rows).T.astype(BF)

    @pl.when(jnp.logical_and(j == 0, ctx))
    def _():
        for c, rows in enumerate(chunks):
            norm(rows)
            q, k = proj(rows)
            q_ref[rows, :] = (q * Q_SCALE).astype(BF)
            kt = k.T
            kt_ref[:, rows] = kt.astype(BF)
            for s in range(seqs_per_chunk):
                nk_ref[c * seqs_per_chunk + s] = kt[:, s * SEQ:(s + 1) * SEQ]

    @pl.when(jnp.logical_and(j == 1, lat))
    def _():
        for rows in chunks:
            v, gb = proj(rows)
            v_ref[rows, :] = v.astype(BF)
            gb_s[rows, :] = gb

    @pl.when(jnp.logical_and(j == 1, ctx))
    def _():
        for c, rows in enumerate(chunks):
            v, gb = proj(rows)
            v_ref[rows, :] = v.astype(BF)
            gb_s[rows, :] = gb
            for s in range(seqs_per_chunk):
                for h in range(N_HEADS):
                    nv_ref[c * seqs_per_chunk + s, pl.ds(h, SEQ, stride=N_HEADS), :] = (
                        v[s * SEQ:(s + 1) * SEQ, h * V_DIM:(h + 1) * V_DIM])

    @pl.when(j == 2)
    def _():
        for rows in chunks:
            gc, xin = proj(rows)
            gc_s[rows, :] = gc * xin
        u = gc_s[...]
        seq = jnp.where(lat, DEC_SEQ, SEQ)
        pos = lax.broadcasted_iota(jnp.int32, (TM, 1), 0) & (seq - 1)
        prev = jnp.where(pos == 0, 0.0, pltpu.roll(u, 1, 0))
        nxt = jnp.where(pos == seq - 1, 0.0, pltpu.roll(u, TM - 1, 0))
        cw = cw_ref[layer]
        conv = prev * cw[0:1] + u * cw[1:2] + nxt * cw[2:3]
        conv_ref[...] = (gb_s[...] * conv).astype(BF)


def _in_proj(layer, xs, mod, g_mix, w_in, cos_t, sin_t, conv_w, new_kv):
    pair = isinstance(xs, tuple)
    xa, xb = xs if pair else (xs, xs)
    spec_a, spec_b = _stream_specs(pair)
    ctx_i = lambda i: jnp.minimum(i, CTX_TILES - 1)
    in_specs = [
        spec_a, spec_b,
        WHOLE, WHOLE,
        pl.BlockSpec((None, D_MODEL, TN_IN), lambda i, j: (layer, 0, jnp.where(i == 0, j, N_IN_TILES - 1))),
        WHOLE, WHOLE, WHOLE,
    ]
    args = [xa, xb, mod, g_mix, w_in, cos_t, sin_t, conv_w]
    aliases = {}
    if new_kv is not None:
        in_specs += [pl.BlockSpec(memory_space=pl.ANY), pl.BlockSpec(memory_space=pl.ANY)]
        args += list(new_kv)
        aliases = {8: 4, 9: 5}
    row_tile = pl.BlockSpec((TM, ATT_WIDTH), lambda i, j: (i, 0))
    return pl.pallas_call(
        functools.partial(_in_kernel, layer=layer, aliased=new_kv is not None),
        grid=(N_TILES, N_IN_TILES),
        in_specs=in_specs,
        out_specs=[
            row_tile,
            row_tile,
            pl.BlockSpec((None, ATT_WIDTH, TM), lambda i, j: (i, 0, 0)),
            row_tile,
            pl.BlockSpec((SEQ_PER_TILE, None, ATT_WIDTH, SEQ), lambda i, j: (ctx_i(i), layer, 0, 0)),
            pl.BlockSpec((SEQ_PER_TILE, None, SEQ * N_HEADS, V_DIM), lambda i, j: (ctx_i(i), layer, 0, 0)),
        ],
        out_shape=[
            jax.ShapeDtypeStruct((N_TOK, ATT_WIDTH), BF),
            jax.ShapeDtypeStruct((N_TOK, ATT_WIDTH), BF),
            jax.ShapeDtypeStruct((N_TILES, ATT_WIDTH, TM), BF),
            jax.ShapeDtypeStruct((N_TOK, CONV_WIDTH), BF),
            jax.ShapeDtypeStruct((BATCH, DEPTH, ATT_WIDTH, SEQ), F32),
            jax.ShapeDtypeStruct((BATCH, DEPTH, SEQ * N_HEADS, V_DIM), F32),
        ],
        scratch_shapes=[
            pltpu.VMEM((TM, D_MODEL), BF),
            pltpu.VMEM((N_IN_TILES, D_MODEL, TN_IN), BF),
            pltpu.VMEM((TM, CONV_WIDTH), F32),
            pltpu.VMEM((TM, CONV_WIDTH), F32),
        ],
        input_output_aliases=aliases,
        compiler_params=_params(("arbitrary", "arbitrary")),
        name=f"in_proj_l{layer}",
    )(*args)


def _lambda(lq_ref, layer, lam_init):
    lq = lq_ref[layer]
    a = jnp.exp(jnp.sum(lq[0:1] * lq[1:2], axis=-1, keepdims=True))
    b = jnp.exp(jnp.sum(lq[2:3] * lq[3:4], axis=-1, keepdims=True))
    return a - b + lam_init


def _head_norm(o, sg, lam_init):
    return _rms(o) * sg * (1.0 - lam_init)


def _attn_ctx_kernel(q_ref, kt_ref, v_ref, lq_ref, sg_ref, o_ref, sc_s, *, layer, lam_init):
    lam = _lambda(lq_ref, layer, lam_init)
    sg = sg_ref[layer:layer + 1, :]

    units = [(b, h) for b in range(CTX_SEQ_PER_STEP) for h in range(N_HEADS)]

    def scores(u):
        b, h = units[u]
        pos = slice(b * SEQ, (b + 1) * SEQ)
        for s in range(2):
            d = slice(h * V_DIM + s * QK_DIM, h * V_DIM + (s + 1) * QK_DIM)
            sc_s[u % 2, s] = jnp.dot(q_ref[pos, d], kt_ref[d, pos], preferred_element_type=F32)

    def finish(u):
        b, h = units[u]
        pos = slice(b * SEQ, (b + 1) * SEQ)
        cols = slice(h * V_DIM, (h + 1) * V_DIM)
        v = v_ref[pos, cols]
        outs = []
        for s in range(2):
            sc = sc_s[u % 2, s]
            e = jnp.exp2(sc - jnp.max(sc, axis=-1, keepdims=True))
            r = 1.0 / jnp.sum(e, axis=-1, keepdims=True)
            outs.append(jnp.dot(e.astype(BF), v, preferred_element_type=F32) * r)
        o = outs[0] - lam * outs[1]
        o_ref[pos, cols] = _head_norm(o, sg, lam_init).astype(BF)

    scores(0)
    for u in range(len(units)):
        if u + 1 < len(units):
            scores(u + 1)
        finish(u)


def _attn_ctx(layer, q, kt, v, lambda_qk, subln_g, lam_init):
    rows = CTX_SEQ_PER_STEP * SEQ
    per_tile = TM // rows
    return pl.pallas_call(
        functools.partial(_attn_ctx_kernel, layer=layer, lam_init=lam_init),
        grid=(N_CTX // rows,),
        in_specs=[
            pl.BlockSpec((rows, ATT_WIDTH), lambda b: (b, 0)),
            pl.BlockSpec((None, ATT_WIDTH, rows), lambda b: (b // per_tile, 0, b % per_tile)),
            pl.BlockSpec((rows, ATT_WIDTH), lambda b: (b, 0)),
            WHOLE, WHOLE,
        ],
        out_specs=pl.BlockSpec((rows, ATT_WIDTH), lambda b: (b, 0)),
        out_shape=jax.ShapeDtypeStruct((N_TOK, ATT_WIDTH), BF),
        scratch_shapes=[pltpu.VMEM((2, 2, SEQ, SEQ), F32)],
        compiler_params=_params(("arbitrary",)),
        name=f"attn_ctx_l{layer}",
    )(q, kt, v, lambda_qk, subln_g)


TQ = 256
LAT_Q_PER_STEP = 2


def _attn_lat_kernel(q_ref, kt_ref, v_ref, ckt_ref, cv_ref, lq_ref, sg_ref, att_in_ref, o_ref, sc_s, *,
                     layer, lam_init):
    del att_in_ref
    lam = _lambda(lq_ref, layer, lam_init)
    sg = sg_ref[layer:layer + 1, :]

    units = [(b, h) for b in range(LAT_Q_PER_STEP) for h in range(N_HEADS)]

    def scores(u, s):
        b, h = units[u]
        d = slice(h * V_DIM + s * QK_DIM, h * V_DIM + (s + 1) * QK_DIM)
        q = q_ref[b * TQ:(b + 1) * TQ, d]
        sc_s[u % 2, s, :, :PAST_LEN] = jnp.dot(q, ckt_ref[d, :].astype(BF), preferred_element_type=F32)
        sc_s[u % 2, s, :, PAST_LEN:] = jnp.dot(q, kt_ref[d, :], preferred_element_type=F32)

    def softmax(u, s):
        sc = sc_s[u % 2, s]
        e = jnp.exp2(sc - jnp.max(sc, axis=-1, keepdims=True))
        return e, 1.0 / jnp.sum(e, axis=-1, keepdims=True)

    def finish(u, p1, p2):
        b, h = units[u]
        cols = slice(h * V_DIM, (h + 1) * V_DIM)
        e = jnp.concatenate([p1[0].astype(BF), p2[0].astype(BF)], axis=0)
        vc = cv_ref[pl.ds(h, PAST_LEN, stride=N_HEADS), :].astype(BF)
        pv = jnp.dot(e[:, :PAST_LEN], vc, preferred_element_type=F32)
        pv = pv + jnp.dot(e[:, PAST_LEN:], v_ref[:, cols], preferred_element_type=F32)
        o = pv[:TQ] * p1[1] - pv[TQ:] * (lam * p2[1])
        o_ref[b * TQ:(b + 1) * TQ, cols] = _head_norm(o, sg, lam_init).astype(BF)

    scores(0, 0)
    scores(0, 1)
    for u in range(len(units)):
        more = u + 1 < len(units)
        if more:
            scores(u + 1, 0)
        p1 = softmax(u, 0)
        if more:
            scores(u + 1, 1)
        finish(u, p1, softmax(u, 1))


def _attn_lat(layer, q, kt, v, cache_kt, cache_v, lambda_qk, subln_g, att, lam_init):
    rows = LAT_Q_PER_STEP * TQ
    nqb = DEC_SEQ // rows
    q0 = N_CTX // rows
    return pl.pallas_call(
        functools.partial(_attn_lat_kernel, layer=layer, lam_init=lam_init),
        grid=(DEC_BATCH, nqb),
        in_specs=[
            pl.BlockSpec((rows, ATT_WIDTH), lambda b, t: (q0 + b * nqb + t, 0)),
            pl.BlockSpec((None, ATT_WIDTH, DEC_SEQ), lambda b, t: (CTX_TILES + b, 0, 0)),
            pl.BlockSpec((DEC_SEQ, ATT_WIDTH), lambda b, t: (CTX_TILES + b, 0)),
            pl.BlockSpec((None, None, ATT_WIDTH, PAST_LEN), lambda b, t: (b, layer, 0, 0)),
            pl.BlockSpec((None, None, PAST_LEN * N_HEADS, V_DIM), lambda b, t: (b, layer, 0, 0)),
            WHOLE, WHOLE,
            pl.BlockSpec(memory_space=pl.ANY),
        ],
        out_specs=pl.BlockSpec((rows, ATT_WIDTH), lambda b, t: (q0 + b * nqb + t, 0)),
        out_shape=jax.ShapeDtypeStruct((N_TOK, ATT_WIDTH), BF),
        scratch_shapes=[pltpu.VMEM((2, 2, TQ, PAST_LEN + DEC_SEQ), F32)],
        input_output_aliases={7: 0},
        compiler_params=_params(("arbitrary", "arbitrary")),
        name=f"attn_lat_l{layer}",
    )(q, kt, v, cache_kt, cache_v, lambda_qk, subln_g, att)


def _out_kernel(att_ref, conv_ref, w_ref, xa_ref, xb_ref, mod_ref, gf_ref, *rest, layer, route):
    if route:
        wr_ref, xo_ref, h2_ref, rt_ref, n_ref, wb_s = rest
    else:
        xo_ref, h2_ref, wb_s = rest
    i = pl.program_id(0)

    @pl.when(i == 0)
    def _():
        wb_s[...] = w_ref[...].astype(BF)

    lat = i >= CTX_TILES
    gain = gf_ref[layer:layer + 1, :] * (1.0 + _mod(mod_ref, layer, i, 4))
    for c in range(TM // ROW_CHUNK):
        rows = slice(c * ROW_CHUNK, (c + 1) * ROW_CHUNK)
        mo = jnp.dot(att_ref[rows, :], wb_s[:ATT_WIDTH, :], preferred_element_type=F32)
        mo = mo + jnp.dot(conv_ref[rows, :], wb_s[ATT_WIDTH:, :], preferred_element_type=F32)
        xn = jnp.where(lat, xb_ref[rows, :], xa_ref[rows, :]) + _mod(mod_ref, layer, i, 2) * mo
        xo_ref[rows, :] = xn
        h2_ref[rows, :] = (_rms(xn) * gain + _mod(mod_ref, layer, i, 3)).astype(BF)
    if route:
        _route(h2_ref[...], wr_ref, rt_ref, n_ref)


def _out_proj(layer, att, conv, w_out, xs, mod, g_ffn, w_router_pad=None):
    pair = isinstance(xs, tuple)
    xa, xb = xs if pair else (xs, xs)
    spec_a, spec_b = _stream_specs(pair)
    row_spec = pl.BlockSpec((TM, D_MODEL), lambda i: (i, 0))
    lane_spec = pl.BlockSpec((TM, LANES), lambda i: (i, 0))
    in_specs = [
        pl.BlockSpec((TM, ATT_WIDTH), lambda i: (i, 0)),
        pl.BlockSpec((TM, CONV_WIDTH), lambda i: (i, 0)),
        pl.BlockSpec((None, D_MODEL, D_MODEL), lambda i: (layer, 0, 0)),
        spec_a, spec_b,
        WHOLE, WHOLE,
    ]
    args = [att, conv, w_out, xa, xb, mod, g_ffn]
    out_specs = [row_spec, row_spec]
    out_shape = [jax.ShapeDtypeStruct((N_TOK, D_MODEL), F32), jax.ShapeDtypeStruct((N_TOK, D_MODEL), BF)]
    route = w_router_pad is not None
    if route:
        in_specs.append(WHOLE)
        args.append(w_router_pad)
        out_specs += [lane_spec, pl.BlockSpec((None, SUBLANES, LANES), lambda i: (i, 0, 0))]
        out_shape += [jax.ShapeDtypeStruct((N_TOK, LANES), F32),
                      jax.ShapeDtypeStruct((N_TILES, SUBLANES, LANES), jnp.int32)]
    return pl.pallas_call(
        functools.partial(_out_kernel, layer=layer, route=route),
        grid=(N_TILES,),
        in_specs=in_specs,
        out_specs=out_specs,
        out_shape=out_shape,
        scratch_shapes=[pltpu.VMEM((D_MODEL, D_MODEL), BF)],
        compiler_params=_params(("arbitrary",)),
        name=f"out_proj_l{layer}",
    )(*args)


def _weight_pieces(layer):
    pieces = []
    for c0 in range(0, D_MODEL, STAGE_W_COLS):
        cols = slice(c0, c0 + STAGE_W_COLS)
        pieces.append((0, layer, slice(0, D_MODEL), cols, None, slice(0, D_MODEL), cols))
    for half in range(FF_HALVES):
        for part in range(2):
            src0 = part * D_FF + half * FF_HALF
            for off in range(0, FF_HALF, STAGE_W_COLS):
                n = min(STAGE_W_COLS, FF_HALF - off)
                pieces.append((1, 0, slice(0, D_MODEL), slice(src0 + off, src0 + off + n),
                               half, slice(0, D_MODEL), slice(part * FF_HALF + off, part * FF_HALF + off + n)))
    for r0 in range(0, D_FF, STAGE_W_ROWS):
        rows = slice(r0, min(r0 + STAGE_W_ROWS, D_FF))
        for c0 in range(0, D_MODEL, STAGE_W_COLS):
            cols = slice(c0, c0 + STAGE_W_COLS)
            pieces.append((2, 0, rows, cols, None, rows, cols))
    return pieces


def _load_dense_weights(layer, hbm, resident, stage_s, sem):
    pieces = _weight_pieces(layer)

    def copy(k):
        src, idx, rows, cols, _, _, _ = pieces[k]
        nr, nc = rows.stop - rows.start, cols.stop - cols.start
        return pltpu.make_async_copy(hbm[src].at[idx, rows, cols], stage_s.at[k % 2, :nr, :nc], sem.at[k % 2])

    copy(0).start()
    for k, (src, _, rows, cols, didx, drows, dcols) in enumerate(pieces):
        if k + 1 < len(pieces):
            copy(k + 1).start()
        copy(k).wait()
        nr, nc = rows.stop - rows.start, cols.stop - cols.start
        piece = stage_s[k % 2, :nr, :nc].astype(BF)
        if didx is None:
            resident[src][drows, dcols] = piece
        else:
            resident[src][didx, drows, dcols] = piece


def _dense_layer_kernel(att_ref, conv_ref, xa_ref, xb_ref, mod_ref, gf_ref, wo_hbm, wgu_hbm, wd_hbm,
                        o_ref, wo_b, wgu_b, wd_b, stage_s, sem, *, layer):
    i = pl.program_id(0)
    tile = i // (TM // FF_TM)

    @pl.when(i == 0)
    def _():
        _load_dense_weights(layer, (wo_hbm, wgu_hbm, wd_hbm), (wo_b, wgu_b, wd_b), stage_s, sem)

    mo = jnp.dot(att_ref[...], wo_b[:ATT_WIDTH, :], preferred_element_type=F32)
    mo = mo + jnp.dot(conv_ref[...], wo_b[ATT_WIDTH:, :], preferred_element_type=F32)
    x = jnp.where(tile >= CTX_TILES, xb_ref[...], xa_ref[...])
    xn = x + _mod(mod_ref, layer, tile, 2) * mo
    h = (_rms(xn) * gf_ref[layer:layer + 1, :]) * (1.0 + _mod(mod_ref, layer, tile, 4)) + _mod(mod_ref, layer, tile, 3)
    h = h.astype(BF)
    y = None
    for half in range(FF_HALVES):
        gu = jnp.dot(h, wgu_b[half], preferred_element_type=F32)
        act = (_silu(gu[:, :FF_HALF]) * gu[:, FF_HALF:]).astype(BF)
        part = jnp.dot(act, wd_b[half * FF_HALF:(half + 1) * FF_HALF, :], preferred_element_type=F32)
        y = part if y is None else y + part
    o_ref[...] = xn + _mod(mod_ref, layer, tile, 5) * y


def _dense_layer(layer, att, conv, xs, mod, g_ffn, w_out, w_gu, w_down):
    xa, xb = xs
    n_ctx = N_CTX // FF_TM
    row = lambda width: pl.BlockSpec((FF_TM, width), lambda i: (i, 0))
    hbm = pl.BlockSpec(memory_space=pl.ANY)
    return pl.pallas_call(
        functools.partial(_dense_layer_kernel, layer=layer),
        grid=(N_TOK // FF_TM,),
        in_specs=[
            row(ATT_WIDTH), row(CONV_WIDTH),
            pl.BlockSpec((FF_TM, D_MODEL), lambda i: (jnp.minimum(i, n_ctx - 1), 0)),
            pl.BlockSpec((FF_TM, D_MODEL), lambda i: (jnp.maximum(i - n_ctx, 0), 0)),
            WHOLE, WHOLE, hbm, hbm, hbm,
        ],
        out_specs=row(D_MODEL),
        out_shape=jax.ShapeDtypeStruct((N_TOK, D_MODEL), F32),
        scratch_shapes=[
            pltpu.VMEM((D_MODEL, D_MODEL), BF),
            pltpu.VMEM((FF_HALVES, D_MODEL, 2 * FF_HALF), BF),
            pltpu.VMEM((D_FF, D_MODEL), BF),
            pltpu.VMEM((2, STAGE_W_ROWS, STAGE_W_COLS), F32),
            pltpu.SemaphoreType.DMA((2,)),
        ],
        compiler_params=_params(("arbitrary",)),
        name="dense_layer",
    )(att, conv, xa, xb, mod, g_ffn, w_out, w_gu, w_down)


def _route(h, wr_ref, rt_ref, n_ref):
    logits = jnp.dot(h, wr_ref[...].astype(BF), preferred_element_type=F32)
    lane = lax.broadcasted_iota(jnp.int32, logits.shape, 1)
    lg = jnp.where(lane < N_EXPERTS, logits, -jnp.inf)
    m1 = jnp.max(lg, axis=-1, keepdims=True)
    i1 = jnp.min(jnp.where(lg == m1, lane, LANES), axis=-1, keepdims=True)
    lg2 = jnp.where(lane == i1, -jnp.inf, lg)
    m2 = jnp.max(lg2, axis=-1, keepdims=True)
    i2 = jnp.min(jnp.where(lg2 == m2, lane, LANES), axis=-1, keepdims=True)
    e2 = jnp.exp(m2 - m1)
    w1 = 1.0 / (1.0 + e2)
    w2 = e2 / (1.0 + e2)

    sel1 = lane == i1
    sel2 = lane == i2
    onehot = jnp.logical_or(sel1, sel2)
    rows = lax.broadcasted_iota(jnp.int32, (ST, ST), 0)
    colsi = lax.broadcasted_iota(jnp.int32, (ST, ST), 1)
    earlier = (colsi < rows).astype(BF)
    onehot_b = onehot.astype(BF)
    before = jnp.concatenate(
        [jnp.dot(earlier, onehot_b[s * ST:(s + 1) * ST], preferred_element_type=F32) for s in range(SUB_PER_TILE)],
        axis=0)
    onehot_f = onehot.astype(F32)
    counts = [jnp.sum(onehot_f[s * ST:(s + 1) * ST], axis=0, keepdims=True) for s in range(SUB_PER_TILE)]
    counts = jnp.concatenate(counts + [jnp.zeros((SUBLANES - SUB_PER_TILE, LANES), F32)], axis=0).astype(jnp.int32)
    seg_len = ((counts + (SEG_ALIGN - 1)) // SEG_ALIGN) * SEG_ALIGN
    n_ref[...] = seg_len
    la = lax.broadcasted_iota(jnp.int32, (LANES, LANES), 0)
    lb = lax.broadcasted_iota(jnp.int32, (LANES, LANES), 1)
    seg_start = jnp.dot(seg_len.astype(F32).astype(BF), (la < lb).astype(BF), preferred_element_type=F32)
    start = jnp.concatenate(
        [jnp.broadcast_to(seg_start[s:s + 1], (ST, LANES)) for s in range(SUB_PER_TILE)], axis=0)
    where = before + start
    lp1 = jnp.sum(jnp.where(sel1, where, 0.0), axis=-1, keepdims=True)
    lp2 = jnp.sum(jnp.where(sel2, where, 0.0), axis=-1, keepdims=True)
    rt = jnp.where(lane == GATE_LANE, w1, jnp.where(lane == GATE_LANE + 1, w2, 0.0))
    rt_ref[...] = jnp.where(lane == ROW_LANE, lp1, jnp.where(lane == ROW_LANE + 1, lp2, rt))


def _chunk_copies(s, cnt_ref, cdst_ref, stage, rows_hbm, sem, *, to_hbm, wait):
    def copy(v, h):
        return pltpu.make_async_copy(v, h, sem) if to_hbm else pltpu.make_async_copy(h, v, sem)

    if wait:
        for z in WAIT_PIECES:
            @pl.when((cnt_ref[s] & z) != 0)
            def _():
                copy(stage.at[pl.ds(0, z * SEG_ALIGN)], rows_hbm.at[pl.ds(0, z * SEG_ALIGN)]).wait()
        return

    def start(c, priority):
        v = stage.at[pl.ds(pl.multiple_of(c * SEG_ALIGN, SEG_ALIGN), SEG_ALIGN)]
        h = rows_hbm.at[pl.ds(pl.multiple_of(cdst_ref[s * STAGE_CHUNKS + c], SEG_ALIGN), SEG_ALIGN)]
        copy(v, h).start(priority=priority)

    def pair(p, carry):
        start(2 * p, 0)
        start(2 * p + 1, 1)
        return carry

    n = cnt_ref[s]
    lax.fori_loop(0, n // 2, pair, 0)

    @pl.when((n & 1) != 0)
    def _():
        start(n - 1, 0)


def _dispatch_kernel(cnt_ref, cdst_ref, h_ref, rt_ref, xs_ref, stage_s, sem):
    copies = functools.partial(_chunk_copies, cnt_ref=cnt_ref, cdst_ref=cdst_ref, rows_hbm=xs_ref, to_hbm=True)
    for k in range(SUB_PER_TILE):
        s = pl.program_id(0) * SUB_PER_TILE + k
        slot = k % 2
        rows = slice(k * ST, (k + 1) * ST)

        @pl.when(s >= 2)
        def _():
            copies(s - 2, stage=stage_s.at[slot], sem=sem.at[slot], wait=True)

        at = rt_ref[rows, :].T[ROW_LANE:ROW_LANE + 2, :].astype(jnp.int32)
        r = lax.broadcasted_iota(jnp.int32, (STAGE_ROWS, ST), 0)
        pick = jnp.logical_or(r == at[0:1, :], r == at[1:2, :]).astype(BF)
        stage_s[slot] = jnp.dot(pick, h_ref[rows, :], preferred_element_type=F32).astype(BF)
        copies(s, stage=stage_s.at[slot], sem=sem.at[slot], wait=False)

    @pl.when(pl.program_id(0) == N_TILES - 1)
    def _():
        copies(N_SUB - 2, stage=stage_s.at[0], sem=sem.at[0], wait=True)
        copies(N_SUB - 1, stage=stage_s.at[1], sem=sem.at[1], wait=True)


def _dispatch(cnt, cdst, h2, rt):
    assert SUB_PER_TILE % 2 == 0
    grid_spec = pltpu.PrefetchScalarGridSpec(
        num_scalar_prefetch=2,
        grid=(N_TILES,),
        in_specs=[
            pl.BlockSpec((TM, D_MODEL), lambda i, *_: (i, 0)),
            pl.BlockSpec((TM, LANES), lambda i, *_: (i, 0)),
        ],
        out_specs=pl.BlockSpec(memory_space=pl.ANY),
        scratch_shapes=[pltpu.VMEM((2, STAGE_ROWS, D_MODEL), BF), pltpu.SemaphoreType.DMA((2,))],
    )
    return pl.pallas_call(
        _dispatch_kernel,
        grid_spec=grid_spec,
        out_shape=jax.ShapeDtypeStruct((R_PAD, D_MODEL), BF),
        compiler_params=_params(("arbitrary",)),
        name="moe_dispatch",
    )(cnt, cdst, h2, rt)


def _fetch_expert(w_hbm, e, wf_s, sem):
    return pltpu.make_async_copy(w_hbm.at[0, e], wf_s, sem)


def _expert_weights(r, te_ref, nt_ref, nxt_ref, w_hbm, wf_s, wb_s, sem, *, first_started=False):
    fetch = functools.partial(_fetch_expert, w_hbm, wf_s=wf_s, sem=sem)

    if not first_started:
        @pl.when(r == 0)
        def _():
            fetch(te_ref[0]).start()

    first = jnp.logical_or(r == 0, te_ref[r] != te_ref[jnp.maximum(r - 1, 0)])

    @pl.when(jnp.logical_and(r < nt_ref[0], first))
    def _():
        fetch(te_ref[r]).wait()
        wb_s[...] = wf_s[...].astype(BF)

        @pl.when(nxt_ref[r] >= 0)
        def _():
            fetch(nxt_ref[r]).start()


def _moe_ffn_kernel(te_ref, nt_ref, nxt_ref, used_ref, x_ref, wgu_hbm, wd_hbm, o_ref,
                    wgu_f, wgu_b, wd_f, wd_b, sem):
    def ffn(r, rows):
        gu = jnp.dot(x_ref[rows, :], wgu_b[...], preferred_element_type=F32)
        act = (_silu(gu[:, :D_FF_EXPERT]) * gu[:, D_FF_EXPERT:]).astype(BF)
        _expert_weights(r, te_ref, nt_ref, nxt_ref, wd_hbm, wd_f, wd_b, sem.at[1], first_started=True)
        o_ref[rows, :] = jnp.dot(act, wd_b[...], preferred_element_type=F32).astype(BF)

    for k in range(TILES_PER_STEP):
        r = pl.program_id(0) * TILES_PER_STEP + k

        @pl.when(r == 0)
        def _():
            _fetch_expert(wd_hbm, te_ref[0], wd_f, sem.at[1]).start()

        _expert_weights(r, te_ref, nt_ref, nxt_ref, wgu_hbm, wgu_f, wgu_b, sem.at[0])

        @pl.when(jnp.logical_and(r < nt_ref[0], used_ref[r] > TG // 2))
        def _():
            ffn(r, slice(k * TG, (k + 1) * TG))

        @pl.when(jnp.logical_and(r < nt_ref[0], used_ref[r] <= TG // 2))
        def _():
            ffn(r, slice(k * TG, k * TG + TG // 2))


def _moe_ffn(te, nt, nxt, used, rows, w_gu, w_down):
    tile_map = lambda i, te, nt, nxt, used: (jnp.minimum(i, (nt[0] - 1) // TILES_PER_STEP), 0)
    grid_spec = pltpu.PrefetchScalarGridSpec(
        num_scalar_prefetch=4,
        grid=(NT_G // TILES_PER_STEP,),
        in_specs=[pl.BlockSpec((TILES_PER_STEP * TG, D_MODEL), tile_map),
                  pl.BlockSpec(memory_space=pl.ANY), pl.BlockSpec(memory_space=pl.ANY)],
        out_specs=pl.BlockSpec((TILES_PER_STEP * TG, D_MODEL), tile_map),
        scratch_shapes=[
            pltpu.VMEM((D_MODEL, 2 * D_FF_EXPERT), F32), pltpu.VMEM((D_MODEL, 2 * D_FF_EXPERT), BF),
            pltpu.VMEM((D_FF_EXPERT, D_MODEL), F32), pltpu.VMEM((D_FF_EXPERT, D_MODEL), BF),
            pltpu.SemaphoreType.DMA((2,)),
        ],
    )
    return pl.pallas_call(
        _moe_ffn_kernel,
        grid_spec=grid_spec,
        out_shape=jax.ShapeDtypeStruct((R_PAD, D_MODEL), BF),
        compiler_params=_params(("arbitrary",)),
        name="moe_ffn",
    )(te, nt, nxt, used, rows, w_gu, w_down)


def _combine_kernel(cnt_ref, cdst_ref, ys_ref, rt_ref, x_ref, mod_ref, fg_ref,
                    oa_ref, ob_ref, stage_s, sem, *, layer):
    i = pl.program_id(0)
    copies = functools.partial(_chunk_copies, cnt_ref=cnt_ref, cdst_ref=cdst_ref, rows_hbm=ys_ref, to_hbm=False)

    @pl.when(i == 0)
    def _():
        stage_s[...] = jnp.zeros_like(stage_s)
        for s in range(COMBINE_AHEAD):
            copies(s, stage=stage_s.at[s], sem=sem.at[s], wait=False)

    for k in range(SUB_PER_TILE):
        s = i * SUB_PER_TILE + k
        slot = k % COMBINE_SLOTS
        ahead = (k + COMBINE_AHEAD) % COMBINE_SLOTS
        rows = slice(k * ST, (k + 1) * ST)

        @pl.when(s + COMBINE_AHEAD < N_SUB)
        def _():
            copies(s + COMBINE_AHEAD, stage=stage_s.at[ahead], sem=sem.at[ahead], wait=False)

        copies(s, stage=stage_s.at[slot], sem=sem.at[slot], wait=True)

        rt = rt_ref[rows, :]
        at = rt[:, ROW_LANE:ROW_LANE + 2].astype(jnp.int32)
        r = lax.broadcasted_iota(jnp.int32, (ST, STAGE_ROWS), 1)
        staged = stage_s[slot]
        a = jnp.dot((r == at[:, 0:1]).astype(BF), staged, preferred_element_type=F32)
        b = jnp.dot((r == at[:, 1:2]).astype(BF), staged, preferred_element_type=F32)
        y = rt[:, GATE_LANE:GATE_LANE + 1] * a + rt[:, GATE_LANE + 1:GATE_LANE + 2] * b
        xn = x_ref[rows, :] + _mod(mod_ref, layer, i, 5) * y
        out = _rms(xn) * fg_ref[...]

        @pl.when(i < CTX_TILES)
        def _():
            oa_ref[rows, :] = out

        @pl.when(i >= CTX_TILES)
        def _():
            ob_ref[rows, :] = out


def _combine(layer, cnt, cdst, ys, rt, x, mod, final_g):
    assert SUB_PER_TILE % COMBINE_SLOTS == 0
    grid_spec = pltpu.PrefetchScalarGridSpec(
        num_scalar_prefetch=2,
        grid=(N_TILES,),
        in_specs=[
            pl.BlockSpec(memory_space=pl.ANY),
            pl.BlockSpec((TM, LANES), lambda i, *_: (i, 0)),
            pl.BlockSpec((TM, D_MODEL), lambda i, *_: (i, 0)),
            WHOLE, WHOLE,
        ],
        out_specs=[
            pl.BlockSpec((TM, D_MODEL), lambda i, *_: (jnp.minimum(i, CTX_TILES - 1), 0)),
            pl.BlockSpec((TM, D_MODEL), lambda i, *_: (jnp.maximum(i - CTX_TILES, 0), 0)),
        ],
        scratch_shapes=[pltpu.VMEM((COMBINE_SLOTS, STAGE_ROWS, D_MODEL), BF),
                        pltpu.SemaphoreType.DMA((COMBINE_SLOTS,))],
    )
    return pl.pallas_call(
        functools.partial(_combine_kernel, layer=layer),
        grid_spec=grid_spec,
        out_shape=[
            jax.ShapeDtypeStruct((N_CTX, D_MODEL), F32),
            jax.ShapeDtypeStruct((N_LAT, D_MODEL), F32),
        ],
        compiler_params=_params(("arbitrary",)),
        name="moe_combine",
    )(cnt, cdst, ys, rt, x, mod, final_g)


def _group_layout(n_tiles):
    n = n_tiles[:, :SUB_PER_TILE, :N_EXPERTS].reshape(N_SUB, N_EXPERTS)
    tiles = (jnp.sum(n, axis=0) + TG - 1) // TG
    tile_end = jnp.cumsum(tiles)
    region = (tile_end - tiles) * TG
    dst = region[None, :] + jnp.cumsum(n, axis=0) - n
    seg_end = jnp.cumsum(n, axis=1)
    seg = seg_end - n
    row = jnp.arange(STAGE_CHUNKS, dtype=jnp.int32) * SEG_ALIGN
    owner = jnp.sum((row[None, :, None] >= seg_end[:, None, :]).astype(jnp.int32), axis=-1)
    own = jnp.minimum(owner, N_EXPERTS - 1)[..., None] == jnp.arange(N_EXPERTS)
    cdst = jnp.sum(jnp.where(own, (dst - seg)[:, None, :], 0), axis=-1) + row[None, :]
    cnt = seg_end[:, -1] // SEG_ALIGN
    nt = tile_end[-1]
    tile_id = jnp.minimum(jnp.arange(NT_G, dtype=jnp.int32), nt - 1)
    te = jnp.sum((tile_id[:, None] >= tile_end[None, :]).astype(jnp.int32), axis=-1)
    after = jnp.sum(jnp.where(te[:, None] == jnp.arange(N_EXPERTS), tile_end[None, :], 0), axis=-1)
    nxt = jnp.where(after < nt, jnp.sum((after[:, None] >= tile_end[None, :]).astype(jnp.int32), axis=-1), -1)
    mine = te[:, None] == jnp.arange(N_EXPERTS)
    region_end = jnp.sum(jnp.where(mine, (region + jnp.sum(n, axis=0))[None, :], 0), axis=-1)
    used = jnp.clip(region_end - tile_id * TG, 0, TG)
    i32 = lambda a: a.astype(jnp.int32)
    return (i32(cnt), i32(cdst.reshape(N_SUB * STAGE_CHUNKS)), i32(te), i32(nt.reshape(1)), i32(nxt), i32(used))


def _rope_tables():
    p = np.arange(DEC_SEQ)
    row = (p // GRID_W).astype(np.float32)
    col = (p % GRID_W).astype(np.float32)
    half = QK_DIM // 4
    freqs = (ROPE_BASE ** (-np.arange(half, dtype=np.float32) / half)).astype(np.float32)
    lane = np.arange(V_DIM)
    f = freqs[lane & (half - 1)]
    use_col = (lane & (2 * half)) != 0
    ang = (np.where(use_col[None, :], col[:, None], row[:, None]) * f[None, :]).astype(np.float32)
    upper = (lane & half) != 0
    sin = np.sin(ang)
    return jnp.asarray(np.cos(ang), F32), jnp.asarray(np.where(upper[None, :], sin, -sin), F32)


def kernel(x_prompt, x_sample, cache_k, cache_v, c, c_ctx, w_ada, b_ada, norm_mix_g, norm_ffn_g,
           w_in, lambda_qk, subln_g, conv_w, w_out, w_gu_dense, w_down_dense, w_router,
           w_gu_moe, w_down_moe, final_g):
    assert DEPTH == 2
    xs = (x_prompt.reshape(N_CTX, D_MODEL), x_sample.reshape(N_LAT, D_MODEL))
    mod = _ada(c_ctx, c, w_ada, b_ada)
    cos_t, sin_t = _rope_tables()
    cache_kt = jnp.transpose(cache_k, (0, 1, 3, 4, 5, 2)).reshape(DEC_BATCH, DEPTH, ATT_WIDTH, PAST_LEN)
    cache_v4 = cache_v.reshape(DEC_BATCH, DEPTH, PAST_LEN * N_HEADS, V_DIM)

    new_kv = None
    for layer in range(DEPTH):
        lam_init = 0.8 - 0.6 * math.exp(-0.3 * layer)
        q, v, kt, conv, nk, nv = _in_proj(layer, xs, mod, norm_mix_g, w_in, cos_t, sin_t, conv_w, new_kv)
        new_kv = (nk, nv)
        att = _attn_ctx(layer, q, kt, v, lambda_qk, subln_g, lam_init)
        att = _attn_lat(layer, q, kt, v, cache_kt, cache_v4, lambda_qk, subln_g, att, lam_init)
        if layer == 0:
            xs = _dense_layer(layer, att, conv, xs, mod, norm_ffn_g, w_out, w_gu_dense, w_down_dense)
        else:
            wr = jnp.pad(w_router[0], ((0, 0), (0, LANES - N_EXPERTS)))
            x1, h2, rt, n_tiles = _out_proj(layer, att, conv, w_out, xs, mod, norm_ffn_g, wr)
            cnt, cdst, te, nt, nxt, used = _group_layout(n_tiles)
            xsort = _dispatch(cnt, cdst, h2, rt)
            ys = _moe_ffn(te, nt, nxt, used, xsort, w_gu_moe, w_down_moe)
            y_ctx, y_lat = _combine(layer, cnt, cdst, ys, rt, x1, mod, final_g.reshape(1, D_MODEL))
    nk, nv = new_kv
    new_k = jnp.transpose(nk.reshape(BATCH, DEPTH, N_HEADS, 2, QK_DIM, SEQ), (0, 1, 5, 2, 3, 4))
    new_v = nv.reshape(BATCH, DEPTH, SEQ, N_HEADS, V_DIM)
    return (y_ctx.reshape(BATCH, SEQ, D_MODEL), y_lat.reshape(DEC_BATCH, DEC_SEQ, D_MODEL), new_k, new_v)
```

```python
import functools
import math

import numpy as np
import jax
import jax.numpy as jnp
from jax import lax
from jax.experimental import pallas as pl
from jax.experimental.pallas import tpu as pltpu

D_MODEL = 1024
BATCH = 16
SEQ = 256
DEPTH = 2
DEC_BATCH = 4
DEC_SEQ = 1024
PAST_LEN = 512
GRID_W = 64
ATT_WIDTH = 512
CONV_WIDTH = 512
N_HEADS = 4
V_DIM = 128
QK_DIM = 64
ROPE_BASE = 10000.0
D_FF = 2816
N_EXPERTS = 8
D_FF_EXPERT = 1408
N_MOD = 6
NORM_EPS = 1e-6
Q_SCALE = QK_DIM ** -0.5 * math.log2(math.e)
IN_COLS = 3 * ATT_WIDTH + 3 * CONV_WIDTH

N_CTX = BATCH * SEQ
N_LAT = DEC_BATCH * DEC_SEQ
N_TOK = N_CTX + N_LAT
TM = 1024
N_TILES = N_TOK // TM
CTX_TILES = N_CTX // TM
SEQ_PER_TILE = TM // SEQ
CTX_SEQ_PER_STEP = 4
COND_ROWS = 8
TN_IN = 1024
N_IN_TILES = IN_COLS // TN_IN
ROW_CHUNK = 512
FF_TM = 512
FF_HALVES = 2
FF_HALF = D_FF // FF_HALVES
STAGE_W_ROWS = 1024
STAGE_W_COLS = 512
TN_ADA = 1536
TG = 512
ST = 256
SUB_PER_TILE = TM // ST
N_SUB = N_TOK // ST
CTX_SUB = N_CTX // ST
SEG_ALIGN = 8
STAGE_ROWS = 640
STAGE_CHUNKS = STAGE_ROWS // SEG_ALIGN
WAIT_PIECES = (64, 32, 16, 8, 4, 2, 1)
COMBINE_SLOTS = 4
COMBINE_AHEAD = 2
TILES_PER_STEP = 2
_MAX_SORTED_ROWS = 2 * N_TOK + N_SUB * N_EXPERTS * (SEG_ALIGN - 1) + N_EXPERTS * (TG - SEG_ALIGN)
NT_G = -(-_MAX_SORTED_ROWS // (TG * TILES_PER_STEP)) * TILES_PER_STEP
R_PAD = NT_G * TG
LANES = 128
SUBLANES = 8
GATE_LANE = 0
ROW_LANE = 2
VMEM_LIMIT = 60 * 1024 * 1024

BF = jnp.bfloat16
F32 = jnp.float32


def _params(sem, vmem=VMEM_LIMIT):
    return pltpu.CompilerParams(dimension_semantics=sem, vmem_limit_bytes=vmem)


def _mod_row(i):
    return jnp.where(i < CTX_TILES, 0, i - (CTX_TILES - 1))


WHOLE = pl.BlockSpec(memory_space=pltpu.VMEM)


def _mod(mod_ref, layer, i, c):
    return mod_ref[layer, pl.ds(_mod_row(i), 1), c * D_MODEL:(c + 1) * D_MODEL]


def _stream_specs(pair, width=D_MODEL):
    a = pl.BlockSpec((TM, width), lambda i, *_: (jnp.minimum(i, CTX_TILES - 1), 0))
    if pair:
        b = pl.BlockSpec((TM, width), lambda i, *_: (jnp.maximum(i - CTX_TILES, 0), 0))
    else:
        b = pl.BlockSpec((TM, width), lambda i, *_: (jnp.maximum(i, CTX_TILES), 0))
    return a, b


def _silu(x):
    return x / (1.0 + jnp.exp(-x))


def _rms(x):
    return x * lax.rsqrt(jnp.mean(x * x, axis=-1, keepdims=True) + NORM_EPS)


def _ada_kernel(cc_ref, c_ref, w_ref, b_ref, o_ref):
    pad = jnp.zeros((COND_ROWS - 1 - DEC_BATCH, D_MODEL), F32)
    cond = jnp.concatenate([cc_ref[...], c_ref[...], pad], axis=0)
    s = _silu(cond).astype(BF)
    bias = b_ref[pl.ds(pl.program_id(0), 1), :]
    o_ref[...] = jnp.dot(s, w_ref[...].astype(BF), preferred_element_type=F32) + bias


def _ada(c_ctx, c, w_ada, b_ada):
    n = N_MOD * D_MODEL
    return pl.pallas_call(
        _ada_kernel,
        grid=(DEPTH, n // TN_ADA),
        in_specs=[
            WHOLE, WHOLE,
            pl.BlockSpec((None, D_MODEL, TN_ADA), lambda l, j: (l, 0, j)),
            pl.BlockSpec((DEPTH, TN_ADA), lambda l, j: (0, j)),
        ],
        out_specs=pl.BlockSpec((None, COND_ROWS, TN_ADA), lambda l, j: (l, 0, j)),
        out_shape=jax.ShapeDtypeStruct((DEPTH, COND_ROWS, n), F32),
        compiler_params=_params(("arbitrary", "arbitrary")),
        name="ada_mod",
    )(c_ctx.reshape(1, D_MODEL), c, w_ada, b_ada)


def _in_weight_pieces(layer):
    pieces = []
    for c0 in range(0, IN_COLS, STAGE_W_COLS):
        g, off = divmod(c0, TN_IN)
        pieces.append((0, layer, slice(0, D_MODEL), slice(c0, c0 + STAGE_W_COLS),
                       g, slice(0, D_MODEL), slice(off, off + STAGE_W_COLS)))
    return pieces


def _in_kernel(*refs, layer, aliased):
    xa_ref, xb_ref, mod_ref, g_ref, w_hbm, cos_ref, sin_ref, cw_ref = refs[:8]
    refs = refs[10:] if aliased else refs[8:]
    q_ref, v_ref, kt_ref, conv_ref, nk_ref, nv_ref, h_s, wb_s, gb_s, gc_s, stage_s, sem = refs
    i = pl.program_id(0)

    @pl.when(i == 0)
    def _():
        _load_weights(_in_weight_pieces(layer), (w_hbm,), (wb_s,), stage_s, sem)

    chunks = [slice(c * ROW_CHUNK, (c + 1) * ROW_CHUNK) for c in range(TM // ROW_CHUNK)]
    seqs_per_chunk = ROW_CHUNK // SEQ

    def norm(x_ref, rows):
        gain = g_ref[layer:layer + 1, :] * (1.0 + _mod(mod_ref, layer, i, 1))
        h_s[rows, :] = (_rms(x_ref[rows, :]) * gain + _mod(mod_ref, layer, i, 0)).astype(BF)

    def proj(rows, group):
        acc = jnp.dot(h_s[rows, :], wb_s[group], preferred_element_type=F32)
        return acc[:, :ATT_WIDTH], acc[:, ATT_WIDTH:]

    def roped(a, rows):
        cos = jnp.concatenate([cos_ref[rows, :]] * N_HEADS, axis=1)
        sin = jnp.concatenate([sin_ref[rows, :]] * N_HEADS, axis=1)
        lane = lax.broadcasted_iota(jnp.int32, a.shape, 1)
        upper = (lane & (QK_DIM // 4)) != 0
        partner = jnp.where(upper, pltpu.roll(a, QK_DIM // 4, 1), pltpu.roll(a, ATT_WIDTH - QK_DIM // 4, 1))
        return a * cos + partner * sin

    def gated_conv(seq):
        for rows in chunks:
            gc, xin = proj(rows, 2)
            gc_s[rows, :] = gc * xin
        u = gc_s[...]
        pos = lax.broadcasted_iota(jnp.int32, (TM, 1), 0) & (seq - 1)
        prev = jnp.where(pos == 0, 0.0, pltpu.roll(u, 1, 0))
        nxt = jnp.where(pos == seq - 1, 0.0, pltpu.roll(u, TM - 1, 0))
        cw = cw_ref[layer]
        conv = prev * cw[0:1] + u * cw[1:2] + nxt * cw[2:3]
        conv_ref[...] = (gb_s[...] * conv).astype(BF)

    @pl.when(i >= CTX_TILES)
    def _():
        for rows in chunks:
            norm(xb_ref, rows)
            q, k = proj(rows, 0)
            q_ref[rows, :] = (roped(q, rows) * Q_SCALE).astype(BF)
            kt_ref[:, rows] = roped(k, rows).T.astype(BF)
        for rows in chunks:
            v, gb = proj(rows, 1)
            v_ref[rows, :] = v.astype(BF)
            gb_s[rows, :] = gb
        gated_conv(DEC_SEQ)

    @pl.when(i < CTX_TILES)
    def _():
        for c, rows in enumerate(chunks):
            norm(xa_ref, rows)
            q, k = proj(rows, 0)
            q_ref[rows, :] = (q * Q_SCALE).astype(BF)
            kt = k.T
            kt_ref[:, rows] = kt.astype(BF)
            for s in range(seqs_per_chunk):
                nk_ref[c * seqs_per_chunk + s] = kt[:, s * SEQ:(s + 1) * SEQ]
        for c, rows in enumerate(chunks):
            v, gb = proj(rows, 1)
            v_ref[rows, :] = v.astype(BF)
            gb_s[rows, :] = gb
            for s in range(seqs_per_chunk):
                for h in range(N_HEADS):
                    nv_ref[c * seqs_per_chunk + s, pl.ds(h, SEQ, stride=N_HEADS), :] = (
                        v[s * SEQ:(s + 1) * SEQ, h * V_DIM:(h + 1) * V_DIM])
        gated_conv(SEQ)


def _in_proj(layer, xs, mod, g_mix, w_in, cos_t, sin_t, conv_w, new_kv):
    pair = isinstance(xs, tuple)
    xa, xb = xs if pair else (xs, xs)
    spec_a, spec_b = _stream_specs(pair)
    ctx_i = lambda i: jnp.minimum(i, CTX_TILES - 1)
    in_specs = [
        spec_a, spec_b,
        WHOLE, WHOLE,
        pl.BlockSpec(memory_space=pl.ANY),
        WHOLE, WHOLE, WHOLE,
    ]
    args = [xa, xb, mod, g_mix, w_in, cos_t, sin_t, conv_w]
    aliases = {}
    if new_kv is not None:
        in_specs += [pl.BlockSpec(memory_space=pl.ANY), pl.BlockSpec(memory_space=pl.ANY)]
        args += list(new_kv)
        aliases = {8: 4, 9: 5}
    row_tile = pl.BlockSpec((TM, ATT_WIDTH), lambda i: (i, 0))
    return pl.pallas_call(
        functools.partial(_in_kernel, layer=layer, aliased=new_kv is not None),
        grid=(N_TILES,),
        in_specs=in_specs,
        out_specs=[
            row_tile,
            row_tile,
            pl.BlockSpec((None, ATT_WIDTH, TM), lambda i: (i, 0, 0)),
            row_tile,
            pl.BlockSpec((SEQ_PER_TILE, None, ATT_WIDTH, SEQ), lambda i: (ctx_i(i), layer, 0, 0)),
            pl.BlockSpec((SEQ_PER_TILE, None, SEQ * N_HEADS, V_DIM), lambda i: (ctx_i(i), layer, 0, 0)),
        ],
        out_shape=[
            jax.ShapeDtypeStruct((N_TOK, ATT_WIDTH), BF),
            jax.ShapeDtypeStruct((N_TOK, ATT_WIDTH), BF),
            jax.ShapeDtypeStruct((N_TILES, ATT_WIDTH, TM), BF),
            jax.ShapeDtypeStruct((N_TOK, CONV_WIDTH), BF),
            jax.ShapeDtypeStruct((BATCH, DEPTH, ATT_WIDTH, SEQ), F32),
            jax.ShapeDtypeStruct((BATCH, DEPTH, SEQ * N_HEADS, V_DIM), F32),
        ],
        scratch_shapes=[
            pltpu.VMEM((TM, D_MODEL), BF),
            pltpu.VMEM((N_IN_TILES, D_MODEL, TN_IN), BF),
            pltpu.VMEM((TM, CONV_WIDTH), F32),
            pltpu.VMEM((TM, CONV_WIDTH), F32),
            pltpu.VMEM((2, STAGE_W_ROWS, STAGE_W_COLS), F32),
            pltpu.SemaphoreType.DMA((2,)),
        ],
        input_output_aliases=aliases,
        compiler_params=_params(("arbitrary",)),
        name=f"in_proj_l{layer}",
    )(*args)


def _lambda(lq_ref, layer, lam_init):
    lq = lq_ref[layer]
    a = jnp.exp(jnp.sum(lq[0:1] * lq[1:2], axis=-1, keepdims=True))
    b = jnp.exp(jnp.sum(lq[2:3] * lq[3:4], axis=-1, keepdims=True))
    return a - b + lam_init


def _head_norm(o, sg, lam_init):
    return _rms(o) * sg * (1.0 - lam_init)


def _attn_ctx_kernel(q_ref, kt_ref, v_ref, lq_ref, sg_ref, o_ref, sc_s, *, layer, lam_init):
    lam = _lambda(lq_ref, layer, lam_init)
    sg = sg_ref[layer:layer + 1, :]

    def sequence(b, carry):
        pos = pl.ds(pl.multiple_of(b * SEQ, SEQ), SEQ)

        def scores(h):
            for s in range(2):
                d = slice(h * V_DIM + s * QK_DIM, h * V_DIM + (s + 1) * QK_DIM)
                sc_s[h % 2, s] = jnp.dot(q_ref[pos, d], kt_ref[d, pos], preferred_element_type=F32)

        def finish(h):
            cols = slice(h * V_DIM, (h + 1) * V_DIM)
            v = v_ref[pos, cols]
            outs = []
            for s in range(2):
                sc = sc_s[h % 2, s]
                e = jnp.exp2(sc - jnp.max(sc, axis=-1, keepdims=True))
                r = 1.0 / jnp.sum(e, axis=-1, keepdims=True)
                outs.append(jnp.dot(e.astype(BF), v, preferred_element_type=F32) * r)
            o = outs[0] - lam * outs[1]
            o_ref[pos, cols] = _head_norm(o, sg, lam_init).astype(BF)

        scores(0)
        for h in range(N_HEADS):
            if h + 1 < N_HEADS:
                scores(h + 1)
            finish(h)
        return carry

    lax.fori_loop(0, CTX_SEQ_PER_STEP, sequence, 0)


def _attn_ctx(layer, q, kt, v, lambda_qk, subln_g, lam_init):
    rows = CTX_SEQ_PER_STEP * SEQ
    per_tile = TM // rows
    return pl.pallas_call(
        functools.partial(_attn_ctx_kernel, layer=layer, lam_init=lam_init),
        grid=(N_CTX // rows,),
        in_specs=[
            pl.BlockSpec((rows, ATT_WIDTH), lambda b: (b, 0)),
            pl.BlockSpec((None, ATT_WIDTH, rows), lambda b: (b // per_tile, 0, b % per_tile)),
            pl.BlockSpec((rows, ATT_WIDTH), lambda b: (b, 0)),
            WHOLE, WHOLE,
        ],
        out_specs=pl.BlockSpec((rows, ATT_WIDTH), lambda b: (b, 0)),
        out_shape=jax.ShapeDtypeStruct((N_TOK, ATT_WIDTH), BF),
        scratch_shapes=[pltpu.VMEM((2, 2, SEQ, SEQ), F32)],
        compiler_params=_params(("arbitrary",)),
        name=f"attn_ctx_l{layer}",
    )(q, kt, v, lambda_qk, subln_g)


TQ = 256
LAT_Q_PER_STEP = 2


def _attn_lat_kernel(q_ref, kt_ref, v_ref, ckt_ref, cv_ref, lq_ref, sg_ref, att_in_ref, o_ref, sc_s, *,
                     layer, lam_init):
    del att_in_ref
    lam = _lambda(lq_ref, layer, lam_init)
    sg = sg_ref[layer:layer + 1, :]

    units = [(b, h) for b in range(LAT_Q_PER_STEP) for h in range(N_HEADS)]

    def scores(u, s):
        b, h = units[u]
        d = slice(h * V_DIM + s * QK_DIM, h * V_DIM + (s + 1) * QK_DIM)
        q = q_ref[b * TQ:(b + 1) * TQ, d]
        sc_s[u % 2, s, :, :PAST_LEN] = jnp.dot(q, ckt_ref[d, :].astype(BF), preferred_element_type=F32)
        sc_s[u % 2, s, :, PAST_LEN:] = jnp.dot(q, kt_ref[d, :], preferred_element_type=F32)

    def softmax(u, s):
        sc = sc_s[u % 2, s]
        e = jnp.exp2(sc - jnp.max(sc, axis=-1, keepdims=True))
        return e, 1.0 / jnp.sum(e, axis=-1, keepdims=True)

    def finish(u, p1, p2):
        b, h = units[u]
        cols = slice(h * V_DIM, (h + 1) * V_DIM)
        e = jnp.concatenate([p1[0].astype(BF), p2[0].astype(BF)], axis=0)
        vc = cv_ref[pl.ds(h, PAST_LEN, stride=N_HEADS), :].astype(BF)
        pv = jnp.dot(e[:, :PAST_LEN], vc, preferred_element_type=F32)
        pv = pv + jnp.dot(e[:, PAST_LEN:], v_ref[:, cols], preferred_element_type=F32)
        o = pv[:TQ] * p1[1] - pv[TQ:] * (lam * p2[1])
        o_ref[b * TQ:(b + 1) * TQ, cols] = _head_norm(o, sg, lam_init).astype(BF)

    scores(0, 0)
    scores(0, 1)
    for u in range(len(units)):
        more = u + 1 < len(units)
        if more:
            scores(u + 1, 0)
        p1 = softmax(u, 0)
        if more:
            scores(u + 1, 1)
        finish(u, p1, softmax(u, 1))


def _attn_lat(layer, q, kt, v, cache_kt, cache_v, lambda_qk, subln_g, att, lam_init):
    rows = LAT_Q_PER_STEP * TQ
    nqb = DEC_SEQ // rows
    q0 = N_CTX // rows
    return pl.pallas_call(
        functools.partial(_attn_lat_kernel, layer=layer, lam_init=lam_init),
        grid=(DEC_BATCH, nqb),
        in_specs=[
            pl.BlockSpec((rows, ATT_WIDTH), lambda b, t: (q0 + b * nqb + t, 0)),
            pl.BlockSpec((None, ATT_WIDTH, DEC_SEQ), lambda b, t: (CTX_TILES + b, 0, 0)),
            pl.BlockSpec((DEC_SEQ, ATT_WIDTH), lambda b, t: (CTX_TILES + b, 0)),
            pl.BlockSpec((None, None, ATT_WIDTH, PAST_LEN), lambda b, t: (b, layer, 0, 0)),
            pl.BlockSpec((None, None, PAST_LEN * N_HEADS, V_DIM), lambda b, t: (b, layer, 0, 0)),
            WHOLE, WHOLE,
            pl.BlockSpec(memory_space=pl.ANY),
        ],
        out_specs=pl.BlockSpec((rows, ATT_WIDTH), lambda b, t: (q0 + b * nqb + t, 0)),
        out_shape=jax.ShapeDtypeStruct((N_TOK, ATT_WIDTH), BF),
        scratch_shapes=[pltpu.VMEM((2, 2, TQ, PAST_LEN + DEC_SEQ), F32)],
        input_output_aliases={7: 0},
        compiler_params=_params(("arbitrary", "arbitrary")),
        name=f"attn_lat_l{layer}",
    )(q, kt, v, cache_kt, cache_v, lambda_qk, subln_g, att)


def _out_kernel(att_ref, conv_ref, w_ref, xa_ref, xb_ref, mod_ref, gf_ref, *rest, layer, route):
    if route:
        wr_ref, xo_ref, h2_ref, rt_ref, n_ref, wb_s = rest
    else:
        xo_ref, h2_ref, wb_s = rest
    i = pl.program_id(0)

    @pl.when(i == 0)
    def _():
        wb_s[...] = w_ref[...].astype(BF)

    lat = i >= CTX_TILES
    gain = gf_ref[layer:layer + 1, :] * (1.0 + _mod(mod_ref, layer, i, 4))
    for c in range(TM // ROW_CHUNK):
        rows = slice(c * ROW_CHUNK, (c + 1) * ROW_CHUNK)
        mo = jnp.dot(att_ref[rows, :], wb_s[:ATT_WIDTH, :], preferred_element_type=F32)
        mo = mo + jnp.dot(conv_ref[rows, :], wb_s[ATT_WIDTH:, :], preferred_element_type=F32)
        xn = jnp.where(lat, xb_ref[rows, :], xa_ref[rows, :]) + _mod(mod_ref, layer, i, 2) * mo
        xo_ref[rows, :] = xn
        h2_ref[rows, :] = (_rms(xn) * gain + _mod(mod_ref, layer, i, 3)).astype(BF)
    if route:
        _route(h2_ref[...], wr_ref, rt_ref, n_ref)


def _out_proj(layer, att, conv, w_out, xs, mod, g_ffn, w_router_pad=None):
    pair = isinstance(xs, tuple)
    xa, xb = xs if pair else (xs, xs)
    spec_a, spec_b = _stream_specs(pair)
    row_spec = pl.BlockSpec((TM, D_MODEL), lambda i: (i, 0))
    lane_spec = pl.BlockSpec((TM, LANES), lambda i: (i, 0))
    in_specs = [
        pl.BlockSpec((TM, ATT_WIDTH), lambda i: (i, 0)),
        pl.BlockSpec((TM, CONV_WIDTH), lambda i: (i, 0)),
        pl.BlockSpec((None, D_MODEL, D_MODEL), lambda i: (layer, 0, 0)),
        spec_a, spec_b,
        WHOLE, WHOLE,
    ]
    args = [att, conv, w_out, xa, xb, mod, g_ffn]
    out_specs = [row_spec, row_spec]
    out_shape = [jax.ShapeDtypeStruct((N_TOK, D_MODEL), F32), jax.ShapeDtypeStruct((N_TOK, D_MODEL), BF)]
    route = w_router_pad is not None
    if route:
        in_specs.append(WHOLE)
        args.append(w_router_pad)
        out_specs += [lane_spec, pl.BlockSpec((None, SUBLANES, LANES), lambda i: (i, 0, 0))]
        out_shape += [jax.ShapeDtypeStruct((N_TOK, LANES), F32),
                      jax.ShapeDtypeStruct((N_TILES, SUBLANES, LANES), jnp.int32)]
    return pl.pallas_call(
        functools.partial(_out_kernel, layer=layer, route=route),
        grid=(N_TILES,),
        in_specs=in_specs,
        out_specs=out_specs,
        out_shape=out_shape,
        scratch_shapes=[pltpu.VMEM((D_MODEL, D_MODEL), BF)],
        compiler_params=_params(("arbitrary",)),
        name=f"out_proj_l{layer}",
    )(*args)


def _weight_pieces(layer):
    pieces = []
    for c0 in range(0, D_MODEL, STAGE_W_COLS):
        cols = slice(c0, c0 + STAGE_W_COLS)
        pieces.append((0, layer, slice(0, D_MODEL), cols, None, slice(0, D_MODEL), cols))
    for half in range(FF_HALVES):
        for part in range(2):
            src0 = part * D_FF + half * FF_HALF
            for off in range(0, FF_HALF, STAGE_W_COLS):
                n = min(STAGE_W_COLS, FF_HALF - off)
                pieces.append((1, 0, slice(0, D_MODEL), slice(src0 + off, src0 + off + n),
                               half, slice(0, D_MODEL), slice(part * FF_HALF + off, part * FF_HALF + off + n)))
    for r0 in range(0, D_FF, STAGE_W_ROWS):
        rows = slice(r0, min(r0 + STAGE_W_ROWS, D_FF))
        for c0 in range(0, D_MODEL, STAGE_W_COLS):
            cols = slice(c0, c0 + STAGE_W_COLS)
            pieces.append((2, 0, rows, cols, None, rows, cols))
    return pieces


def _load_weights(pieces, hbm, resident, stage_s, sem):
    def copy(k):
        src, idx, rows, cols, _, _, _ = pieces[k]
        nr, nc = rows.stop - rows.start, cols.stop - cols.start
        return pltpu.make_async_copy(hbm[src].at[idx, rows, cols], stage_s.at[k % 2, :nr, :nc], sem.at[k % 2])

    copy(0).start()
    for k, (src, _, rows, cols, didx, drows, dcols) in enumerate(pieces):
        if k + 1 < len(pieces):
            copy(k + 1).start()
        copy(k).wait()
        nr, nc = rows.stop - rows.start, cols.stop - cols.start
        piece = stage_s[k % 2, :nr, :nc].astype(BF)
        if didx is None:
            resident[src][drows, dcols] = piece
        else:
            resident[src][didx, drows, dcols] = piece


def _dense_layer_kernel(att_ref, conv_ref, xa_ref, xb_ref, mod_ref, gf_ref, wo_hbm, wgu_hbm, wd_hbm,
                        o_ref, wo_b, wgu_b, wd_b, stage_s, sem, *, layer):
    i = pl.program_id(0)
    tile = i // (TM // FF_TM)

    @pl.when(i == 0)
    def _():
        _load_weights(_weight_pieces(layer), (wo_hbm, wgu_hbm, wd_hbm), (wo_b, wgu_b, wd_b), stage_s, sem)

    mo = jnp.dot(att_ref[...], wo_b[:ATT_WIDTH, :], preferred_element_type=F32)
    mo = mo + jnp.dot(conv_ref[...], wo_b[ATT_WIDTH:, :], preferred_element_type=F32)
    x = jnp.where(tile >= CTX_TILES, xb_ref[...], xa_ref[...])
    xn = x + _mod(mod_ref, layer, tile, 2) * mo
    h = (_rms(xn) * gf_ref[layer:layer + 1, :]) * (1.0 + _mod(mod_ref, layer, tile, 4)) + _mod(mod_ref, layer, tile, 3)
    h = h.astype(BF)
    y = None
    for half in range(FF_HALVES):
        gu = jnp.dot(h, wgu_b[half], preferred_element_type=F32)
        act = (_silu(gu[:, :FF_HALF]) * gu[:, FF_HALF:]).astype(BF)
        part = jnp.dot(act, wd_b[half * FF_HALF:(half + 1) * FF_HALF, :], preferred_element_type=F32)
        y = part if y is None else y + part
    o_ref[...] = xn + _mod(mod_ref, layer, tile, 5) * y


def _dense_layer(layer, att, conv, xs, mod, g_ffn, w_out, w_gu, w_down):
    xa, xb = xs
    n_ctx = N_CTX // FF_TM
    row = lambda width: pl.BlockSpec((FF_TM, width), lambda i: (i, 0))
    hbm = pl.BlockSpec(memory_space=pl.ANY)
    return pl.pallas_call(
        functools.partial(_dense_layer_kernel, layer=layer),
        grid=(N_TOK // FF_TM,),
        in_specs=[
            row(ATT_WIDTH), row(CONV_WIDTH),
            pl.BlockSpec((FF_TM, D_MODEL), lambda i: (jnp.minimum(i, n_ctx - 1), 0)),
            pl.BlockSpec((FF_TM, D_MODEL), lambda i: (jnp.maximum(i - n_ctx, 0), 0)),
            WHOLE, WHOLE, hbm, hbm, hbm,
        ],
        out_specs=row(D_MODEL),
        out_shape=jax.ShapeDtypeStruct((N_TOK, D_MODEL), F32),
        scratch_shapes=[
            pltpu.VMEM((D_MODEL, D_MODEL), BF),
            pltpu.VMEM((FF_HALVES, D_MODEL, 2 * FF_HALF), BF),
            pltpu.VMEM((D_FF, D_MODEL), BF),
            pltpu.VMEM((2, STAGE_W_ROWS, STAGE_W_COLS), F32),
            pltpu.SemaphoreType.DMA((2,)),
        ],
        compiler_params=_params(("arbitrary",)),
        name="dense_layer",
    )(att, conv, xa, xb, mod, g_ffn, w_out, w_gu, w_down)


def _route(h, wr_ref, rt_ref, n_ref):
    logits = jnp.dot(h, wr_ref[...].astype(BF), preferred_element_type=F32)
    lane = lax.broadcasted_iota(jnp.int32, logits.shape, 1)
    lg = jnp.where(lane < N_EXPERTS, logits, -jnp.inf)
    m1 = jnp.max(lg, axis=-1, keepdims=True)
    i1 = jnp.min(jnp.where(lg == m1, lane, LANES), axis=-1, keepdims=True)
    lg2 = jnp.where(lane == i1, -jnp.inf, lg)
    m2 = jnp.max(lg2, axis=-1, keepdims=True)
    i2 = jnp.min(jnp.where(lg2 == m2, lane, LANES), axis=-1, keepdims=True)
    e2 = jnp.exp(m2 - m1)
    w1 = 1.0 / (1.0 + e2)
    w2 = e2 / (1.0 + e2)

    sel1 = lane == i1
    sel2 = lane == i2
    onehot = jnp.logical_or(sel1, sel2)
    rows = lax.broadcasted_iota(jnp.int32, (ST, ST), 0)
    colsi = lax.broadcasted_iota(jnp.int32, (ST, ST), 1)
    earlier = (colsi < rows).astype(BF)
    onehot_b = onehot.astype(BF)
    before = jnp.concatenate(
        [jnp.dot(earlier, onehot_b[s * ST:(s + 1) * ST], preferred_element_type=F32) for s in range(SUB_PER_TILE)],
        axis=0)
    onehot_f = onehot.astype(F32)
    counts = [jnp.sum(onehot_f[s * ST:(s + 1) * ST], axis=0, keepdims=True) for s in range(SUB_PER_TILE)]
    counts = jnp.concatenate(counts + [jnp.zeros((SUBLANES - SUB_PER_TILE, LANES), F32)], axis=0).astype(jnp.int32)
    seg_len = ((counts + (SEG_ALIGN - 1)) // SEG_ALIGN) * SEG_ALIGN
    n_ref[...] = seg_len
    la = lax.broadcasted_iota(jnp.int32, (LANES, LANES), 0)
    lb = lax.broadcasted_iota(jnp.int32, (LANES, LANES), 1)
    seg_start = jnp.dot(seg_len.astype(F32).astype(BF), (la < lb).astype(BF), preferred_element_type=F32)
    start = jnp.concatenate(
        [jnp.broadcast_to(seg_start[s:s + 1], (ST, LANES)) for s in range(SUB_PER_TILE)], axis=0)
    where = before + start
    lp1 = jnp.sum(jnp.where(sel1, where, 0.0), axis=-1, keepdims=True)
    lp2 = jnp.sum(jnp.where(sel2, where, 0.0), axis=-1, keepdims=True)
    rt = jnp.where(lane == GATE_LANE, w1, jnp.where(lane == GATE_LANE + 1, w2, 0.0))
    rt_ref[...] = jnp.where(lane == ROW_LANE, lp1, jnp.where(lane == ROW_LANE + 1, lp2, rt))


def _chunk_copies(s, cnt_ref, cdst_ref, stage, rows_hbm, sem, *, to_hbm, wait):
    def copy(v, h):
        return pltpu.make_async_copy(v, h, sem) if to_hbm else pltpu.make_async_copy(h, v, sem)

    if wait:
        for z in WAIT_PIECES:
            @pl.when((cnt_ref[s] & z) != 0)
            def _():
                copy(stage.at[pl.ds(0, z * SEG_ALIGN)], rows_hbm.at[pl.ds(0, z * SEG_ALIGN)]).wait()
        return

    def start(c, priority):
        v = stage.at[pl.ds(pl.multiple_of(c * SEG_ALIGN, SEG_ALIGN), SEG_ALIGN)]
        h = rows_hbm.at[pl.ds(pl.multiple_of(cdst_ref[s * STAGE_CHUNKS + c], SEG_ALIGN), SEG_ALIGN)]
        copy(v, h).start(priority=priority)

    def pair(p, carry):
        start(2 * p, 0)
        start(2 * p + 1, 1)
        return carry

    n = cnt_ref[s]
    lax.fori_loop(0, n // 2, pair, 0)

    @pl.when((n & 1) != 0)
    def _():
        start(n - 1, 0)


def _dispatch_kernel(cnt_ref, cdst_ref, h_ref, rt_ref, xs_ref, stage_s, sem):
    copies = functools.partial(_chunk_copies, cnt_ref=cnt_ref, cdst_ref=cdst_ref, rows_hbm=xs_ref, to_hbm=True)
    for k in range(SUB_PER_TILE):
        s = pl.program_id(0) * SUB_PER_TILE + k
        slot = k % 2
        rows = slice(k * ST, (k + 1) * ST)

        @pl.when(s >= 2)
        def _():
            copies(s - 2, stage=stage_s.at[slot], sem=sem.at[slot], wait=True)

        at = rt_ref[rows, :].T[ROW_LANE:ROW_LANE + 2, :].astype(jnp.int32)
        r = lax.broadcasted_iota(jnp.int32, (STAGE_ROWS, ST), 0)
        pick = jnp.logical_or(r == at[0:1, :], r == at[1:2, :]).astype(BF)
        stage_s[slot] = jnp.dot(pick, h_ref[rows, :], preferred_element_type=F32).astype(BF)
        copies(s, stage=stage_s.at[slot], sem=sem.at[slot], wait=False)

    @pl.when(pl.program_id(0) == N_TILES - 1)
    def _():
        copies(N_SUB - 2, stage=stage_s.at[0], sem=sem.at[0], wait=True)
        copies(N_SUB - 1, stage=stage_s.at[1], sem=sem.at[1], wait=True)


def _dispatch(cnt, cdst, h2, rt):
    assert SUB_PER_TILE % 2 == 0
    grid_spec = pltpu.PrefetchScalarGridSpec(
        num_scalar_prefetch=2,
        grid=(N_TILES,),
        in_specs=[
            pl.BlockSpec((TM, D_MODEL), lambda i, *_: (i, 0)),
            pl.BlockSpec((TM, LANES), lambda i, *_: (i, 0)),
        ],
        out_specs=pl.BlockSpec(memory_space=pl.ANY),
        scratch_shapes=[pltpu.VMEM((2, STAGE_ROWS, D_MODEL), BF), pltpu.SemaphoreType.DMA((2,))],
    )
    return pl.pallas_call(
        _dispatch_kernel,
        grid_spec=grid_spec,
        out_shape=jax.ShapeDtypeStruct((R_PAD, D_MODEL), BF),
        compiler_params=_params(("arbitrary",)),
        name="moe_dispatch",
    )(cnt, cdst, h2, rt)


def _expert_weights(r, te_ref, nt_ref, nxt_ref, w_hbm, wf_s, wb_s, sem):
    def fetch(e):
        return pltpu.make_async_copy(w_hbm.at[0, e], wf_s, sem)

    @pl.when(r == 0)
    def _():
        fetch(te_ref[0]).start()

    first = jnp.logical_or(r == 0, te_ref[r] != te_ref[jnp.maximum(r - 1, 0)])

    @pl.when(jnp.logical_and(r < nt_ref[0], first))
    def _():
        fetch(te_ref[r]).wait()
        wb_s[...] = wf_s[...].astype(BF)

        @pl.when(nxt_ref[r] >= 0)
        def _():
            fetch(nxt_ref[r]).start()


def _moe_ffn_kernel(te_ref, nt_ref, nxt_ref, used_ref, x_ref, wgu_hbm, wd_hbm, o_ref,
                    wgu_f, wgu_b, wd_f, wd_b, sem):
    def ffn(rows):
        gu = jnp.dot(x_ref[rows, :], wgu_b[...], preferred_element_type=F32)
        act = (_silu(gu[:, :D_FF_EXPERT]) * gu[:, D_FF_EXPERT:]).astype(BF)
        o_ref[rows, :] = jnp.dot(act, wd_b[...], preferred_element_type=F32).astype(BF)

    for k in range(TILES_PER_STEP):
        r = pl.program_id(0) * TILES_PER_STEP + k
        _expert_weights(r, te_ref, nt_ref, nxt_ref, wgu_hbm, wgu_f, wgu_b, sem.at[0])
        _expert_weights(r, te_ref, nt_ref, nxt_ref, wd_hbm, wd_f, wd_b, sem.at[1])

        @pl.when(jnp.logical_and(r < nt_ref[0], used_ref[r] > TG // 2))
        def _():
            ffn(slice(k * TG, (k + 1) * TG))

        @pl.when(jnp.logical_and(r < nt_ref[0], used_ref[r] <= TG // 2))
        def _():
            ffn(slice(k * TG, k * TG + TG // 2))


def _moe_ffn(te, nt, nxt, used, rows, w_gu, w_down):
    tile_map = lambda i, te, nt, nxt, used: (jnp.minimum(i, (nt[0] - 1) // TILES_PER_STEP), 0)
    grid_spec = pltpu.PrefetchScalarGridSpec(
        num_scalar_prefetch=4,
        grid=(NT_G // TILES_PER_STEP,),
        in_specs=[pl.BlockSpec((TILES_PER_STEP * TG, D_MODEL), tile_map),
                  pl.BlockSpec(memory_space=pl.ANY), pl.BlockSpec(memory_space=pl.ANY)],
        out_specs=pl.BlockSpec((TILES_PER_STEP * TG, D_MODEL), tile_map),
        scratch_shapes=[
            pltpu.VMEM((D_MODEL, 2 * D_FF_EXPERT), F32), pltpu.VMEM((D_MODEL, 2 * D_FF_EXPERT), BF),
            pltpu.VMEM((D_FF_EXPERT, D_MODEL), F32), pltpu.VMEM((D_FF_EXPERT, D_MODEL), BF),
            pltpu.SemaphoreType.DMA((2,)),
        ],
    )
    return pl.pallas_call(
        _moe_ffn_kernel,
        grid_spec=grid_spec,
        out_shape=jax.ShapeDtypeStruct((R_PAD, D_MODEL), BF),
        compiler_params=_params(("arbitrary",)),
        name="moe_ffn",
    )(te, nt, nxt, used, rows, w_gu, w_down)


def _combine_kernel(cnt_ref, cdst_ref, ys_ref, rt_ref, x_ref, mod_ref, fg_ref,
                    oa_ref, ob_ref, stage_s, sem, *, layer):
    i = pl.program_id(0)
    copies = functools.partial(_chunk_copies, cnt_ref=cnt_ref, cdst_ref=cdst_ref, rows_hbm=ys_ref, to_hbm=False)

    @pl.when(i == 0)
    def _():
        stage_s[...] = jnp.zeros_like(stage_s)
        for s in range(COMBINE_AHEAD):
            copies(s, stage=stage_s.at[s], sem=sem.at[s], wait=False)

    for k in range(SUB_PER_TILE):
        s = i * SUB_PER_TILE + k
        slot = k % COMBINE_SLOTS
        ahead = (k + COMBINE_AHEAD) % COMBINE_SLOTS
        rows = slice(k * ST, (k + 1) * ST)

        @pl.when(s + COMBINE_AHEAD < N_SUB)
        def _():
            copies(s + COMBINE_AHEAD, stage=stage_s.at[ahead], sem=sem.at[ahead], wait=False)

        copies(s, stage=stage_s.at[slot], sem=sem.at[slot], wait=True)

        rt = rt_ref[rows, :]
        at = rt[:, ROW_LANE:ROW_LANE + 2].astype(jnp.int32)
        r = lax.broadcasted_iota(jnp.int32, (ST, STAGE_ROWS), 1)
        staged = stage_s[slot]
        a = jnp.dot((r == at[:, 0:1]).astype(BF), staged, preferred_element_type=F32)
        b = jnp.dot((r == at[:, 1:2]).astype(BF), staged, preferred_element_type=F32)
        y = rt[:, GATE_LANE:GATE_LANE + 1] * a + rt[:, GATE_LANE + 1:GATE_LANE + 2] * b
        xn = x_ref[rows, :] + _mod(mod_ref, layer, i, 5) * y
        out = _rms(xn) * fg_ref[...]

        @pl.when(i < CTX_TILES)
        def _():
            oa_ref[rows, :] = out

        @pl.when(i >= CTX_TILES)
        def _():
            ob_ref[rows, :] = out


def _combine(layer, cnt, cdst, ys, rt, x, mod, final_g):
    assert SUB_PER_TILE % COMBINE_SLOTS == 0
    grid_spec = pltpu.PrefetchScalarGridSpec(
        num_scalar_prefetch=2,
        grid=(N_TILES,),
        in_specs=[
            pl.BlockSpec(memory_space=pl.ANY),
            pl.BlockSpec((TM, LANES), lambda i, *_: (i, 0)),
            pl.BlockSpec((TM, D_MODEL), lambda i, *_: (i, 0)),
            WHOLE, WHOLE,
        ],
        out_specs=[
            pl.BlockSpec((TM, D_MODEL), lambda i, *_: (jnp.minimum(i, CTX_TILES - 1), 0)),
            pl.BlockSpec((TM, D_MODEL), lambda i, *_: (jnp.maximum(i - CTX_TILES, 0), 0)),
        ],
        scratch_shapes=[pltpu.VMEM((COMBINE_SLOTS, STAGE_ROWS, D_MODEL), BF),
                        pltpu.SemaphoreType.DMA((COMBINE_SLOTS,))],
    )
    return pl.pallas_call(
        functools.partial(_combine_kernel, layer=layer),
        grid_spec=grid_spec,
        out_shape=[
            jax.ShapeDtypeStruct((N_CTX, D_MODEL), F32),
            jax.ShapeDtypeStruct((N_LAT, D_MODEL), F32),
        ],
        compiler_params=_params(("arbitrary",)),
        name="moe_combine",
    )(cnt, cdst, ys, rt, x, mod, final_g)


def _group_layout(n_tiles):
    n = n_tiles[:, :SUB_PER_TILE, :N_EXPERTS].reshape(N_SUB, N_EXPERTS)
    tiles = (jnp.sum(n, axis=0) + TG - 1) // TG
    tile_end = jnp.cumsum(tiles)
    region = (tile_end - tiles) * TG
    dst = region[None, :] + jnp.cumsum(n, axis=0) - n
    seg_end = jnp.cumsum(n, axis=1)
    seg = seg_end - n
    row = jnp.arange(STAGE_CHUNKS, dtype=jnp.int32) * SEG_ALIGN
    owner = jnp.sum((row[None, :, None] >= seg_end[:, None, :]).astype(jnp.int32), axis=-1)
    own = jnp.minimum(owner, N_EXPERTS - 1)[..., None] == jnp.arange(N_EXPERTS)
    cdst = jnp.sum(jnp.where(own, (dst - seg)[:, None, :], 0), axis=-1) + row[None, :]
    cnt = seg_end[:, -1] // SEG_ALIGN
    nt = tile_end[-1]
    tile_id = jnp.minimum(jnp.arange(NT_G, dtype=jnp.int32), nt - 1)
    te = jnp.sum((tile_id[:, None] >= tile_end[None, :]).astype(jnp.int32), axis=-1)
    after = jnp.sum(jnp.where(te[:, None] == jnp.arange(N_EXPERTS), tile_end[None, :], 0), axis=-1)
    nxt = jnp.where(after < nt, jnp.sum((after[:, None] >= tile_end[None, :]).astype(jnp.int32), axis=-1), -1)
    mine = te[:, None] == jnp.arange(N_EXPERTS)
    region_end = jnp.sum(jnp.where(mine, (region + jnp.sum(n, axis=0))[None, :], 0), axis=-1)
    used = jnp.clip(region_end - tile_id * TG, 0, TG)
    i32 = lambda a: a.astype(jnp.int32)
    return (i32(cnt), i32(cdst.reshape(N_SUB * STAGE_CHUNKS)), i32(te), i32(nt.reshape(1)), i32(nxt), i32(used))


def _rope_tables():
    p = np.arange(DEC_SEQ)
    row = (p // GRID_W).astype(np.float32)
    col = (p % GRID_W).astype(np.float32)
    half = QK_DIM // 4
    freqs = (ROPE_BASE ** (-np.arange(half, dtype=np.float32) / half)).astype(np.float32)
    lane = np.arange(V_DIM)
    f = freqs[lane & (half - 1)]
    use_col = (lane & (2 * half)) != 0
    ang = (np.where(use_col[None, :], col[:, None], row[:, None]) * f[None, :]).astype(np.float32)
    upper = (lane & half) != 0
    sin = np.sin(ang)
    return jnp.asarray(np.cos(ang), F32), jnp.asarray(np.where(upper[None, :], sin, -sin), F32)


def kernel(x_prompt, x_sample, cache_k, cache_v, c, c_ctx, w_ada, b_ada, norm_mix_g, norm_ffn_g,
           w_in, lambda_qk, subln_g, conv_w, w_out, w_gu_dense, w_down_dense, w_router,
           w_gu_moe, w_down_moe, final_g):
    assert DEPTH == 2
    xs = (x_prompt.reshape(N_CTX, D_MODEL), x_sample.reshape(N_LAT, D_MODEL))
    mod = _ada(c_ctx, c, w_ada, b_ada)
    cos_t, sin_t = _rope_tables()
    cache_kt = jnp.transpose(cache_k, (0, 1, 3, 4, 5, 2)).reshape(DEC_BATCH, DEPTH, ATT_WIDTH, PAST_LEN)
    cache_v4 = cache_v.reshape(DEC_BATCH, DEPTH, PAST_LEN * N_HEADS, V_DIM)

    new_kv = None
    for layer in range(DEPTH):
        lam_init = 0.8 - 0.6 * math.exp(-0.3 * layer)
        q, v, kt, conv, nk, nv = _in_proj(layer, xs, mod, norm_mix_g, w_in, cos_t, sin_t, conv_w, new_kv)
        new_kv = (nk, nv)
        att = _attn_ctx(layer, q, kt, v, lambda_qk, subln_g, lam_init)
        att = _attn_lat(layer, q, kt, v, cache_kt, cache_v4, lambda_qk, subln_g, att, lam_init)
        if layer == 0:
            xs = _dense_layer(layer, att, conv, xs, mod, norm_ffn_g, w_out, w_gu_dense, w_down_dense)
        else:
            wr = jnp.pad(w_router[0], ((0, 0), (0, LANES - N_EXPERTS)))
            x1, h2, rt, n_tiles = _out_proj(layer, att, conv, w_out, xs, mod, norm_ffn_g, wr)
            cnt, cdst, te, nt, nxt, used = _group_layout(n_tiles)
            xsort = _dispatch(cnt, cdst, h2, rt)
            ys = _moe_ffn(te, nt, nxt, used, xsort, w_gu_moe, w_down_moe)
            y_ctx, y_lat = _combine(layer, cnt, cdst, ys, rt, x1, mod, final_g.reshape(1, D_MODEL))
    nk, nv = new_kv
    new_k = jnp.transpose(nk.reshape(BATCH, DEPTH, N_HEADS, 2, QK_DIM, SEQ), (0, 1, 5, 2, 3, 4))
    new_v = nv.reshape(BATCH, DEPTH, SEQ, N_HEADS, V_DIM)
    return (y_ctx.reshape(BATCH, SEQ, D_MODEL), y_lat.reshape(DEC_BATCH, DEC_SEQ, D_MODEL), new_k, new_v)
```

```python
import functools
import math

import numpy as np
import jax
import jax.numpy as jnp
from jax import lax
from jax.experimental import pallas as pl
from jax.experimental.pallas import tpu as pltpu

D_MODEL = 1024
BATCH = 16
SEQ = 256
DEPTH = 2
DEC_BATCH = 4
DEC_SEQ = 1024
PAST_LEN = 512
GRID_W = 64
ATT_WIDTH = 512
CONV_WIDTH = 512
N_HEADS = 4
V_DIM = 128
QK_DIM = 64
ROPE_BASE = 10000.0
D_FF = 2816
N_EXPERTS = 8
D_FF_EXPERT = 1408
N_MOD = 6
NORM_EPS = 1e-6
Q_SCALE = QK_DIM ** -0.5 * math.log2(math.e)
IN_COLS = 3 * ATT_WIDTH + 3 * CONV_WIDTH

N_CTX = BATCH * SEQ
N_LAT = DEC_BATCH * DEC_SEQ
N_TOK = N_CTX + N_LAT
TM = 1024
N_TILES = N_TOK // TM
CTX_TILES = N_CTX // TM
SEQ_PER_TILE = TM // SEQ
CTX_SEQ_PER_STEP = 4
COND_ROWS = 8
TN_IN = 1024
N_IN_TILES = IN_COLS // TN_IN
ROW_CHUNK = 256
FF_TM = 512
FF_HALVES = 2
FF_HALF = D_FF // FF_HALVES
STAGE_W_ROWS = 1024
STAGE_W_COLS = 512
TN_ADA = 1536
TG = 512
ST = 256
SUB_PER_TILE = TM // ST
N_SUB = N_TOK // ST
SEG_ALIGN = 8
STAGE_ROWS = 640
STAGE_CHUNKS = STAGE_ROWS // SEG_ALIGN
WAIT_PIECES = (64, 32, 16, 8, 4, 2, 1)
COMBINE_SLOTS = 4
COMBINE_AHEAD = 2
TILES_PER_STEP = 2
_MAX_SORTED_ROWS = 2 * N_TOK + N_SUB * N_EXPERTS * (SEG_ALIGN - 1) + N_EXPERTS * (TG - SEG_ALIGN)
NT_G = -(-_MAX_SORTED_ROWS // (TG * TILES_PER_STEP)) * TILES_PER_STEP
R_PAD = NT_G * TG
LANES = 128
SUBLANES = 8
GATE_LANE = 0
ROW_LANE = 2
VMEM_LIMIT = 60 * 1024 * 1024

BF = jnp.bfloat16
F32 = jnp.float32


def _params(sem, vmem=VMEM_LIMIT):
    return pltpu.CompilerParams(dimension_semantics=sem, vmem_limit_bytes=vmem)


def _mod_row(i):
    return jnp.where(i < CTX_TILES, 0, i - (CTX_TILES - 1))


WHOLE = pl.BlockSpec(memory_space=pltpu.VMEM)


def _mod(mod_ref, layer, i, c):
    return mod_ref[layer, pl.ds(_mod_row(i), 1), c * D_MODEL:(c + 1) * D_MODEL]


def _stream_specs(pair, width=D_MODEL):
    a = pl.BlockSpec((TM, width), lambda i, *_: (jnp.minimum(i, CTX_TILES - 1), 0))
    if pair:
        b = pl.BlockSpec((TM, width), lambda i, *_: (jnp.maximum(i - CTX_TILES, 0), 0))
    else:
        b = pl.BlockSpec((TM, width), lambda i, *_: (jnp.maximum(i, CTX_TILES), 0))
    return a, b


def _silu(x):
    return x / (1.0 + jnp.exp(-x))


def _rms(x):
    return x * lax.rsqrt(jnp.mean(x * x, axis=-1, keepdims=True) + NORM_EPS)


def _ada_kernel(cc_ref, c_ref, w_ref, b_ref, o_ref):
    pad = jnp.zeros((COND_ROWS - 1 - DEC_BATCH, D_MODEL), F32)
    cond = jnp.concatenate([cc_ref[...], c_ref[...], pad], axis=0)
    s = _silu(cond).astype(BF)
    bias = b_ref[pl.ds(pl.program_id(0), 1), :]
    o_ref[...] = jnp.dot(s, w_ref[...].astype(BF), preferred_element_type=F32) + bias


def _ada(c_ctx, c, w_ada, b_ada):
    n = N_MOD * D_MODEL
    return pl.pallas_call(
        _ada_kernel,
        grid=(DEPTH, n // TN_ADA),
        in_specs=[
            WHOLE, WHOLE,
            pl.BlockSpec((None, D_MODEL, TN_ADA), lambda l, j: (l, 0, j)),
            pl.BlockSpec((DEPTH, TN_ADA), lambda l, j: (0, j)),
        ],
        out_specs=pl.BlockSpec((None, COND_ROWS, TN_ADA), lambda l, j: (l, 0, j)),
        out_shape=jax.ShapeDtypeStruct((DEPTH, COND_ROWS, n), F32),
        compiler_params=_params(("arbitrary", "arbitrary")),
        name="ada_mod",
    )(c_ctx.reshape(1, D_MODEL), c, w_ada, b_ada)


def _in_weight_pieces(layer):
    pieces = []
    for c0 in range(0, IN_COLS, STAGE_W_COLS):
        g, off = divmod(c0, TN_IN)
        pieces.append((0, layer, slice(0, D_MODEL), slice(c0, c0 + STAGE_W_COLS),
                       g, slice(0, D_MODEL), slice(off, off + STAGE_W_COLS)))
    return pieces


def _in_kernel(*refs, layer, aliased):
    xa_ref, xb_ref, mod_ref, g_ref, w_hbm, cos_ref, sin_ref, cw_ref = refs[:8]
    refs = refs[10:] if aliased else refs[8:]
    q_ref, v_ref, kt_ref, conv_ref, nk_ref, nv_ref, h_s, wb_s, gb_s, gc_s, stage_s, sem = refs
    i = pl.program_id(0)

    @pl.when(i == 0)
    def _():
        _load_weights(_in_weight_pieces(layer), (w_hbm,), (wb_s,), stage_s, sem)

    chunks = [slice(c * ROW_CHUNK, (c + 1) * ROW_CHUNK) for c in range(TM // ROW_CHUNK)]
    seqs_per_chunk = ROW_CHUNK // SEQ

    def norm(x_ref, rows):
        gain = g_ref[layer:layer + 1, :] * (1.0 + _mod(mod_ref, layer, i, 1))
        h_s[rows, :] = (_rms(x_ref[rows, :]) * gain + _mod(mod_ref, layer, i, 0)).astype(BF)

    def proj(rows, group):
        acc = jnp.dot(h_s[rows, :], wb_s[group], preferred_element_type=F32)
        return acc[:, :ATT_WIDTH], acc[:, ATT_WIDTH:]

    def roped(a, rows):
        cos = jnp.concatenate([cos_ref[rows, :]] * N_HEADS, axis=1)
        sin = jnp.concatenate([sin_ref[rows, :]] * N_HEADS, axis=1)
        lane = lax.broadcasted_iota(jnp.int32, a.shape, 1)
        upper = (lane & (QK_DIM // 4)) != 0
        partner = jnp.where(upper, pltpu.roll(a, QK_DIM // 4, 1), pltpu.roll(a, ATT_WIDTH - QK_DIM // 4, 1))
        return a * cos + partner * sin

    def gated_conv(seq):
        for rows in chunks:
            gc, xin = proj(rows, 2)
            gc_s[rows, :] = gc * xin
        u = gc_s[...]
        pos = lax.broadcasted_iota(jnp.int32, (TM, 1), 0) & (seq - 1)
        prev = jnp.where(pos == 0, 0.0, pltpu.roll(u, 1, 0))
        nxt = jnp.where(pos == seq - 1, 0.0, pltpu.roll(u, TM - 1, 0))
        cw = cw_ref[layer]
        conv = prev * cw[0:1] + u * cw[1:2] + nxt * cw[2:3]
        conv_ref[...] = (gb_s[...] * conv).astype(BF)

    @pl.when(i >= CTX_TILES)
    def _():
        for rows in chunks:
            norm(xb_ref, rows)
            q, k = proj(rows, 0)
            q_ref[rows, :] = (roped(q, rows) * Q_SCALE).astype(BF)
            kt_ref[:, rows] = roped(k, rows).T.astype(BF)
        for rows in chunks:
            v, gb = proj(rows, 1)
            v_ref[rows, :] = v.astype(BF)
            gb_s[rows, :] = gb
        gated_conv(DEC_SEQ)

    @pl.when(i < CTX_TILES)
    def _():
        for c, rows in enumerate(chunks):
            norm(xa_ref, rows)
            q, k = proj(rows, 0)
            q_ref[rows, :] = (q * Q_SCALE).astype(BF)
            kt = k.T
            kt_ref[:, rows] = kt.astype(BF)
            for s in range(seqs_per_chunk):
                nk_ref[c * seqs_per_chunk + s] = kt[:, s * SEQ:(s + 1) * SEQ]
        for c, rows in enumerate(chunks):
            v, gb = proj(rows, 1)
            v_ref[rows, :] = v.astype(BF)
            gb_s[rows, :] = gb
            for s in range(seqs_per_chunk):
                for h in range(N_HEADS):
                    nv_ref[c * seqs_per_chunk + s, pl.ds(h, SEQ, stride=N_HEADS), :] = (
                        v[s * SEQ:(s + 1) * SEQ, h * V_DIM:(h + 1) * V_DIM])
        gated_conv(SEQ)


def _in_proj(layer, xs, mod, g_mix, w_in, cos_t, sin_t, conv_w, new_kv):
    pair = isinstance(xs, tuple)
    xa, xb = xs if pair else (xs, xs)
    spec_a, spec_b = _stream_specs(pair)
    ctx_i = lambda i: jnp.minimum(i, CTX_TILES - 1)
    in_specs = [
        spec_a, spec_b,
        WHOLE, WHOLE,
        pl.BlockSpec(memory_space=pl.ANY),
        WHOLE, WHOLE, WHOLE,
    ]
    args = [xa, xb, mod, g_mix, w_in, cos_t, sin_t, conv_w]
    aliases = {}
    if new_kv is not None:
        in_specs += [pl.BlockSpec(memory_space=pl.ANY), pl.BlockSpec(memory_space=pl.ANY)]
        args += list(new_kv)
        aliases = {8: 4, 9: 5}
    row_tile = pl.BlockSpec((TM, ATT_WIDTH), lambda i: (i, 0))
    return pl.pallas_call(
        functools.partial(_in_kernel, layer=layer, aliased=new_kv is not None),
        grid=(N_TILES,),
        in_specs=in_specs,
        out_specs=[
            row_tile,
            row_tile,
            pl.BlockSpec((None, ATT_WIDTH, TM), lambda i: (i, 0, 0)),
            row_tile,
            pl.BlockSpec((SEQ_PER_TILE, None, ATT_WIDTH, SEQ), lambda i: (ctx_i(i), layer, 0, 0)),
            pl.BlockSpec((SEQ_PER_TILE, None, SEQ * N_HEADS, V_DIM), lambda i: (ctx_i(i), layer, 0, 0)),
        ],
        out_shape=[
            jax.ShapeDtypeStruct((N_TOK, ATT_WIDTH), BF),
            jax.ShapeDtypeStruct((N_TOK, ATT_WIDTH), BF),
            jax.ShapeDtypeStruct((N_TILES, ATT_WIDTH, TM), BF),
            jax.ShapeDtypeStruct((N_TOK, CONV_WIDTH), BF),
            jax.ShapeDtypeStruct((BATCH, DEPTH, ATT_WIDTH, SEQ), F32),
            jax.ShapeDtypeStruct((BATCH, DEPTH, SEQ * N_HEADS, V_DIM), F32),
        ],
        scratch_shapes=[
            pltpu.VMEM((TM, D_MODEL), BF),
            pltpu.VMEM((N_IN_TILES, D_MODEL, TN_IN), BF),
            pltpu.VMEM((TM, CONV_WIDTH), F32),
            pltpu.VMEM((TM, CONV_WIDTH), F32),
            pltpu.VMEM((2, STAGE_W_ROWS, STAGE_W_COLS), F32),
            pltpu.SemaphoreType.DMA((2,)),
        ],
        input_output_aliases=aliases,
        compiler_params=_params(("arbitrary",)),
        name=f"in_proj_l{layer}",
    )(*args)


def _lambda(lq_ref, layer, lam_init):
    lq = lq_ref[layer]
    a = jnp.exp(jnp.sum(lq[0:1] * lq[1:2], axis=-1, keepdims=True))
    b = jnp.exp(jnp.sum(lq[2:3] * lq[3:4], axis=-1, keepdims=True))
    return a - b + lam_init


def _head_norm(o, sg, lam_init):
    return _rms(o) * sg * (1.0 - lam_init)


def _attn_ctx_kernel(q_ref, kt_ref, v_ref, lq_ref, sg_ref, o_ref, sc_s, *, layer, lam_init):
    lam = _lambda(lq_ref, layer, lam_init)
    sg = sg_ref[layer:layer + 1, :]

    def sequence(b, carry):
        pos = pl.ds(pl.multiple_of(b * SEQ, SEQ), SEQ)

        def scores(h):
            for s in range(2):
                d = slice(h * V_DIM + s * QK_DIM, h * V_DIM + (s + 1) * QK_DIM)
                sc_s[h % 2, s] = jnp.dot(q_ref[pos, d], kt_ref[d, pos], preferred_element_type=F32)

        def finish(h):
            cols = slice(h * V_DIM, (h + 1) * V_DIM)
            v = v_ref[pos, cols]
            outs = []
            for s in range(2):
                sc = sc_s[h % 2, s]
                e = jnp.exp2(sc - jnp.max(sc, axis=-1, keepdims=True))
                r = 1.0 / jnp.sum(e, axis=-1, keepdims=True)
                outs.append(jnp.dot(e.astype(BF), v, preferred_element_type=F32) * r)
            o = outs[0] - lam * outs[1]
            o_ref[pos, cols] = _head_norm(o, sg, lam_init).astype(BF)

        scores(0)
        for h in range(N_HEADS):
            if h + 1 < N_HEADS:
                scores(h + 1)
            finish(h)
        return carry

    lax.fori_loop(0, CTX_SEQ_PER_STEP, sequence, 0)


def _attn_ctx(layer, q, kt, v, lambda_qk, subln_g, lam_init):
    rows = CTX_SEQ_PER_STEP * SEQ
    per_tile = TM // rows
    return pl.pallas_call(
        functools.partial(_attn_ctx_kernel, layer=layer, lam_init=lam_init),
        grid=(N_CTX // rows,),
        in_specs=[
            pl.BlockSpec((rows, ATT_WIDTH), lambda b: (b, 0)),
            pl.BlockSpec((None, ATT_WIDTH, rows), lambda b: (b // per_tile, 0, b % per_tile)),
            pl.BlockSpec((rows, ATT_WIDTH), lambda b: (b, 0)),
            WHOLE, WHOLE,
        ],
        out_specs=pl.BlockSpec((rows, ATT_WIDTH), lambda b: (b, 0)),
        out_shape=jax.ShapeDtypeStruct((N_TOK, ATT_WIDTH), BF),
        scratch_shapes=[pltpu.VMEM((2, 2, SEQ, SEQ), F32)],
        compiler_params=_params(("arbitrary",)),
        name=f"attn_ctx_l{layer}",
    )(q, kt, v, lambda_qk, subln_g)


TQ = 256
LAT_Q_PER_STEP = 4


def _attn_lat_kernel(q_ref, kt_ref, v_ref, ckt_ref, cv_ref, lq_ref, sg_ref, att_in_ref, o_ref, sc_s, *,
                     layer, lam_init):
    del att_in_ref
    lam = _lambda(lq_ref, layer, lam_init)
    sg = sg_ref[layer:layer + 1, :]

    units = [(b, h) for b in range(LAT_Q_PER_STEP) for h in range(N_HEADS)]

    def scores(u, s):
        b, h = units[u]
        d = slice(h * V_DIM + s * QK_DIM, h * V_DIM + (s + 1) * QK_DIM)
        q = q_ref[b * TQ:(b + 1) * TQ, d]
        sc_s[u % 2, s, :, :PAST_LEN] = jnp.dot(q, ckt_ref[d, :].astype(BF), preferred_element_type=F32)
        sc_s[u % 2, s, :, PAST_LEN:] = jnp.dot(q, kt_ref[d, :], preferred_element_type=F32)

    def softmax(u, s):
        sc = sc_s[u % 2, s]
        e = jnp.exp2(sc - jnp.max(sc, axis=-1, keepdims=True))
        return e, 1.0 / jnp.sum(e, axis=-1, keepdims=True)

    def finish(u, p1, p2):
        b, h = units[u]
        cols = slice(h * V_DIM, (h + 1) * V_DIM)
        e = jnp.concatenate([p1[0].astype(BF), p2[0].astype(BF)], axis=0)
        vc = cv_ref[pl.ds(h, PAST_LEN, stride=N_HEADS), :].astype(BF)
        pv = jnp.dot(e[:, :PAST_LEN], vc, preferred_element_type=F32)
        pv = pv + jnp.dot(e[:, PAST_LEN:], v_ref[:, cols], preferred_element_type=F32)
        o = pv[:TQ] * p1[1] - pv[TQ:] * (lam * p2[1])
        o_ref[b * TQ:(b + 1) * TQ, cols] = _head_norm(o, sg, lam_init).astype(BF)

    scores(0, 0)
    scores(0, 1)
    for u in range(len(units)):
        more = u + 1 < len(units)
        if more:
            scores(u + 1, 0)
        p1 = softmax(u, 0)
        if more:
            scores(u + 1, 1)
        finish(u, p1, softmax(u, 1))


def _attn_lat(layer, q, kt, v, cache_kt, cache_v, lambda_qk, subln_g, att, lam_init):
    rows = LAT_Q_PER_STEP * TQ
    nqb = DEC_SEQ // rows
    q0 = N_CTX // rows
    return pl.pallas_call(
        functools.partial(_attn_lat_kernel, layer=layer, lam_init=lam_init),
        grid=(DEC_BATCH, nqb),
        in_specs=[
            pl.BlockSpec((rows, ATT_WIDTH), lambda b, t: (q0 + b * nqb + t, 0)),
            pl.BlockSpec((None, ATT_WIDTH, DEC_SEQ), lambda b, t: (CTX_TILES + b, 0, 0)),
            pl.BlockSpec((DEC_SEQ, ATT_WIDTH), lambda b, t: (CTX_TILES + b, 0)),
            pl.BlockSpec((None, None, ATT_WIDTH, PAST_LEN), lambda b, t: (b, layer, 0, 0)),
            pl.BlockSpec((None, None, PAST_LEN * N_HEADS, V_DIM), lambda b, t: (b, layer, 0, 0)),
            WHOLE, WHOLE,
            pl.BlockSpec(memory_space=pl.ANY),
        ],
        out_specs=pl.BlockSpec((rows, ATT_WIDTH), lambda b, t: (q0 + b * nqb + t, 0)),
        out_shape=jax.ShapeDtypeStruct((N_TOK, ATT_WIDTH), BF),
        scratch_shapes=[pltpu.VMEM((2, 2, TQ, PAST_LEN + DEC_SEQ), F32)],
        input_output_aliases={7: 0},
        compiler_params=_params(("arbitrary", "arbitrary")),
        name=f"attn_lat_l{layer}",
    )(q, kt, v, cache_kt, cache_v, lambda_qk, subln_g, att)


def _out_kernel(att_ref, conv_ref, w_ref, xa_ref, xb_ref, mod_ref, gf_ref, *rest, layer, route):
    if route:
        wr_ref, xo_ref, h2_ref, rt_ref, n_ref, wb_s = rest
    else:
        xo_ref, h2_ref, wb_s = rest
    i = pl.program_id(0)

    @pl.when(i == 0)
    def _():
        wb_s[...] = w_ref[...].astype(BF)

    lat = i >= CTX_TILES
    gain = gf_ref[layer:layer + 1, :] * (1.0 + _mod(mod_ref, layer, i, 4))
    for c in range(TM // ROW_CHUNK):
        rows = slice(c * ROW_CHUNK, (c + 1) * ROW_CHUNK)
        mo = jnp.dot(att_ref[rows, :], wb_s[:ATT_WIDTH, :], preferred_element_type=F32)
        mo = mo + jnp.dot(conv_ref[rows, :], wb_s[ATT_WIDTH:, :], preferred_element_type=F32)
        xn = jnp.where(lat, xb_ref[rows, :], xa_ref[rows, :]) + _mod(mod_ref, layer, i, 2) * mo
        xo_ref[rows, :] = xn
        h2_ref[rows, :] = (_rms(xn) * gain + _mod(mod_ref, layer, i, 3)).astype(BF)
    if route:
        _route(h2_ref[...], wr_ref, rt_ref, n_ref)


def _out_proj(layer, att, conv, w_out, xs, mod, g_ffn, w_router_pad=None):
    pair = isinstance(xs, tuple)
    xa, xb = xs if pair else (xs, xs)
    spec_a, spec_b = _stream_specs(pair)
    row_spec = pl.BlockSpec((TM, D_MODEL), lambda i: (i, 0))
    lane_spec = pl.BlockSpec((TM, LANES), lambda i: (i, 0))
    in_specs = [
        pl.BlockSpec((TM, ATT_WIDTH), lambda i: (i, 0)),
        pl.BlockSpec((TM, CONV_WIDTH), lambda i: (i, 0)),
        pl.BlockSpec((None, D_MODEL, D_MODEL), lambda i: (layer, 0, 0)),
        spec_a, spec_b,
        WHOLE, WHOLE,
    ]
    args = [att, conv, w_out, xa, xb, mod, g_ffn]
    out_specs = [row_spec, row_spec]
    out_shape = [jax.ShapeDtypeStruct((N_TOK, D_MODEL), F32), jax.ShapeDtypeStruct((N_TOK, D_MODEL), BF)]
    route = w_router_pad is not None
    if route:
        in_specs.append(WHOLE)
        args.append(w_router_pad)
        out_specs += [lane_spec, pl.BlockSpec((None, SUBLANES, LANES), lambda i: (i, 0, 0))]
        out_shape += [jax.ShapeDtypeStruct((N_TOK, LANES), F32),
                      jax.ShapeDtypeStruct((N_TILES, SUBLANES, LANES), jnp.int32)]
    return pl.pallas_call(
        functools.partial(_out_kernel, layer=layer, route=route),
        grid=(N_TILES,),
        in_specs=in_specs,
        out_specs=out_specs,
        out_shape=out_shape,
        scratch_shapes=[pltpu.VMEM((D_MODEL, D_MODEL), BF)],
        compiler_params=_params(("arbitrary",)),
        name=f"out_proj_l{layer}",
    )(*args)


def _weight_pieces(layer):
    pieces = []
    for c0 in range(0, D_MODEL, STAGE_W_COLS):
        cols = slice(c0, c0 + STAGE_W_COLS)
        pieces.append((0, layer, slice(0, D_MODEL), cols, None, slice(0, D_MODEL), cols))
    for half in range(FF_HALVES):
        for part in range(2):
            src0 = part * D_FF + half * FF_HALF
            for off in range(0, FF_HALF, STAGE_W_COLS):
                n = min(STAGE_W_COLS, FF_HALF - off)
                pieces.append((1, 0, slice(0, D_MODEL), slice(src0 + off, src0 + off + n),
                               half, slice(0, D_MODEL), slice(part * FF_HALF + off, part * FF_HALF + off + n)))
    for r0 in range(0, D_FF, STAGE_W_ROWS):
        rows = slice(r0, min(r0 + STAGE_W_ROWS, D_FF))
        for c0 in range(0, D_MODEL, STAGE_W_COLS):
            cols = slice(c0, c0 + STAGE_W_COLS)
            pieces.append((2, 0, rows, cols, None, rows, cols))
    return pieces


def _load_weights(pieces, hbm, resident, stage_s, sem):
    def copy(k):
        src, idx, rows, cols, _, _, _ = pieces[k]
        nr, nc = rows.stop - rows.start, cols.stop - cols.start
        return pltpu.make_async_copy(hbm[src].at[idx, rows, cols], stage_s.at[k % 2, :nr, :nc], sem.at[k % 2])

    copy(0).start()
    for k, (src, _, rows, cols, didx, drows, dcols) in enumerate(pieces):
        if k + 1 < len(pieces):
            copy(k + 1).start()
        copy(k).wait()
        nr, nc = rows.stop - rows.start, cols.stop - cols.start
        piece = stage_s[k % 2, :nr, :nc].astype(BF)
        if didx is None:
            resident[src][drows, dcols] = piece
        else:
            resident[src][didx, drows, dcols] = piece


def _dense_layer_kernel(att_ref, conv_ref, xa_ref, xb_ref, mod_ref, gf_ref, wo_hbm, wgu_hbm, wd_hbm,
                        o_ref, wo_b, wgu_b, wd_b, stage_s, sem, *, layer):
    i = pl.program_id(0)
    tile = i // (TM // FF_TM)

    @pl.when(i == 0)
    def _():
        _load_weights(_weight_pieces(layer), (wo_hbm, wgu_hbm, wd_hbm), (wo_b, wgu_b, wd_b), stage_s, sem)

    mo = jnp.dot(att_ref[...], wo_b[:ATT_WIDTH, :], preferred_element_type=F32)
    mo = mo + jnp.dot(conv_ref[...], wo_b[ATT_WIDTH:, :], preferred_element_type=F32)
    x = jnp.where(tile >= CTX_TILES, xb_ref[...], xa_ref[...])
    xn = x + _mod(mod_ref, layer, tile, 2) * mo
    h = (_rms(xn) * gf_ref[layer:layer + 1, :]) * (1.0 + _mod(mod_ref, layer, tile, 4)) + _mod(mod_ref, layer, tile, 3)
    h = h.astype(BF)
    y = None
    for half in range(FF_HALVES):
        gu = jnp.dot(h, wgu_b[half], preferred_element_type=F32)
        act = (_silu(gu[:, :FF_HALF]) * gu[:, FF_HALF:]).astype(BF)
        part = jnp.dot(act, wd_b[half * FF_HALF:(half + 1) * FF_HALF, :], preferred_element_type=F32)
        y = part if y is None else y + part
    o_ref[...] = xn + _mod(mod_ref, layer, tile, 5) * y


def _dense_layer(layer, att, conv, xs, mod, g_ffn, w_out, w_gu, w_down):
    xa, xb = xs
    n_ctx = N_CTX // FF_TM
    row = lambda width: pl.BlockSpec((FF_TM, width), lambda i: (i, 0))
    hbm = pl.BlockSpec(memory_space=pl.ANY)
    return pl.pallas_call(
        functools.partial(_dense_layer_kernel, layer=layer),
        grid=(N_TOK // FF_TM,),
        in_specs=[
            row(ATT_WIDTH), row(CONV_WIDTH),
            pl.BlockSpec((FF_TM, D_MODEL), lambda i: (jnp.minimum(i, n_ctx - 1), 0)),
            pl.BlockSpec((FF_TM, D_MODEL), lambda i: (jnp.maximum(i - n_ctx, 0), 0)),
            WHOLE, WHOLE, hbm, hbm, hbm,
        ],
        out_specs=row(D_MODEL),
        out_shape=jax.ShapeDtypeStruct((N_TOK, D_MODEL), F32),
        scratch_shapes=[
            pltpu.VMEM((D_MODEL, D_MODEL), BF),
            pltpu.VMEM((FF_HALVES, D_MODEL, 2 * FF_HALF), BF),
            pltpu.VMEM((D_FF, D_MODEL), BF),
            pltpu.VMEM((2, STAGE_W_ROWS, STAGE_W_COLS), F32),
            pltpu.SemaphoreType.DMA((2,)),
        ],
        compiler_params=_params(("arbitrary",)),
        name="dense_layer",
    )(att, conv, xa, xb, mod, g_ffn, w_out, w_gu, w_down)


def _route(h, wr_ref, rt_ref, n_ref):
    logits = jnp.dot(h, wr_ref[...].astype(BF), preferred_element_type=F32)
    lane = lax.broadcasted_iota(jnp.int32, logits.shape, 1)
    lg = jnp.where(lane < N_EXPERTS, logits, -jnp.inf)
    m1 = jnp.max(lg, axis=-1, keepdims=True)
    i1 = jnp.min(jnp.where(lg == m1, lane, LANES), axis=-1, keepdims=True)
    lg2 = jnp.where(lane == i1, -jnp.inf, lg)
    m2 = jnp.max(lg2, axis=-1, keepdims=True)
    i2 = jnp.min(jnp.where(lg2 == m2, lane, LANES), axis=-1, keepdims=True)
    e2 = jnp.exp(m2 - m1)
    w1 = 1.0 / (1.0 + e2)
    w2 = e2 / (1.0 + e2)

    sel1 = lane == i1
    sel2 = lane == i2
    onehot = jnp.logical_or(sel1, sel2)
    rows = lax.broadcasted_iota(jnp.int32, (ST, ST), 0)
    colsi = lax.broadcasted_iota(jnp.int32, (ST, ST), 1)
    earlier = (colsi < rows).astype(BF)
    onehot_b = onehot.astype(BF)
    before = jnp.concatenate(
        [jnp.dot(earlier, onehot_b[s * ST:(s + 1) * ST], preferred_element_type=F32) for s in range(SUB_PER_TILE)],
        axis=0)
    onehot_f = onehot.astype(F32)
    counts = [jnp.sum(onehot_f[s * ST:(s + 1) * ST], axis=0, keepdims=True) for s in range(SUB_PER_TILE)]
    counts = jnp.concatenate(counts + [jnp.zeros((SUBLANES - SUB_PER_TILE, LANES), F32)], axis=0).astype(jnp.int32)
    seg_len = ((counts + (SEG_ALIGN - 1)) // SEG_ALIGN) * SEG_ALIGN
    n_ref[...] = seg_len
    la = lax.broadcasted_iota(jnp.int32, (LANES, LANES), 0)
    lb = lax.broadcasted_iota(jnp.int32, (LANES, LANES), 1)
    seg_start = jnp.dot(seg_len.astype(F32).astype(BF), (la < lb).astype(BF), preferred_element_type=F32)
    start = jnp.concatenate(
        [jnp.broadcast_to(seg_start[s:s + 1], (ST, LANES)) for s in range(SUB_PER_TILE)], axis=0)
    where = before + start
    lp1 = jnp.sum(jnp.where(sel1, where, 0.0), axis=-1, keepdims=True)
    lp2 = jnp.sum(jnp.where(sel2, where, 0.0), axis=-1, keepdims=True)
    rt = jnp.where(lane == GATE_LANE, w1, jnp.where(lane == GATE_LANE + 1, w2, 0.0))
    rt_ref[...] = jnp.where(lane == ROW_LANE, lp1, jnp.where(lane == ROW_LANE + 1, lp2, rt))


def _chunk_copies(s, cnt_ref, cdst_ref, stage, rows_hbm, sem, *, to_hbm, wait):
    def copy(v, h):
        return pltpu.make_async_copy(v, h, sem) if to_hbm else pltpu.make_async_copy(h, v, sem)

    if wait:
        for z in WAIT_PIECES:
            @pl.when((cnt_ref[s] & z) != 0)
            def _():
                copy(stage.at[pl.ds(0, z * SEG_ALIGN)], rows_hbm.at[pl.ds(0, z * SEG_ALIGN)]).wait()
        return

    def start(c, priority):
        v = stage.at[pl.ds(pl.multiple_of(c * SEG_ALIGN, SEG_ALIGN), SEG_ALIGN)]
        h = rows_hbm.at[pl.ds(pl.multiple_of(cdst_ref[s * STAGE_CHUNKS + c], SEG_ALIGN), SEG_ALIGN)]
        copy(v, h).start(priority=priority)

    def pair(p, carry):
        start(2 * p, 0)
        start(2 * p + 1, 1)
        return carry

    n = cnt_ref[s]
    lax.fori_loop(0, n // 2, pair, 0)

    @pl.when((n & 1) != 0)
    def _():
        start(n - 1, 0)


def _dispatch_kernel(cnt_ref, cdst_ref, h_ref, rt_ref, xs_ref, stage_s, sem):
    copies = functools.partial(_chunk_copies, cnt_ref=cnt_ref, cdst_ref=cdst_ref, rows_hbm=xs_ref, to_hbm=True)
    for k in range(SUB_PER_TILE):
        s = pl.program_id(0) * SUB_PER_TILE + k
        slot = k % 2
        rows = slice(k * ST, (k + 1) * ST)

        @pl.when(s >= 2)
        def _():
            copies(s - 2, stage=stage_s.at[slot], sem=sem.at[slot], wait=True)

        at = rt_ref[rows, :].T[ROW_LANE:ROW_LANE + 2, :].astype(jnp.int32)
        r = lax.broadcasted_iota(jnp.int32, (STAGE_ROWS, ST), 0)
        pick = jnp.logical_or(r == at[0:1, :], r == at[1:2, :]).astype(BF)
        stage_s[slot] = jnp.dot(pick, h_ref[rows, :], preferred_element_type=F32).astype(BF)
        copies(s, stage=stage_s.at[slot], sem=sem.at[slot], wait=False)

    @pl.when(pl.program_id(0) == N_TILES - 1)
    def _():
        copies(N_SUB - 2, stage=stage_s.at[0], sem=sem.at[0], wait=True)
        copies(N_SUB - 1, stage=stage_s.at[1], sem=sem.at[1], wait=True)


def _dispatch(cnt, cdst, h2, rt):
    assert SUB_PER_TILE % 2 == 0
    grid_spec = pltpu.PrefetchScalarGridSpec(
        num_scalar_prefetch=2,
        grid=(N_TILES,),
        in_specs=[
            pl.BlockSpec((TM, D_MODEL), lambda i, *_: (i, 0)),
            pl.BlockSpec((TM, LANES), lambda i, *_: (i, 0)),
        ],
        out_specs=pl.BlockSpec(memory_space=pl.ANY),
        scratch_shapes=[pltpu.VMEM((2, STAGE_ROWS, D_MODEL), BF), pltpu.SemaphoreType.DMA((2,))],
    )
    return pl.pallas_call(
        _dispatch_kernel,
        grid_spec=grid_spec,
        out_shape=jax.ShapeDtypeStruct((R_PAD, D_MODEL), BF),
        compiler_params=_params(("arbitrary",)),
        name="moe_dispatch",
    )(cnt, cdst, h2, rt)


def _expert_weights(r, te_ref, nt_ref, nxt_ref, w_hbm, wf_s, wb_s, sem):
    def fetch(e):
        return pltpu.make_async_copy(w_hbm.at[0, e], wf_s, sem)

    @pl.when(r == 0)
    def _():
        fetch(te_ref[0]).start()

    first = jnp.logical_or(r == 0, te_ref[r] != te_ref[jnp.maximum(r - 1, 0)])

    @pl.when(jnp.logical_and(r < nt_ref[0], first))
    def _():
        fetch(te_ref[r]).wait()
        wb_s[...] = wf_s[...].astype(BF)

        @pl.when(nxt_ref[r] >= 0)
        def _():
            fetch(nxt_ref[r]).start()


def _moe_ffn_kernel(te_ref, nt_ref, nxt_ref, used_ref, x_ref, wgu_hbm, wd_hbm, o_ref,
                    wgu_f, wgu_b, wd_f, wd_b, sem):
    def ffn(rows):
        gu = jnp.dot(x_ref[rows, :], wgu_b[...], preferred_element_type=F32)
        act = (_silu(gu[:, :D_FF_EXPERT]) * gu[:, D_FF_EXPERT:]).astype(BF)
        o_ref[rows, :] = jnp.dot(act, wd_b[...], preferred_element_type=F32).astype(BF)

    for k in range(TILES_PER_STEP):
        r = pl.program_id(0) * TILES_PER_STEP + k
        _expert_weights(r, te_ref, nt_ref, nxt_ref, wgu_hbm, wgu_f, wgu_b, sem.at[0])
        _expert_weights(r, te_ref, nt_ref, nxt_ref, wd_hbm, wd_f, wd_b, sem.at[1])

        @pl.when(jnp.logical_and(r < nt_ref[0], used_ref[r] > TG // 2))
        def _():
            ffn(slice(k * TG, (k + 1) * TG))

        @pl.when(jnp.logical_and(r < nt_ref[0], used_ref[r] <= TG // 2))
        def _():
            ffn(slice(k * TG, k * TG + TG // 2))


def _moe_ffn(te, nt, nxt, used, rows, w_gu, w_down):
    tile_map = lambda i, te, nt, nxt, used: (jnp.minimum(i, (nt[0] - 1) // TILES_PER_STEP), 0)
    grid_spec = pltpu.PrefetchScalarGridSpec(
        num_scalar_prefetch=4,
        grid=(NT_G // TILES_PER_STEP,),
        in_specs=[pl.BlockSpec((TILES_PER_STEP * TG, D_MODEL), tile_map),
                  pl.BlockSpec(memory_space=pl.ANY), pl.BlockSpec(memory_space=pl.ANY)],
        out_specs=pl.BlockSpec((TILES_PER_STEP * TG, D_MODEL), tile_map),
        scratch_shapes=[
            pltpu.VMEM((D_MODEL, 2 * D_FF_EXPERT), F32), pltpu.VMEM((D_MODEL, 2 * D_FF_EXPERT), BF),
            pltpu.VMEM((D_FF_EXPERT, D_MODEL), F32), pltpu.VMEM((D_FF_EXPERT, D_MODEL), BF),
            pltpu.SemaphoreType.DMA((2,)),
        ],
    )
    return pl.pallas_call(
        _moe_ffn_kernel,
        grid_spec=grid_spec,
        out_shape=jax.ShapeDtypeStruct((R_PAD, D_MODEL), BF),
        compiler_params=_params(("arbitrary",)),
        name="moe_ffn",
    )(te, nt, nxt, used, rows, w_gu, w_down)


def _combine_kernel(cnt_ref, cdst_ref, ys_ref, rt_ref, x_ref, mod_ref, fg_ref,
                    oa_ref, ob_ref, stage_s, sem, *, layer):
    i = pl.program_id(0)
    copies = functools.partial(_chunk_copies, cnt_ref=cnt_ref, cdst_ref=cdst_ref, rows_hbm=ys_ref, to_hbm=False)

    @pl.when(i == 0)
    def _():
        stage_s[...] = jnp.zeros_like(stage_s)
        for s in range(COMBINE_AHEAD):
            copies(s, stage=stage_s.at[s], sem=sem.at[s], wait=False)

    for k in range(SUB_PER_TILE):
        s = i * SUB_PER_TILE + k
        slot = k % COMBINE_SLOTS
        ahead = (k + COMBINE_AHEAD) % COMBINE_SLOTS
        rows = slice(k * ST, (k + 1) * ST)

        @pl.when(s + COMBINE_AHEAD < N_SUB)
        def _():
            copies(s + COMBINE_AHEAD, stage=stage_s.at[ahead], sem=sem.at[ahead], wait=False)

        copies(s, stage=stage_s.at[slot], sem=sem.at[slot], wait=True)

        rt = rt_ref[rows, :]
        at = rt[:, ROW_LANE:ROW_LANE + 2].astype(jnp.int32)
        r = lax.broadcasted_iota(jnp.int32, (ST, STAGE_ROWS), 1)
        staged = stage_s[slot]
        a = jnp.dot((r == at[:, 0:1]).astype(BF), staged, preferred_element_type=F32)
        b = jnp.dot((r == at[:, 1:2]).astype(BF), staged, preferred_element_type=F32)
        y = rt[:, GATE_LANE:GATE_LANE + 1] * a + rt[:, GATE_LANE + 1:GATE_LANE + 2] * b
        xn = x_ref[rows, :] + _mod(mod_ref, layer, i, 5) * y
        out = _rms(xn) * fg_ref[...]

        @pl.when(i < CTX_TILES)
        def _():
            oa_ref[rows, :] = out

        @pl.when(i >= CTX_TILES)
        def _():
            ob_ref[rows, :] = out


def _combine(layer, cnt, cdst, ys, rt, x, mod, final_g):
    assert SUB_PER_TILE % COMBINE_SLOTS == 0
    grid_spec = pltpu.PrefetchScalarGridSpec(
        num_scalar_prefetch=2,
        grid=(N_TILES,),
        in_specs=[
            pl.BlockSpec(memory_space=pl.ANY),
            pl.BlockSpec((TM, LANES), lambda i, *_: (i, 0)),
            pl.BlockSpec((TM, D_MODEL), lambda i, *_: (i, 0)),
            WHOLE, WHOLE,
        ],
        out_specs=[
            pl.BlockSpec((TM, D_MODEL), lambda i, *_: (jnp.minimum(i, CTX_TILES - 1), 0)),
            pl.BlockSpec((TM, D_MODEL), lambda i, *_: (jnp.maximum(i - CTX_TILES, 0), 0)),
        ],
        scratch_shapes=[pltpu.VMEM((COMBINE_SLOTS, STAGE_ROWS, D_MODEL), BF),
                        pltpu.SemaphoreType.DMA((COMBINE_SLOTS,))],
    )
    return pl.pallas_call(
        functools.partial(_combine_kernel, layer=layer),
        grid_spec=grid_spec,
        out_shape=[
            jax.ShapeDtypeStruct((N_CTX, D_MODEL), F32),
            jax.ShapeDtypeStruct((N_LAT, D_MODEL), F32),
        ],
        compiler_params=_params(("arbitrary",)),
        name="moe_combine",
    )(cnt, cdst, ys, rt, x, mod, final_g)


def _group_layout(n_tiles):
    n = n_tiles[:, :SUB_PER_TILE, :N_EXPERTS].reshape(N_SUB, N_EXPERTS)
    tiles = (jnp.sum(n, axis=0) + TG - 1) // TG
    tile_end = jnp.cumsum(tiles)
    region = (tile_end - tiles) * TG
    dst = region[None, :] + jnp.cumsum(n, axis=0) - n
    seg_end = jnp.cumsum(n, axis=1)
    seg = seg_end - n
    row = jnp.arange(STAGE_CHUNKS, dtype=jnp.int32) * SEG_ALIGN
    owner = jnp.sum((row[None, :, None] >= seg_end[:, None, :]).astype(jnp.int32), axis=-1)
    own = jnp.minimum(owner, N_EXPERTS - 1)[..., None] == jnp.arange(N_EXPERTS)
    cdst = jnp.sum(jnp.where(own, (dst - seg)[:, None, :], 0), axis=-1) + row[None, :]
    cnt = seg_end[:, -1] // SEG_ALIGN
    nt = tile_end[-1]
    tile_id = jnp.minimum(jnp.arange(NT_G, dtype=jnp.int32), nt - 1)
    te = jnp.sum((tile_id[:, None] >= tile_end[None, :]).astype(jnp.int32), axis=-1)
    after = jnp.sum(jnp.where(te[:, None] == jnp.arange(N_EXPERTS), tile_end[None, :], 0), axis=-1)
    nxt = jnp.where(after < nt, jnp.sum((after[:, None] >= tile_end[None, :]).astype(jnp.int32), axis=-1), -1)
    mine = te[:, None] == jnp.arange(N_EXPERTS)
    region_end = jnp.sum(jnp.where(mine, (region + jnp.sum(n, axis=0))[None, :], 0), axis=-1)
    used = jnp.clip(region_end - tile_id * TG, 0, TG)
    i32 = lambda a: a.astype(jnp.int32)
    return (i32(cnt), i32(cdst.reshape(N_SUB * STAGE_CHUNKS)), i32(te), i32(nt.reshape(1)), i32(nxt), i32(used))


def _rope_tables():
    p = np.arange(DEC_SEQ)
    row = (p // GRID_W).astype(np.float32)
    col = (p % GRID_W).astype(np.float32)
    half = QK_DIM // 4
    freqs = (ROPE_BASE ** (-np.arange(half, dtype=np.float32) / half)).astype(np.float32)
    lane = np.arange(V_DIM)
    f = freqs[lane & (half - 1)]
    use_col = (lane & (2 * half)) != 0
    ang = (np.where(use_col[None, :], col[:, None], row[:, None]) * f[None, :]).astype(np.float32)
    upper = (lane & half) != 0
    sin = np.sin(ang)
    return jnp.asarray(np.cos(ang), F32), jnp.asarray(np.where(upper[None, :], sin, -sin), F32)


def kernel(x_prompt, x_sample, cache_k, cache_v, c, c_ctx, w_ada, b_ada, norm_mix_g, norm_ffn_g,
           w_in, lambda_qk, subln_g, conv_w, w_out, w_gu_dense, w_down_dense, w_router,
           w_gu_moe, w_down_moe, final_g):
    assert DEPTH == 2
    xs = (x_prompt.reshape(N_CTX, D_MODEL), x_sample.reshape(N_LAT, D_MODEL))
    mod = _ada(c_ctx, c, w_ada, b_ada)
    cos_t, sin_t = _rope_tables()
    cache_kt = jnp.transpose(cache_k, (0, 1, 3, 4, 5, 2)).reshape(DEC_BATCH, DEPTH, ATT_WIDTH, PAST_LEN)
    cache_v4 = cache_v.reshape(DEC_BATCH, DEPTH, PAST_LEN * N_HEADS, V_DIM)

    new_kv = None
    for layer in range(DEPTH):
        lam_init = 0.8 - 0.6 * math.exp(-0.3 * layer)
        q, v, kt, conv, nk, nv = _in_proj(layer, xs, mod, norm_mix_g, w_in, cos_t, sin_t, conv_w, new_kv)
        new_kv = (nk, nv)
        att = _attn_ctx(layer, q, kt, v, lambda_qk, subln_g, lam_init)
        att = _attn_lat(layer, q, kt, v, cache_kt, cache_v4, lambda_qk, subln_g, att, lam_init)
        if layer == 0:
            xs = _dense_layer(layer, att, conv, xs, mod, norm_ffn_g, w_out, w_gu_dense, w_down_dense)
        else:
            wr = jnp.pad(w_router[0], ((0, 0), (0, LANES - N_EXPERTS)))
            x1, h2, rt, n_tiles = _out_proj(layer, att, conv, w_out, xs, mod, norm_ffn_g, wr)
            cnt, cdst, te, nt, nxt, used = _group_layout(n_tiles)
            xsort = _dispatch(cnt, cdst, h2, rt)
            ys = _moe_ffn(te, nt, nxt, used, xsort, w_gu_moe, w_down_moe)
            y_ctx, y_lat = _combine(layer, cnt, cdst, ys, rt, x1, mod, final_g.reshape(1, D_MODEL))
    nk, nv = new_kv
    new_k = jnp.transpose(nk.reshape(BATCH, DEPTH, N_HEADS, 2, QK_DIM, SEQ), (0, 1, 5, 2, 3, 4))
    new_v = nv.reshape(BATCH, DEPTH, SEQ, N_HEADS, V_DIM)
    return (y_ctx.reshape(BATCH, SEQ, D_MODEL), y_lat.reshape(DEC_BATCH, DEC_SEQ, D_MODEL), new_k, new_v)
```

```python
import functools
import math

import numpy as np
import jax
import jax.numpy as jnp
from jax import lax
from jax.experimental import pallas as pl
from jax.experimental.pallas import tpu as pltpu

D_MODEL = 1024
BATCH = 16
SEQ = 256
DEPTH = 2
DEC_BATCH = 4
DEC_SEQ = 1024
PAST_LEN = 512
GRID_W = 64
ATT_WIDTH = 512
CONV_WIDTH = 512
N_HEADS = 4
V_DIM = 128
QK_DIM = 64
ROPE_BASE = 10000.0
D_FF = 2816
N_EXPERTS = 8
D_FF_EXPERT = 1408
N_MOD = 6
NORM_EPS = 1e-6
Q_SCALE = QK_DIM ** -0.5 * math.log2(math.e)
IN_COLS = 3 * ATT_WIDTH + 3 * CONV_WIDTH

N_CTX = BATCH * SEQ
N_LAT = DEC_BATCH * DEC_SEQ
N_TOK = N_CTX + N_LAT
TM = 1024
N_TILES = N_TOK // TM
CTX_TILES = N_CTX // TM
SEQ_PER_TILE = TM // SEQ
CTX_SEQ_PER_STEP = 4
COND_ROWS = 8
TN_IN = 1024
N_IN_TILES = IN_COLS // TN_IN
ROW_CHUNK = 512
FF_TM = 512
FF_HALVES = 2
FF_HALF = D_FF // FF_HALVES
STAGE_W_ROWS = 1024
STAGE_W_COLS = 512
TN_ADA = 1536
TG = 512
ST = 256
SUB_PER_TILE = TM // ST
N_SUB = N_TOK // ST
SEG_ALIGN = 8
STAGE_ROWS = 640
STAGE_CHUNKS = STAGE_ROWS // SEG_ALIGN
WAIT_PIECES = (64, 32, 16, 8, 4, 2, 1)
COMBINE_SLOTS = 4
COMBINE_AHEAD = 2
TILES_PER_STEP = 2
_MAX_SORTED_ROWS = 2 * N_TOK + N_SUB * N_EXPERTS * (SEG_ALIGN - 1) + N_EXPERTS * (TG - SEG_ALIGN)
NT_G = -(-_MAX_SORTED_ROWS // (TG * TILES_PER_STEP)) * TILES_PER_STEP
R_PAD = NT_G * TG
LANES = 128
SUBLANES = 8
GATE_LANE = 0
ROW_LANE = 2
VMEM_LIMIT = 60 * 1024 * 1024

BF = jnp.bfloat16
F32 = jnp.float32


def _params(sem, vmem=VMEM_LIMIT):
    return pltpu.CompilerParams(dimension_semantics=sem, vmem_limit_bytes=vmem)


def _mod_row(i):
    return jnp.where(i < CTX_TILES, 0, i - (CTX_TILES - 1))


WHOLE = pl.BlockSpec(memory_space=pltpu.VMEM)


def _mod(mod_ref, layer, i, c):
    return mod_ref[layer, pl.ds(_mod_row(i), 1), c * D_MODEL:(c + 1) * D_MODEL]


def _stream_specs(pair, width=D_MODEL):
    a = pl.BlockSpec((TM, width), lambda i, *_: (jnp.minimum(i, CTX_TILES - 1), 0))
    if pair:
        b = pl.BlockSpec((TM, width), lambda i, *_: (jnp.maximum(i - CTX_TILES, 0), 0))
    else:
        b = pl.BlockSpec((TM, width), lambda i, *_: (jnp.maximum(i, CTX_TILES), 0))
    return a, b


def _silu(x):
    return x / (1.0 + jnp.exp(-x))


def _rms(x):
    return x * lax.rsqrt(jnp.mean(x * x, axis=-1, keepdims=True) + NORM_EPS)


def _ada_kernel(cc_ref, c_ref, w_ref, b_ref, o_ref):
    pad = jnp.zeros((COND_ROWS - 1 - DEC_BATCH, D_MODEL), F32)
    cond = jnp.concatenate([cc_ref[...], c_ref[...], pad], axis=0)
    s = _silu(cond).astype(BF)
    bias = b_ref[pl.ds(pl.program_id(0), 1), :]
    o_ref[...] = jnp.dot(s, w_ref[...].astype(BF), preferred_element_type=F32) + bias


def _ada(c_ctx, c, w_ada, b_ada):
    n = N_MOD * D_MODEL
    return pl.pallas_call(
        _ada_kernel,
        grid=(DEPTH, n // TN_ADA),
        in_specs=[
            WHOLE, WHOLE,
            pl.BlockSpec((None, D_MODEL, TN_ADA), lambda l, j: (l, 0, j)),
            pl.BlockSpec((DEPTH, TN_ADA), lambda l, j: (0, j)),
        ],
        out_specs=pl.BlockSpec((None, COND_ROWS, TN_ADA), lambda l, j: (l, 0, j)),
        out_shape=jax.ShapeDtypeStruct((DEPTH, COND_ROWS, n), F32),
        compiler_params=_params(("arbitrary", "arbitrary")),
        name="ada_mod",
    )(c_ctx.reshape(1, D_MODEL), c, w_ada, b_ada)


def _in_weight_pieces(layer):
    pieces = []
    for c0 in range(0, IN_COLS, STAGE_W_COLS):
        g, off = divmod(c0, TN_IN)
        pieces.append((0, layer, slice(0, D_MODEL), slice(c0, c0 + STAGE_W_COLS),
                       g, slice(0, D_MODEL), slice(off, off + STAGE_W_COLS)))
    return pieces


def _in_kernel(*refs, layer, aliased):
    xa_ref, xb_ref, mod_ref, g_ref, w_hbm, cos_ref, sin_ref, cw_ref = refs[:8]
    refs = refs[10:] if aliased else refs[8:]
    q_ref, v_ref, kt_ref, conv_ref, nk_ref, nv_ref, h_s, wb_s, gb_s, gc_s, stage_s, sem = refs
    i = pl.program_id(0)

    @pl.when(i == 0)
    def _():
        _load_weights(_in_weight_pieces(layer), (w_hbm,), (wb_s,), stage_s, sem)

    chunks = [slice(c * ROW_CHUNK, (c + 1) * ROW_CHUNK) for c in range(TM // ROW_CHUNK)]
    seqs_per_chunk = ROW_CHUNK // SEQ

    def norm(x_ref, rows):
        gain = g_ref[layer:layer + 1, :] * (1.0 + _mod(mod_ref, layer, i, 1))
        h_s[rows, :] = (_rms(x_ref[rows, :]) * gain + _mod(mod_ref, layer, i, 0)).astype(BF)

    def proj(rows, group):
        acc = jnp.dot(h_s[rows, :], wb_s[group], preferred_element_type=F32)
        return acc[:, :ATT_WIDTH], acc[:, ATT_WIDTH:]

    def roped(a, rows):
        cos = jnp.concatenate([cos_ref[rows, :]] * N_HEADS, axis=1)
        sin = jnp.concatenate([sin_ref[rows, :]] * N_HEADS, axis=1)
        lane = lax.broadcasted_iota(jnp.int32, a.shape, 1)
        upper = (lane & (QK_DIM // 4)) != 0
        partner = jnp.where(upper, pltpu.roll(a, QK_DIM // 4, 1), pltpu.roll(a, ATT_WIDTH - QK_DIM // 4, 1))
        return a * cos + partner * sin

    def gated_conv(seq):
        for rows in chunks:
            gc, xin = proj(rows, 2)
            gc_s[rows, :] = gc * xin
        u = gc_s[...]
        pos = lax.broadcasted_iota(jnp.int32, (TM, 1), 0) & (seq - 1)
        prev = jnp.where(pos == 0, 0.0, pltpu.roll(u, 1, 0))
        nxt = jnp.where(pos == seq - 1, 0.0, pltpu.roll(u, TM - 1, 0))
        cw = cw_ref[layer]
        conv = prev * cw[0:1] + u * cw[1:2] + nxt * cw[2:3]
        conv_ref[...] = (gb_s[...] * conv).astype(BF)

    @pl.when(i >= CTX_TILES)
    def _():
        for rows in chunks:
            norm(xb_ref, rows)
            q, k = proj(rows, 0)
            q_ref[rows, :] = (roped(q, rows) * Q_SCALE).astype(BF)
            kt_ref[:, rows] = roped(k, rows).T.astype(BF)
        for rows in chunks:
            v, gb = proj(rows, 1)
            v_ref[rows, :] = v.astype(BF)
            gb_s[rows, :] = gb
        gated_conv(DEC_SEQ)

    @pl.when(i < CTX_TILES)
    def _():
        for c, rows in enumerate(chunks):
            norm(xa_ref, rows)
            q, k = proj(rows, 0)
            q_ref[rows, :] = (q * Q_SCALE).astype(BF)
            kt = k.T
            kt_ref[:, rows] = kt.astype(BF)
            for s in range(seqs_per_chunk):
                nk_ref[c * seqs_per_chunk + s] = kt[:, s * SEQ:(s + 1) * SEQ]
        for c, rows in enumerate(chunks):
            v, gb = proj(rows, 1)
            v_ref[rows, :] = v.astype(BF)
            gb_s[rows, :] = gb
            for s in range(seqs_per_chunk):
                for h in range(N_HEADS):
                    nv_ref[c * seqs_per_chunk + s, pl.ds(h, SEQ, stride=N_HEADS), :] = (
                        v[s * SEQ:(s + 1) * SEQ, h * V_DIM:(h + 1) * V_DIM])
        gated_conv(SEQ)


def _in_proj(layer, xs, mod, g_mix, w_in, cos_t, sin_t, conv_w, new_kv):
    pair = isinstance(xs, tuple)
    xa, xb = xs if pair else (xs, xs)
    spec_a, spec_b = _stream_specs(pair)
    ctx_i = lambda i: jnp.minimum(i, CTX_TILES - 1)
    in_specs = [
        spec_a, spec_b,
        WHOLE, WHOLE,
        pl.BlockSpec(memory_space=pl.ANY),
        WHOLE, WHOLE, WHOLE,
    ]
    args = [xa, xb, mod, g_mix, w_in, cos_t, sin_t, conv_w]
    aliases = {}
    if new_kv is not None:
        in_specs += [pl.BlockSpec(memory_space=pl.ANY), pl.BlockSpec(memory_space=pl.ANY)]
        args += list(new_kv)
        aliases = {8: 4, 9: 5}
    row_tile = pl.BlockSpec((TM, ATT_WIDTH), lambda i: (i, 0))
    return pl.pallas_call(
        functools.partial(_in_kernel, layer=layer, aliased=new_kv is not None),
        grid=(N_TILES,),
        in_specs=in_specs,
        out_specs=[
            row_tile,
            row_tile,
            pl.BlockSpec((None, ATT_WIDTH, TM), lambda i: (i, 0, 0)),
            row_tile,
            pl.BlockSpec((SEQ_PER_TILE, None, ATT_WIDTH, SEQ), lambda i: (ctx_i(i), layer, 0, 0)),
            pl.BlockSpec((SEQ_PER_TILE, None, SEQ * N_HEADS, V_DIM), lambda i: (ctx_i(i), layer, 0, 0)),
        ],
        out_shape=[
            jax.ShapeDtypeStruct((N_TOK, ATT_WIDTH), BF),
            jax.ShapeDtypeStruct((N_TOK, ATT_WIDTH), BF),
            jax.ShapeDtypeStruct((N_TILES, ATT_WIDTH, TM), BF),
            jax.ShapeDtypeStruct((N_TOK, CONV_WIDTH), BF),
            jax.ShapeDtypeStruct((BATCH, DEPTH, ATT_WIDTH, SEQ), F32),
            jax.ShapeDtypeStruct((BATCH, DEPTH, SEQ * N_HEADS, V_DIM), F32),
        ],
        scratch_shapes=[
            pltpu.VMEM((TM, D_MODEL), BF),
            pltpu.VMEM((N_IN_TILES, D_MODEL, TN_IN), BF),
            pltpu.VMEM((TM, CONV_WIDTH), F32),
            pltpu.VMEM((TM, CONV_WIDTH), F32),
            pltpu.VMEM((2, STAGE_W_ROWS, STAGE_W_COLS), F32),
            pltpu.SemaphoreType.DMA((2,)),
        ],
        input_output_aliases=aliases,
        compiler_params=_params(("arbitrary",)),
        name=f"in_proj_l{layer}",
    )(*args)


def _lambda(lq_ref, layer, lam_init):
    lq = lq_ref[layer]
    a = jnp.exp(jnp.sum(lq[0:1] * lq[1:2], axis=-1, keepdims=True))
    b = jnp.exp(jnp.sum(lq[2:3] * lq[3:4], axis=-1, keepdims=True))
    return a - b + lam_init


def _head_norm(o, sg, lam_init):
    return _rms(o) * sg * (1.0 - lam_init)


def _attn_ctx_kernel(q_ref, kt_ref, v_ref, lq_ref, sg_ref, o_ref, sc_s, *, layer, lam_init):
    lam = _lambda(lq_ref, layer, lam_init)
    sg = sg_ref[layer:layer + 1, :]

    def sequence(b, carry):
        pos = pl.ds(pl.multiple_of(b * SEQ, SEQ), SEQ)

        def scores(h):
            for s in range(2):
                d = slice(h * V_DIM + s * QK_DIM, h * V_DIM + (s + 1) * QK_DIM)
                sc_s[h % 2, s] = jnp.dot(q_ref[pos, d], kt_ref[d, pos], preferred_element_type=F32)

        def finish(h):
            cols = slice(h * V_DIM, (h + 1) * V_DIM)
            v = v_ref[pos, cols]
            outs = []
            for s in range(2):
                sc = sc_s[h % 2, s]
                e = jnp.exp2(sc - jnp.max(sc, axis=-1, keepdims=True))
                r = 1.0 / jnp.sum(e, axis=-1, keepdims=True)
                outs.append(jnp.dot(e.astype(BF), v, preferred_element_type=F32) * r)
            o = outs[0] - lam * outs[1]
            o_ref[pos, cols] = _head_norm(o, sg, lam_init).astype(BF)

        scores(0)
        for h in range(N_HEADS):
            if h + 1 < N_HEADS:
                scores(h + 1)
            finish(h)
        return carry

    lax.fori_loop(0, CTX_SEQ_PER_STEP, sequence, 0)


def _attn_ctx(layer, q, kt, v, lambda_qk, subln_g, lam_init):
    rows = CTX_SEQ_PER_STEP * SEQ
    per_tile = TM // rows
    return pl.pallas_call(
        functools.partial(_attn_ctx_kernel, layer=layer, lam_init=lam_init),
        grid=(N_CTX // rows,),
        in_specs=[
            pl.BlockSpec((rows, ATT_WIDTH), lambda b: (b, 0)),
            pl.BlockSpec((None, ATT_WIDTH, rows), lambda b: (b // per_tile, 0, b % per_tile)),
            pl.BlockSpec((rows, ATT_WIDTH), lambda b: (b, 0)),
            WHOLE, WHOLE,
        ],
        out_specs=pl.BlockSpec((rows, ATT_WIDTH), lambda b: (b, 0)),
        out_shape=jax.ShapeDtypeStruct((N_TOK, ATT_WIDTH), BF),
        scratch_shapes=[pltpu.VMEM((2, 2, SEQ, SEQ), F32)],
        compiler_params=_params(("arbitrary",)),
        name=f"attn_ctx_l{layer}",
    )(q, kt, v, lambda_qk, subln_g)


TQ = 256
LAT_Q_PER_STEP = 2


def _attn_lat_kernel(q_ref, kt_ref, v_ref, ckt_ref, cv_ref, lq_ref, sg_ref, att_in_ref, o_ref, sc_s, *,
                     layer, lam_init):
    del att_in_ref
    lam = _lambda(lq_ref, layer, lam_init)
    sg = sg_ref[layer:layer + 1, :]

    units = [(b, h) for b in range(LAT_Q_PER_STEP) for h in range(N_HEADS)]

    def scores(u, s):
        b, h = units[u]
        d = slice(h * V_DIM + s * QK_DIM, h * V_DIM + (s + 1) * QK_DIM)
        q = q_ref[b * TQ:(b + 1) * TQ, d]
        sc_s[u % 2, s, :, :PAST_LEN] = jnp.dot(q, ckt_ref[d, :].astype(BF), preferred_element_type=F32)
        sc_s[u % 2, s, :, PAST_LEN:] = jnp.dot(q, kt_ref[d, :], preferred_element_type=F32)

    def softmax(u, s):
        sc = sc_s[u % 2, s]
        e = jnp.exp2(sc - jnp.max(sc, axis=-1, keepdims=True))
        return e, 1.0 / jnp.sum(e, axis=-1, keepdims=True)

    def finish(u, p1, p2):
        b, h = units[u]
        cols = slice(h * V_DIM, (h + 1) * V_DIM)
        e = jnp.concatenate([p1[0].astype(BF), p2[0].astype(BF)], axis=0)
        vc = cv_ref[pl.ds(h, PAST_LEN, stride=N_HEADS), :].astype(BF)
        pv = jnp.dot(e[:, :PAST_LEN], vc, preferred_element_type=F32)
        pv = pv + jnp.dot(e[:, PAST_LEN:], v_ref[:, cols], preferred_element_type=F32)
        o = pv[:TQ] * p1[1] - pv[TQ:] * (lam * p2[1])
        o_ref[b * TQ:(b + 1) * TQ, cols] = _head_norm(o, sg, lam_init).astype(BF)

    scores(0, 0)
    scores(0, 1)
    for u in range(len(units)):
        more = u + 1 < len(units)
        if more:
            scores(u + 1, 0)
        p1 = softmax(u, 0)
        if more:
            scores(u + 1, 1)
        finish(u, p1, softmax(u, 1))


def _attn_lat(layer, q, kt, v, cache_kt, cache_v, lambda_qk, subln_g, att, lam_init):
    rows = LAT_Q_PER_STEP * TQ
    nqb = DEC_SEQ // rows
    q0 = N_CTX // rows
    return pl.pallas_call(
        functools.partial(_attn_lat_kernel, layer=layer, lam_init=lam_init),
        grid=(DEC_BATCH, nqb),
        in_specs=[
            pl.BlockSpec((rows, ATT_WIDTH), lambda b, t: (q0 + b * nqb + t, 0)),
            pl.BlockSpec((None, ATT_WIDTH, DEC_SEQ), lambda b, t: (CTX_TILES + b, 0, 0)),
            pl.BlockSpec((DEC_SEQ, ATT_WIDTH), lambda b, t: (CTX_TILES + b, 0)),
            pl.BlockSpec((None, None, ATT_WIDTH, PAST_LEN), lambda b, t: (b, layer, 0, 0)),
            pl.BlockSpec((None, None, PAST_LEN * N_HEADS, V_DIM), lambda b, t: (b, layer, 0, 0)),
            WHOLE, WHOLE,
            pl.BlockSpec(memory_space=pl.ANY),
        ],
        out_specs=pl.BlockSpec((rows, ATT_WIDTH), lambda b, t: (q0 + b * nqb + t, 0)),
        out_shape=jax.ShapeDtypeStruct((N_TOK, ATT_WIDTH), BF),
        scratch_shapes=[pltpu.VMEM((2, 2, TQ, PAST_LEN + DEC_SEQ), F32)],
        input_output_aliases={7: 0},
        compiler_params=_params(("arbitrary", "arbitrary")),
        name=f"attn_lat_l{layer}",
    )(q, kt, v, cache_kt, cache_v, lambda_qk, subln_g, att)


def _out_kernel(att_ref, conv_ref, w_ref, xa_ref, xb_ref, mod_ref, gf_ref, *rest, layer, route):
    if route:
        wr_ref, xo_ref, h2_ref, rt_ref, n_ref, wb_s = rest
    else:
        xo_ref, h2_ref, wb_s = rest
    i = pl.program_id(0)

    @pl.when(i == 0)
    def _():
        wb_s[...] = w_ref[...].astype(BF)

    lat = i >= CTX_TILES
    gain = gf_ref[layer:layer + 1, :] * (1.0 + _mod(mod_ref, layer, i, 4))
    for c in range(TM // ROW_CHUNK):
        rows = slice(c * ROW_CHUNK, (c + 1) * ROW_CHUNK)
        mo = jnp.dot(att_ref[rows, :], wb_s[:ATT_WIDTH, :], preferred_element_type=F32)
        mo = mo + jnp.dot(conv_ref[rows, :], wb_s[ATT_WIDTH:, :], preferred_element_type=F32)
        xn = jnp.where(lat, xb_ref[rows, :], xa_ref[rows, :]) + _mod(mod_ref, layer, i, 2) * mo
        xo_ref[rows, :] = xn
        h2_ref[rows, :] = (_rms(xn) * gain + _mod(mod_ref, layer, i, 3)).astype(BF)
    if route:
        _route(h2_ref[...], wr_ref, rt_ref, n_ref)


def _out_proj(layer, att, conv, w_out, xs, mod, g_ffn, w_router_pad=None):
    pair = isinstance(xs, tuple)
    xa, xb = xs if pair else (xs, xs)
    spec_a, spec_b = _stream_specs(pair)
    row_spec = pl.BlockSpec((TM, D_MODEL), lambda i: (i, 0))
    lane_spec = pl.BlockSpec((TM, LANES), lambda i: (i, 0))
    in_specs = [
        pl.BlockSpec((TM, ATT_WIDTH), lambda i: (i, 0)),
        pl.BlockSpec((TM, CONV_WIDTH), lambda i: (i, 0)),
        pl.BlockSpec((None, D_MODEL, D_MODEL), lambda i: (layer, 0, 0)),
        spec_a, spec_b,
        WHOLE, WHOLE,
    ]
    args = [att, conv, w_out, xa, xb, mod, g_ffn]
    out_specs = [row_spec, row_spec]
    out_shape = [jax.ShapeDtypeStruct((N_TOK, D_MODEL), F32), jax.ShapeDtypeStruct((N_TOK, D_MODEL), BF)]
    route = w_router_pad is not None
    if route:
        in_specs.append(WHOLE)
        args.append(w_router_pad)
        out_specs += [lane_spec, pl.BlockSpec((None, SUBLANES, LANES), lambda i: (i, 0, 0))]
        out_shape += [jax.ShapeDtypeStruct((N_TOK, LANES), F32),
                      jax.ShapeDtypeStruct((N_TILES, SUBLANES, LANES), jnp.int32)]
    return pl.pallas_call(
        functools.partial(_out_kernel, layer=layer, route=route),
        grid=(N_TILES,),
        in_specs=in_specs,
        out_specs=out_specs,
        out_shape=out_shape,
        scratch_shapes=[pltpu.VMEM((D_MODEL, D_MODEL), BF)],
        compiler_params=_params(("arbitrary",)),
        name=f"out_proj_l{layer}",
    )(*args)


def _weight_pieces(layer):
    pieces = []
    for c0 in range(0, D_MODEL, STAGE_W_COLS):
        cols = slice(c0, c0 + STAGE_W_COLS)
        pieces.append((0, layer, slice(0, D_MODEL), cols, None, slice(0, D_MODEL), cols))
    for half in range(FF_HALVES):
        for part in range(2):
            src0 = part * D_FF + half * FF_HALF
            for off in range(0, FF_HALF, STAGE_W_COLS):
                n = min(STAGE_W_COLS, FF_HALF - off)
                pieces.append((1, 0, slice(0, D_MODEL), slice(src0 + off, src0 + off + n),
                               half, slice(0, D_MODEL), slice(part * FF_HALF + off, part * FF_HALF + off + n)))
    for r0 in range(0, D_FF, STAGE_W_ROWS):
        rows = slice(r0, min(r0 + STAGE_W_ROWS, D_FF))
        for c0 in range(0, D_MODEL, STAGE_W_COLS):
            cols = slice(c0, c0 + STAGE_W_COLS)
            pieces.append((2, 0, rows, cols, None, rows, cols))
    return pieces


def _load_weights(pieces, hbm, resident, stage_s, sem):
    def copy(k):
        src, idx, rows, cols, _, _, _ = pieces[k]
        nr, nc = rows.stop - rows.start, cols.stop - cols.start
        return pltpu.make_async_copy(hbm[src].at[idx, rows, cols], stage_s.at[k % 2, :nr, :nc], sem.at[k % 2])

    copy(0).start()
    for k, (src, _, rows, cols, didx, drows, dcols) in enumerate(pieces):
        if k + 1 < len(pieces):
            copy(k + 1).start()
        copy(k).wait()
        nr, nc = rows.stop - rows.start, cols.stop - cols.start
        piece = stage_s[k % 2, :nr, :nc].astype(BF)
        if didx is None:
            resident[src][drows, dcols] = piece
        else:
            resident[src][didx, drows, dcols] = piece


def _dense_layer_kernel(att_ref, conv_ref, xa_ref, xb_ref, mod_ref, gf_ref, wo_hbm, wgu_hbm, wd_hbm,
                        o_ref, wo_b, wgu_b, wd_b, stage_s, sem, *, layer):
    i = pl.program_id(0)
    tile = i // (TM // FF_TM)

    @pl.when(i == 0)
    def _():
        _load_weights(_weight_pieces(layer), (wo_hbm, wgu_hbm, wd_hbm), (wo_b, wgu_b, wd_b), stage_s, sem)

    mo = jnp.dot(att_ref[...], wo_b[:ATT_WIDTH, :], preferred_element_type=F32)
    mo = mo + jnp.dot(conv_ref[...], wo_b[ATT_WIDTH:, :], preferred_element_type=F32)
    x = jnp.where(tile >= CTX_TILES, xb_ref[...], xa_ref[...])
    xn = x + _mod(mod_ref, layer, tile, 2) * mo
    h = (_rms(xn) * gf_ref[layer:layer + 1, :]) * (1.0 + _mod(mod_ref, layer, tile, 4)) + _mod(mod_ref, layer, tile, 3)
    h = h.astype(BF)
    y = None
    for half in range(FF_HALVES):
        gu = jnp.dot(h, wgu_b[half], preferred_element_type=F32)
        act = (_silu(gu[:, :FF_HALF]) * gu[:, FF_HALF:]).astype(BF)
        part = jnp.dot(act, wd_b[half * FF_HALF:(half + 1) * FF_HALF, :], preferred_element_type=F32)
        y = part if y is None else y + part
    o_ref[...] = xn + _mod(mod_ref, layer, tile, 5) * y


def _dense_layer(layer, att, conv, xs, mod, g_ffn, w_out, w_gu, w_down):
    xa, xb = xs
    n_ctx = N_CTX // FF_TM
    row = lambda width: pl.BlockSpec((FF_TM, width), lambda i: (i, 0))
    hbm = pl.BlockSpec(memory_space=pl.ANY)
    return pl.pallas_call(
        functools.partial(_dense_layer_kernel, layer=layer),
        grid=(N_TOK // FF_TM,),
        in_specs=[
            row(ATT_WIDTH), row(CONV_WIDTH),
            pl.BlockSpec((FF_TM, D_MODEL), lambda i: (jnp.minimum(i, n_ctx - 1), 0)),
            pl.BlockSpec((FF_TM, D_MODEL), lambda i: (jnp.maximum(i - n_ctx, 0), 0)),
            WHOLE, WHOLE, hbm, hbm, hbm,
        ],
        out_specs=row(D_MODEL),
        out_shape=jax.ShapeDtypeStruct((N_TOK, D_MODEL), F32),
        scratch_shapes=[
            pltpu.VMEM((D_MODEL, D_MODEL), BF),
            pltpu.VMEM((FF_HALVES, D_MODEL, 2 * FF_HALF), BF),
            pltpu.VMEM((D_FF, D_MODEL), BF),
            pltpu.VMEM((2, STAGE_W_ROWS, STAGE_W_COLS), F32),
            pltpu.SemaphoreType.DMA((2,)),
        ],
        compiler_params=_params(("arbitrary",)),
        name="dense_layer",
    )(att, conv, xa, xb, mod, g_ffn, w_out, w_gu, w_down)


def _route(h, wr_ref, rt_ref, n_ref):
    logits = jnp.dot(h, wr_ref[...].astype(BF), preferred_element_type=F32)
    lane = lax.broadcasted_iota(jnp.int32, logits.shape, 1)
    lg = jnp.where(lane < N_EXPERTS, logits, -jnp.inf)
    m1 = jnp.max(lg, axis=-1, keepdims=True)
    i1 = jnp.min(jnp.where(lg == m1, lane, LANES), axis=-1, keepdims=True)
    lg2 = jnp.where(lane == i1, -jnp.inf, lg)
    m2 = jnp.max(lg2, axis=-1, keepdims=True)
    i2 = jnp.min(jnp.where(lg2 == m2, lane, LANES), axis=-1, keepdims=True)
    e2 = jnp.exp(m2 - m1)
    w1 = 1.0 / (1.0 + e2)
    w2 = e2 / (1.0 + e2)

    sel1 = lane == i1
    sel2 = lane == i2
    onehot = jnp.logical_or(sel1, sel2)
    rows = lax.broadcasted_iota(jnp.int32, (ST, ST), 0)
    colsi = lax.broadcasted_iota(jnp.int32, (ST, ST), 1)
    earlier = (colsi < rows).astype(BF)
    onehot_b = onehot.astype(BF)
    before = jnp.concatenate(
        [jnp.dot(earlier, onehot_b[s * ST:(s + 1) * ST], preferred_element_type=F32) for s in range(SUB_PER_TILE)],
        axis=0)
    onehot_f = onehot.astype(F32)
    counts = [jnp.sum(onehot_f[s * ST:(s + 1) * ST], axis=0, keepdims=True) for s in range(SUB_PER_TILE)]
    counts = jnp.concatenate(counts + [jnp.zeros((SUBLANES - SUB_PER_TILE, LANES), F32)], axis=0).astype(jnp.int32)
    seg_len = ((counts + (SEG_ALIGN - 1)) // SEG_ALIGN) * SEG_ALIGN
    n_ref[...] = seg_len
    la = lax.broadcasted_iota(jnp.int32, (LANES, LANES), 0)
    lb = lax.broadcasted_iota(jnp.int32, (LANES, LANES), 1)
    seg_start = jnp.dot(seg_len.astype(F32).astype(BF), (la < lb).astype(BF), preferred_element_type=F32)
    start = jnp.concatenate(
        [jnp.broadcast_to(seg_start[s:s + 1], (ST, LANES)) for s in range(SUB_PER_TILE)], axis=0)
    where = before + start
    lp1 = jnp.sum(jnp.where(sel1, where, 0.0), axis=-1, keepdims=True)
    lp2 = jnp.sum(jnp.where(sel2, where, 0.0), axis=-1, keepdims=True)
    rt = jnp.where(lane == GATE_LANE, w1, jnp.where(lane == GATE_LANE + 1, w2, 0.0))
    rt_ref[...] = jnp.where(lane == ROW_LANE, lp1, jnp.where(lane == ROW_LANE + 1, lp2, rt))


def _chunk_copies(s, cnt_ref, cdst_ref, stage, rows_hbm, sem, *, to_hbm, wait):
    def copy(v, h):
        return pltpu.make_async_copy(v, h, sem) if to_hbm else pltpu.make_async_copy(h, v, sem)

    if wait:
        for z in WAIT_PIECES:
            @pl.when((cnt_ref[s] & z) != 0)
            def _():
                copy(stage.at[pl.ds(0, z * SEG_ALIGN)], rows_hbm.at[pl.ds(0, z * SEG_ALIGN)]).wait()
        return

    def start(c, priority):
        v = stage.at[pl.ds(pl.multiple_of(c * SEG_ALIGN, SEG_ALIGN), SEG_ALIGN)]
        h = rows_hbm.at[pl.ds(pl.multiple_of(cdst_ref[s * STAGE_CHUNKS + c], SEG_ALIGN), SEG_ALIGN)]
        copy(v, h).start(priority=priority)

    def pair(p, carry):
        start(2 * p, 0)
        start(2 * p + 1, 1)
        return carry

    n = cnt_ref[s]
    lax.fori_loop(0, n // 2, pair, 0)

    @pl.when((n & 1) != 0)
    def _():
        start(n - 1, 0)


def _dispatch_kernel(cnt_ref, cdst_ref, h_ref, rt_ref, xs_ref, stage_s, sem):
    copies = functools.partial(_chunk_copies, cnt_ref=cnt_ref, cdst_ref=cdst_ref, rows_hbm=xs_ref, to_hbm=True)
    for k in range(SUB_PER_TILE):
        s = pl.program_id(0) * SUB_PER_TILE + k
        slot = k % 2
        rows = slice(k * ST, (k + 1) * ST)

        @pl.when(s >= 2)
        def _():
            copies(s - 2, stage=stage_s.at[slot], sem=sem.at[slot], wait=True)

        at = rt_ref[rows, :].T[ROW_LANE:ROW_LANE + 2, :].astype(jnp.int32)
        r = lax.broadcasted_iota(jnp.int32, (STAGE_ROWS, ST), 0)
        pick = jnp.logical_or(r == at[0:1, :], r == at[1:2, :]).astype(BF)
        stage_s[slot] = jnp.dot(pick, h_ref[rows, :], preferred_element_type=F32).astype(BF)
        copies(s, stage=stage_s.at[slot], sem=sem.at[slot], wait=False)

    @pl.when(pl.program_id(0) == N_TILES - 1)
    def _():
        copies(N_SUB - 2, stage=stage_s.at[0], sem=sem.at[0], wait=True)
        copies(N_SUB - 1, stage=stage_s.at[1], sem=sem.at[1], wait=True)


def _dispatch(cnt, cdst, h2, rt):
    assert SUB_PER_TILE % 2 == 0
    grid_spec = pltpu.PrefetchScalarGridSpec(
        num_scalar_prefetch=2,
        grid=(N_TILES,),
        in_specs=[
            pl.BlockSpec((TM, D_MODEL), lambda i, *_: (i, 0)),
            pl.BlockSpec((TM, LANES), lambda i, *_: (i, 0)),
        ],
        out_specs=pl.BlockSpec(memory_space=pl.ANY),
        scratch_shapes=[pltpu.VMEM((2, STAGE_ROWS, D_MODEL), BF), pltpu.SemaphoreType.DMA((2,))],
    )
    return pl.pallas_call(
        _dispatch_kernel,
        grid_spec=grid_spec,
        out_shape=jax.ShapeDtypeStruct((R_PAD, D_MODEL), BF),
        compiler_params=_params(("arbitrary",)),
        name="moe_dispatch",
    )(cnt, cdst, h2, rt)


def _expert_weights(r, te_ref, nt_ref, nxt_ref, w_hbm, wf_s, wb_s, sem):
    def fetch(e):
        return pltpu.make_async_copy(w_hbm.at[0, e], wf_s, sem)

    @pl.when(r == 0)
    def _():
        fetch(te_ref[0]).start()

    first = jnp.logical_or(r == 0, te_ref[r] != te_ref[jnp.maximum(r - 1, 0)])

    @pl.when(jnp.logical_and(r < nt_ref[0], first))
    def _():
        fetch(te_ref[r]).wait()
        wb_s[...] = wf_s[...].astype(BF)

        @pl.when(nxt_ref[r] >= 0)
        def _():
            fetch(nxt_ref[r]).start()


def _moe_ffn_kernel(te_ref, nt_ref, nxt_ref, used_ref, x_ref, wgu_hbm, wd_hbm, o_ref,
                    wgu_f, wgu_b, wd_f, wd_b, sem):
    def ffn(rows):
        gu = jnp.dot(x_ref[rows, :], wgu_b[...], preferred_element_type=F32)
        act = (_silu(gu[:, :D_FF_EXPERT]) * gu[:, D_FF_EXPERT:]).astype(BF)
        o_ref[rows, :] = jnp.dot(act, wd_b[...], preferred_element_type=F32).astype(BF)

    for k in range(TILES_PER_STEP):
        r = pl.program_id(0) * TILES_PER_STEP + k
        _expert_weights(r, te_ref, nt_ref, nxt_ref, wgu_hbm, wgu_f, wgu_b, sem.at[0])
        _expert_weights(r, te_ref, nt_ref, nxt_ref, wd_hbm, wd_f, wd_b, sem.at[1])

        @pl.when(jnp.logical_and(r < nt_ref[0], used_ref[r] > TG // 2))
        def _():
            ffn(slice(k * TG, (k + 1) * TG))

        @pl.when(jnp.logical_and(r < nt_ref[0], used_ref[r] <= TG // 2))
        def _():
            ffn(slice(k * TG, k * TG + TG // 2))


def _moe_ffn(te, nt, nxt, used, rows, w_gu, w_down):
    tile_map = lambda i, te, nt, nxt, used: (jnp.minimum(i, (nt[0] - 1) // TILES_PER_STEP), 0)
    grid_spec = pltpu.PrefetchScalarGridSpec(
        num_scalar_prefetch=4,
        grid=(NT_G // TILES_PER_STEP,),
        in_specs=[pl.BlockSpec((TILES_PER_STEP * TG, D_MODEL), tile_map),
                  pl.BlockSpec(memory_space=pl.ANY), pl.BlockSpec(memory_space=pl.ANY)],
        out_specs=pl.BlockSpec((TILES_PER_STEP * TG, D_MODEL), tile_map),
        scratch_shapes=[
            pltpu.VMEM((D_MODEL, 2 * D_FF_EXPERT), F32), pltpu.VMEM((D_MODEL, 2 * D_FF_EXPERT), BF),
            pltpu.VMEM((D_FF_EXPERT, D_MODEL), F32), pltpu.VMEM((D_FF_EXPERT, D_MODEL), BF),
            pltpu.SemaphoreType.DMA((2,)),
        ],
    )
    return pl.pallas_call(
        _moe_ffn_kernel,
        grid_spec=grid_spec,
        out_shape=jax.ShapeDtypeStruct((R_PAD, D_MODEL), BF),
        compiler_params=_params(("arbitrary",)),
        name="moe_ffn",
    )(te, nt, nxt, used, rows, w_gu, w_down)


def _combine_kernel(cnt_ref, cdst_ref, ys_ref, rt_ref, x_ref, mod_ref, fg_ref,
                    oa_ref, ob_ref, stage_s, sem, *, layer):
    i = pl.program_id(0)
    copies = functools.partial(_chunk_copies, cnt_ref=cnt_ref, cdst_ref=cdst_ref, rows_hbm=ys_ref, to_hbm=False)

    @pl.when(i == 0)
    def _():
        stage_s[...] = jnp.zeros_like(stage_s)
        for s in range(COMBINE_AHEAD):
            copies(s, stage=stage_s.at[s], sem=sem.at[s], wait=False)

    for k in range(SUB_PER_TILE):
        s = i * SUB_PER_TILE + k
        slot = k % COMBINE_SLOTS
        ahead = (k + COMBINE_AHEAD) % COMBINE_SLOTS
        rows = slice(k * ST, (k + 1) * ST)

        @pl.when(s + COMBINE_AHEAD < N_SUB)
        def _():
            copies(s + COMBINE_AHEAD, stage=stage_s.at[ahead], sem=sem.at[ahead], wait=False)

        copies(s, stage=stage_s.at[slot], sem=sem.at[slot], wait=True)

        rt = rt_ref[rows, :]
        at = rt[:, ROW_LANE:ROW_LANE + 2].astype(jnp.int32)
        r = lax.broadcasted_iota(jnp.int32, (ST, STAGE_ROWS), 1)
        staged = stage_s[slot]
        a = jnp.dot((r == at[:, 0:1]).astype(BF), staged, preferred_element_type=F32)
        b = jnp.dot((r == at[:, 1:2]).astype(BF), staged, preferred_element_type=F32)
        y = rt[:, GATE_LANE:GATE_LANE + 1] * a + rt[:, GATE_LANE + 1:GATE_LANE + 2] * b
        xn = x_ref[rows, :] + _mod(mod_ref, layer, i, 5) * y
        out = _rms(xn) * fg_ref[...]

        @pl.when(i < CTX_TILES)
        def _():
            oa_ref[rows, :] = out

        @pl.when(i >= CTX_TILES)
        def _():
            ob_ref[rows, :] = out


def _combine(layer, cnt, cdst, ys, rt, x, mod, final_g):
    assert SUB_PER_TILE % COMBINE_SLOTS == 0
    grid_spec = pltpu.PrefetchScalarGridSpec(
        num_scalar_prefetch=2,
        grid=(N_TILES,),
        in_specs=[
            pl.BlockSpec(memory_space=pl.ANY),
            pl.BlockSpec((TM, LANES), lambda i, *_: (i, 0)),
            pl.BlockSpec((TM, D_MODEL), lambda i, *_: (i, 0)),
            WHOLE, WHOLE,
        ],
        out_specs=[
            pl.BlockSpec((TM, D_MODEL), lambda i, *_: (jnp.minimum(i, CTX_TILES - 1), 0)),
            pl.BlockSpec((TM, D_MODEL), lambda i, *_: (jnp.maximum(i - CTX_TILES, 0), 0)),
        ],
        scratch_shapes=[pltpu.VMEM((COMBINE_SLOTS, STAGE_ROWS, D_MODEL), BF),
                        pltpu.SemaphoreType.DMA((COMBINE_SLOTS,))],
    )
    return pl.pallas_call(
        functools.partial(_combine_kernel, layer=layer),
        grid_spec=grid_spec,
        out_shape=[
            jax.ShapeDtypeStruct((N_CTX, D_MODEL), F32),
            jax.ShapeDtypeStruct((N_LAT, D_MODEL), F32),
        ],
        compiler_params=_params(("arbitrary",)),
        name="moe_combine",
    )(cnt, cdst, ys, rt, x, mod, final_g)


def _group_layout(n_tiles):
    n = n_tiles[:, :SUB_PER_TILE, :N_EXPERTS].reshape(N_SUB, N_EXPERTS)
    tiles = (jnp.sum(n, axis=0) + TG - 1) // TG
    tile_end = jnp.cumsum(tiles)
    region = (tile_end - tiles) * TG
    dst = region[None, :] + jnp.cumsum(n, axis=0) - n
    seg_end = jnp.cumsum(n, axis=1)
    seg = seg_end - n
    row = jnp.arange(STAGE_CHUNKS, dtype=jnp.int32) * SEG_ALIGN
    owner = jnp.sum((row[None, :, None] >= seg_end[:, None, :]).astype(jnp.int32), axis=-1)
    own = jnp.minimum(owner, N_EXPERTS - 1)[..., None] == jnp.arange(N_EXPERTS)
    cdst = jnp.sum(jnp.where(own, (dst - seg)[:, None, :], 0), axis=-1) + row[None, :]
    cnt = seg_end[:, -1] // SEG_ALIGN
    nt = tile_end[-1]
    tile_id = jnp.minimum(jnp.arange(NT_G, dtype=jnp.int32), nt - 1)
    te = jnp.sum((tile_id[:, None] >= tile_end[None, :]).astype(jnp.int32), axis=-1)
    after = jnp.sum(jnp.where(te[:, None] == jnp.arange(N_EXPERTS), tile_end[None, :], 0), axis=-1)
    nxt = jnp.where(after < nt, jnp.sum((after[:, None] >= tile_end[None, :]).astype(jnp.int32), axis=-1), -1)
    mine = te[:, None] == jnp.arange(N_EXPERTS)
    region_end = jnp.sum(jnp.where(mine, (region + jnp.sum(n, axis=0))[None, :], 0), axis=-1)
    used = jnp.clip(region_end - tile_id * TG, 0, TG)
    i32 = lambda a: a.astype(jnp.int32)
    return (i32(cnt), i32(cdst.reshape(N_SUB * STAGE_CHUNKS)), i32(te), i32(nt.reshape(1)), i32(nxt), i32(used))


def _rope_tables():
    p = np.arange(DEC_SEQ)
    row = (p // GRID_W).astype(np.float32)
    col = (p % GRID_W).astype(np.float32)
    half = QK_DIM // 4
    freqs = (ROPE_BASE ** (-np.arange(half, dtype=np.float32) / half)).astype(np.float32)
    lane = np.arange(V_DIM)
    f = freqs[lane & (half - 1)]
    use_col = (lane & (2 * half)) != 0
    ang = (np.where(use_col[None, :], col[:, None], row[:, None]) * f[None, :]).astype(np.float32)
    upper = (lane & half) != 0
    sin = np.sin(ang)
    return jnp.asarray(np.cos(ang), F32), jnp.asarray(np.where(upper[None, :], sin, -sin), F32)


def kernel(x_prompt, x_sample, cache_k, cache_v, c, c_ctx, w_ada, b_ada, norm_mix_g, norm_ffn_g,
           w_in, lambda_qk, subln_g, conv_w, w_out, w_gu_dense, w_down_dense, w_router,
           w_gu_moe, w_down_moe, final_g):
    assert DEPTH == 2
    xs = (x_prompt.reshape(N_CTX, D_MODEL), x_sample.reshape(N_LAT, D_MODEL))
    mod = _ada(c_ctx, c, w_ada, b_ada)
    cos_t, sin_t = _rope_tables()
    cache_kt = jnp.transpose(cache_k, (0, 1, 3, 4, 5, 2)).reshape(DEC_BATCH, DEPTH, ATT_WIDTH, PAST_LEN)
    cache_v4 = cache_v.reshape(DEC_BATCH, DEPTH, PAST_LEN * N_HEADS, V_DIM)

    new_kv = None
    for layer in range(DEPTH):
        lam_init = 0.8 - 0.6 * math.exp(-0.3 * layer)
        q, v, kt, conv, nk, nv = _in_proj(layer, xs, mod, norm_mix_g, w_in, cos_t, sin_t, conv_w, new_kv)
        new_kv = (nk, nv)
        att = _attn_ctx(layer, q, kt, v, lambda_qk, subln_g, lam_init)
        att = _attn_lat(layer, q, kt, v, cache_kt, cache_v4, lambda_qk, subln_g, att, lam_init)
        if layer == 0:
            xs = _dense_layer(layer, att, conv, xs, mod, norm_ffn_g, w_out, w_gu_dense, w_down_dense)
        else:
            wr = jnp.pad(w_router[0], ((0, 0), (0, LANES - N_EXPERTS)))
            x1, h2, rt, n_tiles = _out_proj(layer, att, conv, w_out, xs, mod, norm_ffn_g, wr)
            cnt, cdst, te, nt, nxt, used = _group_layout(n_tiles)
            xsort = _dispatch(cnt, cdst, h2, rt)
            ys = _moe_ffn(te, nt, nxt, used, xsort, w_gu_moe, w_down_moe)
            y_ctx, y_lat = _combine(layer, cnt, cdst, ys, rt, x1, mod, final_g.reshape(1, D_MODEL))
    nk, nv = new_kv
    new_k = jnp.transpose(nk.reshape(BATCH, DEPTH, N_HEADS, 2, QK_DIM, SEQ), (0, 1, 5, 2, 3, 4))
    new_v = nv.reshape(BATCH, DEPTH, SEQ, N_HEADS, V_DIM)
    return (y_ctx.reshape(BATCH, SEQ, D_MODEL), y_lat.reshape(DEC_BATCH, DEC_SEQ, D_MODEL), new_k, new_v)
```

```python
import functools
import math

import numpy as np
import jax
import jax.numpy as jnp
from jax import lax
from jax.experimental import pallas as pl
from jax.experimental.pallas import tpu as pltpu

D_MODEL = 1024
BATCH = 16
SEQ = 256
DEPTH = 2
DEC_BATCH = 4
DEC_SEQ = 1024
PAST_LEN = 512
GRID_W = 64
ATT_WIDTH = 512
CONV_WIDTH = 512
N_HEADS = 4
V_DIM = 128
QK_DIM = 64
ROPE_BASE = 10000.0
D_FF = 2816
N_EXPERTS = 8
D_FF_EXPERT = 1408
N_MOD = 6
NORM_EPS = 1e-6
Q_SCALE = QK_DIM ** -0.5 * math.log2(math.e)
IN_COLS = 3 * ATT_WIDTH + 3 * CONV_WIDTH

N_CTX = BATCH * SEQ
N_LAT = DEC_BATCH * DEC_SEQ
N_TOK = N_CTX + N_LAT
TM = 1024
N_TILES = N_TOK // TM
CTX_TILES = N_CTX // TM
SEQ_PER_TILE = TM // SEQ
CTX_SEQ_PER_STEP = 4
COND_ROWS = 8
TN_IN = 1024
N_IN_TILES = IN_COLS // TN_IN
ROW_CHUNK = 512
FF_TM = 512
FF_HALVES = 2
FF_HALF = D_FF // FF_HALVES
STAGE_W_ROWS = 1024
STAGE_W_COLS = 512
FF_STAGE_SLOTS = 4
TN_ADA = 1536
TG = 512
ST = 256
SUB_PER_TILE = TM // ST
N_SUB = N_TOK // ST
SEG_ALIGN = 8
STAGE_ROWS = 640
STAGE_CHUNKS = STAGE_ROWS // SEG_ALIGN
WAIT_PIECES = (64, 32, 16, 8, 4, 2, 1)
COMBINE_SLOTS = 4
COMBINE_AHEAD = 2
TILES_PER_STEP = 2
_MAX_SORTED_ROWS = 2 * N_TOK + N_SUB * N_EXPERTS * (SEG_ALIGN - 1) + N_EXPERTS * (TG - SEG_ALIGN)
NT_G = -(-_MAX_SORTED_ROWS // (TG * TILES_PER_STEP)) * TILES_PER_STEP
R_PAD = NT_G * TG
LANES = 128
SUBLANES = 8
GATE_LANE = 0
ROW_LANE = 2
VMEM_LIMIT = 60 * 1024 * 1024

BF = jnp.bfloat16
F32 = jnp.float32


def _params(sem, vmem=VMEM_LIMIT):
    return pltpu.CompilerParams(dimension_semantics=sem, vmem_limit_bytes=vmem)


def _mod_row(i):
    return jnp.where(i < CTX_TILES, 0, i - (CTX_TILES - 1))


WHOLE = pl.BlockSpec(memory_space=pltpu.VMEM)


def _mod(mod_ref, layer, i, c):
    return mod_ref[layer, pl.ds(_mod_row(i), 1), c * D_MODEL:(c + 1) * D_MODEL]


def _stream_specs(pair, width=D_MODEL):
    a = pl.BlockSpec((TM, width), lambda i, *_: (jnp.minimum(i, CTX_TILES - 1), 0))
    if pair:
        b = pl.BlockSpec((TM, width), lambda i, *_: (jnp.maximum(i - CTX_TILES, 0), 0))
    else:
        b = pl.BlockSpec((TM, width), lambda i, *_: (jnp.maximum(i, CTX_TILES), 0))
    return a, b


def _silu(x):
    return x / (1.0 + jnp.exp(-x))


def _rms(x):
    return x * lax.rsqrt(jnp.mean(x * x, axis=-1, keepdims=True) + NORM_EPS)


def _ada_kernel(cc_ref, c_ref, w_ref, b_ref, o_ref):
    pad = jnp.zeros((COND_ROWS - 1 - DEC_BATCH, D_MODEL), F32)
    cond = jnp.concatenate([cc_ref[...], c_ref[...], pad], axis=0)
    s = _silu(cond).astype(BF)
    bias = b_ref[pl.ds(pl.program_id(0), 1), :]
    o_ref[...] = jnp.dot(s, w_ref[...].astype(BF), preferred_element_type=F32) + bias


def _ada(c_ctx, c, w_ada, b_ada):
    n = N_MOD * D_MODEL
    return pl.pallas_call(
        _ada_kernel,
        grid=(DEPTH, n // TN_ADA),
        in_specs=[
            WHOLE, WHOLE,
            pl.BlockSpec((None, D_MODEL, TN_ADA), lambda l, j: (l, 0, j)),
            pl.BlockSpec((DEPTH, TN_ADA), lambda l, j: (0, j)),
        ],
        out_specs=pl.BlockSpec((None, COND_ROWS, TN_ADA), lambda l, j: (l, 0, j)),
        out_shape=jax.ShapeDtypeStruct((DEPTH, COND_ROWS, n), F32),
        compiler_params=_params(("arbitrary", "arbitrary")),
        name="ada_mod",
    )(c_ctx.reshape(1, D_MODEL), c, w_ada, b_ada)


def _in_weight_pieces(layer):
    pieces = []
    for c0 in range(0, IN_COLS, STAGE_W_COLS):
        g, off = divmod(c0, TN_IN)
        pieces.append((0, layer, slice(0, D_MODEL), slice(c0, c0 + STAGE_W_COLS),
                       g, slice(0, D_MODEL), slice(off, off + STAGE_W_COLS)))
    return pieces


def _in_kernel(*refs, layer, aliased):
    xa_ref, xb_ref, mod_ref, g_ref, w_hbm, cos_ref, sin_ref, cw_ref = refs[:8]
    refs = refs[10:] if aliased else refs[8:]
    q_ref, v_ref, kt_ref, conv_ref, nk_ref, nv_ref, h_s, wb_s, gb_s, gc_s, stage_s, sem = refs
    i = pl.program_id(0)

    @pl.when(i == 0)
    def _():
        _load_weights(_in_weight_pieces(layer), (w_hbm,), (wb_s,), stage_s, sem)

    chunks = [slice(c * ROW_CHUNK, (c + 1) * ROW_CHUNK) for c in range(TM // ROW_CHUNK)]
    seqs_per_chunk = ROW_CHUNK // SEQ

    def norm(x_ref, rows):
        gain = g_ref[layer:layer + 1, :] * (1.0 + _mod(mod_ref, layer, i, 1))
        h_s[rows, :] = (_rms(x_ref[rows, :]) * gain + _mod(mod_ref, layer, i, 0)).astype(BF)

    def proj(rows, group):
        acc = jnp.dot(h_s[rows, :], wb_s[group], preferred_element_type=F32)
        return acc[:, :ATT_WIDTH], acc[:, ATT_WIDTH:]

    def roped(a, rows):
        cos = jnp.concatenate([cos_ref[rows, :]] * N_HEADS, axis=1)
        sin = jnp.concatenate([sin_ref[rows, :]] * N_HEADS, axis=1)
        lane = lax.broadcasted_iota(jnp.int32, a.shape, 1)
        upper = (lane & (QK_DIM // 4)) != 0
        partner = jnp.where(upper, pltpu.roll(a, QK_DIM // 4, 1), pltpu.roll(a, ATT_WIDTH - QK_DIM // 4, 1))
        return a * cos + partner * sin

    def gated_conv(seq):
        for rows in chunks:
            gc, xin = proj(rows, 2)
            gc_s[rows, :] = gc * xin
        u = gc_s[...]
        pos = lax.broadcasted_iota(jnp.int32, (TM, 1), 0) & (seq - 1)
        prev = jnp.where(pos == 0, 0.0, pltpu.roll(u, 1, 0))
        nxt = jnp.where(pos == seq - 1, 0.0, pltpu.roll(u, TM - 1, 0))
        cw = cw_ref[layer]
        conv = prev * cw[0:1] + u * cw[1:2] + nxt * cw[2:3]
        conv_ref[...] = (gb_s[...] * conv).astype(BF)

    @pl.when(i >= CTX_TILES)
    def _():
        for rows in chunks:
            norm(xb_ref, rows)
            q, k = proj(rows, 0)
            q_ref[rows, :] = (roped(q, rows) * Q_SCALE).astype(BF)
            kt_ref[:, rows] = roped(k, rows).T.astype(BF)
        for rows in chunks:
            v, gb = proj(rows, 1)
            v_ref[rows, :] = v.astype(BF)
            gb_s[rows, :] = gb
        gated_conv(DEC_SEQ)

    @pl.when(i < CTX_TILES)
    def _():
        for c, rows in enumerate(chunks):
            norm(xa_ref, rows)
            q, k = proj(rows, 0)
            q_ref[rows, :] = (q * Q_SCALE).astype(BF)
            kt = k.T
            kt_ref[:, rows] = kt.astype(BF)
            for s in range(seqs_per_chunk):
                nk_ref[c * seqs_per_chunk + s] = kt[:, s * SEQ:(s + 1) * SEQ]
        for c, rows in enumerate(chunks):
            v, gb = proj(rows, 1)
            v_ref[rows, :] = v.astype(BF)
            gb_s[rows, :] = gb
            for s in range(seqs_per_chunk):
                for h in range(N_HEADS):
                    nv_ref[c * seqs_per_chunk + s, pl.ds(h, SEQ, stride=N_HEADS), :] = (
                        v[s * SEQ:(s + 1) * SEQ, h * V_DIM:(h + 1) * V_DIM])
        gated_conv(SEQ)


def _in_proj(layer, xs, mod, g_mix, w_in, cos_t, sin_t, conv_w, new_kv):
    pair = isinstance(xs, tuple)
    xa, xb = xs if pair else (xs, xs)
    spec_a, spec_b = _stream_specs(pair)
    ctx_i = lambda i: jnp.minimum(i, CTX_TILES - 1)
    in_specs = [
        spec_a, spec_b,
        WHOLE, WHOLE,
        pl.BlockSpec(memory_space=pl.ANY),
        WHOLE, WHOLE, WHOLE,
    ]
    args = [xa, xb, mod, g_mix, w_in, cos_t, sin_t, conv_w]
    aliases = {}
    if new_kv is not None:
        in_specs += [pl.BlockSpec(memory_space=pl.ANY), pl.BlockSpec(memory_space=pl.ANY)]
        args += list(new_kv)
        aliases = {8: 4, 9: 5}
    row_tile = pl.BlockSpec((TM, ATT_WIDTH), lambda i: (i, 0))
    return pl.pallas_call(
        functools.partial(_in_kernel, layer=layer, aliased=new_kv is not None),
        grid=(N_TILES,),
        in_specs=in_specs,
        out_specs=[
            row_tile,
            row_tile,
            pl.BlockSpec((None, ATT_WIDTH, TM), lambda i: (i, 0, 0)),
            row_tile,
            pl.BlockSpec((SEQ_PER_TILE, None, ATT_WIDTH, SEQ), lambda i: (ctx_i(i), layer, 0, 0)),
            pl.BlockSpec((SEQ_PER_TILE, None, SEQ * N_HEADS, V_DIM), lambda i: (ctx_i(i), layer, 0, 0)),
        ],
        out_shape=[
            jax.ShapeDtypeStruct((N_TOK, ATT_WIDTH), BF),
            jax.ShapeDtypeStruct((N_TOK, ATT_WIDTH), BF),
            jax.ShapeDtypeStruct((N_TILES, ATT_WIDTH, TM), BF),
            jax.ShapeDtypeStruct((N_TOK, CONV_WIDTH), BF),
            jax.ShapeDtypeStruct((BATCH, DEPTH, ATT_WIDTH, SEQ), F32),
            jax.ShapeDtypeStruct((BATCH, DEPTH, SEQ * N_HEADS, V_DIM), F32),
        ],
        scratch_shapes=[
            pltpu.VMEM((TM, D_MODEL), BF),
            pltpu.VMEM((N_IN_TILES, D_MODEL, TN_IN), BF),
            pltpu.VMEM((TM, CONV_WIDTH), F32),
            pltpu.VMEM((TM, CONV_WIDTH), F32),
            pltpu.VMEM((2, STAGE_W_ROWS, STAGE_W_COLS), F32),
            pltpu.SemaphoreType.DMA((2,)),
        ],
        input_output_aliases=aliases,
        compiler_params=_params(("arbitrary",)),
        name=f"in_proj_l{layer}",
    )(*args)


def _lambda(lq_ref, layer, lam_init):
    lq = lq_ref[layer]
    a = jnp.exp(jnp.sum(lq[0:1] * lq[1:2], axis=-1, keepdims=True))
    b = jnp.exp(jnp.sum(lq[2:3] * lq[3:4], axis=-1, keepdims=True))
    return a - b + lam_init


def _head_norm(o, sg, lam_init):
    return _rms(o) * sg * (1.0 - lam_init)


def _attn_ctx_kernel(q_ref, kt_ref, v_ref, lq_ref, sg_ref, o_ref, sc_s, *, layer, lam_init):
    lam = _lambda(lq_ref, layer, lam_init)
    sg = sg_ref[layer:layer + 1, :]

    def sequence(b, carry):
        pos = pl.ds(pl.multiple_of(b * SEQ, SEQ), SEQ)

        def scores(h):
            for s in range(2):
                d = slice(h * V_DIM + s * QK_DIM, h * V_DIM + (s + 1) * QK_DIM)
                sc_s[h % 2, s] = jnp.dot(q_ref[pos, d], kt_ref[d, pos], preferred_element_type=F32)

        def finish(h):
            cols = slice(h * V_DIM, (h + 1) * V_DIM)
            v = v_ref[pos, cols]
            outs = []
            for s in range(2):
                sc = sc_s[h % 2, s]
                e = jnp.exp2(sc - jnp.max(sc, axis=-1, keepdims=True))
                r = 1.0 / jnp.sum(e, axis=-1, keepdims=True)
                outs.append(jnp.dot(e.astype(BF), v, preferred_element_type=F32) * r)
            o = outs[0] - lam * outs[1]
            o_ref[pos, cols] = _head_norm(o, sg, lam_init).astype(BF)

        scores(0)
        for h in range(N_HEADS):
            if h + 1 < N_HEADS:
                scores(h + 1)
            finish(h)
        return carry

    lax.fori_loop(0, CTX_SEQ_PER_STEP, sequence, 0)


def _attn_ctx(layer, q, kt, v, lambda_qk, subln_g, lam_init):
    rows = CTX_SEQ_PER_STEP * SEQ
    per_tile = TM // rows
    return pl.pallas_call(
        functools.partial(_attn_ctx_kernel, layer=layer, lam_init=lam_init),
        grid=(N_CTX // rows,),
        in_specs=[
            pl.BlockSpec((rows, ATT_WIDTH), lambda b: (b, 0)),
            pl.BlockSpec((None, ATT_WIDTH, rows), lambda b: (b // per_tile, 0, b % per_tile)),
            pl.BlockSpec((rows, ATT_WIDTH), lambda b: (b, 0)),
            WHOLE, WHOLE,
        ],
        out_specs=pl.BlockSpec((rows, ATT_WIDTH), lambda b: (b, 0)),
        out_shape=jax.ShapeDtypeStruct((N_TOK, ATT_WIDTH), BF),
        scratch_shapes=[pltpu.VMEM((2, 2, SEQ, SEQ), F32)],
        compiler_params=_params(("arbitrary",)),
        name=f"attn_ctx_l{layer}",
    )(q, kt, v, lambda_qk, subln_g)


TQ = 256
LAT_Q_PER_STEP = 2


def _attn_lat_kernel(q_ref, kt_ref, v_ref, ckt_ref, cv_ref, lq_ref, sg_ref, att_in_ref, o_ref, sc_s, *,
                     layer, lam_init):
    del att_in_ref
    lam = _lambda(lq_ref, layer, lam_init)
    sg = sg_ref[layer:layer + 1, :]

    units = [(b, h) for b in range(LAT_Q_PER_STEP) for h in range(N_HEADS)]

    def scores(u, s):
        b, h = units[u]
        d = slice(h * V_DIM + s * QK_DIM, h * V_DIM + (s + 1) * QK_DIM)
        q = q_ref[b * TQ:(b + 1) * TQ, d]
        sc_s[u % 2, s, :, :PAST_LEN] = jnp.dot(q, ckt_ref[d, :].astype(BF), preferred_element_type=F32)
        sc_s[u % 2, s, :, PAST_LEN:] = jnp.dot(q, kt_ref[d, :], preferred_element_type=F32)

    def softmax(u, s):
        sc = sc_s[u % 2, s]
        e = jnp.exp2(sc - jnp.max(sc, axis=-1, keepdims=True))
        return e, 1.0 / jnp.sum(e, axis=-1, keepdims=True)

    def finish(u, p1, p2):
        b, h = units[u]
        cols = slice(h * V_DIM, (h + 1) * V_DIM)
        e = jnp.concatenate([p1[0].astype(BF), p2[0].astype(BF)], axis=0)
        vc = cv_ref[pl.ds(h, PAST_LEN, stride=N_HEADS), :].astype(BF)
        pv = jnp.dot(e[:, :PAST_LEN], vc, preferred_element_type=F32)
        pv = pv + jnp.dot(e[:, PAST_LEN:], v_ref[:, cols], preferred_element_type=F32)
        o = pv[:TQ] * p1[1] - pv[TQ:] * (lam * p2[1])
        o_ref[b * TQ:(b + 1) * TQ, cols] = _head_norm(o, sg, lam_init).astype(BF)

    scores(0, 0)
    scores(0, 1)
    for u in range(len(units)):
        more = u + 1 < len(units)
        if more:
            scores(u + 1, 0)
        p1 = softmax(u, 0)
        if more:
            scores(u + 1, 1)
        finish(u, p1, softmax(u, 1))


def _attn_lat(layer, q, kt, v, cache_kt, cache_v, lambda_qk, subln_g, att, lam_init):
    rows = LAT_Q_PER_STEP * TQ
    nqb = DEC_SEQ // rows
    q0 = N_CTX // rows
    return pl.pallas_call(
        functools.partial(_attn_lat_kernel, layer=layer, lam_init=lam_init),
        grid=(DEC_BATCH, nqb),
        in_specs=[
            pl.BlockSpec((rows, ATT_WIDTH), lambda b, t: (q0 + b * nqb + t, 0)),
            pl.BlockSpec((None, ATT_WIDTH, DEC_SEQ), lambda b, t: (CTX_TILES + b, 0, 0)),
            pl.BlockSpec((DEC_SEQ, ATT_WIDTH), lambda b, t: (CTX_TILES + b, 0)),
            pl.BlockSpec((None, None, ATT_WIDTH, PAST_LEN), lambda b, t: (b, layer, 0, 0)),
            pl.BlockSpec((None, None, PAST_LEN * N_HEADS, V_DIM), lambda b, t: (b, layer, 0, 0)),
            WHOLE, WHOLE,
            pl.BlockSpec(memory_space=pl.ANY),
        ],
        out_specs=pl.BlockSpec((rows, ATT_WIDTH), lambda b, t: (q0 + b * nqb + t, 0)),
        out_shape=jax.ShapeDtypeStruct((N_TOK, ATT_WIDTH), BF),
        scratch_shapes=[pltpu.VMEM((2, 2, TQ, PAST_LEN + DEC_SEQ), F32)],
        input_output_aliases={7: 0},
        compiler_params=_params(("arbitrary", "arbitrary")),
        name=f"attn_lat_l{layer}",
    )(q, kt, v, cache_kt, cache_v, lambda_qk, subln_g, att)


def _out_kernel(att_ref, conv_ref, w_ref, xa_ref, xb_ref, mod_ref, gf_ref, *rest, layer, route):
    if route:
        wr_ref, xo_ref, h2_ref, rt_ref, n_ref, wb_s = rest
    else:
        xo_ref, h2_ref, wb_s = rest
    i = pl.program_id(0)

    @pl.when(i == 0)
    def _():
        wb_s[...] = w_ref[...].astype(BF)

    lat = i >= CTX_TILES
    gain = gf_ref[layer:layer + 1, :] * (1.0 + _mod(mod_ref, layer, i, 4))
    for c in range(TM // ROW_CHUNK):
        rows = slice(c * ROW_CHUNK, (c + 1) * ROW_CHUNK)
        mo = jnp.dot(att_ref[rows, :], wb_s[:ATT_WIDTH, :], preferred_element_type=F32)
        mo = mo + jnp.dot(conv_ref[rows, :], wb_s[ATT_WIDTH:, :], preferred_element_type=F32)
        xn = jnp.where(lat, xb_ref[rows, :], xa_ref[rows, :]) + _mod(mod_ref, layer, i, 2) * mo
        xo_ref[rows, :] = xn
        h2_ref[rows, :] = (_rms(xn) * gain + _mod(mod_ref, layer, i, 3)).astype(BF)
    if route:
        _route(h2_ref[...], wr_ref, rt_ref, n_ref)


def _out_proj(layer, att, conv, w_out, xs, mod, g_ffn, w_router_pad=None):
    pair = isinstance(xs, tuple)
    xa, xb = xs if pair else (xs, xs)
    spec_a, spec_b = _stream_specs(pair)
    row_spec = pl.BlockSpec((TM, D_MODEL), lambda i: (i, 0))
    lane_spec = pl.BlockSpec((TM, LANES), lambda i: (i, 0))
    in_specs = [
        pl.BlockSpec((TM, ATT_WIDTH), lambda i: (i, 0)),
        pl.BlockSpec((TM, CONV_WIDTH), lambda i: (i, 0)),
        pl.BlockSpec((None, D_MODEL, D_MODEL), lambda i: (layer, 0, 0)),
        spec_a, spec_b,
        WHOLE, WHOLE,
    ]
    args = [att, conv, w_out, xa, xb, mod, g_ffn]
    out_specs = [row_spec, row_spec]
    out_shape = [jax.ShapeDtypeStruct((N_TOK, D_MODEL), F32), jax.ShapeDtypeStruct((N_TOK, D_MODEL), BF)]
    route = w_router_pad is not None
    if route:
        in_specs.append(WHOLE)
        args.append(w_router_pad)
        out_specs += [lane_spec, pl.BlockSpec((None, SUBLANES, LANES), lambda i: (i, 0, 0))]
        out_shape += [jax.ShapeDtypeStruct((N_TOK, LANES), F32),
                      jax.ShapeDtypeStruct((N_TILES, SUBLANES, LANES), jnp.int32)]
    return pl.pallas_call(
        functools.partial(_out_kernel, layer=layer, route=route),
        grid=(N_TILES,),
        in_specs=in_specs,
        out_specs=out_specs,
        out_shape=out_shape,
        scratch_shapes=[pltpu.VMEM((D_MODEL, D_MODEL), BF)],
        compiler_params=_params(("arbitrary",)),
        name=f"out_proj_l{layer}",
    )(*args)


def _weight_pieces(layer):
    pieces = []
    for c0 in range(0, D_MODEL, STAGE_W_COLS):
        cols = slice(c0, c0 + STAGE_W_COLS)
        pieces.append((0, layer, slice(0, D_MODEL), cols, None, slice(0, D_MODEL), cols))
    for half in range(FF_HALVES):
        for part in range(2):
            src0 = part * D_FF + half * FF_HALF
            for off in range(0, FF_HALF, STAGE_W_COLS):
                n = min(STAGE_W_COLS, FF_HALF - off)
                pieces.append((1, 0, slice(0, D_MODEL), slice(src0 + off, src0 + off + n),
                               half, slice(0, D_MODEL), slice(part * FF_HALF + off, part * FF_HALF + off + n)))
    for r0 in range(0, D_FF, STAGE_W_ROWS):
        rows = slice(r0, min(r0 + STAGE_W_ROWS, D_FF))
        for c0 in range(0, D_MODEL, STAGE_W_COLS):
            cols = slice(c0, c0 + STAGE_W_COLS)
            pieces.append((2, 0, rows, cols, None, rows, cols))
    return pieces


def _load_weights(pieces, hbm, resident, stage_s, sem):
    slots = stage_s.shape[0]

    def copy(k):
        src, idx, rows, cols, _, _, _ = pieces[k]
        nr, nc = rows.stop - rows.start, cols.stop - cols.start
        slot = k % slots
        return pltpu.make_async_copy(hbm[src].at[idx, rows, cols], stage_s.at[slot, :nr, :nc], sem.at[slot])

    for k in range(min(slots - 1, len(pieces))):
        copy(k).start()
    for k, (src, _, rows, cols, didx, drows, dcols) in enumerate(pieces):
        if k + slots - 1 < len(pieces):
            copy(k + slots - 1).start()
        copy(k).wait()
        nr, nc = rows.stop - rows.start, cols.stop - cols.start
        piece = stage_s[k % slots, :nr, :nc].astype(BF)
        if didx is None:
            resident[src][drows, dcols] = piece
        else:
            resident[src][didx, drows, dcols] = piece


def _dense_layer_kernel(att_ref, conv_ref, xa_ref, xb_ref, mod_ref, gf_ref, wo_hbm, wgu_hbm, wd_hbm,
                        o_ref, wo_b, wgu_b, wd_b, stage_s, sem, *, layer):
    i = pl.program_id(0)
    tile = i // (TM // FF_TM)

    @pl.when(i == 0)
    def _():
        _load_weights(_weight_pieces(layer), (wo_hbm, wgu_hbm, wd_hbm), (wo_b, wgu_b, wd_b), stage_s, sem)

    mo = jnp.dot(att_ref[...], wo_b[:ATT_WIDTH, :], preferred_element_type=F32)
    mo = mo + jnp.dot(conv_ref[...], wo_b[ATT_WIDTH:, :], preferred_element_type=F32)
    x = jnp.where(tile >= CTX_TILES, xb_ref[...], xa_ref[...])
    xn = x + _mod(mod_ref, layer, tile, 2) * mo
    h = (_rms(xn) * gf_ref[layer:layer + 1, :]) * (1.0 + _mod(mod_ref, layer, tile, 4)) + _mod(mod_ref, layer, tile, 3)
    h = h.astype(BF)
    y = None
    for half in range(FF_HALVES):
        gu = jnp.dot(h, wgu_b[half], preferred_element_type=F32)
        act = (_silu(gu[:, :FF_HALF]) * gu[:, FF_HALF:]).astype(BF)
        part = jnp.dot(act, wd_b[half * FF_HALF:(half + 1) * FF_HALF, :], preferred_element_type=F32)
        y = part if y is None else y + part
    o_ref[...] = xn + _mod(mod_ref, layer, tile, 5) * y


def _dense_layer(layer, att, conv, xs, mod, g_ffn, w_out, w_gu, w_down):
    xa, xb = xs
    n_ctx = N_CTX // FF_TM
    row = lambda width: pl.BlockSpec((FF_TM, width), lambda i: (i, 0))
    hbm = pl.BlockSpec(memory_space=pl.ANY)
    return pl.pallas_call(
        functools.partial(_dense_layer_kernel, layer=layer),
        grid=(N_TOK // FF_TM,),
        in_specs=[
            row(ATT_WIDTH), row(CONV_WIDTH),
            pl.BlockSpec((FF_TM, D_MODEL), lambda i: (jnp.minimum(i, n_ctx - 1), 0)),
            pl.BlockSpec((FF_TM, D_MODEL), lambda i: (jnp.maximum(i - n_ctx, 0), 0)),
            WHOLE, WHOLE, hbm, hbm, hbm,
        ],
        out_specs=row(D_MODEL),
        out_shape=jax.ShapeDtypeStruct((N_TOK, D_MODEL), F32),
        scratch_shapes=[
            pltpu.VMEM((D_MODEL, D_MODEL), BF),
            pltpu.VMEM((FF_HALVES, D_MODEL, 2 * FF_HALF), BF),
            pltpu.VMEM((D_FF, D_MODEL), BF),
            pltpu.VMEM((FF_STAGE_SLOTS, STAGE_W_ROWS, STAGE_W_COLS), F32),
            pltpu.SemaphoreType.DMA((FF_STAGE_SLOTS,)),
        ],
        compiler_params=_params(("arbitrary",)),
        name="dense_layer",
    )(att, conv, xa, xb, mod, g_ffn, w_out, w_gu, w_down)


def _route(h, wr_ref, rt_ref, n_ref):
    logits = jnp.dot(h, wr_ref[...].astype(BF), preferred_element_type=F32)
    lane = lax.broadcasted_iota(jnp.int32, logits.shape, 1)
    lg = jnp.where(lane < N_EXPERTS, logits, -jnp.inf)
    m1 = jnp.max(lg, axis=-1, keepdims=True)
    i1 = jnp.min(jnp.where(lg == m1, lane, LANES), axis=-1, keepdims=True)
    lg2 = jnp.where(lane == i1, -jnp.inf, lg)
    m2 = jnp.max(lg2, axis=-1, keepdims=True)
    i2 = jnp.min(jnp.where(lg2 == m2, lane, LANES), axis=-1, keepdims=True)
    e2 = jnp.exp(m2 - m1)
    w1 = 1.0 / (1.0 + e2)
    w2 = e2 / (1.0 + e2)

    sel1 = lane == i1
    sel2 = lane == i2
    onehot = jnp.logical_or(sel1, sel2)
    rows = lax.broadcasted_iota(jnp.int32, (ST, ST), 0)
    colsi = lax.broadcasted_iota(jnp.int32, (ST, ST), 1)
    earlier = (colsi < rows).astype(BF)
    onehot_b = onehot.astype(BF)
    before = jnp.concatenate(
        [jnp.dot(earlier, onehot_b[s * ST:(s + 1) * ST], preferred_element_type=F32) for s in range(SUB_PER_TILE)],
        axis=0)
    onehot_f = onehot.astype(F32)
    counts = [jnp.sum(onehot_f[s * ST:(s + 1) * ST], axis=0, keepdims=True) for s in range(SUB_PER_TILE)]
    counts = jnp.concatenate(counts + [jnp.zeros((SUBLANES - SUB_PER_TILE, LANES), F32)], axis=0).astype(jnp.int32)
    seg_len = ((counts + (SEG_ALIGN - 1)) // SEG_ALIGN) * SEG_ALIGN
    n_ref[...] = seg_len
    la = lax.broadcasted_iota(jnp.int32, (LANES, LANES), 0)
    lb = lax.broadcasted_iota(jnp.int32, (LANES, LANES), 1)
    seg_start = jnp.dot(seg_len.astype(F32).astype(BF), (la < lb).astype(BF), preferred_element_type=F32)
    start = jnp.concatenate(
        [jnp.broadcast_to(seg_start[s:s + 1], (ST, LANES)) for s in range(SUB_PER_TILE)], axis=0)
    where = before + start
    lp1 = jnp.sum(jnp.where(sel1, where, 0.0), axis=-1, keepdims=True)
    lp2 = jnp.sum(jnp.where(sel2, where, 0.0), axis=-1, keepdims=True)
    rt = jnp.where(lane == GATE_LANE, w1, jnp.where(lane == GATE_LANE + 1, w2, 0.0))
    rt_ref[...] = jnp.where(lane == ROW_LANE, lp1, jnp.where(lane == ROW_LANE + 1, lp2, rt))


def _chunk_copies(s, cnt_ref, cdst_ref, stage, rows_hbm, sem, *, to_hbm, wait):
    def copy(v, h):
        return pltpu.make_async_copy(v, h, sem) if to_hbm else pltpu.make_async_copy(h, v, sem)

    if wait:
        for z in WAIT_PIECES:
            @pl.when((cnt_ref[s] & z) != 0)
            def _():
                copy(stage.at[pl.ds(0, z * SEG_ALIGN)], rows_hbm.at[pl.ds(0, z * SEG_ALIGN)]).wait()
        return

    def start(c, priority):
        v = stage.at[pl.ds(pl.multiple_of(c * SEG_ALIGN, SEG_ALIGN), SEG_ALIGN)]
        h = rows_hbm.at[pl.ds(pl.multiple_of(cdst_ref[s * STAGE_CHUNKS + c], SEG_ALIGN), SEG_ALIGN)]
        copy(v, h).start(priority=priority)

    def pair(p, carry):
        start(2 * p, 0)
        start(2 * p + 1, 1)
        return carry

    n = cnt_ref[s]
    lax.fori_loop(0, n // 2, pair, 0)

    @pl.when((n & 1) != 0)
    def _():
        start(n - 1, 0)


def _dispatch_kernel(cnt_ref, cdst_ref, h_ref, rt_ref, xs_ref, stage_s, sem):
    copies = functools.partial(_chunk_copies, cnt_ref=cnt_ref, cdst_ref=cdst_ref, rows_hbm=xs_ref, to_hbm=True)
    for k in range(SUB_PER_TILE):
        s = pl.program_id(0) * SUB_PER_TILE + k
        slot = k % 2
        rows = slice(k * ST, (k + 1) * ST)

        @pl.when(s >= 2)
        def _():
            copies(s - 2, stage=stage_s.at[slot], sem=sem.at[slot], wait=True)

        at = rt_ref[rows, :].T[ROW_LANE:ROW_LANE + 2, :].astype(jnp.int32)
        r = lax.broadcasted_iota(jnp.int32, (STAGE_ROWS, ST), 0)
        pick = jnp.logical_or(r == at[0:1, :], r == at[1:2, :]).astype(BF)
        stage_s[slot] = jnp.dot(pick, h_ref[rows, :], preferred_element_type=F32).astype(BF)
        copies(s, stage=stage_s.at[slot], sem=sem.at[slot], wait=False)

    @pl.when(pl.program_id(0) == N_TILES - 1)
    def _():
        copies(N_SUB - 2, stage=stage_s.at[0], sem=sem.at[0], wait=True)
        copies(N_SUB - 1, stage=stage_s.at[1], sem=sem.at[1], wait=True)


def _dispatch(cnt, cdst, h2, rt):
    assert SUB_PER_TILE % 2 == 0
    grid_spec = pltpu.PrefetchScalarGridSpec(
        num_scalar_prefetch=2,
        grid=(N_TILES,),
        in_specs=[
            pl.BlockSpec((TM, D_MODEL), lambda i, *_: (i, 0)),
            pl.BlockSpec((TM, LANES), lambda i, *_: (i, 0)),
        ],
        out_specs=pl.BlockSpec(memory_space=pl.ANY),
        scratch_shapes=[pltpu.VMEM((2, STAGE_ROWS, D_MODEL), BF), pltpu.SemaphoreType.DMA((2,))],
    )
    return pl.pallas_call(
        _dispatch_kernel,
        grid_spec=grid_spec,
        out_shape=jax.ShapeDtypeStruct((R_PAD, D_MODEL), BF),
        compiler_params=_params(("arbitrary",)),
        name="moe_dispatch",
    )(cnt, cdst, h2, rt)


def _expert_weights(r, te_ref, nt_ref, nxt_ref, w_hbm, wf_s, wb_s, sem):
    def fetch(e):
        return pltpu.make_async_copy(w_hbm.at[0, e], wf_s, sem)

    @pl.when(r == 0)
    def _():
        fetch(te_ref[0]).start()

    first = jnp.logical_or(r == 0, te_ref[r] != te_ref[jnp.maximum(r - 1, 0)])

    @pl.when(jnp.logical_and(r < nt_ref[0], first))
    def _():
        fetch(te_ref[r]).wait()
        wb_s[...] = wf_s[...].astype(BF)

        @pl.when(nxt_ref[r] >= 0)
        def _():
            fetch(nxt_ref[r]).start()


def _moe_ffn_kernel(te_ref, nt_ref, nxt_ref, used_ref, x_ref, wgu_hbm, wd_hbm, o_ref,
                    wgu_f, wgu_b, wd_f, wd_b, sem):
    def ffn(rows):
        gu = jnp.dot(x_ref[rows, :], wgu_b[...], preferred_element_type=F32)
        act = (_silu(gu[:, :D_FF_EXPERT]) * gu[:, D_FF_EXPERT:]).astype(BF)
        o_ref[rows, :] = jnp.dot(act, wd_b[...], preferred_element_type=F32).astype(BF)

    for k in range(TILES_PER_STEP):
        r = pl.program_id(0) * TILES_PER_STEP + k
        _expert_weights(r, te_ref, nt_ref, nxt_ref, wgu_hbm, wgu_f, wgu_b, sem.at[0])
        _expert_weights(r, te_ref, nt_ref, nxt_ref, wd_hbm, wd_f, wd_b, sem.at[1])

        @pl.when(jnp.logical_and(r < nt_ref[0], used_ref[r] > TG // 2))
        def _():
            ffn(slice(k * TG, (k + 1) * TG))

        @pl.when(jnp.logical_and(r < nt_ref[0], used_ref[r] <= TG // 2))
        def _():
            ffn(slice(k * TG, k * TG + TG // 2))


def _moe_ffn(te, nt, nxt, used, rows, w_gu, w_down):
    tile_map = lambda i, te, nt, nxt, used: (jnp.minimum(i, (nt[0] - 1) // TILES_PER_STEP), 0)
    grid_spec = pltpu.PrefetchScalarGridSpec(
        num_scalar_prefetch=4,
        grid=(NT_G // TILES_PER_STEP,),
        in_specs=[pl.BlockSpec((TILES_PER_STEP * TG, D_MODEL), tile_map),
                  pl.BlockSpec(memory_space=pl.ANY), pl.BlockSpec(memory_space=pl.ANY)],
        out_specs=pl.BlockSpec((TILES_PER_STEP * TG, D_MODEL), tile_map),
        scratch_shapes=[
            pltpu.VMEM((D_MODEL, 2 * D_FF_EXPERT), F32), pltpu.VMEM((D_MODEL, 2 * D_FF_EXPERT), BF),
            pltpu.VMEM((D_FF_EXPERT, D_MODEL), F32), pltpu.VMEM((D_FF_EXPERT, D_MODEL), BF),
            pltpu.SemaphoreType.DMA((2,)),
        ],
    )
    return pl.pallas_call(
        _moe_ffn_kernel,
        grid_spec=grid_spec,
        out_shape=jax.ShapeDtypeStruct((R_PAD, D_MODEL), BF),
        compiler_params=_params(("arbitrary",)),
        name="moe_ffn",
    )(te, nt, nxt, used, rows, w_gu, w_down)


def _combine_kernel(cnt_ref, cdst_ref, ys_ref, rt_ref, x_ref, mod_ref, fg_ref,
                    oa_ref, ob_ref, stage_s, sem, *, layer):
    i = pl.program_id(0)
    copies = functools.partial(_chunk_copies, cnt_ref=cnt_ref, cdst_ref=cdst_ref, rows_hbm=ys_ref, to_hbm=False)

    @pl.when(i == 0)
    def _():
        stage_s[...] = jnp.zeros_like(stage_s)
        for s in range(COMBINE_AHEAD):
            copies(s, stage=stage_s.at[s], sem=sem.at[s], wait=False)

    for k in range(SUB_PER_TILE):
        s = i * SUB_PER_TILE + k
        slot = k % COMBINE_SLOTS
        ahead = (k + COMBINE_AHEAD) % COMBINE_SLOTS
        rows = slice(k * ST, (k + 1) * ST)

        @pl.when(s + COMBINE_AHEAD < N_SUB)
        def _():
            copies(s + COMBINE_AHEAD, stage=stage_s.at[ahead], sem=sem.at[ahead], wait=False)

        copies(s, stage=stage_s.at[slot], sem=sem.at[slot], wait=True)

        rt = rt_ref[rows, :]
        at = rt[:, ROW_LANE:ROW_LANE + 2].astype(jnp.int32)
        r = lax.broadcasted_iota(jnp.int32, (ST, STAGE_ROWS), 1)
        staged = stage_s[slot]
        a = jnp.dot((r == at[:, 0:1]).astype(BF), staged, preferred_element_type=F32)
        b = jnp.dot((r == at[:, 1:2]).astype(BF), staged, preferred_element_type=F32)
        y = rt[:, GATE_LANE:GATE_LANE + 1] * a + rt[:, GATE_LANE + 1:GATE_LANE + 2] * b
        xn = x_ref[rows, :] + _mod(mod_ref, layer, i, 5) * y
        out = _rms(xn) * fg_ref[...]

        @pl.when(i < CTX_TILES)
        def _():
            oa_ref[rows, :] = out

        @pl.when(i >= CTX_TILES)
        def _():
            ob_ref[rows, :] = out


def _combine(layer, cnt, cdst, ys, rt, x, mod, final_g):
    assert SUB_PER_TILE % COMBINE_SLOTS == 0
    grid_spec = pltpu.PrefetchScalarGridSpec(
        num_scalar_prefetch=2,
        grid=(N_TILES,),
        in_specs=[
            pl.BlockSpec(memory_space=pl.ANY),
            pl.BlockSpec((TM, LANES), lambda i, *_: (i, 0)),
            pl.BlockSpec((TM, D_MODEL), lambda i, *_: (i, 0)),
            WHOLE, WHOLE,
        ],
        out_specs=[
            pl.BlockSpec((TM, D_MODEL), lambda i, *_: (jnp.minimum(i, CTX_TILES - 1), 0)),
            pl.BlockSpec((TM, D_MODEL), lambda i, *_: (jnp.maximum(i - CTX_TILES, 0), 0)),
        ],
        scratch_shapes=[pltpu.VMEM((COMBINE_SLOTS, STAGE_ROWS, D_MODEL), BF),
                        pltpu.SemaphoreType.DMA((COMBINE_SLOTS,))],
    )
    return pl.pallas_call(
        functools.partial(_combine_kernel, layer=layer),
        grid_spec=grid_spec,
        out_shape=[
            jax.ShapeDtypeStruct((N_CTX, D_MODEL), F32),
            jax.ShapeDtypeStruct((N_LAT, D_MODEL), F32),
        ],
        compiler_params=_params(("arbitrary",)),
        name="moe_combine",
    )(cnt, cdst, ys, rt, x, mod, final_g)


def _group_layout(n_tiles):
    n = n_tiles[:, :SUB_PER_TILE, :N_EXPERTS].reshape(N_SUB, N_EXPERTS)
    tiles = (jnp.sum(n, axis=0) + TG - 1) // TG
    tile_end = jnp.cumsum(tiles)
    region = (tile_end - tiles) * TG
    dst = region[None, :] + jnp.cumsum(n, axis=0) - n
    seg_end = jnp.cumsum(n, axis=1)
    seg = seg_end - n
    row = jnp.arange(STAGE_CHUNKS, dtype=jnp.int32) * SEG_ALIGN
    owner = jnp.sum((row[None, :, None] >= seg_end[:, None, :]).astype(jnp.int32), axis=-1)
    own = jnp.minimum(owner, N_EXPERTS - 1)[..., None] == jnp.arange(N_EXPERTS)
    cdst = jnp.sum(jnp.where(own, (dst - seg)[:, None, :], 0), axis=-1) + row[None, :]
    cnt = seg_end[:, -1] // SEG_ALIGN
    nt = tile_end[-1]
    tile_id = jnp.minimum(jnp.arange(NT_G, dtype=jnp.int32), nt - 1)
    te = jnp.sum((tile_id[:, None] >= tile_end[None, :]).astype(jnp.int32), axis=-1)
    after = jnp.sum(jnp.where(te[:, None] == jnp.arange(N_EXPERTS), tile_end[None, :], 0), axis=-1)
    nxt = jnp.where(after < nt, jnp.sum((after[:, None] >= tile_end[None, :]).astype(jnp.int32), axis=-1), -1)
    mine = te[:, None] == jnp.arange(N_EXPERTS)
    region_end = jnp.sum(jnp.where(mine, (region + jnp.sum(n, axis=0))[None, :], 0), axis=-1)
    used = jnp.clip(region_end - tile_id * TG, 0, TG)
    i32 = lambda a: a.astype(jnp.int32)
    return (i32(cnt), i32(cdst.reshape(N_SUB * STAGE_CHUNKS)), i32(te), i32(nt.reshape(1)), i32(nxt), i32(used))


def _rope_tables():
    p = np.arange(DEC_SEQ)
    row = (p // GRID_W).astype(np.float32)
    col = (p % GRID_W).astype(np.float32)
    half = QK_DIM // 4
    freqs = (ROPE_BASE ** (-np.arange(half, dtype=np.float32) / half)).astype(np.float32)
    lane = np.arange(V_DIM)
    f = freqs[lane & (half - 1)]
    use_col = (lane & (2 * half)) != 0
    ang = (np.where(use_col[None, :], col[:, None], row[:, None]) * f[None, :]).astype(np.float32)
    upper = (lane & half) != 0
    sin = np.sin(ang)
    return jnp.asarray(np.cos(ang), F32), jnp.asarray(np.where(upper[None, :], sin, -sin), F32)


def kernel(x_prompt, x_sample, cache_k, cache_v, c, c_ctx, w_ada, b_ada, norm_mix_g, norm_ffn_g,
           w_in, lambda_qk, subln_g, conv_w, w_out, w_gu_dense, w_down_dense, w_router,
           w_gu_moe, w_down_moe, final_g):
    assert DEPTH == 2
    xs = (x_prompt.reshape(N_CTX, D_MODEL), x_sample.reshape(N_LAT, D_MODEL))
    mod = _ada(c_ctx, c, w_ada, b_ada)
    cos_t, sin_t = _rope_tables()
    cache_kt = jnp.transpose(cache_k, (0, 1, 3, 4, 5, 2)).reshape(DEC_BATCH, DEPTH, ATT_WIDTH, PAST_LEN)
    cache_v4 = cache_v.reshape(DEC_BATCH, DEPTH, PAST_LEN * N_HEADS, V_DIM)

    new_kv = None
    for layer in range(DEPTH):
        lam_init = 0.8 - 0.6 * math.exp(-0.3 * layer)
        q, v, kt, conv, nk, nv = _in_proj(layer, xs, mod, norm_mix_g, w_in, cos_t, sin_t, conv_w, new_kv)
        new_kv = (nk, nv)
        att = _attn_ctx(layer, q, kt, v, lambda_qk, subln_g, lam_init)
        att = _attn_lat(layer, q, kt, v, cache_kt, cache_v4, lambda_qk, subln_g, att, lam_init)
        if layer == 0:
            xs = _dense_layer(layer, att, conv, xs, mod, norm_ffn_g, w_out, w_gu_dense, w_down_dense)
        else:
            wr = jnp.pad(w_router[0], ((0, 0), (0, LANES - N_EXPERTS)))
            x1, h2, rt, n_tiles = _out_proj(layer, att, conv, w_out, xs, mod, norm_ffn_g, wr)
            cnt, cdst, te, nt, nxt, used = _group_layout(n_tiles)
            xsort = _dispatch(cnt, cdst, h2, rt)
            ys = _moe_ffn(te, nt, nxt, used, xsort, w_gu_moe, w_down_moe)
            y_ctx, y_lat = _combine(layer, cnt, cdst, ys, rt, x1, mod, final_g.reshape(1, D_MODEL))
    nk, nv = new_kv
    new_k = jnp.transpose(nk.reshape(BATCH, DEPTH, N_HEADS, 2, QK_DIM, SEQ), (0, 1, 5, 2, 3, 4))
    new_v = nv.reshape(BATCH, DEPTH, SEQ, N_HEADS, V_DIM)
    return (y_ctx.reshape(BATCH, SEQ, D_MODEL), y_lat.reshape(DEC_BATCH, DEC_SEQ, D_MODEL), new_k, new_v)
```

```python
import functools
import math

import numpy as np
import jax
import jax.numpy as jnp
from jax import lax
from jax.experimental import pallas as pl
from jax.experimental.pallas import tpu as pltpu

D_MODEL = 1024
BATCH = 16
SEQ = 256
DEPTH = 2
DEC_BATCH = 4
DEC_SEQ = 1024
PAST_LEN = 512
GRID_W = 64
ATT_WIDTH = 512
CONV_WIDTH = 512
N_HEADS = 4
V_DIM = 128
QK_DIM = 64
ROPE_BASE = 10000.0
D_FF = 2816
N_EXPERTS = 8
D_FF_EXPERT = 1408
N_MOD = 6
NORM_EPS = 1e-6
Q_SCALE = QK_DIM ** -0.5 * math.log2(math.e)
IN_COLS = 3 * ATT_WIDTH + 3 * CONV_WIDTH

N_CTX = BATCH * SEQ
N_LAT = DEC_BATCH * DEC_SEQ
N_TOK = N_CTX + N_LAT
TM = 1024
N_TILES = N_TOK // TM
CTX_TILES = N_CTX // TM
SEQ_PER_TILE = TM // SEQ
CTX_SEQ_PER_STEP = 4
COND_ROWS = 8
TN_IN = 1024
N_IN_TILES = IN_COLS // TN_IN
ROW_CHUNK = 512
FF_TM = 512
FF_HALVES = 2
FF_HALF = D_FF // FF_HALVES
STAGE_W_ROWS = 1024
STAGE_W_COLS = 512
FF_STAGE_SLOTS = 6
IN_STAGE_SLOTS = 3
TN_ADA = 1536
ADA_SPLIT = 4
TG = 512
ST = 256
SUB_PER_TILE = TM // ST
N_SUB = N_TOK // ST
SEG_ALIGN = 8
STAGE_ROWS = 640
STAGE_CHUNKS = STAGE_ROWS // SEG_ALIGN
WAIT_PIECES = (64, 32, 16, 8, 4, 2, 1)
COMBINE_SLOTS = 4
COMBINE_AHEAD = 2
TILES_PER_STEP = 2
EXPERT_FETCH_PARTS = 4
_MAX_SORTED_ROWS = 2 * N_TOK + N_SUB * N_EXPERTS * (SEG_ALIGN - 1) + N_EXPERTS * (TG - SEG_ALIGN)
NT_G = -(-_MAX_SORTED_ROWS // (TG * TILES_PER_STEP)) * TILES_PER_STEP
R_PAD = NT_G * TG
LANES = 128
SUBLANES = 8
GATE_LANE = 0
ROW_LANE = 2
VMEM_LIMIT = 60 * 1024 * 1024

BF = jnp.bfloat16
F32 = jnp.float32


def _params(sem, vmem=VMEM_LIMIT):
    return pltpu.CompilerParams(dimension_semantics=sem, vmem_limit_bytes=vmem)


def _mod_row(i):
    return jnp.where(i < CTX_TILES, 0, i - (CTX_TILES - 1))


WHOLE = pl.BlockSpec(memory_space=pltpu.VMEM)


def _mod(mod_ref, layer, i, c):
    return mod_ref[layer, pl.ds(_mod_row(i), 1), c * D_MODEL:(c + 1) * D_MODEL]


def _stream_specs(pair, width=D_MODEL):
    a = pl.BlockSpec((TM, width), lambda i, *_: (jnp.minimum(i, CTX_TILES - 1), 0))
    if pair:
        b = pl.BlockSpec((TM, width), lambda i, *_: (jnp.maximum(i - CTX_TILES, 0), 0))
    else:
        b = pl.BlockSpec((TM, width), lambda i, *_: (jnp.maximum(i, CTX_TILES), 0))
    return a, b


def _silu(x):
    return x / (1.0 + jnp.exp(-x))


def _rms(x):
    return x * lax.rsqrt(jnp.mean(x * x, axis=-1, keepdims=True) + NORM_EPS)


def _ada_kernel(cc_ref, c_ref, *refs):
    w_refs, (b_ref, o_ref) = refs[:ADA_SPLIT], refs[ADA_SPLIT:]
    pad = jnp.zeros((COND_ROWS - 1 - DEC_BATCH, D_MODEL), F32)
    cond = jnp.concatenate([cc_ref[...], c_ref[...], pad], axis=0)
    s = _silu(cond).astype(BF)
    bias = b_ref[pl.ds(pl.program_id(0), 1), :]
    width = TN_ADA // ADA_SPLIT
    for p, w_ref in enumerate(w_refs):
        cols = slice(p * width, (p + 1) * width)
        o_ref[:, cols] = jnp.dot(s, w_ref[...].astype(BF), preferred_element_type=F32) + bias[:, cols]


def _ada(c_ctx, c, w_ada, b_ada):
    n = N_MOD * D_MODEL
    width = TN_ADA // ADA_SPLIT
    w_specs = [pl.BlockSpec((None, D_MODEL, width), lambda l, j, p=p: (l, 0, j * ADA_SPLIT + p))
               for p in range(ADA_SPLIT)]
    return pl.pallas_call(
        _ada_kernel,
        grid=(DEPTH, n // TN_ADA),
        in_specs=[
            WHOLE, WHOLE,
            *w_specs,
            pl.BlockSpec((DEPTH, TN_ADA), lambda l, j: (0, j)),
        ],
        out_specs=pl.BlockSpec((None, COND_ROWS, TN_ADA), lambda l, j: (l, 0, j)),
        out_shape=jax.ShapeDtypeStruct((DEPTH, COND_ROWS, n), F32),
        compiler_params=_params(("arbitrary", "arbitrary")),
        name="ada_mod",
    )(c_ctx.reshape(1, D_MODEL), c, *([w_ada] * ADA_SPLIT), b_ada)


def _in_weight_pieces(layer):
    pieces = []
    for c0 in range(0, IN_COLS, STAGE_W_COLS):
        g, off = divmod(c0, TN_IN)
        pieces.append((0, layer, slice(0, D_MODEL), slice(c0, c0 + STAGE_W_COLS),
                       g, slice(0, D_MODEL), slice(off, off + STAGE_W_COLS)))
    return pieces


def _in_kernel(*refs, layer, aliased):
    xa_ref, xb_ref, mod_ref, g_ref, w_hbm, cos_ref, sin_ref, cw_ref = refs[:8]
    refs = refs[10:] if aliased else refs[8:]
    q_ref, v_ref, kt_ref, conv_ref, nk_ref, nv_ref, h_s, wb_s, gb_s, gc_s, stage_s, sem = refs
    i = pl.program_id(0)

    @pl.when(i == 0)
    def _():
        _load_weights(_in_weight_pieces(layer), (w_hbm,), (wb_s,), stage_s, sem)

    chunks = [slice(c * ROW_CHUNK, (c + 1) * ROW_CHUNK) for c in range(TM // ROW_CHUNK)]
    seqs_per_chunk = ROW_CHUNK // SEQ

    def norm(x_ref, rows):
        gain = g_ref[layer:layer + 1, :] * (1.0 + _mod(mod_ref, layer, i, 1))
        h_s[rows, :] = (_rms(x_ref[rows, :]) * gain + _mod(mod_ref, layer, i, 0)).astype(BF)

    def proj(rows, group):
        acc = jnp.dot(h_s[rows, :], wb_s[group], preferred_element_type=F32)
        return acc[:, :ATT_WIDTH], acc[:, ATT_WIDTH:]

    def roped(a, rows):
        cos = jnp.concatenate([cos_ref[rows, :]] * N_HEADS, axis=1)
        sin = jnp.concatenate([sin_ref[rows, :]] * N_HEADS, axis=1)
        lane = lax.broadcasted_iota(jnp.int32, a.shape, 1)
        upper = (lane & (QK_DIM // 4)) != 0
        partner = jnp.where(upper, pltpu.roll(a, QK_DIM // 4, 1), pltpu.roll(a, ATT_WIDTH - QK_DIM // 4, 1))
        return a * cos + partner * sin

    def gated_conv(seq):
        for rows in chunks:
            gc, xin = proj(rows, 2)
            gc_s[rows, :] = gc * xin
        u = gc_s[...]
        pos = lax.broadcasted_iota(jnp.int32, (TM, 1), 0) & (seq - 1)
        prev = jnp.where(pos == 0, 0.0, pltpu.roll(u, 1, 0))
        nxt = jnp.where(pos == seq - 1, 0.0, pltpu.roll(u, TM - 1, 0))
        cw = cw_ref[layer]
        conv = prev * cw[0:1] + u * cw[1:2] + nxt * cw[2:3]
        conv_ref[...] = (gb_s[...] * conv).astype(BF)

    @pl.when(i >= CTX_TILES)
    def _():
        for rows in chunks:
            norm(xb_ref, rows)
            q, k = proj(rows, 0)
            q_ref[rows, :] = (roped(q, rows) * Q_SCALE).astype(BF)
            kt_ref[:, rows] = roped(k, rows).T.astype(BF)
        for rows in chunks:
            v, gb = proj(rows, 1)
            v_ref[rows, :] = v.astype(BF)
            gb_s[rows, :] = gb
        gated_conv(DEC_SEQ)

    @pl.when(i < CTX_TILES)
    def _():
        for c, rows in enumerate(chunks):
            norm(xa_ref, rows)
            q, k = proj(rows, 0)
            q_ref[rows, :] = (q * Q_SCALE).astype(BF)
            kt = k.T
            kt_ref[:, rows] = kt.astype(BF)
            for s in range(seqs_per_chunk):
                nk_ref[c * seqs_per_chunk + s] = kt[:, s * SEQ:(s + 1) * SEQ]
        for c, rows in enumerate(chunks):
            v, gb = proj(rows, 1)
            v_ref[rows, :] = v.astype(BF)
            gb_s[rows, :] = gb
            for s in range(seqs_per_chunk):
                for h in range(N_HEADS):
                    nv_ref[c * seqs_per_chunk + s, pl.ds(h, SEQ, stride=N_HEADS), :] = (
                        v[s * SEQ:(s + 1) * SEQ, h * V_DIM:(h + 1) * V_DIM])
        gated_conv(SEQ)


def _in_proj(layer, xs, mod, g_mix, w_in, cos_t, sin_t, conv_w, new_kv):
    pair = isinstance(xs, tuple)
    xa, xb = xs if pair else (xs, xs)
    spec_a, spec_b = _stream_specs(pair)
    ctx_i = lambda i: jnp.minimum(i, CTX_TILES - 1)
    in_specs = [
        spec_a, spec_b,
        WHOLE, WHOLE,
        pl.BlockSpec(memory_space=pl.ANY),
        WHOLE, WHOLE, WHOLE,
    ]
    args = [xa, xb, mod, g_mix, w_in, cos_t, sin_t, conv_w]
    aliases = {}
    if new_kv is not None:
        in_specs += [pl.BlockSpec(memory_space=pl.ANY), pl.BlockSpec(memory_space=pl.ANY)]
        args += list(new_kv)
        aliases = {8: 4, 9: 5}
    row_tile = pl.BlockSpec((TM, ATT_WIDTH), lambda i: (i, 0))
    return pl.pallas_call(
        functools.partial(_in_kernel, layer=layer, aliased=new_kv is not None),
        grid=(N_TILES,),
        in_specs=in_specs,
        out_specs=[
            row_tile,
            row_tile,
            pl.BlockSpec((None, ATT_WIDTH, TM), lambda i: (i, 0, 0)),
            row_tile,
            pl.BlockSpec((SEQ_PER_TILE, None, ATT_WIDTH, SEQ), lambda i: (ctx_i(i), layer, 0, 0)),
            pl.BlockSpec((SEQ_PER_TILE, None, SEQ * N_HEADS, V_DIM), lambda i: (ctx_i(i), layer, 0, 0)),
        ],
        out_shape=[
            jax.ShapeDtypeStruct((N_TOK, ATT_WIDTH), BF),
            jax.ShapeDtypeStruct((N_TOK, ATT_WIDTH), BF),
            jax.ShapeDtypeStruct((N_TILES, ATT_WIDTH, TM), BF),
            jax.ShapeDtypeStruct((N_TOK, CONV_WIDTH), BF),
            jax.ShapeDtypeStruct((BATCH, DEPTH, ATT_WIDTH, SEQ), F32),
            jax.ShapeDtypeStruct((BATCH, DEPTH, SEQ * N_HEADS, V_DIM), F32),
        ],
        scratch_shapes=[
            pltpu.VMEM((TM, D_MODEL), BF),
            pltpu.VMEM((N_IN_TILES, D_MODEL, TN_IN), BF),
            pltpu.VMEM((TM, CONV_WIDTH), F32),
            pltpu.VMEM((TM, CONV_WIDTH), F32),
            pltpu.VMEM((IN_STAGE_SLOTS, STAGE_W_ROWS, STAGE_W_COLS), F32),
            pltpu.SemaphoreType.DMA((IN_STAGE_SLOTS,)),
        ],
        input_output_aliases=aliases,
        compiler_params=_params(("arbitrary",)),
        name=f"in_proj_l{layer}",
    )(*args)


def _lambda(lq_ref, layer, lam_init):
    lq = lq_ref[layer]
    a = jnp.exp(jnp.sum(lq[0:1] * lq[1:2], axis=-1, keepdims=True))
    b = jnp.exp(jnp.sum(lq[2:3] * lq[3:4], axis=-1, keepdims=True))
    return a - b + lam_init


def _head_norm(o, sg, lam_init):
    return _rms(o) * sg * (1.0 - lam_init)


def _attn_ctx_kernel(q_ref, kt_ref, v_ref, lq_ref, sg_ref, o_ref, sc_s, *, layer, lam_init):
    lam = _lambda(lq_ref, layer, lam_init)
    sg = sg_ref[layer:layer + 1, :]

    def sequence(b, carry):
        pos = pl.ds(pl.multiple_of(b * SEQ, SEQ), SEQ)

        def scores(h):
            for s in range(2):
                d = slice(h * V_DIM + s * QK_DIM, h * V_DIM + (s + 1) * QK_DIM)
                sc_s[h % 2, s] = jnp.dot(q_ref[pos, d], kt_ref[d, pos], preferred_element_type=F32)

        def finish(h):
            cols = slice(h * V_DIM, (h + 1) * V_DIM)
            v = v_ref[pos, cols]
            outs = []
            for s in range(2):
                sc = sc_s[h % 2, s]
                e = jnp.exp2(sc - jnp.max(sc, axis=-1, keepdims=True))
                r = 1.0 / jnp.sum(e, axis=-1, keepdims=True)
                outs.append(jnp.dot(e.astype(BF), v, preferred_element_type=F32) * r)
            o = outs[0] - lam * outs[1]
            o_ref[pos, cols] = _head_norm(o, sg, lam_init).astype(BF)

        scores(0)
        for h in range(N_HEADS):
            if h + 1 < N_HEADS:
                scores(h + 1)
            finish(h)
        return carry

    lax.fori_loop(0, CTX_SEQ_PER_STEP, sequence, 0)


def _attn_ctx(layer, q, kt, v, lambda_qk, subln_g, lam_init):
    rows = CTX_SEQ_PER_STEP * SEQ
    per_tile = TM // rows
    return pl.pallas_call(
        functools.partial(_attn_ctx_kernel, layer=layer, lam_init=lam_init),
        grid=(N_CTX // rows,),
        in_specs=[
            pl.BlockSpec((rows, ATT_WIDTH), lambda b: (b, 0)),
            pl.BlockSpec((None, ATT_WIDTH, rows), lambda b: (b // per_tile, 0, b % per_tile)),
            pl.BlockSpec((rows, ATT_WIDTH), lambda b: (b, 0)),
            WHOLE, WHOLE,
        ],
        out_specs=pl.BlockSpec((rows, ATT_WIDTH), lambda b: (b, 0)),
        out_shape=jax.ShapeDtypeStruct((N_TOK, ATT_WIDTH), BF),
        scratch_shapes=[pltpu.VMEM((2, 2, SEQ, SEQ), F32)],
        compiler_params=_params(("arbitrary",)),
        name=f"attn_ctx_l{layer}",
    )(q, kt, v, lambda_qk, subln_g)


TQ = 256
LAT_Q_PER_STEP = 2


def _attn_lat_kernel(q_ref, kt_ref, v_ref, ckt_ref, cv_ref, lq_ref, sg_ref, att_in_ref, o_ref, sc_s, *,
                     layer, lam_init):
    del att_in_ref
    lam = _lambda(lq_ref, layer, lam_init)
    sg = sg_ref[layer:layer + 1, :]

    units = [(b, h) for b in range(LAT_Q_PER_STEP) for h in range(N_HEADS)]

    def scores(u, s):
        b, h = units[u]
        d = slice(h * V_DIM + s * QK_DIM, h * V_DIM + (s + 1) * QK_DIM)
        q = q_ref[b * TQ:(b + 1) * TQ, d]
        sc_s[u % 2, s, :, :PAST_LEN] = jnp.dot(q, ckt_ref[d, :].astype(BF), preferred_element_type=F32)
        sc_s[u % 2, s, :, PAST_LEN:] = jnp.dot(q, kt_ref[d, :], preferred_element_type=F32)

    def softmax(u, s):
        sc = sc_s[u % 2, s]
        e = jnp.exp2(sc - jnp.max(sc, axis=-1, keepdims=True))
        return e, 1.0 / jnp.sum(e, axis=-1, keepdims=True)

    def finish(u, p1, p2):
        b, h = units[u]
        cols = slice(h * V_DIM, (h + 1) * V_DIM)
        e = jnp.concatenate([p1[0].astype(BF), p2[0].astype(BF)], axis=0)
        vc = cv_ref[pl.ds(h, PAST_LEN, stride=N_HEADS), :].astype(BF)
        pv = jnp.dot(e[:, :PAST_LEN], vc, preferred_element_type=F32)
        pv = pv + jnp.dot(e[:, PAST_LEN:], v_ref[:, cols], preferred_element_type=F32)
        o = pv[:TQ] * p1[1] - pv[TQ:] * (lam * p2[1])
        o_ref[b * TQ:(b + 1) * TQ, cols] = _head_norm(o, sg, lam_init).astype(BF)

    scores(0, 0)
    scores(0, 1)
    for u in range(len(units)):
        more = u + 1 < len(units)
        if more:
            scores(u + 1, 0)
        p1 = softmax(u, 0)
        if more:
            scores(u + 1, 1)
        finish(u, p1, softmax(u, 1))


def _attn_lat(layer, q, kt, v, cache_kt, cache_v, lambda_qk, subln_g, att, lam_init):
    rows = LAT_Q_PER_STEP * TQ
    nqb = DEC_SEQ // rows
    q0 = N_CTX // rows
    return pl.pallas_call(
        functools.partial(_attn_lat_kernel, layer=layer, lam_init=lam_init),
        grid=(DEC_BATCH, nqb),
        in_specs=[
            pl.BlockSpec((rows, ATT_WIDTH), lambda b, t: (q0 + b * nqb + t, 0)),
            pl.BlockSpec((None, ATT_WIDTH, DEC_SEQ), lambda b, t: (CTX_TILES + b, 0, 0)),
            pl.BlockSpec((DEC_SEQ, ATT_WIDTH), lambda b, t: (CTX_TILES + b, 0)),
            pl.BlockSpec((None, None, ATT_WIDTH, PAST_LEN), lambda b, t: (b, layer, 0, 0)),
            pl.BlockSpec((None, None, PAST_LEN * N_HEADS, V_DIM), lambda b, t: (b, layer, 0, 0)),
            WHOLE, WHOLE,
            pl.BlockSpec(memory_space=pl.ANY),
        ],
        out_specs=pl.BlockSpec((rows, ATT_WIDTH), lambda b, t: (q0 + b * nqb + t, 0)),
        out_shape=jax.ShapeDtypeStruct((N_TOK, ATT_WIDTH), BF),
        scratch_shapes=[pltpu.VMEM((2, 2, TQ, PAST_LEN + DEC_SEQ), F32)],
        input_output_aliases={7: 0},
        compiler_params=_params(("arbitrary", "arbitrary")),
        name=f"attn_lat_l{layer}",
    )(q, kt, v, cache_kt, cache_v, lambda_qk, subln_g, att)


def _out_kernel(att_ref, conv_ref, w_ref, xa_ref, xb_ref, mod_ref, gf_ref, *rest, layer, route):
    if route:
        wr_ref, xo_ref, h2_ref, rt_ref, n_ref, wb_s = rest
    else:
        xo_ref, h2_ref, wb_s = rest
    i = pl.program_id(0)

    @pl.when(i == 0)
    def _():
        wb_s[...] = w_ref[...].astype(BF)

    lat = i >= CTX_TILES
    gain = gf_ref[layer:layer + 1, :] * (1.0 + _mod(mod_ref, layer, i, 4))
    for c in range(TM // ROW_CHUNK):
        rows = slice(c * ROW_CHUNK, (c + 1) * ROW_CHUNK)
        mo = jnp.dot(att_ref[rows, :], wb_s[:ATT_WIDTH, :], preferred_element_type=F32)
        mo = mo + jnp.dot(conv_ref[rows, :], wb_s[ATT_WIDTH:, :], preferred_element_type=F32)
        xn = jnp.where(lat, xb_ref[rows, :], xa_ref[rows, :]) + _mod(mod_ref, layer, i, 2) * mo
        xo_ref[rows, :] = xn
        h2_ref[rows, :] = (_rms(xn) * gain + _mod(mod_ref, layer, i, 3)).astype(BF)
    if route:
        _route(h2_ref[...], wr_ref, rt_ref, n_ref)


def _out_proj(layer, att, conv, w_out, xs, mod, g_ffn, w_router_pad=None):
    pair = isinstance(xs, tuple)
    xa, xb = xs if pair else (xs, xs)
    spec_a, spec_b = _stream_specs(pair)
    row_spec = pl.BlockSpec((TM, D_MODEL), lambda i: (i, 0))
    lane_spec = pl.BlockSpec((TM, LANES), lambda i: (i, 0))
    in_specs = [
        pl.BlockSpec((TM, ATT_WIDTH), lambda i: (i, 0)),
        pl.BlockSpec((TM, CONV_WIDTH), lambda i: (i, 0)),
        pl.BlockSpec((None, D_MODEL, D_MODEL), lambda i: (layer, 0, 0)),
        spec_a, spec_b,
        WHOLE, WHOLE,
    ]
    args = [att, conv, w_out, xa, xb, mod, g_ffn]
    out_specs = [row_spec, row_spec]
    out_shape = [jax.ShapeDtypeStruct((N_TOK, D_MODEL), F32), jax.ShapeDtypeStruct((N_TOK, D_MODEL), BF)]
    route = w_router_pad is not None
    if route:
        in_specs.append(WHOLE)
        args.append(w_router_pad)
        out_specs += [lane_spec, pl.BlockSpec((None, SUBLANES, LANES), lambda i: (i, 0, 0))]
        out_shape += [jax.ShapeDtypeStruct((N_TOK, LANES), F32),
                      jax.ShapeDtypeStruct((N_TILES, SUBLANES, LANES), jnp.int32)]
    return pl.pallas_call(
        functools.partial(_out_kernel, layer=layer, route=route),
        grid=(N_TILES,),
        in_specs=in_specs,
        out_specs=out_specs,
        out_shape=out_shape,
        scratch_shapes=[pltpu.VMEM((D_MODEL, D_MODEL), BF)],
        compiler_params=_params(("arbitrary",)),
        name=f"out_proj_l{layer}",
    )(*args)


def _weight_pieces(layer):
    pieces = []
    for c0 in range(0, D_MODEL, STAGE_W_COLS):
        cols = slice(c0, c0 + STAGE_W_COLS)
        pieces.append((0, layer, slice(0, D_MODEL), cols, None, slice(0, D_MODEL), cols))
    for half in range(FF_HALVES):
        for part in range(2):
            src0 = part * D_FF + half * FF_HALF
            for off in range(0, FF_HALF, STAGE_W_COLS):
                n = min(STAGE_W_COLS, FF_HALF - off)
                pieces.append((1, 0, slice(0, D_MODEL), slice(src0 + off, src0 + off + n),
                               half, slice(0, D_MODEL), slice(part * FF_HALF + off, part * FF_HALF + off + n)))
    for r0 in range(0, D_FF, STAGE_W_ROWS):
        rows = slice(r0, min(r0 + STAGE_W_ROWS, D_FF))
        for c0 in range(0, D_MODEL, STAGE_W_COLS):
            cols = slice(c0, c0 + STAGE_W_COLS)
            pieces.append((2, 0, rows, cols, None, rows, cols))
    return pieces


def _load_weights(pieces, hbm, resident, stage_s, sem):
    slots = stage_s.shape[0]

    def copy(k):
        src, idx, rows, cols, _, _, _ = pieces[k]
        nr, nc = rows.stop - rows.start, cols.stop - cols.start
        slot = k % slots
        return pltpu.make_async_copy(hbm[src].at[idx, rows, cols], stage_s.at[slot, :nr, :nc], sem.at[slot])

    for k in range(min(slots - 1, len(pieces))):
        copy(k).start()
    for k, (src, _, rows, cols, didx, drows, dcols) in enumerate(pieces):
        if k + slots - 1 < len(pieces):
            copy(k + slots - 1).start()
        copy(k).wait()
        nr, nc = rows.stop - rows.start, cols.stop - cols.start
        piece = stage_s[k % slots, :nr, :nc].astype(BF)
        if didx is None:
            resident[src][drows, dcols] = piece
        else:
            resident[src][didx, drows, dcols] = piece


def _dense_layer_kernel(att_ref, conv_ref, xa_ref, xb_ref, mod_ref, gf_ref, wo_hbm, wgu_hbm, wd_hbm,
                        o_ref, wo_b, wgu_b, wd_b, stage_s, sem, *, layer):
    i = pl.program_id(0)
    tile = i // (TM // FF_TM)

    @pl.when(i == 0)
    def _():
        _load_weights(_weight_pieces(layer), (wo_hbm, wgu_hbm, wd_hbm), (wo_b, wgu_b, wd_b), stage_s, sem)

    mo = jnp.dot(att_ref[...], wo_b[:ATT_WIDTH, :], preferred_element_type=F32)
    mo = mo + jnp.dot(conv_ref[...], wo_b[ATT_WIDTH:, :], preferred_element_type=F32)
    x = jnp.where(tile >= CTX_TILES, xb_ref[...], xa_ref[...])
    xn = x + _mod(mod_ref, layer, tile, 2) * mo
    h = (_rms(xn) * gf_ref[layer:layer + 1, :]) * (1.0 + _mod(mod_ref, layer, tile, 4)) + _mod(mod_ref, layer, tile, 3)
    h = h.astype(BF)
    y = None
    for half in range(FF_HALVES):
        gu = jnp.dot(h, wgu_b[half], preferred_element_type=F32)
        act = (_silu(gu[:, :FF_HALF]) * gu[:, FF_HALF:]).astype(BF)
        part = jnp.dot(act, wd_b[half * FF_HALF:(half + 1) * FF_HALF, :], preferred_element_type=F32)
        y = part if y is None else y + part
    o_ref[...] = xn + _mod(mod_ref, layer, tile, 5) * y


def _dense_layer(layer, att, conv, xs, mod, g_ffn, w_out, w_gu, w_down):
    xa, xb = xs
    n_ctx = N_CTX // FF_TM
    row = lambda width: pl.BlockSpec((FF_TM, width), lambda i: (i, 0))
    hbm = pl.BlockSpec(memory_space=pl.ANY)
    return pl.pallas_call(
        functools.partial(_dense_layer_kernel, layer=layer),
        grid=(N_TOK // FF_TM,),
        in_specs=[
            row(ATT_WIDTH), row(CONV_WIDTH),
            pl.BlockSpec((FF_TM, D_MODEL), lambda i: (jnp.minimum(i, n_ctx - 1), 0)),
            pl.BlockSpec((FF_TM, D_MODEL), lambda i: (jnp.maximum(i - n_ctx, 0), 0)),
            WHOLE, WHOLE, hbm, hbm, hbm,
        ],
        out_specs=row(D_MODEL),
        out_shape=jax.ShapeDtypeStruct((N_TOK, D_MODEL), F32),
        scratch_shapes=[
            pltpu.VMEM((D_MODEL, D_MODEL), BF),
            pltpu.VMEM((FF_HALVES, D_MODEL, 2 * FF_HALF), BF),
            pltpu.VMEM((D_FF, D_MODEL), BF),
            pltpu.VMEM((FF_STAGE_SLOTS, STAGE_W_ROWS, STAGE_W_COLS), F32),
            pltpu.SemaphoreType.DMA((FF_STAGE_SLOTS,)),
        ],
        compiler_params=_params(("arbitrary",)),
        name="dense_layer",
    )(att, conv, xa, xb, mod, g_ffn, w_out, w_gu, w_down)


def _route(h, wr_ref, rt_ref, n_ref):
    logits = jnp.dot(h, wr_ref[...].astype(BF), preferred_element_type=F32)
    lane = lax.broadcasted_iota(jnp.int32, logits.shape, 1)
    lg = jnp.where(lane < N_EXPERTS, logits, -jnp.inf)
    m1 = jnp.max(lg, axis=-1, keepdims=True)
    i1 = jnp.min(jnp.where(lg == m1, lane, LANES), axis=-1, keepdims=True)
    lg2 = jnp.where(lane == i1, -jnp.inf, lg)
    m2 = jnp.max(lg2, axis=-1, keepdims=True)
    i2 = jnp.min(jnp.where(lg2 == m2, lane, LANES), axis=-1, keepdims=True)
    e2 = jnp.exp(m2 - m1)
    w1 = 1.0 / (1.0 + e2)
    w2 = e2 / (1.0 + e2)

    sel1 = lane == i1
    sel2 = lane == i2
    onehot = jnp.logical_or(sel1, sel2)
    rows = lax.broadcasted_iota(jnp.int32, (ST, ST), 0)
    colsi = lax.broadcasted_iota(jnp.int32, (ST, ST), 1)
    earlier = (colsi < rows).astype(BF)
    onehot_b = onehot.astype(BF)
    before = jnp.concatenate(
        [jnp.dot(earlier, onehot_b[s * ST:(s + 1) * ST], preferred_element_type=F32) for s in range(SUB_PER_TILE)],
        axis=0)
    onehot_f = onehot.astype(F32)
    counts = [jnp.sum(onehot_f[s * ST:(s + 1) * ST], axis=0, keepdims=True) for s in range(SUB_PER_TILE)]
    counts = jnp.concatenate(counts + [jnp.zeros((SUBLANES - SUB_PER_TILE, LANES), F32)], axis=0).astype(jnp.int32)
    seg_len = ((counts + (SEG_ALIGN - 1)) // SEG_ALIGN) * SEG_ALIGN
    n_ref[...] = seg_len
    la = lax.broadcasted_iota(jnp.int32, (LANES, LANES), 0)
    lb = lax.broadcasted_iota(jnp.int32, (LANES, LANES), 1)
    seg_start = jnp.dot(seg_len.astype(F32).astype(BF), (la < lb).astype(BF), preferred_element_type=F32)
    start = jnp.concatenate(
        [jnp.broadcast_to(seg_start[s:s + 1], (ST, LANES)) for s in range(SUB_PER_TILE)], axis=0)
    where = before + start
    lp1 = jnp.sum(jnp.where(sel1, where, 0.0), axis=-1, keepdims=True)
    lp2 = jnp.sum(jnp.where(sel2, where, 0.0), axis=-1, keepdims=True)
    rt = jnp.where(lane == GATE_LANE, w1, jnp.where(lane == GATE_LANE + 1, w2, 0.0))
    rt_ref[...] = jnp.where(lane == ROW_LANE, lp1, jnp.where(lane == ROW_LANE + 1, lp2, rt))


def _chunk_copies(s, cnt_ref, cdst_ref, stage, rows_hbm, sem, *, to_hbm, wait):
    def copy(v, h):
        return pltpu.make_async_copy(v, h, sem) if to_hbm else pltpu.make_async_copy(h, v, sem)

    if wait:
        for z in WAIT_PIECES:
            @pl.when((cnt_ref[s] & z) != 0)
            def _():
                copy(stage.at[pl.ds(0, z * SEG_ALIGN)], rows_hbm.at[pl.ds(0, z * SEG_ALIGN)]).wait()
        return

    def start(c, priority):
        v = stage.at[pl.ds(pl.multiple_of(c * SEG_ALIGN, SEG_ALIGN), SEG_ALIGN)]
        h = rows_hbm.at[pl.ds(pl.multiple_of(cdst_ref[s * STAGE_CHUNKS + c], SEG_ALIGN), SEG_ALIGN)]
        copy(v, h).start(priority=priority)

    def pair(p, carry):
        start(2 * p, 0)
        start(2 * p + 1, 1)
        return carry

    n = cnt_ref[s]
    lax.fori_loop(0, n // 2, pair, 0)

    @pl.when((n & 1) != 0)
    def _():
        start(n - 1, 0)


def _dispatch_kernel(cnt_ref, cdst_ref, h_ref, rt_ref, xs_ref, stage_s, sem):
    copies = functools.partial(_chunk_copies, cnt_ref=cnt_ref, cdst_ref=cdst_ref, rows_hbm=xs_ref, to_hbm=True)
    for k in range(SUB_PER_TILE):
        s = pl.program_id(0) * SUB_PER_TILE + k
        slot = k % 2
        rows = slice(k * ST, (k + 1) * ST)

        @pl.when(s >= 2)
        def _():
            copies(s - 2, stage=stage_s.at[slot], sem=sem.at[slot], wait=True)

        at = rt_ref[rows, :].T[ROW_LANE:ROW_LANE + 2, :].astype(jnp.int32)
        r = lax.broadcasted_iota(jnp.int32, (STAGE_ROWS, ST), 0)
        pick = jnp.logical_or(r == at[0:1, :], r == at[1:2, :]).astype(BF)
        stage_s[slot] = jnp.dot(pick, h_ref[rows, :], preferred_element_type=F32).astype(BF)
        copies(s, stage=stage_s.at[slot], sem=sem.at[slot], wait=False)

    @pl.when(pl.program_id(0) == N_TILES - 1)
    def _():
        copies(N_SUB - 2, stage=stage_s.at[0], sem=sem.at[0], wait=True)
        copies(N_SUB - 1, stage=stage_s.at[1], sem=sem.at[1], wait=True)


def _dispatch(cnt, cdst, h2, rt):
    assert SUB_PER_TILE % 2 == 0
    grid_spec = pltpu.PrefetchScalarGridSpec(
        num_scalar_prefetch=2,
        grid=(N_TILES,),
        in_specs=[
            pl.BlockSpec((TM, D_MODEL), lambda i, *_: (i, 0)),
            pl.BlockSpec((TM, LANES), lambda i, *_: (i, 0)),
        ],
        out_specs=pl.BlockSpec(memory_space=pl.ANY),
        scratch_shapes=[pltpu.VMEM((2, STAGE_ROWS, D_MODEL), BF), pltpu.SemaphoreType.DMA((2,))],
    )
    return pl.pallas_call(
        _dispatch_kernel,
        grid_spec=grid_spec,
        out_shape=jax.ShapeDtypeStruct((R_PAD, D_MODEL), BF),
        compiler_params=_params(("arbitrary",)),
        name="moe_dispatch",
    )(cnt, cdst, h2, rt)


def _expert_weights(r, te_ref, nt_ref, nxt_ref, w_hbm, wf_s, wb_s, sem):
    part = wf_s.shape[0] // EXPERT_FETCH_PARTS

    def start(e):
        for p in range(EXPERT_FETCH_PARTS):
            rows = pl.ds(p * part, part)
            pltpu.make_async_copy(w_hbm.at[0, e, rows], wf_s.at[rows], sem).start()

    def wait(e):
        pltpu.make_async_copy(w_hbm.at[0, e], wf_s, sem).wait()

    @pl.when(r == 0)
    def _():
        start(te_ref[0])

    first = jnp.logical_or(r == 0, te_ref[r] != te_ref[jnp.maximum(r - 1, 0)])

    @pl.when(jnp.logical_and(r < nt_ref[0], first))
    def _():
        wait(te_ref[r])
        wb_s[...] = wf_s[...].astype(BF)

        @pl.when(nxt_ref[r] >= 0)
        def _():
            start(nxt_ref[r])


def _moe_ffn_kernel(te_ref, nt_ref, nxt_ref, used_ref, x_ref, wgu_hbm, wd_hbm, o_ref,
                    wgu_f, wgu_b, wd_f, wd_b, sem):
    def ffn(rows):
        gu = jnp.dot(x_ref[rows, :], wgu_b[...], preferred_element_type=F32)
        act = (_silu(gu[:, :D_FF_EXPERT]) * gu[:, D_FF_EXPERT:]).astype(BF)
        o_ref[rows, :] = jnp.dot(act, wd_b[...], preferred_element_type=F32).astype(BF)

    for k in range(TILES_PER_STEP):
        r = pl.program_id(0) * TILES_PER_STEP + k
        _expert_weights(r, te_ref, nt_ref, nxt_ref, wgu_hbm, wgu_f, wgu_b, sem.at[0])
        _expert_weights(r, te_ref, nt_ref, nxt_ref, wd_hbm, wd_f, wd_b, sem.at[1])

        @pl.when(jnp.logical_and(r < nt_ref[0], used_ref[r] > TG // 2))
        def _():
            ffn(slice(k * TG, (k + 1) * TG))

        @pl.when(jnp.logical_and(r < nt_ref[0], used_ref[r] <= TG // 2))
        def _():
            ffn(slice(k * TG, k * TG + TG // 2))


def _moe_ffn(te, nt, nxt, used, rows, w_gu, w_down):
    tile_map = lambda i, te, nt, nxt, used: (jnp.minimum(i, (nt[0] - 1) // TILES_PER_STEP), 0)
    grid_spec = pltpu.PrefetchScalarGridSpec(
        num_scalar_prefetch=4,
        grid=(NT_G // TILES_PER_STEP,),
        in_specs=[pl.BlockSpec((TILES_PER_STEP * TG, D_MODEL), tile_map),
                  pl.BlockSpec(memory_space=pl.ANY), pl.BlockSpec(memory_space=pl.ANY)],
        out_specs=pl.BlockSpec((TILES_PER_STEP * TG, D_MODEL), tile_map),
        scratch_shapes=[
            pltpu.VMEM((D_MODEL, 2 * D_FF_EXPERT), F32), pltpu.VMEM((D_MODEL, 2 * D_FF_EXPERT), BF),
            pltpu.VMEM((D_FF_EXPERT, D_MODEL), F32), pltpu.VMEM((D_FF_EXPERT, D_MODEL), BF),
            pltpu.SemaphoreType.DMA((2,)),
        ],
    )
    return pl.pallas_call(
        _moe_ffn_kernel,
        grid_spec=grid_spec,
        out_shape=jax.ShapeDtypeStruct((R_PAD, D_MODEL), BF),
        compiler_params=_params(("arbitrary",)),
        name="moe_ffn",
    )(te, nt, nxt, used, rows, w_gu, w_down)


def _combine_kernel(cnt_ref, cdst_ref, ys_ref, rt_ref, x_ref, mod_ref, fg_ref,
                    oa_ref, ob_ref, stage_s, sem, *, layer):
    i = pl.program_id(0)
    copies = functools.partial(_chunk_copies, cnt_ref=cnt_ref, cdst_ref=cdst_ref, rows_hbm=ys_ref, to_hbm=False)

    @pl.when(i == 0)
    def _():
        stage_s[...] = jnp.zeros_like(stage_s)
        for s in range(COMBINE_AHEAD):
            copies(s, stage=stage_s.at[s], sem=sem.at[s], wait=False)

    for k in range(SUB_PER_TILE):
        s = i * SUB_PER_TILE + k
        slot = k % COMBINE_SLOTS
        ahead = (k + COMBINE_AHEAD) % COMBINE_SLOTS
        rows = slice(k * ST, (k + 1) * ST)

        @pl.when(s + COMBINE_AHEAD < N_SUB)
        def _():
            copies(s + COMBINE_AHEAD, stage=stage_s.at[ahead], sem=sem.at[ahead], wait=False)

        copies(s, stage=stage_s.at[slot], sem=sem.at[slot], wait=True)

        rt = rt_ref[rows, :]
        at = rt[:, ROW_LANE:ROW_LANE + 2].astype(jnp.int32)
        r = lax.broadcasted_iota(jnp.int32, (ST, STAGE_ROWS), 1)
        staged = stage_s[slot]
        a = jnp.dot((r == at[:, 0:1]).astype(BF), staged, preferred_element_type=F32)
        b = jnp.dot((r == at[:, 1:2]).astype(BF), staged, preferred_element_type=F32)
        y = rt[:, GATE_LANE:GATE_LANE + 1] * a + rt[:, GATE_LANE + 1:GATE_LANE + 2] * b
        xn = x_ref[rows, :] + _mod(mod_ref, layer, i, 5) * y
        out = _rms(xn) * fg_ref[...]

        @pl.when(i < CTX_TILES)
        def _():
            oa_ref[rows, :] = out

        @pl.when(i >= CTX_TILES)
        def _():
            ob_ref[rows, :] = out


def _combine(layer, cnt, cdst, ys, rt, x, mod, final_g):
    assert SUB_PER_TILE % COMBINE_SLOTS == 0
    grid_spec = pltpu.PrefetchScalarGridSpec(
        num_scalar_prefetch=2,
        grid=(N_TILES,),
        in_specs=[
            pl.BlockSpec(memory_space=pl.ANY),
            pl.BlockSpec((TM, LANES), lambda i, *_: (i, 0)),
            pl.BlockSpec((TM, D_MODEL), lambda i, *_: (i, 0)),
            WHOLE, WHOLE,
        ],
        out_specs=[
            pl.BlockSpec((TM, D_MODEL), lambda i, *_: (jnp.minimum(i, CTX_TILES - 1), 0)),
            pl.BlockSpec((TM, D_MODEL), lambda i, *_: (jnp.maximum(i - CTX_TILES, 0), 0)),
        ],
        scratch_shapes=[pltpu.VMEM((COMBINE_SLOTS, STAGE_ROWS, D_MODEL), BF),
                        pltpu.SemaphoreType.DMA((COMBINE_SLOTS,))],
    )
    return pl.pallas_call(
        functools.partial(_combine_kernel, layer=layer),
        grid_spec=grid_spec,
        out_shape=[
            jax.ShapeDtypeStruct((N_CTX, D_MODEL), F32),
            jax.ShapeDtypeStruct((N_LAT, D_MODEL), F32),
        ],
        compiler_params=_params(("arbitrary",)),
        name="moe_combine",
    )(cnt, cdst, ys, rt, x, mod, final_g)


def _group_layout(n_tiles):
    n = n_tiles[:, :SUB_PER_TILE, :N_EXPERTS].reshape(N_SUB, N_EXPERTS)
    tiles = (jnp.sum(n, axis=0) + TG - 1) // TG
    tile_end = jnp.cumsum(tiles)
    region = (tile_end - tiles) * TG
    dst = region[None, :] + jnp.cumsum(n, axis=0) - n
    seg_end = jnp.cumsum(n, axis=1)
    seg = seg_end - n
    row = jnp.arange(STAGE_CHUNKS, dtype=jnp.int32) * SEG_ALIGN
    owner = jnp.sum((row[None, :, None] >= seg_end[:, None, :]).astype(jnp.int32), axis=-1)
    own = jnp.minimum(owner, N_EXPERTS - 1)[..., None] == jnp.arange(N_EXPERTS)
    cdst = jnp.sum(jnp.where(own, (dst - seg)[:, None, :], 0), axis=-1) + row[None, :]
    cnt = seg_end[:, -1] // SEG_ALIGN
    nt = tile_end[-1]
    tile_id = jnp.minimum(jnp.arange(NT_G, dtype=jnp.int32), nt - 1)
    te = jnp.sum((tile_id[:, None] >= tile_end[None, :]).astype(jnp.int32), axis=-1)
    after = jnp.sum(jnp.where(te[:, None] == jnp.arange(N_EXPERTS), tile_end[None, :], 0), axis=-1)
    nxt = jnp.where(after < nt, jnp.sum((after[:, None] >= tile_end[None, :]).astype(jnp.int32), axis=-1), -1)
    mine = te[:, None] == jnp.arange(N_EXPERTS)
    region_end = jnp.sum(jnp.where(mine, (region + jnp.sum(n, axis=0))[None, :], 0), axis=-1)
    used = jnp.clip(region_end - tile_id * TG, 0, TG)
    i32 = lambda a: a.astype(jnp.int32)
    return (i32(cnt), i32(cdst.reshape(N_SUB * STAGE_CHUNKS)), i32(te), i32(nt.reshape(1)), i32(nxt), i32(used))


def _rope_tables():
    p = np.arange(DEC_SEQ)
    row = (p // GRID_W).astype(np.float32)
    col = (p % GRID_W).astype(np.float32)
    half = QK_DIM // 4
    freqs = (ROPE_BASE ** (-np.arange(half, dtype=np.float32) / half)).astype(np.float32)
    lane = np.arange(V_DIM)
    f = freqs[lane & (half - 1)]
    use_col = (lane & (2 * half)) != 0
    ang = (np.where(use_col[None, :], col[:, None], row[:, None]) * f[None, :]).astype(np.float32)
    upper = (lane & half) != 0
    sin = np.sin(ang)
    return jnp.asarray(np.cos(ang), F32), jnp.asarray(np.where(upper[None, :], sin, -sin), F32)


def kernel(x_prompt, x_sample, cache_k, cache_v, c, c_ctx, w_ada, b_ada, norm_mix_g, norm_ffn_g,
           w_in, lambda_qk, subln_g, conv_w, w_out, w_gu_dense, w_down_dense, w_router,
           w_gu_moe, w_down_moe, final_g):
    assert DEPTH == 2
    xs = (x_prompt.reshape(N_CTX, D_MODEL), x_sample.reshape(N_LAT, D_MODEL))
    mod = _ada(c_ctx, c, w_ada, b_ada)
    cos_t, sin_t = _rope_tables()
    cache_kt = jnp.transpose(cache_k, (0, 1, 3, 4, 5, 2)).reshape(DEC_BATCH, DEPTH, ATT_WIDTH, PAST_LEN)
    cache_v4 = cache_v.reshape(DEC_BATCH, DEPTH, PAST_LEN * N_HEADS, V_DIM)

    new_kv = None
    for layer in range(DEPTH):
        lam_init = 0.8 - 0.6 * math.exp(-0.3 * layer)
        q, v, kt, conv, nk, nv = _in_proj(layer, xs, mod, norm_mix_g, w_in, cos_t, sin_t, conv_w, new_kv)
        new_kv = (nk, nv)
        att = _attn_ctx(layer, q, kt, v, lambda_qk, subln_g, lam_init)
        att = _attn_lat(layer, q, kt, v, cache_kt, cache_v4, lambda_qk, subln_g, att, lam_init)
        if layer == 0:
            xs = _dense_layer(layer, att, conv, xs, mod, norm_ffn_g, w_out, w_gu_dense, w_down_dense)
        else:
            wr = jnp.pad(w_router[0], ((0, 0), (0, LANES - N_EXPERTS)))
            x1, h2, rt, n_tiles = _out_proj(layer, att, conv, w_out, xs, mod, norm_ffn_g, wr)
            cnt, cdst, te, nt, nxt, used = _group_layout(n_tiles)
            xsort = _dispatch(cnt, cdst, h2, rt)
            ys = _moe_ffn(te, nt, nxt, used, xsort, w_gu_moe, w_down_moe)
            y_ctx, y_lat = _combine(layer, cnt, cdst, ys, rt, x1, mod, final_g.reshape(1, D_MODEL))
    nk, nv = new_kv
    new_k = jnp.transpose(nk.reshape(BATCH, DEPTH, N_HEADS, 2, QK_DIM, SEQ), (0, 1, 5, 2, 3, 4))
    new_v = nv.reshape(BATCH, DEPTH, SEQ, N_HEADS, V_DIM)
    return (y_ctx.reshape(BATCH, SEQ, D_MODEL), y_lat.reshape(DEC_BATCH, DEC_SEQ, D_MODEL), new_k, new_v)
```

```python
import functools
import math

import numpy as np
import jax
import jax.numpy as jnp
from jax import lax
from jax.experimental import pallas as pl
from jax.experimental.pallas import tpu as pltpu

D_MODEL = 1024
BATCH = 16
SEQ = 256
DEPTH = 2
DEC_BATCH = 4
DEC_SEQ = 1024
PAST_LEN = 512
GRID_W = 64
ATT_WIDTH = 512
CONV_WIDTH = 512
N_HEADS = 4
V_DIM = 128
QK_DIM = 64
ROPE_BASE = 10000.0
D_FF = 2816
N_EXPERTS = 8
D_FF_EXPERT = 1408
N_MOD = 6
NORM_EPS = 1e-6
Q_SCALE = QK_DIM ** -0.5 * math.log2(math.e)
IN_COLS = 3 * ATT_WIDTH + 3 * CONV_WIDTH

N_CTX = BATCH * SEQ
N_LAT = DEC_BATCH * DEC_SEQ
N_TOK = N_CTX + N_LAT
TM = 1024
N_TILES = N_TOK // TM
CTX_TILES = N_CTX // TM
SEQ_PER_TILE = TM // SEQ
CTX_SEQ_PER_STEP = 4
COND_ROWS = 8
TN_IN = 1024
N_IN_TILES = IN_COLS // TN_IN
ROW_CHUNK = 512
FF_TM = 512
FF_HALVES = 2
FF_HALF = D_FF // FF_HALVES
STAGE_W_ROWS = 1024
STAGE_W_COLS = 512
FF_STAGE_SLOTS = 6
IN_STAGE_SLOTS = 3
TN_ADA = 1536
TG = 512
ST = 256
SUB_PER_TILE = TM // ST
N_SUB = N_TOK // ST
SEG_ALIGN = 8
STAGE_ROWS = 640
STAGE_CHUNKS = STAGE_ROWS // SEG_ALIGN
WAIT_PIECES = (64, 32, 16, 8, 4, 2, 1)
COMBINE_SLOTS = 4
COMBINE_AHEAD = 2
TILES_PER_STEP = 2
_MAX_SORTED_ROWS = 2 * N_TOK + N_SUB * N_EXPERTS * (SEG_ALIGN - 1) + N_EXPERTS * (TG - SEG_ALIGN)
NT_G = -(-_MAX_SORTED_ROWS // (TG * TILES_PER_STEP)) * TILES_PER_STEP
R_PAD = NT_G * TG
LANES = 128
SUBLANES = 8
GATE_LANE = 0
ROW_LANE = 2
VMEM_LIMIT = 60 * 1024 * 1024

BF = jnp.bfloat16
F32 = jnp.float32


def _params(sem, vmem=VMEM_LIMIT):
    return pltpu.CompilerParams(dimension_semantics=sem, vmem_limit_bytes=vmem)


def _mod_row(i):
    return jnp.where(i < CTX_TILES, 0, i - (CTX_TILES - 1))


WHOLE = pl.BlockSpec(memory_space=pltpu.VMEM)


def _mod(mod_ref, layer, i, c):
    return mod_ref[layer, pl.ds(_mod_row(i), 1), c * D_MODEL:(c + 1) * D_MODEL]


def _stream_specs(pair, width=D_MODEL):
    a = pl.BlockSpec((TM, width), lambda i, *_: (jnp.minimum(i, CTX_TILES - 1), 0))
    if pair:
        b = pl.BlockSpec((TM, width), lambda i, *_: (jnp.maximum(i - CTX_TILES, 0), 0))
    else:
        b = pl.BlockSpec((TM, width), lambda i, *_: (jnp.maximum(i, CTX_TILES), 0))
    return a, b


def _silu(x):
    return x / (1.0 + jnp.exp(-x))


def _rms(x):
    return x * lax.rsqrt(jnp.mean(x * x, axis=-1, keepdims=True) + NORM_EPS)


def _ada_kernel(cc_ref, c_ref, w_ref, b_ref, o_ref):
    pad = jnp.zeros((COND_ROWS - 1 - DEC_BATCH, D_MODEL), F32)
    cond = jnp.concatenate([cc_ref[...], c_ref[...], pad], axis=0)
    s = _silu(cond).astype(BF)
    bias = b_ref[pl.ds(pl.program_id(0), 1), :]
    o_ref[...] = jnp.dot(s, w_ref[...].astype(BF), preferred_element_type=F32) + bias


def _ada(c_ctx, c, w_ada, b_ada):
    n = N_MOD * D_MODEL
    return pl.pallas_call(
        _ada_kernel,
        grid=(DEPTH, n // TN_ADA),
        in_specs=[
            WHOLE, WHOLE,
            pl.BlockSpec((None, D_MODEL, TN_ADA), lambda l, j: (l, 0, j)),
            pl.BlockSpec((DEPTH, TN_ADA), lambda l, j: (0, j)),
        ],
        out_specs=pl.BlockSpec((None, COND_ROWS, TN_ADA), lambda l, j: (l, 0, j)),
        out_shape=jax.ShapeDtypeStruct((DEPTH, COND_ROWS, n), F32),
        compiler_params=_params(("arbitrary", "arbitrary")),
        name="ada_mod",
    )(c_ctx.reshape(1, D_MODEL), c, w_ada, b_ada)


def _in_weight_pieces(layer):
    pieces = []
    for c0 in range(0, IN_COLS, STAGE_W_COLS):
        g, off = divmod(c0, TN_IN)
        pieces.append((0, layer, slice(0, D_MODEL), slice(c0, c0 + STAGE_W_COLS),
                       g, slice(0, D_MODEL), slice(off, off + STAGE_W_COLS)))
    return pieces


def _in_kernel(*refs, layer, aliased):
    xa_ref, xb_ref, mod_ref, g_ref, w_hbm, cos_ref, sin_ref, cw_ref = refs[:8]
    refs = refs[10:] if aliased else refs[8:]
    q_ref, v_ref, kt_ref, conv_ref, nk_ref, nv_ref, h_s, wb_s, gb_s, gc_s, stage_s, sem = refs
    i = pl.program_id(0)

    @pl.when(i == 0)
    def _():
        _load_weights(_in_weight_pieces(layer), (w_hbm,), (wb_s,), stage_s, sem)

    chunks = [slice(c * ROW_CHUNK, (c + 1) * ROW_CHUNK) for c in range(TM // ROW_CHUNK)]
    seqs_per_chunk = ROW_CHUNK // SEQ

    def norm(x_ref, rows):
        gain = g_ref[layer:layer + 1, :] * (1.0 + _mod(mod_ref, layer, i, 1))
        h_s[rows, :] = (_rms(x_ref[rows, :]) * gain + _mod(mod_ref, layer, i, 0)).astype(BF)

    def proj(rows, group):
        acc = jnp.dot(h_s[rows, :], wb_s[group], preferred_element_type=F32)
        return acc[:, :ATT_WIDTH], acc[:, ATT_WIDTH:]

    def roped(a, rows):
        cos = jnp.concatenate([cos_ref[rows, :]] * N_HEADS, axis=1)
        sin = jnp.concatenate([sin_ref[rows, :]] * N_HEADS, axis=1)
        lane = lax.broadcasted_iota(jnp.int32, a.shape, 1)
        upper = (lane & (QK_DIM // 4)) != 0
        partner = jnp.where(upper, pltpu.roll(a, QK_DIM // 4, 1), pltpu.roll(a, ATT_WIDTH - QK_DIM // 4, 1))
        return a * cos + partner * sin

    def gated_conv(seq):
        for rows in chunks:
            gc, xin = proj(rows, 2)
            gc_s[rows, :] = gc * xin
        u = gc_s[...]
        pos = lax.broadcasted_iota(jnp.int32, (TM, 1), 0) & (seq - 1)
        prev = jnp.where(pos == 0, 0.0, pltpu.roll(u, 1, 0))
        nxt = jnp.where(pos == seq - 1, 0.0, pltpu.roll(u, TM - 1, 0))
        cw = cw_ref[layer]
        conv = prev * cw[0:1] + u * cw[1:2] + nxt * cw[2:3]
        conv_ref[...] = (gb_s[...] * conv).astype(BF)

    @pl.when(i >= CTX_TILES)
    def _():
        for rows in chunks:
            norm(xb_ref, rows)
            q, k = proj(rows, 0)
            q_ref[rows, :] = (roped(q, rows) * Q_SCALE).astype(BF)
            kt_ref[:, rows] = roped(k, rows).T.astype(BF)
        for rows in chunks:
            v, gb = proj(rows, 1)
            v_ref[rows, :] = v.astype(BF)
            gb_s[rows, :] = gb
        gated_conv(DEC_SEQ)

    @pl.when(i < CTX_TILES)
    def _():
        for c, rows in enumerate(chunks):
            norm(xa_ref, rows)
            q, k = proj(rows, 0)
            q_ref[rows, :] = (q * Q_SCALE).astype(BF)
            kt = k.T
            kt_ref[:, rows] = kt.astype(BF)
            for s in range(seqs_per_chunk):
                nk_ref[c * seqs_per_chunk + s] = kt[:, s * SEQ:(s + 1) * SEQ]
        for c, rows in enumerate(chunks):
            v, gb = proj(rows, 1)
            v_ref[rows, :] = v.astype(BF)
            gb_s[rows, :] = gb
            for s in range(seqs_per_chunk):
                for h in range(N_HEADS):
                    nv_ref[c * seqs_per_chunk + s, pl.ds(h, SEQ, stride=N_HEADS), :] = (
                        v[s * SEQ:(s + 1) * SEQ, h * V_DIM:(h + 1) * V_DIM])
        gated_conv(SEQ)


def _in_proj(layer, xs, mod, g_mix, w_in, cos_t, sin_t, conv_w, new_kv):
    pair = isinstance(xs, tuple)
    xa, xb = xs if pair else (xs, xs)
    spec_a, spec_b = _stream_specs(pair)
    ctx_i = lambda i: jnp.minimum(i, CTX_TILES - 1)
    in_specs = [
        spec_a, spec_b,
        WHOLE, WHOLE,
        pl.BlockSpec(memory_space=pl.ANY),
        WHOLE, WHOLE, WHOLE,
    ]
    args = [xa, xb, mod, g_mix, w_in, cos_t, sin_t, conv_w]
    aliases = {}
    if new_kv is not None:
        in_specs += [pl.BlockSpec(memory_space=pl.ANY), pl.BlockSpec(memory_space=pl.ANY)]
        args += list(new_kv)
        aliases = {8: 4, 9: 5}
    row_tile = pl.BlockSpec((TM, ATT_WIDTH), lambda i: (i, 0))
    return pl.pallas_call(
        functools.partial(_in_kernel, layer=layer, aliased=new_kv is not None),
        grid=(N_TILES,),
        in_specs=in_specs,
        out_specs=[
            row_tile,
            row_tile,
            pl.BlockSpec((None, ATT_WIDTH, TM), lambda i: (i, 0, 0)),
            row_tile,
            pl.BlockSpec((SEQ_PER_TILE, None, ATT_WIDTH, SEQ), lambda i: (ctx_i(i), layer, 0, 0)),
            pl.BlockSpec((SEQ_PER_TILE, None, SEQ * N_HEADS, V_DIM), lambda i: (ctx_i(i), layer, 0, 0)),
        ],
        out_shape=[
            jax.ShapeDtypeStruct((N_TOK, ATT_WIDTH), BF),
            jax.ShapeDtypeStruct((N_TOK, ATT_WIDTH), BF),
            jax.ShapeDtypeStruct((N_TILES, ATT_WIDTH, TM), BF),
            jax.ShapeDtypeStruct((N_TOK, CONV_WIDTH), BF),
            jax.ShapeDtypeStruct((BATCH, DEPTH, ATT_WIDTH, SEQ), F32),
            jax.ShapeDtypeStruct((BATCH, DEPTH, SEQ * N_HEADS, V_DIM), F32),
        ],
        scratch_shapes=[
            pltpu.VMEM((TM, D_MODEL), BF),
            pltpu.VMEM((N_IN_TILES, D_MODEL, TN_IN), BF),
            pltpu.VMEM((TM, CONV_WIDTH), F32),
            pltpu.VMEM((TM, CONV_WIDTH), F32),
            pltpu.VMEM((IN_STAGE_SLOTS, STAGE_W_ROWS, STAGE_W_COLS), F32),
            pltpu.SemaphoreType.DMA((IN_STAGE_SLOTS,)),
        ],
        input_output_aliases=aliases,
        compiler_params=_params(("arbitrary",)),
        name=f"in_proj_l{layer}",
    )(*args)


def _lambda(lq_ref, layer, lam_init):
    lq = lq_ref[layer]
    a = jnp.exp(jnp.sum(lq[0:1] * lq[1:2], axis=-1, keepdims=True))
    b = jnp.exp(jnp.sum(lq[2:3] * lq[3:4], axis=-1, keepdims=True))
    return a - b + lam_init


def _head_norm(o, sg, lam_init):
    return _rms(o) * sg * (1.0 - lam_init)


def _attn_ctx_kernel(q_ref, kt_ref, v_ref, lq_ref, sg_ref, o_ref, sc_s, *, layer, lam_init):
    lam = _lambda(lq_ref, layer, lam_init)
    sg = sg_ref[layer:layer + 1, :]

    def sequence(b, carry):
        pos = pl.ds(pl.multiple_of(b * SEQ, SEQ), SEQ)

        def scores(h):
            for s in range(2):
                d = slice(h * V_DIM + s * QK_DIM, h * V_DIM + (s + 1) * QK_DIM)
                sc_s[h % 2, s] = jnp.dot(q_ref[pos, d], kt_ref[d, pos], preferred_element_type=F32)

        def finish(h):
            cols = slice(h * V_DIM, (h + 1) * V_DIM)
            v = v_ref[pos, cols]
            outs = []
            for s in range(2):
                sc = sc_s[h % 2, s]
                e = jnp.exp2(sc - jnp.max(sc, axis=-1, keepdims=True))
                r = 1.0 / jnp.sum(e, axis=-1, keepdims=True)
                outs.append(jnp.dot(e.astype(BF), v, preferred_element_type=F32) * r)
            o = outs[0] - lam * outs[1]
            o_ref[pos, cols] = _head_norm(o, sg, lam_init).astype(BF)

        scores(0)
        for h in range(N_HEADS):
            if h + 1 < N_HEADS:
                scores(h + 1)
            finish(h)
        return carry

    lax.fori_loop(0, CTX_SEQ_PER_STEP, sequence, 0)


def _attn_ctx(layer, q, kt, v, lambda_qk, subln_g, lam_init):
    rows = CTX_SEQ_PER_STEP * SEQ
    per_tile = TM // rows
    return pl.pallas_call(
        functools.partial(_attn_ctx_kernel, layer=layer, lam_init=lam_init),
        grid=(N_CTX // rows,),
        in_specs=[
            pl.BlockSpec((rows, ATT_WIDTH), lambda b: (b, 0)),
            pl.BlockSpec((None, ATT_WIDTH, rows), lambda b: (b // per_tile, 0, b % per_tile)),
            pl.BlockSpec((rows, ATT_WIDTH), lambda b: (b, 0)),
            WHOLE, WHOLE,
        ],
        out_specs=pl.BlockSpec((rows, ATT_WIDTH), lambda b: (b, 0)),
        out_shape=jax.ShapeDtypeStruct((N_TOK, ATT_WIDTH), BF),
        scratch_shapes=[pltpu.VMEM((2, 2, SEQ, SEQ), F32)],
        compiler_params=_params(("arbitrary",)),
        name=f"attn_ctx_l{layer}",
    )(q, kt, v, lambda_qk, subln_g)


TQ = 256
LAT_Q_PER_STEP = 2


def _attn_lat_kernel(q_ref, kt_ref, v_ref, ckt_ref, cv_ref, lq_ref, sg_ref, att_in_ref, o_ref, sc_s, *,
                     layer, lam_init):
    del att_in_ref
    lam = _lambda(lq_ref, layer, lam_init)
    sg = sg_ref[layer:layer + 1, :]

    units = [(b, h) for b in range(LAT_Q_PER_STEP) for h in range(N_HEADS)]

    def scores(u, s):
        b, h = units[u]
        d = slice(h * V_DIM + s * QK_DIM, h * V_DIM + (s + 1) * QK_DIM)
        q = q_ref[b * TQ:(b + 1) * TQ, d]
        sc_s[u % 2, s, :, :PAST_LEN] = jnp.dot(q, ckt_ref[d, :].astype(BF), preferred_element_type=F32)
        sc_s[u % 2, s, :, PAST_LEN:] = jnp.dot(q, kt_ref[d, :], preferred_element_type=F32)

    def softmax(u, s):
        sc = sc_s[u % 2, s]
        e = jnp.exp2(sc - jnp.max(sc, axis=-1, keepdims=True))
        return e, 1.0 / jnp.sum(e, axis=-1, keepdims=True)

    def finish(u, p1, p2):
        b, h = units[u]
        cols = slice(h * V_DIM, (h + 1) * V_DIM)
        e = jnp.concatenate([p1[0].astype(BF), p2[0].astype(BF)], axis=0)
        vc = cv_ref[pl.ds(h, PAST_LEN, stride=N_HEADS), :].astype(BF)
        pv = jnp.dot(e[:, :PAST_LEN], vc, preferred_element_type=F32)
        pv = pv + jnp.dot(e[:, PAST_LEN:], v_ref[:, cols], preferred_element_type=F32)
        o = pv[:TQ] * p1[1] - pv[TQ:] * (lam * p2[1])
        o_ref[b * TQ:(b + 1) * TQ, cols] = _head_norm(o, sg, lam_init).astype(BF)

    scores(0, 0)
    scores(0, 1)
    for u in range(len(units)):
        more = u + 1 < len(units)
        if more:
            scores(u + 1, 0)
        p1 = softmax(u, 0)
        if more:
            scores(u + 1, 1)
        finish(u, p1, softmax(u, 1))


def _attn_lat(layer, q, kt, v, cache_kt, cache_v, lambda_qk, subln_g, att, lam_init):
    rows = LAT_Q_PER_STEP * TQ
    nqb = DEC_SEQ // rows
    q0 = N_CTX // rows
    return pl.pallas_call(
        functools.partial(_attn_lat_kernel, layer=layer, lam_init=lam_init),
        grid=(DEC_BATCH, nqb),
        in_specs=[
            pl.BlockSpec((rows, ATT_WIDTH), lambda b, t: (q0 + b * nqb + t, 0)),
            pl.BlockSpec((None, ATT_WIDTH, DEC_SEQ), lambda b, t: (CTX_TILES + b, 0, 0)),
            pl.BlockSpec((DEC_SEQ, ATT_WIDTH), lambda b, t: (CTX_TILES + b, 0)),
            pl.BlockSpec((None, None, ATT_WIDTH, PAST_LEN), lambda b, t: (b, layer, 0, 0)),
            pl.BlockSpec((None, None, PAST_LEN * N_HEADS, V_DIM), lambda b, t: (b, layer, 0, 0)),
            WHOLE, WHOLE,
            pl.BlockSpec(memory_space=pl.ANY),
        ],
        out_specs=pl.BlockSpec((rows, ATT_WIDTH), lambda b, t: (q0 + b * nqb + t, 0)),
        out_shape=jax.ShapeDtypeStruct((N_TOK, ATT_WIDTH), BF),
        scratch_shapes=[pltpu.VMEM((2, 2, TQ, PAST_LEN + DEC_SEQ), F32)],
        input_output_aliases={7: 0},
        compiler_params=_params(("arbitrary", "arbitrary")),
        name=f"attn_lat_l{layer}",
    )(q, kt, v, cache_kt, cache_v, lambda_qk, subln_g, att)


def _out_kernel(att_ref, conv_ref, w_ref, xa_ref, xb_ref, mod_ref, gf_ref, *rest, layer, route):
    if route:
        wr_ref, xo_ref, h2_ref, rt_ref, n_ref, wb_s = rest
    else:
        xo_ref, h2_ref, wb_s = rest
    i = pl.program_id(0)

    @pl.when(i == 0)
    def _():
        wb_s[...] = w_ref[...].astype(BF)

    lat = i >= CTX_TILES
    gain = gf_ref[layer:layer + 1, :] * (1.0 + _mod(mod_ref, layer, i, 4))
    for c in range(TM // ROW_CHUNK):
        rows = slice(c * ROW_CHUNK, (c + 1) * ROW_CHUNK)
        mo = jnp.dot(att_ref[rows, :], wb_s[:ATT_WIDTH, :], preferred_element_type=F32)
        mo = mo + jnp.dot(conv_ref[rows, :], wb_s[ATT_WIDTH:, :], preferred_element_type=F32)
        xn = jnp.where(lat, xb_ref[rows, :], xa_ref[rows, :]) + _mod(mod_ref, layer, i, 2) * mo
        xo_ref[rows, :] = xn
        h2_ref[rows, :] = (_rms(xn) * gain + _mod(mod_ref, layer, i, 3)).astype(BF)
    if route:
        _route(h2_ref[...], wr_ref, rt_ref, n_ref)


def _out_proj(layer, att, conv, w_out, xs, mod, g_ffn, w_router_pad=None):
    pair = isinstance(xs, tuple)
    xa, xb = xs if pair else (xs, xs)
    spec_a, spec_b = _stream_specs(pair)
    row_spec = pl.BlockSpec((TM, D_MODEL), lambda i: (i, 0))
    lane_spec = pl.BlockSpec((TM, LANES), lambda i: (i, 0))
    in_specs = [
        pl.BlockSpec((TM, ATT_WIDTH), lambda i: (i, 0)),
        pl.BlockSpec((TM, CONV_WIDTH), lambda i: (i, 0)),
        pl.BlockSpec((None, D_MODEL, D_MODEL), lambda i: (layer, 0, 0)),
        spec_a, spec_b,
        WHOLE, WHOLE,
    ]
    args = [att, conv, w_out, xa, xb, mod, g_ffn]
    out_specs = [row_spec, row_spec]
    out_shape = [jax.ShapeDtypeStruct((N_TOK, D_MODEL), F32), jax.ShapeDtypeStruct((N_TOK, D_MODEL), BF)]
    route = w_router_pad is not None
    if route:
        in_specs.append(WHOLE)
        args.append(w_router_pad)
        out_specs += [lane_spec, pl.BlockSpec((None, SUBLANES, LANES), lambda i: (i, 0, 0))]
        out_shape += [jax.ShapeDtypeStruct((N_TOK, LANES), F32),
                      jax.ShapeDtypeStruct((N_TILES, SUBLANES, LANES), jnp.int32)]
    return pl.pallas_call(
        functools.partial(_out_kernel, layer=layer, route=route),
        grid=(N_TILES,),
        in_specs=in_specs,
        out_specs=out_specs,
        out_shape=out_shape,
        scratch_shapes=[pltpu.VMEM((D_MODEL, D_MODEL), BF)],
        compiler_params=_params(("arbitrary",)),
        name=f"out_proj_l{layer}",
    )(*args)


def _weight_pieces(layer):
    pieces = []
    for c0 in range(0, D_MODEL, STAGE_W_COLS):
        cols = slice(c0, c0 + STAGE_W_COLS)
        pieces.append((0, layer, slice(0, D_MODEL), cols, None, slice(0, D_MODEL), cols))
    for half in range(FF_HALVES):
        for part in range(2):
            src0 = part * D_FF + half * FF_HALF
            for off in range(0, FF_HALF, STAGE_W_COLS):
                n = min(STAGE_W_COLS, FF_HALF - off)
                pieces.append((1, 0, slice(0, D_MODEL), slice(src0 + off, src0 + off + n),
                               half, slice(0, D_MODEL), slice(part * FF_HALF + off, part * FF_HALF + off + n)))
    for r0 in range(0, D_FF, STAGE_W_ROWS):
        rows = slice(r0, min(r0 + STAGE_W_ROWS, D_FF))
        for c0 in range(0, D_MODEL, STAGE_W_COLS):
            cols = slice(c0, c0 + STAGE_W_COLS)
            pieces.append((2, 0, rows, cols, None, rows, cols))
    return pieces


def _load_weights(pieces, hbm, resident, stage_s, sem):
    slots = stage_s.shape[0]

    def copy(k):
        src, idx, rows, cols, _, _, _ = pieces[k]
        nr, nc = rows.stop - rows.start, cols.stop - cols.start
        slot = k % slots
        return pltpu.make_async_copy(hbm[src].at[idx, rows, cols], stage_s.at[slot, :nr, :nc], sem.at[slot])

    for k in range(min(slots - 1, len(pieces))):
        copy(k).start()
    for k, (src, _, rows, cols, didx, drows, dcols) in enumerate(pieces):
        if k + slots - 1 < len(pieces):
            copy(k + slots - 1).start()
        copy(k).wait()
        nr, nc = rows.stop - rows.start, cols.stop - cols.start
        piece = stage_s[k % slots, :nr, :nc].astype(BF)
        if didx is None:
            resident[src][drows, dcols] = piece
        else:
            resident[src][didx, drows, dcols] = piece


def _dense_layer_kernel(att_ref, conv_ref, xa_ref, xb_ref, mod_ref, gf_ref, wo_hbm, wgu_hbm, wd_hbm,
                        o_ref, wo_b, wgu_b, wd_b, stage_s, sem, *, layer):
    i = pl.program_id(0)
    tile = i // (TM // FF_TM)

    @pl.when(i == 0)
    def _():
        _load_weights(_weight_pieces(layer), (wo_hbm, wgu_hbm, wd_hbm), (wo_b, wgu_b, wd_b), stage_s, sem)

    mo = jnp.dot(att_ref[...], wo_b[:ATT_WIDTH, :], preferred_element_type=F32)
    mo = mo + jnp.dot(conv_ref[...], wo_b[ATT_WIDTH:, :], preferred_element_type=F32)
    x = jnp.where(tile >= CTX_TILES, xb_ref[...], xa_ref[...])
    xn = x + _mod(mod_ref, layer, tile, 2) * mo
    h = (_rms(xn) * gf_ref[layer:layer + 1, :]) * (1.0 + _mod(mod_ref, layer, tile, 4)) + _mod(mod_ref, layer, tile, 3)
    h = h.astype(BF)
    y = None
    for half in range(FF_HALVES):
        gu = jnp.dot(h, wgu_b[half], preferred_element_type=F32)
        act = (_silu(gu[:, :FF_HALF]) * gu[:, FF_HALF:]).astype(BF)
        part = jnp.dot(act, wd_b[half * FF_HALF:(half + 1) * FF_HALF, :], preferred_element_type=F32)
        y = part if y is None else y + part
    o_ref[...] = xn + _mod(mod_ref, layer, tile, 5) * y


def _dense_layer(layer, att, conv, xs, mod, g_ffn, w_out, w_gu, w_down):
    xa, xb = xs
    n_ctx = N_CTX // FF_TM
    row = lambda width: pl.BlockSpec((FF_TM, width), lambda i: (i, 0))
    hbm = pl.BlockSpec(memory_space=pl.ANY)
    return pl.pallas_call(
        functools.partial(_dense_layer_kernel, layer=layer),
        grid=(N_TOK // FF_TM,),
        in_specs=[
            row(ATT_WIDTH), row(CONV_WIDTH),
            pl.BlockSpec((FF_TM, D_MODEL), lambda i: (jnp.minimum(i, n_ctx - 1), 0)),
            pl.BlockSpec((FF_TM, D_MODEL), lambda i: (jnp.maximum(i - n_ctx, 0), 0)),
            WHOLE, WHOLE, hbm, hbm, hbm,
        ],
        out_specs=row(D_MODEL),
        out_shape=jax.ShapeDtypeStruct((N_TOK, D_MODEL), F32),
        scratch_shapes=[
            pltpu.VMEM((D_MODEL, D_MODEL), BF),
            pltpu.VMEM((FF_HALVES, D_MODEL, 2 * FF_HALF), BF),
            pltpu.VMEM((D_FF, D_MODEL), BF),
            pltpu.VMEM((FF_STAGE_SLOTS, STAGE_W_ROWS, STAGE_W_COLS), F32),
            pltpu.SemaphoreType.DMA((FF_STAGE_SLOTS,)),
        ],
        compiler_params=_params(("arbitrary",)),
        name="dense_layer",
    )(att, conv, xa, xb, mod, g_ffn, w_out, w_gu, w_down)


def _route(h, wr_ref, rt_ref, n_ref):
    logits = jnp.dot(h, wr_ref[...].astype(BF), preferred_element_type=F32)
    lane = lax.broadcasted_iota(jnp.int32, logits.shape, 1)
    lg = jnp.where(lane < N_EXPERTS, logits, -jnp.inf)
    m1 = jnp.max(lg, axis=-1, keepdims=True)
    i1 = jnp.min(jnp.where(lg == m1, lane, LANES), axis=-1, keepdims=True)
    lg2 = jnp.where(lane == i1, -jnp.inf, lg)
    m2 = jnp.max(lg2, axis=-1, keepdims=True)
    i2 = jnp.min(jnp.where(lg2 == m2, lane, LANES), axis=-1, keepdims=True)
    e2 = jnp.exp(m2 - m1)
    w1 = 1.0 / (1.0 + e2)
    w2 = e2 / (1.0 + e2)

    sel1 = lane == i1
    sel2 = lane == i2
    onehot = jnp.logical_or(sel1, sel2)
    rows = lax.broadcasted_iota(jnp.int32, (ST, ST), 0)
    colsi = lax.broadcasted_iota(jnp.int32, (ST, ST), 1)
    earlier = (colsi < rows).astype(BF)
    onehot_b = onehot.astype(BF)
    before = jnp.concatenate(
        [jnp.dot(earlier, onehot_b[s * ST:(s + 1) * ST], preferred_element_type=F32) for s in range(SUB_PER_TILE)],
        axis=0)
    onehot_f = onehot.astype(F32)
    counts = [jnp.sum(onehot_f[s * ST:(s + 1) * ST], axis=0, keepdims=True) for s in range(SUB_PER_TILE)]
    counts = jnp.concatenate(counts + [jnp.zeros((SUBLANES - SUB_PER_TILE, LANES), F32)], axis=0).astype(jnp.int32)
    seg_len = ((counts + (SEG_ALIGN - 1)) // SEG_ALIGN) * SEG_ALIGN
    n_ref[...] = seg_len
    la = lax.broadcasted_iota(jnp.int32, (LANES, LANES), 0)
    lb = lax.broadcasted_iota(jnp.int32, (LANES, LANES), 1)
    seg_start = jnp.dot(seg_len.astype(F32).astype(BF), (la < lb).astype(BF), preferred_element_type=F32)
    start = jnp.concatenate(
        [jnp.broadcast_to(seg_start[s:s + 1], (ST, LANES)) for s in range(SUB_PER_TILE)], axis=0)
    where = before + start
    lp1 = jnp.sum(jnp.where(sel1, where, 0.0), axis=-1, keepdims=True)
    lp2 = jnp.sum(jnp.where(sel2, where, 0.0), axis=-1, keepdims=True)
    rt = jnp.where(lane == GATE_LANE, w1, jnp.where(lane == GATE_LANE + 1, w2, 0.0))
    rt_ref[...] = jnp.where(lane == ROW_LANE, lp1, jnp.where(lane == ROW_LANE + 1, lp2, rt))


def _chunk_copies(s, cnt_ref, cdst_ref, stage, rows_hbm, sem, *, to_hbm, wait):
    def copy(v, h):
        return pltpu.make_async_copy(v, h, sem) if to_hbm else pltpu.make_async_copy(h, v, sem)

    if wait:
        for z in WAIT_PIECES:
            @pl.when((cnt_ref[s] & z) != 0)
            def _():
                copy(stage.at[pl.ds(0, z * SEG_ALIGN)], rows_hbm.at[pl.ds(0, z * SEG_ALIGN)]).wait()
        return

    def start(c, priority):
        v = stage.at[pl.ds(pl.multiple_of(c * SEG_ALIGN, SEG_ALIGN), SEG_ALIGN)]
        h = rows_hbm.at[pl.ds(pl.multiple_of(cdst_ref[s * STAGE_CHUNKS + c], SEG_ALIGN), SEG_ALIGN)]
        copy(v, h).start(priority=priority)

    def pair(p, carry):
        start(2 * p, 0)
        start(2 * p + 1, 1)
        return carry

    n = cnt_ref[s]
    lax.fori_loop(0, n // 2, pair, 0)

    @pl.when((n & 1) != 0)
    def _():
        start(n - 1, 0)


def _dispatch_kernel(cnt_ref, cdst_ref, h_ref, rt_ref, xs_ref, stage_s, sem):
    copies = functools.partial(_chunk_copies, cnt_ref=cnt_ref, cdst_ref=cdst_ref, rows_hbm=xs_ref, to_hbm=True)
    for k in range(SUB_PER_TILE):
        s = pl.program_id(0) * SUB_PER_TILE + k
        slot = k % 2
        rows = slice(k * ST, (k + 1) * ST)

        @pl.when(s >= 2)
        def _():
            copies(s - 2, stage=stage_s.at[slot], sem=sem.at[slot], wait=True)

        at = rt_ref[rows, :].T[ROW_LANE:ROW_LANE + 2, :].astype(jnp.int32)
        r = lax.broadcasted_iota(jnp.int32, (STAGE_ROWS, ST), 0)
        pick = jnp.logical_or(r == at[0:1, :], r == at[1:2, :]).astype(BF)
        stage_s[slot] = jnp.dot(pick, h_ref[rows, :], preferred_element_type=F32).astype(BF)
        copies(s, stage=stage_s.at[slot], sem=sem.at[slot], wait=False)

    @pl.when(pl.program_id(0) == N_TILES - 1)
    def _():
        copies(N_SUB - 2, stage=stage_s.at[0], sem=sem.at[0], wait=True)
        copies(N_SUB - 1, stage=stage_s.at[1], sem=sem.at[1], wait=True)


def _dispatch(cnt, cdst, h2, rt):
    assert SUB_PER_TILE % 2 == 0
    grid_spec = pltpu.PrefetchScalarGridSpec(
        num_scalar_prefetch=2,
        grid=(N_TILES,),
        in_specs=[
            pl.BlockSpec((TM, D_MODEL), lambda i, *_: (i, 0)),
            pl.BlockSpec((TM, LANES), lambda i, *_: (i, 0)),
        ],
        out_specs=pl.BlockSpec(memory_space=pl.ANY),
        scratch_shapes=[pltpu.VMEM((2, STAGE_ROWS, D_MODEL), BF), pltpu.SemaphoreType.DMA((2,))],
    )
    return pl.pallas_call(
        _dispatch_kernel,
        grid_spec=grid_spec,
        out_shape=jax.ShapeDtypeStruct((R_PAD, D_MODEL), BF),
        compiler_params=_params(("arbitrary",)),
        name="moe_dispatch",
    )(cnt, cdst, h2, rt)


def _expert_weights(r, te_ref, nt_ref, nxt_ref, w_hbm, wf_s, wb_s, sem):
    def fetch(e):
        return pltpu.make_async_copy(w_hbm.at[0, e], wf_s, sem)

    @pl.when(r == 0)
    def _():
        fetch(te_ref[0]).start()

    first = jnp.logical_or(r == 0, te_ref[r] != te_ref[jnp.maximum(r - 1, 0)])

    @pl.when(jnp.logical_and(r < nt_ref[0], first))
    def _():
        fetch(te_ref[r]).wait()
        wb_s[...] = wf_s[...].astype(BF)

        @pl.when(nxt_ref[r] >= 0)
        def _():
            fetch(nxt_ref[r]).start()


def _moe_ffn_kernel(te_ref, nt_ref, nxt_ref, used_ref, x_ref, wgu_hbm, wd_hbm, o_ref,
                    wgu_f, wgu_b, wd_f, wd_b, sem):
    def ffn(rows):
        gu = jnp.dot(x_ref[rows, :], wgu_b[...], preferred_element_type=F32)
        act = (_silu(gu[:, :D_FF_EXPERT]) * gu[:, D_FF_EXPERT:]).astype(BF)
        o_ref[rows, :] = jnp.dot(act, wd_b[...], preferred_element_type=F32).astype(BF)

    for k in range(TILES_PER_STEP):
        r = pl.program_id(0) * TILES_PER_STEP + k
        _expert_weights(r, te_ref, nt_ref, nxt_ref, wgu_hbm, wgu_f, wgu_b, sem.at[0])
        _expert_weights(r, te_ref, nt_ref, nxt_ref, wd_hbm, wd_f, wd_b, sem.at[1])

        @pl.when(jnp.logical_and(r < nt_ref[0], used_ref[r] > TG // 2))
        def _():
            ffn(slice(k * TG, (k + 1) * TG))

        @pl.when(jnp.logical_and(r < nt_ref[0], used_ref[r] <= TG // 2))
        def _():
            ffn(slice(k * TG, k * TG + TG // 2))


def _moe_ffn(te, nt, nxt, used, rows, w_gu, w_down):
    tile_map = lambda i, te, nt, nxt, used: (jnp.minimum(i, (nt[0] - 1) // TILES_PER_STEP), 0)
    grid_spec = pltpu.PrefetchScalarGridSpec(
        num_scalar_prefetch=4,
        grid=(NT_G // TILES_PER_STEP,),
        in_specs=[pl.BlockSpec((TILES_PER_STEP * TG, D_MODEL), tile_map),
                  pl.BlockSpec(memory_space=pl.ANY), pl.BlockSpec(memory_space=pl.ANY)],
        out_specs=pl.BlockSpec((TILES_PER_STEP * TG, D_MODEL), tile_map),
        scratch_shapes=[
            pltpu.VMEM((D_MODEL, 2 * D_FF_EXPERT), F32), pltpu.VMEM((D_MODEL, 2 * D_FF_EXPERT), BF),
            pltpu.VMEM((D_FF_EXPERT, D_MODEL), F32), pltpu.VMEM((D_FF_EXPERT, D_MODEL), BF),
            pltpu.SemaphoreType.DMA((2,)),
        ],
    )
    return pl.pallas_call(
        _moe_ffn_kernel,
        grid_spec=grid_spec,
        out_shape=jax.ShapeDtypeStruct((R_PAD, D_MODEL), BF),
        compiler_params=_params(("arbitrary",)),
        name="moe_ffn",
    )(te, nt, nxt, used, rows, w_gu, w_down)


def _combine_kernel(cnt_ref, cdst_ref, ys_ref, rt_ref, x_ref, mod_ref, fg_ref,
                    oa_ref, ob_ref, stage_s, sem, *, layer):
    i = pl.program_id(0)
    copies = functools.partial(_chunk_copies, cnt_ref=cnt_ref, cdst_ref=cdst_ref, rows_hbm=ys_ref, to_hbm=False)

    @pl.when(i == 0)
    def _():
        stage_s[...] = jnp.zeros_like(stage_s)
        for s in range(COMBINE_AHEAD):
            copies(s, stage=stage_s.at[s], sem=sem.at[s], wait=False)

    for k in range(SUB_PER_TILE):
        s = i * SUB_PER_TILE + k
        slot = k % COMBINE_SLOTS
        ahead = (k + COMBINE_AHEAD) % COMBINE_SLOTS
        rows = slice(k * ST, (k + 1) * ST)

        @pl.when(s + COMBINE_AHEAD < N_SUB)
        def _():
            copies(s + COMBINE_AHEAD, stage=stage_s.at[ahead], sem=sem.at[ahead], wait=False)

        copies(s, stage=stage_s.at[slot], sem=sem.at[slot], wait=True)

        rt = rt_ref[rows, :]
        at = rt[:, ROW_LANE:ROW_LANE + 2].astype(jnp.int32)
        r = lax.broadcasted_iota(jnp.int32, (ST, STAGE_ROWS), 1)
        staged = stage_s[slot]
        a = jnp.dot((r == at[:, 0:1]).astype(BF), staged, preferred_element_type=F32)
        b = jnp.dot((r == at[:, 1:2]).astype(BF), staged, preferred_element_type=F32)
        y = rt[:, GATE_LANE:GATE_LANE + 1] * a + rt[:, GATE_LANE + 1:GATE_LANE + 2] * b
        xn = x_ref[rows, :] + _mod(mod_ref, layer, i, 5) * y
        out = _rms(xn) * fg_ref[...]

        @pl.when(i < CTX_TILES)
        def _():
            oa_ref[rows, :] = out

        @pl.when(i >= CTX_TILES)
        def _():
            ob_ref[rows, :] = out


def _combine(layer, cnt, cdst, ys, rt, x, mod, final_g):
    assert SUB_PER_TILE % COMBINE_SLOTS == 0
    grid_spec = pltpu.PrefetchScalarGridSpec(
        num_scalar_prefetch=2,
        grid=(N_TILES,),
        in_specs=[
            pl.BlockSpec(memory_space=pl.ANY),
            pl.BlockSpec((TM, LANES), lambda i, *_: (i, 0)),
            pl.BlockSpec((TM, D_MODEL), lambda i, *_: (i, 0)),
            WHOLE, WHOLE,
        ],
        out_specs=[
            pl.BlockSpec((TM, D_MODEL), lambda i, *_: (jnp.minimum(i, CTX_TILES - 1), 0)),
            pl.BlockSpec((TM, D_MODEL), lambda i, *_: (jnp.maximum(i - CTX_TILES, 0), 0)),
        ],
        scratch_shapes=[pltpu.VMEM((COMBINE_SLOTS, STAGE_ROWS, D_MODEL), BF),
                        pltpu.SemaphoreType.DMA((COMBINE_SLOTS,))],
    )
    return pl.pallas_call(
        functools.partial(_combine_kernel, layer=layer),
        grid_spec=grid_spec,
        out_shape=[
            jax.ShapeDtypeStruct((N_CTX, D_MODEL), F32),
            jax.ShapeDtypeStruct((N_LAT, D_MODEL), F32),
        ],
        compiler_params=_params(("arbitrary",)),
        name="moe_combine",
    )(cnt, cdst, ys, rt, x, mod, final_g)


def _group_layout(n_tiles):
    n = n_tiles[:, :SUB_PER_TILE, :N_EXPERTS].reshape(N_SUB, N_EXPERTS)
    tiles = (jnp.sum(n, axis=0) + TG - 1) // TG
    tile_end = jnp.cumsum(tiles)
    region = (tile_end - tiles) * TG
    dst = region[None, :] + jnp.cumsum(n, axis=0) - n
    seg_end = jnp.cumsum(n, axis=1)
    seg = seg_end - n
    row = jnp.arange(STAGE_CHUNKS, dtype=jnp.int32) * SEG_ALIGN
    owner = jnp.sum((row[None, :, None] >= seg_end[:, None, :]).astype(jnp.int32), axis=-1)
    own = jnp.minimum(owner, N_EXPERTS - 1)[..., None] == jnp.arange(N_EXPERTS)
    cdst = jnp.sum(jnp.where(own, (dst - seg)[:, None, :], 0), axis=-1) + row[None, :]
    cnt = seg_end[:, -1] // SEG_ALIGN
    nt = tile_end[-1]
    tile_id = jnp.minimum(jnp.arange(NT_G, dtype=jnp.int32), nt - 1)
    te = jnp.sum((tile_id[:, None] >= tile_end[None, :]).astype(jnp.int32), axis=-1)
    after = jnp.sum(jnp.where(te[:, None] == jnp.arange(N_EXPERTS), tile_end[None, :], 0), axis=-1)
    nxt = jnp.where(after < nt, jnp.sum((after[:, None] >= tile_end[None, :]).astype(jnp.int32), axis=-1), -1)
    mine = te[:, None] == jnp.arange(N_EXPERTS)
    region_end = jnp.sum(jnp.where(mine, (region + jnp.sum(n, axis=0))[None, :], 0), axis=-1)
    used = jnp.clip(region_end - tile_id * TG, 0, TG)
    i32 = lambda a: a.astype(jnp.int32)
    return (i32(cnt), i32(cdst.reshape(N_SUB * STAGE_CHUNKS)), i32(te), i32(nt.reshape(1)), i32(nxt), i32(used))


def _rope_tables():
    p = np.arange(DEC_SEQ)
    row = (p // GRID_W).astype(np.float32)
    col = (p % GRID_W).astype(np.float32)
    half = QK_DIM // 4
    freqs = (ROPE_BASE ** (-np.arange(half, dtype=np.float32) / half)).astype(np.float32)
    lane = np.arange(V_DIM)
    f = freqs[lane & (half - 1)]
    use_col = (lane & (2 * half)) != 0
    ang = (np.where(use_col[None, :], col[:, None], row[:, None]) * f[None, :]).astype(np.float32)
    upper = (lane & half) != 0
    sin = np.sin(ang)
    return jnp.asarray(np.cos(ang), F32), jnp.asarray(np.where(upper[None, :], sin, -sin), F32)


def kernel(x_prompt, x_sample, cache_k, cache_v, c, c_ctx, w_ada, b_ada, norm_mix_g, norm_ffn_g,
           w_in, lambda_qk, subln_g, conv_w, w_out, w_gu_dense, w_down_dense, w_router,
           w_gu_moe, w_down_moe, final_g):
    assert DEPTH == 2
    xs = (x_prompt.reshape(N_CTX, D_MODEL), x_sample.reshape(N_LAT, D_MODEL))
    mod = _ada(c_ctx, c, w_ada, b_ada)
    cos_t, sin_t = _rope_tables()
    cache_kt = jnp.transpose(cache_k, (0, 1, 3, 4, 5, 2)).reshape(DEC_BATCH, DEPTH, ATT_WIDTH, PAST_LEN)
    cache_v4 = cache_v.reshape(DEC_BATCH, DEPTH, PAST_LEN * N_HEADS, V_DIM)

    new_kv = None
    for layer in range(DEPTH):
        lam_init = 0.8 - 0.6 * math.exp(-0.3 * layer)
        q, v, kt, conv, nk, nv = _in_proj(layer, xs, mod, norm_mix_g, w_in, cos_t, sin_t, conv_w, new_kv)
        new_kv = (nk, nv)
        att = _attn_ctx(layer, q, kt, v, lambda_qk, subln_g, lam_init)
        att = _attn_lat(layer, q, kt, v, cache_kt, cache_v4, lambda_qk, subln_g, att, lam_init)
        if layer == 0:
            xs = _dense_layer(layer, att, conv, xs, mod, norm_ffn_g, w_out, w_gu_dense, w_down_dense)
        else:
            wr = jnp.pad(w_router[0], ((0, 0), (0, LANES - N_EXPERTS)))
            x1, h2, rt, n_tiles = _out_proj(layer, att, conv, w_out, xs, mod, norm_ffn_g, wr)
            cnt, cdst, te, nt, nxt, used = _group_layout(n_tiles)
            xsort = _dispatch(cnt, cdst, h2, rt)
            ys = _moe_ffn(te, nt, nxt, used, xsort, w_gu_moe, w_down_moe)
            y_ctx, y_lat = _combine(layer, cnt, cdst, ys, rt, x1, mod, final_g.reshape(1, D_MODEL))
    nk, nv = new_kv
    new_k = jnp.transpose(nk.reshape(BATCH, DEPTH, N_HEADS, 2, QK_DIM, SEQ), (0, 1, 5, 2, 3, 4))
    new_v = nv.reshape(BATCH, DEPTH, SEQ, N_HEADS, V_DIM)
    return (y_ctx.reshape(BATCH, SEQ, D_MODEL), y_lat.reshape(DEC_BATCH, DEC_SEQ, D_MODEL), new_k, new_v)
```

```python
import functools
import math

import numpy as np
import jax
import jax.numpy as jnp
from jax import lax
from jax.experimental import pallas as pl
from jax.experimental.pallas import tpu as pltpu

D_MODEL = 1024
BATCH = 16
SEQ = 256
DEPTH = 2
DEC_BATCH = 4
DEC_SEQ = 1024
PAST_LEN = 512
GRID_W = 64
ATT_WIDTH = 512
CONV_WIDTH = 512
N_HEADS = 4
V_DIM = 128
QK_DIM = 64
ROPE_BASE = 10000.0
D_FF = 2816
N_EXPERTS = 8
D_FF_EXPERT = 1408
N_MOD = 6
NORM_EPS = 1e-6
Q_SCALE = QK_DIM ** -0.5 * math.log2(math.e)
IN_COLS = 3 * ATT_WIDTH + 3 * CONV_WIDTH

N_CTX = BATCH * SEQ
N_LAT = DEC_BATCH * DEC_SEQ
N_TOK = N_CTX + N_LAT
TM = 1024
N_TILES = N_TOK // TM
CTX_TILES = N_CTX // TM
SEQ_PER_TILE = TM // SEQ
CTX_SEQ_PER_STEP = 4
COND_ROWS = 8
TN_IN = 1024
N_IN_TILES = IN_COLS // TN_IN
ROW_CHUNK = 512
FF_TM = 512
FF_HALVES = 2
FF_HALF = D_FF // FF_HALVES
STAGE_W_ROWS = 1024
STAGE_W_COLS = 512
FF_STAGE_SLOTS = 8
IN_STAGE_SLOTS = 3
TN_ADA = 1536
TG = 512
ST = 256
SUB_PER_TILE = TM // ST
N_SUB = N_TOK // ST
SEG_ALIGN = 8
STAGE_ROWS = 640
STAGE_CHUNKS = STAGE_ROWS // SEG_ALIGN
WAIT_PIECES = (64, 32, 16, 8, 4, 2, 1)
DISPATCH_SLOTS = 4
COMBINE_SLOTS = 8
COMBINE_AHEAD = 6
TILES_PER_STEP = 2
_MAX_SORTED_ROWS = 2 * N_TOK + N_SUB * N_EXPERTS * (SEG_ALIGN - 1) + N_EXPERTS * (TG - SEG_ALIGN)
NT_G = -(-_MAX_SORTED_ROWS // (TG * TILES_PER_STEP)) * TILES_PER_STEP
R_PAD = NT_G * TG
LANES = 128
SUBLANES = 8
GATE_LANE = 0
ROW_LANE = 2
VMEM_LIMIT = 60 * 1024 * 1024

BF = jnp.bfloat16
F32 = jnp.float32


def _params(sem, vmem=VMEM_LIMIT):
    return pltpu.CompilerParams(dimension_semantics=sem, vmem_limit_bytes=vmem)


def _mod_row(i):
    return jnp.where(i < CTX_TILES, 0, i - (CTX_TILES - 1))


WHOLE = pl.BlockSpec(memory_space=pltpu.VMEM)


def _mod(mod_ref, layer, i, c):
    return mod_ref[layer, pl.ds(_mod_row(i), 1), c * D_MODEL:(c + 1) * D_MODEL]


def _stream_specs(pair, width=D_MODEL):
    a = pl.BlockSpec((TM, width), lambda i, *_: (jnp.minimum(i, CTX_TILES - 1), 0))
    if pair:
        b = pl.BlockSpec((TM, width), lambda i, *_: (jnp.maximum(i - CTX_TILES, 0), 0))
    else:
        b = pl.BlockSpec((TM, width), lambda i, *_: (jnp.maximum(i, CTX_TILES), 0))
    return a, b


def _silu(x):
    return x / (1.0 + jnp.exp(-x))


def _rms(x):
    return x * lax.rsqrt(jnp.mean(x * x, axis=-1, keepdims=True) + NORM_EPS)


def _ada_kernel(cc_ref, c_ref, w_ref, b_ref, o_ref):
    pad = jnp.zeros((COND_ROWS - 1 - DEC_BATCH, D_MODEL), F32)
    cond = jnp.concatenate([cc_ref[...], c_ref[...], pad], axis=0)
    s = _silu(cond).astype(BF)
    bias = b_ref[pl.ds(pl.program_id(0), 1), :]
    o_ref[...] = jnp.dot(s, w_ref[...].astype(BF), preferred_element_type=F32) + bias


def _ada(c_ctx, c, w_ada, b_ada):
    n = N_MOD * D_MODEL
    return pl.pallas_call(
        _ada_kernel,
        grid=(DEPTH, n // TN_ADA),
        in_specs=[
            WHOLE, WHOLE,
            pl.BlockSpec((None, D_MODEL, TN_ADA), lambda l, j: (l, 0, j)),
            pl.BlockSpec((DEPTH, TN_ADA), lambda l, j: (0, j)),
        ],
        out_specs=pl.BlockSpec((None, COND_ROWS, TN_ADA), lambda l, j: (l, 0, j)),
        out_shape=jax.ShapeDtypeStruct((DEPTH, COND_ROWS, n), F32),
        compiler_params=_params(("arbitrary", "arbitrary")),
        name="ada_mod",
    )(c_ctx.reshape(1, D_MODEL), c, w_ada, b_ada)


def _in_weight_pieces(layer):
    pieces = []
    for c0 in range(0, IN_COLS, STAGE_W_COLS):
        g, off = divmod(c0, TN_IN)
        pieces.append((0, layer, slice(0, D_MODEL), slice(c0, c0 + STAGE_W_COLS),
                       g, slice(0, D_MODEL), slice(off, off + STAGE_W_COLS)))
    return pieces


def _in_kernel(*refs, layer, aliased):
    xa_ref, xb_ref, mod_ref, g_ref, w_hbm, cos_ref, sin_ref, cw_ref = refs[:8]
    refs = refs[10:] if aliased else refs[8:]
    q_ref, v_ref, kt_ref, conv_ref, nk_ref, nv_ref, h_s, wb_s, gb_s, gc_s, stage_s, sem = refs
    i = pl.program_id(0)

    @pl.when(i == 0)
    def _():
        _load_weights(_in_weight_pieces(layer), (w_hbm,), (wb_s,), stage_s, sem)

    chunks = [slice(c * ROW_CHUNK, (c + 1) * ROW_CHUNK) for c in range(TM // ROW_CHUNK)]
    seqs_per_chunk = ROW_CHUNK // SEQ

    def norm(x_ref, rows):
        gain = g_ref[layer:layer + 1, :] * (1.0 + _mod(mod_ref, layer, i, 1))
        h_s[rows, :] = (_rms(x_ref[rows, :]) * gain + _mod(mod_ref, layer, i, 0)).astype(BF)

    def proj(rows, group):
        acc = jnp.dot(h_s[rows, :], wb_s[group], preferred_element_type=F32)
        return acc[:, :ATT_WIDTH], acc[:, ATT_WIDTH:]

    def roped(a, rows):
        cos = jnp.concatenate([cos_ref[rows, :]] * N_HEADS, axis=1)
        sin = jnp.concatenate([sin_ref[rows, :]] * N_HEADS, axis=1)
        lane = lax.broadcasted_iota(jnp.int32, a.shape, 1)
        upper = (lane & (QK_DIM // 4)) != 0
        partner = jnp.where(upper, pltpu.roll(a, QK_DIM // 4, 1), pltpu.roll(a, ATT_WIDTH - QK_DIM // 4, 1))
        return a * cos + partner * sin

    def gated_conv(seq):
        for rows in chunks:
            gc, xin = proj(rows, 2)
            gc_s[rows, :] = gc * xin
        u = gc_s[...]
        pos = lax.broadcasted_iota(jnp.int32, (TM, 1), 0) & (seq - 1)
        prev = jnp.where(pos == 0, 0.0, pltpu.roll(u, 1, 0))
        nxt = jnp.where(pos == seq - 1, 0.0, pltpu.roll(u, TM - 1, 0))
        cw = cw_ref[layer]
        conv = prev * cw[0:1] + u * cw[1:2] + nxt * cw[2:3]
        conv_ref[...] = (gb_s[...] * conv).astype(BF)

    @pl.when(i >= CTX_TILES)
    def _():
        for rows in chunks:
            norm(xb_ref, rows)
            q, k = proj(rows, 0)
            q_ref[rows, :] = (roped(q, rows) * Q_SCALE).astype(BF)
            kt_ref[:, rows] = roped(k, rows).T.astype(BF)
        for rows in chunks:
            v, gb = proj(rows, 1)
            v_ref[rows, :] = v.astype(BF)
            gb_s[rows, :] = gb
        gated_conv(DEC_SEQ)

    @pl.when(i < CTX_TILES)
    def _():
        for c, rows in enumerate(chunks):
            norm(xa_ref, rows)
            q, k = proj(rows, 0)
            q_ref[rows, :] = (q * Q_SCALE).astype(BF)
            kt = k.T
            kt_ref[:, rows] = kt.astype(BF)
            for s in range(seqs_per_chunk):
                nk_ref[c * seqs_per_chunk + s] = kt[:, s * SEQ:(s + 1) * SEQ]
        for c, rows in enumerate(chunks):
            v, gb = proj(rows, 1)
            v_ref[rows, :] = v.astype(BF)
            gb_s[rows, :] = gb
            for s in range(seqs_per_chunk):
                for h in range(N_HEADS):
                    nv_ref[c * seqs_per_chunk + s, pl.ds(h, SEQ, stride=N_HEADS), :] = (
                        v[s * SEQ:(s + 1) * SEQ, h * V_DIM:(h + 1) * V_DIM])
        gated_conv(SEQ)


def _in_proj(layer, xs, mod, g_mix, w_in, cos_t, sin_t, conv_w, new_kv):
    pair = isinstance(xs, tuple)
    xa, xb = xs if pair else (xs, xs)
    spec_a, spec_b = _stream_specs(pair)
    ctx_i = lambda i: jnp.minimum(i, CTX_TILES - 1)
    in_specs = [
        spec_a, spec_b,
        WHOLE, WHOLE,
        pl.BlockSpec(memory_space=pl.ANY),
        WHOLE, WHOLE, WHOLE,
    ]
    args = [xa, xb, mod, g_mix, w_in, cos_t, sin_t, conv_w]
    aliases = {}
    if new_kv is not None:
        in_specs += [pl.BlockSpec(memory_space=pl.ANY), pl.BlockSpec(memory_space=pl.ANY)]
        args += list(new_kv)
        aliases = {8: 4, 9: 5}
    row_tile = pl.BlockSpec((TM, ATT_WIDTH), lambda i: (i, 0))
    return pl.pallas_call(
        functools.partial(_in_kernel, layer=layer, aliased=new_kv is not None),
        grid=(N_TILES,),
        in_specs=in_specs,
        out_specs=[
            row_tile,
            row_tile,
            pl.BlockSpec((None, ATT_WIDTH, TM), lambda i: (i, 0, 0)),
            row_tile,
            pl.BlockSpec((SEQ_PER_TILE, None, ATT_WIDTH, SEQ), lambda i: (ctx_i(i), layer, 0, 0)),
            pl.BlockSpec((SEQ_PER_TILE, None, SEQ * N_HEADS, V_DIM), lambda i: (ctx_i(i), layer, 0, 0)),
        ],
        out_shape=[
            jax.ShapeDtypeStruct((N_TOK, ATT_WIDTH), BF),
            jax.ShapeDtypeStruct((N_TOK, ATT_WIDTH), BF),
            jax.ShapeDtypeStruct((N_TILES, ATT_WIDTH, TM), BF),
            jax.ShapeDtypeStruct((N_TOK, CONV_WIDTH), BF),
            jax.ShapeDtypeStruct((BATCH, DEPTH, ATT_WIDTH, SEQ), F32),
            jax.ShapeDtypeStruct((BATCH, DEPTH, SEQ * N_HEADS, V_DIM), F32),
        ],
        scratch_shapes=[
            pltpu.VMEM((TM, D_MODEL), BF),
            pltpu.VMEM((N_IN_TILES, D_MODEL, TN_IN), BF),
            pltpu.VMEM((TM, CONV_WIDTH), F32),
            pltpu.VMEM((TM, CONV_WIDTH), F32),
            pltpu.VMEM((IN_STAGE_SLOTS, STAGE_W_ROWS, STAGE_W_COLS), F32),
            pltpu.SemaphoreType.DMA((IN_STAGE_SLOTS,)),
        ],
        input_output_aliases=aliases,
        compiler_params=_params(("arbitrary",)),
        name=f"in_proj_l{layer}",
    )(*args)


def _lambda(lq_ref, layer, lam_init):
    lq = lq_ref[layer]
    a = jnp.exp(jnp.sum(lq[0:1] * lq[1:2], axis=-1, keepdims=True))
    b = jnp.exp(jnp.sum(lq[2:3] * lq[3:4], axis=-1, keepdims=True))
    return a - b + lam_init


def _head_norm(o, sg, lam_init):
    return _rms(o) * sg * (1.0 - lam_init)


def _attn_ctx_kernel(q_ref, kt_ref, v_ref, lq_ref, sg_ref, o_ref, sc_s, *, layer, lam_init):
    lam = _lambda(lq_ref, layer, lam_init)
    sg = sg_ref[layer:layer + 1, :]

    def sequence(b, carry):
        pos = pl.ds(pl.multiple_of(b * SEQ, SEQ), SEQ)

        def scores(h):
            for s in range(2):
                d = slice(h * V_DIM + s * QK_DIM, h * V_DIM + (s + 1) * QK_DIM)
                sc_s[h % 2, s] = jnp.dot(q_ref[pos, d], kt_ref[d, pos], preferred_element_type=F32)

        def finish(h):
            cols = slice(h * V_DIM, (h + 1) * V_DIM)
            v = v_ref[pos, cols]
            outs = []
            for s in range(2):
                sc = sc_s[h % 2, s]
                e = jnp.exp2(sc - jnp.max(sc, axis=-1, keepdims=True))
                r = 1.0 / jnp.sum(e, axis=-1, keepdims=True)
                outs.append(jnp.dot(e.astype(BF), v, preferred_element_type=F32) * r)
            o = outs[0] - lam * outs[1]
            o_ref[pos, cols] = _head_norm(o, sg, lam_init).astype(BF)

        scores(0)
        for h in range(N_HEADS):
            if h + 1 < N_HEADS:
                scores(h + 1)
            finish(h)
        return carry

    lax.fori_loop(0, CTX_SEQ_PER_STEP, sequence, 0)


def _attn_ctx(layer, q, kt, v, lambda_qk, subln_g, lam_init):
    rows = CTX_SEQ_PER_STEP * SEQ
    per_tile = TM // rows
    return pl.pallas_call(
        functools.partial(_attn_ctx_kernel, layer=layer, lam_init=lam_init),
        grid=(N_CTX // rows,),
        in_specs=[
            pl.BlockSpec((rows, ATT_WIDTH), lambda b: (b, 0)),
            pl.BlockSpec((None, ATT_WIDTH, rows), lambda b: (b // per_tile, 0, b % per_tile)),
            pl.BlockSpec((rows, ATT_WIDTH), lambda b: (b, 0)),
            WHOLE, WHOLE,
        ],
        out_specs=pl.BlockSpec((rows, ATT_WIDTH), lambda b: (b, 0)),
        out_shape=jax.ShapeDtypeStruct((N_TOK, ATT_WIDTH), BF),
        scratch_shapes=[pltpu.VMEM((2, 2, SEQ, SEQ), F32)],
        compiler_params=_params(("arbitrary",)),
        name=f"attn_ctx_l{layer}",
    )(q, kt, v, lambda_qk, subln_g)


TQ = 256
LAT_Q_PER_STEP = 2


def _attn_lat_kernel(q_ref, kt_ref, v_ref, ckt_ref, cv_ref, lq_ref, sg_ref, att_in_ref, o_ref, sc_s, *,
                     layer, lam_init):
    del att_in_ref
    lam = _lambda(lq_ref, layer, lam_init)
    sg = sg_ref[layer:layer + 1, :]

    units = [(b, h) for b in range(LAT_Q_PER_STEP) for h in range(N_HEADS)]

    def scores(u, s):
        b, h = units[u]
        d = slice(h * V_DIM + s * QK_DIM, h * V_DIM + (s + 1) * QK_DIM)
        q = q_ref[b * TQ:(b + 1) * TQ, d]
        sc_s[u % 2, s, :, :PAST_LEN] = jnp.dot(q, ckt_ref[d, :].astype(BF), preferred_element_type=F32)
        sc_s[u % 2, s, :, PAST_LEN:] = jnp.dot(q, kt_ref[d, :], preferred_element_type=F32)

    def softmax(u, s):
        sc = sc_s[u % 2, s]
        e = jnp.exp2(sc - jnp.max(sc, axis=-1, keepdims=True))
        return e, 1.0 / jnp.sum(e, axis=-1, keepdims=True)

    def finish(u, p1, p2):
        b, h = units[u]
        cols = slice(h * V_DIM, (h + 1) * V_DIM)
        e = jnp.concatenate([p1[0].astype(BF), p2[0].astype(BF)], axis=0)
        vc = cv_ref[pl.ds(h, PAST_LEN, stride=N_HEADS), :].astype(BF)
        pv = jnp.dot(e[:, :PAST_LEN], vc, preferred_element_type=F32)
        pv = pv + jnp.dot(e[:, PAST_LEN:], v_ref[:, cols], preferred_element_type=F32)
        o = pv[:TQ] * p1[1] - pv[TQ:] * (lam * p2[1])
        o_ref[b * TQ:(b + 1) * TQ, cols] = _head_norm(o, sg, lam_init).astype(BF)

    scores(0, 0)
    scores(0, 1)
    for u in range(len(units)):
        more = u + 1 < len(units)
        if more:
            scores(u + 1, 0)
        p1 = softmax(u, 0)
        if more:
            scores(u + 1, 1)
        finish(u, p1, softmax(u, 1))


def _attn_lat(layer, q, kt, v, cache_kt, cache_v, lambda_qk, subln_g, att, lam_init):
    rows = LAT_Q_PER_STEP * TQ
    nqb = DEC_SEQ // rows
    q0 = N_CTX // rows
    return pl.pallas_call(
        functools.partial(_attn_lat_kernel, layer=layer, lam_init=lam_init),
        grid=(DEC_BATCH, nqb),
        in_specs=[
            pl.BlockSpec((rows, ATT_WIDTH), lambda b, t: (q0 + b * nqb + t, 0)),
            pl.BlockSpec((None, ATT_WIDTH, DEC_SEQ), lambda b, t: (CTX_TILES + b, 0, 0)),
            pl.BlockSpec((DEC_SEQ, ATT_WIDTH), lambda b, t: (CTX_TILES + b, 0)),
            pl.BlockSpec((None, None, ATT_WIDTH, PAST_LEN), lambda b, t: (b, layer, 0, 0)),
            pl.BlockSpec((None, None, PAST_LEN * N_HEADS, V_DIM), lambda b, t: (b, layer, 0, 0)),
            WHOLE, WHOLE,
            pl.BlockSpec(memory_space=pl.ANY),
        ],
        out_specs=pl.BlockSpec((rows, ATT_WIDTH), lambda b, t: (q0 + b * nqb + t, 0)),
        out_shape=jax.ShapeDtypeStruct((N_TOK, ATT_WIDTH), BF),
        scratch_shapes=[pltpu.VMEM((2, 2, TQ, PAST_LEN + DEC_SEQ), F32)],
        input_output_aliases={7: 0},
        compiler_params=_params(("arbitrary", "arbitrary")),
        name=f"attn_lat_l{layer}",
    )(q, kt, v, cache_kt, cache_v, lambda_qk, subln_g, att)


def _out_kernel(att_ref, conv_ref, w_ref, xa_ref, xb_ref, mod_ref, gf_ref, *rest, layer, route):
    if route:
        wr_ref, xo_ref, h2_ref, rt_ref, n_ref, wb_s = rest
    else:
        xo_ref, h2_ref, wb_s = rest
    i = pl.program_id(0)

    @pl.when(i == 0)
    def _():
        wb_s[...] = w_ref[...].astype(BF)

    lat = i >= CTX_TILES
    gain = gf_ref[layer:layer + 1, :] * (1.0 + _mod(mod_ref, layer, i, 4))
    for c in range(TM // ROW_CHUNK):
        rows = slice(c * ROW_CHUNK, (c + 1) * ROW_CHUNK)
        mo = jnp.dot(att_ref[rows, :], wb_s[:ATT_WIDTH, :], preferred_element_type=F32)
        mo = mo + jnp.dot(conv_ref[rows, :], wb_s[ATT_WIDTH:, :], preferred_element_type=F32)
        xn = jnp.where(lat, xb_ref[rows, :], xa_ref[rows, :]) + _mod(mod_ref, layer, i, 2) * mo
        xo_ref[rows, :] = xn
        h2_ref[rows, :] = (_rms(xn) * gain + _mod(mod_ref, layer, i, 3)).astype(BF)
    if route:
        _route(h2_ref[...], wr_ref, rt_ref, n_ref)


def _out_proj(layer, att, conv, w_out, xs, mod, g_ffn, w_router_pad=None):
    pair = isinstance(xs, tuple)
    xa, xb = xs if pair else (xs, xs)
    spec_a, spec_b = _stream_specs(pair)
    row_spec = pl.BlockSpec((TM, D_MODEL), lambda i: (i, 0))
    lane_spec = pl.BlockSpec((TM, LANES), lambda i: (i, 0))
    in_specs = [
        pl.BlockSpec((TM, ATT_WIDTH), lambda i: (i, 0)),
        pl.BlockSpec((TM, CONV_WIDTH), lambda i: (i, 0)),
        pl.BlockSpec((None, D_MODEL, D_MODEL), lambda i: (layer, 0, 0)),
        spec_a, spec_b,
        WHOLE, WHOLE,
    ]
    args = [att, conv, w_out, xa, xb, mod, g_ffn]
    out_specs = [row_spec, row_spec]
    out_shape = [jax.ShapeDtypeStruct((N_TOK, D_MODEL), F32), jax.ShapeDtypeStruct((N_TOK, D_MODEL), BF)]
    route = w_router_pad is not None
    if route:
        in_specs.append(WHOLE)
        args.append(w_router_pad)
        out_specs += [lane_spec, pl.BlockSpec((None, SUBLANES, LANES), lambda i: (i, 0, 0))]
        out_shape += [jax.ShapeDtypeStruct((N_TOK, LANES), F32),
                      jax.ShapeDtypeStruct((N_TILES, SUBLANES, LANES), jnp.int32)]
    return pl.pallas_call(
        functools.partial(_out_kernel, layer=layer, route=route),
        grid=(N_TILES,),
        in_specs=in_specs,
        out_specs=out_specs,
        out_shape=out_shape,
        scratch_shapes=[pltpu.VMEM((D_MODEL, D_MODEL), BF)],
        compiler_params=_params(("arbitrary",)),
        name=f"out_proj_l{layer}",
    )(*args)


def _weight_pieces(layer):
    pieces = []
    for c0 in range(0, D_MODEL, STAGE_W_COLS):
        cols = slice(c0, c0 + STAGE_W_COLS)
        pieces.append((0, layer, slice(0, D_MODEL), cols, None, slice(0, D_MODEL), cols))
    for half in range(FF_HALVES):
        for part in range(2):
            src0 = part * D_FF + half * FF_HALF
            for off in range(0, FF_HALF, STAGE_W_COLS):
                n = min(STAGE_W_COLS, FF_HALF - off)
                pieces.append((1, 0, slice(0, D_MODEL), slice(src0 + off, src0 + off + n),
                               half, slice(0, D_MODEL), slice(part * FF_HALF + off, part * FF_HALF + off + n)))
    for r0 in range(0, D_FF, STAGE_W_ROWS):
        rows = slice(r0, min(r0 + STAGE_W_ROWS, D_FF))
        for c0 in range(0, D_MODEL, STAGE_W_COLS):
            cols = slice(c0, c0 + STAGE_W_COLS)
            pieces.append((2, 0, rows, cols, None, rows, cols))
    return pieces


def _load_weights(pieces, hbm, resident, stage_s, sem):
    slots = stage_s.shape[0]

    def copy(k):
        src, idx, rows, cols, _, _, _ = pieces[k]
        nr, nc = rows.stop - rows.start, cols.stop - cols.start
        slot = k % slots
        return pltpu.make_async_copy(hbm[src].at[idx, rows, cols], stage_s.at[slot, :nr, :nc], sem.at[slot])

    for k in range(min(slots - 1, len(pieces))):
        copy(k).start()
    for k, (src, _, rows, cols, didx, drows, dcols) in enumerate(pieces):
        if k + slots - 1 < len(pieces):
            copy(k + slots - 1).start()
        copy(k).wait()
        nr, nc = rows.stop - rows.start, cols.stop - cols.start
        piece = stage_s[k % slots, :nr, :nc].astype(BF)
        if didx is None:
            resident[src][drows, dcols] = piece
        else:
            resident[src][didx, drows, dcols] = piece


def _dense_layer_kernel(att_ref, conv_ref, xa_ref, xb_ref, mod_ref, gf_ref, wo_hbm, wgu_hbm, wd_hbm,
                        o_ref, wo_b, wgu_b, wd_b, stage_s, sem, *, layer):
    i = pl.program_id(0)
    tile = i // (TM // FF_TM)

    @pl.when(i == 0)
    def _():
        _load_weights(_weight_pieces(layer), (wo_hbm, wgu_hbm, wd_hbm), (wo_b, wgu_b, wd_b), stage_s, sem)

    mo = jnp.dot(att_ref[...], wo_b[:ATT_WIDTH, :], preferred_element_type=F32)
    mo = mo + jnp.dot(conv_ref[...], wo_b[ATT_WIDTH:, :], preferred_element_type=F32)
    x = jnp.where(tile >= CTX_TILES, xb_ref[...], xa_ref[...])
    xn = x + _mod(mod_ref, layer, tile, 2) * mo
    h = (_rms(xn) * gf_ref[layer:layer + 1, :]) * (1.0 + _mod(mod_ref, layer, tile, 4)) + _mod(mod_ref, layer, tile, 3)
    h = h.astype(BF)
    y = None
    for half in range(FF_HALVES):
        gu = jnp.dot(h, wgu_b[half], preferred_element_type=F32)
        act = (_silu(gu[:, :FF_HALF]) * gu[:, FF_HALF:]).astype(BF)
        part = jnp.dot(act, wd_b[half * FF_HALF:(half + 1) * FF_HALF, :], preferred_element_type=F32)
        y = part if y is None else y + part
    o_ref[...] = xn + _mod(mod_ref, layer, tile, 5) * y


def _dense_layer(layer, att, conv, xs, mod, g_ffn, w_out, w_gu, w_down):
    xa, xb = xs
    n_ctx = N_CTX // FF_TM
    row = lambda width: pl.BlockSpec((FF_TM, width), lambda i: (i, 0))
    hbm = pl.BlockSpec(memory_space=pl.ANY)
    return pl.pallas_call(
        functools.partial(_dense_layer_kernel, layer=layer),
        grid=(N_TOK // FF_TM,),
        in_specs=[
            row(ATT_WIDTH), row(CONV_WIDTH),
            pl.BlockSpec((FF_TM, D_MODEL), lambda i: (jnp.minimum(i, n_ctx - 1), 0)),
            pl.BlockSpec((FF_TM, D_MODEL), lambda i: (jnp.maximum(i - n_ctx, 0), 0)),
            WHOLE, WHOLE, hbm, hbm, hbm,
        ],
        out_specs=row(D_MODEL),
        out_shape=jax.ShapeDtypeStruct((N_TOK, D_MODEL), F32),
        scratch_shapes=[
            pltpu.VMEM((D_MODEL, D_MODEL), BF),
            pltpu.VMEM((FF_HALVES, D_MODEL, 2 * FF_HALF), BF),
            pltpu.VMEM((D_FF, D_MODEL), BF),
            pltpu.VMEM((FF_STAGE_SLOTS, STAGE_W_ROWS, STAGE_W_COLS), F32),
            pltpu.SemaphoreType.DMA((FF_STAGE_SLOTS,)),
        ],
        compiler_params=_params(("arbitrary",)),
        name="dense_layer",
    )(att, conv, xa, xb, mod, g_ffn, w_out, w_gu, w_down)


def _route(h, wr_ref, rt_ref, n_ref):
    logits = jnp.dot(h, wr_ref[...].astype(BF), preferred_element_type=F32)
    lane = lax.broadcasted_iota(jnp.int32, logits.shape, 1)
    lg = jnp.where(lane < N_EXPERTS, logits, -jnp.inf)
    m1 = jnp.max(lg, axis=-1, keepdims=True)
    i1 = jnp.min(jnp.where(lg == m1, lane, LANES), axis=-1, keepdims=True)
    lg2 = jnp.where(lane == i1, -jnp.inf, lg)
    m2 = jnp.max(lg2, axis=-1, keepdims=True)
    i2 = jnp.min(jnp.where(lg2 == m2, lane, LANES), axis=-1, keepdims=True)
    e2 = jnp.exp(m2 - m1)
    w1 = 1.0 / (1.0 + e2)
    w2 = e2 / (1.0 + e2)

    sel1 = lane == i1
    sel2 = lane == i2
    onehot = jnp.logical_or(sel1, sel2)
    rows = lax.broadcasted_iota(jnp.int32, (ST, ST), 0)
    colsi = lax.broadcasted_iota(jnp.int32, (ST, ST), 1)
    earlier = (colsi < rows).astype(BF)
    onehot_b = onehot.astype(BF)
    before = jnp.concatenate(
        [jnp.dot(earlier, onehot_b[s * ST:(s + 1) * ST], preferred_element_type=F32) for s in range(SUB_PER_TILE)],
        axis=0)
    onehot_f = onehot.astype(F32)
    counts = [jnp.sum(onehot_f[s * ST:(s + 1) * ST], axis=0, keepdims=True) for s in range(SUB_PER_TILE)]
    counts = jnp.concatenate(counts + [jnp.zeros((SUBLANES - SUB_PER_TILE, LANES), F32)], axis=0).astype(jnp.int32)
    seg_len = ((counts + (SEG_ALIGN - 1)) // SEG_ALIGN) * SEG_ALIGN
    n_ref[...] = seg_len
    la = lax.broadcasted_iota(jnp.int32, (LANES, LANES), 0)
    lb = lax.broadcasted_iota(jnp.int32, (LANES, LANES), 1)
    seg_start = jnp.dot(seg_len.astype(F32).astype(BF), (la < lb).astype(BF), preferred_element_type=F32)
    start = jnp.concatenate(
        [jnp.broadcast_to(seg_start[s:s + 1], (ST, LANES)) for s in range(SUB_PER_TILE)], axis=0)
    where = before + start
    lp1 = jnp.sum(jnp.where(sel1, where, 0.0), axis=-1, keepdims=True)
    lp2 = jnp.sum(jnp.where(sel2, where, 0.0), axis=-1, keepdims=True)
    rt = jnp.where(lane == GATE_LANE, w1, jnp.where(lane == GATE_LANE + 1, w2, 0.0))
    rt_ref[...] = jnp.where(lane == ROW_LANE, lp1, jnp.where(lane == ROW_LANE + 1, lp2, rt))


def _chunk_copies(s, cnt_ref, cdst_ref, stage, rows_hbm, sem, *, to_hbm, wait):
    def copy(v, h):
        return pltpu.make_async_copy(v, h, sem) if to_hbm else pltpu.make_async_copy(h, v, sem)

    if wait:
        for z in WAIT_PIECES:
            @pl.when((cnt_ref[s] & z) != 0)
            def _():
                copy(stage.at[pl.ds(0, z * SEG_ALIGN)], rows_hbm.at[pl.ds(0, z * SEG_ALIGN)]).wait()
        return

    def start(c, priority):
        v = stage.at[pl.ds(pl.multiple_of(c * SEG_ALIGN, SEG_ALIGN), SEG_ALIGN)]
        h = rows_hbm.at[pl.ds(pl.multiple_of(cdst_ref[s * STAGE_CHUNKS + c], SEG_ALIGN), SEG_ALIGN)]
        copy(v, h).start(priority=priority)

    def pair(p, carry):
        start(2 * p, 0)
        start(2 * p + 1, 1)
        return carry

    n = cnt_ref[s]
    lax.fori_loop(0, n // 2, pair, 0)

    @pl.when((n & 1) != 0)
    def _():
        start(n - 1, 0)


def _dispatch_kernel(cnt_ref, cdst_ref, h_ref, rt_ref, xs_ref, stage_s, sem):
    copies = functools.partial(_chunk_copies, cnt_ref=cnt_ref, cdst_ref=cdst_ref, rows_hbm=xs_ref, to_hbm=True)
    for k in range(SUB_PER_TILE):
        s = pl.program_id(0) * SUB_PER_TILE + k
        slot = k % DISPATCH_SLOTS
        rows = slice(k * ST, (k + 1) * ST)

        @pl.when(s >= DISPATCH_SLOTS)
        def _():
            copies(s - DISPATCH_SLOTS, stage=stage_s.at[slot], sem=sem.at[slot], wait=True)

        at = rt_ref[rows, :].T[ROW_LANE:ROW_LANE + 2, :].astype(jnp.int32)
        r = lax.broadcasted_iota(jnp.int32, (STAGE_ROWS, ST), 0)
        pick = jnp.logical_or(r == at[0:1, :], r == at[1:2, :]).astype(BF)
        stage_s[slot] = jnp.dot(pick, h_ref[rows, :], preferred_element_type=F32).astype(BF)
        copies(s, stage=stage_s.at[slot], sem=sem.at[slot], wait=False)

    @pl.when(pl.program_id(0) == N_TILES - 1)
    def _():
        for s in range(N_SUB - DISPATCH_SLOTS, N_SUB):
            copies(s, stage=stage_s.at[s % DISPATCH_SLOTS], sem=sem.at[s % DISPATCH_SLOTS], wait=True)


def _dispatch(cnt, cdst, h2, rt):
    assert SUB_PER_TILE % DISPATCH_SLOTS == 0
    grid_spec = pltpu.PrefetchScalarGridSpec(
        num_scalar_prefetch=2,
        grid=(N_TILES,),
        in_specs=[
            pl.BlockSpec((TM, D_MODEL), lambda i, *_: (i, 0)),
            pl.BlockSpec((TM, LANES), lambda i, *_: (i, 0)),
        ],
        out_specs=pl.BlockSpec(memory_space=pl.ANY),
        scratch_shapes=[pltpu.VMEM((DISPATCH_SLOTS, STAGE_ROWS, D_MODEL), BF),
                        pltpu.SemaphoreType.DMA((DISPATCH_SLOTS,))],
    )
    return pl.pallas_call(
        _dispatch_kernel,
        grid_spec=grid_spec,
        out_shape=jax.ShapeDtypeStruct((R_PAD, D_MODEL), BF),
        compiler_params=_params(("arbitrary",)),
        name="moe_dispatch",
    )(cnt, cdst, h2, rt)


def _expert_weights(r, te_ref, nt_ref, nxt_ref, w_hbm, wf_s, wb_s, sem):
    def fetch(e):
        return pltpu.make_async_copy(w_hbm.at[0, e], wf_s, sem)

    @pl.when(r == 0)
    def _():
        fetch(te_ref[0]).start()

    first = jnp.logical_or(r == 0, te_ref[r] != te_ref[jnp.maximum(r - 1, 0)])

    @pl.when(jnp.logical_and(r < nt_ref[0], first))
    def _():
        fetch(te_ref[r]).wait()
        wb_s[...] = wf_s[...].astype(BF)

        @pl.when(nxt_ref[r] >= 0)
        def _():
            fetch(nxt_ref[r]).start()


def _moe_ffn_kernel(te_ref, nt_ref, nxt_ref, used_ref, x_ref, wgu_hbm, wd_hbm, o_ref,
                    wgu_f, wgu_b, wd_f, wd_b, sem):
    def ffn(rows):
        gu = jnp.dot(x_ref[rows, :], wgu_b[...], preferred_element_type=F32)
        act = (_silu(gu[:, :D_FF_EXPERT]) * gu[:, D_FF_EXPERT:]).astype(BF)
        o_ref[rows, :] = jnp.dot(act, wd_b[...], preferred_element_type=F32).astype(BF)

    for k in range(TILES_PER_STEP):
        r = pl.program_id(0) * TILES_PER_STEP + k
        _expert_weights(r, te_ref, nt_ref, nxt_ref, wgu_hbm, wgu_f, wgu_b, sem.at[0])
        _expert_weights(r, te_ref, nt_ref, nxt_ref, wd_hbm, wd_f, wd_b, sem.at[1])

        @pl.when(jnp.logical_and(r < nt_ref[0], used_ref[r] > TG // 2))
        def _():
            ffn(slice(k * TG, (k + 1) * TG))

        @pl.when(jnp.logical_and(r < nt_ref[0], used_ref[r] <= TG // 2))
        def _():
            ffn(slice(k * TG, k * TG + TG // 2))


def _moe_ffn(te, nt, nxt, used, rows, w_gu, w_down):
    tile_map = lambda i, te, nt, nxt, used: (jnp.minimum(i, (nt[0] - 1) // TILES_PER_STEP), 0)
    grid_spec = pltpu.PrefetchScalarGridSpec(
        num_scalar_prefetch=4,
        grid=(NT_G // TILES_PER_STEP,),
        in_specs=[pl.BlockSpec((TILES_PER_STEP * TG, D_MODEL), tile_map),
                  pl.BlockSpec(memory_space=pl.ANY), pl.BlockSpec(memory_space=pl.ANY)],
        out_specs=pl.BlockSpec((TILES_PER_STEP * TG, D_MODEL), tile_map),
        scratch_shapes=[
            pltpu.VMEM((D_MODEL, 2 * D_FF_EXPERT), F32), pltpu.VMEM((D_MODEL, 2 * D_FF_EXPERT), BF),
            pltpu.VMEM((D_FF_EXPERT, D_MODEL), F32), pltpu.VMEM((D_FF_EXPERT, D_MODEL), BF),
            pltpu.SemaphoreType.DMA((2,)),
        ],
    )
    return pl.pallas_call(
        _moe_ffn_kernel,
        grid_spec=grid_spec,
        out_shape=jax.ShapeDtypeStruct((R_PAD, D_MODEL), BF),
        compiler_params=_params(("arbitrary",)),
        name="moe_ffn",
    )(te, nt, nxt, used, rows, w_gu, w_down)


def _combine_kernel(cnt_ref, cdst_ref, ys_ref, rt_ref, x_ref, mod_ref, fg_ref,
                    oa_ref, ob_ref, stage_s, sem, *, layer):
    i = pl.program_id(0)
    copies = functools.partial(_chunk_copies, cnt_ref=cnt_ref, cdst_ref=cdst_ref, rows_hbm=ys_ref, to_hbm=False)

    @pl.when(i == 0)
    def _():
        stage_s[...] = jnp.zeros_like(stage_s)
        for s in range(COMBINE_AHEAD):
            copies(s, stage=stage_s.at[s], sem=sem.at[s], wait=False)

    for k in range(SUB_PER_TILE):
        s = i * SUB_PER_TILE + k
        slot = s % COMBINE_SLOTS
        ahead = (s + COMBINE_AHEAD) % COMBINE_SLOTS
        rows = slice(k * ST, (k + 1) * ST)

        @pl.when(s + COMBINE_AHEAD < N_SUB)
        def _():
            copies(s + COMBINE_AHEAD, stage=stage_s.at[ahead], sem=sem.at[ahead], wait=False)

        copies(s, stage=stage_s.at[slot], sem=sem.at[slot], wait=True)

        rt = rt_ref[rows, :]
        at = rt[:, ROW_LANE:ROW_LANE + 2].astype(jnp.int32)
        r = lax.broadcasted_iota(jnp.int32, (ST, STAGE_ROWS), 1)
        staged = stage_s[slot]
        a = jnp.dot((r == at[:, 0:1]).astype(BF), staged, preferred_element_type=F32)
        b = jnp.dot((r == at[:, 1:2]).astype(BF), staged, preferred_element_type=F32)
        y = rt[:, GATE_LANE:GATE_LANE + 1] * a + rt[:, GATE_LANE + 1:GATE_LANE + 2] * b
        xn = x_ref[rows, :] + _mod(mod_ref, layer, i, 5) * y
        out = _rms(xn) * fg_ref[...]

        @pl.when(i < CTX_TILES)
        def _():
            oa_ref[rows, :] = out

        @pl.when(i >= CTX_TILES)
        def _():
            ob_ref[rows, :] = out


def _combine(layer, cnt, cdst, ys, rt, x, mod, final_g):
    assert COMBINE_AHEAD < COMBINE_SLOTS
    grid_spec = pltpu.PrefetchScalarGridSpec(
        num_scalar_prefetch=2,
        grid=(N_TILES,),
        in_specs=[
            pl.BlockSpec(memory_space=pl.ANY),
            pl.BlockSpec((TM, LANES), lambda i, *_: (i, 0)),
            pl.BlockSpec((TM, D_MODEL), lambda i, *_: (i, 0)),
            WHOLE, WHOLE,
        ],
        out_specs=[
            pl.BlockSpec((TM, D_MODEL), lambda i, *_: (jnp.minimum(i, CTX_TILES - 1), 0)),
            pl.BlockSpec((TM, D_MODEL), lambda i, *_: (jnp.maximum(i - CTX_TILES, 0), 0)),
        ],
        scratch_shapes=[pltpu.VMEM((COMBINE_SLOTS, STAGE_ROWS, D_MODEL), BF),
                        pltpu.SemaphoreType.DMA((COMBINE_SLOTS,))],
    )
    return pl.pallas_call(
        functools.partial(_combine_kernel, layer=layer),
        grid_spec=grid_spec,
        out_shape=[
            jax.ShapeDtypeStruct((N_CTX, D_MODEL), F32),
            jax.ShapeDtypeStruct((N_LAT, D_MODEL), F32),
        ],
        compiler_params=_params(("arbitrary",)),
        name="moe_combine",
    )(cnt, cdst, ys, rt, x, mod, final_g)


def _group_layout(n_tiles):
    n = n_tiles[:, :SUB_PER_TILE, :N_EXPERTS].reshape(N_SUB, N_EXPERTS)
    tiles = (jnp.sum(n, axis=0) + TG - 1) // TG
    tile_end = jnp.cumsum(tiles)
    region = (tile_end - tiles) * TG
    dst = region[None, :] + jnp.cumsum(n, axis=0) - n
    seg_end = jnp.cumsum(n, axis=1)
    seg = seg_end - n
    row = jnp.arange(STAGE_CHUNKS, dtype=jnp.int32) * SEG_ALIGN
    owner = jnp.sum((row[None, :, None] >= seg_end[:, None, :]).astype(jnp.int32), axis=-1)
    own = jnp.minimum(owner, N_EXPERTS - 1)[..., None] == jnp.arange(N_EXPERTS)
    cdst = jnp.sum(jnp.where(own, (dst - seg)[:, None, :], 0), axis=-1) + row[None, :]
    cnt = seg_end[:, -1] // SEG_ALIGN
    nt = tile_end[-1]
    tile_id = jnp.minimum(jnp.arange(NT_G, dtype=jnp.int32), nt - 1)
    te = jnp.sum((tile_id[:, None] >= tile_end[None, :]).astype(jnp.int32), axis=-1)
    after = jnp.sum(jnp.where(te[:, None] == jnp.arange(N_EXPERTS), tile_end[None, :], 0), axis=-1)
    nxt = jnp.where(after < nt, jnp.sum((after[:, None] >= tile_end[None, :]).astype(jnp.int32), axis=-1), -1)
    mine = te[:, None] == jnp.arange(N_EXPERTS)
    region_end = jnp.sum(jnp.where(mine, (region + jnp.sum(n, axis=0))[None, :], 0), axis=-1)
    used = jnp.clip(region_end - tile_id * TG, 0, TG)
    i32 = lambda a: a.astype(jnp.int32)
    return (i32(cnt), i32(cdst.reshape(N_SUB * STAGE_CHUNKS)), i32(te), i32(nt.reshape(1)), i32(nxt), i32(used))


def _rope_tables():
    p = np.arange(DEC_SEQ)
    row = (p // GRID_W).astype(np.float32)
    col = (p % GRID_W).astype(np.float32)
    half = QK_DIM // 4
    freqs = (ROPE_BASE ** (-np.arange(half, dtype=np.float32) / half)).astype(np.float32)
    lane = np.arange(V_DIM)
    f = freqs[lane & (half - 1)]
    use_col = (lane & (2 * half)) != 0
    ang = (np.where(use_col[None, :], col[:, None], row[:, None]) * f[None, :]).astype(np.float32)
    upper = (lane & half) != 0
    sin = np.sin(ang)
    return jnp.asarray(np.cos(ang), F32), jnp.asarray(np.where(upper[None, :], sin, -sin), F32)


def kernel(x_prompt, x_sample, cache_k, cache_v, c, c_ctx, w_ada, b_ada, norm_mix_g, norm_ffn_g,
           w_in, lambda_qk, subln_g, conv_w, w_out, w_gu_dense, w_down_dense, w_router,
           w_gu_moe, w_down_moe, final_g):
    assert DEPTH == 2
    xs = (x_prompt.reshape(N_CTX, D_MODEL), x_sample.reshape(N_LAT, D_MODEL))
    mod = _ada(c_ctx, c, w_ada, b_ada)
    cos_t, sin_t = _rope_tables()
    cache_kt = jnp.transpose(cache_k, (0, 1, 3, 4, 5, 2)).reshape(DEC_BATCH, DEPTH, ATT_WIDTH, PAST_LEN)
    cache_v4 = cache_v.reshape(DEC_BATCH, DEPTH, PAST_LEN * N_HEADS, V_DIM)

    new_kv = None
    for layer in range(DEPTH):
        lam_init = 0.8 - 0.6 * math.exp(-0.3 * layer)
        q, v, kt, conv, nk, nv = _in_proj(layer, xs, mod, norm_mix_g, w_in, cos_t, sin_t, conv_w, new_kv)
        new_kv = (nk, nv)
        att = _attn_ctx(layer, q, kt, v, lambda_qk, subln_g, lam_init)
        att = _attn_lat(layer, q, kt, v, cache_kt, cache_v4, lambda_qk, subln_g, att, lam_init)
        if layer == 0:
            xs = _dense_layer(layer, att, conv, xs, mod, norm_ffn_g, w_out, w_gu_dense, w_down_dense)
        else:
            wr = jnp.pad(w_router[0], ((0, 0), (0, LANES - N_EXPERTS)))
            x1, h2, rt, n_tiles = _out_proj(layer, att, conv, w_out, xs, mod, norm_ffn_g, wr)
            cnt, cdst, te, nt, nxt, used = _group_layout(n_tiles)
            xsort = _dispatch(cnt, cdst, h2, rt)
            ys = _moe_ffn(te, nt, nxt, used, xsort, w_gu_moe, w_down_moe)
            y_ctx, y_lat = _combine(layer, cnt, cdst, ys, rt, x1, mod, final_g.reshape(1, D_MODEL))
    nk, nv = new_kv
    new_k = jnp.transpose(nk.reshape(BATCH, DEPTH, N_HEADS, 2, QK_DIM, SEQ), (0, 1, 5, 2, 3, 4))
    new_v = nv.reshape(BATCH, DEPTH, SEQ, N_HEADS, V_DIM)
    return (y_ctx.reshape(BATCH, SEQ, D_MODEL), y_lat.reshape(DEC_BATCH, DEC_SEQ, D_MODEL), new_k, new_v)
```

```python
import functools
import math

import numpy as np
import jax
import jax.numpy as jnp
from jax import lax
from jax.experimental import pallas as pl
from jax.experimental.pallas import tpu as pltpu

D_MODEL = 1024
BATCH = 16
SEQ = 256
DEPTH = 2
DEC_BATCH = 4
DEC_SEQ = 1024
PAST_LEN = 512
GRID_W = 64
ATT_WIDTH = 512
CONV_WIDTH = 512
N_HEADS = 4
V_DIM = 128
QK_DIM = 64
ROPE_BASE = 10000.0
D_FF = 2816
N_EXPERTS = 8
D_FF_EXPERT = 1408
N_MOD = 6
NORM_EPS = 1e-6
Q_SCALE = QK_DIM ** -0.5 * math.log2(math.e)
IN_COLS = 3 * ATT_WIDTH + 3 * CONV_WIDTH

N_CTX = BATCH * SEQ
N_LAT = DEC_BATCH * DEC_SEQ
N_TOK = N_CTX + N_LAT
TM = 1024
N_TILES = N_TOK // TM
CTX_TILES = N_CTX // TM
SEQ_PER_TILE = TM // SEQ
CTX_SEQ_PER_STEP = 4
COND_ROWS = 8
TN_IN = 1024
N_IN_TILES = IN_COLS // TN_IN
ROW_CHUNK = 512
FF_TM = 512
FF_HALVES = 2
FF_HALF = D_FF // FF_HALVES
STAGE_W_ROWS = 1024
STAGE_W_COLS = 512
FF_STAGE_SLOTS = 8
IN_STAGE_SLOTS = 4
TN_ADA = 1536
TG = 512
ST = 256
SUB_PER_TILE = TM // ST
N_SUB = N_TOK // ST
SEG_ALIGN = 8
STAGE_ROWS = 640
STAGE_CHUNKS = STAGE_ROWS // SEG_ALIGN
WAIT_PIECES = (64, 32, 16, 8, 4, 2, 1)
COMBINE_SLOTS = 4
COMBINE_AHEAD = 3
TILES_PER_STEP = 2
_MAX_SORTED_ROWS = 2 * N_TOK + N_SUB * N_EXPERTS * (SEG_ALIGN - 1) + N_EXPERTS * (TG - SEG_ALIGN)
NT_G = -(-_MAX_SORTED_ROWS // (TG * TILES_PER_STEP)) * TILES_PER_STEP
R_PAD = NT_G * TG
LANES = 128
SUBLANES = 8
GATE_LANE = 0
ROW_LANE = 2
VMEM_LIMIT = 60 * 1024 * 1024

BF = jnp.bfloat16
F32 = jnp.float32


def _params(sem, vmem=VMEM_LIMIT):
    return pltpu.CompilerParams(dimension_semantics=sem, vmem_limit_bytes=vmem)


def _mod_row(i):
    return jnp.where(i < CTX_TILES, 0, i - (CTX_TILES - 1))


WHOLE = pl.BlockSpec(memory_space=pltpu.VMEM)


def _mod(mod_ref, layer, i, c):
    return mod_ref[layer, pl.ds(_mod_row(i), 1), c * D_MODEL:(c + 1) * D_MODEL]


def _stream_specs(pair, width=D_MODEL):
    a = pl.BlockSpec((TM, width), lambda i, *_: (jnp.minimum(i, CTX_TILES - 1), 0))
    if pair:
        b = pl.BlockSpec((TM, width), lambda i, *_: (jnp.maximum(i - CTX_TILES, 0), 0))
    else:
        b = pl.BlockSpec((TM, width), lambda i, *_: (jnp.maximum(i, CTX_TILES), 0))
    return a, b


def _silu(x):
    return x / (1.0 + jnp.exp(-x))


def _rms(x):
    return x * lax.rsqrt(jnp.mean(x * x, axis=-1, keepdims=True) + NORM_EPS)


def _ada_kernel(cc_ref, c_ref, w_ref, b_ref, o_ref):
    pad = jnp.zeros((COND_ROWS - 1 - DEC_BATCH, D_MODEL), F32)
    cond = jnp.concatenate([cc_ref[...], c_ref[...], pad], axis=0)
    s = _silu(cond).astype(BF)
    bias = b_ref[pl.ds(pl.program_id(0), 1), :]
    o_ref[...] = jnp.dot(s, w_ref[...].astype(BF), preferred_element_type=F32) + bias


def _ada(c_ctx, c, w_ada, b_ada):
    n = N_MOD * D_MODEL
    return pl.pallas_call(
        _ada_kernel,
        grid=(DEPTH, n // TN_ADA),
        in_specs=[
            WHOLE, WHOLE,
            pl.BlockSpec((None, D_MODEL, TN_ADA), lambda l, j: (l, 0, j)),
            pl.BlockSpec((DEPTH, TN_ADA), lambda l, j: (0, j)),
        ],
        out_specs=pl.BlockSpec((None, COND_ROWS, TN_ADA), lambda l, j: (l, 0, j)),
        out_shape=jax.ShapeDtypeStruct((DEPTH, COND_ROWS, n), F32),
        compiler_params=_params(("arbitrary", "arbitrary")),
        name="ada_mod",
    )(c_ctx.reshape(1, D_MODEL), c, w_ada, b_ada)


def _in_weight_pieces(layer):
    pieces = []
    for c0 in range(0, IN_COLS, STAGE_W_COLS):
        g, off = divmod(c0, TN_IN)
        pieces.append((0, layer, slice(0, D_MODEL), slice(c0, c0 + STAGE_W_COLS),
                       g, slice(0, D_MODEL), slice(off, off + STAGE_W_COLS)))
    return pieces


def _in_kernel(*refs, layer, aliased):
    xa_ref, xb_ref, mod_ref, g_ref, w_hbm, cos_ref, sin_ref, cw_ref = refs[:8]
    refs = refs[10:] if aliased else refs[8:]
    q_ref, v_ref, kt_ref, conv_ref, nk_ref, nv_ref, h_s, wb_s, gb_s, gc_s, stage_s, sem = refs
    i = pl.program_id(0)

    @pl.when(i == 0)
    def _():
        _load_weights(_in_weight_pieces(layer), (w_hbm,), (wb_s,), stage_s, sem)

    chunks = [slice(c * ROW_CHUNK, (c + 1) * ROW_CHUNK) for c in range(TM // ROW_CHUNK)]
    seqs_per_chunk = ROW_CHUNK // SEQ

    def norm(x_ref, rows):
        gain = g_ref[layer:layer + 1, :] * (1.0 + _mod(mod_ref, layer, i, 1))
        h_s[rows, :] = (_rms(x_ref[rows, :]) * gain + _mod(mod_ref, layer, i, 0)).astype(BF)

    def proj(rows, group):
        acc = jnp.dot(h_s[rows, :], wb_s[group], preferred_element_type=F32)
        return acc[:, :ATT_WIDTH], acc[:, ATT_WIDTH:]

    def roped(a, rows):
        cos = jnp.concatenate([cos_ref[rows, :]] * N_HEADS, axis=1)
        sin = jnp.concatenate([sin_ref[rows, :]] * N_HEADS, axis=1)
        lane = lax.broadcasted_iota(jnp.int32, a.shape, 1)
        upper = (lane & (QK_DIM // 4)) != 0
        partner = jnp.where(upper, pltpu.roll(a, QK_DIM // 4, 1), pltpu.roll(a, ATT_WIDTH - QK_DIM // 4, 1))
        return a * cos + partner * sin

    def gated_conv(seq):
        for rows in chunks:
            gc, xin = proj(rows, 2)
            gc_s[rows, :] = gc * xin
        u = gc_s[...]
        pos = lax.broadcasted_iota(jnp.int32, (TM, 1), 0) & (seq - 1)
        prev = jnp.where(pos == 0, 0.0, pltpu.roll(u, 1, 0))
        nxt = jnp.where(pos == seq - 1, 0.0, pltpu.roll(u, TM - 1, 0))
        cw = cw_ref[layer]
        conv = prev * cw[0:1] + u * cw[1:2] + nxt * cw[2:3]
        conv_ref[...] = (gb_s[...] * conv).astype(BF)

    @pl.when(i >= CTX_TILES)
    def _():
        for rows in chunks:
            norm(xb_ref, rows)
            q, k = proj(rows, 0)
            q_ref[rows, :] = (roped(q, rows) * Q_SCALE).astype(BF)
            kt_ref[:, rows] = roped(k, rows).T.astype(BF)
        for rows in chunks:
            v, gb = proj(rows, 1)
            v_ref[rows, :] = v.astype(BF)
            gb_s[rows, :] = gb
        gated_conv(DEC_SEQ)

    @pl.when(i < CTX_TILES)
    def _():
        for c, rows in enumerate(chunks):
            norm(xa_ref, rows)
            q, k = proj(rows, 0)
            q_ref[rows, :] = (q * Q_SCALE).astype(BF)
            kt = k.T
            kt_ref[:, rows] = kt.astype(BF)
            for s in range(seqs_per_chunk):
                nk_ref[c * seqs_per_chunk + s] = kt[:, s * SEQ:(s + 1) * SEQ]
        for c, rows in enumerate(chunks):
            v, gb = proj(rows, 1)
            v_ref[rows, :] = v.astype(BF)
            gb_s[rows, :] = gb
            for s in range(seqs_per_chunk):
                for h in range(N_HEADS):
                    nv_ref[c * seqs_per_chunk + s, pl.ds(h, SEQ, stride=N_HEADS), :] = (
                        v[s * SEQ:(s + 1) * SEQ, h * V_DIM:(h + 1) * V_DIM])
        gated_conv(SEQ)


def _in_proj(layer, xs, mod, g_mix, w_in, cos_t, sin_t, conv_w, new_kv):
    pair = isinstance(xs, tuple)
    xa, xb = xs if pair else (xs, xs)
    spec_a, spec_b = _stream_specs(pair)
    ctx_i = lambda i: jnp.minimum(i, CTX_TILES - 1)
    in_specs = [
        spec_a, spec_b,
        WHOLE, WHOLE,
        pl.BlockSpec(memory_space=pl.ANY),
        WHOLE, WHOLE, WHOLE,
    ]
    args = [xa, xb, mod, g_mix, w_in, cos_t, sin_t, conv_w]
    aliases = {}
    if new_kv is not None:
        in_specs += [pl.BlockSpec(memory_space=pl.ANY), pl.BlockSpec(memory_space=pl.ANY)]
        args += list(new_kv)
        aliases = {8: 4, 9: 5}
    row_tile = pl.BlockSpec((TM, ATT_WIDTH), lambda i: (i, 0))
    return pl.pallas_call(
        functools.partial(_in_kernel, layer=layer, aliased=new_kv is not None),
        grid=(N_TILES,),
        in_specs=in_specs,
        out_specs=[
            row_tile,
            row_tile,
            pl.BlockSpec((None, ATT_WIDTH, TM), lambda i: (i, 0, 0)),
            row_tile,
            pl.BlockSpec((SEQ_PER_TILE, None, ATT_WIDTH, SEQ), lambda i: (ctx_i(i), layer, 0, 0)),
            pl.BlockSpec((SEQ_PER_TILE, None, SEQ * N_HEADS, V_DIM), lambda i: (ctx_i(i), layer, 0, 0)),
        ],
        out_shape=[
            jax.ShapeDtypeStruct((N_TOK, ATT_WIDTH), BF),
            jax.ShapeDtypeStruct((N_TOK, ATT_WIDTH), BF),
            jax.ShapeDtypeStruct((N_TILES, ATT_WIDTH, TM), BF),
            jax.ShapeDtypeStruct((N_TOK, CONV_WIDTH), BF),
            jax.ShapeDtypeStruct((BATCH, DEPTH, ATT_WIDTH, SEQ), F32),
            jax.ShapeDtypeStruct((BATCH, DEPTH, SEQ * N_HEADS, V_DIM), F32),
        ],
        scratch_shapes=[
            pltpu.VMEM((TM, D_MODEL), BF),
            pltpu.VMEM((N_IN_TILES, D_MODEL, TN_IN), BF),
            pltpu.VMEM((TM, CONV_WIDTH), F32),
            pltpu.VMEM((TM, CONV_WIDTH), F32),
            pltpu.VMEM((IN_STAGE_SLOTS, STAGE_W_ROWS, STAGE_W_COLS), F32),
            pltpu.SemaphoreType.DMA((IN_STAGE_SLOTS,)),
        ],
        input_output_aliases=aliases,
        compiler_params=_params(("arbitrary",)),
        name=f"in_proj_l{layer}",
    )(*args)


def _lambda(lq_ref, layer, lam_init):
    lq = lq_ref[layer]
    a = jnp.exp(jnp.sum(lq[0:1] * lq[1:2], axis=-1, keepdims=True))
    b = jnp.exp(jnp.sum(lq[2:3] * lq[3:4], axis=-1, keepdims=True))
    return a - b + lam_init


def _head_norm(o, sg, lam_init):
    return _rms(o) * sg * (1.0 - lam_init)


def _attn_ctx_kernel(q_ref, kt_ref, v_ref, lq_ref, sg_ref, o_ref, sc_s, *, layer, lam_init):
    lam = _lambda(lq_ref, layer, lam_init)
    sg = sg_ref[layer:layer + 1, :]

    def sequence(b, carry):
        pos = pl.ds(pl.multiple_of(b * SEQ, SEQ), SEQ)

        def scores(h):
            for s in range(2):
                d = slice(h * V_DIM + s * QK_DIM, h * V_DIM + (s + 1) * QK_DIM)
                sc_s[h % 2, s] = jnp.dot(q_ref[pos, d], kt_ref[d, pos], preferred_element_type=F32)

        def finish(h):
            cols = slice(h * V_DIM, (h + 1) * V_DIM)
            v = v_ref[pos, cols]
            outs = []
            for s in range(2):
                sc = sc_s[h % 2, s]
                e = jnp.exp2(sc - jnp.max(sc, axis=-1, keepdims=True))
                r = 1.0 / jnp.sum(e, axis=-1, keepdims=True)
                outs.append(jnp.dot(e.astype(BF), v, preferred_element_type=F32) * r)
            o = outs[0] - lam * outs[1]
            o_ref[pos, cols] = _head_norm(o, sg, lam_init).astype(BF)

        scores(0)
        for h in range(N_HEADS):
            if h + 1 < N_HEADS:
                scores(h + 1)
            finish(h)
        return carry

    lax.fori_loop(0, CTX_SEQ_PER_STEP, sequence, 0)


def _attn_ctx(layer, q, kt, v, lambda_qk, subln_g, lam_init):
    rows = CTX_SEQ_PER_STEP * SEQ
    per_tile = TM // rows
    return pl.pallas_call(
        functools.partial(_attn_ctx_kernel, layer=layer, lam_init=lam_init),
        grid=(N_CTX // rows,),
        in_specs=[
            pl.BlockSpec((rows, ATT_WIDTH), lambda b: (b, 0)),
            pl.BlockSpec((None, ATT_WIDTH, rows), lambda b: (b // per_tile, 0, b % per_tile)),
            pl.BlockSpec((rows, ATT_WIDTH), lambda b: (b, 0)),
            WHOLE, WHOLE,
        ],
        out_specs=pl.BlockSpec((rows, ATT_WIDTH), lambda b: (b, 0)),
        out_shape=jax.ShapeDtypeStruct((N_TOK, ATT_WIDTH), BF),
        scratch_shapes=[pltpu.VMEM((2, 2, SEQ, SEQ), F32)],
        compiler_params=_params(("arbitrary",)),
        name=f"attn_ctx_l{layer}",
    )(q, kt, v, lambda_qk, subln_g)


TQ = 256
LAT_Q_PER_STEP = 4


def _attn_lat_kernel(q_ref, kt_ref, v_ref, ckt_ref, cv_ref, lq_ref, sg_ref, att_in_ref, o_ref, sc_s, *,
                     layer, lam_init):
    del att_in_ref
    lam = _lambda(lq_ref, layer, lam_init)
    sg = sg_ref[layer:layer + 1, :]

    units = [(b, h) for b in range(LAT_Q_PER_STEP) for h in range(N_HEADS)]

    def scores(u, s):
        b, h = units[u]
        d = slice(h * V_DIM + s * QK_DIM, h * V_DIM + (s + 1) * QK_DIM)
        q = q_ref[b * TQ:(b + 1) * TQ, d]
        sc_s[u % 2, s, :, :PAST_LEN] = jnp.dot(q, ckt_ref[d, :].astype(BF), preferred_element_type=F32)
        sc_s[u % 2, s, :, PAST_LEN:] = jnp.dot(q, kt_ref[d, :], preferred_element_type=F32)

    def softmax(u, s):
        sc = sc_s[u % 2, s]
        e = jnp.exp2(sc - jnp.max(sc, axis=-1, keepdims=True))
        return e, 1.0 / jnp.sum(e, axis=-1, keepdims=True)

    def finish(u, p1, p2):
        b, h = units[u]
        cols = slice(h * V_DIM, (h + 1) * V_DIM)
        e = jnp.concatenate([p1[0].astype(BF), p2[0].astype(BF)], axis=0)
        vc = cv_ref[pl.ds(h, PAST_LEN, stride=N_HEADS), :].astype(BF)
        pv = jnp.dot(e[:, :PAST_LEN], vc, preferred_element_type=F32)
        pv = pv + jnp.dot(e[:, PAST_LEN:], v_ref[:, cols], preferred_element_type=F32)
        o = pv[:TQ] * p1[1] - pv[TQ:] * (lam * p2[1])
        o_ref[b * TQ:(b + 1) * TQ, cols] = _head_norm(o, sg, lam_init).astype(BF)

    scores(0, 0)
    scores(0, 1)
    for u in range(len(units)):
        more = u + 1 < len(units)
        if more:
            scores(u + 1, 0)
        p1 = softmax(u, 0)
        if more:
            scores(u + 1, 1)
        finish(u, p1, softmax(u, 1))


def _attn_lat(layer, q, kt, v, cache_kt, cache_v, lambda_qk, subln_g, att, lam_init):
    rows = LAT_Q_PER_STEP * TQ
    nqb = DEC_SEQ // rows
    q0 = N_CTX // rows
    return pl.pallas_call(
        functools.partial(_attn_lat_kernel, layer=layer, lam_init=lam_init),
        grid=(DEC_BATCH, nqb),
        in_specs=[
            pl.BlockSpec((rows, ATT_WIDTH), lambda b, t: (q0 + b * nqb + t, 0)),
            pl.BlockSpec((None, ATT_WIDTH, DEC_SEQ), lambda b, t: (CTX_TILES + b, 0, 0)),
            pl.BlockSpec((DEC_SEQ, ATT_WIDTH), lambda b, t: (CTX_TILES + b, 0)),
            pl.BlockSpec((None, None, ATT_WIDTH, PAST_LEN), lambda b, t: (b, layer, 0, 0)),
            pl.BlockSpec((None, None, PAST_LEN * N_HEADS, V_DIM), lambda b, t: (b, layer, 0, 0)),
            WHOLE, WHOLE,
            pl.BlockSpec(memory_space=pl.ANY),
        ],
        out_specs=pl.BlockSpec((rows, ATT_WIDTH), lambda b, t: (q0 + b * nqb + t, 0)),
        out_shape=jax.ShapeDtypeStruct((N_TOK, ATT_WIDTH), BF),
        scratch_shapes=[pltpu.VMEM((2, 2, TQ, PAST_LEN + DEC_SEQ), F32)],
        input_output_aliases={7: 0},
        compiler_params=_params(("arbitrary", "arbitrary")),
        name=f"attn_lat_l{layer}",
    )(q, kt, v, cache_kt, cache_v, lambda_qk, subln_g, att)


def _out_kernel(att_ref, conv_ref, w_ref, xa_ref, xb_ref, mod_ref, gf_ref, *rest, layer, route):
    if route:
        wr_ref, xo_ref, h2_ref, rt_ref, n_ref, wb_s = rest
    else:
        xo_ref, h2_ref, wb_s = rest
    i = pl.program_id(0)

    @pl.when(i == 0)
    def _():
        wb_s[...] = w_ref[...].astype(BF)

    lat = i >= CTX_TILES
    gain = gf_ref[layer:layer + 1, :] * (1.0 + _mod(mod_ref, layer, i, 4))
    for c in range(TM // ROW_CHUNK):
        rows = slice(c * ROW_CHUNK, (c + 1) * ROW_CHUNK)
        mo = jnp.dot(att_ref[rows, :], wb_s[:ATT_WIDTH, :], preferred_element_type=F32)
        mo = mo + jnp.dot(conv_ref[rows, :], wb_s[ATT_WIDTH:, :], preferred_element_type=F32)
        xn = jnp.where(lat, xb_ref[rows, :], xa_ref[rows, :]) + _mod(mod_ref, layer, i, 2) * mo
        xo_ref[rows, :] = xn
        h2_ref[rows, :] = (_rms(xn) * gain + _mod(mod_ref, layer, i, 3)).astype(BF)
    if route:
        _route(h2_ref[...], wr_ref, rt_ref, n_ref)


def _out_proj(layer, att, conv, w_out, xs, mod, g_ffn, w_router_pad=None):
    pair = isinstance(xs, tuple)
    xa, xb = xs if pair else (xs, xs)
    spec_a, spec_b = _stream_specs(pair)
    row_spec = pl.BlockSpec((TM, D_MODEL), lambda i: (i, 0))
    lane_spec = pl.BlockSpec((TM, LANES), lambda i: (i, 0))
    in_specs = [
        pl.BlockSpec((TM, ATT_WIDTH), lambda i: (i, 0)),
        pl.BlockSpec((TM, CONV_WIDTH), lambda i: (i, 0)),
        pl.BlockSpec((None, D_MODEL, D_MODEL), lambda i: (layer, 0, 0)),
        spec_a, spec_b,
        WHOLE, WHOLE,
    ]
    args = [att, conv, w_out, xa, xb, mod, g_ffn]
    out_specs = [row_spec, row_spec]
    out_shape = [jax.ShapeDtypeStruct((N_TOK, D_MODEL), F32), jax.ShapeDtypeStruct((N_TOK, D_MODEL), BF)]
    route = w_router_pad is not None
    if route:
        in_specs.append(WHOLE)
        args.append(w_router_pad)
        out_specs += [lane_spec, pl.BlockSpec((None, SUBLANES, LANES), lambda i: (i, 0, 0))]
        out_shape += [jax.ShapeDtypeStruct((N_TOK, LANES), F32),
                      jax.ShapeDtypeStruct((N_TILES, SUBLANES, LANES), jnp.int32)]
    return pl.pallas_call(
        functools.partial(_out_kernel, layer=layer, route=route),
        grid=(N_TILES,),
        in_specs=in_specs,
        out_specs=out_specs,
        out_shape=out_shape,
        scratch_shapes=[pltpu.VMEM((D_MODEL, D_MODEL), BF)],
        compiler_params=_params(("arbitrary",)),
        name=f"out_proj_l{layer}",
    )(*args)


def _weight_pieces(layer):
    pieces = []
    for c0 in range(0, D_MODEL, STAGE_W_COLS):
        cols = slice(c0, c0 + STAGE_W_COLS)
        pieces.append((0, layer, slice(0, D_MODEL), cols, None, slice(0, D_MODEL), cols))
    for half in range(FF_HALVES):
        for part in range(2):
            src0 = part * D_FF + half * FF_HALF
            for off in range(0, FF_HALF, STAGE_W_COLS):
                n = min(STAGE_W_COLS, FF_HALF - off)
                pieces.append((1, 0, slice(0, D_MODEL), slice(src0 + off, src0 + off + n),
                               half, slice(0, D_MODEL), slice(part * FF_HALF + off, part * FF_HALF + off + n)))
    for r0 in range(0, D_FF, STAGE_W_ROWS):
        rows = slice(r0, min(r0 + STAGE_W_ROWS, D_FF))
        for c0 in range(0, D_MODEL, STAGE_W_COLS):
            cols = slice(c0, c0 + STAGE_W_COLS)
            pieces.append((2, 0, rows, cols, None, rows, cols))
    return pieces


def _load_weights(pieces, hbm, resident, stage_s, sem):
    slots = stage_s.shape[0]

    def copy(k):
        src, idx, rows, cols, _, _, _ = pieces[k]
        nr, nc = rows.stop - rows.start, cols.stop - cols.start
        slot = k % slots
        return pltpu.make_async_copy(hbm[src].at[idx, rows, cols], stage_s.at[slot, :nr, :nc], sem.at[slot])

    for k in range(min(slots - 1, len(pieces))):
        copy(k).start()
    for k, (src, _, rows, cols, didx, drows, dcols) in enumerate(pieces):
        if k + slots - 1 < len(pieces):
            copy(k + slots - 1).start()
        copy(k).wait()
        nr, nc = rows.stop - rows.start, cols.stop - cols.start
        piece = stage_s[k % slots, :nr, :nc].astype(BF)
        if didx is None:
            resident[src][drows, dcols] = piece
        else:
            resident[src][didx, drows, dcols] = piece


def _dense_layer_kernel(att_ref, conv_ref, xa_ref, xb_ref, mod_ref, gf_ref, wo_hbm, wgu_hbm, wd_hbm,
                        o_ref, wo_b, wgu_b, wd_b, stage_s, sem, *, layer):
    i = pl.program_id(0)
    tile = i // (TM // FF_TM)

    @pl.when(i == 0)
    def _():
        _load_weights(_weight_pieces(layer), (wo_hbm, wgu_hbm, wd_hbm), (wo_b, wgu_b, wd_b), stage_s, sem)

    mo = jnp.dot(att_ref[...], wo_b[:ATT_WIDTH, :], preferred_element_type=F32)
    mo = mo + jnp.dot(conv_ref[...], wo_b[ATT_WIDTH:, :], preferred_element_type=F32)
    x = jnp.where(tile >= CTX_TILES, xb_ref[...], xa_ref[...])
    xn = x + _mod(mod_ref, layer, tile, 2) * mo
    h = (_rms(xn) * gf_ref[layer:layer + 1, :]) * (1.0 + _mod(mod_ref, layer, tile, 4)) + _mod(mod_ref, layer, tile, 3)
    h = h.astype(BF)
    y = None
    for half in range(FF_HALVES):
        gu = jnp.dot(h, wgu_b[half], preferred_element_type=F32)
        act = (_silu(gu[:, :FF_HALF]) * gu[:, FF_HALF:]).astype(BF)
        part = jnp.dot(act, wd_b[half * FF_HALF:(half + 1) * FF_HALF, :], preferred_element_type=F32)
        y = part if y is None else y + part
    o_ref[...] = xn + _mod(mod_ref, layer, tile, 5) * y


def _dense_layer(layer, att, conv, xs, mod, g_ffn, w_out, w_gu, w_down):
    xa, xb = xs
    n_ctx = N_CTX // FF_TM
    row = lambda width: pl.BlockSpec((FF_TM, width), lambda i: (i, 0))
    hbm = pl.BlockSpec(memory_space=pl.ANY)
    return pl.pallas_call(
        functools.partial(_dense_layer_kernel, layer=layer),
        grid=(N_TOK // FF_TM,),
        in_specs=[
            row(ATT_WIDTH), row(CONV_WIDTH),
            pl.BlockSpec((FF_TM, D_MODEL), lambda i: (jnp.minimum(i, n_ctx - 1), 0)),
            pl.BlockSpec((FF_TM, D_MODEL), lambda i: (jnp.maximum(i - n_ctx, 0), 0)),
            WHOLE, WHOLE, hbm, hbm, hbm,
        ],
        out_specs=row(D_MODEL),
        out_shape=jax.ShapeDtypeStruct((N_TOK, D_MODEL), F32),
        scratch_shapes=[
            pltpu.VMEM((D_MODEL, D_MODEL), BF),
            pltpu.VMEM((FF_HALVES, D_MODEL, 2 * FF_HALF), BF),
            pltpu.VMEM((D_FF, D_MODEL), BF),
            pltpu.VMEM((FF_STAGE_SLOTS, STAGE_W_ROWS, STAGE_W_COLS), F32),
            pltpu.SemaphoreType.DMA((FF_STAGE_SLOTS,)),
        ],
        compiler_params=_params(("arbitrary",)),
        name="dense_layer",
    )(att, conv, xa, xb, mod, g_ffn, w_out, w_gu, w_down)


def _route(h, wr_ref, rt_ref, n_ref):
    logits = jnp.dot(h, wr_ref[...].astype(BF), preferred_element_type=F32)
    lane = lax.broadcasted_iota(jnp.int32, logits.shape, 1)
    lg = jnp.where(lane < N_EXPERTS, logits, -jnp.inf)
    m1 = jnp.max(lg, axis=-1, keepdims=True)
    i1 = jnp.min(jnp.where(lg == m1, lane, LANES), axis=-1, keepdims=True)
    lg2 = jnp.where(lane == i1, -jnp.inf, lg)
    m2 = jnp.max(lg2, axis=-1, keepdims=True)
    i2 = jnp.min(jnp.where(lg2 == m2, lane, LANES), axis=-1, keepdims=True)
    e2 = jnp.exp(m2 - m1)
    w1 = 1.0 / (1.0 + e2)
    w2 = e2 / (1.0 + e2)

    sel1 = lane == i1
    sel2 = lane == i2
    onehot = jnp.logical_or(sel1, sel2)
    rows = lax.broadcasted_iota(jnp.int32, (ST, ST), 0)
    colsi = lax.broadcasted_iota(jnp.int32, (ST, ST), 1)
    earlier = (colsi < rows).astype(BF)
    onehot_b = onehot.astype(BF)
    before = jnp.concatenate(
        [jnp.dot(earlier, onehot_b[s * ST:(s + 1) * ST], preferred_element_type=F32) for s in range(SUB_PER_TILE)],
        axis=0)
    onehot_f = onehot.astype(F32)
    counts = [jnp.sum(onehot_f[s * ST:(s + 1) * ST], axis=0, keepdims=True) for s in range(SUB_PER_TILE)]
    counts = jnp.concatenate(counts + [jnp.zeros((SUBLANES - SUB_PER_TILE, LANES), F32)], axis=0).astype(jnp.int32)
    seg_len = ((counts + (SEG_ALIGN - 1)) // SEG_ALIGN) * SEG_ALIGN
    n_ref[...] = seg_len
    la = lax.broadcasted_iota(jnp.int32, (LANES, LANES), 0)
    lb = lax.broadcasted_iota(jnp.int32, (LANES, LANES), 1)
    seg_start = jnp.dot(seg_len.astype(F32).astype(BF), (la < lb).astype(BF), preferred_element_type=F32)
    start = jnp.concatenate(
        [jnp.broadcast_to(seg_start[s:s + 1], (ST, LANES)) for s in range(SUB_PER_TILE)], axis=0)
    where = before + start
    lp1 = jnp.sum(jnp.where(sel1, where, 0.0), axis=-1, keepdims=True)
    lp2 = jnp.sum(jnp.where(sel2, where, 0.0), axis=-1, keepdims=True)
    rt = jnp.where(lane == GATE_LANE, w1, jnp.where(lane == GATE_LANE + 1, w2, 0.0))
    rt_ref[...] = jnp.where(lane == ROW_LANE, lp1, jnp.where(lane == ROW_LANE + 1, lp2, rt))


def _chunk_copies(s, cnt_ref, cdst_ref, stage, rows_hbm, sem, *, to_hbm, wait):
    def copy(v, h):
        return pltpu.make_async_copy(v, h, sem) if to_hbm else pltpu.make_async_copy(h, v, sem)

    if wait:
        for z in WAIT_PIECES:
            @pl.when((cnt_ref[s] & z) != 0)
            def _():
                copy(stage.at[pl.ds(0, z * SEG_ALIGN)], rows_hbm.at[pl.ds(0, z * SEG_ALIGN)]).wait()
        return

    def start(c, priority):
        v = stage.at[pl.ds(pl.multiple_of(c * SEG_ALIGN, SEG_ALIGN), SEG_ALIGN)]
        h = rows_hbm.at[pl.ds(pl.multiple_of(cdst_ref[s * STAGE_CHUNKS + c], SEG_ALIGN), SEG_ALIGN)]
        copy(v, h).start(priority=priority)

    def pair(p, carry):
        start(2 * p, 0)
        start(2 * p + 1, 1)
        return carry

    n = cnt_ref[s]
    lax.fori_loop(0, n // 2, pair, 0)

    @pl.when((n & 1) != 0)
    def _():
        start(n - 1, 0)


def _dispatch_kernel(cnt_ref, cdst_ref, h_ref, rt_ref, xs_ref, stage_s, sem):
    copies = functools.partial(_chunk_copies, cnt_ref=cnt_ref, cdst_ref=cdst_ref, rows_hbm=xs_ref, to_hbm=True)
    for k in range(SUB_PER_TILE):
        s = pl.program_id(0) * SUB_PER_TILE + k
        slot = k % 2
        rows = slice(k * ST, (k + 1) * ST)

        @pl.when(s >= 2)
        def _():
            copies(s - 2, stage=stage_s.at[slot], sem=sem.at[slot], wait=True)

        at = rt_ref[rows, :].T[ROW_LANE:ROW_LANE + 2, :].astype(jnp.int32)
        r = lax.broadcasted_iota(jnp.int32, (STAGE_ROWS, ST), 0)
        pick = jnp.logical_or(r == at[0:1, :], r == at[1:2, :]).astype(BF)
        stage_s[slot] = jnp.dot(pick, h_ref[rows, :], preferred_element_type=F32).astype(BF)
        copies(s, stage=stage_s.at[slot], sem=sem.at[slot], wait=False)

    @pl.when(pl.program_id(0) == N_TILES - 1)
    def _():
        copies(N_SUB - 2, stage=stage_s.at[0], sem=sem.at[0], wait=True)
        copies(N_SUB - 1, stage=stage_s.at[1], sem=sem.at[1], wait=True)


def _dispatch(cnt, cdst, h2, rt):
    assert SUB_PER_TILE % 2 == 0
    grid_spec = pltpu.PrefetchScalarGridSpec(
        num_scalar_prefetch=2,
        grid=(N_TILES,),
        in_specs=[
            pl.BlockSpec((TM, D_MODEL), lambda i, *_: (i, 0)),
            pl.BlockSpec((TM, LANES), lambda i, *_: (i, 0)),
        ],
        out_specs=pl.BlockSpec(memory_space=pl.ANY),
        scratch_shapes=[pltpu.VMEM((2, STAGE_ROWS, D_MODEL), BF), pltpu.SemaphoreType.DMA((2,))],
    )
    return pl.pallas_call(
        _dispatch_kernel,
        grid_spec=grid_spec,
        out_shape=jax.ShapeDtypeStruct((R_PAD, D_MODEL), BF),
        compiler_params=_params(("arbitrary",)),
        name="moe_dispatch",
    )(cnt, cdst, h2, rt)


def _expert_weights(r, te_ref, nt_ref, nxt_ref, w_hbm, wf_s, wb_s, sem):
    def fetch(e):
        return pltpu.make_async_copy(w_hbm.at[0, e], wf_s, sem)

    @pl.when(r == 0)
    def _():
        fetch(te_ref[0]).start()

    first = jnp.logical_or(r == 0, te_ref[r] != te_ref[jnp.maximum(r - 1, 0)])

    @pl.when(jnp.logical_and(r < nt_ref[0], first))
    def _():
        fetch(te_ref[r]).wait()
        wb_s[...] = wf_s[...].astype(BF)

        @pl.when(nxt_ref[r] >= 0)
        def _():
            fetch(nxt_ref[r]).start()


def _moe_ffn_kernel(te_ref, nt_ref, nxt_ref, used_ref, x_ref, wgu_hbm, wd_hbm, o_ref,
                    wgu_f, wgu_b, wd_f, wd_b, sem):
    def ffn(rows):
        gu = jnp.dot(x_ref[rows, :], wgu_b[...], preferred_element_type=F32)
        act = (_silu(gu[:, :D_FF_EXPERT]) * gu[:, D_FF_EXPERT:]).astype(BF)
        o_ref[rows, :] = jnp.dot(act, wd_b[...], preferred_element_type=F32).astype(BF)

    for k in range(TILES_PER_STEP):
        r = pl.program_id(0) * TILES_PER_STEP + k
        _expert_weights(r, te_ref, nt_ref, nxt_ref, wgu_hbm, wgu_f, wgu_b, sem.at[0])
        _expert_weights(r, te_ref, nt_ref, nxt_ref, wd_hbm, wd_f, wd_b, sem.at[1])

        @pl.when(jnp.logical_and(r < nt_ref[0], used_ref[r] > TG // 2))
        def _():
            ffn(slice(k * TG, (k + 1) * TG))

        @pl.when(jnp.logical_and(r < nt_ref[0], used_ref[r] <= TG // 2))
        def _():
            ffn(slice(k * TG, k * TG + TG // 2))


def _moe_ffn(te, nt, nxt, used, rows, w_gu, w_down):
    tile_map = lambda i, te, nt, nxt, used: (jnp.minimum(i, (nt[0] - 1) // TILES_PER_STEP), 0)
    grid_spec = pltpu.PrefetchScalarGridSpec(
        num_scalar_prefetch=4,
        grid=(NT_G // TILES_PER_STEP,),
        in_specs=[pl.BlockSpec((TILES_PER_STEP * TG, D_MODEL), tile_map),
                  pl.BlockSpec(memory_space=pl.ANY), pl.BlockSpec(memory_space=pl.ANY)],
        out_specs=pl.BlockSpec((TILES_PER_STEP * TG, D_MODEL), tile_map),
        scratch_shapes=[
            pltpu.VMEM((D_MODEL, 2 * D_FF_EXPERT), F32), pltpu.VMEM((D_MODEL, 2 * D_FF_EXPERT), BF),
            pltpu.VMEM((D_FF_EXPERT, D_MODEL), F32), pltpu.VMEM((D_FF_EXPERT, D_MODEL), BF),
            pltpu.SemaphoreType.DMA((2,)),
        ],
    )
    return pl.pallas_call(
        _moe_ffn_kernel,
        grid_spec=grid_spec,
        out_shape=jax.ShapeDtypeStruct((R_PAD, D_MODEL), BF),
        compiler_params=_params(("arbitrary",)),
        name="moe_ffn",
    )(te, nt, nxt, used, rows, w_gu, w_down)


def _combine_kernel(cnt_ref, cdst_ref, ys_ref, rt_ref, x_ref, mod_ref, fg_ref,
                    oa_ref, ob_ref, stage_s, sem, *, layer):
    i = pl.program_id(0)
    copies = functools.partial(_chunk_copies, cnt_ref=cnt_ref, cdst_ref=cdst_ref, rows_hbm=ys_ref, to_hbm=False)

    @pl.when(i == 0)
    def _():
        stage_s[...] = jnp.zeros_like(stage_s)
        for s in range(COMBINE_AHEAD):
            copies(s, stage=stage_s.at[s], sem=sem.at[s], wait=False)

    for k in range(SUB_PER_TILE):
        s = i * SUB_PER_TILE + k
        slot = k % COMBINE_SLOTS
        ahead = (k + COMBINE_AHEAD) % COMBINE_SLOTS
        rows = slice(k * ST, (k + 1) * ST)

        @pl.when(s + COMBINE_AHEAD < N_SUB)
        def _():
            copies(s + COMBINE_AHEAD, stage=stage_s.at[ahead], sem=sem.at[ahead], wait=False)

        copies(s, stage=stage_s.at[slot], sem=sem.at[slot], wait=True)

        rt = rt_ref[rows, :]
        at = rt[:, ROW_LANE:ROW_LANE + 2].astype(jnp.int32)
        r = lax.broadcasted_iota(jnp.int32, (ST, STAGE_ROWS), 1)
        staged = stage_s[slot]
        a = jnp.dot((r == at[:, 0:1]).astype(BF), staged, preferred_element_type=F32)
        b = jnp.dot((r == at[:, 1:2]).astype(BF), staged, preferred_element_type=F32)
        y = rt[:, GATE_LANE:GATE_LANE + 1] * a + rt[:, GATE_LANE + 1:GATE_LANE + 2] * b
        xn = x_ref[rows, :] + _mod(mod_ref, layer, i, 5) * y
        out = _rms(xn) * fg_ref[...]

        @pl.when(i < CTX_TILES)
        def _():
            oa_ref[rows, :] = out

        @pl.when(i >= CTX_TILES)
        def _():
            ob_ref[rows, :] = out


def _combine(layer, cnt, cdst, ys, rt, x, mod, final_g):
    assert SUB_PER_TILE % COMBINE_SLOTS == 0
    grid_spec = pltpu.PrefetchScalarGridSpec(
        num_scalar_prefetch=2,
        grid=(N_TILES,),
        in_specs=[
            pl.BlockSpec(memory_space=pl.ANY),
            pl.BlockSpec((TM, LANES), lambda i, *_: (i, 0)),
            pl.BlockSpec((TM, D_MODEL), lambda i, *_: (i, 0)),
            WHOLE, WHOLE,
        ],
        out_specs=[
            pl.BlockSpec((TM, D_MODEL), lambda i, *_: (jnp.minimum(i, CTX_TILES - 1), 0)),
            pl.BlockSpec((TM, D_MODEL), lambda i, *_: (jnp.maximum(i - CTX_TILES, 0), 0)),
        ],
        scratch_shapes=[pltpu.VMEM((COMBINE_SLOTS, STAGE_ROWS, D_MODEL), BF),
                        pltpu.SemaphoreType.DMA((COMBINE_SLOTS,))],
    )
    return pl.pallas_call(
        functools.partial(_combine_kernel, layer=layer),
        grid_spec=grid_spec,
        out_shape=[
            jax.ShapeDtypeStruct((N_CTX, D_MODEL), F32),
            jax.ShapeDtypeStruct((N_LAT, D_MODEL), F32),
        ],
        compiler_params=_params(("arbitrary",)),
        name="moe_combine",
    )(cnt, cdst, ys, rt, x, mod, final_g)


def _group_layout(n_tiles):
    n = n_tiles[:, :SUB_PER_TILE, :N_EXPERTS].reshape(N_SUB, N_EXPERTS)
    tiles = (jnp.sum(n, axis=0) + TG - 1) // TG
    tile_end = jnp.cumsum(tiles)
    region = (tile_end - tiles) * TG
    dst = region[None, :] + jnp.cumsum(n, axis=0) - n
    seg_end = jnp.cumsum(n, axis=1)
    seg = seg_end - n
    row = jnp.arange(STAGE_CHUNKS, dtype=jnp.int32) * SEG_ALIGN
    owner = jnp.sum((row[None, :, None] >= seg_end[:, None, :]).astype(jnp.int32), axis=-1)
    own = jnp.minimum(owner, N_EXPERTS - 1)[..., None] == jnp.arange(N_EXPERTS)
    cdst = jnp.sum(jnp.where(own, (dst - seg)[:, None, :], 0), axis=-1) + row[None, :]
    cnt = seg_end[:, -1] // SEG_ALIGN
    nt = tile_end[-1]
    tile_id = jnp.minimum(jnp.arange(NT_G, dtype=jnp.int32), nt - 1)
    te = jnp.sum((tile_id[:, None] >= tile_end[None, :]).astype(jnp.int32), axis=-1)
    after = jnp.sum(jnp.where(te[:, None] == jnp.arange(N_EXPERTS), tile_end[None, :], 0), axis=-1)
    nxt = jnp.where(after < nt, jnp.sum((after[:, None] >= tile_end[None, :]).astype(jnp.int32), axis=-1), -1)
    mine = te[:, None] == jnp.arange(N_EXPERTS)
    region_end = jnp.sum(jnp.where(mine, (region + jnp.sum(n, axis=0))[None, :], 0), axis=-1)
    used = jnp.clip(region_end - tile_id * TG, 0, TG)
    i32 = lambda a: a.astype(jnp.int32)
    return (i32(cnt), i32(cdst.reshape(N_SUB * STAGE_CHUNKS)), i32(te), i32(nt.reshape(1)), i32(nxt), i32(used))


def _rope_tables():
    p = np.arange(DEC_SEQ)
    row = (p // GRID_W).astype(np.float32)
    col = (p % GRID_W).astype(np.float32)
    half = QK_DIM // 4
    freqs = (ROPE_BASE ** (-np.arange(half, dtype=np.float32) / half)).astype(np.float32)
    lane = np.arange(V_DIM)
    f = freqs[lane & (half - 1)]
    use_col = (lane & (2 * half)) != 0
    ang = (np.where(use_col[None, :], col[:, None], row[:, None]) * f[None, :]).astype(np.float32)
    upper = (lane & half) != 0
    sin = np.sin(ang)
    return jnp.asarray(np.cos(ang), F32), jnp.asarray(np.where(upper[None, :], sin, -sin), F32)


def kernel(x_prompt, x_sample, cache_k, cache_v, c, c_ctx, w_ada, b_ada, norm_mix_g, norm_ffn_g,
           w_in, lambda_qk, subln_g, conv_w, w_out, w_gu_dense, w_down_dense, w_router,
           w_gu_moe, w_down_moe, final_g):
    assert DEPTH == 2
    xs = (x_prompt.reshape(N_CTX, D_MODEL), x_sample.reshape(N_LAT, D_MODEL))
    mod = _ada(c_ctx, c, w_ada, b_ada)
    cos_t, sin_t = _rope_tables()
    cache_kt = jnp.transpose(cache_k, (0, 1, 3, 4, 5, 2)).reshape(DEC_BATCH, DEPTH, ATT_WIDTH, PAST_LEN)
    cache_v4 = cache_v.reshape(DEC_BATCH, DEPTH, PAST_LEN * N_HEADS, V_DIM)

    new_kv = None
    for layer in range(DEPTH):
        lam_init = 0.8 - 0.6 * math.exp(-0.3 * layer)
        q, v, kt, conv, nk, nv = _in_proj(layer, xs, mod, norm_mix_g, w_in, cos_t, sin_t, conv_w, new_kv)
        new_kv = (nk, nv)
        att = _attn_ctx(layer, q, kt, v, lambda_qk, subln_g, lam_init)
        att = _attn_lat(layer, q, kt, v, cache_kt, cache_v4, lambda_qk, subln_g, att, lam_init)
        if layer == 0:
            xs = _dense_layer(layer, att, conv, xs, mod, norm_ffn_g, w_out, w_gu_dense, w_down_dense)
        else:
            wr = jnp.pad(w_router[0], ((0, 0), (0, LANES - N_EXPERTS)))
            x1, h2, rt, n_tiles = _out_proj(layer, att, conv, w_out, xs, mod, norm_ffn_g, wr)
            cnt, cdst, te, nt, nxt, used = _group_layout(n_tiles)
            xsort = _dispatch(cnt, cdst, h2, rt)
            ys = _moe_ffn(te, nt, nxt, used, xsort, w_gu_moe, w_down_moe)
            y_ctx, y_lat = _combine(layer, cnt, cdst, ys, rt, x1, mod, final_g.reshape(1, D_MODEL))
    nk, nv = new_kv
    new_k = jnp.transpose(nk.reshape(BATCH, DEPTH, N_HEADS, 2, QK_DIM, SEQ), (0, 1, 5, 2, 3, 4))
    new_v = nv.reshape(BATCH, DEPTH, SEQ, N_HEADS, V_DIM)
    return (y_ctx.reshape(BATCH, SEQ, D_MODEL), y_lat.reshape(DEC_BATCH, DEC_SEQ, D_MODEL), new_k, new_v)
```

```python
import functools
import math

import numpy as np
import jax
import jax.numpy as jnp
from jax import lax
from jax.experimental import pallas as pl
from jax.experimental.pallas import tpu as pltpu

D_MODEL = 1024
BATCH = 16
SEQ = 256
DEPTH = 2
DEC_BATCH = 4
DEC_SEQ = 1024
PAST_LEN = 512
GRID_W = 64
ATT_WIDTH = 512
CONV_WIDTH = 512
N_HEADS = 4
V_DIM = 128
QK_DIM = 64
ROPE_BASE = 10000.0
D_FF = 2816
N_EXPERTS = 8
D_FF_EXPERT = 1408
N_MOD = 6
NORM_EPS = 1e-6
Q_SCALE = QK_DIM ** -0.5 * math.log2(math.e)
IN_COLS = 3 * ATT_WIDTH + 3 * CONV_WIDTH

N_CTX = BATCH * SEQ
N_LAT = DEC_BATCH * DEC_SEQ
N_TOK = N_CTX + N_LAT
TM = 1024
N_TILES = N_TOK // TM
CTX_TILES = N_CTX // TM
SEQ_PER_TILE = TM // SEQ
CTX_SEQ_PER_STEP = 4
COND_ROWS = 8
TN_IN = 1024
N_IN_TILES = IN_COLS // TN_IN
ROW_CHUNK = 512
FF_TM = 512
FF_HALVES = 2
FF_HALF = D_FF // FF_HALVES
STAGE_W_ROWS = 1024
STAGE_W_COLS = 512
FF_STAGE_SLOTS = 8
IN_STAGE_SLOTS = 4
TN_ADA = 1536
TG = 512
ST = 256
SUB_PER_TILE = TM // ST
N_SUB = N_TOK // ST
SEG_ALIGN = 8
STAGE_ROWS = 640
STAGE_CHUNKS = STAGE_ROWS // SEG_ALIGN
WAIT_PIECES = (64, 32, 16, 8, 4, 2, 1)
DISPATCH_IN_BUFFERS = 3
COMBINE_SLOTS = 4
COMBINE_AHEAD = 3
TILES_PER_STEP = 2
_MAX_SORTED_ROWS = 2 * N_TOK + N_SUB * N_EXPERTS * (SEG_ALIGN - 1) + N_EXPERTS * (TG - SEG_ALIGN)
NT_G = -(-_MAX_SORTED_ROWS // (TG * TILES_PER_STEP)) * TILES_PER_STEP
R_PAD = NT_G * TG
LANES = 128
SUBLANES = 8
GATE_LANE = 0
ROW_LANE = 2
VMEM_LIMIT = 60 * 1024 * 1024

BF = jnp.bfloat16
F32 = jnp.float32


def _params(sem, vmem=VMEM_LIMIT):
    return pltpu.CompilerParams(dimension_semantics=sem, vmem_limit_bytes=vmem)


def _mod_row(i):
    return jnp.where(i < CTX_TILES, 0, i - (CTX_TILES - 1))


WHOLE = pl.BlockSpec(memory_space=pltpu.VMEM)


def _mod(mod_ref, layer, i, c):
    return mod_ref[layer, pl.ds(_mod_row(i), 1), c * D_MODEL:(c + 1) * D_MODEL]


def _stream_specs(pair, width=D_MODEL):
    a = pl.BlockSpec((TM, width), lambda i, *_: (jnp.minimum(i, CTX_TILES - 1), 0))
    if pair:
        b = pl.BlockSpec((TM, width), lambda i, *_: (jnp.maximum(i - CTX_TILES, 0), 0))
    else:
        b = pl.BlockSpec((TM, width), lambda i, *_: (jnp.maximum(i, CTX_TILES), 0))
    return a, b


def _silu(x):
    return x / (1.0 + jnp.exp(-x))


def _rms(x):
    return x * lax.rsqrt(jnp.mean(x * x, axis=-1, keepdims=True) + NORM_EPS)


def _ada_kernel(cc_ref, c_ref, w_ref, b_ref, o_ref):
    pad = jnp.zeros((COND_ROWS - 1 - DEC_BATCH, D_MODEL), F32)
    cond = jnp.concatenate([cc_ref[...], c_ref[...], pad], axis=0)
    s = _silu(cond).astype(BF)
    bias = b_ref[pl.ds(pl.program_id(0), 1), :]
    o_ref[...] = jnp.dot(s, w_ref[...].astype(BF), preferred_element_type=F32) + bias


def _ada(c_ctx, c, w_ada, b_ada):
    n = N_MOD * D_MODEL
    return pl.pallas_call(
        _ada_kernel,
        grid=(DEPTH, n // TN_ADA),
        in_specs=[
            WHOLE, WHOLE,
            pl.BlockSpec((None, D_MODEL, TN_ADA), lambda l, j: (l, 0, j)),
            pl.BlockSpec((DEPTH, TN_ADA), lambda l, j: (0, j)),
        ],
        out_specs=pl.BlockSpec((None, COND_ROWS, TN_ADA), lambda l, j: (l, 0, j)),
        out_shape=jax.ShapeDtypeStruct((DEPTH, COND_ROWS, n), F32),
        compiler_params=_params(("arbitrary", "arbitrary")),
        name="ada_mod",
    )(c_ctx.reshape(1, D_MODEL), c, w_ada, b_ada)


def _in_weight_pieces(layer):
    pieces = []
    for c0 in range(0, IN_COLS, STAGE_W_COLS):
        g, off = divmod(c0, TN_IN)
        pieces.append((0, layer, slice(0, D_MODEL), slice(c0, c0 + STAGE_W_COLS),
                       g, slice(0, D_MODEL), slice(off, off + STAGE_W_COLS)))
    return pieces


def _in_kernel(*refs, layer, aliased):
    xa_ref, xb_ref, mod_ref, g_ref, w_hbm, cos_ref, sin_ref, cw_ref = refs[:8]
    refs = refs[10:] if aliased else refs[8:]
    q_ref, v_ref, kt_ref, conv_ref, nk_ref, nv_ref, h_s, wb_s, gb_s, gc_s, stage_s, sem = refs
    i = pl.program_id(0)

    @pl.when(i == 0)
    def _():
        _load_weights(_in_weight_pieces(layer), (w_hbm,), (wb_s,), stage_s, sem)

    chunks = [slice(c * ROW_CHUNK, (c + 1) * ROW_CHUNK) for c in range(TM // ROW_CHUNK)]
    seqs_per_chunk = ROW_CHUNK // SEQ

    def norm(x_ref, rows):
        gain = g_ref[layer:layer + 1, :] * (1.0 + _mod(mod_ref, layer, i, 1))
        h_s[rows, :] = (_rms(x_ref[rows, :]) * gain + _mod(mod_ref, layer, i, 0)).astype(BF)

    def proj(rows, group):
        acc = jnp.dot(h_s[rows, :], wb_s[group], preferred_element_type=F32)
        return acc[:, :ATT_WIDTH], acc[:, ATT_WIDTH:]

    def roped(a, rows):
        cos = jnp.concatenate([cos_ref[rows, :]] * N_HEADS, axis=1)
        sin = jnp.concatenate([sin_ref[rows, :]] * N_HEADS, axis=1)
        lane = lax.broadcasted_iota(jnp.int32, a.shape, 1)
        upper = (lane & (QK_DIM // 4)) != 0
        partner = jnp.where(upper, pltpu.roll(a, QK_DIM // 4, 1), pltpu.roll(a, ATT_WIDTH - QK_DIM // 4, 1))
        return a * cos + partner * sin

    def gated_conv(seq):
        for rows in chunks:
            gc, xin = proj(rows, 2)
            gc_s[rows, :] = gc * xin
        u = gc_s[...]
        pos = lax.broadcasted_iota(jnp.int32, (TM, 1), 0) & (seq - 1)
        prev = jnp.where(pos == 0, 0.0, pltpu.roll(u, 1, 0))
        nxt = jnp.where(pos == seq - 1, 0.0, pltpu.roll(u, TM - 1, 0))
        cw = cw_ref[layer]
        conv = prev * cw[0:1] + u * cw[1:2] + nxt * cw[2:3]
        conv_ref[...] = (gb_s[...] * conv).astype(BF)

    @pl.when(i >= CTX_TILES)
    def _():
        for rows in chunks:
            norm(xb_ref, rows)
            q, k = proj(rows, 0)
            q_ref[rows, :] = (roped(q, rows) * Q_SCALE).astype(BF)
            kt_ref[:, rows] = roped(k, rows).T.astype(BF)
        for rows in chunks:
            v, gb = proj(rows, 1)
            v_ref[rows, :] = v.astype(BF)
            gb_s[rows, :] = gb
        gated_conv(DEC_SEQ)

    @pl.when(i < CTX_TILES)
    def _():
        for c, rows in enumerate(chunks):
            norm(xa_ref, rows)
            q, k = proj(rows, 0)
            q_ref[rows, :] = (q * Q_SCALE).astype(BF)
            kt = k.T
            kt_ref[:, rows] = kt.astype(BF)
            for s in range(seqs_per_chunk):
                nk_ref[c * seqs_per_chunk + s] = kt[:, s * SEQ:(s + 1) * SEQ]
        for c, rows in enumerate(chunks):
            v, gb = proj(rows, 1)
            v_ref[rows, :] = v.astype(BF)
            gb_s[rows, :] = gb
            for s in range(seqs_per_chunk):
                for h in range(N_HEADS):
                    nv_ref[c * seqs_per_chunk + s, pl.ds(h, SEQ, stride=N_HEADS), :] = (
                        v[s * SEQ:(s + 1) * SEQ, h * V_DIM:(h + 1) * V_DIM])
        gated_conv(SEQ)


def _in_proj(layer, xs, mod, g_mix, w_in, cos_t, sin_t, conv_w, new_kv):
    pair = isinstance(xs, tuple)
    xa, xb = xs if pair else (xs, xs)
    spec_a, spec_b = _stream_specs(pair)
    ctx_i = lambda i: jnp.minimum(i, CTX_TILES - 1)
    in_specs = [
        spec_a, spec_b,
        WHOLE, WHOLE,
        pl.BlockSpec(memory_space=pl.ANY),
        WHOLE, WHOLE, WHOLE,
    ]
    args = [xa, xb, mod, g_mix, w_in, cos_t, sin_t, conv_w]
    aliases = {}
    if new_kv is not None:
        in_specs += [pl.BlockSpec(memory_space=pl.ANY), pl.BlockSpec(memory_space=pl.ANY)]
        args += list(new_kv)
        aliases = {8: 4, 9: 5}
    row_tile = pl.BlockSpec((TM, ATT_WIDTH), lambda i: (i, 0))
    return pl.pallas_call(
        functools.partial(_in_kernel, layer=layer, aliased=new_kv is not None),
        grid=(N_TILES,),
        in_specs=in_specs,
        out_specs=[
            row_tile,
            row_tile,
            pl.BlockSpec((None, ATT_WIDTH, TM), lambda i: (i, 0, 0)),
            row_tile,
            pl.BlockSpec((SEQ_PER_TILE, None, ATT_WIDTH, SEQ), lambda i: (ctx_i(i), layer, 0, 0)),
            pl.BlockSpec((SEQ_PER_TILE, None, SEQ * N_HEADS, V_DIM), lambda i: (ctx_i(i), layer, 0, 0)),
        ],
        out_shape=[
            jax.ShapeDtypeStruct((N_TOK, ATT_WIDTH), BF),
            jax.ShapeDtypeStruct((N_TOK, ATT_WIDTH), BF),
            jax.ShapeDtypeStruct((N_TILES, ATT_WIDTH, TM), BF),
            jax.ShapeDtypeStruct((N_TOK, CONV_WIDTH), BF),
            jax.ShapeDtypeStruct((BATCH, DEPTH, ATT_WIDTH, SEQ), F32),
            jax.ShapeDtypeStruct((BATCH, DEPTH, SEQ * N_HEADS, V_DIM), F32),
        ],
        scratch_shapes=[
            pltpu.VMEM((TM, D_MODEL), BF),
            pltpu.VMEM((N_IN_TILES, D_MODEL, TN_IN), BF),
            pltpu.VMEM((TM, CONV_WIDTH), F32),
            pltpu.VMEM((TM, CONV_WIDTH), F32),
            pltpu.VMEM((IN_STAGE_SLOTS, STAGE_W_ROWS, STAGE_W_COLS), F32),
            pltpu.SemaphoreType.DMA((IN_STAGE_SLOTS,)),
        ],
        input_output_aliases=aliases,
        compiler_params=_params(("arbitrary",)),
        name=f"in_proj_l{layer}",
    )(*args)


def _lambda(lq_ref, layer, lam_init):
    lq = lq_ref[layer]
    a = jnp.exp(jnp.sum(lq[0:1] * lq[1:2], axis=-1, keepdims=True))
    b = jnp.exp(jnp.sum(lq[2:3] * lq[3:4], axis=-1, keepdims=True))
    return a - b + lam_init


def _head_norm(o, sg, lam_init):
    return _rms(o) * sg * (1.0 - lam_init)


def _attn_ctx_kernel(q_ref, kt_ref, v_ref, lq_ref, sg_ref, o_ref, sc_s, *, layer, lam_init):
    lam = _lambda(lq_ref, layer, lam_init)
    sg = sg_ref[layer:layer + 1, :]

    def sequence(b, carry):
        pos = pl.ds(pl.multiple_of(b * SEQ, SEQ), SEQ)

        def scores(h):
            for s in range(2):
                d = slice(h * V_DIM + s * QK_DIM, h * V_DIM + (s + 1) * QK_DIM)
                sc_s[h % 2, s] = jnp.dot(q_ref[pos, d], kt_ref[d, pos], preferred_element_type=F32)

        def finish(h):
            cols = slice(h * V_DIM, (h + 1) * V_DIM)
            v = v_ref[pos, cols]
            outs = []
            for s in range(2):
                sc = sc_s[h % 2, s]
                e = jnp.exp2(sc - jnp.max(sc, axis=-1, keepdims=True))
                r = 1.0 / jnp.sum(e, axis=-1, keepdims=True)
                outs.append(jnp.dot(e.astype(BF), v, preferred_element_type=F32) * r)
            o = outs[0] - lam * outs[1]
            o_ref[pos, cols] = _head_norm(o, sg, lam_init).astype(BF)

        scores(0)
        for h in range(N_HEADS):
            if h + 1 < N_HEADS:
                scores(h + 1)
            finish(h)
        return carry

    lax.fori_loop(0, CTX_SEQ_PER_STEP, sequence, 0)


def _attn_ctx(layer, q, kt, v, lambda_qk, subln_g, lam_init):
    rows = CTX_SEQ_PER_STEP * SEQ
    per_tile = TM // rows
    return pl.pallas_call(
        functools.partial(_attn_ctx_kernel, layer=layer, lam_init=lam_init),
        grid=(N_CTX // rows,),
        in_specs=[
            pl.BlockSpec((rows, ATT_WIDTH), lambda b: (b, 0)),
            pl.BlockSpec((None, ATT_WIDTH, rows), lambda b: (b // per_tile, 0, b % per_tile)),
            pl.BlockSpec((rows, ATT_WIDTH), lambda b: (b, 0)),
            WHOLE, WHOLE,
        ],
        out_specs=pl.BlockSpec((rows, ATT_WIDTH), lambda b: (b, 0)),
        out_shape=jax.ShapeDtypeStruct((N_TOK, ATT_WIDTH), BF),
        scratch_shapes=[pltpu.VMEM((2, 2, SEQ, SEQ), F32)],
        compiler_params=_params(("arbitrary",)),
        name=f"attn_ctx_l{layer}",
    )(q, kt, v, lambda_qk, subln_g)


TQ = 256
LAT_Q_PER_STEP = 4


def _attn_lat_kernel(q_ref, kt_ref, v_ref, ckt_ref, cv_ref, lq_ref, sg_ref, att_in_ref, o_ref, sc_s, *,
                     layer, lam_init):
    del att_in_ref
    lam = _lambda(lq_ref, layer, lam_init)
    sg = sg_ref[layer:layer + 1, :]

    units = [(b, h) for b in range(LAT_Q_PER_STEP) for h in range(N_HEADS)]

    def scores(u, s):
        b, h = units[u]
        d = slice(h * V_DIM + s * QK_DIM, h * V_DIM + (s + 1) * QK_DIM)
        q = q_ref[b * TQ:(b + 1) * TQ, d]
        sc_s[u % 2, s, :, :PAST_LEN] = jnp.dot(q, ckt_ref[d, :].astype(BF), preferred_element_type=F32)
        sc_s[u % 2, s, :, PAST_LEN:] = jnp.dot(q, kt_ref[d, :], preferred_element_type=F32)

    def softmax(u, s):
        sc = sc_s[u % 2, s]
        e = jnp.exp2(sc - jnp.max(sc, axis=-1, keepdims=True))
        return e, 1.0 / jnp.sum(e, axis=-1, keepdims=True)

    def finish(u, p1, p2):
        b, h = units[u]
        cols = slice(h * V_DIM, (h + 1) * V_DIM)
        e = jnp.concatenate([p1[0].astype(BF), p2[0].astype(BF)], axis=0)
        vc = cv_ref[pl.ds(h, PAST_LEN, stride=N_HEADS), :].astype(BF)
        pv = jnp.dot(e[:, :PAST_LEN], vc, preferred_element_type=F32)
        pv = pv + jnp.dot(e[:, PAST_LEN:], v_ref[:, cols], preferred_element_type=F32)
        o = pv[:TQ] * p1[1] - pv[TQ:] * (lam * p2[1])
        o_ref[b * TQ:(b + 1) * TQ, cols] = _head_norm(o, sg, lam_init).astype(BF)

    scores(0, 0)
    scores(0, 1)
    for u in range(len(units)):
        more = u + 1 < len(units)
        if more:
            scores(u + 1, 0)
        p1 = softmax(u, 0)
        if more:
            scores(u + 1, 1)
        finish(u, p1, softmax(u, 1))


def _attn_lat(layer, q, kt, v, cache_kt, cache_v, lambda_qk, subln_g, att, lam_init):
    rows = LAT_Q_PER_STEP * TQ
    nqb = DEC_SEQ // rows
    q0 = N_CTX // rows
    return pl.pallas_call(
        functools.partial(_attn_lat_kernel, layer=layer, lam_init=lam_init),
        grid=(DEC_BATCH, nqb),
        in_specs=[
            pl.BlockSpec((rows, ATT_WIDTH), lambda b, t: (q0 + b * nqb + t, 0)),
            pl.BlockSpec((None, ATT_WIDTH, DEC_SEQ), lambda b, t: (CTX_TILES + b, 0, 0)),
            pl.BlockSpec((DEC_SEQ, ATT_WIDTH), lambda b, t: (CTX_TILES + b, 0)),
            pl.BlockSpec((None, None, ATT_WIDTH, PAST_LEN), lambda b, t: (b, layer, 0, 0)),
            pl.BlockSpec((None, None, PAST_LEN * N_HEADS, V_DIM), lambda b, t: (b, layer, 0, 0)),
            WHOLE, WHOLE,
            pl.BlockSpec(memory_space=pl.ANY),
        ],
        out_specs=pl.BlockSpec((rows, ATT_WIDTH), lambda b, t: (q0 + b * nqb + t, 0)),
        out_shape=jax.ShapeDtypeStruct((N_TOK, ATT_WIDTH), BF),
        scratch_shapes=[pltpu.VMEM((2, 2, TQ, PAST_LEN + DEC_SEQ), F32)],
        input_output_aliases={7: 0},
        compiler_params=_params(("arbitrary", "arbitrary")),
        name=f"attn_lat_l{layer}",
    )(q, kt, v, cache_kt, cache_v, lambda_qk, subln_g, att)


def _out_kernel(att_ref, conv_ref, w_ref, xa_ref, xb_ref, mod_ref, gf_ref, *rest, layer, route):
    if route:
        wr_ref, xo_ref, h2_ref, rt_ref, n_ref, wb_s = rest
    else:
        xo_ref, h2_ref, wb_s = rest
    i = pl.program_id(0)

    @pl.when(i == 0)
    def _():
        wb_s[...] = w_ref[...].astype(BF)

    lat = i >= CTX_TILES
    gain = gf_ref[layer:layer + 1, :] * (1.0 + _mod(mod_ref, layer, i, 4))
    for c in range(TM // ROW_CHUNK):
        rows = slice(c * ROW_CHUNK, (c + 1) * ROW_CHUNK)
        mo = jnp.dot(att_ref[rows, :], wb_s[:ATT_WIDTH, :], preferred_element_type=F32)
        mo = mo + jnp.dot(conv_ref[rows, :], wb_s[ATT_WIDTH:, :], preferred_element_type=F32)
        xn = jnp.where(lat, xb_ref[rows, :], xa_ref[rows, :]) + _mod(mod_ref, layer, i, 2) * mo
        xo_ref[rows, :] = xn
        h2_ref[rows, :] = (_rms(xn) * gain + _mod(mod_ref, layer, i, 3)).astype(BF)
    if route:
        _route(h2_ref[...], wr_ref, rt_ref, n_ref)


def _out_proj(layer, att, conv, w_out, xs, mod, g_ffn, w_router_pad=None):
    pair = isinstance(xs, tuple)
    xa, xb = xs if pair else (xs, xs)
    spec_a, spec_b = _stream_specs(pair)
    row_spec = pl.BlockSpec((TM, D_MODEL), lambda i: (i, 0))
    lane_spec = pl.BlockSpec((TM, LANES), lambda i: (i, 0))
    in_specs = [
        pl.BlockSpec((TM, ATT_WIDTH), lambda i: (i, 0)),
        pl.BlockSpec((TM, CONV_WIDTH), lambda i: (i, 0)),
        pl.BlockSpec((None, D_MODEL, D_MODEL), lambda i: (layer, 0, 0)),
        spec_a, spec_b,
        WHOLE, WHOLE,
    ]
    args = [att, conv, w_out, xa, xb, mod, g_ffn]
    out_specs = [row_spec, row_spec]
    out_shape = [jax.ShapeDtypeStruct((N_TOK, D_MODEL), F32), jax.ShapeDtypeStruct((N_TOK, D_MODEL), BF)]
    route = w_router_pad is not None
    if route:
        in_specs.append(WHOLE)
        args.append(w_router_pad)
        out_specs += [lane_spec, pl.BlockSpec((None, SUBLANES, LANES), lambda i: (i, 0, 0))]
        out_shape += [jax.ShapeDtypeStruct((N_TOK, LANES), F32),
                      jax.ShapeDtypeStruct((N_TILES, SUBLANES, LANES), jnp.int32)]
    return pl.pallas_call(
        functools.partial(_out_kernel, layer=layer, route=route),
        grid=(N_TILES,),
        in_specs=in_specs,
        out_specs=out_specs,
        out_shape=out_shape,
        scratch_shapes=[pltpu.VMEM((D_MODEL, D_MODEL), BF)],
        compiler_params=_params(("arbitrary",)),
        name=f"out_proj_l{layer}",
    )(*args)


def _weight_pieces(layer):
    pieces = []
    for c0 in range(0, D_MODEL, STAGE_W_COLS):
        cols = slice(c0, c0 + STAGE_W_COLS)
        pieces.append((0, layer, slice(0, D_MODEL), cols, None, slice(0, D_MODEL), cols))
    for half in range(FF_HALVES):
        for part in range(2):
            src0 = part * D_FF + half * FF_HALF
            for off in range(0, FF_HALF, STAGE_W_COLS):
                n = min(STAGE_W_COLS, FF_HALF - off)
                pieces.append((1, 0, slice(0, D_MODEL), slice(src0 + off, src0 + off + n),
                               half, slice(0, D_MODEL), slice(part * FF_HALF + off, part * FF_HALF + off + n)))
    for r0 in range(0, D_FF, STAGE_W_ROWS):
        rows = slice(r0, min(r0 + STAGE_W_ROWS, D_FF))
        for c0 in range(0, D_MODEL, STAGE_W_COLS):
            cols = slice(c0, c0 + STAGE_W_COLS)
            pieces.append((2, 0, rows, cols, None, rows, cols))
    return pieces


def _load_weights(pieces, hbm, resident, stage_s, sem):
    slots = stage_s.shape[0]

    def copy(k):
        src, idx, rows, cols, _, _, _ = pieces[k]
        nr, nc = rows.stop - rows.start, cols.stop - cols.start
        slot = k % slots
        return pltpu.make_async_copy(hbm[src].at[idx, rows, cols], stage_s.at[slot, :nr, :nc], sem.at[slot])

    for k in range(min(slots - 1, len(pieces))):
        copy(k).start()
    for k, (src, _, rows, cols, didx, drows, dcols) in enumerate(pieces):
        if k + slots - 1 < len(pieces):
            copy(k + slots - 1).start()
        copy(k).wait()
        nr, nc = rows.stop - rows.start, cols.stop - cols.start
        piece = stage_s[k % slots, :nr, :nc].astype(BF)
        if didx is None:
            resident[src][drows, dcols] = piece
        else:
            resident[src][didx, drows, dcols] = piece


def _dense_layer_kernel(att_ref, conv_ref, xa_ref, xb_ref, mod_ref, gf_ref, wo_hbm, wgu_hbm, wd_hbm,
                        o_ref, wo_b, wgu_b, wd_b, stage_s, sem, *, layer):
    i = pl.program_id(0)
    tile = i // (TM // FF_TM)

    @pl.when(i == 0)
    def _():
        _load_weights(_weight_pieces(layer), (wo_hbm, wgu_hbm, wd_hbm), (wo_b, wgu_b, wd_b), stage_s, sem)

    mo = jnp.dot(att_ref[...], wo_b[:ATT_WIDTH, :], preferred_element_type=F32)
    mo = mo + jnp.dot(conv_ref[...], wo_b[ATT_WIDTH:, :], preferred_element_type=F32)
    x = jnp.where(tile >= CTX_TILES, xb_ref[...], xa_ref[...])
    xn = x + _mod(mod_ref, layer, tile, 2) * mo
    h = (_rms(xn) * gf_ref[layer:layer + 1, :]) * (1.0 + _mod(mod_ref, layer, tile, 4)) + _mod(mod_ref, layer, tile, 3)
    h = h.astype(BF)
    y = None
    for half in range(FF_HALVES):
        gu = jnp.dot(h, wgu_b[half], preferred_element_type=F32)
        act = (_silu(gu[:, :FF_HALF]) * gu[:, FF_HALF:]).astype(BF)
        part = jnp.dot(act, wd_b[half * FF_HALF:(half + 1) * FF_HALF, :], preferred_element_type=F32)
        y = part if y is None else y + part
    o_ref[...] = xn + _mod(mod_ref, layer, tile, 5) * y


def _dense_layer(layer, att, conv, xs, mod, g_ffn, w_out, w_gu, w_down):
    xa, xb = xs
    n_ctx = N_CTX // FF_TM
    row = lambda width: pl.BlockSpec((FF_TM, width), lambda i: (i, 0))
    hbm = pl.BlockSpec(memory_space=pl.ANY)
    return pl.pallas_call(
        functools.partial(_dense_layer_kernel, layer=layer),
        grid=(N_TOK // FF_TM,),
        in_specs=[
            row(ATT_WIDTH), row(CONV_WIDTH),
            pl.BlockSpec((FF_TM, D_MODEL), lambda i: (jnp.minimum(i, n_ctx - 1), 0)),
            pl.BlockSpec((FF_TM, D_MODEL), lambda i: (jnp.maximum(i - n_ctx, 0), 0)),
            WHOLE, WHOLE, hbm, hbm, hbm,
        ],
        out_specs=row(D_MODEL),
        out_shape=jax.ShapeDtypeStruct((N_TOK, D_MODEL), F32),
        scratch_shapes=[
            pltpu.VMEM((D_MODEL, D_MODEL), BF),
            pltpu.VMEM((FF_HALVES, D_MODEL, 2 * FF_HALF), BF),
            pltpu.VMEM((D_FF, D_MODEL), BF),
            pltpu.VMEM((FF_STAGE_SLOTS, STAGE_W_ROWS, STAGE_W_COLS), F32),
            pltpu.SemaphoreType.DMA((FF_STAGE_SLOTS,)),
        ],
        compiler_params=_params(("arbitrary",)),
        name="dense_layer",
    )(att, conv, xa, xb, mod, g_ffn, w_out, w_gu, w_down)


def _route(h, wr_ref, rt_ref, n_ref):
    logits = jnp.dot(h, wr_ref[...].astype(BF), preferred_element_type=F32)
    lane = lax.broadcasted_iota(jnp.int32, logits.shape, 1)
    lg = jnp.where(lane < N_EXPERTS, logits, -jnp.inf)
    m1 = jnp.max(lg, axis=-1, keepdims=True)
    i1 = jnp.min(jnp.where(lg == m1, lane, LANES), axis=-1, keepdims=True)
    lg2 = jnp.where(lane == i1, -jnp.inf, lg)
    m2 = jnp.max(lg2, axis=-1, keepdims=True)
    i2 = jnp.min(jnp.where(lg2 == m2, lane, LANES), axis=-1, keepdims=True)
    e2 = jnp.exp(m2 - m1)
    w1 = 1.0 / (1.0 + e2)
    w2 = e2 / (1.0 + e2)

    sel1 = lane == i1
    sel2 = lane == i2
    onehot = jnp.logical_or(sel1, sel2)
    rows = lax.broadcasted_iota(jnp.int32, (ST, ST), 0)
    colsi = lax.broadcasted_iota(jnp.int32, (ST, ST), 1)
    earlier = (colsi < rows).astype(BF)
    onehot_b = onehot.astype(BF)
    before = jnp.concatenate(
        [jnp.dot(earlier, onehot_b[s * ST:(s + 1) * ST], preferred_element_type=F32) for s in range(SUB_PER_TILE)],
        axis=0)
    onehot_f = onehot.astype(F32)
    counts = [jnp.sum(onehot_f[s * ST:(s + 1) * ST], axis=0, keepdims=True) for s in range(SUB_PER_TILE)]
    counts = jnp.concatenate(counts + [jnp.zeros((SUBLANES - SUB_PER_TILE, LANES), F32)], axis=0).astype(jnp.int32)
    seg_len = ((counts + (SEG_ALIGN - 1)) // SEG_ALIGN) * SEG_ALIGN
    n_ref[...] = seg_len
    la = lax.broadcasted_iota(jnp.int32, (LANES, LANES), 0)
    lb = lax.broadcasted_iota(jnp.int32, (LANES, LANES), 1)
    seg_start = jnp.dot(seg_len.astype(F32).astype(BF), (la < lb).astype(BF), preferred_element_type=F32)
    start = jnp.concatenate(
        [jnp.broadcast_to(seg_start[s:s + 1], (ST, LANES)) for s in range(SUB_PER_TILE)], axis=0)
    where = before + start
    lp1 = jnp.sum(jnp.where(sel1, where, 0.0), axis=-1, keepdims=True)
    lp2 = jnp.sum(jnp.where(sel2, where, 0.0), axis=-1, keepdims=True)
    rt = jnp.where(lane == GATE_LANE, w1, jnp.where(lane == GATE_LANE + 1, w2, 0.0))
    rt_ref[...] = jnp.where(lane == ROW_LANE, lp1, jnp.where(lane == ROW_LANE + 1, lp2, rt))


def _chunk_copies(s, cnt_ref, cdst_ref, stage, rows_hbm, sem, *, to_hbm, wait):
    def copy(v, h):
        return pltpu.make_async_copy(v, h, sem) if to_hbm else pltpu.make_async_copy(h, v, sem)

    if wait:
        for z in WAIT_PIECES:
            @pl.when((cnt_ref[s] & z) != 0)
            def _():
                copy(stage.at[pl.ds(0, z * SEG_ALIGN)], rows_hbm.at[pl.ds(0, z * SEG_ALIGN)]).wait()
        return

    def start(c, priority):
        v = stage.at[pl.ds(pl.multiple_of(c * SEG_ALIGN, SEG_ALIGN), SEG_ALIGN)]
        h = rows_hbm.at[pl.ds(pl.multiple_of(cdst_ref[s * STAGE_CHUNKS + c], SEG_ALIGN), SEG_ALIGN)]
        copy(v, h).start(priority=priority)

    def pair(p, carry):
        start(2 * p, 0)
        start(2 * p + 1, 1)
        return carry

    n = cnt_ref[s]
    lax.fori_loop(0, n // 2, pair, 0)

    @pl.when((n & 1) != 0)
    def _():
        start(n - 1, 0)


def _dispatch_kernel(cnt_ref, cdst_ref, h_hbm, rt_hbm, xs_ref, stage_s, sem, step):
    copies = functools.partial(_chunk_copies, cnt_ref=cnt_ref, cdst_ref=cdst_ref, rows_hbm=xs_ref, to_hbm=True)

    step[0] = 0

    def tile(h_ref, rt_ref):
        i = step[0]
        step[0] = i + 1
        for k in range(SUB_PER_TILE):
            s = i * SUB_PER_TILE + k
            slot = k % 2
            rows = slice(k * ST, (k + 1) * ST)

            @pl.when(s >= 2)
            def _():
                copies(s - 2, stage=stage_s.at[slot], sem=sem.at[slot], wait=True)

            at = rt_ref[rows, :].T[ROW_LANE:ROW_LANE + 2, :].astype(jnp.int32)
            r = lax.broadcasted_iota(jnp.int32, (STAGE_ROWS, ST), 0)
            pick = jnp.logical_or(r == at[0:1, :], r == at[1:2, :]).astype(BF)
            stage_s[slot] = jnp.dot(pick, h_ref[rows, :], preferred_element_type=F32).astype(BF)
            copies(s, stage=stage_s.at[slot], sem=sem.at[slot], wait=False)

    pltpu.emit_pipeline(
        tile,
        grid=(N_TILES,),
        in_specs=[
            pl.BlockSpec((TM, D_MODEL), lambda i: (i, 0), pipeline_mode=pl.Buffered(DISPATCH_IN_BUFFERS)),
            pl.BlockSpec((TM, LANES), lambda i: (i, 0), pipeline_mode=pl.Buffered(DISPATCH_IN_BUFFERS)),
        ],
    )(h_hbm, rt_hbm)
    copies(N_SUB - 2, stage=stage_s.at[0], sem=sem.at[0], wait=True)
    copies(N_SUB - 1, stage=stage_s.at[1], sem=sem.at[1], wait=True)


def _dispatch(cnt, cdst, h2, rt):
    assert SUB_PER_TILE % 2 == 0
    hbm = pl.BlockSpec(memory_space=pl.ANY)
    grid_spec = pltpu.PrefetchScalarGridSpec(
        num_scalar_prefetch=2,
        grid=(1,),
        in_specs=[hbm, hbm],
        out_specs=hbm,
        scratch_shapes=[pltpu.VMEM((2, STAGE_ROWS, D_MODEL), BF), pltpu.SemaphoreType.DMA((2,)),
                        pltpu.SMEM((1,), jnp.int32)],
    )
    return pl.pallas_call(
        _dispatch_kernel,
        grid_spec=grid_spec,
        out_shape=jax.ShapeDtypeStruct((R_PAD, D_MODEL), BF),
        compiler_params=_params(("arbitrary",)),
        name="moe_dispatch",
    )(cnt, cdst, h2, rt)


def _expert_weights(r, te_ref, nt_ref, nxt_ref, w_hbm, wf_s, wb_s, sem):
    def fetch(e):
        return pltpu.make_async_copy(w_hbm.at[0, e], wf_s, sem)

    @pl.when(r == 0)
    def _():
        fetch(te_ref[0]).start()

    first = jnp.logical_or(r == 0, te_ref[r] != te_ref[jnp.maximum(r - 1, 0)])

    @pl.when(jnp.logical_and(r < nt_ref[0], first))
    def _():
        fetch(te_ref[r]).wait()
        wb_s[...] = wf_s[...].astype(BF)

        @pl.when(nxt_ref[r] >= 0)
        def _():
            fetch(nxt_ref[r]).start()


def _moe_ffn_kernel(te_ref, nt_ref, nxt_ref, used_ref, x_ref, wgu_hbm, wd_hbm, o_ref,
                    wgu_f, wgu_b, wd_f, wd_b, sem):
    def ffn(rows):
        gu = jnp.dot(x_ref[rows, :], wgu_b[...], preferred_element_type=F32)
        act = (_silu(gu[:, :D_FF_EXPERT]) * gu[:, D_FF_EXPERT:]).astype(BF)
        o_ref[rows, :] = jnp.dot(act, wd_b[...], preferred_element_type=F32).astype(BF)

    for k in range(TILES_PER_STEP):
        r = pl.program_id(0) * TILES_PER_STEP + k
        _expert_weights(r, te_ref, nt_ref, nxt_ref, wgu_hbm, wgu_f, wgu_b, sem.at[0])
        _expert_weights(r, te_ref, nt_ref, nxt_ref, wd_hbm, wd_f, wd_b, sem.at[1])

        @pl.when(jnp.logical_and(r < nt_ref[0], used_ref[r] > TG // 2))
        def _():
            ffn(slice(k * TG, (k + 1) * TG))

        @pl.when(jnp.logical_and(r < nt_ref[0], used_ref[r] <= TG // 2))
        def _():
            ffn(slice(k * TG, k * TG + TG // 2))


def _moe_ffn(te, nt, nxt, used, rows, w_gu, w_down):
    tile_map = lambda i, te, nt, nxt, used: (jnp.minimum(i, (nt[0] - 1) // TILES_PER_STEP), 0)
    grid_spec = pltpu.PrefetchScalarGridSpec(
        num_scalar_prefetch=4,
        grid=(NT_G // TILES_PER_STEP,),
        in_specs=[pl.BlockSpec((TILES_PER_STEP * TG, D_MODEL), tile_map),
                  pl.BlockSpec(memory_space=pl.ANY), pl.BlockSpec(memory_space=pl.ANY)],
        out_specs=pl.BlockSpec((TILES_PER_STEP * TG, D_MODEL), tile_map),
        scratch_shapes=[
            pltpu.VMEM((D_MODEL, 2 * D_FF_EXPERT), F32), pltpu.VMEM((D_MODEL, 2 * D_FF_EXPERT), BF),
            pltpu.VMEM((D_FF_EXPERT, D_MODEL), F32), pltpu.VMEM((D_FF_EXPERT, D_MODEL), BF),
            pltpu.SemaphoreType.DMA((2,)),
        ],
    )
    return pl.pallas_call(
        _moe_ffn_kernel,
        grid_spec=grid_spec,
        out_shape=jax.ShapeDtypeStruct((R_PAD, D_MODEL), BF),
        compiler_params=_params(("arbitrary",)),
        name="moe_ffn",
    )(te, nt, nxt, used, rows, w_gu, w_down)


def _combine_kernel(cnt_ref, cdst_ref, ys_ref, rt_ref, x_ref, mod_ref, fg_ref,
                    oa_ref, ob_ref, stage_s, sem, *, layer):
    i = pl.program_id(0)
    copies = functools.partial(_chunk_copies, cnt_ref=cnt_ref, cdst_ref=cdst_ref, rows_hbm=ys_ref, to_hbm=False)

    @pl.when(i == 0)
    def _():
        stage_s[...] = jnp.zeros_like(stage_s)
        for s in range(COMBINE_AHEAD):
            copies(s, stage=stage_s.at[s], sem=sem.at[s], wait=False)

    for k in range(SUB_PER_TILE):
        s = i * SUB_PER_TILE + k
        slot = k % COMBINE_SLOTS
        ahead = (k + COMBINE_AHEAD) % COMBINE_SLOTS
        rows = slice(k * ST, (k + 1) * ST)

        @pl.when(s + COMBINE_AHEAD < N_SUB)
        def _():
            copies(s + COMBINE_AHEAD, stage=stage_s.at[ahead], sem=sem.at[ahead], wait=False)

        copies(s, stage=stage_s.at[slot], sem=sem.at[slot], wait=True)

        rt = rt_ref[rows, :]
        at = rt[:, ROW_LANE:ROW_LANE + 2].astype(jnp.int32)
        r = lax.broadcasted_iota(jnp.int32, (ST, STAGE_ROWS), 1)
        staged = stage_s[slot]
        a = jnp.dot((r == at[:, 0:1]).astype(BF), staged, preferred_element_type=F32)
        b = jnp.dot((r == at[:, 1:2]).astype(BF), staged, preferred_element_type=F32)
        y = rt[:, GATE_LANE:GATE_LANE + 1] * a + rt[:, GATE_LANE + 1:GATE_LANE + 2] * b
        xn = x_ref[rows, :] + _mod(mod_ref, layer, i, 5) * y
        out = _rms(xn) * fg_ref[...]

        @pl.when(i < CTX_TILES)
        def _():
            oa_ref[rows, :] = out

        @pl.when(i >= CTX_TILES)
        def _():
            ob_ref[rows, :] = out


def _combine(layer, cnt, cdst, ys, rt, x, mod, final_g):
    assert SUB_PER_TILE % COMBINE_SLOTS == 0
    grid_spec = pltpu.PrefetchScalarGridSpec(
        num_scalar_prefetch=2,
        grid=(N_TILES,),
        in_specs=[
            pl.BlockSpec(memory_space=pl.ANY),
            pl.BlockSpec((TM, LANES), lambda i, *_: (i, 0)),
            pl.BlockSpec((TM, D_MODEL), lambda i, *_: (i, 0)),
            WHOLE, WHOLE,
        ],
        out_specs=[
            pl.BlockSpec((TM, D_MODEL), lambda i, *_: (jnp.minimum(i, CTX_TILES - 1), 0)),
            pl.BlockSpec((TM, D_MODEL), lambda i, *_: (jnp.maximum(i - CTX_TILES, 0), 0)),
        ],
        scratch_shapes=[pltpu.VMEM((COMBINE_SLOTS, STAGE_ROWS, D_MODEL), BF),
                        pltpu.SemaphoreType.DMA((COMBINE_SLOTS,))],
    )
    return pl.pallas_call(
        functools.partial(_combine_kernel, layer=layer),
        grid_spec=grid_spec,
        out_shape=[
            jax.ShapeDtypeStruct((N_CTX, D_MODEL), F32),
            jax.ShapeDtypeStruct((N_LAT, D_MODEL), F32),
        ],
        compiler_params=_params(("arbitrary",)),
        name="moe_combine",
    )(cnt, cdst, ys, rt, x, mod, final_g)


def _group_layout(n_tiles):
    n = n_tiles[:, :SUB_PER_TILE, :N_EXPERTS].reshape(N_SUB, N_EXPERTS)
    tiles = (jnp.sum(n, axis=0) + TG - 1) // TG
    tile_end = jnp.cumsum(tiles)
    region = (tile_end - tiles) * TG
    dst = region[None, :] + jnp.cumsum(n, axis=0) - n
    seg_end = jnp.cumsum(n, axis=1)
    seg = seg_end - n
    row = jnp.arange(STAGE_CHUNKS, dtype=jnp.int32) * SEG_ALIGN
    owner = jnp.sum((row[None, :, None] >= seg_end[:, None, :]).astype(jnp.int32), axis=-1)
    own = jnp.minimum(owner, N_EXPERTS - 1)[..., None] == jnp.arange(N_EXPERTS)
    cdst = jnp.sum(jnp.where(own, (dst - seg)[:, None, :], 0), axis=-1) + row[None, :]
    cnt = seg_end[:, -1] // SEG_ALIGN
    nt = tile_end[-1]
    tile_id = jnp.minimum(jnp.arange(NT_G, dtype=jnp.int32), nt - 1)
    te = jnp.sum((tile_id[:, None] >= tile_end[None, :]).astype(jnp.int32), axis=-1)
    after = jnp.sum(jnp.where(te[:, None] == jnp.arange(N_EXPERTS), tile_end[None, :], 0), axis=-1)
    nxt = jnp.where(after < nt, jnp.sum((after[:, None] >= tile_end[None, :]).astype(jnp.int32), axis=-1), -1)
    mine = te[:, None] == jnp.arange(N_EXPERTS)
    region_end = jnp.sum(jnp.where(mine, (region + jnp.sum(n, axis=0))[None, :], 0), axis=-1)
    used = jnp.clip(region_end - tile_id * TG, 0, TG)
    i32 = lambda a: a.astype(jnp.int32)
    return (i32(cnt), i32(cdst.reshape(N_SUB * STAGE_CHUNKS)), i32(te), i32(nt.reshape(1)), i32(nxt), i32(used))


def _rope_tables():
    p = np.arange(DEC_SEQ)
    row = (p // GRID_W).astype(np.float32)
    col = (p % GRID_W).astype(np.float32)
    half = QK_DIM // 4
    freqs = (ROPE_BASE ** (-np.arange(half, dtype=np.float32) / half)).astype(np.float32)
    lane = np.arange(V_DIM)
    f = freqs[lane & (half - 1)]
    use_col = (lane & (2 * half)) != 0
    ang = (np.where(use_col[None, :], col[:, None], row[:, None]) * f[None, :]).astype(np.float32)
    upper = (lane & half) != 0
    sin = np.sin(ang)
    return jnp.asarray(np.cos(ang), F32), jnp.asarray(np.where(upper[None, :], sin, -sin), F32)


def kernel(x_prompt, x_sample, cache_k, cache_v, c, c_ctx, w_ada, b_ada, norm_mix_g, norm_ffn_g,
           w_in, lambda_qk, subln_g, conv_w, w_out, w_gu_dense, w_down_dense, w_router,
           w_gu_moe, w_down_moe, final_g):
    assert DEPTH == 2
    xs = (x_prompt.reshape(N_CTX, D_MODEL), x_sample.reshape(N_LAT, D_MODEL))
    mod = _ada(c_ctx, c, w_ada, b_ada)
    cos_t, sin_t = _rope_tables()
    cache_kt = jnp.transpose(cache_k, (0, 1, 3, 4, 5, 2)).reshape(DEC_BATCH, DEPTH, ATT_WIDTH, PAST_LEN)
    cache_v4 = cache_v.reshape(DEC_BATCH, DEPTH, PAST_LEN * N_HEADS, V_DIM)

    new_kv = None
    for layer in range(DEPTH):
        lam_init = 0.8 - 0.6 * math.exp(-0.3 * layer)
        q, v, kt, conv, nk, nv = _in_proj(layer, xs, mod, norm_mix_g, w_in, cos_t, sin_t, conv_w, new_kv)
        new_kv = (nk, nv)
        att = _attn_ctx(layer, q, kt, v, lambda_qk, subln_g, lam_init)
        att = _attn_lat(layer, q, kt, v, cache_kt, cache_v4, lambda_qk, subln_g, att, lam_init)
        if layer == 0:
            xs = _dense_layer(layer, att, conv, xs, mod, norm_ffn_g, w_out, w_gu_dense, w_down_dense)
        else:
            wr = jnp.pad(w_router[0], ((0, 0), (0, LANES - N_EXPERTS)))
            x1, h2, rt, n_tiles = _out_proj(layer, att, conv, w_out, xs, mod, norm_ffn_g, wr)
            cnt, cdst, te, nt, nxt, used = _group_layout(n_tiles)
            xsort = _dispatch(cnt, cdst, h2, rt)
            ys = _moe_ffn(te, nt, nxt, used, xsort, w_gu_moe, w_down_moe)
            y_ctx, y_lat = _combine(layer, cnt, cdst, ys, rt, x1, mod, final_g.reshape(1, D_MODEL))
    nk, nv = new_kv
    new_k = jnp.transpose(nk.reshape(BATCH, DEPTH, N_HEADS, 2, QK_DIM, SEQ), (0, 1, 5, 2, 3, 4))
    new_v = nv.reshape(BATCH, DEPTH, SEQ, N_HEADS, V_DIM)
    return (y_ctx.reshape(BATCH, SEQ, D_MODEL), y_lat.reshape(DEC_BATCH, DEC_SEQ, D_MODEL), new_k, new_v)
```

```python
import functools
import math

import numpy as np
import jax
import jax.numpy as jnp
from jax import lax
from jax.experimental import pallas as pl
from jax.experimental.pallas import tpu as pltpu

D_MODEL = 1024
BATCH = 16
SEQ = 256
DEPTH = 2
DEC_BATCH = 4
DEC_SEQ = 1024
PAST_LEN = 512
GRID_W = 64
ATT_WIDTH = 512
CONV_WIDTH = 512
N_HEADS = 4
V_DIM = 128
QK_DIM = 64
ROPE_BASE = 10000.0
D_FF = 2816
N_EXPERTS = 8
D_FF_EXPERT = 1408
N_MOD = 6
NORM_EPS = 1e-6
Q_SCALE = QK_DIM ** -0.5 * math.log2(math.e)
IN_COLS = 3 * ATT_WIDTH + 3 * CONV_WIDTH

N_CTX = BATCH * SEQ
N_LAT = DEC_BATCH * DEC_SEQ
N_TOK = N_CTX + N_LAT
TM = 1024
N_TILES = N_TOK // TM
CTX_TILES = N_CTX // TM
SEQ_PER_TILE = TM // SEQ
CTX_SEQ_PER_STEP = 4
COND_ROWS = 8
TN_IN = 1024
N_IN_TILES = IN_COLS // TN_IN
ROW_CHUNK = 512
FF_TM = 512
FF_HALVES = 2
FF_HALF = D_FF // FF_HALVES
STAGE_W_ROWS = 1024
STAGE_W_COLS = 512
FF_STAGE_SLOTS = 8
IN_STAGE_SLOTS = 4
TN_ADA = 1536
TG = 512
ST = 256
SUB_PER_TILE = TM // ST
N_SUB = N_TOK // ST
SEG_ALIGN = 8
STAGE_ROWS = 640
STAGE_CHUNKS = STAGE_ROWS // SEG_ALIGN
WAIT_PIECES = (64, 32, 16, 8, 4, 2, 1)
DISPATCH_IN_BUFFERS = 2
COMBINE_SLOTS = 4
COMBINE_AHEAD = 3
TILES_PER_STEP = 2
_MAX_SORTED_ROWS = 2 * N_TOK + N_SUB * N_EXPERTS * (SEG_ALIGN - 1) + N_EXPERTS * (TG - SEG_ALIGN)
NT_G = -(-_MAX_SORTED_ROWS // (TG * TILES_PER_STEP)) * TILES_PER_STEP
R_PAD = NT_G * TG
LANES = 128
SUBLANES = 8
GATE_LANE = 0
ROW_LANE = 2
VMEM_LIMIT = 60 * 1024 * 1024

BF = jnp.bfloat16
F32 = jnp.float32


def _params(sem, vmem=VMEM_LIMIT):
    return pltpu.CompilerParams(dimension_semantics=sem, vmem_limit_bytes=vmem)


def _mod_row(i):
    return jnp.where(i < CTX_TILES, 0, i - (CTX_TILES - 1))


WHOLE = pl.BlockSpec(memory_space=pltpu.VMEM)


def _mod(mod_ref, layer, i, c):
    return mod_ref[layer, pl.ds(_mod_row(i), 1), c * D_MODEL:(c + 1) * D_MODEL]


def _stream_specs(pair, width=D_MODEL):
    a = pl.BlockSpec((TM, width), lambda i, *_: (jnp.minimum(i, CTX_TILES - 1), 0))
    if pair:
        b = pl.BlockSpec((TM, width), lambda i, *_: (jnp.maximum(i - CTX_TILES, 0), 0))
    else:
        b = pl.BlockSpec((TM, width), lambda i, *_: (jnp.maximum(i, CTX_TILES), 0))
    return a, b


def _silu(x):
    return x / (1.0 + jnp.exp(-x))


def _rms(x):
    return x * lax.rsqrt(jnp.mean(x * x, axis=-1, keepdims=True) + NORM_EPS)


def _ada_kernel(cc_ref, c_ref, w_ref, b_ref, o_ref):
    pad = jnp.zeros((COND_ROWS - 1 - DEC_BATCH, D_MODEL), F32)
    cond = jnp.concatenate([cc_ref[...], c_ref[...], pad], axis=0)
    s = _silu(cond).astype(BF)
    bias = b_ref[pl.ds(pl.program_id(0), 1), :]
    o_ref[...] = jnp.dot(s, w_ref[...].astype(BF), preferred_element_type=F32) + bias


def _ada(c_ctx, c, w_ada, b_ada):
    n = N_MOD * D_MODEL
    return pl.pallas_call(
        _ada_kernel,
        grid=(DEPTH, n // TN_ADA),
        in_specs=[
            WHOLE, WHOLE,
            pl.BlockSpec((None, D_MODEL, TN_ADA), lambda l, j: (l, 0, j)),
            pl.BlockSpec((DEPTH, TN_ADA), lambda l, j: (0, j)),
        ],
        out_specs=pl.BlockSpec((None, COND_ROWS, TN_ADA), lambda l, j: (l, 0, j)),
        out_shape=jax.ShapeDtypeStruct((DEPTH, COND_ROWS, n), F32),
        compiler_params=_params(("arbitrary", "arbitrary")),
        name="ada_mod",
    )(c_ctx.reshape(1, D_MODEL), c, w_ada, b_ada)


def _in_weight_pieces(layer):
    pieces = []
    for c0 in range(0, IN_COLS, STAGE_W_COLS):
        g, off = divmod(c0, TN_IN)
        pieces.append((0, layer, slice(0, D_MODEL), slice(c0, c0 + STAGE_W_COLS),
                       g, slice(0, D_MODEL), slice(off, off + STAGE_W_COLS)))
    return pieces


def _in_kernel(*refs, layer, aliased):
    xa_ref, xb_ref, mod_ref, g_ref, w_hbm, cos_ref, sin_ref, cw_ref = refs[:8]
    refs = refs[10:] if aliased else refs[8:]
    q_ref, v_ref, kt_ref, conv_ref, nk_ref, nv_ref, h_s, wb_s, gb_s, gc_s, stage_s, sem = refs
    i = pl.program_id(0)

    @pl.when(i == 0)
    def _():
        _load_weights(_in_weight_pieces(layer), (w_hbm,), (wb_s,), stage_s, sem)

    chunks = [slice(c * ROW_CHUNK, (c + 1) * ROW_CHUNK) for c in range(TM // ROW_CHUNK)]
    seqs_per_chunk = ROW_CHUNK // SEQ

    def norm(x_ref, rows):
        gain = g_ref[layer:layer + 1, :] * (1.0 + _mod(mod_ref, layer, i, 1))
        h_s[rows, :] = (_rms(x_ref[rows, :]) * gain + _mod(mod_ref, layer, i, 0)).astype(BF)

    def proj(rows, group):
        acc = jnp.dot(h_s[rows, :], wb_s[group], preferred_element_type=F32)
        return acc[:, :ATT_WIDTH], acc[:, ATT_WIDTH:]

    def roped(a, rows):
        cos = jnp.concatenate([cos_ref[rows, :]] * N_HEADS, axis=1)
        sin = jnp.concatenate([sin_ref[rows, :]] * N_HEADS, axis=1)
        lane = lax.broadcasted_iota(jnp.int32, a.shape, 1)
        upper = (lane & (QK_DIM // 4)) != 0
        partner = jnp.where(upper, pltpu.roll(a, QK_DIM // 4, 1), pltpu.roll(a, ATT_WIDTH - QK_DIM // 4, 1))
        return a * cos + partner * sin

    def gated_conv(seq):
        for rows in chunks:
            gc, xin = proj(rows, 2)
            gc_s[rows, :] = gc * xin
        u = gc_s[...]
        pos = lax.broadcasted_iota(jnp.int32, (TM, 1), 0) & (seq - 1)
        prev = jnp.where(pos == 0, 0.0, pltpu.roll(u, 1, 0))
        nxt = jnp.where(pos == seq - 1, 0.0, pltpu.roll(u, TM - 1, 0))
        cw = cw_ref[layer]
        conv = prev * cw[0:1] + u * cw[1:2] + nxt * cw[2:3]
        conv_ref[...] = (gb_s[...] * conv).astype(BF)

    @pl.when(i >= CTX_TILES)
    def _():
        for rows in chunks:
            norm(xb_ref, rows)
            q, k = proj(rows, 0)
            q_ref[rows, :] = (roped(q, rows) * Q_SCALE).astype(BF)
            kt_ref[:, rows] = roped(k, rows).T.astype(BF)
        for rows in chunks:
            v, gb = proj(rows, 1)
            v_ref[rows, :] = v.astype(BF)
            gb_s[rows, :] = gb
        gated_conv(DEC_SEQ)

    @pl.when(i < CTX_TILES)
    def _():
        for c, rows in enumerate(chunks):
            norm(xa_ref, rows)
            q, k = proj(rows, 0)
            q_ref[rows, :] = (q * Q_SCALE).astype(BF)
            kt = k.T
            kt_ref[:, rows] = kt.astype(BF)
            for s in range(seqs_per_chunk):
                nk_ref[c * seqs_per_chunk + s] = kt[:, s * SEQ:(s + 1) * SEQ]
        for c, rows in enumerate(chunks):
            v, gb = proj(rows, 1)
            v_ref[rows, :] = v.astype(BF)
            gb_s[rows, :] = gb
            for s in range(seqs_per_chunk):
                for h in range(N_HEADS):
                    nv_ref[c * seqs_per_chunk + s, pl.ds(h, SEQ, stride=N_HEADS), :] = (
                        v[s * SEQ:(s + 1) * SEQ, h * V_DIM:(h + 1) * V_DIM])
        gated_conv(SEQ)


def _in_proj(layer, xs, mod, g_mix, w_in, cos_t, sin_t, conv_w, new_kv):
    pair = isinstance(xs, tuple)
    xa, xb = xs if pair else (xs, xs)
    spec_a, spec_b = _stream_specs(pair)
    ctx_i = lambda i: jnp.minimum(i, CTX_TILES - 1)
    in_specs = [
        spec_a, spec_b,
        WHOLE, WHOLE,
        pl.BlockSpec(memory_space=pl.ANY),
        WHOLE, WHOLE, WHOLE,
    ]
    args = [xa, xb, mod, g_mix, w_in, cos_t, sin_t, conv_w]
    aliases = {}
    if new_kv is not None:
        in_specs += [pl.BlockSpec(memory_space=pl.ANY), pl.BlockSpec(memory_space=pl.ANY)]
        args += list(new_kv)
        aliases = {8: 4, 9: 5}
    row_tile = pl.BlockSpec((TM, ATT_WIDTH), lambda i: (i, 0))
    return pl.pallas_call(
        functools.partial(_in_kernel, layer=layer, aliased=new_kv is not None),
        grid=(N_TILES,),
        in_specs=in_specs,
        out_specs=[
            row_tile,
            row_tile,
            pl.BlockSpec((None, ATT_WIDTH, TM), lambda i: (i, 0, 0)),
            row_tile,
            pl.BlockSpec((SEQ_PER_TILE, None, ATT_WIDTH, SEQ), lambda i: (ctx_i(i), layer, 0, 0)),
            pl.BlockSpec((SEQ_PER_TILE, None, SEQ * N_HEADS, V_DIM), lambda i: (ctx_i(i), layer, 0, 0)),
        ],
        out_shape=[
            jax.ShapeDtypeStruct((N_TOK, ATT_WIDTH), BF),
            jax.ShapeDtypeStruct((N_TOK, ATT_WIDTH), BF),
            jax.ShapeDtypeStruct((N_TILES, ATT_WIDTH, TM), BF),
            jax.ShapeDtypeStruct((N_TOK, CONV_WIDTH), BF),
            jax.ShapeDtypeStruct((BATCH, DEPTH, ATT_WIDTH, SEQ), F32),
            jax.ShapeDtypeStruct((BATCH, DEPTH, SEQ * N_HEADS, V_DIM), F32),
        ],
        scratch_shapes=[
            pltpu.VMEM((TM, D_MODEL), BF),
            pltpu.VMEM((N_IN_TILES, D_MODEL, TN_IN), BF),
            pltpu.VMEM((TM, CONV_WIDTH), F32),
            pltpu.VMEM((TM, CONV_WIDTH), F32),
            pltpu.VMEM((IN_STAGE_SLOTS, STAGE_W_ROWS, STAGE_W_COLS), F32),
            pltpu.SemaphoreType.DMA((IN_STAGE_SLOTS,)),
        ],
        input_output_aliases=aliases,
        compiler_params=_params(("arbitrary",)),
        name=f"in_proj_l{layer}",
    )(*args)


def _lambda(lq_ref, layer, lam_init):
    lq = lq_ref[layer]
    a = jnp.exp(jnp.sum(lq[0:1] * lq[1:2], axis=-1, keepdims=True))
    b = jnp.exp(jnp.sum(lq[2:3] * lq[3:4], axis=-1, keepdims=True))
    return a - b + lam_init


def _head_norm(o, sg, lam_init):
    return _rms(o) * sg * (1.0 - lam_init)


def _attn_ctx_kernel(q_ref, kt_ref, v_ref, lq_ref, sg_ref, o_ref, sc_s, *, layer, lam_init):
    lam = _lambda(lq_ref, layer, lam_init)
    sg = sg_ref[layer:layer + 1, :]

    def sequence(b, carry):
        pos = pl.ds(pl.multiple_of(b * SEQ, SEQ), SEQ)

        def scores(h):
            for s in range(2):
                d = slice(h * V_DIM + s * QK_DIM, h * V_DIM + (s + 1) * QK_DIM)
                sc_s[h % 2, s] = jnp.dot(q_ref[pos, d], kt_ref[d, pos], preferred_element_type=F32)

        def finish(h):
            cols = slice(h * V_DIM, (h + 1) * V_DIM)
            v = v_ref[pos, cols]
            outs = []
            for s in range(2):
                sc = sc_s[h % 2, s]
                e = jnp.exp2(sc - jnp.max(sc, axis=-1, keepdims=True))
                r = 1.0 / jnp.sum(e, axis=-1, keepdims=True)
                outs.append(jnp.dot(e.astype(BF), v, preferred_element_type=F32) * r)
            o = outs[0] - lam * outs[1]
            o_ref[pos, cols] = _head_norm(o, sg, lam_init).astype(BF)

        scores(0)
        for h in range(N_HEADS):
            if h + 1 < N_HEADS:
                scores(h + 1)
            finish(h)
        return carry

    lax.fori_loop(0, CTX_SEQ_PER_STEP, sequence, 0)


def _attn_ctx(layer, q, kt, v, lambda_qk, subln_g, lam_init):
    rows = CTX_SEQ_PER_STEP * SEQ
    per_tile = TM // rows
    return pl.pallas_call(
        functools.partial(_attn_ctx_kernel, layer=layer, lam_init=lam_init),
        grid=(N_CTX // rows,),
        in_specs=[
            pl.BlockSpec((rows, ATT_WIDTH), lambda b: (b, 0)),
            pl.BlockSpec((None, ATT_WIDTH, rows), lambda b: (b // per_tile, 0, b % per_tile)),
            pl.BlockSpec((rows, ATT_WIDTH), lambda b: (b, 0)),
            WHOLE, WHOLE,
        ],
        out_specs=pl.BlockSpec((rows, ATT_WIDTH), lambda b: (b, 0)),
        out_shape=jax.ShapeDtypeStruct((N_TOK, ATT_WIDTH), BF),
        scratch_shapes=[pltpu.VMEM((2, 2, SEQ, SEQ), F32)],
        compiler_params=_params(("arbitrary",)),
        name=f"attn_ctx_l{layer}",
    )(q, kt, v, lambda_qk, subln_g)


TQ = 256
LAT_Q_PER_STEP = 4


def _attn_lat_kernel(q_ref, kt_ref, v_ref, ckt_ref, cv_ref, lq_ref, sg_ref, att_in_ref, o_ref, sc_s, *,
                     layer, lam_init):
    del att_in_ref
    lam = _lambda(lq_ref, layer, lam_init)
    sg = sg_ref[layer:layer + 1, :]

    units = [(b, h) for b in range(LAT_Q_PER_STEP) for h in range(N_HEADS)]

    def scores(u, s):
        b, h = units[u]
        d = slice(h * V_DIM + s * QK_DIM, h * V_DIM + (s + 1) * QK_DIM)
        q = q_ref[b * TQ:(b + 1) * TQ, d]
        sc_s[u % 2, s, :, :PAST_LEN] = jnp.dot(q, ckt_ref[d, :].astype(BF), preferred_element_type=F32)
        sc_s[u % 2, s, :, PAST_LEN:] = jnp.dot(q, kt_ref[d, :], preferred_element_type=F32)

    def softmax(u, s):
        sc = sc_s[u % 2, s]
        e = jnp.exp2(sc - jnp.max(sc, axis=-1, keepdims=True))
        return e, 1.0 / jnp.sum(e, axis=-1, keepdims=True)

    def finish(u, p1, p2):
        b, h = units[u]
        cols = slice(h * V_DIM, (h + 1) * V_DIM)
        e = jnp.concatenate([p1[0].astype(BF), p2[0].astype(BF)], axis=0)
        vc = cv_ref[pl.ds(h, PAST_LEN, stride=N_HEADS), :].astype(BF)
        pv = jnp.dot(e[:, :PAST_LEN], vc, preferred_element_type=F32)
        pv = pv + jnp.dot(e[:, PAST_LEN:], v_ref[:, cols], preferred_element_type=F32)
        o = pv[:TQ] * p1[1] - pv[TQ:] * (lam * p2[1])
        o_ref[b * TQ:(b + 1) * TQ, cols] = _head_norm(o, sg, lam_init).astype(BF)

    scores(0, 0)
    scores(0, 1)
    for u in range(len(units)):
        more = u + 1 < len(units)
        if more:
            scores(u + 1, 0)
        p1 = softmax(u, 0)
        if more:
            scores(u + 1, 1)
        finish(u, p1, softmax(u, 1))


def _attn_lat(layer, q, kt, v, cache_kt, cache_v, lambda_qk, subln_g, att, lam_init):
    rows = LAT_Q_PER_STEP * TQ
    nqb = DEC_SEQ // rows
    q0 = N_CTX // rows
    return pl.pallas_call(
        functools.partial(_attn_lat_kernel, layer=layer, lam_init=lam_init),
        grid=(DEC_BATCH, nqb),
        in_specs=[
            pl.BlockSpec((rows, ATT_WIDTH), lambda b, t: (q0 + b * nqb + t, 0)),
            pl.BlockSpec((None, ATT_WIDTH, DEC_SEQ), lambda b, t: (CTX_TILES + b, 0, 0)),
            pl.BlockSpec((DEC_SEQ, ATT_WIDTH), lambda b, t: (CTX_TILES + b, 0)),
            pl.BlockSpec((None, None, ATT_WIDTH, PAST_LEN), lambda b, t: (b, layer, 0, 0)),
            pl.BlockSpec((None, None, PAST_LEN * N_HEADS, V_DIM), lambda b, t: (b, layer, 0, 0)),
            WHOLE, WHOLE,
            pl.BlockSpec(memory_space=pl.ANY),
        ],
        out_specs=pl.BlockSpec((rows, ATT_WIDTH), lambda b, t: (q0 + b * nqb + t, 0)),
        out_shape=jax.ShapeDtypeStruct((N_TOK, ATT_WIDTH), BF),
        scratch_shapes=[pltpu.VMEM((2, 2, TQ, PAST_LEN + DEC_SEQ), F32)],
        input_output_aliases={7: 0},
        compiler_params=_params(("arbitrary", "arbitrary")),
        name=f"attn_lat_l{layer}",
    )(q, kt, v, cache_kt, cache_v, lambda_qk, subln_g, att)


def _out_kernel(att_ref, conv_ref, w_ref, xa_ref, xb_ref, mod_ref, gf_ref, *rest, layer, route):
    if route:
        wr_ref, xo_ref, h2_ref, rt_ref, n_ref, wb_s = rest
    else:
        xo_ref, h2_ref, wb_s = rest
    i = pl.program_id(0)

    @pl.when(i == 0)
    def _():
        wb_s[...] = w_ref[...].astype(BF)

    lat = i >= CTX_TILES
    gain = gf_ref[layer:layer + 1, :] * (1.0 + _mod(mod_ref, layer, i, 4))
    for c in range(TM // ROW_CHUNK):
        rows = slice(c * ROW_CHUNK, (c + 1) * ROW_CHUNK)
        mo = jnp.dot(att_ref[rows, :], wb_s[:ATT_WIDTH, :], preferred_element_type=F32)
        mo = mo + jnp.dot(conv_ref[rows, :], wb_s[ATT_WIDTH:, :], preferred_element_type=F32)
        xn = jnp.where(lat, xb_ref[rows, :], xa_ref[rows, :]) + _mod(mod_ref, layer, i, 2) * mo
        xo_ref[rows, :] = xn
        h2_ref[rows, :] = (_rms(xn) * gain + _mod(mod_ref, layer, i, 3)).astype(BF)
    if route:
        _route(h2_ref[...], wr_ref, rt_ref, n_ref)


def _out_proj(layer, att, conv, w_out, xs, mod, g_ffn, w_router_pad=None):
    pair = isinstance(xs, tuple)
    xa, xb = xs if pair else (xs, xs)
    spec_a, spec_b = _stream_specs(pair)
    row_spec = pl.BlockSpec((TM, D_MODEL), lambda i: (i, 0))
    lane_spec = pl.BlockSpec((TM, LANES), lambda i: (i, 0))
    in_specs = [
        pl.BlockSpec((TM, ATT_WIDTH), lambda i: (i, 0)),
        pl.BlockSpec((TM, CONV_WIDTH), lambda i: (i, 0)),
        pl.BlockSpec((None, D_MODEL, D_MODEL), lambda i: (layer, 0, 0)),
        spec_a, spec_b,
        WHOLE, WHOLE,
    ]
    args = [att, conv, w_out, xa, xb, mod, g_ffn]
    out_specs = [row_spec, row_spec]
    out_shape = [jax.ShapeDtypeStruct((N_TOK, D_MODEL), F32), jax.ShapeDtypeStruct((N_TOK, D_MODEL), BF)]
    route = w_router_pad is not None
    if route:
        in_specs.append(WHOLE)
        args.append(w_router_pad)
        out_specs += [lane_spec, pl.BlockSpec((None, SUBLANES, LANES), lambda i: (i, 0, 0))]
        out_shape += [jax.ShapeDtypeStruct((N_TOK, LANES), F32),
                      jax.ShapeDtypeStruct((N_TILES, SUBLANES, LANES), jnp.int32)]
    return pl.pallas_call(
        functools.partial(_out_kernel, layer=layer, route=route),
        grid=(N_TILES,),
        in_specs=in_specs,
        out_specs=out_specs,
        out_shape=out_shape,
        scratch_shapes=[pltpu.VMEM((D_MODEL, D_MODEL), BF)],
        compiler_params=_params(("arbitrary",)),
        name=f"out_proj_l{layer}",
    )(*args)


def _weight_pieces(layer):
    pieces = []
    for c0 in range(0, D_MODEL, STAGE_W_COLS):
        cols = slice(c0, c0 + STAGE_W_COLS)
        pieces.append((0, layer, slice(0, D_MODEL), cols, None, slice(0, D_MODEL), cols))
    for half in range(FF_HALVES):
        for part in range(2):
            src0 = part * D_FF + half * FF_HALF
            for off in range(0, FF_HALF, STAGE_W_COLS):
                n = min(STAGE_W_COLS, FF_HALF - off)
                pieces.append((1, 0, slice(0, D_MODEL), slice(src0 + off, src0 + off + n),
                               half, slice(0, D_MODEL), slice(part * FF_HALF + off, part * FF_HALF + off + n)))
    for r0 in range(0, D_FF, STAGE_W_ROWS):
        rows = slice(r0, min(r0 + STAGE_W_ROWS, D_FF))
        for c0 in range(0, D_MODEL, STAGE_W_COLS):
            cols = slice(c0, c0 + STAGE_W_COLS)
            pieces.append((2, 0, rows, cols, None, rows, cols))
    return pieces


def _load_weights(pieces, hbm, resident, stage_s, sem):
    slots = stage_s.shape[0]

    def copy(k):
        src, idx, rows, cols, _, _, _ = pieces[k]
        nr, nc = rows.stop - rows.start, cols.stop - cols.start
        slot = k % slots
        return pltpu.make_async_copy(hbm[src].at[idx, rows, cols], stage_s.at[slot, :nr, :nc], sem.at[slot])

    for k in range(min(slots - 1, len(pieces))):
        copy(k).start()
    for k, (src, _, rows, cols, didx, drows, dcols) in enumerate(pieces):
        if k + slots - 1 < len(pieces):
            copy(k + slots - 1).start()
        copy(k).wait()
        nr, nc = rows.stop - rows.start, cols.stop - cols.start
        piece = stage_s[k % slots, :nr, :nc].astype(BF)
        if didx is None:
            resident[src][drows, dcols] = piece
        else:
            resident[src][didx, drows, dcols] = piece


def _dense_layer_kernel(att_ref, conv_ref, xa_ref, xb_ref, mod_ref, gf_ref, wo_hbm, wgu_hbm, wd_hbm,
                        o_ref, wo_b, wgu_b, wd_b, stage_s, sem, *, layer):
    i = pl.program_id(0)
    tile = i // (TM // FF_TM)

    @pl.when(i == 0)
    def _():
        _load_weights(_weight_pieces(layer), (wo_hbm, wgu_hbm, wd_hbm), (wo_b, wgu_b, wd_b), stage_s, sem)

    mo = jnp.dot(att_ref[...], wo_b[:ATT_WIDTH, :], preferred_element_type=F32)
    mo = mo + jnp.dot(conv_ref[...], wo_b[ATT_WIDTH:, :], preferred_element_type=F32)
    x = jnp.where(tile >= CTX_TILES, xb_ref[...], xa_ref[...])
    xn = x + _mod(mod_ref, layer, tile, 2) * mo
    h = (_rms(xn) * gf_ref[layer:layer + 1, :]) * (1.0 + _mod(mod_ref, layer, tile, 4)) + _mod(mod_ref, layer, tile, 3)
    h = h.astype(BF)
    y = None
    for half in range(FF_HALVES):
        gu = jnp.dot(h, wgu_b[half], preferred_element_type=F32)
        act = (_silu(gu[:, :FF_HALF]) * gu[:, FF_HALF:]).astype(BF)
        part = jnp.dot(act, wd_b[half * FF_HALF:(half + 1) * FF_HALF, :], preferred_element_type=F32)
        y = part if y is None else y + part
    o_ref[...] = xn + _mod(mod_ref, layer, tile, 5) * y


def _dense_layer(layer, att, conv, xs, mod, g_ffn, w_out, w_gu, w_down):
    xa, xb = xs
    n_ctx = N_CTX // FF_TM
    row = lambda width: pl.BlockSpec((FF_TM, width), lambda i: (i, 0))
    hbm = pl.BlockSpec(memory_space=pl.ANY)
    return pl.pallas_call(
        functools.partial(_dense_layer_kernel, layer=layer),
        grid=(N_TOK // FF_TM,),
        in_specs=[
            row(ATT_WIDTH), row(CONV_WIDTH),
            pl.BlockSpec((FF_TM, D_MODEL), lambda i: (jnp.minimum(i, n_ctx - 1), 0)),
            pl.BlockSpec((FF_TM, D_MODEL), lambda i: (jnp.maximum(i - n_ctx, 0), 0)),
            WHOLE, WHOLE, hbm, hbm, hbm,
        ],
        out_specs=row(D_MODEL),
        out_shape=jax.ShapeDtypeStruct((N_TOK, D_MODEL), F32),
        scratch_shapes=[
            pltpu.VMEM((D_MODEL, D_MODEL), BF),
            pltpu.VMEM((FF_HALVES, D_MODEL, 2 * FF_HALF), BF),
            pltpu.VMEM((D_FF, D_MODEL), BF),
            pltpu.VMEM((FF_STAGE_SLOTS, STAGE_W_ROWS, STAGE_W_COLS), F32),
            pltpu.SemaphoreType.DMA((FF_STAGE_SLOTS,)),
        ],
        compiler_params=_params(("arbitrary",)),
        name="dense_layer",
    )(att, conv, xa, xb, mod, g_ffn, w_out, w_gu, w_down)


def _route(h, wr_ref, rt_ref, n_ref):
    logits = jnp.dot(h, wr_ref[...].astype(BF), preferred_element_type=F32)
    lane = lax.broadcasted_iota(jnp.int32, logits.shape, 1)
    lg = jnp.where(lane < N_EXPERTS, logits, -jnp.inf)
    m1 = jnp.max(lg, axis=-1, keepdims=True)
    i1 = jnp.min(jnp.where(lg == m1, lane, LANES), axis=-1, keepdims=True)
    lg2 = jnp.where(lane == i1, -jnp.inf, lg)
    m2 = jnp.max(lg2, axis=-1, keepdims=True)
    i2 = jnp.min(jnp.where(lg2 == m2, lane, LANES), axis=-1, keepdims=True)
    e2 = jnp.exp(m2 - m1)
    w1 = 1.0 / (1.0 + e2)
    w2 = e2 / (1.0 + e2)

    sel1 = lane == i1
    sel2 = lane == i2
    onehot = jnp.logical_or(sel1, sel2)
    rows = lax.broadcasted_iota(jnp.int32, (ST, ST), 0)
    colsi = lax.broadcasted_iota(jnp.int32, (ST, ST), 1)
    earlier = (colsi < rows).astype(BF)
    onehot_b = onehot.astype(BF)
    before = jnp.concatenate(
        [jnp.dot(earlier, onehot_b[s * ST:(s + 1) * ST], preferred_element_type=F32) for s in range(SUB_PER_TILE)],
        axis=0)
    onehot_f = onehot.astype(F32)
    counts = [jnp.sum(onehot_f[s * ST:(s + 1) * ST], axis=0, keepdims=True) for s in range(SUB_PER_TILE)]
    counts = jnp.concatenate(counts + [jnp.zeros((SUBLANES - SUB_PER_TILE, LANES), F32)], axis=0).astype(jnp.int32)
    seg_len = ((counts + (SEG_ALIGN - 1)) // SEG_ALIGN) * SEG_ALIGN
    n_ref[...] = seg_len
    la = lax.broadcasted_iota(jnp.int32, (LANES, LANES), 0)
    lb = lax.broadcasted_iota(jnp.int32, (LANES, LANES), 1)
    seg_start = jnp.dot(seg_len.astype(F32).astype(BF), (la < lb).astype(BF), preferred_element_type=F32)
    start = jnp.concatenate(
        [jnp.broadcast_to(seg_start[s:s + 1], (ST, LANES)) for s in range(SUB_PER_TILE)], axis=0)
    where = before + start
    lp1 = jnp.sum(jnp.where(sel1, where, 0.0), axis=-1, keepdims=True)
    lp2 = jnp.sum(jnp.where(sel2, where, 0.0), axis=-1, keepdims=True)
    rt = jnp.where(lane == GATE_LANE, w1, jnp.where(lane == GATE_LANE + 1, w2, 0.0))
    rt_ref[...] = jnp.where(lane == ROW_LANE, lp1, jnp.where(lane == ROW_LANE + 1, lp2, rt))


def _chunk_copies(s, cnt_ref, cdst_ref, stage, rows_hbm, sem, *, to_hbm, wait):
    def copy(v, h):
        return pltpu.make_async_copy(v, h, sem) if to_hbm else pltpu.make_async_copy(h, v, sem)

    if wait:
        for z in WAIT_PIECES:
            @pl.when((cnt_ref[s] & z) != 0)
            def _():
                copy(stage.at[pl.ds(0, z * SEG_ALIGN)], rows_hbm.at[pl.ds(0, z * SEG_ALIGN)]).wait()
        return

    def start(c, priority):
        v = stage.at[pl.ds(pl.multiple_of(c * SEG_ALIGN, SEG_ALIGN), SEG_ALIGN)]
        h = rows_hbm.at[pl.ds(pl.multiple_of(cdst_ref[s * STAGE_CHUNKS + c], SEG_ALIGN), SEG_ALIGN)]
        copy(v, h).start(priority=priority)

    def pair(p, carry):
        start(2 * p, 0)
        start(2 * p + 1, 1)
        return carry

    n = cnt_ref[s]
    lax.fori_loop(0, n // 2, pair, 0)

    @pl.when((n & 1) != 0)
    def _():
        start(n - 1, 0)


def _dispatch_kernel(cnt_ref, cdst_ref, h_hbm, rt_hbm, xs_ref, stage_s, sem, step):
    copies = functools.partial(_chunk_copies, cnt_ref=cnt_ref, cdst_ref=cdst_ref, rows_hbm=xs_ref, to_hbm=True)

    step[0] = 0

    def tile(h_ref, rt_ref):
        i = step[0]
        step[0] = i + 1
        for k in range(SUB_PER_TILE):
            s = i * SUB_PER_TILE + k
            slot = k % 2
            rows = slice(k * ST, (k + 1) * ST)

            @pl.when(s >= 2)
            def _():
                copies(s - 2, stage=stage_s.at[slot], sem=sem.at[slot], wait=True)

            at = rt_ref[rows, :].T[ROW_LANE:ROW_LANE + 2, :].astype(jnp.int32)
            r = lax.broadcasted_iota(jnp.int32, (STAGE_ROWS, ST), 0)
            pick = jnp.logical_or(r == at[0:1, :], r == at[1:2, :]).astype(BF)
            stage_s[slot] = jnp.dot(pick, h_ref[rows, :], preferred_element_type=F32).astype(BF)
            copies(s, stage=stage_s.at[slot], sem=sem.at[slot], wait=False)

    pltpu.emit_pipeline(
        tile,
        grid=(N_TILES,),
        in_specs=[
            pl.BlockSpec((TM, D_MODEL), lambda i: (i, 0), pipeline_mode=pl.Buffered(DISPATCH_IN_BUFFERS)),
            pl.BlockSpec((TM, LANES), lambda i: (i, 0), pipeline_mode=pl.Buffered(DISPATCH_IN_BUFFERS)),
        ],
    )(h_hbm, rt_hbm)
    copies(N_SUB - 2, stage=stage_s.at[0], sem=sem.at[0], wait=True)
    copies(N_SUB - 1, stage=stage_s.at[1], sem=sem.at[1], wait=True)


def _dispatch(cnt, cdst, h2, rt):
    assert SUB_PER_TILE % 2 == 0
    hbm = pl.BlockSpec(memory_space=pl.ANY)
    grid_spec = pltpu.PrefetchScalarGridSpec(
        num_scalar_prefetch=2,
        grid=(1,),
        in_specs=[hbm, hbm],
        out_specs=hbm,
        scratch_shapes=[pltpu.VMEM((2, STAGE_ROWS, D_MODEL), BF), pltpu.SemaphoreType.DMA((2,)),
                        pltpu.SMEM((1,), jnp.int32)],
    )
    return pl.pallas_call(
        _dispatch_kernel,
        grid_spec=grid_spec,
        out_shape=jax.ShapeDtypeStruct((R_PAD, D_MODEL), BF),
        compiler_params=_params(("arbitrary",)),
        name="moe_dispatch",
    )(cnt, cdst, h2, rt)


def _expert_weights(r, te_ref, nt_ref, nxt_ref, w_hbm, wf_s, wb_s, sem):
    def fetch(e):
        return pltpu.make_async_copy(w_hbm.at[0, e], wf_s, sem)

    @pl.when(r == 0)
    def _():
        fetch(te_ref[0]).start()

    first = jnp.logical_or(r == 0, te_ref[r] != te_ref[jnp.maximum(r - 1, 0)])

    @pl.when(jnp.logical_and(r < nt_ref[0], first))
    def _():
        fetch(te_ref[r]).wait()
        wb_s[...] = wf_s[...].astype(BF)

        @pl.when(nxt_ref[r] >= 0)
        def _():
            fetch(nxt_ref[r]).start()


def _moe_ffn_kernel(te_ref, nt_ref, nxt_ref, used_ref, x_ref, wgu_hbm, wd_hbm, o_ref,
                    wgu_f, wgu_b, wd_f, wd_b, sem):
    def ffn(rows):
        gu = jnp.dot(x_ref[rows, :], wgu_b[...], preferred_element_type=F32)
        act = (_silu(gu[:, :D_FF_EXPERT]) * gu[:, D_FF_EXPERT:]).astype(BF)
        o_ref[rows, :] = jnp.dot(act, wd_b[...], preferred_element_type=F32).astype(BF)

    for k in range(TILES_PER_STEP):
        r = pl.program_id(0) * TILES_PER_STEP + k
        _expert_weights(r, te_ref, nt_ref, nxt_ref, wgu_hbm, wgu_f, wgu_b, sem.at[0])
        _expert_weights(r, te_ref, nt_ref, nxt_ref, wd_hbm, wd_f, wd_b, sem.at[1])

        @pl.when(jnp.logical_and(r < nt_ref[0], used_ref[r] > TG // 2))
        def _():
            ffn(slice(k * TG, (k + 1) * TG))

        @pl.when(jnp.logical_and(r < nt_ref[0], used_ref[r] <= TG // 2))
        def _():
            ffn(slice(k * TG, k * TG + TG // 2))


def _moe_ffn(te, nt, nxt, used, rows, w_gu, w_down):
    tile_map = lambda i, te, nt, nxt, used: (jnp.minimum(i, (nt[0] - 1) // TILES_PER_STEP), 0)
    grid_spec = pltpu.PrefetchScalarGridSpec(
        num_scalar_prefetch=4,
        grid=(NT_G // TILES_PER_STEP,),
        in_specs=[pl.BlockSpec((TILES_PER_STEP * TG, D_MODEL), tile_map),
                  pl.BlockSpec(memory_space=pl.ANY), pl.BlockSpec(memory_space=pl.ANY)],
        out_specs=pl.BlockSpec((TILES_PER_STEP * TG, D_MODEL), tile_map),
        scratch_shapes=[
            pltpu.VMEM((D_MODEL, 2 * D_FF_EXPERT), F32), pltpu.VMEM((D_MODEL, 2 * D_FF_EXPERT), BF),
            pltpu.VMEM((D_FF_EXPERT, D_MODEL), F32), pltpu.VMEM((D_FF_EXPERT, D_MODEL), BF),
            pltpu.SemaphoreType.DMA((2,)),
        ],
    )
    return pl.pallas_call(
        _moe_ffn_kernel,
        grid_spec=grid_spec,
        out_shape=jax.ShapeDtypeStruct((R_PAD, D_MODEL), BF),
        compiler_params=_params(("arbitrary",)),
        name="moe_ffn",
    )(te, nt, nxt, used, rows, w_gu, w_down)


def _combine_kernel(cnt_ref, cdst_ref, ys_ref, rt_ref, x_ref, mod_ref, fg_ref,
                    oa_ref, ob_ref, stage_s, sem, *, layer):
    i = pl.program_id(0)
    copies = functools.partial(_chunk_copies, cnt_ref=cnt_ref, cdst_ref=cdst_ref, rows_hbm=ys_ref, to_hbm=False)

    @pl.when(i == 0)
    def _():
        stage_s[...] = jnp.zeros_like(stage_s)
        for s in range(COMBINE_AHEAD):
            copies(s, stage=stage_s.at[s], sem=sem.at[s], wait=False)

    for k in range(SUB_PER_TILE):
        s = i * SUB_PER_TILE + k
        slot = k % COMBINE_SLOTS
        ahead = (k + COMBINE_AHEAD) % COMBINE_SLOTS
        rows = slice(k * ST, (k + 1) * ST)

        @pl.when(s + COMBINE_AHEAD < N_SUB)
        def _():
            copies(s + COMBINE_AHEAD, stage=stage_s.at[ahead], sem=sem.at[ahead], wait=False)

        copies(s, stage=stage_s.at[slot], sem=sem.at[slot], wait=True)

        rt = rt_ref[rows, :]
        at = rt[:, ROW_LANE:ROW_LANE + 2].astype(jnp.int32)
        r = lax.broadcasted_iota(jnp.int32, (ST, STAGE_ROWS), 1)
        staged = stage_s[slot]
        a = jnp.dot((r == at[:, 0:1]).astype(BF), staged, preferred_element_type=F32)
        b = jnp.dot((r == at[:, 1:2]).astype(BF), staged, preferred_element_type=F32)
        y = rt[:, GATE_LANE:GATE_LANE + 1] * a + rt[:, GATE_LANE + 1:GATE_LANE + 2] * b
        xn = x_ref[rows, :] + _mod(mod_ref, layer, i, 5) * y
        out = _rms(xn) * fg_ref[...]

        @pl.when(i < CTX_TILES)
        def _():
            oa_ref[rows, :] = out

        @pl.when(i >= CTX_TILES)
        def _():
            ob_ref[rows, :] = out


def _combine(layer, cnt, cdst, ys, rt, x, mod, final_g):
    assert SUB_PER_TILE % COMBINE_SLOTS == 0
    grid_spec = pltpu.PrefetchScalarGridSpec(
        num_scalar_prefetch=2,
        grid=(N_TILES,),
        in_specs=[
            pl.BlockSpec(memory_space=pl.ANY),
            pl.BlockSpec((TM, LANES), lambda i, *_: (i, 0)),
            pl.BlockSpec((TM, D_MODEL), lambda i, *_: (i, 0)),
            WHOLE, WHOLE,
        ],
        out_specs=[
            pl.BlockSpec((TM, D_MODEL), lambda i, *_: (jnp.minimum(i, CTX_TILES - 1), 0)),
            pl.BlockSpec((TM, D_MODEL), lambda i, *_: (jnp.maximum(i - CTX_TILES, 0), 0)),
        ],
        scratch_shapes=[pltpu.VMEM((COMBINE_SLOTS, STAGE_ROWS, D_MODEL), BF),
                        pltpu.SemaphoreType.DMA((COMBINE_SLOTS,))],
    )
    return pl.pallas_call(
        functools.partial(_combine_kernel, layer=layer),
        grid_spec=grid_spec,
        out_shape=[
            jax.ShapeDtypeStruct((N_CTX, D_MODEL), F32),
            jax.ShapeDtypeStruct((N_LAT, D_MODEL), F32),
        ],
        compiler_params=_params(("arbitrary",)),
        name="moe_combine",
    )(cnt, cdst, ys, rt, x, mod, final_g)


def _group_layout(n_tiles):
    n = n_tiles[:, :SUB_PER_TILE, :N_EXPERTS].reshape(N_SUB, N_EXPERTS)
    tiles = (jnp.sum(n, axis=0) + TG - 1) // TG
    tile_end = jnp.cumsum(tiles)
    region = (tile_end - tiles) * TG
    dst = region[None, :] + jnp.cumsum(n, axis=0) - n
    seg_end = jnp.cumsum(n, axis=1)
    seg = seg_end - n
    row = jnp.arange(STAGE_CHUNKS, dtype=jnp.int32) * SEG_ALIGN
    owner = jnp.sum((row[None, :, None] >= seg_end[:, None, :]).astype(jnp.int32), axis=-1)
    own = jnp.minimum(owner, N_EXPERTS - 1)[..., None] == jnp.arange(N_EXPERTS)
    cdst = jnp.sum(jnp.where(own, (dst - seg)[:, None, :], 0), axis=-1) + row[None, :]
    cnt = seg_end[:, -1] // SEG_ALIGN
    nt = tile_end[-1]
    tile_id = jnp.minimum(jnp.arange(NT_G, dtype=jnp.int32), nt - 1)
    te = jnp.sum((tile_id[:, None] >= tile_end[None, :]).astype(jnp.int32), axis=-1)
    after = jnp.sum(jnp.where(te[:, None] == jnp.arange(N_EXPERTS), tile_end[None, :], 0), axis=-1)
    nxt = jnp.where(after < nt, jnp.sum((after[:, None] >= tile_end[None, :]).astype(jnp.int32), axis=-1), -1)
    mine = te[:, None] == jnp.arange(N_EXPERTS)
    region_end = jnp.sum(jnp.where(mine, (region + jnp.sum(n, axis=0))[None, :], 0), axis=-1)
    used = jnp.clip(region_end - tile_id * TG, 0, TG)
    i32 = lambda a: a.astype(jnp.int32)
    return (i32(cnt), i32(cdst.reshape(N_SUB * STAGE_CHUNKS)), i32(te), i32(nt.reshape(1)), i32(nxt), i32(used))


def _rope_tables():
    p = np.arange(DEC_SEQ)
    row = (p // GRID_W).astype(np.float32)
    col = (p % GRID_W).astype(np.float32)
    half = QK_DIM // 4
    freqs = (ROPE_BASE ** (-np.arange(half, dtype=np.float32) / half)).astype(np.float32)
    lane = np.arange(V_DIM)
    f = freqs[lane & (half - 1)]
    use_col = (lane & (2 * half)) != 0
    ang = (np.where(use_col[None, :], col[:, None], row[:, None]) * f[None, :]).astype(np.float32)
    upper = (lane & half) != 0
    sin = np.sin(ang)
    return jnp.asarray(np.cos(ang), F32), jnp.asarray(np.where(upper[None, :], sin, -sin), F32)


def kernel(x_prompt, x_sample, cache_k, cache_v, c, c_ctx, w_ada, b_ada, norm_mix_g, norm_ffn_g,
           w_in, lambda_qk, subln_g, conv_w, w_out, w_gu_dense, w_down_dense, w_router,
           w_gu_moe, w_down_moe, final_g):
    assert DEPTH == 2
    xs = (x_prompt.reshape(N_CTX, D_MODEL), x_sample.reshape(N_LAT, D_MODEL))
    mod = _ada(c_ctx, c, w_ada, b_ada)
    cos_t, sin_t = _rope_tables()
    cache_kt = jnp.transpose(cache_k, (0, 1, 3, 4, 5, 2)).reshape(DEC_BATCH, DEPTH, ATT_WIDTH, PAST_LEN)
    cache_v4 = cache_v.reshape(DEC_BATCH, DEPTH, PAST_LEN * N_HEADS, V_DIM)

    new_kv = None
    for layer in range(DEPTH):
        lam_init = 0.8 - 0.6 * math.exp(-0.3 * layer)
        q, v, kt, conv, nk, nv = _in_proj(layer, xs, mod, norm_mix_g, w_in, cos_t, sin_t, conv_w, new_kv)
        new_kv = (nk, nv)
        att = _attn_ctx(layer, q, kt, v, lambda_qk, subln_g, lam_init)
        att = _attn_lat(layer, q, kt, v, cache_kt, cache_v4, lambda_qk, subln_g, att, lam_init)
        if layer == 0:
            xs = _dense_layer(layer, att, conv, xs, mod, norm_ffn_g, w_out, w_gu_dense, w_down_dense)
        else:
            wr = jnp.pad(w_router[0], ((0, 0), (0, LANES - N_EXPERTS)))
            x1, h2, rt, n_tiles = _out_proj(layer, att, conv, w_out, xs, mod, norm_ffn_g, wr)
            cnt, cdst, te, nt, nxt, used = _group_layout(n_tiles)
            xsort = _dispatch(cnt, cdst, h2, rt)
            ys = _moe_ffn(te, nt, nxt, used, xsort, w_gu_moe, w_down_moe)
            y_ctx, y_lat = _combine(layer, cnt, cdst, ys, rt, x1, mod, final_g.reshape(1, D_MODEL))
    nk, nv = new_kv
    new_k = jnp.transpose(nk.reshape(BATCH, DEPTH, N_HEADS, 2, QK_DIM, SEQ), (0, 1, 5, 2, 3, 4))
    new_v = nv.reshape(BATCH, DEPTH, SEQ, N_HEADS, V_DIM)
    return (y_ctx.reshape(BATCH, SEQ, D_MODEL), y_lat.reshape(DEC_BATCH, DEC_SEQ, D_MODEL), new_k, new_v)
```
